```python
import jax, jax.numpy as jnp
from jax import lax
import numpy as np

D_MODEL = 1024
BATCH = 8
SEQ = 4096
DEPTH = 2

N_MIXERS = 2
N_ATTN_LAYERS = (DEPTH + 1) // 2
N_POOL_LAYERS = DEPTH // 2
GRID_W = 64
EPS = 1e-6

HEAD_DIM = 64
N_HEADS = D_MODEL // HEAD_DIM
N_KV_HEADS = 4
GQA_GROUP = N_HEADS // N_KV_HEADS
Q_BLOCK = 128
ROPE_THETA = 10000.0
ROPE_PAIRS = HEAD_DIM // 4
QKV_DIM = (N_HEADS + 2 * N_KV_HEADS) * HEAD_DIM

POOL_WINDOWS = (2, 4, 8, 16)
N_POOL_GROUPS = len(POOL_WINDOWS)
POOL_GROUP_W = D_MODEL // N_POOL_GROUPS

MEM_LEN = 256
X_HEADS = 4
X_HEAD_DIM = D_MODEL // X_HEADS

D_FF = 2816
CONV_W = 3

kernel_name = "hybrid_axial_gqa_pool_encoder"


def rmsnorm(x, gain):
    x32 = x.astype(jnp.float32)
    y = x32 * lax.rsqrt(jnp.mean(x32 * x32, axis=-1, keepdims=True) + EPS)
    return (y * gain.astype(jnp.float32)).astype(x.dtype)


def axial_rope_tables(seq_len):
    n_rows = seq_len // GRID_W
    row = jnp.repeat(jnp.arange(n_rows, dtype=jnp.float32), GRID_W)
    col = jnp.tile(jnp.arange(GRID_W, dtype=jnp.float32), n_rows)
    inv_freq = ROPE_THETA ** (-jnp.arange(ROPE_PAIRS, dtype=jnp.float32) / ROPE_PAIRS)
    ang = jnp.stack([row[:, None] * inv_freq, col[:, None] * inv_freq], axis=1)
    return jnp.cos(ang), jnp.sin(ang)


def apply_axial_rope(x, cos, sin):
    b, s, h, d = x.shape
    xs = x.astype(jnp.float32).reshape(b, s, h, 2, 2, ROPE_PAIRS)
    x1, x2 = xs[..., 0, :], xs[..., 1, :]
    c = cos[None, :, None]
    sn = sin[None, :, None]
    out = jnp.stack([x1 * c - x2 * sn, x2 * c + x1 * sn], axis=-2)
    return out.reshape(b, s, h, d).astype(x.dtype)


def axial_gqa_attention(h, w_qkv, q_gain, k_gain, w_o):
    b, s, _ = h.shape
    qkv = h @ w_qkv
    q_end = N_HEADS * HEAD_DIM
    k_end = q_end + N_KV_HEADS * HEAD_DIM
    q = qkv[..., :q_end].reshape(b, s, N_HEADS, HEAD_DIM)
    k = qkv[..., q_end:k_end].reshape(b, s, N_KV_HEADS, HEAD_DIM)
    v = qkv[..., k_end:].reshape(b, s, N_KV_HEADS, HEAD_DIM)
    q = rmsnorm(q, q_gain)
    k = rmsnorm(k, k_gain)
    cos, sin = axial_rope_tables(s)
    q = apply_axial_rope(q, cos, sin) * (HEAD_DIM ** -0.5)
    k = apply_axial_rope(k, cos, sin)
    n_blk = s // Q_BLOCK
    qb = q.reshape(b, n_blk, Q_BLOCK, N_KV_HEADS, GQA_GROUP, HEAD_DIM).transpose(1, 0, 3, 4, 2, 5)
    kt = k.transpose(0, 2, 1, 3)
    vt = v.transpose(0, 2, 1, 3)

    def attend_block(q_blk):
        scores = jnp.einsum('bkgqd,bksd->bkgqs', q_blk, kt).astype(jnp.float32)
        p = jax.nn.softmax(scores, axis=-1).astype(vt.dtype)
        return jnp.einsum('bkgqs,bksd->bkgqd', p, vt)

    o = lax.map(attend_block, qb)
    o = o.transpose(1, 0, 4, 2, 3, 5).reshape(b, s, N_HEADS * HEAD_DIM)
    return o @ w_o


def multiscale_pool_mixer(h, w_grp, scale):
    b, s, d = h.shape
    hg = h.astype(jnp.float32).reshape(b, s, N_POOL_GROUPS, POOL_GROUP_W)
    csum = jnp.concatenate(
        [jnp.zeros((b, 1, N_POOL_GROUPS, POOL_GROUP_W), jnp.float32), jnp.cumsum(hg, axis=1)], axis=1)
    t = jnp.arange(s)[:, None]
    win = jnp.array(POOL_WINDOWS, dtype=jnp.int32)[None, :]
    lo = jnp.clip(t - win // 2, 0, s)
    hi = jnp.clip(t + win - win // 2, 0, s)
    s_hi = jnp.take_along_axis(csum, hi[None, :, :, None], axis=1)
    s_lo = jnp.take_along_axis(csum, lo[None, :, :, None], axis=1)
    count = (hi - lo).astype(jnp.float32)[None, :, :, None]
    mixed = (s_hi - s_lo) / count - hg
    y = jnp.einsum('bsgc,gcd->bsgd', mixed, w_grp.astype(jnp.float32)).reshape(b, s, d)
    return (y * scale.astype(jnp.float32)).astype(h.dtype)


def memory_cross_attention(h, mem_n, w_q, w_kv, w_o):
    b, s, _ = h.shape
    m = mem_n.shape[1]
    q = (h @ w_q).reshape(b, s, X_HEADS, X_HEAD_DIM) * (X_HEAD_DIM ** -0.5)
    kv = mem_n @ w_kv
    k = kv[..., :D_MODEL].reshape(b, m, X_HEADS, X_HEAD_DIM)
    v = kv[..., D_MODEL:].reshape(b, m, X_HEADS, X_HEAD_DIM)
    scores = jnp.einsum('bshd,bmhd->bhsm', q, k).astype(jnp.float32)
    p = jax.nn.softmax(scores, axis=-1).astype(v.dtype)
    o = jnp.einsum('bhsm,bmhd->bshd', p, v).reshape(b, s, D_MODEL)
    return o @ w_o


def conv_gated_ffn(h, w_up, conv_w, conv_b, w_down):
    u = h @ w_up
    up = jnp.pad(u, ((0, 0), (1, 1), (0, 0)))
    u = up[:, :-2] * conv_w[0] + up[:, 1:-1] * conv_w[1] + up[:, 2:] * conv_w[2] + conv_b
    gate, val = u[..., :D_FF], u[..., D_FF:]
    return (jax.nn.silu(gate) * val) @ w_down


def _fwd_setup_inputs(seed: int = 0) -> dict:
    key = jax.random.key(seed)
    ks = jax.random.split(key, 24)
    f32 = jnp.float32

    def nrm(k, shape, scale):
        return jax.random.normal(k, shape, f32) * scale

    def gain(k, shape):
        return 1.0 + 0.05 * jax.random.normal(k, shape, f32)

    na, nb = N_ATTN_LAYERS, N_POOL_LAYERS
    return {
        "x": nrm(ks[0], (BATCH, SEQ, D_MODEL), 1.0),
        "mem": nrm(ks[1], (BATCH, MEM_LEN, D_MODEL), 1.0),
        "attn_norm": gain(ks[2], (na, D_MODEL)),
        "attn_w_qkv": nrm(ks[3], (na, D_MODEL, QKV_DIM), D_MODEL ** -0.5),
        "attn_q_gain": gain(ks[4], (na, HEAD_DIM)),
        "attn_k_gain": gain(ks[5], (na, HEAD_DIM)),
        "attn_w_o": nrm(ks[6], (na, N_HEADS * HEAD_DIM, D_MODEL), (N_HEADS * HEAD_DIM) ** -0.5),
        "pool_norm": gain(ks[7], (nb, D_MODEL)),
        "pool_w": nrm(ks[8], (nb, N_POOL_GROUPS, POOL_GROUP_W, POOL_GROUP_W), POOL_GROUP_W ** -0.5),
        "pool_scale": gain(ks[9], (nb, D_MODEL)),
        "xattn_norm": gain(ks[10], (DEPTH, D_MODEL)),
        "mem_norm": gain(ks[11], (DEPTH, D_MODEL)),
        "xattn_w_q": nrm(ks[12], (DEPTH, D_MODEL, D_MODEL), D_MODEL ** -0.5),
        "xattn_w_kv": nrm(ks[13], (DEPTH, D_MODEL, 2 * D_MODEL), D_MODEL ** -0.5),
        "xattn_w_o": nrm(ks[14], (DEPTH, D_MODEL, D_MODEL), D_MODEL ** -0.5),
        "ffn_norm": gain(ks[15], (DEPTH, D_MODEL)),
        "ffn_w_up": nrm(ks[16], (DEPTH, D_MODEL, 2 * D_FF), D_MODEL ** -0.5),
        "ffn_conv_w": nrm(ks[17], (DEPTH, CONV_W, 2 * D_FF), CONV_W ** -0.5),
        "ffn_conv_b": nrm(ks[18], (DEPTH, 2 * D_FF), 0.02),
        "ffn_w_down": nrm(ks[19], (DEPTH, D_FF, D_MODEL), D_FF ** -0.5),
        "final_norm": gain(ks[20], (D_MODEL,)),
    }


def _fwd_reference(x, mem, attn_norm, attn_w_qkv, attn_q_gain, attn_k_gain, attn_w_o,
              pool_norm, pool_w, pool_scale,
              xattn_norm, mem_norm, xattn_w_q, xattn_w_kv, xattn_w_o,
              ffn_norm, ffn_w_up, ffn_conv_w, ffn_conv_b, ffn_w_down,
              final_norm):
    ia = 0
    ib = 0
    for i in range(DEPTH):
        if i % N_MIXERS == 0:
            x = x + axial_gqa_attention(rmsnorm(x, attn_norm[ia]), attn_w_qkv[ia],
                                        attn_q_gain[ia], attn_k_gain[ia], attn_w_o[ia])
            ia += 1
        else:
            x = x + multiscale_pool_mixer(rmsnorm(x, pool_norm[ib]), pool_w[ib], pool_scale[ib])
            ib += 1
        x = x + memory_cross_attention(rmsnorm(x, xattn_norm[i]), rmsnorm(mem, mem_norm[i]),
                                       xattn_w_q[i], xattn_w_kv[i], xattn_w_o[i])
        x = x + conv_gated_ffn(rmsnorm(x, ffn_norm[i]), ffn_w_up[i], ffn_conv_w[i],
                               ffn_conv_b[i], ffn_w_down[i])
    return rmsnorm(x, final_norm)


import jax as _jax
import jax.numpy as _jnp

TWIN_FORMAT = 'train_step'
FWD_PARAMS = ['x', 'mem', 'attn_norm', 'attn_w_qkv', 'attn_q_gain', 'attn_k_gain', 'attn_w_o', 'pool_norm', 'pool_w', 'pool_scale', 'xattn_norm', 'mem_norm', 'xattn_w_q', 'xattn_w_kv', 'xattn_w_o', 'ffn_norm', 'ffn_w_up', 'ffn_conv_w', 'ffn_conv_b', 'ffn_w_down', 'final_norm']
TWIN_WEIGHTS = ['attn_norm', 'attn_w_qkv', 'attn_q_gain', 'attn_k_gain', 'attn_w_o', 'pool_norm', 'pool_w', 'pool_scale', 'xattn_norm', 'mem_norm', 'xattn_w_q', 'xattn_w_kv', 'xattn_w_o', 'ffn_norm', 'ffn_w_up', 'ffn_conv_w', 'ffn_conv_b', 'ffn_w_down', 'final_norm']
TWIN_DIFF_INPUT = 'x'
TWIN_INPUTS = ['x', 'mem', 'attn_norm', 'attn_w_qkv', 'attn_q_gain', 'attn_k_gain', 'attn_w_o', 'pool_norm', 'pool_w', 'pool_scale', 'xattn_norm', 'mem_norm', 'xattn_w_q', 'xattn_w_kv', 'xattn_w_o', 'ffn_norm', 'ffn_w_up', 'ffn_conv_w', 'ffn_conv_b', 'ffn_w_down', 'final_norm', 'loss_target', 'm_attn_norm', 'm_attn_w_qkv', 'm_attn_q_gain', 'm_attn_k_gain', 'm_attn_w_o', 'm_pool_norm', 'm_pool_w', 'm_pool_scale', 'm_xattn_norm', 'm_mem_norm', 'm_xattn_w_q', 'm_xattn_w_kv', 'm_xattn_w_o', 'm_ffn_norm', 'm_ffn_w_up', 'm_ffn_conv_w', 'm_ffn_conv_b', 'm_ffn_w_down', 'm_final_norm', 'v_attn_norm', 'v_attn_w_qkv', 'v_attn_q_gain', 'v_attn_k_gain', 'v_attn_w_o', 'v_pool_norm', 'v_pool_w', 'v_pool_scale', 'v_xattn_norm', 'v_mem_norm', 'v_xattn_w_q', 'v_xattn_w_kv', 'v_xattn_w_o', 'v_ffn_norm', 'v_ffn_w_up', 'v_ffn_conv_w', 'v_ffn_conv_b', 'v_ffn_w_down', 'v_final_norm']
TWIN_OUTPUTS = ['loss', 'grad_x', 'grad_attn_norm', 'grad_attn_w_qkv', 'grad_attn_q_gain', 'grad_attn_k_gain', 'grad_attn_w_o', 'grad_pool_norm', 'grad_pool_w', 'grad_pool_scale', 'grad_xattn_norm', 'grad_mem_norm', 'grad_xattn_w_q', 'grad_xattn_w_kv', 'grad_xattn_w_o', 'grad_ffn_norm', 'grad_ffn_w_up', 'grad_ffn_conv_w', 'grad_ffn_conv_b', 'grad_ffn_w_down', 'grad_final_norm', 'delta_attn_norm', 'delta_attn_w_qkv', 'delta_attn_q_gain', 'delta_attn_k_gain', 'delta_attn_w_o', 'delta_pool_norm', 'delta_pool_w', 'delta_pool_scale', 'delta_xattn_norm', 'delta_mem_norm', 'delta_xattn_w_q', 'delta_xattn_w_kv', 'delta_xattn_w_o', 'delta_ffn_norm', 'delta_ffn_w_up', 'delta_ffn_conv_w', 'delta_ffn_conv_b', 'delta_ffn_w_down', 'delta_final_norm', 'new_m_attn_norm', 'new_m_attn_w_qkv', 'new_m_attn_q_gain', 'new_m_attn_k_gain', 'new_m_attn_w_o', 'new_m_pool_norm', 'new_m_pool_w', 'new_m_pool_scale', 'new_m_xattn_norm', 'new_m_mem_norm', 'new_m_xattn_w_q', 'new_m_xattn_w_kv', 'new_m_xattn_w_o', 'new_m_ffn_norm', 'new_m_ffn_w_up', 'new_m_ffn_conv_w', 'new_m_ffn_conv_b', 'new_m_ffn_w_down', 'new_m_final_norm', 'new_v_attn_norm', 'new_v_attn_w_qkv', 'new_v_attn_q_gain', 'new_v_attn_k_gain', 'new_v_attn_w_o', 'new_v_pool_norm', 'new_v_pool_w', 'new_v_pool_scale', 'new_v_xattn_norm', 'new_v_mem_norm', 'new_v_xattn_w_q', 'new_v_xattn_w_kv', 'new_v_xattn_w_o', 'new_v_ffn_norm', 'new_v_ffn_w_up', 'new_v_ffn_conv_w', 'new_v_ffn_conv_b', 'new_v_ffn_w_down', 'new_v_final_norm']
TWIN_LEAF_KINDS = {'loss': 'loss', 'grad_x': 'grad_x', 'grad_attn_norm': 'grad_w', 'grad_attn_w_qkv': 'grad_w', 'grad_attn_q_gain': 'grad_w', 'grad_attn_k_gain': 'grad_w', 'grad_attn_w_o': 'grad_w', 'grad_pool_norm': 'grad_w', 'grad_pool_w': 'grad_w', 'grad_pool_scale': 'grad_w', 'grad_xattn_norm': 'grad_w', 'grad_mem_norm': 'grad_w', 'grad_xattn_w_q': 'grad_w', 'grad_xattn_w_kv': 'grad_w', 'grad_xattn_w_o': 'grad_w', 'grad_ffn_norm': 'grad_w', 'grad_ffn_w_up': 'grad_w', 'grad_ffn_conv_w': 'grad_w', 'grad_ffn_conv_b': 'grad_w', 'grad_ffn_w_down': 'grad_w', 'grad_final_norm': 'grad_w', 'delta_attn_norm': 'delta_w', 'delta_attn_w_qkv': 'delta_w', 'delta_attn_q_gain': 'delta_w', 'delta_attn_k_gain': 'delta_w', 'delta_attn_w_o': 'delta_w', 'delta_pool_norm': 'delta_w', 'delta_pool_w': 'delta_w', 'delta_pool_scale': 'delta_w', 'delta_xattn_norm': 'delta_w', 'delta_mem_norm': 'delta_w', 'delta_xattn_w_q': 'delta_w', 'delta_xattn_w_kv': 'delta_w', 'delta_xattn_w_o': 'delta_w', 'delta_ffn_norm': 'delta_w', 'delta_ffn_w_up': 'delta_w', 'delta_ffn_conv_w': 'delta_w', 'delta_ffn_conv_b': 'delta_w', 'delta_ffn_w_down': 'delta_w', 'delta_final_norm': 'delta_w', 'new_m_attn_norm': 'new_m', 'new_m_attn_w_qkv': 'new_m', 'new_m_attn_q_gain': 'new_m', 'new_m_attn_k_gain': 'new_m', 'new_m_attn_w_o': 'new_m', 'new_m_pool_norm': 'new_m', 'new_m_pool_w': 'new_m', 'new_m_pool_scale': 'new_m', 'new_m_xattn_norm': 'new_m', 'new_m_mem_norm': 'new_m', 'new_m_xattn_w_q': 'new_m', 'new_m_xattn_w_kv': 'new_m', 'new_m_xattn_w_o': 'new_m', 'new_m_ffn_norm': 'new_m', 'new_m_ffn_w_up': 'new_m', 'new_m_ffn_conv_w': 'new_m', 'new_m_ffn_conv_b': 'new_m', 'new_m_ffn_w_down': 'new_m', 'new_m_final_norm': 'new_m', 'new_v_attn_norm': 'new_v', 'new_v_attn_w_qkv': 'new_v', 'new_v_attn_q_gain': 'new_v', 'new_v_attn_k_gain': 'new_v', 'new_v_attn_w_o': 'new_v', 'new_v_pool_norm': 'new_v', 'new_v_pool_w': 'new_v', 'new_v_pool_scale': 'new_v', 'new_v_xattn_norm': 'new_v', 'new_v_mem_norm': 'new_v', 'new_v_xattn_w_q': 'new_v', 'new_v_xattn_w_kv': 'new_v', 'new_v_xattn_w_o': 'new_v', 'new_v_ffn_norm': 'new_v', 'new_v_ffn_w_up': 'new_v', 'new_v_ffn_conv_w': 'new_v', 'new_v_ffn_conv_b': 'new_v', 'new_v_ffn_w_down': 'new_v', 'new_v_final_norm': 'new_v'}


def _forward(args):
    return _fwd_reference(*[args[k] for k in FWD_PARAMS])


def _output_shape():
    out = _jax.eval_shape(lambda: _forward(_fwd_setup_inputs(0)))
    return out.shape, out.dtype

N_MICROBATCH = 1
ADAM_LR = 0.001
ADAM_B1 = 0.9
ADAM_B2 = 0.999
ADAM_EPS = 1e-08
ADAM_WD = 0.01
ADAM_STEP = 10
PER_EXAMPLE_BATCH_AXIS = {'x': 0, 'mem': 0, 'loss_target': 0}
SHARED_INPUTS = []
_WEIGHT_DTYPES = {'attn_norm': _jnp.float32, 'attn_w_qkv': _jnp.float32, 'attn_q_gain': _jnp.float32, 'attn_k_gain': _jnp.float32, 'attn_w_o': _jnp.float32, 'pool_norm': _jnp.float32, 'pool_w': _jnp.float32, 'pool_scale': _jnp.float32, 'xattn_norm': _jnp.float32, 'mem_norm': _jnp.float32, 'xattn_w_q': _jnp.float32, 'xattn_w_kv': _jnp.float32, 'xattn_w_o': _jnp.float32, 'ffn_norm': _jnp.float32, 'ffn_w_up': _jnp.float32, 'ffn_conv_w': _jnp.float32, 'ffn_conv_b': _jnp.float32, 'ffn_w_down': _jnp.float32, 'final_norm': _jnp.float32}
MOMENT_SCALE = {'attn_norm': 3.785281e-02, 'attn_w_qkv': 3.092940e-02, 'attn_q_gain': 9.353551e-02, 'attn_k_gain': 8.523991e-02, 'attn_w_o': 2.155892e-02, 'pool_norm': 1.160813e-01, 'pool_w': 1.237522e-01, 'pool_scale': 7.905832e-01, 'xattn_norm': 1.912923e-02, 'mem_norm': 2.766156e-02, 'xattn_w_q': 1.879714e-02, 'xattn_w_kv': 1.892282e-02, 'xattn_w_o': 1.862993e-02, 'ffn_norm': 1.231980e-01, 'ffn_w_up': 5.270758e-02, 'ffn_conv_w': 5.243679e-02, 'ffn_conv_b': 5.052559e-02, 'ffn_w_down': 8.634536e-02, 'final_norm': 3.217653e+01}


def _to_microbatches(a, axis):
    t = _jnp.moveaxis(a, axis, 0)
    t = t.reshape((N_MICROBATCH, t.shape[0] // N_MICROBATCH) + t.shape[1:])
    return _jnp.moveaxis(t, 1, axis + 1)


def setup_inputs(seed: int = 0) -> dict:
    inp = _fwd_setup_inputs(seed)
    key = _jax.random.fold_in(_jax.random.key(seed), 7919)
    shape, _ = _output_shape()
    out = dict(inp)
    out["loss_target"] = _jax.random.normal(_jax.random.fold_in(key, 0), shape, _jnp.float32)
    for i, name in enumerate(TWIN_WEIGHTS):
        w = inp[name].astype(_jnp.float32)
        if MOMENT_SCALE is None:
            s = _jnp.sqrt(_jnp.mean(_jnp.square(w)) + 1e-30)
        else:
            s = MOMENT_SCALE[name]
        km, kv = _jax.random.split(_jax.random.fold_in(key, i + 1))
        out[name] = w
        out["m_" + name] = s * _jax.random.normal(km, w.shape, _jnp.float32)
        out["v_" + name] = (s * s) * _jax.random.uniform(kv, w.shape, _jnp.float32, 0.5, 1.5)
    if N_MICROBATCH > 1:
        for name, axis in PER_EXAMPLE_BATCH_AXIS.items():
            out[name] = _to_microbatches(out[name], axis)
    return {'x': out['x'], 'mem': out['mem'], 'attn_norm': out['attn_norm'], 'attn_w_qkv': out['attn_w_qkv'], 'attn_q_gain': out['attn_q_gain'], 'attn_k_gain': out['attn_k_gain'], 'attn_w_o': out['attn_w_o'], 'pool_norm': out['pool_norm'], 'pool_w': out['pool_w'], 'pool_scale': out['pool_scale'], 'xattn_norm': out['xattn_norm'], 'mem_norm': out['mem_norm'], 'xattn_w_q': out['xattn_w_q'], 'xattn_w_kv': out['xattn_w_kv'], 'xattn_w_o': out['xattn_w_o'], 'ffn_norm': out['ffn_norm'], 'ffn_w_up': out['ffn_w_up'], 'ffn_conv_w': out['ffn_conv_w'], 'ffn_conv_b': out['ffn_conv_b'], 'ffn_w_down': out['ffn_w_down'], 'final_norm': out['final_norm'], 'loss_target': out['loss_target'], 'm_attn_norm': out['m_attn_norm'], 'm_attn_w_qkv': out['m_attn_w_qkv'], 'm_attn_q_gain': out['m_attn_q_gain'], 'm_attn_k_gain': out['m_attn_k_gain'], 'm_attn_w_o': out['m_attn_w_o'], 'm_pool_norm': out['m_pool_norm'], 'm_pool_w': out['m_pool_w'], 'm_pool_scale': out['m_pool_scale'], 'm_xattn_norm': out['m_xattn_norm'], 'm_mem_norm': out['m_mem_norm'], 'm_xattn_w_q': out['m_xattn_w_q'], 'm_xattn_w_kv': out['m_xattn_w_kv'], 'm_xattn_w_o': out['m_xattn_w_o'], 'm_ffn_norm': out['m_ffn_norm'], 'm_ffn_w_up': out['m_ffn_w_up'], 'm_ffn_conv_w': out['m_ffn_conv_w'], 'm_ffn_conv_b': out['m_ffn_conv_b'], 'm_ffn_w_down': out['m_ffn_w_down'], 'm_final_norm': out['m_final_norm'], 'v_attn_norm': out['v_attn_norm'], 'v_attn_w_qkv': out['v_attn_w_qkv'], 'v_attn_q_gain': out['v_attn_q_gain'], 'v_attn_k_gain': out['v_attn_k_gain'], 'v_attn_w_o': out['v_attn_w_o'], 'v_pool_norm': out['v_pool_norm'], 'v_pool_w': out['v_pool_w'], 'v_pool_scale': out['v_pool_scale'], 'v_xattn_norm': out['v_xattn_norm'], 'v_mem_norm': out['v_mem_norm'], 'v_xattn_w_q': out['v_xattn_w_q'], 'v_xattn_w_kv': out['v_xattn_w_kv'], 'v_xattn_w_o': out['v_xattn_w_o'], 'v_ffn_norm': out['v_ffn_norm'], 'v_ffn_w_up': out['v_ffn_w_up'], 'v_ffn_conv_w': out['v_ffn_conv_w'], 'v_ffn_conv_b': out['v_ffn_conv_b'], 'v_ffn_w_down': out['v_ffn_w_down'], 'v_final_norm': out['v_final_norm']}


def _loss(weights, diff, rest, loss_target):
    with _jax.named_scope("forward"):
        args = {**rest, TWIN_DIFF_INPUT: diff, **{k: w.astype(_WEIGHT_DTYPES[k]) for k, w in weights.items()}}
        y = _forward(args)
    with _jax.named_scope("loss_head"):
        err = _jnp.square(y.astype(_jnp.float32) - loss_target)
        return 0.5 * _jnp.sum(_jnp.mean(err, axis=-1)) if err.ndim else 0.5 * err


def _adamw(w, g, m, v):
    m = ADAM_B1 * m + (1.0 - ADAM_B1) * g
    v = ADAM_B2 * v + (1.0 - ADAM_B2) * _jnp.square(g)
    m_hat = m / (1.0 - ADAM_B1 ** ADAM_STEP)
    v_hat = v / (1.0 - ADAM_B2 ** ADAM_STEP)
    delta = -ADAM_LR * (m_hat / (_jnp.sqrt(v_hat) + ADAM_EPS) + ADAM_WD * w)
    return delta, m, v


def reference(x, mem, attn_norm, attn_w_qkv, attn_q_gain, attn_k_gain, attn_w_o, pool_norm, pool_w, pool_scale, xattn_norm, mem_norm, xattn_w_q, xattn_w_kv, xattn_w_o, ffn_norm, ffn_w_up, ffn_conv_w, ffn_conv_b, ffn_w_down, final_norm, loss_target, m_attn_norm, m_attn_w_qkv, m_attn_q_gain, m_attn_k_gain, m_attn_w_o, m_pool_norm, m_pool_w, m_pool_scale, m_xattn_norm, m_mem_norm, m_xattn_w_q, m_xattn_w_kv, m_xattn_w_o, m_ffn_norm, m_ffn_w_up, m_ffn_conv_w, m_ffn_conv_b, m_ffn_w_down, m_final_norm, v_attn_norm, v_attn_w_qkv, v_attn_q_gain, v_attn_k_gain, v_attn_w_o, v_pool_norm, v_pool_w, v_pool_scale, v_xattn_norm, v_mem_norm, v_xattn_w_q, v_xattn_w_kv, v_xattn_w_o, v_ffn_norm, v_ffn_w_up, v_ffn_conv_w, v_ffn_conv_b, v_ffn_w_down, v_final_norm):
    given = dict(x=x, mem=mem, attn_norm=attn_norm, attn_w_qkv=attn_w_qkv, attn_q_gain=attn_q_gain, attn_k_gain=attn_k_gain, attn_w_o=attn_w_o, pool_norm=pool_norm, pool_w=pool_w, pool_scale=pool_scale, xattn_norm=xattn_norm, mem_norm=mem_norm, xattn_w_q=xattn_w_q, xattn_w_kv=xattn_w_kv, xattn_w_o=xattn_w_o, ffn_norm=ffn_norm, ffn_w_up=ffn_w_up, ffn_conv_w=ffn_conv_w, ffn_conv_b=ffn_conv_b, ffn_w_down=ffn_w_down, final_norm=final_norm, loss_target=loss_target, m_attn_norm=m_attn_norm, m_attn_w_qkv=m_attn_w_qkv, m_attn_q_gain=m_attn_q_gain, m_attn_k_gain=m_attn_k_gain, m_attn_w_o=m_attn_w_o, m_pool_norm=m_pool_norm, m_pool_w=m_pool_w, m_pool_scale=m_pool_scale, m_xattn_norm=m_xattn_norm, m_mem_norm=m_mem_norm, m_xattn_w_q=m_xattn_w_q, m_xattn_w_kv=m_xattn_w_kv, m_xattn_w_o=m_xattn_w_o, m_ffn_norm=m_ffn_norm, m_ffn_w_up=m_ffn_w_up, m_ffn_conv_w=m_ffn_conv_w, m_ffn_conv_b=m_ffn_conv_b, m_ffn_w_down=m_ffn_w_down, m_final_norm=m_final_norm, v_attn_norm=v_attn_norm, v_attn_w_qkv=v_attn_w_qkv, v_attn_q_gain=v_attn_q_gain, v_attn_k_gain=v_attn_k_gain, v_attn_w_o=v_attn_w_o, v_pool_norm=v_pool_norm, v_pool_w=v_pool_w, v_pool_scale=v_pool_scale, v_xattn_norm=v_xattn_norm, v_mem_norm=v_mem_norm, v_xattn_w_q=v_xattn_w_q, v_xattn_w_kv=v_xattn_w_kv, v_xattn_w_o=v_xattn_w_o, v_ffn_norm=v_ffn_norm, v_ffn_w_up=v_ffn_w_up, v_ffn_conv_w=v_ffn_conv_w, v_ffn_conv_b=v_ffn_conv_b, v_ffn_w_down=v_ffn_w_down, v_final_norm=v_final_norm)
    weights = {n: given[n] for n in TWIN_WEIGHTS}
    shared = {n: given[n] for n in SHARED_INPUTS}
    per_example = {n: given[n] for n in ['x', 'mem']}
    grad_fn = _jax.value_and_grad(_loss, argnums=(0, 1))

    def one_microbatch(ex, loss_target):
        ex = dict(ex)
        diff = ex.pop(TWIN_DIFF_INPUT)
        return grad_fn(weights, diff, {**shared, **ex}, loss_target)

    if N_MICROBATCH == 1:
        loss, (grad_w, grad_x) = one_microbatch(per_example, given["loss_target"])
    else:
        def body(carry, xs):
            loss_sum, grad_sum = carry
            l_k, (gw_k, gx_k) = one_microbatch(xs[0], xs[1])
            with _jax.named_scope("update"):
                return (loss_sum + l_k, _jax.tree.map(_jnp.add, grad_sum, gw_k)), gx_k

        init = (_jnp.zeros((), _jnp.float32), _jax.tree.map(_jnp.zeros_like, weights))
        (loss, grad_w), grad_x = _jax.lax.scan(body, init, (per_example, given["loss_target"]))
    with _jax.named_scope("update"):
        delta_w, new_m, new_v = {}, {}, {}
        for n in TWIN_WEIGHTS:
            delta_w[n], new_m[n], new_v[n] = _adamw(weights[n], grad_w[n], given["m_" + n], given["v_" + n])
    return (loss, grad_x, *[grad_w[n] for n in TWIN_WEIGHTS], *[delta_w[n] for n in TWIN_WEIGHTS],
            *[new_m[n] for n in TWIN_WEIGHTS], *[new_v[n] for n in TWIN_WEIGHTS])
```

```python
import jax
import jax.numpy as jnp
from jax import lax
from jax.experimental import pallas as pl
from jax.experimental.pallas import tpu as pltpu

F32 = jnp.float32
BF16 = jnp.bfloat16
MESH = pl.DeviceIdType.MESH

N_DEV = 8
EPS = 1e-6
HEAD_DIM = 64
N_KV_HEADS = 4
X_HEADS = 4
GRID_W = 64
ROPE_THETA = 10000.0
ROPE_PAIRS = HEAD_DIM // 4
POOL_WINDOWS = (2, 4, 8, 16)
POOL_PAD = 16
LANES = 128
PACK_W = 1024
ADAM_LR, ADAM_B1, ADAM_B2, ADAM_EPS, ADAM_WD, ADAM_STEP = 0.001, 0.9, 0.999, 1e-08, 0.01, 10

_NN = (((1,), (0,)), ((), ()))
_NT = (((1,), (1,)), ((), ()))
_TN = (((0,), (0,)), ((), ()))


def _pc(body, *, name, **kw):
    return pl.pallas_call(body, name=name, **kw)


def _sem(*kinds):
    return pltpu.CompilerParams(dimension_semantics=kinds)


def _tile(n, pref, mult):
    best = None
    for t in range(mult, min(n, pref) + 1, mult):
        if n % t == 0:
            best = t
    return n if best is None else best


def _round_up(n, m):
    return (n + m - 1) // m * m


def _mm_call(name, a, b, dims, grid, a_spec, b_spec, o_spec, out_shape, kaxis, res=None, res_spec=None):
    nk = grid[kaxis]
    acc_shape = tuple(d for d in o_spec.block_shape if d is not None)

    def body(*refs):
        if res is None:
            a_ref, b_ref, o_ref, acc = refs
            r_ref = None
        else:
            a_ref, b_ref, r_ref, o_ref, acc = refs
        k = pl.program_id(kaxis)

        @pl.when(k == 0)
        def _():
            acc[...] = jnp.zeros_like(acc)

        acc[...] += lax.dot_general(a_ref[...].astype(BF16), b_ref[...].astype(BF16), dims,
                                    preferred_element_type=F32)

        @pl.when(k == nk - 1)
        def _():
            r = acc[...]
            if r_ref is not None:
                r = r + r_ref[...]
            o_ref[...] = r.astype(o_ref.dtype)

    sem = tuple("arbitrary" if ax == kaxis else "parallel" for ax in range(len(grid)))
    ins = [a, b] if res is None else [a, b, res]
    specs = [a_spec, b_spec] if res is None else [a_spec, b_spec, res_spec]
    return _pc(body, name=name, grid=grid, in_specs=specs, out_specs=o_spec, out_shape=out_shape,
               scratch_shapes=[pltpu.VMEM(acc_shape, F32)], compiler_params=_sem(*sem))(*ins)


def _mm_nn(name, a, b, out_dtype, res=None):
    M, K = a.shape
    N = b.shape[1]
    tm, tn, tk = _tile(M, 512, 16), _tile(N, 1024, LANES), _tile(K, 1024, LANES)
    return _mm_call(name, a, b, _NN, (M // tm, N // tn, K // tk),
                    pl.BlockSpec((tm, tk), lambda i, j, k: (i, k)),
                    pl.BlockSpec((tk, tn), lambda i, j, k: (k, j)),
                    pl.BlockSpec((tm, tn), lambda i, j, k: (i, j)),
                    jax.ShapeDtypeStruct((M, N), out_dtype), 2, res,
                    pl.BlockSpec((tm, tn), lambda i, j, k: (i, j)))


def _mm_nt(name, a, b, out_dtype):
    M, K = a.shape
    N = b.shape[0]
    tm, tn, tk = _tile(M, 512, 16), _tile(N, 1024, LANES), _tile(K, 1024, LANES)
    return _mm_call(name, a, b, _NT, (M // tm, N // tn, K // tk),
                    pl.BlockSpec((tm, tk), lambda i, j, k: (i, k)),
                    pl.BlockSpec((tn, tk), lambda i, j, k: (j, k)),
                    pl.BlockSpec((tm, tn), lambda i, j, k: (i, j)),
                    jax.ShapeDtypeStruct((M, N), out_dtype), 2)


def _mm_tn(name, a, b, out_dtype):
    R, M = a.shape
    N = b.shape[1]
    tm, tn, tr = _tile(M, 1024, LANES), _tile(N, 1024, LANES), _tile(R, 512, 16)
    return _mm_call(name, a, b, _TN, (M // tm, N // tn, R // tr),
                    pl.BlockSpec((tr, tm), lambda i, j, k: (k, i)),
                    pl.BlockSpec((tr, tn), lambda i, j, k: (k, j)),
                    pl.BlockSpec((tm, tn), lambda i, j, k: (i, j)),
                    jax.ShapeDtypeStruct((M, N), out_dtype), 2)


def _mm_nn_bs(name, a, b, out_dtype):
    M, K = a.shape
    J, _, n = b.shape
    tm, tk = _tile(M, 512, 16), _tile(K, 1024, LANES)
    return _mm_call(name, a, b, _NN, (J, M // tm, K // tk),
                    pl.BlockSpec((tm, tk), lambda j, i, k: (i, k)),
                    pl.BlockSpec((None, tk, n), lambda j, i, k: (j, k, 0)),
                    pl.BlockSpec((None, tm, n), lambda j, i, k: (j, i, 0)),
                    jax.ShapeDtypeStruct((J, M, n), out_dtype), 2)


def _mm_nt_bs(name, a, b, J, out_dtype):
    M, K = a.shape
    n = b.shape[0] // J
    tm, tk = _tile(M, 512, 16), _tile(K, 1024, LANES)
    return _mm_call(name, a, b, _NT, (J, M // tm, K // tk),
                    pl.BlockSpec((tm, tk), lambda j, i, k: (i, k)),
                    pl.BlockSpec((n, tk), lambda j, i, k: (j, k)),
                    pl.BlockSpec((None, tm, n), lambda j, i, k: (j, i, 0)),
                    jax.ShapeDtypeStruct((J, M, n), out_dtype), 2)


def _mm_nn_as(name, a, b, out_dtype, res):
    J, M, n = a.shape
    N = b.shape[1]
    tm, tn = _tile(M, 512, 16), _tile(N, 1024, LANES)
    return _mm_call(name, a, b, _NN, (M // tm, N // tn, J),
                    pl.BlockSpec((None, tm, n), lambda i, j, k: (k, i, 0)),
                    pl.BlockSpec((n, tn), lambda i, j, k: (k, j)),
                    pl.BlockSpec((tm, tn), lambda i, j, k: (i, j)),
                    jax.ShapeDtypeStruct((M, N), out_dtype), 2, res,
                    pl.BlockSpec((tm, tn), lambda i, j, k: (i, j)))


def _mm_nt_abs(name, a, b, out_dtype):
    J, M, n = a.shape
    N = b.shape[1]
    tm, tn = _tile(M, 512, 16), _tile(N, 1024, LANES)
    return _mm_call(name, a, b, _NT, (M // tm, N // tn, J),
                    pl.BlockSpec((None, tm, n), lambda i, j, k: (k, i, 0)),
                    pl.BlockSpec((None, tn, n), lambda i, j, k: (k, j, 0)),
                    pl.BlockSpec((tm, tn), lambda i, j, k: (i, j)),
                    jax.ShapeDtypeStruct((M, N), out_dtype), 2)


def _mm_tn_bs(name, a, b, out_dtype):
    R, M = a.shape
    J, _, n = b.shape
    tm, tr = _tile(M, 1024, LANES), _tile(R, 512, 16)
    return _mm_call(name, a, b, _TN, (J, M // tm, R // tr),
                    pl.BlockSpec((tr, tm), lambda j, i, k: (k, i)),
                    pl.BlockSpec((None, tr, n), lambda j, i, k: (j, k, 0)),
                    pl.BlockSpec((None, tm, n), lambda j, i, k: (j, i, 0)),
                    jax.ShapeDtypeStruct((J, M, n), out_dtype), 2)


def _mm_tn_as(name, a, b, out_dtype):
    J, R, n = a.shape
    N = b.shape[1]
    tn, tr = _tile(N, 1024, LANES), _tile(R, 512, 16)
    return _mm_call(name, a, b, _TN, (J, N // tn, R // tr),
                    pl.BlockSpec((None, tr, n), lambda j, jn, k: (j, k, 0)),
                    pl.BlockSpec((tr, tn), lambda j, jn, k: (k, jn)),
                    pl.BlockSpec((n, tn), lambda j, jn, k: (j, jn)),
                    jax.ShapeDtypeStruct((J * n, N), out_dtype), 2)


def _rmsnorm(name, x, g, out_dtype):
    R, D = x.shape
    tm = _tile(R, 512, 16)

    def body(x_ref, g_ref, o_ref):
        xv = x_ref[...]
        r = lax.rsqrt(jnp.mean(xv * xv, axis=-1, keepdims=True) + EPS)
        o_ref[...] = (xv * r * g_ref[...]).astype(o_ref.dtype)

    return _pc(body, name=name, grid=(R // tm,),
               in_specs=[pl.BlockSpec((tm, D), lambda i: (i, 0)), pl.BlockSpec((1, D), lambda i: (0, 0))],
               out_specs=pl.BlockSpec((tm, D), lambda i: (i, 0)),
               out_shape=jax.ShapeDtypeStruct((R, D), out_dtype), compiler_params=_sem("parallel"))(x, g)


def _rmsnorm_bwd(name, x, g, dh, dres=None):
    R, D = x.shape
    tm = _tile(R, 512, 16)

    def body(*refs):
        if dres is None:
            x_ref, g_ref, dh_ref, dx_ref, dg_ref = refs
            dres_ref = None
        else:
            x_ref, g_ref, dh_ref, dres_ref, dx_ref, dg_ref = refs
        xv = x_ref[...]
        r = lax.rsqrt(jnp.mean(xv * xv, axis=-1, keepdims=True) + EPS)
        xh = xv * r
        dhv = dh_ref[...].astype(F32)

        @pl.when(pl.program_id(0) == 0)
        def _():
            dg_ref[...] = jnp.zeros_like(dg_ref)

        dg_ref[...] += jnp.sum(dhv * xh, axis=0, keepdims=True)
        dxh = dhv * g_ref[...]
        dx = r * (dxh - xh * jnp.mean(dxh * xh, axis=-1, keepdims=True))
        if dres_ref is not None:
            dx = dx + dres_ref[...]
        dx_ref[...] = dx

    row = pl.BlockSpec((tm, D), lambda i: (i, 0))
    vec = pl.BlockSpec((1, D), lambda i: (0, 0))
    ins = [x, g, dh] + ([] if dres is None else [dres])
    specs = [row, vec, row] + ([] if dres is None else [row])
    return _pc(body, name=name, grid=(R // tm,), in_specs=specs, out_specs=(row, vec),
               out_shape=(jax.ShapeDtypeStruct((R, D), F32), jax.ShapeDtypeStruct((1, D), F32)),
               compiler_params=_sem("arbitrary"))(*ins)


def _loss_head(name, x, g, tgt):
    R, D = x.shape
    tm = _tile(R, 512, 16)

    def body(x_ref, g_ref, t_ref, dx_ref, dg_ref, l_ref):
        xv = x_ref[...]
        r = lax.rsqrt(jnp.mean(xv * xv, axis=-1, keepdims=True) + EPS)
        xh = xv * r
        err = xh * g_ref[...] - t_ref[...]

        @pl.when(pl.program_id(0) == 0)
        def _():
            dg_ref[...] = jnp.zeros_like(dg_ref)
            l_ref[...] = jnp.zeros_like(l_ref)

        l_ref[...] += jnp.sum(err * err, axis=0, keepdims=True)
        dy = err * (1.0 / D)
        dg_ref[...] += jnp.sum(dy * xh, axis=0, keepdims=True)
        dxh = dy * g_ref[...]
        dx_ref[...] = r * (dxh - xh * jnp.mean(dxh * xh, axis=-1, keepdims=True))

    row = pl.BlockSpec((tm, D), lambda i: (i, 0))
    vec = pl.BlockSpec((1, D), lambda i: (0, 0))
    return _pc(body, name=name, grid=(R // tm,), in_specs=[row, vec, row], out_specs=(row, vec, vec),
               out_shape=(jax.ShapeDtypeStruct((R, D), F32), jax.ShapeDtypeStruct((1, D), F32),
                          jax.ShapeDtypeStruct((1, D), F32)),
               compiler_params=_sem("arbitrary"))(x, g, tgt)


def _rope_tables(S):
    n_rows = S // GRID_W
    row = jnp.repeat(jnp.arange(n_rows, dtype=F32), GRID_W)
    col = jnp.tile(jnp.arange(GRID_W, dtype=F32), n_rows)
    inv_freq = ROPE_THETA ** (-jnp.arange(ROPE_PAIRS, dtype=F32) / ROPE_PAIRS)
    ang = jnp.stack([row[:, None] * inv_freq, col[:, None] * inv_freq], axis=1)
    cos, sin = jnp.cos(ang), jnp.sin(ang)
    c = jnp.broadcast_to(cos[:, :, None, :], (S, 2, 2, ROPE_PAIRS)).reshape(S, HEAD_DIM)
    s = jnp.stack([-sin, sin], axis=2).reshape(S, HEAD_DIM)
    reps = LANES // HEAD_DIM
    return jnp.tile(c, (1, reps)), jnp.tile(s, (1, reps))


def _head_mean_matrix():
    h = jnp.arange(LANES) // HEAD_DIM
    return jnp.where(h[:, None] == h[None, :], 1.0 / HEAD_DIM, 0.0).astype(F32)


def _swap_halves(y):
    lane = lax.broadcasted_iota(jnp.int32, y.shape, 1)
    return jnp.where(lane % 32 < 16, pltpu.roll(y, LANES - 16, 1), pltpu.roll(y, 16, 1))


def _qk_rope(name, qkv, gain, scale, cos, sin, bd, n_rot):
    S, W = qkv.shape
    tm = _tile(S, 512, 16)

    def body(x_ref, g_ref, s_ref, c_ref, sn_ref, bd_ref, o_ref):
        j = pl.program_id(1)
        xv = x_ref[...]

        @pl.when(j < n_rot)
        def _():
            ms = jnp.dot(xv * xv, bd_ref[...], precision=lax.Precision.HIGHEST, preferred_element_type=F32)
            y = xv * lax.rsqrt(ms + EPS) * g_ref[...] * s_ref[...]
            o_ref[...] = (y * c_ref[...] + _swap_halves(y) * sn_ref[...]).astype(BF16)

        @pl.when(j >= n_rot)
        def _():
            o_ref[...] = xv.astype(BF16)

    blk = pl.BlockSpec((tm, LANES), lambda i, j: (i, j))
    vec = pl.BlockSpec((1, LANES), lambda i, j: (0, j))
    tab = pl.BlockSpec((tm, LANES), lambda i, j: (i, 0))
    return _pc(body, name=name, grid=(S // tm, W // LANES),
               in_specs=[blk, vec, vec, tab, tab, pl.BlockSpec((LANES, LANES), lambda i, j: (0, 0))],
               out_specs=blk, out_shape=jax.ShapeDtypeStruct((S, W), BF16),
               compiler_params=_sem("parallel", "parallel"))(qkv, gain, scale, cos, sin, bd)


def _qk_rope_bwd(name, d_out, qkv, gain, scale, cos, sin, bd, n_rot):
    S, W = qkv.shape
    tm = _tile(S, 512, 16)

    def body(d_ref, x_ref, g_ref, s_ref, c_ref, sn_ref, bd_ref, dx_ref, dg_ref):
        j, i = pl.program_id(0), pl.program_id(1)
        dv = d_ref[...]

        @pl.when(i == 0)
        def _():
            dg_ref[...] = jnp.zeros_like(dg_ref)

        @pl.when(j < n_rot)
        def _():
            xv = x_ref[...]
            ms = jnp.dot(xv * xv, bd_ref[...], precision=lax.Precision.HIGHEST, preferred_element_type=F32)
            r = lax.rsqrt(ms + EPS)
            z = xv * r
            dy = (dv * c_ref[...] - _swap_halves(dv) * sn_ref[...]) * s_ref[...]
            dg_ref[...] += jnp.sum(dy * z, axis=0, keepdims=True)
            dz = dy * g_ref[...]
            mz = jnp.dot(dz * z, bd_ref[...], precision=lax.Precision.HIGHEST, preferred_element_type=F32)
            dx_ref[...] = (r * (dz - z * mz)).astype(BF16)

        @pl.when(j >= n_rot)
        def _():
            dx_ref[...] = dv.astype(BF16)

    blk = pl.BlockSpec((tm, LANES), lambda j, i: (i, j))
    vec = pl.BlockSpec((1, LANES), lambda j, i: (0, j))
    tab = pl.BlockSpec((tm, LANES), lambda j, i: (i, 0))
    return _pc(body, name=name, grid=(W // LANES, S // tm),
               in_specs=[blk, blk, vec, vec, tab, tab, pl.BlockSpec((LANES, LANES), lambda j, i: (0, 0))],
               out_specs=(blk, vec),
               out_shape=(jax.ShapeDtypeStruct((S, W), BF16), jax.ShapeDtypeStruct((1, W), F32)),
               compiler_params=_sem("parallel", "arbitrary"))(d_out, qkv, gain, scale, cos, sin, bd)


def _softmax_rows(s):
    m = jnp.max(s, axis=-1, keepdims=True)
    p = jnp.exp(s - m)
    return p, jnp.sum(p, axis=-1, keepdims=True)


def _attn_fwd(name, q, k, v):
    H, S, dh = q.shape
    G = H // N_KV_HEADS
    tq = _tile(S, 128, 16)

    def body(q_ref, k_ref, v_ref, o_ref):
        kk, vv = k_ref[...], v_ref[...]
        for g in range(G):
            s = lax.dot_general(q_ref[g], kk, _NT, preferred_element_type=F32)
            p, l = _softmax_rows(s)
            o = jnp.dot(p.astype(BF16), vv, preferred_element_type=F32)
            o_ref[g] = (o / l).astype(BF16)

    qs = pl.BlockSpec((G, tq, dh), lambda kv, i: (kv, i, 0))
    ks = pl.BlockSpec((None, S, dh), lambda kv, i: (kv, 0, 0))
    return _pc(body, name=name, grid=(N_KV_HEADS, S // tq), in_specs=[qs, ks, ks], out_specs=qs,
               out_shape=jax.ShapeDtypeStruct((H, S, dh), BF16),
               compiler_params=_sem("parallel", "parallel"))(q, k, v)


def _attn_bwd(name, q, k, v, do):
    H, S, dh = q.shape
    G = H // N_KV_HEADS
    tq = _tile(S, 64, 16)

    def body(q_ref, k_ref, v_ref, do_ref, dq_ref, dk_ref, dv_ref):
        @pl.when(pl.program_id(1) == 0)
        def _():
            dk_ref[...] = jnp.zeros_like(dk_ref)
            dv_ref[...] = jnp.zeros_like(dv_ref)

        kk, vv = k_ref[...], v_ref[...]
        for g in range(G):
            qg, dog = q_ref[g], do_ref[g]
            s = lax.dot_general(qg, kk, _NT, preferred_element_type=F32)
            p, l = _softmax_rows(s)
            pn = p / l
            dv_ref[...] += lax.dot_general(pn.astype(BF16), dog, _TN, preferred_element_type=F32)
            dp = lax.dot_general(dog, vv, _NT, preferred_element_type=F32)
            ds = (pn * (dp - jnp.sum(pn * dp, axis=-1, keepdims=True))).astype(BF16)
            dq_ref[g] = jnp.dot(ds, kk, preferred_element_type=F32)
            dk_ref[...] += lax.dot_general(ds, qg, _TN, preferred_element_type=F32)

    qs = pl.BlockSpec((G, tq, dh), lambda kv, i: (kv, i, 0))
    ks = pl.BlockSpec((None, S, dh), lambda kv, i: (kv, 0, 0))
    return _pc(body, name=name, grid=(N_KV_HEADS, S // tq), in_specs=[qs, ks, ks, qs], out_specs=(qs, ks, ks),
               out_shape=(jax.ShapeDtypeStruct((H, S, dh), F32), jax.ShapeDtypeStruct((N_KV_HEADS, S, dh), F32),
                          jax.ShapeDtypeStruct((N_KV_HEADS, S, dh), F32)),
               compiler_params=_sem("parallel", "arbitrary"))(q, k, v, do)


def _xattn_fwd(name, q, kv):
    S, D = q.shape
    M = kv.shape[0]
    dh = D // X_HEADS
    scale = dh ** -0.5
    tq = _tile(S, 256, 16)

    def body(q_ref, kv_ref, o_ref):
        for h in range(X_HEADS):
            lo, hi = h * dh, (h + 1) * dh
            s = lax.dot_general(q_ref[:, lo:hi], kv_ref[:, lo:hi], _NT, preferred_element_type=F32) * scale
            p, l = _softmax_rows(s)
            o = jnp.dot(p.astype(BF16), kv_ref[:, D + lo:D + hi], preferred_element_type=F32)
            o_ref[:, lo:hi] = (o / l).astype(BF16)

    row = pl.BlockSpec((tq, D), lambda i: (i, 0))
    return _pc(body, name=name, grid=(S // tq,), in_specs=[row, pl.BlockSpec((M, 2 * D), lambda i: (0, 0))],
               out_specs=row, out_shape=jax.ShapeDtypeStruct((S, D), BF16),
               compiler_params=_sem("parallel"))(q, kv)


def _xattn_bwd(name, q, kv, do):
    S, D = q.shape
    M = kv.shape[0]
    dh = D // X_HEADS
    scale = dh ** -0.5
    tq = _tile(S, 256, 16)

    def body(q_ref, kv_ref, do_ref, dq_ref, dkv_ref):
        @pl.when(pl.program_id(0) == 0)
        def _():
            dkv_ref[...] = jnp.zeros_like(dkv_ref)

        for h in range(X_HEADS):
            lo, hi = h * dh, (h + 1) * dh
            qh, kh, vh, doh = q_ref[:, lo:hi], kv_ref[:, lo:hi], kv_ref[:, D + lo:D + hi], do_ref[:, lo:hi]
            s = lax.dot_general(qh, kh, _NT, preferred_element_type=F32) * scale
            p, l = _softmax_rows(s)
            pn = p / l
            dkv_ref[:, D + lo:D + hi] += lax.dot_general(pn.astype(BF16), doh, _TN, preferred_element_type=F32)
            dp = lax.dot_general(doh, vh, _NT, preferred_element_type=F32)
            ds = (pn * (dp - jnp.sum(pn * dp, axis=-1, keepdims=True)) * scale).astype(BF16)
            dq_ref[:, lo:hi] = jnp.dot(ds, kh, preferred_element_type=F32).astype(BF16)
            dkv_ref[:, lo:hi] += lax.dot_general(ds, qh, _TN, preferred_element_type=F32)

    row = pl.BlockSpec((tq, D), lambda i: (i, 0))
    full = pl.BlockSpec((M, 2 * D), lambda i: (0, 0))
    return _pc(body, name=name, grid=(S // tq,), in_specs=[row, full, row], out_specs=(row, full),
               out_shape=(jax.ShapeDtypeStruct((S, D), BF16), jax.ShapeDtypeStruct((M, 2 * D), F32)),
               compiler_params=_sem("arbitrary"))(q, kv, do)


def _sigmoid(x):
    return 1.0 / (1.0 + jnp.exp(-x))


def _halo_specs(tm, n, S):
    nb = tm // 8
    last8 = S // 8 - 1
    main = pl.BlockSpec((2, None, tm, n), lambda j, i: (0, j, i, 0))
    prev = pl.BlockSpec((2, None, 8, n), lambda j, i: (0, j, jnp.maximum(i * nb - 1, 0), 0))
    nxt = pl.BlockSpec((2, None, 8, n), lambda j, i: (0, j, jnp.minimum((i + 1) * nb, last8), 0))
    return main, prev, nxt


def _ffn_act(name, u, cw, cb):
    _, J, S, n = u.shape
    tm = _tile(S, 256, 16)
    nblk = S // tm

    def body(u_ref, up_ref, un_ref, w_ref, b_ref, a_ref):
        i = pl.program_id(1)
        row = lax.broadcasted_iota(jnp.int32, (tm, n), 0)
        c = []
        for half in range(2):
            main = u_ref[half]
            before = jnp.where(i > 0, up_ref[half, 7:8, :], 0.0)
            after = jnp.where(i < nblk - 1, un_ref[half, 0:1, :], 0.0)
            um = jnp.where(row == 0, before, pltpu.roll(main, 1, 0))
            up = jnp.where(row == tm - 1, after, pltpu.roll(main, tm - 1, 0))
            w = w_ref[half]
            c.append(um * w[0:1] + main * w[1:2] + up * w[2:3] + b_ref[half])
        a_ref[...] = (c[0] * _sigmoid(c[0]) * c[1]).astype(BF16)

    main, prev, nxt = _halo_specs(tm, n, S)
    return _pc(body, name=name, grid=(J, nblk),
               in_specs=[main, prev, nxt, pl.BlockSpec((2, None, 3, n), lambda j, i: (0, j, 0, 0)),
                         pl.BlockSpec((2, None, 1, n), lambda j, i: (0, j, 0, 0))],
               out_specs=pl.BlockSpec((None, tm, n), lambda j, i: (j, i, 0)),
               out_shape=jax.ShapeDtypeStruct((J, S, n), BF16),
               compiler_params=_sem("parallel", "parallel"))(u, u, u, cw, cb)


def _ffn_act_bwd(name, u, da, cw, cb):
    _, J, S, n = u.shape
    tm = _tile(S, 256, 16)
    nblk = S // tm
    te = tm + 16
    nb = tm // 8
    last8 = S // 8 - 1

    def body(u_ref, up_ref, un_ref, da_ref, dap_ref, dan_ref, w_ref, b_ref, du_ref, st_ref):
        i = pl.program_id(1)

        @pl.when(i == 0)
        def _():
            st_ref[...] = jnp.zeros_like(st_ref)

        r = lax.broadcasted_iota(jnp.int32, (te, n), 0)
        t = i * tm - 8 + r
        valid = (t >= 0) & (t < S)
        mid = (r >= 8) & (r < tm + 8)
        da_e = jnp.where(valid, jnp.concatenate([dap_ref[...], da_ref[...], dan_ref[...]], axis=0).astype(F32), 0.0)
        ue, c = [], []
        for half in range(2):
            e = jnp.where(valid, jnp.concatenate([up_ref[half], u_ref[half], un_ref[half]], axis=0), 0.0)
            w = w_ref[half]
            ue.append(e)
            c.append(pltpu.roll(e, 1, 0) * w[0:1] + e * w[1:2] + pltpu.roll(e, te - 1, 0) * w[2:3] + b_ref[half])
        sg = _sigmoid(c[0])
        dc = [jnp.where(valid, da_e * c[1] * (sg * (1.0 + c[0] * (1.0 - sg))), 0.0),
              jnp.where(valid, da_e * (c[0] * sg), 0.0)]
        r8 = lax.broadcasted_iota(jnp.int32, (8, n), 0)
        for half in range(2):
            w, d, e = w_ref[half], dc[half], ue[half]
            du = pltpu.roll(d, te - 1, 0) * w[0:1] + d * w[1:2] + pltpu.roll(d, 1, 0) * w[2:3]
            du_ref[half] = du[8:tm + 8].astype(BF16)
            dm = jnp.where(mid, d, 0.0)
            s0 = jnp.sum(dm * pltpu.roll(e, 1, 0), axis=0, keepdims=True)
            s1 = jnp.sum(dm * e, axis=0, keepdims=True)
            s2 = jnp.sum(dm * pltpu.roll(e, te - 1, 0), axis=0, keepdims=True)
            s3 = jnp.sum(dm, axis=0, keepdims=True)
            st_ref[half] += jnp.where(r8 == 0, s0, jnp.where(r8 == 1, s1, jnp.where(r8 == 2, s2,
                                      jnp.where(r8 == 3, s3, 0.0))))

    main, prev, nxt = _halo_specs(tm, n, S)
    dmain = pl.BlockSpec((None, tm, n), lambda j, i: (j, i, 0))
    dprev = pl.BlockSpec((None, 8, n), lambda j, i: (j, jnp.maximum(i * nb - 1, 0), 0))
    dnxt = pl.BlockSpec((None, 8, n), lambda j, i: (j, jnp.minimum((i + 1) * nb, last8), 0))
    return _pc(body, name=name, grid=(J, nblk),
               in_specs=[main, prev, nxt, dmain, dprev, dnxt,
                         pl.BlockSpec((2, None, 3, n), lambda j, i: (0, j, 0, 0)),
                         pl.BlockSpec((2, None, 1, n), lambda j, i: (0, j, 0, 0))],
               out_specs=(main, pl.BlockSpec((2, None, 8, n), lambda j, i: (0, j, 0, 0))),
               out_shape=(jax.ShapeDtypeStruct((2, J, S, n), BF16), jax.ShapeDtypeStruct((2, J, 8, n), F32)),
               compiler_params=_sem("parallel", "arbitrary"))(u, u, u, da, da, da, cw, cb)


def _window_count(t, w, S):
    lo = jnp.maximum(t - w // 2, 0)
    hi = jnp.minimum(t + w - w // 2, S)
    return (hi - lo).astype(F32)


def _trailing_sums(x, w):
    k = 1
    while k < w:
        x = x + pltpu.roll(x, k, 0)
        k *= 2
    return x


def _pool_window(name, h, group_w, adjoint, out_dtype):
    S, D = h.shape
    SP = S + 2 * POOL_PAD
    per_group = group_w // LANES

    def body(h_ref, o_ref, xp):
        g = pl.program_id(0) // per_group
        t = lax.broadcasted_iota(jnp.int32, (S, LANES), 0)
        xp[0:POOL_PAD, :] = jnp.zeros((POOL_PAD, LANES), F32)
        xp[S + POOL_PAD:SP, :] = jnp.zeros((POOL_PAD, LANES), F32)
        for gi, w in enumerate(POOL_WINDOWS):
            @pl.when(g == gi)
            def _():
                hv = h_ref[...]
                cnt = _window_count(t, w, S)
                xp[POOL_PAD:S + POOL_PAD, :] = hv / cnt if adjoint else hv
                ahead = w // 2 if adjoint else w // 2 - 1
                sw = _trailing_sums(xp[...], w)
                if ahead:
                    sw = pltpu.roll(sw, SP - ahead, 0)
                win = sw[POOL_PAD:S + POOL_PAD]
                o_ref[...] = ((win if adjoint else win / cnt) - hv).astype(out_dtype)

    col = pl.BlockSpec((S, LANES), lambda j: (0, j))
    return _pc(body, name=name, grid=(D // LANES,), in_specs=[col], out_specs=col,
               out_shape=jax.ShapeDtypeStruct((S, D), out_dtype),
               scratch_shapes=[pltpu.VMEM((SP, LANES), F32)], compiler_params=_sem("parallel"))(h)


def _pool_proj(name, mixed, w, scale, res):
    S, D = mixed.shape
    G, gw, _ = w.shape
    tm = _tile(S, 512, 16)

    def body(m_ref, w_ref, s_ref, r_ref, o_ref):
        for g in range(G):
            lo, hi = g * gw, (g + 1) * gw
            y = jnp.dot(m_ref[:, lo:hi], w_ref[g], preferred_element_type=F32)
            o_ref[:, lo:hi] = r_ref[:, lo:hi] + y * s_ref[:, lo:hi]

    row = pl.BlockSpec((tm, D), lambda i: (i, 0))
    return _pc(body, name=name, grid=(S // tm,),
               in_specs=[row, pl.BlockSpec((G, gw, gw), lambda i: (0, 0, 0)), pl.BlockSpec((1, D), lambda i: (0, 0)), row],
               out_specs=row, out_shape=jax.ShapeDtypeStruct((S, D), F32),
               compiler_params=_sem("parallel"))(mixed, w, scale, res)


def _pool_proj_bwd(name, dy, mixed, w, scale):
    S, D = mixed.shape
    G, gw, _ = w.shape
    tm = _tile(S, 512, 16)

    def body(dy_ref, m_ref, w_ref, s_ref, dm_ref, dw_ref, ds_ref):
        @pl.when(pl.program_id(0) == 0)
        def _():
            dw_ref[...] = jnp.zeros_like(dw_ref)
            ds_ref[...] = jnp.zeros_like(ds_ref)

        for g in range(G):
            lo, hi = g * gw, (g + 1) * gw
            mg, dyg = m_ref[:, lo:hi], dy_ref[:, lo:hi]
            y = jnp.dot(mg, w_ref[g], preferred_element_type=F32)
            ds_ref[:, lo:hi] += jnp.sum(dyg * y, axis=0, keepdims=True)
            dyp = (dyg * s_ref[:, lo:hi]).astype(BF16)
            dm_ref[:, lo:hi] = lax.dot_general(dyp, w_ref[g], _NT, preferred_element_type=F32)
            dw_ref[g] += lax.dot_general(mg, dyp, _TN, preferred_element_type=F32)

    row = pl.BlockSpec((tm, D), lambda i: (i, 0))
    wsp = pl.BlockSpec((G, gw, gw), lambda i: (0, 0, 0))
    vec = pl.BlockSpec((1, D), lambda i: (0, 0))
    return _pc(body, name=name, grid=(S // tm,), in_specs=[row, row, wsp, vec], out_specs=(row, wsp, vec),
               out_shape=(jax.ShapeDtypeStruct((S, D), F32), jax.ShapeDtypeStruct((G, gw, gw), F32),
                          jax.ShapeDtypeStruct((1, D), F32)),
               compiler_params=_sem("arbitrary"))(dy, mixed, w, scale)


def _adamw(name, w, g, m, v):
    shape = w.shape
    C = shape[-1]
    R = w.size // C
    tm = _tile(R, 512, 8)

    def body(w_ref, g_ref, m_ref, v_ref, d_ref, nm_ref, nv_ref):
        gv = g_ref[...]
        nm = ADAM_B1 * m_ref[...] + (1.0 - ADAM_B1) * gv
        nv = ADAM_B2 * v_ref[...] + (1.0 - ADAM_B2) * (gv * gv)
        m_hat = nm / (1.0 - ADAM_B1 ** ADAM_STEP)
        v_hat = nv / (1.0 - ADAM_B2 ** ADAM_STEP)
        d_ref[...] = -ADAM_LR * (m_hat / (jnp.sqrt(v_hat) + ADAM_EPS) + ADAM_WD * w_ref[...])
        nm_ref[...] = nm
        nv_ref[...] = nv

    blk = pl.BlockSpec((tm, C), lambda i: (i, 0))
    sd = jax.ShapeDtypeStruct((R, C), F32)
    outs = _pc(body, name=name, grid=(R // tm,), in_specs=[blk] * 4, out_specs=(blk,) * 3, out_shape=(sd,) * 3,
               compiler_params=_sem("parallel"))(*(a.reshape(R, C) for a in (w, g, m, v)))
    return tuple(o.reshape(shape) for o in outs)


def _position():
    return lax.axis_index("x"), lax.axis_index("y"), lax.axis_index("c")


def _flip(v, bit):
    return 1 - v if bit else v


def _allgather_small(name, v):
    R, W = v.shape

    def body(v_ref, out_ref, send_sems, recv_sems):
        x, y, c = _position()
        me = 4 * x + 2 * y + c
        out_ref[me] = v_ref[...]
        sends = []
        for k in range(1, N_DEV):
            peer = (_flip(x, k & 4), _flip(y, k & 2), _flip(c, k & 1))
            cp = pltpu.make_async_remote_copy(src_ref=v_ref, dst_ref=out_ref.at[me], send_sem=send_sems.at[k - 1],
                                              recv_sem=recv_sems.at[k - 1], device_id=peer, device_id_type=MESH)
            cp.start()
            sends.append(cp)
        for k in range(1, N_DEV):
            peer = (_flip(x, k & 4), _flip(y, k & 2), _flip(c, k & 1))
            slot = 4 * peer[0] + 2 * peer[1] + peer[2]
            pltpu.make_async_remote_copy(src_ref=v_ref, dst_ref=out_ref.at[slot], send_sem=send_sems.at[k - 1],
                                         recv_sem=recv_sems.at[k - 1], device_id=peer, device_id_type=MESH).wait_recv()
        for cp in sends:
            cp.wait_send()

    vm = pl.BlockSpec(memory_space=pltpu.VMEM)
    return _pc(body, name=name, in_specs=[vm], out_specs=vm, out_shape=jax.ShapeDtypeStruct((N_DEV, R, W), F32),
               scratch_shapes=[pltpu.SemaphoreType.DMA((N_DEV - 1,)), pltpu.SemaphoreType.DMA((N_DEV - 1,))])(v)


def _sum_slots(name, a):
    n, R, W = a.shape

    def body(a_ref, o_ref):
        acc = a_ref[0]
        for s in range(1, n):
            acc = acc + a_ref[s]
        o_ref[...] = acc

    return _pc(body, name=name, grid=(1,), in_specs=[pl.BlockSpec((n, R, W), lambda i: (0, 0, 0))],
               out_specs=pl.BlockSpec((R, W), lambda i: (0, 0)), out_shape=jax.ShapeDtypeStruct((R, W), F32))(a)


def _allgather_big(name, blk):
    L, W = blk.shape

    def body(b_ref, out_ref, send_sems, recv_sems, local_sem):
        x, y, c = _position()
        me, sibling = (x, y, c), (x, y, 1 - c)
        chips = [(1 - x, y), (x, 1 - y), (1 - x, 1 - y)]

        def slot(px, py, pc):
            return out_ref.at[4 * px + 2 * py + pc]

        def copy(k, block, to, src=None):
            return pltpu.make_async_remote_copy(src_ref=slot(*block) if src is None else src, dst_ref=slot(*block),
                                                send_sem=send_sems.at[k], recv_sem=recv_sems.at[k],
                                                device_id=to, device_id_type=MESH)

        mine = pltpu.make_async_copy(b_ref, slot(*me), local_sem)
        mine.start()
        first = [copy(0, me, sibling, src=b_ref)]
        first += [copy(1 + j, me, (*chip, c), src=b_ref) for j, chip in enumerate(chips)]
        for cp in first:
            cp.start()
        passed = [copy(4 + j, (*chip, c), sibling) for j, chip in enumerate(chips)]
        for j, chip in enumerate(chips):
            copy(1 + j, (*chip, c), me).wait_recv()
            passed[j].start()
        copy(0, sibling, me).wait_recv()
        for j, chip in enumerate(chips):
            copy(4 + j, (*chip, 1 - c), me).wait_recv()
        for cp in first + passed:
            cp.wait_send()
        mine.wait()

    hbm = pl.BlockSpec(memory_space=pl.ANY)
    return _pc(body, name=name, in_specs=[hbm], out_specs=hbm,
               out_shape=jax.ShapeDtypeStruct((N_DEV, L, W), blk.dtype),
               scratch_shapes=[pltpu.SemaphoreType.DMA((7,)), pltpu.SemaphoreType.DMA((7,)),
                               pltpu.SemaphoreType.DMA])(blk)


def _swap_with_sibling(name, g4):
    n, _, L, W = g4.shape

    def body(g_ref, r_ref, send_sems, recv_sems):
        x, y, c = _position()
        cps = [pltpu.make_async_remote_copy(src_ref=g_ref.at[k, 1 - c], dst_ref=r_ref.at[k], send_sem=send_sems.at[k],
                                            recv_sem=recv_sems.at[k], device_id=(x, y, 1 - c), device_id_type=MESH)
               for k in range(n)]
        for cp in cps:
            cp.start()
        for cp in cps:
            cp.wait()

    hbm = pl.BlockSpec(memory_space=pl.ANY)
    return _pc(body, name=name, in_specs=[hbm], out_specs=hbm, out_shape=jax.ShapeDtypeStruct((n, L, W), F32),
               scratch_shapes=[pltpu.SemaphoreType.DMA((n,)), pltpu.SemaphoreType.DMA((n,))])(g4)


def _add_sibling(name, g4, r1, pos):
    n, _, L, W = g4.shape
    tl = _tile(L, 512, 16)

    def body(pos_ref, g_ref, r_ref, tb_ref, own_ref):
        t = g_ref[...] + r_ref[...]
        tb_ref[...] = t.astype(BF16)

        @pl.when(pl.program_id(1) == pos_ref[1])
        def _():
            own_ref[...] = t

    gs = pltpu.PrefetchScalarGridSpec(
        num_scalar_prefetch=1, grid=(L // tl, n),
        in_specs=[pl.BlockSpec((None, None, tl, W), lambda i, k, p: (k, p[0], i, 0)),
                  pl.BlockSpec((None, tl, W), lambda i, k, p: (k, i, 0))],
        out_specs=(pl.BlockSpec((None, tl, W), lambda i, k, p: (k, i, 0)),
                   pl.BlockSpec((tl, W), lambda i, k, p: (i, 0))))
    return _pc(body, name=name, grid_spec=gs,
               out_shape=(jax.ShapeDtypeStruct((n, L, W), BF16), jax.ShapeDtypeStruct((L, W), F32)),
               compiler_params=_sem("parallel", "arbitrary"))(pos, g4, r1)


def _swap_with_chips(name, tb):
    _, L, W = tb.shape

    def body(t_ref, r_ref, send_sems, recv_sems):
        x, y, c = _position()
        cps = []
        for j in range(1, 4):
            px, py = _flip(x, j & 2), _flip(y, j & 1)
            cps.append(pltpu.make_async_remote_copy(src_ref=t_ref.at[2 * px + py], dst_ref=r_ref.at[j - 1],
                                                    send_sem=send_sems.at[j - 1], recv_sem=recv_sems.at[j - 1],
                                                    device_id=(px, py, c), device_id_type=MESH))
        for cp in cps:
            cp.start()
        for cp in cps:
            cp.wait()

    hbm = pl.BlockSpec(memory_space=pl.ANY)
    return _pc(body, name=name, in_specs=[hbm], out_specs=hbm, out_shape=jax.ShapeDtypeStruct((3, L, W), BF16),
               scratch_shapes=[pltpu.SemaphoreType.DMA((3,)), pltpu.SemaphoreType.DMA((3,))])(tb)


def _add_chips(name, own, r2):
    L, W = own.shape
    tl = _tile(L, 512, 16)

    def body(o_ref, r_ref, out_ref):
        acc = o_ref[...]
        for j in range(3):
            acc = acc + r_ref[j].astype(F32)
        out_ref[...] = acc

    return _pc(body, name=name, grid=(L // tl,),
               in_specs=[pl.BlockSpec((tl, W), lambda i: (i, 0)), pl.BlockSpec((3, tl, W), lambda i: (0, i, 0))],
               out_specs=pl.BlockSpec((tl, W), lambda i: (i, 0)), out_shape=jax.ShapeDtypeStruct((L, W), F32),
               compiler_params=_sem("parallel"))(own, r2)


def _gathered(stacked, shard_shape, axis):
    a = stacked.reshape((N_DEV,) + tuple(shard_shape))
    a = jnp.moveaxis(a, 0, axis)
    full = list(shard_shape)
    full[axis] *= N_DEV
    return a.reshape(full)


def _scattered(full, shard_shape, axis):
    split = list(full.shape)
    split[axis:axis + 1] = [N_DEV, shard_shape[axis]]
    a = jnp.moveaxis(full.reshape(split), axis, 0)
    return a.reshape(N_DEV, -1, PACK_W)


def _pad_rows(a, rows):
    return jnp.pad(a, ((0, rows - a.shape[0]), (0, 0)))


def _heads_major(a, S):
    return a.reshape(S, -1, HEAD_DIM).transpose(1, 0, 2)


def _heads_minor(a):
    return a.transpose(1, 0, 2).reshape(a.shape[1], -1)


def kernel(x, mem, attn_norm, attn_w_qkv, attn_q_gain, attn_k_gain, attn_w_o, pool_norm, pool_w, pool_scale, xattn_norm, mem_norm, xattn_w_q, xattn_w_kv, xattn_w_o, ffn_norm, ffn_w_up, ffn_conv_w, ffn_conv_b, ffn_w_down, final_norm, loss_target, m_attn_norm, m_attn_w_qkv, m_attn_q_gain, m_attn_k_gain, m_attn_w_o, m_pool_norm, m_pool_w, m_pool_scale, m_xattn_norm, m_mem_norm, m_xattn_w_q, m_xattn_w_kv, m_xattn_w_o, m_ffn_norm, m_ffn_w_up, m_ffn_conv_w, m_ffn_conv_b, m_ffn_w_down, m_final_norm, v_attn_norm, v_attn_w_qkv, v_attn_q_gain, v_attn_k_gain, v_attn_w_o, v_pool_norm, v_pool_w, v_pool_scale, v_xattn_norm, v_mem_norm, v_xattn_w_q, v_xattn_w_kv, v_xattn_w_o, v_ffn_norm, v_ffn_w_up, v_ffn_conv_w, v_ffn_conv_b, v_ffn_w_down, v_final_norm):
    names = ['attn_norm', 'attn_w_qkv', 'attn_q_gain', 'attn_k_gain', 'attn_w_o', 'pool_norm', 'pool_w', 'pool_scale',
             'xattn_norm', 'mem_norm', 'xattn_w_q', 'xattn_w_kv', 'xattn_w_o', 'ffn_norm', 'ffn_w_up', 'ffn_conv_w',
             'ffn_conv_b', 'ffn_w_down', 'final_norm']
    W = dict(zip(names, (attn_norm, attn_w_qkv, attn_q_gain, attn_k_gain, attn_w_o, pool_norm, pool_w, pool_scale,
                         xattn_norm, mem_norm, xattn_w_q, xattn_w_kv, xattn_w_o, ffn_norm, ffn_w_up, ffn_conv_w,
                         ffn_conv_b, ffn_w_down, final_norm)))
    Mo = dict(zip(names, (m_attn_norm, m_attn_w_qkv, m_attn_q_gain, m_attn_k_gain, m_attn_w_o, m_pool_norm, m_pool_w,
                          m_pool_scale, m_xattn_norm, m_mem_norm, m_xattn_w_q, m_xattn_w_kv, m_xattn_w_o, m_ffn_norm,
                          m_ffn_w_up, m_ffn_conv_w, m_ffn_conv_b, m_ffn_w_down, m_final_norm)))
    Vo = dict(zip(names, (v_attn_norm, v_attn_w_qkv, v_attn_q_gain, v_attn_k_gain, v_attn_w_o, v_pool_norm, v_pool_w,
                          v_pool_scale, v_xattn_norm, v_mem_norm, v_xattn_w_q, v_xattn_w_kv, v_xattn_w_o, v_ffn_norm,
                          v_ffn_w_up, v_ffn_conv_w, v_ffn_conv_b, v_ffn_w_down, v_final_norm)))

    S, D = x.shape[1], x.shape[2]
    n_layers = xattn_norm.shape[0]
    n_up = ffn_w_up.shape[2]
    qkv_w = attn_w_qkv.shape[2] * N_DEV
    n_heads = qkv_w // HEAD_DIM - 2 * N_KV_HEADS
    n_rot = (n_heads + N_KV_HEADS) * HEAD_DIM // LANES
    group_w = pool_w.shape[3]
    xs, mems, tgt = x[0], mem[0], loss_target[0]
    xi, yi, ci = _position()
    dev = 4 * xi + 2 * yi + ci
    pos = jnp.stack([ci, 2 * xi + yi]).astype(jnp.int32)

    big = ['attn_w_qkv', 'attn_w_o', 'pool_w', 'xattn_w_q', 'xattn_w_kv', 'xattn_w_o', 'ffn_w_up', 'ffn_w_down']
    shard_axis = {'attn_w_qkv': 2, 'attn_w_o': 1, 'pool_w': 2, 'xattn_w_q': 1, 'xattn_w_kv': 2, 'xattn_w_o': 1,
                  'ffn_w_up': 2, 'ffn_w_down': 1}
    rows = {n: W[n].size // PACK_W for n in big}
    offs, o = {}, 0
    for n in big:
        offs[n] = o
        o += rows[n]
    L = _round_up(o, 256)
    packed = _pad_rows(jnp.concatenate([W[n].reshape(-1, PACK_W).astype(BF16) for n in big], axis=0), L)
    w_all = _allgather_big("allgather_weights", packed)

    def full_w(n):
        return _gathered(w_all[:, offs[n]:offs[n] + rows[n]], W[n].shape, shard_axis[n])

    w_qkv, w_o = full_w('attn_w_qkv')[0], full_w('attn_w_o')[0]
    w_pool = full_w('pool_w')[0]
    w_xq, w_xkv, w_xo, w_down = full_w('xattn_w_q'), full_w('xattn_w_kv'), full_w('xattn_w_o'), full_w('ffn_w_down')
    w_up = w_all[:, offs['ffn_w_up']:offs['ffn_w_up'] + rows['ffn_w_up']].reshape(N_DEV, n_layers, D, n_up)

    small_vec = jnp.concatenate([pool_norm.reshape(-1), pool_scale.reshape(-1), ffn_conv_w.reshape(-1)])
    small_rows = _round_up(-(-small_vec.size // PACK_W), 8)
    small = _allgather_small("allgather_small", jnp.pad(small_vec, (0, small_rows * PACK_W - small_vec.size))
                             .reshape(small_rows, PACK_W)).reshape(N_DEV, -1)
    d_sh = pool_norm.shape[1]
    pool_norm_f = small[:, :d_sh].reshape(1, D)
    pool_scale_f = small[:, d_sh:2 * d_sh].reshape(1, D)
    conv_w_f = small[:, 2 * d_sh:2 * d_sh + ffn_conv_w.size].reshape(N_DEV, n_layers, 3, n_up)
    conv_b_f = ffn_conv_b.reshape(n_layers, N_DEV, 1, n_up)

    cos, sin = _rope_tables(S)
    bd = _head_mean_matrix()
    pad_w = qkv_w - (n_heads + N_KV_HEADS) * HEAD_DIM
    qk_gain = jnp.concatenate([jnp.tile(attn_q_gain[0], n_heads), jnp.tile(attn_k_gain[0], N_KV_HEADS),
                               jnp.ones((pad_w,), F32)]).reshape(1, qkv_w)
    qk_scale = jnp.concatenate([jnp.full((n_heads * HEAD_DIM,), HEAD_DIM ** -0.5, F32),
                                jnp.ones((qkv_w - n_heads * HEAD_DIM,), F32)]).reshape(1, qkv_w)

    saved = []

    def xattn_ffn_fwd(l, xin):
        hx = _rmsnorm(f"xattn_norm{l}", xin, xattn_norm[l:l + 1], BF16)
        memn = _rmsnorm(f"mem_norm{l}", mems, mem_norm[l:l + 1], BF16)
        qx = _mm_nn(f"xattn_q{l}", hx, w_xq[l], BF16)
        kv = _mm_nn(f"xattn_kv{l}", memn, w_xkv[l], BF16)
        ox = _xattn_fwd(f"xattn_fwd{l}", qx, kv)
        x2 = _mm_nn(f"xattn_o{l}", ox, w_xo[l], F32, res=xin)
        hf = _rmsnorm(f"ffn_norm{l}", x2, ffn_norm[l:l + 1], BF16)
        u = _mm_nn_bs(f"ffn_up{l}", hf, w_up[:, l], F32).reshape(2, N_DEV // 2, S, n_up)
        cw = conv_w_f[:, l].reshape(2, N_DEV // 2, 3, n_up)
        cb = conv_b_f[l].reshape(2, N_DEV // 2, 1, n_up)
        act = _ffn_act(f"ffn_act{l}", u, cw, cb)
        x3 = _mm_nn_as(f"ffn_down{l}", act, w_down[l], F32, x2)
        saved.append(dict(xin=xin, hx=hx, memn=memn, qx=qx, kv=kv, ox=ox, x2=x2, hf=hf, u=u, cw=cw, cb=cb, act=act))
        return x3

    h0 = _rmsnorm("attn_norm", xs, attn_norm, BF16)
    qkv = _mm_nn("attn_qkv", h0, w_qkv, F32)
    qkr = _heads_major(_qk_rope("qk_rope", qkv, qk_gain, qk_scale, cos, sin, bd, n_rot), S)
    q_hm, k_hm, v_hm = qkr[:n_heads], qkr[n_heads:n_heads + N_KV_HEADS], qkr[n_heads + N_KV_HEADS:]
    o_att = _heads_minor(_attn_fwd("attn_fwd", q_hm, k_hm, v_hm))
    x1 = _mm_nn("attn_o", o_att, w_o, F32, res=xs)
    x3 = xattn_ffn_fwd(0, x1)
    hp = _rmsnorm("pool_norm", x3, pool_norm_f, F32)
    mixed = _pool_window("pool_window", hp, group_w, False, BF16)
    x4 = _pool_proj("pool_proj", mixed, w_pool, pool_scale_f, x3)
    x6 = xattn_ffn_fwd(1, x4)

    G = {}
    g, d_final, lvec = _loss_head("loss_head", x6, final_norm.reshape(1, D), tgt)
    G['final_norm'] = d_final.reshape(D)
    loss = lax.psum(0.5 * jnp.sum(lvec) / D, ("x", "y", "c"))

    d_xn, d_mn, d_fn, d_xq, d_xkv, d_xo, d_up, d_cw, d_cb, d_down = ([None] * n_layers for _ in range(10))

    def xattn_ffn_bwd(l, g):
        sv = saved[l]
        d_act = _mm_nt_bs(f"ffn_down_dx{l}", g, w_down[l], N_DEV // 2, F32)
        d_down[l] = _mm_tn_as(f"ffn_down_dw{l}", sv['act'], g, F32)
        du, st = _ffn_act_bwd(f"ffn_act_bwd{l}", sv['u'], d_act, sv['cw'], sv['cb'])
        du = du.reshape(N_DEV, S, n_up)
        st = st.reshape(N_DEV, 8, n_up)
        d_cw[l], d_cb[l] = st[:, 0:3], st[:, 3].reshape(-1)
        d_up[l] = _mm_tn_bs(f"ffn_up_dw{l}", sv['hf'], du, F32)
        dhf = _mm_nt_abs(f"ffn_up_dx{l}", du, w_up[:, l], F32)
        g, d_fn[l] = _rmsnorm_bwd(f"ffn_norm_bwd{l}", sv['x2'], ffn_norm[l:l + 1], dhf, g)
        d_xo[l] = _mm_tn(f"xattn_o_dw{l}", sv['ox'], g, F32)
        do = _mm_nt(f"xattn_o_dx{l}", g, w_xo[l], BF16)
        dq, dkv = _xattn_bwd(f"xattn_bwd{l}", sv['qx'], sv['kv'], do)
        d_xq[l] = _mm_tn(f"xattn_q_dw{l}", sv['hx'], dq, F32)
        d_xkv[l] = _mm_tn(f"xattn_kv_dw{l}", sv['memn'], dkv, F32)
        dmemn = _mm_nt(f"xattn_kv_dx{l}", dkv, w_xkv[l], F32)
        _, d_mn[l] = _rmsnorm_bwd(f"mem_norm_bwd{l}", mems, mem_norm[l:l + 1], dmemn)
        dhx = _mm_nt(f"xattn_q_dx{l}", dq, w_xq[l], F32)
        g, d_xn[l] = _rmsnorm_bwd(f"xattn_norm_bwd{l}", sv['xin'], xattn_norm[l:l + 1], dhx, g)
        return g

    g = xattn_ffn_bwd(1, g)
    d_mixed, d_pool_w, d_pool_scale = _pool_proj_bwd("pool_proj_bwd", g, mixed, w_pool, pool_scale_f)
    dhp = _pool_window("pool_window_bwd", d_mixed, group_w, True, F32)
    g, d_pool_norm = _rmsnorm_bwd("pool_norm_bwd", x3, pool_norm_f, dhp, g)
    g = xattn_ffn_bwd(0, g)
    d_wo = _mm_tn("attn_o_dw", o_att, g, F32)
    do_hm = _heads_major(_mm_nt("attn_o_dx", g, w_o, BF16), S)
    dq_hm, dk_hm, dv_hm = _attn_bwd("attn_bwd", q_hm, k_hm, v_hm, do_hm)
    d_qkr = _heads_minor(jnp.concatenate([dq_hm, dk_hm, dv_hm], axis=0))
    d_qkv, d_gain = _qk_rope_bwd("qk_rope_bwd", d_qkr, qkv, qk_gain, qk_scale, cos, sin, bd, n_rot)
    d_wqkv = _mm_tn("attn_qkv_dw", h0, d_qkv, F32)
    dh0 = _mm_nt("attn_qkv_dx", d_qkv, w_qkv, F32)
    grad_x, d_attn_norm = _rmsnorm_bwd("attn_norm_bwd", xs, attn_norm, dh0, g)

    full_g = {'attn_w_qkv': d_wqkv[None], 'attn_w_o': d_wo[None], 'pool_w': d_pool_w[None],
              'xattn_w_q': jnp.stack(d_xq), 'xattn_w_kv': jnp.stack(d_xkv), 'xattn_w_o': jnp.stack(d_xo),
              'ffn_w_down': jnp.stack(d_down)}
    parts = []
    for n in big:
        if n == 'ffn_w_up':
            parts.append(jnp.stack(d_up, axis=1).reshape(N_DEV, -1, PACK_W))
        else:
            parts.append(_scattered(full_g[n], W[n].shape, shard_axis[n]))
    g_all = jnp.pad(jnp.concatenate(parts, axis=1), ((0, 0), (0, L - o), (0, 0)))
    g4 = g_all.reshape(N_DEV // 2, 2, L, PACK_W)
    r1 = _swap_with_sibling("reduce_swap_sibling", g4)
    tb, own = _add_sibling("reduce_add_sibling", g4, r1, pos)
    r2 = _swap_with_chips("reduce_swap_chips", tb)
    red = _add_chips("reduce_add_chips", own, r2)
    for n in big:
        G[n] = red[offs[n]:offs[n] + rows[n]].reshape(W[n].shape)

    hq = n_heads * HEAD_DIM
    small_g = {'attn_norm': d_attn_norm, 'attn_q_gain': d_gain[0, :hq].reshape(n_heads, HEAD_DIM).sum(0),
               'attn_k_gain': d_gain[0, hq:hq + N_KV_HEADS * HEAD_DIM].reshape(N_KV_HEADS, HEAD_DIM).sum(0),
               'pool_norm': d_pool_norm, 'pool_scale': d_pool_scale,
               'xattn_norm': jnp.concatenate(d_xn), 'mem_norm': jnp.concatenate(d_mn), 'ffn_norm': jnp.concatenate(d_fn),
               'ffn_conv_w': jnp.stack(d_cw, axis=1), 'ffn_conv_b': jnp.stack(d_cb)}
    order = list(small_g)
    flat = jnp.concatenate([small_g[n].reshape(-1) for n in order] + [G['final_norm']])
    ar_rows = _round_up(-(-flat.size // PACK_W), 8)
    flat = jnp.pad(flat, (0, ar_rows * PACK_W - flat.size)).reshape(ar_rows, PACK_W)
    summed = _sum_slots("allreduce_sum", _allgather_small("allreduce_gather", flat)).reshape(-1)
    at = 0
    for n in order + ['final_norm']:
        size = G['final_norm'].size if n == 'final_norm' else small_g[n].size
        piece = summed[at:at + size]
        at += size
        if n in ('pool_norm', 'pool_scale'):
            piece = lax.dynamic_slice(piece, (dev * d_sh,), (d_sh,))
        elif n == 'ffn_conv_w':
            piece = lax.dynamic_index_in_dim(piece.reshape(N_DEV, n_layers, 3, n_up), dev, 0, keepdims=False)
        G[n] = piece.reshape(W[n].shape)

    deltas, new_m, new_v = [], [], []
    for n in names:
        d, nm, nv = _adamw(f"adamw_{n}", W[n], G[n], Mo[n], Vo[n])
        deltas.append(d)
        new_m.append(nm)
        new_v.append(nv)
    return (loss, grad_x[None], *[G[n] for n in names], *deltas, *new_m, *new_v)
```

```python
import jax
import jax.numpy as jnp
from jax import lax
from jax.experimental import pallas as pl
from jax.experimental.pallas import tpu as pltpu

F32 = jnp.float32
BF16 = jnp.bfloat16
MESH = pl.DeviceIdType.MESH

N_DEV = 8
EPS = 1e-6
HEAD_DIM = 64
N_KV_HEADS = 4
X_HEADS = 4
GRID_W = 64
ROPE_THETA = 10000.0
ROPE_PAIRS = HEAD_DIM // 4
POOL_WINDOWS = (2, 4, 8, 16)
POOL_PAD = 16
KEY_CHUNK = 1024
LANES = 128
PACK_W = 1024
ADAM_LR, ADAM_B1, ADAM_B2, ADAM_EPS, ADAM_WD, ADAM_STEP = 0.001, 0.9, 0.999, 1e-08, 0.01, 10

_NN = (((1,), (0,)), ((), ()))
_NT = (((1,), (1,)), ((), ()))
_TN = (((0,), (0,)), ((), ()))


def _pc(body, *, name, **kw):
    return pl.pallas_call(body, name=name, **kw)


def _sem(*kinds):
    return pltpu.CompilerParams(dimension_semantics=kinds)


def _tile(n, pref, mult):
    best = None
    for t in range(mult, min(n, pref) + 1, mult):
        if n % t == 0:
            best = t
    return n if best is None else best


def _round_up(n, m):
    return (n + m - 1) // m * m


def _mm_call(name, a, b, dims, grid, a_spec, b_spec, o_spec, out_shape, kaxis, res=None, res_spec=None):
    nk = grid[kaxis]
    acc_shape = tuple(d for d in o_spec.block_shape if d is not None)

    def body(*refs):
        if res is None:
            a_ref, b_ref, o_ref, acc = refs
            r_ref = None
        else:
            a_ref, b_ref, r_ref, o_ref, acc = refs
        k = pl.program_id(kaxis)

        @pl.when(k == 0)
        def _():
            acc[...] = jnp.zeros_like(acc)

        acc[...] += lax.dot_general(a_ref[...].astype(BF16), b_ref[...].astype(BF16), dims,
                                    preferred_element_type=F32)

        @pl.when(k == nk - 1)
        def _():
            r = acc[...]
            if r_ref is not None:
                r = r + r_ref[...]
            o_ref[...] = r.astype(o_ref.dtype)

    sem = tuple("arbitrary" if ax == kaxis else "parallel" for ax in range(len(grid)))
    ins = [a, b] if res is None else [a, b, res]
    specs = [a_spec, b_spec] if res is None else [a_spec, b_spec, res_spec]
    return _pc(body, name=name, grid=grid, in_specs=specs, out_specs=o_spec, out_shape=out_shape,
               scratch_shapes=[pltpu.VMEM(acc_shape, F32)], compiler_params=_sem(*sem))(*ins)


def _mm_nn(name, a, b, out_dtype, res=None):
    M, K = a.shape
    N = b.shape[1]
    tm, tn, tk = _tile(M, 512, 16), _tile(N, 1024, LANES), _tile(K, 1024, LANES)
    return _mm_call(name, a, b, _NN, (M // tm, N // tn, K // tk),
                    pl.BlockSpec((tm, tk), lambda i, j, k: (i, k)),
                    pl.BlockSpec((tk, tn), lambda i, j, k: (k, j)),
                    pl.BlockSpec((tm, tn), lambda i, j, k: (i, j)),
                    jax.ShapeDtypeStruct((M, N), out_dtype), 2, res,
                    pl.BlockSpec((tm, tn), lambda i, j, k: (i, j)))


def _mm_nt(name, a, b, out_dtype):
    M, K = a.shape
    N = b.shape[0]
    tm, tn, tk = _tile(M, 512, 16), _tile(N, 1024, LANES), _tile(K, 1024, LANES)
    return _mm_call(name, a, b, _NT, (M // tm, N // tn, K // tk),
                    pl.BlockSpec((tm, tk), lambda i, j, k: (i, k)),
                    pl.BlockSpec((tn, tk), lambda i, j, k: (j, k)),
                    pl.BlockSpec((tm, tn), lambda i, j, k: (i, j)),
                    jax.ShapeDtypeStruct((M, N), out_dtype), 2)


def _mm_tn(name, a, b, out_dtype):
    R, M = a.shape
    N = b.shape[1]
    tm, tn, tr = _tile(M, 1024, LANES), _tile(N, 1024, LANES), _tile(R, 512, 16)
    return _mm_call(name, a, b, _TN, (M // tm, N // tn, R // tr),
                    pl.BlockSpec((tr, tm), lambda i, j, k: (k, i)),
                    pl.BlockSpec((tr, tn), lambda i, j, k: (k, j)),
                    pl.BlockSpec((tm, tn), lambda i, j, k: (i, j)),
                    jax.ShapeDtypeStruct((M, N), out_dtype), 2)


def _mm_nn_bs(name, a, b, out_dtype, layer=0):
    M, K = a.shape
    J, _, n = b.shape
    tm, tk = _tile(M, 512, 16), _tile(K, 1024, LANES)
    first = layer * (K // tk)
    return _mm_call(name, a, b, _NN, (J, M // tm, K // tk),
                    pl.BlockSpec((tm, tk), lambda j, i, k: (i, k)),
                    pl.BlockSpec((None, tk, n), lambda j, i, k: (j, first + k, 0)),
                    pl.BlockSpec((None, tm, n), lambda j, i, k: (j, i, 0)),
                    jax.ShapeDtypeStruct((J, M, n), out_dtype), 2)


def _mm_nt_bs(name, a, b, J, out_dtype):
    M, K = a.shape
    n = b.shape[0] // J
    tm, tk = _tile(M, 512, 16), _tile(K, 1024, LANES)
    return _mm_call(name, a, b, _NT, (J, M // tm, K // tk),
                    pl.BlockSpec((tm, tk), lambda j, i, k: (i, k)),
                    pl.BlockSpec((n, tk), lambda j, i, k: (j, k)),
                    pl.BlockSpec((None, tm, n), lambda j, i, k: (j, i, 0)),
                    jax.ShapeDtypeStruct((J, M, n), out_dtype), 2)


def _mm_nn_as(name, a, b, out_dtype, res):
    J, M, n = a.shape
    N = b.shape[1]
    tm, tn = _tile(M, 512, 16), _tile(N, 1024, LANES)
    return _mm_call(name, a, b, _NN, (M // tm, N // tn, J),
                    pl.BlockSpec((None, tm, n), lambda i, j, k: (k, i, 0)),
                    pl.BlockSpec((n, tn), lambda i, j, k: (k, j)),
                    pl.BlockSpec((tm, tn), lambda i, j, k: (i, j)),
                    jax.ShapeDtypeStruct((M, N), out_dtype), 2, res,
                    pl.BlockSpec((tm, tn), lambda i, j, k: (i, j)))


def _mm_nt_abs(name, a, b, N, out_dtype, layer=0):
    J, M, n = a.shape
    tm, tn = _tile(M, 512, 16), _tile(N, 1024, LANES)
    first = layer * (N // tn)
    return _mm_call(name, a, b, _NT, (M // tm, N // tn, J),
                    pl.BlockSpec((None, tm, n), lambda i, j, k: (k, i, 0)),
                    pl.BlockSpec((None, tn, n), lambda i, j, k: (k, first + j, 0)),
                    pl.BlockSpec((tm, tn), lambda i, j, k: (i, j)),
                    jax.ShapeDtypeStruct((M, N), out_dtype), 2)


def _mm_tn_bs(name, a, b, out_dtype):
    R, M = a.shape
    J, _, n = b.shape
    tm, tr = _tile(M, 1024, LANES), _tile(R, 512, 16)
    return _mm_call(name, a, b, _TN, (J, M // tm, R // tr),
                    pl.BlockSpec((tr, tm), lambda j, i, k: (k, i)),
                    pl.BlockSpec((None, tr, n), lambda j, i, k: (j, k, 0)),
                    pl.BlockSpec((None, tm, n), lambda j, i, k: (j, i, 0)),
                    jax.ShapeDtypeStruct((J, M, n), out_dtype), 2)


def _mm_tn_as(name, a, b, out_dtype):
    J, R, n = a.shape
    N = b.shape[1]
    tn, tr = _tile(N, 1024, LANES), _tile(R, 512, 16)
    return _mm_call(name, a, b, _TN, (J, N // tn, R // tr),
                    pl.BlockSpec((None, tr, n), lambda j, jn, k: (j, k, 0)),
                    pl.BlockSpec((tr, tn), lambda j, jn, k: (k, jn)),
                    pl.BlockSpec((n, tn), lambda j, jn, k: (j, jn)),
                    jax.ShapeDtypeStruct((J * n, N), out_dtype), 2)


def _rmsnorm(name, x, g, out_dtype):
    R, D = x.shape
    tm = _tile(R, 512, 16)

    def body(x_ref, g_ref, o_ref):
        xv = x_ref[...]
        r = lax.rsqrt(jnp.mean(xv * xv, axis=-1, keepdims=True) + EPS)
        o_ref[...] = (xv * r * g_ref[...]).astype(o_ref.dtype)

    return _pc(body, name=name, grid=(R // tm,),
               in_specs=[pl.BlockSpec((tm, D), lambda i: (i, 0)), pl.BlockSpec((1, D), lambda i: (0, 0))],
               out_specs=pl.BlockSpec((tm, D), lambda i: (i, 0)),
               out_shape=jax.ShapeDtypeStruct((R, D), out_dtype), compiler_params=_sem("parallel"))(x, g)


def _rmsnorm_bwd(name, x, g, dh, dres=None):
    R, D = x.shape
    tm = _tile(R, 512, 16)

    def body(*refs):
        if dres is None:
            x_ref, g_ref, dh_ref, dx_ref, dg_ref = refs
            dres_ref = None
        else:
            x_ref, g_ref, dh_ref, dres_ref, dx_ref, dg_ref = refs
        xv = x_ref[...]
        r = lax.rsqrt(jnp.mean(xv * xv, axis=-1, keepdims=True) + EPS)
        xh = xv * r
        dhv = dh_ref[...].astype(F32)

        @pl.when(pl.program_id(0) == 0)
        def _():
            dg_ref[...] = jnp.zeros_like(dg_ref)

        dg_ref[...] += jnp.sum(dhv * xh, axis=0, keepdims=True)
        dxh = dhv * g_ref[...]
        dx = r * (dxh - xh * jnp.mean(dxh * xh, axis=-1, keepdims=True))
        if dres_ref is not None:
            dx = dx + dres_ref[...]
        dx_ref[...] = dx

    row = pl.BlockSpec((tm, D), lambda i: (i, 0))
    vec = pl.BlockSpec((1, D), lambda i: (0, 0))
    ins = [x, g, dh] + ([] if dres is None else [dres])
    specs = [row, vec, row] + ([] if dres is None else [row])
    return _pc(body, name=name, grid=(R // tm,), in_specs=specs, out_specs=(row, vec),
               out_shape=(jax.ShapeDtypeStruct((R, D), F32), jax.ShapeDtypeStruct((1, D), F32)),
               compiler_params=_sem("arbitrary"))(*ins)


def _loss_head(name, x, g, tgt):
    R, D = x.shape
    tm = _tile(R, 512, 16)

    def body(x_ref, g_ref, t_ref, dx_ref, dg_ref, l_ref):
        xv = x_ref[...]
        r = lax.rsqrt(jnp.mean(xv * xv, axis=-1, keepdims=True) + EPS)
        xh = xv * r
        err = xh * g_ref[...] - t_ref[...]

        @pl.when(pl.program_id(0) == 0)
        def _():
            dg_ref[...] = jnp.zeros_like(dg_ref)
            l_ref[...] = jnp.zeros_like(l_ref)

        l_ref[...] += jnp.sum(err * err, axis=0, keepdims=True)
        dy = err * (1.0 / D)
        dg_ref[...] += jnp.sum(dy * xh, axis=0, keepdims=True)
        dxh = dy * g_ref[...]
        dx_ref[...] = r * (dxh - xh * jnp.mean(dxh * xh, axis=-1, keepdims=True))

    row = pl.BlockSpec((tm, D), lambda i: (i, 0))
    vec = pl.BlockSpec((1, D), lambda i: (0, 0))
    return _pc(body, name=name, grid=(R // tm,), in_specs=[row, vec, row], out_specs=(row, vec, vec),
               out_shape=(jax.ShapeDtypeStruct((R, D), F32), jax.ShapeDtypeStruct((1, D), F32),
                          jax.ShapeDtypeStruct((1, D), F32)),
               compiler_params=_sem("arbitrary"))(x, g, tgt)


def _rope_tables(S):
    n_rows = S // GRID_W
    row = jnp.repeat(jnp.arange(n_rows, dtype=F32), GRID_W)
    col = jnp.tile(jnp.arange(GRID_W, dtype=F32), n_rows)
    inv_freq = ROPE_THETA ** (-jnp.arange(ROPE_PAIRS, dtype=F32) / ROPE_PAIRS)
    ang = jnp.stack([row[:, None] * inv_freq, col[:, None] * inv_freq], axis=1)
    cos, sin = jnp.cos(ang), jnp.sin(ang)
    c = jnp.broadcast_to(cos[:, :, None, :], (S, 2, 2, ROPE_PAIRS)).reshape(S, HEAD_DIM)
    s = jnp.stack([-sin, sin], axis=2).reshape(S, HEAD_DIM)
    reps = LANES // HEAD_DIM
    return jnp.tile(c, (1, reps)), jnp.tile(s, (1, reps))


def _head_mean_matrix():
    h = jnp.arange(LANES) // HEAD_DIM
    return jnp.where(h[:, None] == h[None, :], 1.0 / HEAD_DIM, 0.0).astype(F32)


def _swap_halves(y):
    lane = lax.broadcasted_iota(jnp.int32, y.shape, 1)
    return jnp.where(lane % 32 < 16, pltpu.roll(y, LANES - 16, 1), pltpu.roll(y, 16, 1))


def _qk_rope(name, qkv, gain, scale, cos, sin, bd, n_rot):
    S, W = qkv.shape
    tm = _tile(S, 512, 16)

    def body(x_ref, g_ref, s_ref, c_ref, sn_ref, bd_ref, o_ref):
        j = pl.program_id(1)
        xv = x_ref[...]

        @pl.when(j < n_rot)
        def _():
            ms = jnp.dot(xv * xv, bd_ref[...], precision=lax.Precision.HIGHEST, preferred_element_type=F32)
            y = xv * lax.rsqrt(ms + EPS) * g_ref[...] * s_ref[...]
            o_ref[...] = (y * c_ref[...] + _swap_halves(y) * sn_ref[...]).astype(BF16)

        @pl.when(j >= n_rot)
        def _():
            o_ref[...] = xv.astype(BF16)

    blk = pl.BlockSpec((tm, LANES), lambda i, j: (i, j))
    vec = pl.BlockSpec((1, LANES), lambda i, j: (0, j))
    tab = pl.BlockSpec((tm, LANES), lambda i, j: (i, 0))
    return _pc(body, name=name, grid=(S // tm, W // LANES),
               in_specs=[blk, vec, vec, tab, tab, pl.BlockSpec((LANES, LANES), lambda i, j: (0, 0))],
               out_specs=blk, out_shape=jax.ShapeDtypeStruct((S, W), BF16),
               compiler_params=_sem("parallel", "parallel"))(qkv, gain, scale, cos, sin, bd)


def _qk_rope_bwd(name, d_out, qkv, gain, scale, cos, sin, bd, n_rot):
    S, W = qkv.shape
    tm = _tile(S, 512, 16)

    def body(d_ref, x_ref, g_ref, s_ref, c_ref, sn_ref, bd_ref, dx_ref, dg_ref):
        j, i = pl.program_id(0), pl.program_id(1)
        dv = d_ref[...]

        @pl.when(i == 0)
        def _():
            dg_ref[...] = jnp.zeros_like(dg_ref)

        @pl.when(j < n_rot)
        def _():
            xv = x_ref[...]
            ms = jnp.dot(xv * xv, bd_ref[...], precision=lax.Precision.HIGHEST, preferred_element_type=F32)
            r = lax.rsqrt(ms + EPS)
            z = xv * r
            dy = (dv * c_ref[...] - _swap_halves(dv) * sn_ref[...]) * s_ref[...]
            dg_ref[...] += jnp.sum(dy * z, axis=0, keepdims=True)
            dz = dy * g_ref[...]
            mz = jnp.dot(dz * z, bd_ref[...], precision=lax.Precision.HIGHEST, preferred_element_type=F32)
            dx_ref[...] = (r * (dz - z * mz)).astype(BF16)

        @pl.when(j >= n_rot)
        def _():
            dx_ref[...] = dv.astype(BF16)

    blk = pl.BlockSpec((tm, LANES), lambda j, i: (i, j))
    vec = pl.BlockSpec((1, LANES), lambda j, i: (0, j))
    tab = pl.BlockSpec((tm, LANES), lambda j, i: (i, 0))
    return _pc(body, name=name, grid=(W // LANES, S // tm),
               in_specs=[blk, blk, vec, vec, tab, tab, pl.BlockSpec((LANES, LANES), lambda j, i: (0, 0))],
               out_specs=(blk, vec),
               out_shape=(jax.ShapeDtypeStruct((S, W), BF16), jax.ShapeDtypeStruct((1, W), F32)),
               compiler_params=_sem("parallel", "arbitrary"))(d_out, qkv, gain, scale, cos, sin, bd)


def _softmax_rows(s):
    m = jnp.max(s, axis=-1, keepdims=True)
    p = jnp.exp(s - m)
    return p, jnp.sum(p, axis=-1, keepdims=True)


def _attn_fwd(name, q, k, v):
    H, S, dh = q.shape
    G = H // N_KV_HEADS
    tq = _tile(S, 128, 16)

    kc = _tile(S, KEY_CHUNK, LANES)
    R = G * tq

    def body(q_ref, k_ref, v_ref, o_ref, lse_ref):
        q = q_ref[...].reshape(R, dh)
        m = jnp.full((R, 1), -1e30, F32)
        l = jnp.zeros((R, 1), F32)
        acc = jnp.zeros((R, dh), F32)
        for c in range(S // kc):
            rows = slice(c * kc, (c + 1) * kc)
            s = lax.dot_general(q, k_ref[rows, :], _NT, preferred_element_type=F32)
            m_new = jnp.maximum(m, jnp.max(s, axis=-1, keepdims=True))
            alpha = jnp.exp(m - m_new)
            p = jnp.exp(s - m_new)
            l = alpha * l + jnp.sum(p, axis=-1, keepdims=True)
            acc = alpha * acc + jnp.dot(p.astype(BF16), v_ref[rows, :], preferred_element_type=F32)
            m = m_new
        o_ref[...] = (acc / l).astype(BF16).reshape(G, tq, dh)
        lse_ref[...] = (m + jnp.log(l)).reshape(G, tq, 1)

    qs = pl.BlockSpec((G, tq, dh), lambda kv, i: (kv, i, 0))
    ls = pl.BlockSpec((G, tq, 1), lambda kv, i: (kv, i, 0))
    ks = pl.BlockSpec((None, S, dh), lambda kv, i: (kv, 0, 0))
    return _pc(body, name=name, grid=(N_KV_HEADS, S // tq), in_specs=[qs, ks, ks], out_specs=(qs, ls),
               out_shape=(jax.ShapeDtypeStruct((H, S, dh), BF16), jax.ShapeDtypeStruct((H, S, 1), F32)),
               compiler_params=_sem("parallel", "parallel"))(q, k, v)


def _attn_bwd(name, q, k, v, o, lse, do):
    H, S, dh = q.shape
    G = H // N_KV_HEADS
    tq = _tile(S, 128, 16)
    kc = _tile(S, KEY_CHUNK, LANES)
    R = G * tq

    def body(q_ref, k_ref, v_ref, o_ref, lse_ref, do_ref, dq_ref, dk_ref, dv_ref):
        @pl.when(pl.program_id(1) == 0)
        def _():
            dk_ref[...] = jnp.zeros_like(dk_ref)
            dv_ref[...] = jnp.zeros_like(dv_ref)

        qq, dd = q_ref[...].reshape(R, dh), do_ref[...].reshape(R, dh)
        delta = jnp.sum(dd.astype(F32) * o_ref[...].reshape(R, dh).astype(F32), axis=-1, keepdims=True)
        lse = lse_ref[...].reshape(R, 1)
        dq = jnp.zeros((R, dh), F32)
        for c in range(S // kc):
            rows = slice(c * kc, (c + 1) * kc)
            kk, vv = k_ref[rows, :], v_ref[rows, :]
            p = jnp.exp(lax.dot_general(qq, kk, _NT, preferred_element_type=F32) - lse)
            dv_ref[rows, :] += lax.dot_general(p.astype(BF16), dd, _TN, preferred_element_type=F32)
            dp = lax.dot_general(dd, vv, _NT, preferred_element_type=F32)
            ds = (p * (dp - delta)).astype(BF16)
            dq = dq + jnp.dot(ds, kk, preferred_element_type=F32)
            dk_ref[rows, :] += lax.dot_general(ds, qq, _TN, preferred_element_type=F32)
        dq_ref[...] = dq.reshape(G, tq, dh)

    qs = pl.BlockSpec((G, tq, dh), lambda kv, i: (kv, i, 0))
    ls = pl.BlockSpec((G, tq, 1), lambda kv, i: (kv, i, 0))
    ks = pl.BlockSpec((None, S, dh), lambda kv, i: (kv, 0, 0))
    return _pc(body, name=name, grid=(N_KV_HEADS, S // tq), in_specs=[qs, ks, ks, qs, ls, qs],
               out_specs=(qs, ks, ks),
               out_shape=(jax.ShapeDtypeStruct((H, S, dh), F32), jax.ShapeDtypeStruct((N_KV_HEADS, S, dh), F32),
                          jax.ShapeDtypeStruct((N_KV_HEADS, S, dh), F32)),
               compiler_params=_sem("parallel", "arbitrary"))(q, k, v, o, lse, do)


def _xattn_fwd(name, q, kv):
    S, D = q.shape
    _, M, dh = kv.shape
    scale = dh ** -0.5
    tq = _tile(S, 256, 16)

    def body(q_ref, kv_ref, o_ref):
        for h in range(X_HEADS):
            lo, hi = h * dh, (h + 1) * dh
            s = lax.dot_general(q_ref[:, lo:hi], kv_ref[h], _NT, preferred_element_type=F32) * scale
            p, l = _softmax_rows(s)
            o = jnp.dot(p.astype(BF16), kv_ref[X_HEADS + h], preferred_element_type=F32)
            o_ref[:, lo:hi] = (o / l).astype(BF16)

    row = pl.BlockSpec((tq, D), lambda i: (i, 0))
    return _pc(body, name=name, grid=(S // tq,),
               in_specs=[row, pl.BlockSpec((2 * X_HEADS, M, dh), lambda i: (0, 0, 0))],
               out_specs=row, out_shape=jax.ShapeDtypeStruct((S, D), BF16),
               compiler_params=_sem("parallel"))(q, kv)


def _xattn_bwd(name, q, kv, do):
    S, D = q.shape
    _, M, dh = kv.shape
    scale = dh ** -0.5
    tq = _tile(S, 256, 16)

    def body(q_ref, kv_ref, do_ref, dq_ref, dkv_ref):
        @pl.when(pl.program_id(0) == 0)
        def _():
            dkv_ref[...] = jnp.zeros_like(dkv_ref)

        for h in range(X_HEADS):
            lo, hi = h * dh, (h + 1) * dh
            qh, kh, vh, doh = q_ref[:, lo:hi], kv_ref[h], kv_ref[X_HEADS + h], do_ref[:, lo:hi]
            s = lax.dot_general(qh, kh, _NT, preferred_element_type=F32) * scale
            p, l = _softmax_rows(s)
            pn = p / l
            dkv_ref[X_HEADS + h] += lax.dot_general(pn.astype(BF16), doh, _TN, preferred_element_type=F32)
            dp = lax.dot_general(doh, vh, _NT, preferred_element_type=F32)
            ds = (pn * (dp - jnp.sum(pn * dp, axis=-1, keepdims=True)) * scale).astype(BF16)
            dq_ref[:, lo:hi] = jnp.dot(ds, kh, preferred_element_type=F32).astype(BF16)
            dkv_ref[h] += lax.dot_general(ds, qh, _TN, preferred_element_type=F32)

    row = pl.BlockSpec((tq, D), lambda i: (i, 0))
    full = pl.BlockSpec((2 * X_HEADS, M, dh), lambda i: (0, 0, 0))
    return _pc(body, name=name, grid=(S // tq,), in_specs=[row, full, row], out_specs=(row, full),
               out_shape=(jax.ShapeDtypeStruct((S, D), BF16), jax.ShapeDtypeStruct((2 * X_HEADS, M, dh), F32)),
               compiler_params=_sem("arbitrary"))(q, kv, do)


def _sigmoid(x):
    return 1.0 / (1.0 + jnp.exp(-x))


def _halo_specs(tm, n, S):
    nb = tm // 8
    last8 = S // 8 - 1
    main = pl.BlockSpec((2, None, tm, n), lambda j, i: (0, j, i, 0))
    prev = pl.BlockSpec((2, None, 8, n), lambda j, i: (0, j, jnp.maximum(i * nb - 1, 0), 0))
    nxt = pl.BlockSpec((2, None, 8, n), lambda j, i: (0, j, jnp.minimum((i + 1) * nb, last8), 0))
    return main, prev, nxt


def _ffn_act(name, u, cw, cb):
    _, J, S, n = u.shape
    tm = _tile(S, 256, 16)
    nblk = S // tm

    def body(u_ref, up_ref, un_ref, w_ref, b_ref, a_ref):
        i = pl.program_id(1)
        row = lax.broadcasted_iota(jnp.int32, (tm, n), 0)
        c = []
        for half in range(2):
            main = u_ref[half]
            before = jnp.where(i > 0, up_ref[half, 7:8, :], 0.0)
            after = jnp.where(i < nblk - 1, un_ref[half, 0:1, :], 0.0)
            um = jnp.where(row == 0, before, pltpu.roll(main, 1, 0))
            up = jnp.where(row == tm - 1, after, pltpu.roll(main, tm - 1, 0))
            w = w_ref[half]
            c.append(um * w[0:1] + main * w[1:2] + up * w[2:3] + b_ref[half])
        a_ref[...] = (c[0] * _sigmoid(c[0]) * c[1]).astype(BF16)

    main, prev, nxt = _halo_specs(tm, n, S)
    return _pc(body, name=name, grid=(J, nblk),
               in_specs=[main, prev, nxt, pl.BlockSpec((2, None, 3, n), lambda j, i: (0, j, 0, 0)),
                         pl.BlockSpec((2, None, 1, n), lambda j, i: (0, j, 0, 0))],
               out_specs=pl.BlockSpec((None, tm, n), lambda j, i: (j, i, 0)),
               out_shape=jax.ShapeDtypeStruct((J, S, n), BF16),
               compiler_params=_sem("parallel", "parallel"))(u, u, u, cw, cb)


def _ffn_act_bwd(name, u, da, cw, cb):
    _, J, S, n = u.shape
    tm = _tile(S, 256, 16)
    nblk = S // tm
    te = tm + 16
    nb = tm // 8
    last8 = S // 8 - 1

    def body(u_ref, up_ref, un_ref, da_ref, dap_ref, dan_ref, w_ref, b_ref, du_ref, st_ref):
        i = pl.program_id(1)

        @pl.when(i == 0)
        def _():
            st_ref[...] = jnp.zeros_like(st_ref)

        r = lax.broadcasted_iota(jnp.int32, (te, n), 0)
        t = i * tm - 8 + r
        valid = (t >= 0) & (t < S)
        mid = (r >= 8) & (r < tm + 8)
        da_e = jnp.where(valid, jnp.concatenate([dap_ref[...], da_ref[...], dan_ref[...]], axis=0).astype(F32), 0.0)
        ue, c = [], []
        for half in range(2):
            e = jnp.where(valid, jnp.concatenate([up_ref[half], u_ref[half], un_ref[half]], axis=0), 0.0)
            w = w_ref[half]
            ue.append((pltpu.roll(e, 1, 0), e, pltpu.roll(e, te - 1, 0)))
            c.append(ue[half][0] * w[0:1] + e * w[1:2] + ue[half][2] * w[2:3] + b_ref[half])
        sg = _sigmoid(c[0])
        dc = [jnp.where(valid, da_e * c[1] * (sg * (1.0 + c[0] * (1.0 - sg))), 0.0),
              jnp.where(valid, da_e * (c[0] * sg), 0.0)]
        r8 = lax.broadcasted_iota(jnp.int32, (8, n), 0)
        for half in range(2):
            w, d, (e_before, e, e_after) = w_ref[half], dc[half], ue[half]
            du = pltpu.roll(d, te - 1, 0) * w[0:1] + d * w[1:2] + pltpu.roll(d, 1, 0) * w[2:3]
            du_ref[half] = du[8:tm + 8].astype(BF16)
            dm = jnp.where(mid, d, 0.0)
            s0 = jnp.sum(dm * e_before, axis=0, keepdims=True)
            s1 = jnp.sum(dm * e, axis=0, keepdims=True)
            s2 = jnp.sum(dm * e_after, axis=0, keepdims=True)
            s3 = jnp.sum(dm, axis=0, keepdims=True)
            st_ref[half] += jnp.where(r8 == 0, s0, jnp.where(r8 == 1, s1, jnp.where(r8 == 2, s2,
                                      jnp.where(r8 == 3, s3, 0.0))))

    main, prev, nxt = _halo_specs(tm, n, S)
    dmain = pl.BlockSpec((None, tm, n), lambda j, i: (j, i, 0))
    dprev = pl.BlockSpec((None, 8, n), lambda j, i: (j, jnp.maximum(i * nb - 1, 0), 0))
    dnxt = pl.BlockSpec((None, 8, n), lambda j, i: (j, jnp.minimum((i + 1) * nb, last8), 0))
    return _pc(body, name=name, grid=(J, nblk),
               in_specs=[main, prev, nxt, dmain, dprev, dnxt,
                         pl.BlockSpec((2, None, 3, n), lambda j, i: (0, j, 0, 0)),
                         pl.BlockSpec((2, None, 1, n), lambda j, i: (0, j, 0, 0))],
               out_specs=(main, pl.BlockSpec((2, None, 8, n), lambda j, i: (0, j, 0, 0))),
               out_shape=(jax.ShapeDtypeStruct((2, J, S, n), BF16), jax.ShapeDtypeStruct((2, J, 8, n), F32)),
               compiler_params=_sem("parallel", "arbitrary"))(u, u, u, da, da, da, cw, cb)


def _window_count(t, w, S):
    lo = jnp.maximum(t - w // 2, 0)
    hi = jnp.minimum(t + w - w // 2, S)
    return (hi - lo).astype(F32)


def _trailing_sums(x, w):
    k = 1
    while k < w:
        x = x + pltpu.roll(x, k, 0)
        k *= 2
    return x


def _pool_window(name, h, group_w, adjoint, out_dtype):
    S, D = h.shape
    SP = S + 2 * POOL_PAD
    per_group = group_w // LANES

    def body(h_ref, o_ref, xp):
        g = pl.program_id(0) // per_group
        t = lax.broadcasted_iota(jnp.int32, (S, LANES), 0)
        xp[0:POOL_PAD, :] = jnp.zeros((POOL_PAD, LANES), F32)
        xp[S + POOL_PAD:SP, :] = jnp.zeros((POOL_PAD, LANES), F32)
        for gi, w in enumerate(POOL_WINDOWS):
            @pl.when(g == gi)
            def _():
                hv = h_ref[...]
                cnt = _window_count(t, w, S)
                xp[POOL_PAD:S + POOL_PAD, :] = hv / cnt if adjoint else hv
                ahead = w // 2 if adjoint else w // 2 - 1
                sw = _trailing_sums(xp[...], w)
                if ahead:
                    sw = pltpu.roll(sw, SP - ahead, 0)
                win = sw[POOL_PAD:S + POOL_PAD]
                o_ref[...] = ((win if adjoint else win / cnt) - hv).astype(out_dtype)

    col = pl.BlockSpec((S, LANES), lambda j: (0, j))
    return _pc(body, name=name, grid=(D // LANES,), in_specs=[col], out_specs=col,
               out_shape=jax.ShapeDtypeStruct((S, D), out_dtype),
               scratch_shapes=[pltpu.VMEM((SP, LANES), F32)], compiler_params=_sem("parallel"))(h)


def _pool_proj(name, mixed, w, scale, res):
    S, D = mixed.shape
    G, gw, _ = w.shape
    tm = _tile(S, 512, 16)

    def body(m_ref, w_ref, s_ref, r_ref, o_ref):
        for g in range(G):
            lo, hi = g * gw, (g + 1) * gw
            y = jnp.dot(m_ref[:, lo:hi], w_ref[g], preferred_element_type=F32)
            o_ref[:, lo:hi] = r_ref[:, lo:hi] + y * s_ref[:, lo:hi]

    row = pl.BlockSpec((tm, D), lambda i: (i, 0))
    return _pc(body, name=name, grid=(S // tm,),
               in_specs=[row, pl.BlockSpec((G, gw, gw), lambda i: (0, 0, 0)), pl.BlockSpec((1, D), lambda i: (0, 0)), row],
               out_specs=row, out_shape=jax.ShapeDtypeStruct((S, D), F32),
               compiler_params=_sem("parallel"))(mixed, w, scale, res)


def _pool_proj_bwd(name, dy, mixed, w, scale):
    S, D = mixed.shape
    G, gw, _ = w.shape
    tm = _tile(S, 512, 16)

    def body(dy_ref, m_ref, w_ref, s_ref, dm_ref, dw_ref, ds_ref):
        @pl.when(pl.program_id(0) == 0)
        def _():
            dw_ref[...] = jnp.zeros_like(dw_ref)
            ds_ref[...] = jnp.zeros_like(ds_ref)

        for g in range(G):
            lo, hi = g * gw, (g + 1) * gw
            mg, dyg = m_ref[:, lo:hi], dy_ref[:, lo:hi]
            y = jnp.dot(mg, w_ref[g], preferred_element_type=F32)
            ds_ref[:, lo:hi] += jnp.sum(dyg * y, axis=0, keepdims=True)
            dyp = (dyg * s_ref[:, lo:hi]).astype(BF16)
            dm_ref[:, lo:hi] = lax.dot_general(dyp, w_ref[g], _NT, preferred_element_type=F32)
            dw_ref[g] += lax.dot_general(mg, dyp, _TN, preferred_element_type=F32)

    row = pl.BlockSpec((tm, D), lambda i: (i, 0))
    wsp = pl.BlockSpec((G, gw, gw), lambda i: (0, 0, 0))
    vec = pl.BlockSpec((1, D), lambda i: (0, 0))
    return _pc(body, name=name, grid=(S // tm,), in_specs=[row, row, wsp, vec], out_specs=(row, wsp, vec),
               out_shape=(jax.ShapeDtypeStruct((S, D), F32), jax.ShapeDtypeStruct((G, gw, gw), F32),
                          jax.ShapeDtypeStruct((1, D), F32)),
               compiler_params=_sem("arbitrary"))(dy, mixed, w, scale)


def _adamw(name, w, g, m, v):
    shape = w.shape
    C = shape[-1]
    R = w.size // C
    tm = _tile(R, 512, 8)

    def body(w_ref, g_ref, m_ref, v_ref, d_ref, nm_ref, nv_ref):
        gv = g_ref[...]
        nm = ADAM_B1 * m_ref[...] + (1.0 - ADAM_B1) * gv
        nv = ADAM_B2 * v_ref[...] + (1.0 - ADAM_B2) * (gv * gv)
        m_hat = nm / (1.0 - ADAM_B1 ** ADAM_STEP)
        v_hat = nv / (1.0 - ADAM_B2 ** ADAM_STEP)
        d_ref[...] = -ADAM_LR * (m_hat / (jnp.sqrt(v_hat) + ADAM_EPS) + ADAM_WD * w_ref[...])
        nm_ref[...] = nm
        nv_ref[...] = nv

    blk = pl.BlockSpec((tm, C), lambda i: (i, 0))
    sd = jax.ShapeDtypeStruct((R, C), F32)
    outs = _pc(body, name=name, grid=(R // tm,), in_specs=[blk] * 4, out_specs=(blk,) * 3, out_shape=(sd,) * 3,
               compiler_params=_sem("parallel"))(*(a.reshape(R, C) for a in (w, g, m, v)))
    return tuple(o.reshape(shape) for o in outs)


def _position():
    return lax.axis_index("x"), lax.axis_index("y"), lax.axis_index("c")


def _flip(v, bit):
    return 1 - v if bit else v


def _allgather_small(name, v):
    R, W = v.shape

    def body(v_ref, out_ref, send_sems, recv_sems):
        x, y, c = _position()
        me = 4 * x + 2 * y + c
        out_ref[me] = v_ref[...]
        sends = []
        for k in range(1, N_DEV):
            peer = (_flip(x, k & 4), _flip(y, k & 2), _flip(c, k & 1))
            cp = pltpu.make_async_remote_copy(src_ref=v_ref, dst_ref=out_ref.at[me], send_sem=send_sems.at[k - 1],
                                              recv_sem=recv_sems.at[k - 1], device_id=peer, device_id_type=MESH)
            cp.start()
            sends.append(cp)
        for k in range(1, N_DEV):
            peer = (_flip(x, k & 4), _flip(y, k & 2), _flip(c, k & 1))
            slot = 4 * peer[0] + 2 * peer[1] + peer[2]
            pltpu.make_async_remote_copy(src_ref=v_ref, dst_ref=out_ref.at[slot], send_sem=send_sems.at[k - 1],
                                         recv_sem=recv_sems.at[k - 1], device_id=peer, device_id_type=MESH).wait_recv()
        for cp in sends:
            cp.wait_send()

    vm = pl.BlockSpec(memory_space=pltpu.VMEM)
    return _pc(body, name=name, in_specs=[vm], out_specs=vm, out_shape=jax.ShapeDtypeStruct((N_DEV, R, W), F32),
               scratch_shapes=[pltpu.SemaphoreType.DMA((N_DEV - 1,)), pltpu.SemaphoreType.DMA((N_DEV - 1,))])(v)


def _sum_slots(name, a):
    n, R, W = a.shape

    def body(a_ref, o_ref):
        acc = a_ref[0]
        for s in range(1, n):
            acc = acc + a_ref[s]
        o_ref[...] = acc

    return _pc(body, name=name, grid=(1,), in_specs=[pl.BlockSpec((n, R, W), lambda i: (0, 0, 0))],
               out_specs=pl.BlockSpec((R, W), lambda i: (0, 0)), out_shape=jax.ShapeDtypeStruct((R, W), F32))(a)


def _allgather_blocks(name, blocks):
    n = len(blocks)

    def body(*refs):
        b_refs, out_refs = refs[:n], refs[n:2 * n]
        send_sems, recv_sems, local_sems = refs[2 * n:]
        x, y, c = _position()
        me, sibling = (x, y, c), (x, y, 1 - c)
        chips = [(1 - x, y), (x, 1 - y), (1 - x, 1 - y)]

        def slot(i, px, py, pc):
            return out_refs[i].at[4 * px + 2 * py + pc]

        def copy(i, k, block, to, src=None):
            return pltpu.make_async_remote_copy(src_ref=slot(i, *block) if src is None else src, dst_ref=slot(i, *block),
                                                send_sem=send_sems.at[k, i], recv_sem=recv_sems.at[k, i],
                                                device_id=to, device_id_type=MESH)

        mine = [pltpu.make_async_copy(b_refs[i], slot(i, *me), local_sems.at[i]) for i in range(n)]
        first = [copy(i, 1 + j, me, (*chip, c), src=b_refs[i]) for i in range(n) for j, chip in enumerate(chips)]
        first += [copy(i, 0, me, sibling, src=b_refs[i]) for i in range(n)]
        for cp in mine + first:
            cp.start()
        passed = []
        for j, chip in enumerate(chips):
            for i in range(n):
                copy(i, 1 + j, (*chip, c), me).wait_recv()
                passed.append(copy(i, 4 + j, (*chip, c), sibling))
                passed[-1].start()
        for i in range(n):
            copy(i, 0, sibling, me).wait_recv()
        for j, chip in enumerate(chips):
            for i in range(n):
                copy(i, 4 + j, (*chip, 1 - c), me).wait_recv()
        for cp in first + passed:
            cp.wait_send()
        for cp in mine:
            cp.wait()

    hbm = pl.BlockSpec(memory_space=pl.ANY)
    return _pc(body, name=name, in_specs=[hbm] * n, out_specs=[hbm] * n,
               out_shape=[jax.ShapeDtypeStruct((N_DEV,) + b.shape, b.dtype) for b in blocks],
               scratch_shapes=[pltpu.SemaphoreType.DMA((7, n)), pltpu.SemaphoreType.DMA((7, n)),
                               pltpu.SemaphoreType.DMA((n,))])(*blocks)


def _swap_with_sibling(name, gs):
    n = len(gs)
    n_chips = gs[0].shape[0]

    def body(*refs):
        g_refs, r_refs = refs[:n], refs[n:2 * n]
        send_sems, recv_sems = refs[2 * n:]
        x, y, c = _position()
        cps = [pltpu.make_async_remote_copy(src_ref=g_refs[i].at[k, 1 - c], dst_ref=r_refs[i].at[k],
                                            send_sem=send_sems.at[k, i], recv_sem=recv_sems.at[k, i],
                                            device_id=(x, y, 1 - c), device_id_type=MESH)
               for i in range(n) for k in range(n_chips)]
        for cp in cps:
            cp.start()
        for cp in cps:
            cp.wait()

    hbm = pl.BlockSpec(memory_space=pl.ANY)
    return _pc(body, name=name, in_specs=[hbm] * n, out_specs=[hbm] * n,
               out_shape=[jax.ShapeDtypeStruct((n_chips,) + g.shape[2:], F32) for g in gs],
               scratch_shapes=[pltpu.SemaphoreType.DMA((n_chips, n)), pltpu.SemaphoreType.DMA((n_chips, n))])(*gs)


def _add_sibling(name, g4, r1, pos):
    n, _, L, W = g4.shape
    tl = _tile(L, 512, 16)

    def body(pos_ref, g_ref, r_ref, tb_ref, own_ref):
        t = g_ref[...] + r_ref[...]
        tb_ref[...] = t.astype(BF16)

        @pl.when(pl.program_id(1) == pos_ref[1])
        def _():
            own_ref[...] = t

    gs = pltpu.PrefetchScalarGridSpec(
        num_scalar_prefetch=1, grid=(L // tl, n),
        in_specs=[pl.BlockSpec((None, None, tl, W), lambda i, k, p: (k, p[0], i, 0)),
                  pl.BlockSpec((None, tl, W), lambda i, k, p: (k, i, 0))],
        out_specs=(pl.BlockSpec((None, tl, W), lambda i, k, p: (k, i, 0)),
                   pl.BlockSpec((tl, W), lambda i, k, p: (i, 0))))
    return _pc(body, name=name, grid_spec=gs,
               out_shape=(jax.ShapeDtypeStruct((n, L, W), BF16), jax.ShapeDtypeStruct((L, W), F32)),
               compiler_params=_sem("parallel", "arbitrary"))(pos, g4, r1)


def _swap_with_chips(name, tbs):
    n = len(tbs)

    def body(*refs):
        t_refs, r_refs = refs[:n], refs[n:2 * n]
        send_sems, recv_sems = refs[2 * n:]
        x, y, c = _position()
        cps = []
        for i in range(n):
            for j in range(1, 4):
                px, py = _flip(x, j & 2), _flip(y, j & 1)
                cps.append(pltpu.make_async_remote_copy(src_ref=t_refs[i].at[2 * px + py], dst_ref=r_refs[i].at[j - 1],
                                                        send_sem=send_sems.at[j - 1, i], recv_sem=recv_sems.at[j - 1, i],
                                                        device_id=(px, py, c), device_id_type=MESH))
        for cp in cps:
            cp.start()
        for cp in cps:
            cp.wait()

    hbm = pl.BlockSpec(memory_space=pl.ANY)
    return _pc(body, name=name, in_specs=[hbm] * n, out_specs=[hbm] * n,
               out_shape=[jax.ShapeDtypeStruct((3,) + t.shape[1:], BF16) for t in tbs],
               scratch_shapes=[pltpu.SemaphoreType.DMA((3, n)), pltpu.SemaphoreType.DMA((3, n))])(*tbs)


def _add_chips(name, own, r2):
    L, W = own.shape
    tl = _tile(L, 512, 16)

    def body(o_ref, r_ref, out_ref):
        acc = o_ref[...]
        for j in range(3):
            acc = acc + r_ref[j].astype(F32)
        out_ref[...] = acc

    return _pc(body, name=name, grid=(L // tl,),
               in_specs=[pl.BlockSpec((tl, W), lambda i: (i, 0)), pl.BlockSpec((3, tl, W), lambda i: (0, i, 0))],
               out_specs=pl.BlockSpec((tl, W), lambda i: (i, 0)), out_shape=jax.ShapeDtypeStruct((L, W), F32),
               compiler_params=_sem("parallel"))(own, r2)


def _heads_major(a, S):
    return a.reshape(S, -1, HEAD_DIM).transpose(1, 0, 2)


def _heads_minor(a):
    return a.transpose(1, 0, 2).reshape(a.shape[1], -1)


def kernel(x, mem, attn_norm, attn_w_qkv, attn_q_gain, attn_k_gain, attn_w_o, pool_norm, pool_w, pool_scale, xattn_norm, mem_norm, xattn_w_q, xattn_w_kv, xattn_w_o, ffn_norm, ffn_w_up, ffn_conv_w, ffn_conv_b, ffn_w_down, final_norm, loss_target, m_attn_norm, m_attn_w_qkv, m_attn_q_gain, m_attn_k_gain, m_attn_w_o, m_pool_norm, m_pool_w, m_pool_scale, m_xattn_norm, m_mem_norm, m_xattn_w_q, m_xattn_w_kv, m_xattn_w_o, m_ffn_norm, m_ffn_w_up, m_ffn_conv_w, m_ffn_conv_b, m_ffn_w_down, m_final_norm, v_attn_norm, v_attn_w_qkv, v_attn_q_gain, v_attn_k_gain, v_attn_w_o, v_pool_norm, v_pool_w, v_pool_scale, v_xattn_norm, v_mem_norm, v_xattn_w_q, v_xattn_w_kv, v_xattn_w_o, v_ffn_norm, v_ffn_w_up, v_ffn_conv_w, v_ffn_conv_b, v_ffn_w_down, v_final_norm):
    names = ['attn_norm', 'attn_w_qkv', 'attn_q_gain', 'attn_k_gain', 'attn_w_o', 'pool_norm', 'pool_w', 'pool_scale',
             'xattn_norm', 'mem_norm', 'xattn_w_q', 'xattn_w_kv', 'xattn_w_o', 'ffn_norm', 'ffn_w_up', 'ffn_conv_w',
             'ffn_conv_b', 'ffn_w_down', 'final_norm']
    W = dict(zip(names, (attn_norm, attn_w_qkv, attn_q_gain, attn_k_gain, attn_w_o, pool_norm, pool_w, pool_scale,
                         xattn_norm, mem_norm, xattn_w_q, xattn_w_kv, xattn_w_o, ffn_norm, ffn_w_up, ffn_conv_w,
                         ffn_conv_b, ffn_w_down, final_norm)))
    Mo = dict(zip(names, (m_attn_norm, m_attn_w_qkv, m_attn_q_gain, m_attn_k_gain, m_attn_w_o, m_pool_norm, m_pool_w,
                          m_pool_scale, m_xattn_norm, m_mem_norm, m_xattn_w_q, m_xattn_w_kv, m_xattn_w_o, m_ffn_norm,
                          m_ffn_w_up, m_ffn_conv_w, m_ffn_conv_b, m_ffn_w_down, m_final_norm)))
    Vo = dict(zip(names, (v_attn_norm, v_attn_w_qkv, v_attn_q_gain, v_attn_k_gain, v_attn_w_o, v_pool_norm, v_pool_w,
                          v_pool_scale, v_xattn_norm, v_mem_norm, v_xattn_w_q, v_xattn_w_kv, v_xattn_w_o, v_ffn_norm,
                          v_ffn_w_up, v_ffn_conv_w, v_ffn_conv_b, v_ffn_w_down, v_final_norm)))

    S, D = x.shape[1], x.shape[2]
    n_layers = xattn_norm.shape[0]
    n_up = ffn_w_up.shape[2]
    qkv_w = attn_w_qkv.shape[2] * N_DEV
    n_heads = qkv_w // HEAD_DIM - 2 * N_KV_HEADS
    n_rot = (n_heads + N_KV_HEADS) * HEAD_DIM // LANES
    group_w = pool_w.shape[3]
    xs, mems, tgt = x[0], mem[0], loss_target[0]
    xi, yi, ci = _position()
    dev = 4 * xi + 2 * yi + ci
    pos = jnp.stack([ci, 2 * xi + yi]).astype(jnp.int32)

    layers = range(n_layers)
    blocks = [attn_w_qkv[0], attn_w_o[0], pool_w.reshape(-1, group_w)]
    blocks += [xattn_w_q[l] for l in layers] + [xattn_w_kv.reshape(n_layers * D, -1)]
    blocks += [xattn_w_o[l] for l in layers] + [ffn_w_up.reshape(n_layers * D, n_up)] + [ffn_w_down[l] for l in layers]
    gathered = iter(_allgather_blocks("allgather_weights", [b.astype(BF16) for b in blocks]))
    w_qkv = next(gathered).transpose(1, 0, 2).reshape(D, qkv_w)
    w_o = next(gathered).reshape(-1, D)
    n_groups = pool_w.shape[1]
    w_pool = (next(gathered).reshape(N_DEV, n_groups, -1, group_w).transpose(1, 0, 2, 3)
              .reshape(n_groups, group_w, group_w))
    w_xq = [next(gathered).reshape(D, D) for l in layers]
    w_xkv = next(gathered)
    w_xo = [next(gathered).reshape(D, D) for l in layers]
    w_up = next(gathered)
    w_down = [next(gathered).reshape(-1, D) for l in layers]

    small_vec = jnp.concatenate([pool_norm.reshape(-1), pool_scale.reshape(-1), ffn_conv_w.reshape(-1)])
    small_rows = _round_up(-(-small_vec.size // PACK_W), 8)
    small = _allgather_small("allgather_small", jnp.pad(small_vec, (0, small_rows * PACK_W - small_vec.size))
                             .reshape(small_rows, PACK_W)).reshape(N_DEV, -1)
    d_sh = pool_norm.shape[1]
    pool_norm_f = small[:, :d_sh].reshape(1, D)
    pool_scale_f = small[:, d_sh:2 * d_sh].reshape(1, D)
    conv_w_f = small[:, 2 * d_sh:2 * d_sh + ffn_conv_w.size].reshape(N_DEV, n_layers, 3, n_up)
    conv_b_f = ffn_conv_b.reshape(n_layers, N_DEV, 1, n_up)

    cos, sin = _rope_tables(S)
    bd = _head_mean_matrix()
    pad_w = qkv_w - (n_heads + N_KV_HEADS) * HEAD_DIM
    qk_gain = jnp.concatenate([jnp.tile(attn_q_gain[0], n_heads), jnp.tile(attn_k_gain[0], N_KV_HEADS),
                               jnp.ones((pad_w,), F32)]).reshape(1, qkv_w)
    qk_scale = jnp.concatenate([jnp.full((n_heads * HEAD_DIM,), HEAD_DIM ** -0.5, F32),
                                jnp.ones((qkv_w - n_heads * HEAD_DIM,), F32)]).reshape(1, qkv_w)

    saved = []

    def xattn_ffn_fwd(l, xin):
        hx = _rmsnorm(f"xattn_norm{l}", xin, xattn_norm[l:l + 1], BF16)
        memn = _rmsnorm(f"mem_norm{l}", mems, mem_norm[l:l + 1], BF16)
        qx = _mm_nn(f"xattn_q{l}", hx, w_xq[l], BF16)
        kv = _mm_nn_bs(f"xattn_kv{l}", memn, w_xkv, BF16, l)
        ox = _xattn_fwd(f"xattn_fwd{l}", qx, kv)
        x2 = _mm_nn(f"xattn_o{l}", ox, w_xo[l], F32, res=xin)
        hf = _rmsnorm(f"ffn_norm{l}", x2, ffn_norm[l:l + 1], BF16)
        u = _mm_nn_bs(f"ffn_up{l}", hf, w_up, F32, l).reshape(2, N_DEV // 2, S, n_up)
        cw = conv_w_f[:, l].reshape(2, N_DEV // 2, 3, n_up)
        cb = conv_b_f[l].reshape(2, N_DEV // 2, 1, n_up)
        act = _ffn_act(f"ffn_act{l}", u, cw, cb)
        x3 = _mm_nn_as(f"ffn_down{l}", act, w_down[l], F32, x2)
        saved.append(dict(xin=xin, hx=hx, memn=memn, qx=qx, kv=kv, ox=ox, x2=x2, hf=hf, u=u, cw=cw, cb=cb, act=act))
        return x3

    h0 = _rmsnorm("attn_norm", xs, attn_norm, BF16)
    qkv = _mm_nn("attn_qkv", h0, w_qkv, F32)
    qkr = _heads_major(_qk_rope("qk_rope", qkv, qk_gain, qk_scale, cos, sin, bd, n_rot), S)
    q_hm, k_hm, v_hm = qkr[:n_heads], qkr[n_heads:n_heads + N_KV_HEADS], qkr[n_heads + N_KV_HEADS:]
    o_hm, lse = _attn_fwd("attn_fwd", q_hm, k_hm, v_hm)
    o_att = _heads_minor(o_hm)
    x1 = _mm_nn("attn_o", o_att, w_o, F32, res=xs)
    x3 = xattn_ffn_fwd(0, x1)
    hp = _rmsnorm("pool_norm", x3, pool_norm_f, F32)
    mixed = _pool_window("pool_window", hp, group_w, False, BF16)
    x4 = _pool_proj("pool_proj", mixed, w_pool, pool_scale_f, x3)
    x6 = xattn_ffn_fwd(1, x4)

    G = {}
    g, d_final, lvec = _loss_head("loss_head", x6, final_norm.reshape(1, D), tgt)
    G['final_norm'] = d_final.reshape(D)
    loss = lax.psum(0.5 * jnp.sum(lvec) / D, ("x", "y", "c"))

    d_xn, d_mn, d_fn, d_xq, d_xkv, d_xo, d_up, d_cw, d_cb, d_down = ([None] * n_layers for _ in range(10))

    def xattn_ffn_bwd(l, g):
        sv = saved[l]
        d_act = _mm_nt_bs(f"ffn_down_dx{l}", g, w_down[l], N_DEV // 2, F32)
        d_down[l] = _mm_tn_as(f"ffn_down_dw{l}", sv['act'], g, F32)
        du, st = _ffn_act_bwd(f"ffn_act_bwd{l}", sv['u'], d_act, sv['cw'], sv['cb'])
        du = du.reshape(N_DEV, S, n_up)
        st = st.reshape(N_DEV, 8, n_up)
        d_cw[l], d_cb[l] = st[:, 0:3], st[:, 3].reshape(-1)
        d_up[l] = _mm_tn_bs(f"ffn_up_dw{l}", sv['hf'], du, F32)
        dhf = _mm_nt_abs(f"ffn_up_dx{l}", du, w_up, D, F32, l)
        g, d_fn[l] = _rmsnorm_bwd(f"ffn_norm_bwd{l}", sv['x2'], ffn_norm[l:l + 1], dhf, g)
        d_xo[l] = _mm_tn(f"xattn_o_dw{l}", sv['ox'], g, F32)
        do = _mm_nt(f"xattn_o_dx{l}", g, w_xo[l], BF16)
        dq, dkv = _xattn_bwd(f"xattn_bwd{l}", sv['qx'], sv['kv'], do)
        d_xq[l] = _mm_tn(f"xattn_q_dw{l}", sv['hx'], dq, F32)
        d_xkv[l] = _mm_tn_bs(f"xattn_kv_dw{l}", sv['memn'], dkv, F32)
        dmemn = _mm_nt_abs(f"xattn_kv_dx{l}", dkv, w_xkv, D, F32, l)
        _, d_mn[l] = _rmsnorm_bwd(f"mem_norm_bwd{l}", mems, mem_norm[l:l + 1], dmemn)
        dhx = _mm_nt(f"xattn_q_dx{l}", dq, w_xq[l], F32)
        g, d_xn[l] = _rmsnorm_bwd(f"xattn_norm_bwd{l}", sv['xin'], xattn_norm[l:l + 1], dhx, g)
        return g

    g = xattn_ffn_bwd(1, g)
    d_mixed, d_pool_w, d_pool_scale = _pool_proj_bwd("pool_proj_bwd", g, mixed, w_pool, pool_scale_f)
    dhp = _pool_window("pool_window_bwd", d_mixed, group_w, True, F32)
    g, d_pool_norm = _rmsnorm_bwd("pool_norm_bwd", x3, pool_norm_f, dhp, g)
    g = xattn_ffn_bwd(0, g)
    d_wo = _mm_tn("attn_o_dw", o_att, g, F32)
    do_hm = _heads_major(_mm_nt("attn_o_dx", g, w_o, BF16), S)
    dq_hm, dk_hm, dv_hm = _attn_bwd("attn_bwd", q_hm, k_hm, v_hm, o_hm, lse, do_hm)
    d_qkr = _heads_minor(jnp.concatenate([dq_hm, dk_hm, dv_hm], axis=0))
    d_qkv, d_gain = _qk_rope_bwd("qk_rope_bwd", d_qkr, qkv, qk_gain, qk_scale, cos, sin, bd, n_rot)
    d_wqkv = _mm_tn("attn_qkv_dw", h0, d_qkv, F32)
    dh0 = _mm_nt("attn_qkv_dx", d_qkv, w_qkv, F32)
    grad_x, d_attn_norm = _rmsnorm_bwd("attn_norm_bwd", xs, attn_norm, dh0, g)

    bufs = [d_wqkv.reshape(D, N_DEV, -1).transpose(1, 0, 2), d_wo.reshape(N_DEV, -1, D),
            d_pool_w.reshape(n_groups, N_DEV, -1, group_w).transpose(1, 0, 2, 3).reshape(N_DEV, -1, group_w)]
    bufs += [d.reshape(N_DEV, -1, D) for d in d_xq] + d_xkv + [d.reshape(N_DEV, -1, D) for d in d_xo]
    bufs += d_up + [d.reshape(N_DEV, -1, D) for d in d_down]
    g4s = [b.reshape((N_DEV // 2, 2) + b.shape[1:]) for b in bufs]
    r1s = _swap_with_sibling("reduce_swap_sibling", g4s)
    sums = [_add_sibling(f"reduce_add_sibling{i}", g4, r1, pos) for i, (g4, r1) in enumerate(zip(g4s, r1s))]
    r2s = _swap_with_chips("reduce_swap_chips", [tb for tb, _ in sums])
    red = iter([_add_chips(f"reduce_add_chips{i}", own, r2) for i, ((_, own), r2) in enumerate(zip(sums, r2s))])
    G['attn_w_qkv'], G['attn_w_o'] = next(red)[None], next(red)[None]
    G['pool_w'] = next(red).reshape(pool_w.shape)
    G['xattn_w_q'] = jnp.stack([next(red) for l in layers])
    G['xattn_w_kv'] = jnp.stack([next(red) for l in layers])
    G['xattn_w_o'] = jnp.stack([next(red) for l in layers])
    G['ffn_w_up'] = jnp.stack([next(red) for l in layers])
    G['ffn_w_down'] = jnp.stack([next(red) for l in layers])

    hq = n_heads * HEAD_DIM
    small_g = {'attn_norm': d_attn_norm, 'attn_q_gain': d_gain[0, :hq].reshape(n_heads, HEAD_DIM).sum(0),
               'attn_k_gain': d_gain[0, hq:hq + N_KV_HEADS * HEAD_DIM].reshape(N_KV_HEADS, HEAD_DIM).sum(0),
               'pool_norm': d_pool_norm, 'pool_scale': d_pool_scale,
               'xattn_norm': jnp.concatenate(d_xn), 'mem_norm': jnp.concatenate(d_mn), 'ffn_norm': jnp.concatenate(d_fn),
               'ffn_conv_w': jnp.stack(d_cw, axis=1), 'ffn_conv_b': jnp.stack(d_cb)}
    order = list(small_g)
    flat = jnp.concatenate([small_g[n].reshape(-1) for n in order] + [G['final_norm']])
    ar_rows = _round_up(-(-flat.size // PACK_W), 8)
    flat = jnp.pad(flat, (0, ar_rows * PACK_W - flat.size)).reshape(ar_rows, PACK_W)
    summed = _sum_slots("allreduce_sum", _allgather_small("allreduce_gather", flat)).reshape(-1)
    at = 0
    for n in order + ['final_norm']:
        size = G['final_norm'].size if n == 'final_norm' else small_g[n].size
        piece = summed[at:at + size]
        at += size
        if n in ('pool_norm', 'pool_scale'):
            piece = lax.dynamic_slice(piece, (dev * d_sh,), (d_sh,))
        elif n == 'ffn_conv_w':
            piece = lax.dynamic_index_in_dim(piece.reshape(N_DEV, n_layers, 3, n_up), dev, 0, keepdims=False)
        G[n] = piece.reshape(W[n].shape)

    deltas, new_m, new_v = [], [], []
    for n in names:
        d, nm, nv = _adamw(f"adamw_{n}", W[n], G[n], Mo[n], Vo[n])
        deltas.append(d)
        new_m.append(nm)
        new_v.append(nv)
    return (loss, grad_x[None], *[G[n] for n in names], *deltas, *new_m, *new_v)
```

```python
import jax
import jax.numpy as jnp
from jax import lax
from jax.experimental import pallas as pl
from jax.experimental.pallas import tpu as pltpu

F32 = jnp.float32
BF16 = jnp.bfloat16
MESH = pl.DeviceIdType.MESH

N_DEV = 8
EPS = 1e-6
HEAD_DIM = 64
N_KV_HEADS = 4
X_HEADS = 4
GRID_W = 64
ROPE_THETA = 10000.0
ROPE_PAIRS = HEAD_DIM // 4
POOL_WINDOWS = (2, 4, 8, 16)
POOL_PAD = 16
KEY_CHUNK = 1024
MM_ROWS = 1024
LANES = 128
PACK_W = 1024
ADAM_LR, ADAM_B1, ADAM_B2, ADAM_EPS, ADAM_WD, ADAM_STEP = 0.001, 0.9, 0.999, 1e-08, 0.01, 10

_NN = (((1,), (0,)), ((), ()))
_NT = (((1,), (1,)), ((), ()))
_TN = (((0,), (0,)), ((), ()))


def _pc(body, *, name, **kw):
    return pl.pallas_call(body, name=name, **kw)


def _sem(*kinds):
    return pltpu.CompilerParams(dimension_semantics=kinds)


def _tile(n, pref, mult):
    best = None
    for t in range(mult, min(n, pref) + 1, mult):
        if n % t == 0:
            best = t
    return n if best is None else best


def _round_up(n, m):
    return (n + m - 1) // m * m


def _mm_call(name, a, b, dims, grid, a_spec, b_spec, o_spec, out_shape, kaxis, res=None, res_spec=None):
    nk = grid[kaxis]
    acc_shape = tuple(d for d in o_spec.block_shape if d is not None)
    in_place = out_shape.dtype == F32 and res is None
    use_scratch = nk > 1 and not in_place

    def body(*refs):
        refs = list(refs)
        acc = refs.pop() if use_scratch else None
        a_ref, b_ref = refs[:2]
        r_ref = refs[2] if res is not None else None
        o_ref = refs[-1]
        prod = lax.dot_general(a_ref[...].astype(BF16), b_ref[...].astype(BF16), dims, preferred_element_type=F32)
        if nk == 1:
            if r_ref is not None:
                prod = prod + r_ref[...]
            o_ref[...] = prod.astype(o_ref.dtype)
            return
        k = pl.program_id(kaxis)
        tgt = o_ref if in_place else acc

        @pl.when(k == 0)
        def _():
            tgt[...] = prod

        @pl.when(k > 0)
        def _():
            tgt[...] += prod

        if not in_place:
            @pl.when(k == nk - 1)
            def _():
                r = acc[...]
                if r_ref is not None:
                    r = r + r_ref[...]
                o_ref[...] = r.astype(o_ref.dtype)

    sem = tuple("arbitrary" if ax == kaxis else "parallel" for ax in range(len(grid)))
    ins = [a, b] if res is None else [a, b, res]
    specs = [a_spec, b_spec] if res is None else [a_spec, b_spec, res_spec]
    return _pc(body, name=name, grid=grid, in_specs=specs, out_specs=o_spec, out_shape=out_shape,
               scratch_shapes=[pltpu.VMEM(acc_shape, F32)] if use_scratch else [],
               compiler_params=_sem(*sem))(*ins)


def _mm_nn(name, a, b, out_dtype, res=None):
    M, K = a.shape
    N = b.shape[1]
    tm, tn, tk = _tile(M, MM_ROWS, 16), _tile(N, 1024, LANES), _tile(K, 1024, LANES)
    return _mm_call(name, a, b, _NN, (M // tm, N // tn, K // tk),
                    pl.BlockSpec((tm, tk), lambda i, j, k: (i, k)),
                    pl.BlockSpec((tk, tn), lambda i, j, k: (k, j)),
                    pl.BlockSpec((tm, tn), lambda i, j, k: (i, j)),
                    jax.ShapeDtypeStruct((M, N), out_dtype), 2, res,
                    pl.BlockSpec((tm, tn), lambda i, j, k: (i, j)))


def _mm_nt(name, a, b, out_dtype):
    M, K = a.shape
    N = b.shape[0]
    tm, tn, tk = _tile(M, MM_ROWS, 16), _tile(N, 1024, LANES), _tile(K, 1024, LANES)
    return _mm_call(name, a, b, _NT, (M // tm, N // tn, K // tk),
                    pl.BlockSpec((tm, tk), lambda i, j, k: (i, k)),
                    pl.BlockSpec((tn, tk), lambda i, j, k: (j, k)),
                    pl.BlockSpec((tm, tn), lambda i, j, k: (i, j)),
                    jax.ShapeDtypeStruct((M, N), out_dtype), 2)


def _mm_tn(name, a, b, out_dtype):
    R, M = a.shape
    N = b.shape[1]
    tm, tn, tr = _tile(M, 1024, LANES), _tile(N, 1024, LANES), _tile(R, MM_ROWS, 16)
    return _mm_call(name, a, b, _TN, (M // tm, N // tn, R // tr),
                    pl.BlockSpec((tr, tm), lambda i, j, k: (k, i)),
                    pl.BlockSpec((tr, tn), lambda i, j, k: (k, j)),
                    pl.BlockSpec((tm, tn), lambda i, j, k: (i, j)),
                    jax.ShapeDtypeStruct((M, N), out_dtype), 2)


def _mm_nn_bs(name, a, b, out_dtype, layer=0):
    M, K = a.shape
    J, _, n = b.shape
    tm, tk = _tile(M, MM_ROWS, 16), _tile(K, 1024, LANES)
    first = layer * (K // tk)
    return _mm_call(name, a, b, _NN, (J, M // tm, K // tk),
                    pl.BlockSpec((tm, tk), lambda j, i, k: (i, k)),
                    pl.BlockSpec((None, tk, n), lambda j, i, k: (j, first + k, 0)),
                    pl.BlockSpec((None, tm, n), lambda j, i, k: (j, i, 0)),
                    jax.ShapeDtypeStruct((J, M, n), out_dtype), 2)


def _mm_nt_bs(name, a, b, J, out_dtype):
    M, K = a.shape
    n = b.shape[0] // J
    tm, tk = _tile(M, MM_ROWS, 16), _tile(K, 1024, LANES)
    return _mm_call(name, a, b, _NT, (J, M // tm, K // tk),
                    pl.BlockSpec((tm, tk), lambda j, i, k: (i, k)),
                    pl.BlockSpec((n, tk), lambda j, i, k: (j, k)),
                    pl.BlockSpec((None, tm, n), lambda j, i, k: (j, i, 0)),
                    jax.ShapeDtypeStruct((J, M, n), out_dtype), 2)


def _mm_nn_as(name, a, b, out_dtype, res):
    J, M, n = a.shape
    N = b.shape[1]
    tm, tn = _tile(M, MM_ROWS, 16), _tile(N, 1024, LANES)
    return _mm_call(name, a, b, _NN, (M // tm, N // tn, J),
                    pl.BlockSpec((None, tm, n), lambda i, j, k: (k, i, 0)),
                    pl.BlockSpec((n, tn), lambda i, j, k: (k, j)),
                    pl.BlockSpec((tm, tn), lambda i, j, k: (i, j)),
                    jax.ShapeDtypeStruct((M, N), out_dtype), 2, res,
                    pl.BlockSpec((tm, tn), lambda i, j, k: (i, j)))


def _mm_nt_abs(name, a, b, N, out_dtype, layer=0):
    J, M, n = a.shape
    tm, tn = _tile(M, MM_ROWS, 16), _tile(N, 1024, LANES)
    first = layer * (N // tn)
    return _mm_call(name, a, b, _NT, (M // tm, N // tn, J),
                    pl.BlockSpec((None, tm, n), lambda i, j, k: (k, i, 0)),
                    pl.BlockSpec((None, tn, n), lambda i, j, k: (k, first + j, 0)),
                    pl.BlockSpec((tm, tn), lambda i, j, k: (i, j)),
                    jax.ShapeDtypeStruct((M, N), out_dtype), 2)


def _mm_tn_bs(name, a, b, out_dtype):
    R, M = a.shape
    J, _, n = b.shape
    tm, tr = _tile(M, 1024, LANES), _tile(R, MM_ROWS, 16)
    return _mm_call(name, a, b, _TN, (J, M // tm, R // tr),
                    pl.BlockSpec((tr, tm), lambda j, i, k: (k, i)),
                    pl.BlockSpec((None, tr, n), lambda j, i, k: (j, k, 0)),
                    pl.BlockSpec((None, tm, n), lambda j, i, k: (j, i, 0)),
                    jax.ShapeDtypeStruct((J, M, n), out_dtype), 2)


def _mm_tn_as(name, a, b, out_dtype):
    J, R, n = a.shape
    N = b.shape[1]
    tn, tr = _tile(N, 1024, LANES), _tile(R, MM_ROWS, 16)
    return _mm_call(name, a, b, _TN, (J, N // tn, R // tr),
                    pl.BlockSpec((None, tr, n), lambda j, jn, k: (j, k, 0)),
                    pl.BlockSpec((tr, tn), lambda j, jn, k: (k, jn)),
                    pl.BlockSpec((n, tn), lambda j, jn, k: (j, jn)),
                    jax.ShapeDtypeStruct((J * n, N), out_dtype), 2)


def _rmsnorm(name, x, g, out_dtype):
    R, D = x.shape
    tm = _tile(R, 512, 16)

    def body(x_ref, g_ref, o_ref):
        xv = x_ref[...]
        r = lax.rsqrt(jnp.mean(xv * xv, axis=-1, keepdims=True) + EPS)
        o_ref[...] = (xv * r * g_ref[...]).astype(o_ref.dtype)

    return _pc(body, name=name, grid=(R // tm,),
               in_specs=[pl.BlockSpec((tm, D), lambda i: (i, 0)), pl.BlockSpec((1, D), lambda i: (0, 0))],
               out_specs=pl.BlockSpec((tm, D), lambda i: (i, 0)),
               out_shape=jax.ShapeDtypeStruct((R, D), out_dtype), compiler_params=_sem("parallel"))(x, g)


def _rmsnorm_bwd(name, x, g, dh, dres=None):
    R, D = x.shape
    tm = _tile(R, 512, 16)

    def body(*refs):
        if dres is None:
            x_ref, g_ref, dh_ref, dx_ref, dg_ref = refs
            dres_ref = None
        else:
            x_ref, g_ref, dh_ref, dres_ref, dx_ref, dg_ref = refs
        xv = x_ref[...]
        r = lax.rsqrt(jnp.mean(xv * xv, axis=-1, keepdims=True) + EPS)
        xh = xv * r
        dhv = dh_ref[...].astype(F32)

        @pl.when(pl.program_id(0) == 0)
        def _():
            dg_ref[...] = jnp.zeros_like(dg_ref)

        dg_ref[...] += jnp.sum(dhv * xh, axis=0, keepdims=True)
        dxh = dhv * g_ref[...]
        dx = r * (dxh - xh * jnp.mean(dxh * xh, axis=-1, keepdims=True))
        if dres_ref is not None:
            dx = dx + dres_ref[...]
        dx_ref[...] = dx

    row = pl.BlockSpec((tm, D), lambda i: (i, 0))
    vec = pl.BlockSpec((1, D), lambda i: (0, 0))
    ins = [x, g, dh] + ([] if dres is None else [dres])
    specs = [row, vec, row] + ([] if dres is None else [row])
    return _pc(body, name=name, grid=(R // tm,), in_specs=specs, out_specs=(row, vec),
               out_shape=(jax.ShapeDtypeStruct((R, D), F32), jax.ShapeDtypeStruct((1, D), F32)),
               compiler_params=_sem("arbitrary"))(*ins)


def _loss_head(name, x, g, tgt):
    R, D = x.shape
    tm = _tile(R, 512, 16)

    def body(x_ref, g_ref, t_ref, dx_ref, dg_ref, l_ref):
        xv = x_ref[...]
        r = lax.rsqrt(jnp.mean(xv * xv, axis=-1, keepdims=True) + EPS)
        xh = xv * r
        err = xh * g_ref[...] - t_ref[...]

        @pl.when(pl.program_id(0) == 0)
        def _():
            dg_ref[...] = jnp.zeros_like(dg_ref)
            l_ref[...] = jnp.zeros_like(l_ref)

        l_ref[...] += jnp.sum(err * err, axis=0, keepdims=True)
        dy = err * (1.0 / D)
        dg_ref[...] += jnp.sum(dy * xh, axis=0, keepdims=True)
        dxh = dy * g_ref[...]
        dx_ref[...] = r * (dxh - xh * jnp.mean(dxh * xh, axis=-1, keepdims=True))

    row = pl.BlockSpec((tm, D), lambda i: (i, 0))
    vec = pl.BlockSpec((1, D), lambda i: (0, 0))
    return _pc(body, name=name, grid=(R // tm,), in_specs=[row, vec, row], out_specs=(row, vec, vec),
               out_shape=(jax.ShapeDtypeStruct((R, D), F32), jax.ShapeDtypeStruct((1, D), F32),
                          jax.ShapeDtypeStruct((1, D), F32)),
               compiler_params=_sem("arbitrary"))(x, g, tgt)


def _rope_tables(S):
    n_rows = S // GRID_W
    row = jnp.repeat(jnp.arange(n_rows, dtype=F32), GRID_W)
    col = jnp.tile(jnp.arange(GRID_W, dtype=F32), n_rows)
    inv_freq = ROPE_THETA ** (-jnp.arange(ROPE_PAIRS, dtype=F32) / ROPE_PAIRS)
    ang = jnp.stack([row[:, None] * inv_freq, col[:, None] * inv_freq], axis=1)
    cos, sin = jnp.cos(ang), jnp.sin(ang)
    c = jnp.broadcast_to(cos[:, :, None, :], (S, 2, 2, ROPE_PAIRS)).reshape(S, HEAD_DIM)
    s = jnp.stack([-sin, sin], axis=2).reshape(S, HEAD_DIM)
    reps = LANES // HEAD_DIM
    return jnp.tile(c, (1, reps)), jnp.tile(s, (1, reps))


def _head_mean_matrix():
    h = jnp.arange(LANES) // HEAD_DIM
    m = jnp.where(h[:, None] == h[None, :], 1.0 / HEAD_DIM, 0.0).astype(BF16)
    return jnp.concatenate([m, m], axis=0)


def _head_mean(v, bd):
    hi = v.astype(BF16)
    lo = (v - hi.astype(F32)).astype(BF16)
    return jnp.dot(jnp.concatenate([hi, lo], axis=1), bd, preferred_element_type=F32)


def _swap_halves(y):
    lane = lax.broadcasted_iota(jnp.int32, y.shape, 1)
    return jnp.where(lane % 32 < 16, pltpu.roll(y, LANES - 16, 1), pltpu.roll(y, 16, 1))


def _qk_rope(name, qkv, gain, scale, cos, sin, bd, n_rot):
    S, W = qkv.shape
    tm = _tile(S, 2048, 16)

    def body(x_ref, g_ref, s_ref, c_ref, sn_ref, bd_ref, o_ref):
        j = pl.program_id(1)
        xv = x_ref[...]

        @pl.when(j < n_rot)
        def _():
            ms = _head_mean(xv * xv, bd_ref[...])
            y = xv * lax.rsqrt(ms + EPS) * g_ref[...] * s_ref[...]
            o_ref[...] = (y * c_ref[...] + _swap_halves(y) * sn_ref[...]).astype(BF16)

        @pl.when(j >= n_rot)
        def _():
            o_ref[...] = xv.astype(BF16)

    blk = pl.BlockSpec((tm, LANES), lambda i, j: (i, j))
    vec = pl.BlockSpec((1, LANES), lambda i, j: (0, j))
    tab = pl.BlockSpec((tm, LANES), lambda i, j: (i, 0))
    return _pc(body, name=name, grid=(S // tm, W // LANES),
               in_specs=[blk, vec, vec, tab, tab, pl.BlockSpec((2 * LANES, LANES), lambda i, j: (0, 0))],
               out_specs=blk, out_shape=jax.ShapeDtypeStruct((S, W), BF16),
               compiler_params=_sem("parallel", "parallel"))(qkv, gain, scale, cos, sin, bd)


def _qk_rope_bwd(name, d_out, qkv, gain, scale, cos, sin, bd, n_rot):
    S, W = qkv.shape
    tm = _tile(S, 2048, 16)

    def body(d_ref, x_ref, g_ref, s_ref, c_ref, sn_ref, bd_ref, dx_ref, dg_ref):
        j, i = pl.program_id(0), pl.program_id(1)
        dv = d_ref[...]

        @pl.when(i == 0)
        def _():
            dg_ref[...] = jnp.zeros_like(dg_ref)

        @pl.when(j < n_rot)
        def _():
            xv = x_ref[...]
            ms = _head_mean(xv * xv, bd_ref[...])
            r = lax.rsqrt(ms + EPS)
            z = xv * r
            dy = (dv * c_ref[...] - _swap_halves(dv) * sn_ref[...]) * s_ref[...]
            dg_ref[...] += jnp.sum(dy * z, axis=0, keepdims=True)
            dz = dy * g_ref[...]
            mz = _head_mean(dz * z, bd_ref[...])
            dx_ref[...] = (r * (dz - z * mz)).astype(BF16)

        @pl.when(j >= n_rot)
        def _():
            dx_ref[...] = dv.astype(BF16)

    blk = pl.BlockSpec((tm, LANES), lambda j, i: (i, j))
    vec = pl.BlockSpec((1, LANES), lambda j, i: (0, j))
    tab = pl.BlockSpec((tm, LANES), lambda j, i: (i, 0))
    return _pc(body, name=name, grid=(W // LANES, S // tm),
               in_specs=[blk, blk, vec, vec, tab, tab, pl.BlockSpec((2 * LANES, LANES), lambda j, i: (0, 0))],
               out_specs=(blk, vec),
               out_shape=(jax.ShapeDtypeStruct((S, W), BF16), jax.ShapeDtypeStruct((1, W), F32)),
               compiler_params=_sem("parallel", "arbitrary"))(d_out, qkv, gain, scale, cos, sin, bd)


def _softmax_rows(s):
    m = jnp.max(s, axis=-1, keepdims=True)
    p = jnp.exp(s - m)
    return p, jnp.sum(p, axis=-1, keepdims=True)


def _attn_fwd(name, q, k, v):
    H, S, dh = q.shape
    G = H // N_KV_HEADS
    tq = _tile(S, 128, 16)

    kc = _tile(S, KEY_CHUNK, LANES)
    R = G * tq

    def body(q_ref, k_ref, v_ref, o_ref, lse_ref):
        q = q_ref[...].reshape(R, dh)
        m = jnp.full((R, 1), -1e30, F32)
        l = jnp.zeros((R, 1), F32)
        acc = jnp.zeros((R, dh), F32)
        for c in range(S // kc):
            rows = slice(c * kc, (c + 1) * kc)
            s = lax.dot_general(q, k_ref[rows, :], _NT, preferred_element_type=F32)
            m_new = jnp.maximum(m, jnp.max(s, axis=-1, keepdims=True))
            alpha = jnp.exp(m - m_new)
            p = jnp.exp(s - m_new)
            l = alpha * l + jnp.sum(p, axis=-1, keepdims=True)
            acc = alpha * acc + jnp.dot(p.astype(BF16), v_ref[rows, :], preferred_element_type=F32)
            m = m_new
        o_ref[...] = (acc / l).astype(BF16).reshape(G, tq, dh)
        lse_ref[...] = (m + jnp.log(l)).reshape(G, tq, 1)

    qs = pl.BlockSpec((G, tq, dh), lambda kv, i: (kv, i, 0))
    ls = pl.BlockSpec((G, tq, 1), lambda kv, i: (kv, i, 0))
    ks = pl.BlockSpec((None, S, dh), lambda kv, i: (kv, 0, 0))
    return _pc(body, name=name, grid=(N_KV_HEADS, S // tq), in_specs=[qs, ks, ks], out_specs=(qs, ls),
               out_shape=(jax.ShapeDtypeStruct((H, S, dh), BF16), jax.ShapeDtypeStruct((H, S, 1), F32)),
               compiler_params=_sem("parallel", "parallel"))(q, k, v)


def _attn_bwd(name, q, k, v, o, lse, do):
    H, S, dh = q.shape
    G = H // N_KV_HEADS
    tq = _tile(S, 128, 16)
    kc = _tile(S, KEY_CHUNK, LANES)
    R = G * tq

    def body(q_ref, k_ref, v_ref, o_ref, lse_ref, do_ref, dq_ref, dk_ref, dv_ref):
        @pl.when(pl.program_id(1) == 0)
        def _():
            dk_ref[...] = jnp.zeros_like(dk_ref)
            dv_ref[...] = jnp.zeros_like(dv_ref)

        qq, dd = q_ref[...].reshape(R, dh), do_ref[...].reshape(R, dh)
        delta = jnp.sum(dd.astype(F32) * o_ref[...].reshape(R, dh).astype(F32), axis=-1, keepdims=True)
        lse = lse_ref[...].reshape(R, 1)
        dq = jnp.zeros((R, dh), F32)
        for c in range(S // kc):
            rows = slice(c * kc, (c + 1) * kc)
            kk, vv = k_ref[rows, :], v_ref[rows, :]
            p = jnp.exp(lax.dot_general(qq, kk, _NT, preferred_element_type=F32) - lse)
            dv_ref[rows, :] += lax.dot_general(p.astype(BF16), dd, _TN, preferred_element_type=F32)
            dp = lax.dot_general(dd, vv, _NT, preferred_element_type=F32)
            ds = (p * (dp - delta)).astype(BF16)
            dq = dq + jnp.dot(ds, kk, preferred_element_type=F32)
            dk_ref[rows, :] += lax.dot_general(ds, qq, _TN, preferred_element_type=F32)
        dq_ref[...] = dq.reshape(G, tq, dh)

    qs = pl.BlockSpec((G, tq, dh), lambda kv, i: (kv, i, 0))
    ls = pl.BlockSpec((G, tq, 1), lambda kv, i: (kv, i, 0))
    ks = pl.BlockSpec((None, S, dh), lambda kv, i: (kv, 0, 0))
    return _pc(body, name=name, grid=(N_KV_HEADS, S // tq), in_specs=[qs, ks, ks, qs, ls, qs],
               out_specs=(qs, ks, ks),
               out_shape=(jax.ShapeDtypeStruct((H, S, dh), F32), jax.ShapeDtypeStruct((N_KV_HEADS, S, dh), F32),
                          jax.ShapeDtypeStruct((N_KV_HEADS, S, dh), F32)),
               compiler_params=_sem("parallel", "arbitrary"))(q, k, v, o, lse, do)


def _xattn_fwd(name, q, kv):
    S, D = q.shape
    _, M, dh = kv.shape
    scale = dh ** -0.5
    tq = _tile(S, 256, 16)

    def body(q_ref, kv_ref, o_ref):
        for h in range(X_HEADS):
            lo, hi = h * dh, (h + 1) * dh
            s = lax.dot_general(q_ref[:, lo:hi], kv_ref[h], _NT, preferred_element_type=F32) * scale
            p, l = _softmax_rows(s)
            o = jnp.dot(p.astype(BF16), kv_ref[X_HEADS + h], preferred_element_type=F32)
            o_ref[:, lo:hi] = (o / l).astype(BF16)

    row = pl.BlockSpec((tq, D), lambda i: (i, 0))
    return _pc(body, name=name, grid=(S // tq,),
               in_specs=[row, pl.BlockSpec((2 * X_HEADS, M, dh), lambda i: (0, 0, 0))],
               out_specs=row, out_shape=jax.ShapeDtypeStruct((S, D), BF16),
               compiler_params=_sem("parallel"))(q, kv)


def _xattn_bwd(name, q, kv, do):
    S, D = q.shape
    _, M, dh = kv.shape
    scale = dh ** -0.5
    tq = _tile(S, 256, 16)

    def body(q_ref, kv_ref, do_ref, dq_ref, dkv_ref):
        @pl.when(pl.program_id(0) == 0)
        def _():
            dkv_ref[...] = jnp.zeros_like(dkv_ref)

        for h in range(X_HEADS):
            lo, hi = h * dh, (h + 1) * dh
            qh, kh, vh, doh = q_ref[:, lo:hi], kv_ref[h], kv_ref[X_HEADS + h], do_ref[:, lo:hi]
            s = lax.dot_general(qh, kh, _NT, preferred_element_type=F32) * scale
            p, l = _softmax_rows(s)
            pn = p / l
            dkv_ref[X_HEADS + h] += lax.dot_general(pn.astype(BF16), doh, _TN, preferred_element_type=F32)
            dp = lax.dot_general(doh, vh, _NT, preferred_element_type=F32)
            ds = (pn * (dp - jnp.sum(pn * dp, axis=-1, keepdims=True)) * scale).astype(BF16)
            dq_ref[:, lo:hi] = jnp.dot(ds, kh, preferred_element_type=F32).astype(BF16)
            dkv_ref[h] += lax.dot_general(ds, qh, _TN, preferred_element_type=F32)

    row = pl.BlockSpec((tq, D), lambda i: (i, 0))
    full = pl.BlockSpec((2 * X_HEADS, M, dh), lambda i: (0, 0, 0))
    return _pc(body, name=name, grid=(S // tq,), in_specs=[row, full, row], out_specs=(row, full),
               out_shape=(jax.ShapeDtypeStruct((S, D), BF16), jax.ShapeDtypeStruct((2 * X_HEADS, M, dh), F32)),
               compiler_params=_sem("arbitrary"))(q, kv, do)


def _sigmoid(x):
    return 1.0 / (1.0 + jnp.exp(-x))


def _halo_specs(tm, n, S):
    nb = tm // 8
    last8 = S // 8 - 1
    main = pl.BlockSpec((2, None, tm, n), lambda j, i: (0, j, i, 0))
    prev = pl.BlockSpec((2, None, 8, n), lambda j, i: (0, j, jnp.maximum(i * nb - 1, 0), 0))
    nxt = pl.BlockSpec((2, None, 8, n), lambda j, i: (0, j, jnp.minimum((i + 1) * nb, last8), 0))
    return main, prev, nxt


def _ffn_act(name, u, cw, cb):
    _, J, S, n = u.shape
    tm = _tile(S, 256, 16)
    nblk = S // tm

    def body(u_ref, up_ref, un_ref, w_ref, b_ref, a_ref):
        i = pl.program_id(1)
        row = lax.broadcasted_iota(jnp.int32, (tm, n), 0)
        c = []
        for half in range(2):
            main = u_ref[half]
            before = jnp.where(i > 0, up_ref[half, 7:8, :], 0.0)
            after = jnp.where(i < nblk - 1, un_ref[half, 0:1, :], 0.0)
            um = jnp.where(row == 0, before, pltpu.roll(main, 1, 0))
            up = jnp.where(row == tm - 1, after, pltpu.roll(main, tm - 1, 0))
            w = w_ref[half]
            c.append(um * w[0:1] + main * w[1:2] + up * w[2:3] + b_ref[half])
        a_ref[...] = (c[0] * _sigmoid(c[0]) * c[1]).astype(BF16)

    main, prev, nxt = _halo_specs(tm, n, S)
    return _pc(body, name=name, grid=(J, nblk),
               in_specs=[main, prev, nxt, pl.BlockSpec((2, None, 3, n), lambda j, i: (0, j, 0, 0)),
                         pl.BlockSpec((2, None, 1, n), lambda j, i: (0, j, 0, 0))],
               out_specs=pl.BlockSpec((None, tm, n), lambda j, i: (j, i, 0)),
               out_shape=jax.ShapeDtypeStruct((J, S, n), BF16),
               compiler_params=_sem("parallel", "parallel"))(u, u, u, cw, cb)


def _ffn_act_bwd(name, u, da, cw, cb):
    _, J, S, n = u.shape
    tm = _tile(S, 256, 16)
    nblk = S // tm
    te = tm + 16
    nb = tm // 8
    last8 = S // 8 - 1

    def body(u_ref, up_ref, un_ref, da_ref, dap_ref, dan_ref, w_ref, b_ref, du_ref, st_ref):
        i = pl.program_id(1)

        @pl.when(i == 0)
        def _():
            st_ref[...] = jnp.zeros_like(st_ref)

        r = lax.broadcasted_iota(jnp.int32, (te, n), 0)
        t = i * tm - 8 + r
        valid = (t >= 0) & (t < S)
        mid = (r >= 8) & (r < tm + 8)
        da_e = jnp.where(valid, jnp.concatenate([dap_ref[...], da_ref[...], dan_ref[...]], axis=0).astype(F32), 0.0)
        ue, c = [], []
        for half in range(2):
            e = jnp.where(valid, jnp.concatenate([up_ref[half], u_ref[half], un_ref[half]], axis=0), 0.0)
            w = w_ref[half]
            ue.append((pltpu.roll(e, 1, 0), e, pltpu.roll(e, te - 1, 0)))
            c.append(ue[half][0] * w[0:1] + e * w[1:2] + ue[half][2] * w[2:3] + b_ref[half])
        sg = _sigmoid(c[0])
        dc = [jnp.where(valid, da_e * c[1] * (sg * (1.0 + c[0] * (1.0 - sg))), 0.0),
              jnp.where(valid, da_e * (c[0] * sg), 0.0)]
        r8 = lax.broadcasted_iota(jnp.int32, (8, n), 0)
        for half in range(2):
            w, d, (e_before, e, e_after) = w_ref[half], dc[half], ue[half]
            du = pltpu.roll(d, te - 1, 0) * w[0:1] + d * w[1:2] + pltpu.roll(d, 1, 0) * w[2:3]
            du_ref[half] = du[8:tm + 8].astype(BF16)
            dm = jnp.where(mid, d, 0.0)
            s0 = jnp.sum(dm * e_before, axis=0, keepdims=True)
            s1 = jnp.sum(dm * e, axis=0, keepdims=True)
            s2 = jnp.sum(dm * e_after, axis=0, keepdims=True)
            s3 = jnp.sum(dm, axis=0, keepdims=True)
            st_ref[half] += jnp.where(r8 == 0, s0, jnp.where(r8 == 1, s1, jnp.where(r8 == 2, s2,
                                      jnp.where(r8 == 3, s3, 0.0))))

    main, prev, nxt = _halo_specs(tm, n, S)
    dmain = pl.BlockSpec((None, tm, n), lambda j, i: (j, i, 0))
    dprev = pl.BlockSpec((None, 8, n), lambda j, i: (j, jnp.maximum(i * nb - 1, 0), 0))
    dnxt = pl.BlockSpec((None, 8, n), lambda j, i: (j, jnp.minimum((i + 1) * nb, last8), 0))
    return _pc(body, name=name, grid=(J, nblk),
               in_specs=[main, prev, nxt, dmain, dprev, dnxt,
                         pl.BlockSpec((2, None, 3, n), lambda j, i: (0, j, 0, 0)),
                         pl.BlockSpec((2, None, 1, n), lambda j, i: (0, j, 0, 0))],
               out_specs=(main, pl.BlockSpec((2, None, 8, n), lambda j, i: (0, j, 0, 0))),
               out_shape=(jax.ShapeDtypeStruct((2, J, S, n), BF16), jax.ShapeDtypeStruct((2, J, 8, n), F32)),
               compiler_params=_sem("parallel", "arbitrary"))(u, u, u, da, da, da, cw, cb)


def _window_count(t, w, S):
    lo = jnp.maximum(t - w // 2, 0)
    hi = jnp.minimum(t + w - w // 2, S)
    return (hi - lo).astype(F32)


def _trailing_sums(x, w):
    k = 1
    while k < w:
        x = x + pltpu.roll(x, k, 0)
        k *= 2
    return x


def _pool_window(name, h, group_w, adjoint, out_dtype):
    S, D = h.shape
    SP = S + 2 * POOL_PAD
    per_group = group_w // LANES

    def body(h_ref, o_ref, xp):
        g = pl.program_id(0) // per_group
        t = lax.broadcasted_iota(jnp.int32, (S, LANES), 0)
        xp[0:POOL_PAD, :] = jnp.zeros((POOL_PAD, LANES), F32)
        xp[S + POOL_PAD:SP, :] = jnp.zeros((POOL_PAD, LANES), F32)
        for gi, w in enumerate(POOL_WINDOWS):
            @pl.when(g == gi)
            def _():
                hv = h_ref[...]
                cnt = _window_count(t, w, S)
                xp[POOL_PAD:S + POOL_PAD, :] = hv / cnt if adjoint else hv
                ahead = w // 2 if adjoint else w // 2 - 1
                sw = _trailing_sums(xp[...], w)
                if ahead:
                    sw = pltpu.roll(sw, SP - ahead, 0)
                win = sw[POOL_PAD:S + POOL_PAD]
                o_ref[...] = ((win if adjoint else win / cnt) - hv).astype(out_dtype)

    col = pl.BlockSpec((S, LANES), lambda j: (0, j))
    return _pc(body, name=name, grid=(D // LANES,), in_specs=[col], out_specs=col,
               out_shape=jax.ShapeDtypeStruct((S, D), out_dtype),
               scratch_shapes=[pltpu.VMEM((SP, LANES), F32)], compiler_params=_sem("parallel"))(h)


def _pool_proj(name, mixed, w, scale, res):
    S, D = mixed.shape
    G, gw, _ = w.shape
    tm = _tile(S, 512, 16)

    def body(m_ref, w_ref, s_ref, r_ref, o_ref):
        for g in range(G):
            lo, hi = g * gw, (g + 1) * gw
            y = jnp.dot(m_ref[:, lo:hi], w_ref[g], preferred_element_type=F32)
            o_ref[:, lo:hi] = r_ref[:, lo:hi] + y * s_ref[:, lo:hi]

    row = pl.BlockSpec((tm, D), lambda i: (i, 0))
    return _pc(body, name=name, grid=(S // tm,),
               in_specs=[row, pl.BlockSpec((G, gw, gw), lambda i: (0, 0, 0)), pl.BlockSpec((1, D), lambda i: (0, 0)), row],
               out_specs=row, out_shape=jax.ShapeDtypeStruct((S, D), F32),
               compiler_params=_sem("parallel"))(mixed, w, scale, res)


def _pool_proj_bwd(name, dy, mixed, w, scale):
    S, D = mixed.shape
    G, gw, _ = w.shape
    tm = _tile(S, 512, 16)

    def body(dy_ref, m_ref, w_ref, s_ref, dm_ref, dw_ref, ds_ref):
        @pl.when(pl.program_id(0) == 0)
        def _():
            dw_ref[...] = jnp.zeros_like(dw_ref)
            ds_ref[...] = jnp.zeros_like(ds_ref)

        for g in range(G):
            lo, hi = g * gw, (g + 1) * gw
            mg, dyg = m_ref[:, lo:hi], dy_ref[:, lo:hi]
            y = jnp.dot(mg, w_ref[g], preferred_element_type=F32)
            ds_ref[:, lo:hi] += jnp.sum(dyg * y, axis=0, keepdims=True)
            dyp = (dyg * s_ref[:, lo:hi]).astype(BF16)
            dm_ref[:, lo:hi] = lax.dot_general(dyp, w_ref[g], _NT, preferred_element_type=F32)
            dw_ref[g] += lax.dot_general(mg, dyp, _TN, preferred_element_type=F32)

    row = pl.BlockSpec((tm, D), lambda i: (i, 0))
    wsp = pl.BlockSpec((G, gw, gw), lambda i: (0, 0, 0))
    vec = pl.BlockSpec((1, D), lambda i: (0, 0))
    return _pc(body, name=name, grid=(S // tm,), in_specs=[row, row, wsp, vec], out_specs=(row, wsp, vec),
               out_shape=(jax.ShapeDtypeStruct((S, D), F32), jax.ShapeDtypeStruct((G, gw, gw), F32),
                          jax.ShapeDtypeStruct((1, D), F32)),
               compiler_params=_sem("arbitrary"))(dy, mixed, w, scale)


def _adamw(name, w, g, m, v):
    shape = w.shape
    C = shape[-1]
    R = w.size // C
    tm = _tile(R, 512, 8)

    def body(w_ref, g_ref, m_ref, v_ref, d_ref, nm_ref, nv_ref):
        gv = g_ref[...]
        nm = ADAM_B1 * m_ref[...] + (1.0 - ADAM_B1) * gv
        nv = ADAM_B2 * v_ref[...] + (1.0 - ADAM_B2) * (gv * gv)
        m_hat = nm / (1.0 - ADAM_B1 ** ADAM_STEP)
        v_hat = nv / (1.0 - ADAM_B2 ** ADAM_STEP)
        d_ref[...] = -ADAM_LR * (m_hat / (jnp.sqrt(v_hat) + ADAM_EPS) + ADAM_WD * w_ref[...])
        nm_ref[...] = nm
        nv_ref[...] = nv

    blk = pl.BlockSpec((tm, C), lambda i: (i, 0))
    sd = jax.ShapeDtypeStruct((R, C), F32)
    outs = _pc(body, name=name, grid=(R // tm,), in_specs=[blk] * 4, out_specs=(blk,) * 3, out_shape=(sd,) * 3,
               compiler_params=_sem("parallel"))(*(a.reshape(R, C) for a in (w, g, m, v)))
    return tuple(o.reshape(shape) for o in outs)


def _position():
    return lax.axis_index("x"), lax.axis_index("y"), lax.axis_index("c")


def _flip(v, bit):
    return 1 - v if bit else v


def _allgather_small(name, v):
    R, W = v.shape

    def body(v_ref, out_ref, send_sems, recv_sems):
        x, y, c = _position()
        me = 4 * x + 2 * y + c
        out_ref[me] = v_ref[...]
        sends = []
        for k in range(1, N_DEV):
            peer = (_flip(x, k & 4), _flip(y, k & 2), _flip(c, k & 1))
            cp = pltpu.make_async_remote_copy(src_ref=v_ref, dst_ref=out_ref.at[me], send_sem=send_sems.at[k - 1],
                                              recv_sem=recv_sems.at[k - 1], device_id=peer, device_id_type=MESH)
            cp.start()
            sends.append(cp)
        for k in range(1, N_DEV):
            peer = (_flip(x, k & 4), _flip(y, k & 2), _flip(c, k & 1))
            slot = 4 * peer[0] + 2 * peer[1] + peer[2]
            pltpu.make_async_remote_copy(src_ref=v_ref, dst_ref=out_ref.at[slot], send_sem=send_sems.at[k - 1],
                                         recv_sem=recv_sems.at[k - 1], device_id=peer, device_id_type=MESH).wait_recv()
        for cp in sends:
            cp.wait_send()

    vm = pl.BlockSpec(memory_space=pltpu.VMEM)
    return _pc(body, name=name, in_specs=[vm], out_specs=vm, out_shape=jax.ShapeDtypeStruct((N_DEV, R, W), F32),
               scratch_shapes=[pltpu.SemaphoreType.DMA((N_DEV - 1,)), pltpu.SemaphoreType.DMA((N_DEV - 1,))])(v)


def _sum_slots(name, a):
    n, R, W = a.shape

    def body(a_ref, o_ref):
        acc = a_ref[0]
        for s in range(1, n):
            acc = acc + a_ref[s]
        o_ref[...] = acc

    return _pc(body, name=name, grid=(1,), in_specs=[pl.BlockSpec((n, R, W), lambda i: (0, 0, 0))],
               out_specs=pl.BlockSpec((R, W), lambda i: (0, 0)), out_shape=jax.ShapeDtypeStruct((R, W), F32))(a)


def _allgather_blocks(name, blocks):
    n = len(blocks)

    def body(*refs):
        b_refs, out_refs = refs[:n], refs[n:2 * n]
        send_sems, recv_sems, local_sems = refs[2 * n:]
        x, y, c = _position()
        me, sibling = (x, y, c), (x, y, 1 - c)
        chips = [(1 - x, y), (x, 1 - y), (1 - x, 1 - y)]

        def slot(i, px, py, pc):
            return out_refs[i].at[4 * px + 2 * py + pc]

        def copy(i, k, block, to, src=None):
            return pltpu.make_async_remote_copy(src_ref=slot(i, *block) if src is None else src, dst_ref=slot(i, *block),
                                                send_sem=send_sems.at[k, i], recv_sem=recv_sems.at[k, i],
                                                device_id=to, device_id_type=MESH)

        mine = [pltpu.make_async_copy(b_refs[i], slot(i, *me), local_sems.at[i]) for i in range(n)]
        first = [copy(i, 1 + j, me, (*chip, c), src=b_refs[i]) for i in range(n) for j, chip in enumerate(chips)]
        first += [copy(i, 0, me, sibling, src=b_refs[i]) for i in range(n)]
        for cp in mine + first:
            cp.start()
        passed = []
        for j, chip in enumerate(chips):
            for i in range(n):
                copy(i, 1 + j, (*chip, c), me).wait_recv()
                passed.append(copy(i, 4 + j, (*chip, c), sibling))
                passed[-1].start()
        for i in range(n):
            copy(i, 0, sibling, me).wait_recv()
        for j, chip in enumerate(chips):
            for i in range(n):
                copy(i, 4 + j, (*chip, 1 - c), me).wait_recv()
        for cp in first + passed:
            cp.wait_send()
        for cp in mine:
            cp.wait()

    hbm = pl.BlockSpec(memory_space=pl.ANY)
    return _pc(body, name=name, in_specs=[hbm] * n, out_specs=[hbm] * n,
               out_shape=[jax.ShapeDtypeStruct((N_DEV,) + b.shape, b.dtype) for b in blocks],
               scratch_shapes=[pltpu.SemaphoreType.DMA((7, n)), pltpu.SemaphoreType.DMA((7, n)),
                               pltpu.SemaphoreType.DMA((n,))])(*blocks)


def _swap_with_sibling(name, gs):
    n = len(gs)
    n_chips = gs[0].shape[0]

    def body(*refs):
        g_refs, r_refs = refs[:n], refs[n:2 * n]
        send_sems, recv_sems = refs[2 * n:]
        x, y, c = _position()
        cps = [pltpu.make_async_remote_copy(src_ref=g_refs[i].at[k, 1 - c], dst_ref=r_refs[i].at[k],
                                            send_sem=send_sems.at[k, i], recv_sem=recv_sems.at[k, i],
                                            device_id=(x, y, 1 - c), device_id_type=MESH)
               for i in range(n) for k in range(n_chips)]
        for cp in cps:
            cp.start()
        for cp in cps:
            cp.wait()

    hbm = pl.BlockSpec(memory_space=pl.ANY)
    return _pc(body, name=name, in_specs=[hbm] * n, out_specs=[hbm] * n,
               out_shape=[jax.ShapeDtypeStruct((n_chips,) + g.shape[2:], F32) for g in gs],
               scratch_shapes=[pltpu.SemaphoreType.DMA((n_chips, n)), pltpu.SemaphoreType.DMA((n_chips, n))])(*gs)


def _add_sibling(name, g4, r1, pos):
    n, _, L, W = g4.shape
    tl = _tile(L, 512, 16)

    def body(pos_ref, g_ref, r_ref, tb_ref, own_ref):
        t = g_ref[...] + r_ref[...]
        tb_ref[...] = t.astype(BF16)

        @pl.when(pl.program_id(1) == pos_ref[1])
        def _():
            own_ref[...] = t

    gs = pltpu.PrefetchScalarGridSpec(
        num_scalar_prefetch=1, grid=(L // tl, n),
        in_specs=[pl.BlockSpec((None, None, tl, W), lambda i, k, p: (k, p[0], i, 0)),
                  pl.BlockSpec((None, tl, W), lambda i, k, p: (k, i, 0))],
        out_specs=(pl.BlockSpec((None, tl, W), lambda i, k, p: (k, i, 0)),
                   pl.BlockSpec((tl, W), lambda i, k, p: (i, 0))))
    return _pc(body, name=name, grid_spec=gs,
               out_shape=(jax.ShapeDtypeStruct((n, L, W), BF16), jax.ShapeDtypeStruct((L, W), F32)),
               compiler_params=_sem("parallel", "arbitrary"))(pos, g4, r1)


def _swap_with_chips(name, tbs):
    n = len(tbs)

    def body(*refs):
        t_refs, r_refs = refs[:n], refs[n:2 * n]
        send_sems, recv_sems = refs[2 * n:]
        x, y, c = _position()
        cps = []
        for i in range(n):
            for j in range(1, 4):
                px, py = _flip(x, j & 2), _flip(y, j & 1)
                cps.append(pltpu.make_async_remote_copy(src_ref=t_refs[i].at[2 * px + py], dst_ref=r_refs[i].at[j - 1],
                                                        send_sem=send_sems.at[j - 1, i], recv_sem=recv_sems.at[j - 1, i],
                                                        device_id=(px, py, c), device_id_type=MESH))
        for cp in cps:
            cp.start()
        for cp in cps:
            cp.wait()

    hbm = pl.BlockSpec(memory_space=pl.ANY)
    return _pc(body, name=name, in_specs=[hbm] * n, out_specs=[hbm] * n,
               out_shape=[jax.ShapeDtypeStruct((3,) + t.shape[1:], BF16) for t in tbs],
               scratch_shapes=[pltpu.SemaphoreType.DMA((3, n)), pltpu.SemaphoreType.DMA((3, n))])(*tbs)


def _add_chips(name, own, r2):
    L, W = own.shape
    tl = _tile(L, 512, 16)

    def body(o_ref, r_ref, out_ref):
        acc = o_ref[...]
        for j in range(3):
            acc = acc + r_ref[j].astype(F32)
        out_ref[...] = acc

    return _pc(body, name=name, grid=(L // tl,),
               in_specs=[pl.BlockSpec((tl, W), lambda i: (i, 0)), pl.BlockSpec((3, tl, W), lambda i: (0, i, 0))],
               out_specs=pl.BlockSpec((tl, W), lambda i: (i, 0)), out_shape=jax.ShapeDtypeStruct((L, W), F32),
               compiler_params=_sem("parallel"))(own, r2)


def _heads_major(a, S):
    return a.reshape(S, -1, HEAD_DIM).transpose(1, 0, 2)


def _heads_minor(a):
    return a.transpose(1, 0, 2).reshape(a.shape[1], -1)


def kernel(x, mem, attn_norm, attn_w_qkv, attn_q_gain, attn_k_gain, attn_w_o, pool_norm, pool_w, pool_scale, xattn_norm, mem_norm, xattn_w_q, xattn_w_kv, xattn_w_o, ffn_norm, ffn_w_up, ffn_conv_w, ffn_conv_b, ffn_w_down, final_norm, loss_target, m_attn_norm, m_attn_w_qkv, m_attn_q_gain, m_attn_k_gain, m_attn_w_o, m_pool_norm, m_pool_w, m_pool_scale, m_xattn_norm, m_mem_norm, m_xattn_w_q, m_xattn_w_kv, m_xattn_w_o, m_ffn_norm, m_ffn_w_up, m_ffn_conv_w, m_ffn_conv_b, m_ffn_w_down, m_final_norm, v_attn_norm, v_attn_w_qkv, v_attn_q_gain, v_attn_k_gain, v_attn_w_o, v_pool_norm, v_pool_w, v_pool_scale, v_xattn_norm, v_mem_norm, v_xattn_w_q, v_xattn_w_kv, v_xattn_w_o, v_ffn_norm, v_ffn_w_up, v_ffn_conv_w, v_ffn_conv_b, v_ffn_w_down, v_final_norm):
    names = ['attn_norm', 'attn_w_qkv', 'attn_q_gain', 'attn_k_gain', 'attn_w_o', 'pool_norm', 'pool_w', 'pool_scale',
             'xattn_norm', 'mem_norm', 'xattn_w_q', 'xattn_w_kv', 'xattn_w_o', 'ffn_norm', 'ffn_w_up', 'ffn_conv_w',
             'ffn_conv_b', 'ffn_w_down', 'final_norm']
    W = dict(zip(names, (attn_norm, attn_w_qkv, attn_q_gain, attn_k_gain, attn_w_o, pool_norm, pool_w, pool_scale,
                         xattn_norm, mem_norm, xattn_w_q, xattn_w_kv, xattn_w_o, ffn_norm, ffn_w_up, ffn_conv_w,
                         ffn_conv_b, ffn_w_down, final_norm)))
    Mo = dict(zip(names, (m_attn_norm, m_attn_w_qkv, m_attn_q_gain, m_attn_k_gain, m_attn_w_o, m_pool_norm, m_pool_w,
                          m_pool_scale, m_xattn_norm, m_mem_norm, m_xattn_w_q, m_xattn_w_kv, m_xattn_w_o, m_ffn_norm,
                          m_ffn_w_up, m_ffn_conv_w, m_ffn_conv_b, m_ffn_w_down, m_final_norm)))
    Vo = dict(zip(names, (v_attn_norm, v_attn_w_qkv, v_attn_q_gain, v_attn_k_gain, v_attn_w_o, v_pool_norm, v_pool_w,
                          v_pool_scale, v_xattn_norm, v_mem_norm, v_xattn_w_q, v_xattn_w_kv, v_xattn_w_o, v_ffn_norm,
                          v_ffn_w_up, v_ffn_conv_w, v_ffn_conv_b, v_ffn_w_down, v_final_norm)))

    S, D = x.shape[1], x.shape[2]
    n_layers = xattn_norm.shape[0]
    n_up = ffn_w_up.shape[2]
    qkv_w = attn_w_qkv.shape[2] * N_DEV
    n_heads = qkv_w // HEAD_DIM - 2 * N_KV_HEADS
    n_rot = (n_heads + N_KV_HEADS) * HEAD_DIM // LANES
    group_w = pool_w.shape[3]
    xs, mems, tgt = x[0], mem[0], loss_target[0]
    xi, yi, ci = _position()
    dev = 4 * xi + 2 * yi + ci
    pos = jnp.stack([ci, 2 * xi + yi]).astype(jnp.int32)

    layers = range(n_layers)
    blocks = [attn_w_qkv[0], attn_w_o[0], pool_w.reshape(-1, group_w)]
    blocks += [xattn_w_q[l] for l in layers] + [xattn_w_kv.reshape(n_layers * D, -1)]
    blocks += [xattn_w_o[l] for l in layers] + [ffn_w_up.reshape(n_layers * D, n_up)] + [ffn_w_down[l] for l in layers]
    gathered = iter(_allgather_blocks("allgather_weights", [b.astype(BF16) for b in blocks]))
    w_qkv = next(gathered).transpose(1, 0, 2).reshape(D, qkv_w)
    w_o = next(gathered).reshape(-1, D)
    n_groups = pool_w.shape[1]
    w_pool = (next(gathered).reshape(N_DEV, n_groups, -1, group_w).transpose(1, 0, 2, 3)
              .reshape(n_groups, group_w, group_w))
    w_xq = [next(gathered).reshape(D, D) for l in layers]
    w_xkv = next(gathered)
    w_xo = [next(gathered).reshape(D, D) for l in layers]
    w_up = next(gathered)
    w_down = [next(gathered).reshape(-1, D) for l in layers]

    small_vec = jnp.concatenate([pool_norm.reshape(-1), pool_scale.reshape(-1), ffn_conv_w.reshape(-1)])
    small_rows = _round_up(-(-small_vec.size // PACK_W), 8)
    small = _allgather_small("allgather_small", jnp.pad(small_vec, (0, small_rows * PACK_W - small_vec.size))
                             .reshape(small_rows, PACK_W)).reshape(N_DEV, -1)
    d_sh = pool_norm.shape[1]
    pool_norm_f = small[:, :d_sh].reshape(1, D)
    pool_scale_f = small[:, d_sh:2 * d_sh].reshape(1, D)
    conv_w_f = small[:, 2 * d_sh:2 * d_sh + ffn_conv_w.size].reshape(N_DEV, n_layers, 3, n_up)
    conv_b_f = ffn_conv_b.reshape(n_layers, N_DEV, 1, n_up)

    cos, sin = _rope_tables(S)
    bd = _head_mean_matrix()
    pad_w = qkv_w - (n_heads + N_KV_HEADS) * HEAD_DIM
    qk_gain = jnp.concatenate([jnp.tile(attn_q_gain[0], n_heads), jnp.tile(attn_k_gain[0], N_KV_HEADS),
                               jnp.ones((pad_w,), F32)]).reshape(1, qkv_w)
    qk_scale = jnp.concatenate([jnp.full((n_heads * HEAD_DIM,), HEAD_DIM ** -0.5, F32),
                                jnp.ones((qkv_w - n_heads * HEAD_DIM,), F32)]).reshape(1, qkv_w)

    saved = []

    def xattn_ffn_fwd(l, xin):
        hx = _rmsnorm(f"xattn_norm{l}", xin, xattn_norm[l:l + 1], BF16)
        memn = _rmsnorm(f"mem_norm{l}", mems, mem_norm[l:l + 1], BF16)
        qx = _mm_nn(f"xattn_q{l}", hx, w_xq[l], BF16)
        kv = _mm_nn_bs(f"xattn_kv{l}", memn, w_xkv, BF16, l)
        ox = _xattn_fwd(f"xattn_fwd{l}", qx, kv)
        x2 = _mm_nn(f"xattn_o{l}", ox, w_xo[l], F32, res=xin)
        hf = _rmsnorm(f"ffn_norm{l}", x2, ffn_norm[l:l + 1], BF16)
        u = _mm_nn_bs(f"ffn_up{l}", hf, w_up, F32, l).reshape(2, N_DEV // 2, S, n_up)
        cw = conv_w_f[:, l].reshape(2, N_DEV // 2, 3, n_up)
        cb = conv_b_f[l].reshape(2, N_DEV // 2, 1, n_up)
        act = _ffn_act(f"ffn_act{l}", u, cw, cb)
        x3 = _mm_nn_as(f"ffn_down{l}", act, w_down[l], F32, x2)
        saved.append(dict(xin=xin, hx=hx, memn=memn, qx=qx, kv=kv, ox=ox, x2=x2, hf=hf, u=u, cw=cw, cb=cb, act=act))
        return x3

    h0 = _rmsnorm("attn_norm", xs, attn_norm, BF16)
    qkv = _mm_nn("attn_qkv", h0, w_qkv, F32)
    qkr = _heads_major(_qk_rope("qk_rope", qkv, qk_gain, qk_scale, cos, sin, bd, n_rot), S)
    q_hm, k_hm, v_hm = qkr[:n_heads], qkr[n_heads:n_heads + N_KV_HEADS], qkr[n_heads + N_KV_HEADS:]
    o_hm, lse = _attn_fwd("attn_fwd", q_hm, k_hm, v_hm)
    o_att = _heads_minor(o_hm)
    x1 = _mm_nn("attn_o", o_att, w_o, F32, res=xs)
    x3 = xattn_ffn_fwd(0, x1)
    hp = _rmsnorm("pool_norm", x3, pool_norm_f, F32)
    mixed = _pool_window("pool_window", hp, group_w, False, BF16)
    x4 = _pool_proj("pool_proj", mixed, w_pool, pool_scale_f, x3)
    x6 = xattn_ffn_fwd(1, x4)

    G = {}
    g, d_final, lvec = _loss_head("loss_head", x6, final_norm.reshape(1, D), tgt)
    G['final_norm'] = d_final.reshape(D)
    loss = lax.psum(0.5 * jnp.sum(lvec) / D, ("x", "y", "c"))

    d_xn, d_mn, d_fn, d_xq, d_xkv, d_xo, d_up, d_cw, d_cb, d_down = ([None] * n_layers for _ in range(10))

    def xattn_ffn_bwd(l, g):
        sv = saved[l]
        d_act = _mm_nt_bs(f"ffn_down_dx{l}", g, w_down[l], N_DEV // 2, F32)
        d_down[l] = _mm_tn_as(f"ffn_down_dw{l}", sv['act'], g, F32)
        du, st = _ffn_act_bwd(f"ffn_act_bwd{l}", sv['u'], d_act, sv['cw'], sv['cb'])
        du = du.reshape(N_DEV, S, n_up)
        st = st.reshape(N_DEV, 8, n_up)
        d_cw[l], d_cb[l] = st[:, 0:3], st[:, 3].reshape(-1)
        d_up[l] = _mm_tn_bs(f"ffn_up_dw{l}", sv['hf'], du, F32)
        dhf = _mm_nt_abs(f"ffn_up_dx{l}", du, w_up, D, F32, l)
        g, d_fn[l] = _rmsnorm_bwd(f"ffn_norm_bwd{l}", sv['x2'], ffn_norm[l:l + 1], dhf, g)
        d_xo[l] = _mm_tn(f"xattn_o_dw{l}", sv['ox'], g, F32)
        do = _mm_nt(f"xattn_o_dx{l}", g, w_xo[l], BF16)
        dq, dkv = _xattn_bwd(f"xattn_bwd{l}", sv['qx'], sv['kv'], do)
        d_xq[l] = _mm_tn(f"xattn_q_dw{l}", sv['hx'], dq, F32)
        d_xkv[l] = _mm_tn_bs(f"xattn_kv_dw{l}", sv['memn'], dkv, F32)
        dmemn = _mm_nt_abs(f"xattn_kv_dx{l}", dkv, w_xkv, D, F32, l)
        _, d_mn[l] = _rmsnorm_bwd(f"mem_norm_bwd{l}", mems, mem_norm[l:l + 1], dmemn)
        dhx = _mm_nt(f"xattn_q_dx{l}", dq, w_xq[l], F32)
        g, d_xn[l] = _rmsnorm_bwd(f"xattn_norm_bwd{l}", sv['xin'], xattn_norm[l:l + 1], dhx, g)
        return g

    g = xattn_ffn_bwd(1, g)
    d_mixed, d_pool_w, d_pool_scale = _pool_proj_bwd("pool_proj_bwd", g, mixed, w_pool, pool_scale_f)
    dhp = _pool_window("pool_window_bwd", d_mixed, group_w, True, F32)
    g, d_pool_norm = _rmsnorm_bwd("pool_norm_bwd", x3, pool_norm_f, dhp, g)
    g = xattn_ffn_bwd(0, g)
    d_wo = _mm_tn("attn_o_dw", o_att, g, F32)
    do_hm = _heads_major(_mm_nt("attn_o_dx", g, w_o, BF16), S)
    dq_hm, dk_hm, dv_hm = _attn_bwd("attn_bwd", q_hm, k_hm, v_hm, o_hm, lse, do_hm)
    d_qkr = _heads_minor(jnp.concatenate([dq_hm, dk_hm, dv_hm], axis=0))
    d_qkv, d_gain = _qk_rope_bwd("qk_rope_bwd", d_qkr, qkv, qk_gain, qk_scale, cos, sin, bd, n_rot)
    d_wqkv = _mm_tn("attn_qkv_dw", h0, d_qkv, F32)
    dh0 = _mm_nt("attn_qkv_dx", d_qkv, w_qkv, F32)
    grad_x, d_attn_norm = _rmsnorm_bwd("attn_norm_bwd", xs, attn_norm, dh0, g)

    bufs = [d_wqkv.reshape(D, N_DEV, -1).transpose(1, 0, 2), d_wo.reshape(N_DEV, -1, D),
            d_pool_w.reshape(n_groups, N_DEV, -1, group_w).transpose(1, 0, 2, 3).reshape(N_DEV, -1, group_w)]
    bufs += [d.reshape(N_DEV, -1, D) for d in d_xq] + d_xkv + [d.reshape(N_DEV, -1, D) for d in d_xo]
    bufs += d_up + [d.reshape(N_DEV, -1, D) for d in d_down]
    g4s = [b.reshape((N_DEV // 2, 2) + b.shape[1:]) for b in bufs]
    r1s = _swap_with_sibling("reduce_swap_sibling", g4s)
    sums = [_add_sibling(f"reduce_add_sibling{i}", g4, r1, pos) for i, (g4, r1) in enumerate(zip(g4s, r1s))]
    r2s = _swap_with_chips("reduce_swap_chips", [tb for tb, _ in sums])
    red = iter([_add_chips(f"reduce_add_chips{i}", own, r2) for i, ((_, own), r2) in enumerate(zip(sums, r2s))])
    G['attn_w_qkv'], G['attn_w_o'] = next(red)[None], next(red)[None]
    G['pool_w'] = next(red).reshape(pool_w.shape)
    G['xattn_w_q'] = jnp.stack([next(red) for l in layers])
    G['xattn_w_kv'] = jnp.stack([next(red) for l in layers])
    G['xattn_w_o'] = jnp.stack([next(red) for l in layers])
    G['ffn_w_up'] = jnp.stack([next(red) for l in layers])
    G['ffn_w_down'] = jnp.stack([next(red) for l in layers])

    hq = n_heads * HEAD_DIM
    small_g = {'attn_norm': d_attn_norm, 'attn_q_gain': d_gain[0, :hq].reshape(n_heads, HEAD_DIM).sum(0),
               'attn_k_gain': d_gain[0, hq:hq + N_KV_HEADS * HEAD_DIM].reshape(N_KV_HEADS, HEAD_DIM).sum(0),
               'pool_norm': d_pool_norm, 'pool_scale': d_pool_scale,
               'xattn_norm': jnp.concatenate(d_xn), 'mem_norm': jnp.concatenate(d_mn), 'ffn_norm': jnp.concatenate(d_fn),
               'ffn_conv_w': jnp.stack(d_cw, axis=1), 'ffn_conv_b': jnp.stack(d_cb)}
    order = list(small_g)
    flat = jnp.concatenate([small_g[n].reshape(-1) for n in order] + [G['final_norm']])
    ar_rows = _round_up(-(-flat.size // PACK_W), 8)
    flat = jnp.pad(flat, (0, ar_rows * PACK_W - flat.size)).reshape(ar_rows, PACK_W)
    summed = _sum_slots("allreduce_sum", _allgather_small("allreduce_gather", flat)).reshape(-1)
    at = 0
    for n in order + ['final_norm']:
        size = G['final_norm'].size if n == 'final_norm' else small_g[n].size
        piece = summed[at:at + size]
        at += size
        if n in ('pool_norm', 'pool_scale'):
            piece = lax.dynamic_slice(piece, (dev * d_sh,), (d_sh,))
        elif n == 'ffn_conv_w':
            piece = lax.dynamic_index_in_dim(piece.reshape(N_DEV, n_layers, 3, n_up), dev, 0, keepdims=False)
        G[n] = piece.reshape(W[n].shape)

    deltas, new_m, new_v = [], [], []
    for n in names:
        d, nm, nv = _adamw(f"adamw_{n}", W[n], G[n], Mo[n], Vo[n])
        deltas.append(d)
        new_m.append(nm)
        new_v.append(nv)
    return (loss, grad_x[None], *[G[n] for n in names], *deltas, *new_m, *new_v)
```

```python
import jax
import jax.numpy as jnp
from jax import lax
from jax.experimental import pallas as pl
from jax.experimental.pallas import tpu as pltpu

F32 = jnp.float32
BF16 = jnp.bfloat16
MESH = pl.DeviceIdType.MESH

N_DEV = 8
EPS = 1e-6
HEAD_DIM = 64
N_KV_HEADS = 4
X_HEADS = 4
GRID_W = 64
ROPE_THETA = 10000.0
ROPE_PAIRS = HEAD_DIM // 4
POOL_WINDOWS = (2, 4, 8, 16)
POOL_PAD = 16
KEY_CHUNK = 1024
MM_ROWS = 1024
LANES = 128
PACK_W = 1024
ADAM_LR, ADAM_B1, ADAM_B2, ADAM_EPS, ADAM_WD, ADAM_STEP = 0.001, 0.9, 0.999, 1e-08, 0.01, 10

_NN = (((1,), (0,)), ((), ()))
_NT = (((1,), (1,)), ((), ()))
_TN = (((0,), (0,)), ((), ()))


def _pc(body, *, name, **kw):
    return pl.pallas_call(body, name=name, **kw)


def _sem(*kinds):
    return pltpu.CompilerParams(dimension_semantics=kinds)


def _tile(n, pref, mult):
    best = None
    for t in range(mult, min(n, pref) + 1, mult):
        if n % t == 0:
            best = t
    return n if best is None else best


def _round_up(n, m):
    return (n + m - 1) // m * m


def _mm_call(name, a, b, dims, grid, a_spec, b_spec, o_spec, out_shape, kaxis, res=None, res_spec=None):
    nk = grid[kaxis]
    acc_shape = tuple(d for d in o_spec.block_shape if d is not None)
    in_place = out_shape.dtype == F32 and res is None
    use_scratch = nk > 1 and not in_place

    def body(*refs):
        refs = list(refs)
        acc = refs.pop() if use_scratch else None
        a_ref, b_ref = refs[:2]
        r_ref = refs[2] if res is not None else None
        o_ref = refs[-1]
        prod = lax.dot_general(a_ref[...].astype(BF16), b_ref[...].astype(BF16), dims, preferred_element_type=F32)
        if nk == 1:
            if r_ref is not None:
                prod = prod + r_ref[...]
            o_ref[...] = prod.astype(o_ref.dtype)
            return
        k = pl.program_id(kaxis)
        tgt = o_ref if in_place else acc

        @pl.when(k == 0)
        def _():
            tgt[...] = prod

        @pl.when(k > 0)
        def _():
            tgt[...] += prod

        if not in_place:
            @pl.when(k == nk - 1)
            def _():
                r = acc[...]
                if r_ref is not None:
                    r = r + r_ref[...]
                o_ref[...] = r.astype(o_ref.dtype)

    sem = tuple("arbitrary" if ax == kaxis else "parallel" for ax in range(len(grid)))
    ins = [a, b] if res is None else [a, b, res]
    specs = [a_spec, b_spec] if res is None else [a_spec, b_spec, res_spec]
    return _pc(body, name=name, grid=grid, in_specs=specs, out_specs=o_spec, out_shape=out_shape,
               scratch_shapes=[pltpu.VMEM(acc_shape, F32)] if use_scratch else [],
               compiler_params=_sem(*sem))(*ins)


def _mm_nn(name, a, b, out_dtype, res=None):
    M, K = a.shape
    N = b.shape[1]
    tm, tn, tk = _tile(M, MM_ROWS, 16), _tile(N, 1024, LANES), _tile(K, 1024, LANES)
    return _mm_call(name, a, b, _NN, (M // tm, N // tn, K // tk),
                    pl.BlockSpec((tm, tk), lambda i, j, k: (i, k)),
                    pl.BlockSpec((tk, tn), lambda i, j, k: (k, j)),
                    pl.BlockSpec((tm, tn), lambda i, j, k: (i, j)),
                    jax.ShapeDtypeStruct((M, N), out_dtype), 2, res,
                    pl.BlockSpec((tm, tn), lambda i, j, k: (i, j)))


def _mm_nt(name, a, b, out_dtype):
    M, K = a.shape
    N = b.shape[0]
    tm, tn, tk = _tile(M, MM_ROWS, 16), _tile(N, 1024, LANES), _tile(K, 1024, LANES)
    return _mm_call(name, a, b, _NT, (M // tm, N // tn, K // tk),
                    pl.BlockSpec((tm, tk), lambda i, j, k: (i, k)),
                    pl.BlockSpec((tn, tk), lambda i, j, k: (j, k)),
                    pl.BlockSpec((tm, tn), lambda i, j, k: (i, j)),
                    jax.ShapeDtypeStruct((M, N), out_dtype), 2)


def _mm_tn(name, a, b, out_dtype):
    R, M = a.shape
    N = b.shape[1]
    tm, tn, tr = _tile(M, 1024, LANES), _tile(N, 1024, LANES), _tile(R, MM_ROWS, 16)
    return _mm_call(name, a, b, _TN, (M // tm, N // tn, R // tr),
                    pl.BlockSpec((tr, tm), lambda i, j, k: (k, i)),
                    pl.BlockSpec((tr, tn), lambda i, j, k: (k, j)),
                    pl.BlockSpec((tm, tn), lambda i, j, k: (i, j)),
                    jax.ShapeDtypeStruct((M, N), out_dtype), 2)


def _mm_nn_bs(name, a, b, out_dtype, layer=0):
    M, K = a.shape
    J, _, n = b.shape
    tm, tk = _tile(M, MM_ROWS, 16), _tile(K, 1024, LANES)
    first = layer * (K // tk)
    return _mm_call(name, a, b, _NN, (J, M // tm, K // tk),
                    pl.BlockSpec((tm, tk), lambda j, i, k: (i, k)),
                    pl.BlockSpec((None, tk, n), lambda j, i, k: (j, first + k, 0)),
                    pl.BlockSpec((None, tm, n), lambda j, i, k: (j, i, 0)),
                    jax.ShapeDtypeStruct((J, M, n), out_dtype), 2)


def _mm_nt_bs(name, a, b, J, out_dtype):
    M, K = a.shape
    n = b.shape[0] // J
    tm, tk = _tile(M, MM_ROWS, 16), _tile(K, 1024, LANES)
    return _mm_call(name, a, b, _NT, (J, M // tm, K // tk),
                    pl.BlockSpec((tm, tk), lambda j, i, k: (i, k)),
                    pl.BlockSpec((n, tk), lambda j, i, k: (j, k)),
                    pl.BlockSpec((None, tm, n), lambda j, i, k: (j, i, 0)),
                    jax.ShapeDtypeStruct((J, M, n), out_dtype), 2)


def _mm_nn_as(name, a, b, out_dtype, res):
    J, M, n = a.shape
    N = b.shape[1]
    tm, tn = _tile(M, MM_ROWS, 16), _tile(N, 1024, LANES)
    return _mm_call(name, a, b, _NN, (M // tm, N // tn, J),
                    pl.BlockSpec((None, tm, n), lambda i, j, k: (k, i, 0)),
                    pl.BlockSpec((n, tn), lambda i, j, k: (k, j)),
                    pl.BlockSpec((tm, tn), lambda i, j, k: (i, j)),
                    jax.ShapeDtypeStruct((M, N), out_dtype), 2, res,
                    pl.BlockSpec((tm, tn), lambda i, j, k: (i, j)))


def _mm_nt_abs(name, a, b, N, out_dtype, layer=0):
    J, M, n = a.shape
    tm, tn = _tile(M, MM_ROWS, 16), _tile(N, 1024, LANES)
    first = layer * (N // tn)
    return _mm_call(name, a, b, _NT, (M // tm, N // tn, J),
                    pl.BlockSpec((None, tm, n), lambda i, j, k: (k, i, 0)),
                    pl.BlockSpec((None, tn, n), lambda i, j, k: (k, first + j, 0)),
                    pl.BlockSpec((tm, tn), lambda i, j, k: (i, j)),
                    jax.ShapeDtypeStruct((M, N), out_dtype), 2)


def _mm_tn_bs(name, a, b, out_dtype):
    R, M = a.shape
    J, _, n = b.shape
    tm, tr = _tile(M, 1024, LANES), _tile(R, MM_ROWS, 16)
    return _mm_call(name, a, b, _TN, (J, M // tm, R // tr),
                    pl.BlockSpec((tr, tm), lambda j, i, k: (k, i)),
                    pl.BlockSpec((None, tr, n), lambda j, i, k: (j, k, 0)),
                    pl.BlockSpec((None, tm, n), lambda j, i, k: (j, i, 0)),
                    jax.ShapeDtypeStruct((J, M, n), out_dtype), 2)


def _mm_tn_as(name, a, b, out_dtype):
    J, R, n = a.shape
    N = b.shape[1]
    tn, tr = _tile(N, 1024, LANES), _tile(R, MM_ROWS, 16)
    return _mm_call(name, a, b, _TN, (J, N // tn, R // tr),
                    pl.BlockSpec((None, tr, n), lambda j, jn, k: (j, k, 0)),
                    pl.BlockSpec((tr, tn), lambda j, jn, k: (k, jn)),
                    pl.BlockSpec((n, tn), lambda j, jn, k: (j, jn)),
                    jax.ShapeDtypeStruct((J * n, N), out_dtype), 2)


def _rmsnorm(name, x, g, out_dtype):
    R, D = x.shape
    tm = _tile(R, 512, 16)

    def body(x_ref, g_ref, o_ref):
        xv = x_ref[...]
        r = lax.rsqrt(jnp.mean(xv * xv, axis=-1, keepdims=True) + EPS)
        o_ref[...] = (xv * r * g_ref[...]).astype(o_ref.dtype)

    return _pc(body, name=name, grid=(R // tm,),
               in_specs=[pl.BlockSpec((tm, D), lambda i: (i, 0)), pl.BlockSpec((1, D), lambda i: (0, 0))],
               out_specs=pl.BlockSpec((tm, D), lambda i: (i, 0)),
               out_shape=jax.ShapeDtypeStruct((R, D), out_dtype), compiler_params=_sem("parallel"))(x, g)


def _rmsnorm_bwd(name, x, g, dh, dres=None):
    R, D = x.shape
    tm = _tile(R, 512, 16)

    def body(*refs):
        if dres is None:
            x_ref, g_ref, dh_ref, dx_ref, dg_ref = refs
            dres_ref = None
        else:
            x_ref, g_ref, dh_ref, dres_ref, dx_ref, dg_ref = refs
        xv = x_ref[...]
        r = lax.rsqrt(jnp.mean(xv * xv, axis=-1, keepdims=True) + EPS)
        xh = xv * r
        dhv = dh_ref[...].astype(F32)

        @pl.when(pl.program_id(0) == 0)
        def _():
            dg_ref[...] = jnp.zeros_like(dg_ref)

        dg_ref[...] += jnp.sum(dhv * xh, axis=0, keepdims=True)
        dxh = dhv * g_ref[...]
        dx = r * (dxh - xh * jnp.mean(dxh * xh, axis=-1, keepdims=True))
        if dres_ref is not None:
            dx = dx + dres_ref[...]
        dx_ref[...] = dx

    row = pl.BlockSpec((tm, D), lambda i: (i, 0))
    vec = pl.BlockSpec((1, D), lambda i: (0, 0))
    ins = [x, g, dh] + ([] if dres is None else [dres])
    specs = [row, vec, row] + ([] if dres is None else [row])
    return _pc(body, name=name, grid=(R // tm,), in_specs=specs, out_specs=(row, vec),
               out_shape=(jax.ShapeDtypeStruct((R, D), F32), jax.ShapeDtypeStruct((1, D), F32)),
               compiler_params=_sem("arbitrary"))(*ins)


def _loss_head(name, x, g, tgt):
    R, D = x.shape
    tm = _tile(R, 512, 16)

    def body(x_ref, g_ref, t_ref, dx_ref, dg_ref, l_ref):
        xv = x_ref[...]
        r = lax.rsqrt(jnp.mean(xv * xv, axis=-1, keepdims=True) + EPS)
        xh = xv * r
        err = xh * g_ref[...] - t_ref[...]

        @pl.when(pl.program_id(0) == 0)
        def _():
            dg_ref[...] = jnp.zeros_like(dg_ref)
            l_ref[...] = jnp.zeros_like(l_ref)

        l_ref[...] += jnp.sum(err * err, axis=0, keepdims=True)
        dy = err * (1.0 / D)
        dg_ref[...] += jnp.sum(dy * xh, axis=0, keepdims=True)
        dxh = dy * g_ref[...]
        dx_ref[...] = r * (dxh - xh * jnp.mean(dxh * xh, axis=-1, keepdims=True))

    row = pl.BlockSpec((tm, D), lambda i: (i, 0))
    vec = pl.BlockSpec((1, D), lambda i: (0, 0))
    return _pc(body, name=name, grid=(R // tm,), in_specs=[row, vec, row], out_specs=(row, vec, vec),
               out_shape=(jax.ShapeDtypeStruct((R, D), F32), jax.ShapeDtypeStruct((1, D), F32),
                          jax.ShapeDtypeStruct((1, D), F32)),
               compiler_params=_sem("arbitrary"))(x, g, tgt)


def _rope_tables(S):
    n_rows = S // GRID_W
    row = jnp.repeat(jnp.arange(n_rows, dtype=F32), GRID_W)
    col = jnp.tile(jnp.arange(GRID_W, dtype=F32), n_rows)
    inv_freq = ROPE_THETA ** (-jnp.arange(ROPE_PAIRS, dtype=F32) / ROPE_PAIRS)
    ang = jnp.stack([row[:, None] * inv_freq, col[:, None] * inv_freq], axis=1)
    cos, sin = jnp.cos(ang), jnp.sin(ang)
    c = jnp.broadcast_to(cos[:, :, None, :], (S, 2, 2, ROPE_PAIRS)).reshape(S, HEAD_DIM)
    s = jnp.stack([-sin, sin], axis=2).reshape(S, HEAD_DIM)
    reps = LANES // HEAD_DIM
    return jnp.tile(c, (1, reps)), jnp.tile(s, (1, reps))


def _head_mean_matrix():
    h = jnp.arange(LANES) // HEAD_DIM
    m = jnp.where(h[:, None] == h[None, :], 1.0 / HEAD_DIM, 0.0).astype(BF16)
    return jnp.concatenate([m, m], axis=0)


def _head_mean(v, bd):
    hi = v.astype(BF16)
    lo = (v - hi.astype(F32)).astype(BF16)
    return jnp.dot(jnp.concatenate([hi, lo], axis=1), bd, preferred_element_type=F32)


def _swap_halves(y):
    lane = lax.broadcasted_iota(jnp.int32, y.shape, 1)
    return jnp.where(lane % 32 < 16, pltpu.roll(y, LANES - 16, 1), pltpu.roll(y, 16, 1))


def _qk_rope(name, qkv, gain, scale, cos, sin, bd, n_rot):
    S, W = qkv.shape
    tm = _tile(S, 2048, 16)

    def body(x_ref, g_ref, s_ref, c_ref, sn_ref, bd_ref, o_ref):
        j = pl.program_id(1)
        xv = x_ref[...]

        @pl.when(j < n_rot)
        def _():
            ms = _head_mean(xv * xv, bd_ref[...])
            y = xv * lax.rsqrt(ms + EPS) * g_ref[...] * s_ref[...]
            o_ref[...] = (y * c_ref[...] + _swap_halves(y) * sn_ref[...]).astype(BF16)

        @pl.when(j >= n_rot)
        def _():
            o_ref[...] = xv.astype(BF16)

    blk = pl.BlockSpec((tm, LANES), lambda i, j: (i, j))
    vec = pl.BlockSpec((1, LANES), lambda i, j: (0, j))
    tab = pl.BlockSpec((tm, LANES), lambda i, j: (i, 0))
    return _pc(body, name=name, grid=(S // tm, W // LANES),
               in_specs=[blk, vec, vec, tab, tab, pl.BlockSpec((2 * LANES, LANES), lambda i, j: (0, 0))],
               out_specs=blk, out_shape=jax.ShapeDtypeStruct((S, W), BF16),
               compiler_params=_sem("parallel", "parallel"))(qkv, gain, scale, cos, sin, bd)


def _qk_rope_bwd(name, d_out, qkv, gain, scale, cos, sin, bd, n_rot):
    S, W = qkv.shape
    tm = _tile(S, 2048, 16)

    def body(d_ref, x_ref, g_ref, s_ref, c_ref, sn_ref, bd_ref, dx_ref, dg_ref):
        j, i = pl.program_id(0), pl.program_id(1)
        dv = d_ref[...]

        @pl.when(i == 0)
        def _():
            dg_ref[...] = jnp.zeros_like(dg_ref)

        @pl.when(j < n_rot)
        def _():
            xv = x_ref[...]
            ms = _head_mean(xv * xv, bd_ref[...])
            r = lax.rsqrt(ms + EPS)
            z = xv * r
            dy = (dv * c_ref[...] - _swap_halves(dv) * sn_ref[...]) * s_ref[...]
            dg_ref[...] += jnp.sum(dy * z, axis=0, keepdims=True)
            dz = dy * g_ref[...]
            mz = _head_mean(dz * z, bd_ref[...])
            dx_ref[...] = (r * (dz - z * mz)).astype(BF16)

        @pl.when(j >= n_rot)
        def _():
            dx_ref[...] = dv.astype(BF16)

    blk = pl.BlockSpec((tm, LANES), lambda j, i: (i, j))
    vec = pl.BlockSpec((1, LANES), lambda j, i: (0, j))
    tab = pl.BlockSpec((tm, LANES), lambda j, i: (i, 0))
    return _pc(body, name=name, grid=(W // LANES, S // tm),
               in_specs=[blk, blk, vec, vec, tab, tab, pl.BlockSpec((2 * LANES, LANES), lambda j, i: (0, 0))],
               out_specs=(blk, vec),
               out_shape=(jax.ShapeDtypeStruct((S, W), BF16), jax.ShapeDtypeStruct((1, W), F32)),
               compiler_params=_sem("parallel", "arbitrary"))(d_out, qkv, gain, scale, cos, sin, bd)


def _softmax_rows(s):
    m = jnp.max(s, axis=-1, keepdims=True)
    p = jnp.exp(s - m)
    return p, jnp.sum(p, axis=-1, keepdims=True)


def _attn_fwd(name, q, k, v):
    H, S, dh = q.shape
    G = H // N_KV_HEADS
    tq = _tile(S, 128, 16)

    kc = _tile(S, KEY_CHUNK, LANES)
    R = G * tq

    def body(q_ref, k_ref, v_ref, o_ref, lse_ref):
        q = q_ref[...].reshape(R, dh)
        m = jnp.full((R, 1), -1e30, F32)
        l = jnp.zeros((R, 1), F32)
        acc = jnp.zeros((R, dh), F32)
        for c in range(S // kc):
            rows = slice(c * kc, (c + 1) * kc)
            s = lax.dot_general(q, k_ref[rows, :], _NT, preferred_element_type=F32)
            m_new = jnp.maximum(m, jnp.max(s, axis=-1, keepdims=True))
            alpha = jnp.exp(m - m_new)
            p = jnp.exp(s - m_new)
            l = alpha * l + jnp.sum(p, axis=-1, keepdims=True)
            acc = alpha * acc + jnp.dot(p.astype(BF16), v_ref[rows, :], preferred_element_type=F32)
            m = m_new
        o_ref[...] = (acc / l).astype(BF16).reshape(G, tq, dh)
        lse_ref[...] = (m + jnp.log(l)).reshape(G, tq, 1)

    qs = pl.BlockSpec((G, tq, dh), lambda kv, i: (kv, i, 0))
    ls = pl.BlockSpec((G, tq, 1), lambda kv, i: (kv, i, 0))
    ks = pl.BlockSpec((None, S, dh), lambda kv, i: (kv, 0, 0))
    return _pc(body, name=name, grid=(N_KV_HEADS, S // tq), in_specs=[qs, ks, ks], out_specs=(qs, ls),
               out_shape=(jax.ShapeDtypeStruct((H, S, dh), BF16), jax.ShapeDtypeStruct((H, S, 1), F32)),
               compiler_params=_sem("parallel", "parallel"))(q, k, v)


def _attn_bwd(name, q, k, v, o, lse, do):
    H, S, dh = q.shape
    G = H // N_KV_HEADS
    tq = _tile(S, 128, 16)
    kc = _tile(S, KEY_CHUNK, LANES)
    R = G * tq

    def body(q_ref, k_ref, v_ref, o_ref, lse_ref, do_ref, dq_ref, dk_ref, dv_ref):
        @pl.when(pl.program_id(1) == 0)
        def _():
            dk_ref[...] = jnp.zeros_like(dk_ref)
            dv_ref[...] = jnp.zeros_like(dv_ref)

        qq, dd = q_ref[...].reshape(R, dh), do_ref[...].reshape(R, dh)
        delta = jnp.sum(dd.astype(F32) * o_ref[...].reshape(R, dh).astype(F32), axis=-1, keepdims=True)
        lse = lse_ref[...].reshape(R, 1)
        dq = jnp.zeros((R, dh), F32)
        for c in range(S // kc):
            rows = slice(c * kc, (c + 1) * kc)
            kk, vv = k_ref[rows, :], v_ref[rows, :]
            p = jnp.exp(lax.dot_general(qq, kk, _NT, preferred_element_type=F32) - lse)
            dv_ref[rows, :] += lax.dot_general(p.astype(BF16), dd, _TN, preferred_element_type=F32)
            dp = lax.dot_general(dd, vv, _NT, preferred_element_type=F32)
            ds = (p * (dp - delta)).astype(BF16)
            dq = dq + jnp.dot(ds, kk, preferred_element_type=F32)
            dk_ref[rows, :] += lax.dot_general(ds, qq, _TN, preferred_element_type=F32)
        dq_ref[...] = dq.reshape(G, tq, dh)

    qs = pl.BlockSpec((G, tq, dh), lambda kv, i: (kv, i, 0))
    ls = pl.BlockSpec((G, tq, 1), lambda kv, i: (kv, i, 0))
    ks = pl.BlockSpec((None, S, dh), lambda kv, i: (kv, 0, 0))
    return _pc(body, name=name, grid=(N_KV_HEADS, S // tq), in_specs=[qs, ks, ks, qs, ls, qs],
               out_specs=(qs, ks, ks),
               out_shape=(jax.ShapeDtypeStruct((H, S, dh), F32), jax.ShapeDtypeStruct((N_KV_HEADS, S, dh), F32),
                          jax.ShapeDtypeStruct((N_KV_HEADS, S, dh), F32)),
               compiler_params=_sem("parallel", "arbitrary"))(q, k, v, o, lse, do)


def _xattn_fwd(name, q, kv):
    S, D = q.shape
    _, M, dh = kv.shape
    scale = dh ** -0.5
    tq = _tile(S, 256, 16)

    def body(q_ref, kv_ref, o_ref):
        for h in range(X_HEADS):
            lo, hi = h * dh, (h + 1) * dh
            s = lax.dot_general(q_ref[:, lo:hi], kv_ref[h], _NT, preferred_element_type=F32) * scale
            p, l = _softmax_rows(s)
            o = jnp.dot(p.astype(BF16), kv_ref[X_HEADS + h], preferred_element_type=F32)
            o_ref[:, lo:hi] = (o / l).astype(BF16)

    row = pl.BlockSpec((tq, D), lambda i: (i, 0))
    return _pc(body, name=name, grid=(S // tq,),
               in_specs=[row, pl.BlockSpec((2 * X_HEADS, M, dh), lambda i: (0, 0, 0))],
               out_specs=row, out_shape=jax.ShapeDtypeStruct((S, D), BF16),
               compiler_params=_sem("parallel"))(q, kv)


def _xattn_bwd(name, q, kv, do):
    S, D = q.shape
    _, M, dh = kv.shape
    scale = dh ** -0.5
    tq = _tile(S, 256, 16)

    def body(q_ref, kv_ref, do_ref, dq_ref, dkv_ref):
        @pl.when(pl.program_id(0) == 0)
        def _():
            dkv_ref[...] = jnp.zeros_like(dkv_ref)

        for h in range(X_HEADS):
            lo, hi = h * dh, (h + 1) * dh
            qh, kh, vh, doh = q_ref[:, lo:hi], kv_ref[h], kv_ref[X_HEADS + h], do_ref[:, lo:hi]
            s = lax.dot_general(qh, kh, _NT, preferred_element_type=F32) * scale
            p, l = _softmax_rows(s)
            pn = p / l
            dkv_ref[X_HEADS + h] += lax.dot_general(pn.astype(BF16), doh, _TN, preferred_element_type=F32)
            dp = lax.dot_general(doh, vh, _NT, preferred_element_type=F32)
            ds = (pn * (dp - jnp.sum(pn * dp, axis=-1, keepdims=True)) * scale).astype(BF16)
            dq_ref[:, lo:hi] = jnp.dot(ds, kh, preferred_element_type=F32).astype(BF16)
            dkv_ref[h] += lax.dot_general(ds, qh, _TN, preferred_element_type=F32)

    row = pl.BlockSpec((tq, D), lambda i: (i, 0))
    full = pl.BlockSpec((2 * X_HEADS, M, dh), lambda i: (0, 0, 0))
    return _pc(body, name=name, grid=(S // tq,), in_specs=[row, full, row], out_specs=(row, full),
               out_shape=(jax.ShapeDtypeStruct((S, D), BF16), jax.ShapeDtypeStruct((2 * X_HEADS, M, dh), F32)),
               compiler_params=_sem("arbitrary"))(q, kv, do)


def _sigmoid(x):
    return 1.0 / (1.0 + jnp.exp(-x))


def _halo_specs(tm, n, S):
    nb = tm // 8
    last8 = S // 8 - 1
    main = pl.BlockSpec((2, None, tm, n), lambda j, i: (0, j, i, 0))
    prev = pl.BlockSpec((2, None, 8, n), lambda j, i: (0, j, jnp.maximum(i * nb - 1, 0), 0))
    nxt = pl.BlockSpec((2, None, 8, n), lambda j, i: (0, j, jnp.minimum((i + 1) * nb, last8), 0))
    return main, prev, nxt


def _ffn_act(name, u, cw, cb):
    _, J, S, n = u.shape
    tm = _tile(S, 256, 16)
    nblk = S // tm

    def body(u_ref, up_ref, un_ref, w_ref, b_ref, a_ref):
        i = pl.program_id(1)
        row = lax.broadcasted_iota(jnp.int32, (tm, n), 0)
        c = []
        for half in range(2):
            main = u_ref[half]
            before = jnp.where(i > 0, up_ref[half, 7:8, :], 0.0)
            after = jnp.where(i < nblk - 1, un_ref[half, 0:1, :], 0.0)
            um = jnp.where(row == 0, before, pltpu.roll(main, 1, 0))
            up = jnp.where(row == tm - 1, after, pltpu.roll(main, tm - 1, 0))
            w = w_ref[half]
            c.append(um * w[0:1] + main * w[1:2] + up * w[2:3] + b_ref[half])
        a_ref[...] = (c[0] * _sigmoid(c[0]) * c[1]).astype(BF16)

    main, prev, nxt = _halo_specs(tm, n, S)
    return _pc(body, name=name, grid=(J, nblk),
               in_specs=[main, prev, nxt, pl.BlockSpec((2, None, 3, n), lambda j, i: (0, j, 0, 0)),
                         pl.BlockSpec((2, None, 1, n), lambda j, i: (0, j, 0, 0))],
               out_specs=pl.BlockSpec((None, tm, n), lambda j, i: (j, i, 0)),
               out_shape=jax.ShapeDtypeStruct((J, S, n), BF16),
               compiler_params=_sem("parallel", "parallel"))(u, u, u, cw, cb)


def _ffn_act_bwd(name, u, da, cw, cb):
    _, J, S, n = u.shape
    tm = _tile(S, 256, 16)
    nblk = S // tm
    te = tm + 16
    nb = tm // 8
    last8 = S // 8 - 1

    def body(u_ref, up_ref, un_ref, da_ref, dap_ref, dan_ref, w_ref, b_ref, du_ref, st_ref):
        i = pl.program_id(1)

        @pl.when(i == 0)
        def _():
            st_ref[...] = jnp.zeros_like(st_ref)

        r = lax.broadcasted_iota(jnp.int32, (te, n), 0)
        t = i * tm - 8 + r
        valid = (t >= 0) & (t < S)
        mid = (r >= 8) & (r < tm + 8)
        da_e = jnp.where(valid, jnp.concatenate([dap_ref[...], da_ref[...], dan_ref[...]], axis=0).astype(F32), 0.0)
        ue, c = [], []
        for half in range(2):
            e = jnp.where(valid, jnp.concatenate([up_ref[half], u_ref[half], un_ref[half]], axis=0), 0.0)
            w = w_ref[half]
            ue.append((pltpu.roll(e, 1, 0), e, pltpu.roll(e, te - 1, 0)))
            c.append(ue[half][0] * w[0:1] + e * w[1:2] + ue[half][2] * w[2:3] + b_ref[half])
        sg = _sigmoid(c[0])
        dc = [jnp.where(valid, da_e * c[1] * (sg * (1.0 + c[0] * (1.0 - sg))), 0.0),
              jnp.where(valid, da_e * (c[0] * sg), 0.0)]
        r8 = lax.broadcasted_iota(jnp.int32, (8, n), 0)
        for half in range(2):
            w, d, (e_before, e, e_after) = w_ref[half], dc[half], ue[half]
            du = pltpu.roll(d, te - 1, 0) * w[0:1] + d * w[1:2] + pltpu.roll(d, 1, 0) * w[2:3]
            du_ref[half] = du[8:tm + 8].astype(BF16)
            dm = jnp.where(mid, d, 0.0)
            s0 = jnp.sum(dm * e_before, axis=0, keepdims=True)
            s1 = jnp.sum(dm * e, axis=0, keepdims=True)
            s2 = jnp.sum(dm * e_after, axis=0, keepdims=True)
            s3 = jnp.sum(dm, axis=0, keepdims=True)
            st_ref[half] += jnp.where(r8 == 0, s0, jnp.where(r8 == 1, s1, jnp.where(r8 == 2, s2,
                                      jnp.where(r8 == 3, s3, 0.0))))

    main, prev, nxt = _halo_specs(tm, n, S)
    dmain = pl.BlockSpec((None, tm, n), lambda j, i: (j, i, 0))
    dprev = pl.BlockSpec((None, 8, n), lambda j, i: (j, jnp.maximum(i * nb - 1, 0), 0))
    dnxt = pl.BlockSpec((None, 8, n), lambda j, i: (j, jnp.minimum((i + 1) * nb, last8), 0))
    return _pc(body, name=name, grid=(J, nblk),
               in_specs=[main, prev, nxt, dmain, dprev, dnxt,
                         pl.BlockSpec((2, None, 3, n), lambda j, i: (0, j, 0, 0)),
                         pl.BlockSpec((2, None, 1, n), lambda j, i: (0, j, 0, 0))],
               out_specs=(main, pl.BlockSpec((2, None, 8, n), lambda j, i: (0, j, 0, 0))),
               out_shape=(jax.ShapeDtypeStruct((2, J, S, n), BF16), jax.ShapeDtypeStruct((2, J, 8, n), F32)),
               compiler_params=_sem("parallel", "arbitrary"))(u, u, u, da, da, da, cw, cb)


def _window_count(t, w, S):
    lo = jnp.maximum(t - w // 2, 0)
    hi = jnp.minimum(t + w - w // 2, S)
    return (hi - lo).astype(F32)


def _trailing_sums(x, w):
    k = 1
    while k < w:
        x = x + pltpu.roll(x, k, 0)
        k *= 2
    return x


def _pool_window(name, h, group_w, adjoint, out_dtype):
    S, D = h.shape
    SP = S + 2 * POOL_PAD
    per_group = group_w // LANES

    def body(h_ref, o_ref, xp):
        g = pl.program_id(0) // per_group
        t = lax.broadcasted_iota(jnp.int32, (S, LANES), 0)
        xp[0:POOL_PAD, :] = jnp.zeros((POOL_PAD, LANES), F32)
        xp[S + POOL_PAD:SP, :] = jnp.zeros((POOL_PAD, LANES), F32)
        for gi, w in enumerate(POOL_WINDOWS):
            @pl.when(g == gi)
            def _():
                hv = h_ref[...]
                cnt = _window_count(t, w, S)
                xp[POOL_PAD:S + POOL_PAD, :] = hv / cnt if adjoint else hv
                ahead = w // 2 if adjoint else w // 2 - 1
                sw = _trailing_sums(xp[...], w)
                if ahead:
                    sw = pltpu.roll(sw, SP - ahead, 0)
                win = sw[POOL_PAD:S + POOL_PAD]
                o_ref[...] = ((win if adjoint else win / cnt) - hv).astype(out_dtype)

    col = pl.BlockSpec((S, LANES), lambda j: (0, j))
    return _pc(body, name=name, grid=(D // LANES,), in_specs=[col], out_specs=col,
               out_shape=jax.ShapeDtypeStruct((S, D), out_dtype),
               scratch_shapes=[pltpu.VMEM((SP, LANES), F32)], compiler_params=_sem("parallel"))(h)


def _pool_proj(name, mixed, w, scale, res):
    S, D = mixed.shape
    G, gw, _ = w.shape
    tm = _tile(S, 512, 16)

    def body(m_ref, w_ref, s_ref, r_ref, o_ref):
        for g in range(G):
            lo, hi = g * gw, (g + 1) * gw
            y = jnp.dot(m_ref[:, lo:hi], w_ref[g], preferred_element_type=F32)
            o_ref[:, lo:hi] = r_ref[:, lo:hi] + y * s_ref[:, lo:hi]

    row = pl.BlockSpec((tm, D), lambda i: (i, 0))
    return _pc(body, name=name, grid=(S // tm,),
               in_specs=[row, pl.BlockSpec((G, gw, gw), lambda i: (0, 0, 0)), pl.BlockSpec((1, D), lambda i: (0, 0)), row],
               out_specs=row, out_shape=jax.ShapeDtypeStruct((S, D), F32),
               compiler_params=_sem("parallel"))(mixed, w, scale, res)


def _pool_proj_bwd(name, dy, mixed, w, scale):
    S, D = mixed.shape
    G, gw, _ = w.shape
    tm = _tile(S, 512, 16)

    def body(dy_ref, m_ref, w_ref, s_ref, dm_ref, dw_ref, ds_ref):
        @pl.when(pl.program_id(0) == 0)
        def _():
            dw_ref[...] = jnp.zeros_like(dw_ref)
            ds_ref[...] = jnp.zeros_like(ds_ref)

        for g in range(G):
            lo, hi = g * gw, (g + 1) * gw
            mg, dyg = m_ref[:, lo:hi], dy_ref[:, lo:hi]
            y = jnp.dot(mg, w_ref[g], preferred_element_type=F32)
            ds_ref[:, lo:hi] += jnp.sum(dyg * y, axis=0, keepdims=True)
            dyp = (dyg * s_ref[:, lo:hi]).astype(BF16)
            dm_ref[:, lo:hi] = lax.dot_general(dyp, w_ref[g], _NT, preferred_element_type=F32)
            dw_ref[g] += lax.dot_general(mg, dyp, _TN, preferred_element_type=F32)

    row = pl.BlockSpec((tm, D), lambda i: (i, 0))
    wsp = pl.BlockSpec((G, gw, gw), lambda i: (0, 0, 0))
    vec = pl.BlockSpec((1, D), lambda i: (0, 0))
    return _pc(body, name=name, grid=(S // tm,), in_specs=[row, row, wsp, vec], out_specs=(row, wsp, vec),
               out_shape=(jax.ShapeDtypeStruct((S, D), F32), jax.ShapeDtypeStruct((G, gw, gw), F32),
                          jax.ShapeDtypeStruct((1, D), F32)),
               compiler_params=_sem("arbitrary"))(dy, mixed, w, scale)


def _adamw(name, w, g, m, v):
    shape = w.shape
    C = shape[-1]
    R = w.size // C
    tm = _tile(R, 512, 8)

    def body(w_ref, g_ref, m_ref, v_ref, d_ref, nm_ref, nv_ref):
        gv = g_ref[...]
        nm = ADAM_B1 * m_ref[...] + (1.0 - ADAM_B1) * gv
        nv = ADAM_B2 * v_ref[...] + (1.0 - ADAM_B2) * (gv * gv)
        m_hat = nm / (1.0 - ADAM_B1 ** ADAM_STEP)
        v_hat = nv / (1.0 - ADAM_B2 ** ADAM_STEP)
        d_ref[...] = -ADAM_LR * (m_hat / (jnp.sqrt(v_hat) + ADAM_EPS) + ADAM_WD * w_ref[...])
        nm_ref[...] = nm
        nv_ref[...] = nv

    blk = pl.BlockSpec((tm, C), lambda i: (i, 0))
    sd = jax.ShapeDtypeStruct((R, C), F32)
    outs = _pc(body, name=name, grid=(R // tm,), in_specs=[blk] * 4, out_specs=(blk,) * 3, out_shape=(sd,) * 3,
               compiler_params=_sem("parallel"))(*(a.reshape(R, C) for a in (w, g, m, v)))
    return tuple(o.reshape(shape) for o in outs)


def _position():
    return lax.axis_index("x"), lax.axis_index("y"), lax.axis_index("c")


def _flip(v, bit):
    return 1 - v if bit else v


def _allgather_small(name, v):
    R, W = v.shape

    def body(v_ref, out_ref, send_sems, recv_sems):
        x, y, c = _position()
        me = 4 * x + 2 * y + c
        out_ref[me] = v_ref[...]
        sends = []
        for k in range(1, N_DEV):
            peer = (_flip(x, k & 4), _flip(y, k & 2), _flip(c, k & 1))
            cp = pltpu.make_async_remote_copy(src_ref=v_ref, dst_ref=out_ref.at[me], send_sem=send_sems.at[k - 1],
                                              recv_sem=recv_sems.at[k - 1], device_id=peer, device_id_type=MESH)
            cp.start()
            sends.append(cp)
        for k in range(1, N_DEV):
            peer = (_flip(x, k & 4), _flip(y, k & 2), _flip(c, k & 1))
            slot = 4 * peer[0] + 2 * peer[1] + peer[2]
            pltpu.make_async_remote_copy(src_ref=v_ref, dst_ref=out_ref.at[slot], send_sem=send_sems.at[k - 1],
                                         recv_sem=recv_sems.at[k - 1], device_id=peer, device_id_type=MESH).wait_recv()
        for cp in sends:
            cp.wait_send()

    vm = pl.BlockSpec(memory_space=pltpu.VMEM)
    return _pc(body, name=name, in_specs=[vm], out_specs=vm, out_shape=jax.ShapeDtypeStruct((N_DEV, R, W), F32),
               scratch_shapes=[pltpu.SemaphoreType.DMA((N_DEV - 1,)), pltpu.SemaphoreType.DMA((N_DEV - 1,))])(v)


def _sum_slots(name, a):
    n, R, W = a.shape

    def body(a_ref, o_ref):
        acc = a_ref[0]
        for s in range(1, n):
            acc = acc + a_ref[s]
        o_ref[...] = acc

    return _pc(body, name=name, grid=(1,), in_specs=[pl.BlockSpec((n, R, W), lambda i: (0, 0, 0))],
               out_specs=pl.BlockSpec((R, W), lambda i: (0, 0)), out_shape=jax.ShapeDtypeStruct((R, W), F32))(a)


def _allgather_blocks(name, blocks):
    n = len(blocks)

    def body(*refs):
        b_refs, out_refs, token = refs[:n], refs[n:2 * n], refs[2 * n]
        send_sems, recv_sems, local_sems = refs[2 * n + 1:]
        token[...] = jnp.zeros_like(token)
        x, y, c = _position()
        me, sibling = (x, y, c), (x, y, 1 - c)
        chips = [(1 - x, y), (x, 1 - y), (1 - x, 1 - y)]

        def slot(i, px, py, pc):
            return out_refs[i].at[4 * px + 2 * py + pc]

        def copy(i, k, block, to, src=None):
            return pltpu.make_async_remote_copy(src_ref=slot(i, *block) if src is None else src, dst_ref=slot(i, *block),
                                                send_sem=send_sems.at[k, i], recv_sem=recv_sems.at[k, i],
                                                device_id=to, device_id_type=MESH)

        mine = [pltpu.make_async_copy(b_refs[i], slot(i, *me), local_sems.at[i]) for i in range(n)]
        first = [copy(i, 1 + j, me, (*chip, c), src=b_refs[i]) for i in range(n) for j, chip in enumerate(chips)]
        first += [copy(i, 0, me, sibling, src=b_refs[i]) for i in range(n)]
        for cp in mine + first:
            cp.start()
        passed = []
        for j, chip in enumerate(chips):
            for i in range(n):
                copy(i, 1 + j, (*chip, c), me).wait_recv()
                passed.append(copy(i, 4 + j, (*chip, c), sibling))
                passed[-1].start()
        for i in range(n):
            copy(i, 0, sibling, me).wait_recv()
        for j, chip in enumerate(chips):
            for i in range(n):
                copy(i, 4 + j, (*chip, 1 - c), me).wait_recv()
        for cp in first + passed:
            cp.wait_send()
        for cp in mine:
            cp.wait()

    hbm = pl.BlockSpec(memory_space=pl.ANY)
    return _pc(body, name=name, in_specs=[hbm] * n, out_specs=[hbm] * n + [pl.BlockSpec(memory_space=pltpu.VMEM)],
               out_shape=[jax.ShapeDtypeStruct((N_DEV,) + b.shape, b.dtype) for b in blocks]
               + [jax.ShapeDtypeStruct((8, LANES), F32)],
               scratch_shapes=[pltpu.SemaphoreType.DMA((7, n)), pltpu.SemaphoreType.DMA((7, n)),
                               pltpu.SemaphoreType.DMA((n,))])(*blocks)


def _swap_with_sibling(name, gs):
    n = len(gs)
    n_chips = gs[0].shape[0]

    def body(*refs):
        g_refs, r_refs = refs[:n], refs[n:2 * n]
        send_sems, recv_sems = refs[2 * n:]
        x, y, c = _position()
        cps = [pltpu.make_async_remote_copy(src_ref=g_refs[i].at[k, 1 - c], dst_ref=r_refs[i].at[k],
                                            send_sem=send_sems.at[k, i], recv_sem=recv_sems.at[k, i],
                                            device_id=(x, y, 1 - c), device_id_type=MESH)
               for i in range(n) for k in range(n_chips)]
        for cp in cps:
            cp.start()
        for cp in cps:
            cp.wait()

    hbm = pl.BlockSpec(memory_space=pl.ANY)
    return _pc(body, name=name, in_specs=[hbm] * n, out_specs=[hbm] * n,
               out_shape=[jax.ShapeDtypeStruct((n_chips,) + g.shape[2:], F32) for g in gs],
               scratch_shapes=[pltpu.SemaphoreType.DMA((n_chips, n)), pltpu.SemaphoreType.DMA((n_chips, n))])(*gs)


def _add_sibling(name, g4, r1, pos):
    n, _, L, W = g4.shape
    tl = _tile(L, 512, 16)

    def body(pos_ref, g_ref, r_ref, tb_ref, own_ref):
        t = g_ref[...] + r_ref[...]
        tb_ref[...] = t.astype(BF16)

        @pl.when(pl.program_id(1) == pos_ref[1])
        def _():
            own_ref[...] = t

    gs = pltpu.PrefetchScalarGridSpec(
        num_scalar_prefetch=1, grid=(L // tl, n),
        in_specs=[pl.BlockSpec((None, None, tl, W), lambda i, k, p: (k, p[0], i, 0)),
                  pl.BlockSpec((None, tl, W), lambda i, k, p: (k, i, 0))],
        out_specs=(pl.BlockSpec((None, tl, W), lambda i, k, p: (k, i, 0)),
                   pl.BlockSpec((tl, W), lambda i, k, p: (i, 0))))
    return _pc(body, name=name, grid_spec=gs,
               out_shape=(jax.ShapeDtypeStruct((n, L, W), BF16), jax.ShapeDtypeStruct((L, W), F32)),
               compiler_params=_sem("parallel", "arbitrary"))(pos, g4, r1)


def _swap_with_chips(name, tbs):
    n = len(tbs)

    def body(*refs):
        t_refs, r_refs = refs[:n], refs[n:2 * n]
        send_sems, recv_sems = refs[2 * n:]
        x, y, c = _position()
        cps = []
        for i in range(n):
            for j in range(1, 4):
                px, py = _flip(x, j & 2), _flip(y, j & 1)
                cps.append(pltpu.make_async_remote_copy(src_ref=t_refs[i].at[2 * px + py], dst_ref=r_refs[i].at[j - 1],
                                                        send_sem=send_sems.at[j - 1, i], recv_sem=recv_sems.at[j - 1, i],
                                                        device_id=(px, py, c), device_id_type=MESH))
        for cp in cps:
            cp.start()
        for cp in cps:
            cp.wait()

    hbm = pl.BlockSpec(memory_space=pl.ANY)
    return _pc(body, name=name, in_specs=[hbm] * n, out_specs=[hbm] * n,
               out_shape=[jax.ShapeDtypeStruct((3,) + t.shape[1:], BF16) for t in tbs],
               scratch_shapes=[pltpu.SemaphoreType.DMA((3, n)), pltpu.SemaphoreType.DMA((3, n))])(*tbs)


def _add_chips(name, own, r2):
    L, W = own.shape
    tl = _tile(L, 512, 16)

    def body(o_ref, r_ref, out_ref):
        acc = o_ref[...]
        for j in range(3):
            acc = acc + r_ref[j].astype(F32)
        out_ref[...] = acc

    return _pc(body, name=name, grid=(L // tl,),
               in_specs=[pl.BlockSpec((tl, W), lambda i: (i, 0)), pl.BlockSpec((3, tl, W), lambda i: (0, i, 0))],
               out_specs=pl.BlockSpec((tl, W), lambda i: (i, 0)), out_shape=jax.ShapeDtypeStruct((L, W), F32),
               compiler_params=_sem("parallel"))(own, r2)


_HBM = pl.BlockSpec(memory_space=pltpu.HBM)
_SEM = pl.BlockSpec(memory_space=pltpu.SEMAPHORE)
_EFFECT = pltpu.SideEffectType.DATAFLOW_SIDE_EFFECTING


def _in_hbm(a):
    return pltpu.with_memory_space_constraint(a, pltpu.HBM)


def _after(x, token):
    return x + token[0, 0].astype(x.dtype)


def _copies_start(name, bufs, sem_shape, plan):
    nb = len(bufs)

    def body(*refs):
        for cp in plan(refs[:nb], refs[nb], refs[nb + 1]):
            cp.start()
        refs[-1][...] = jnp.zeros_like(refs[-1])

    out = _pc(body, name=name, in_specs=[_HBM] * nb,
              out_specs=(_SEM, _SEM, *[_HBM] * nb, pl.BlockSpec(memory_space=pltpu.VMEM)),
              out_shape=(pltpu.SemaphoreType.DMA(sem_shape), pltpu.SemaphoreType.DMA(sem_shape),
                         *[pltpu.HBM(b.shape, b.dtype) for b in bufs], jax.ShapeDtypeStruct((8, LANES), F32)),
              input_output_aliases={i: 2 + i for i in range(nb)},
              compiler_params=pltpu.CompilerParams(has_side_effects=_EFFECT))(*[_in_hbm(b) for b in bufs])
    return out[0], out[1], list(out[2:2 + nb]), out[-1]


def _copies_wait(name, bufs, send_sems, recv_sems, plan, after):
    nb = len(bufs)

    def body(*refs):
        for cp in plan(refs[:nb], refs[nb], refs[nb + 1]):
            cp.wait_send()
            cp.wait_recv()

    return list(_pc(body, name=name, in_specs=[_HBM] * nb + [_SEM, _SEM, pl.BlockSpec(memory_space=pl.ANY)],
                    out_specs=[_HBM] * nb, out_shape=[pltpu.HBM(b.shape, b.dtype) for b in bufs],
                    input_output_aliases={i: i for i in range(nb)},
                    compiler_params=pltpu.CompilerParams(has_side_effects=_EFFECT))(*bufs, send_sems, recv_sems, after))


def _plan_gather_chips(n):
    def plan(refs, send_sems, recv_sems):
        x, y, c = _position()
        peers = [(x, y, 1 - c), (1 - x, y, c), (x, 1 - y, c), (1 - x, 1 - y, c)]
        return [pltpu.make_async_remote_copy(src_ref=refs[i], dst_ref=refs[n + i].at[4 * x + 2 * y + c],
                                             send_sem=send_sems.at[k * n + i], recv_sem=recv_sems.at[k * n + i],
                                             device_id=peer, device_id_type=MESH)
                for i in range(n) for k, peer in enumerate(peers)]
    return plan


def _plan_gather_sibling(n):
    def plan(refs, send_sems, recv_sems):
        x, y, c = _position()
        slots = [4 * (1 - x) + 2 * y + c, 4 * x + 2 * (1 - y) + c, 4 * (1 - x) + 2 * (1 - y) + c]
        return [pltpu.make_async_remote_copy(src_ref=refs[i].at[s], dst_ref=refs[i].at[s],
                                             send_sem=send_sems.at[k * n + i], recv_sem=recv_sems.at[k * n + i],
                                             device_id=(x, y, 1 - c), device_id_type=MESH)
                for i in range(n) for k, s in enumerate(slots)]
    return plan


def _plan_reduce_sibling(n):
    def plan(refs, send_sems, recv_sems):
        x, y, c = _position()
        return [pltpu.make_async_remote_copy(src_ref=refs[i].at[k, 1 - c], dst_ref=refs[n + i].at[k],
                                             send_sem=send_sems.at[k * n + i], recv_sem=recv_sems.at[k * n + i],
                                             device_id=(x, y, 1 - c), device_id_type=MESH)
                for i in range(n) for k in range(N_DEV // 2)]
    return plan


def _plan_reduce_chips(n):
    def plan(refs, send_sems, recv_sems):
        x, y, c = _position()
        cps = []
        for i in range(n):
            for j in range(1, 4):
                px, py = _flip(x, j & 2), _flip(y, j & 1)
                sem = (j - 1) * n + i
                cps.append(pltpu.make_async_remote_copy(src_ref=refs[i].at[2 * px + py], dst_ref=refs[n + i].at[j - 1],
                                                        send_sem=send_sems.at[sem], recv_sem=recv_sems.at[sem],
                                                        device_id=(px, py, c), device_id_type=MESH))
        return cps
    return plan


def _heads_major(a, S):
    return a.reshape(S, -1, HEAD_DIM).transpose(1, 0, 2)


def _heads_minor(a):
    return a.transpose(1, 0, 2).reshape(a.shape[1], -1)


def kernel(x, mem, attn_norm, attn_w_qkv, attn_q_gain, attn_k_gain, attn_w_o, pool_norm, pool_w, pool_scale, xattn_norm, mem_norm, xattn_w_q, xattn_w_kv, xattn_w_o, ffn_norm, ffn_w_up, ffn_conv_w, ffn_conv_b, ffn_w_down, final_norm, loss_target, m_attn_norm, m_attn_w_qkv, m_attn_q_gain, m_attn_k_gain, m_attn_w_o, m_pool_norm, m_pool_w, m_pool_scale, m_xattn_norm, m_mem_norm, m_xattn_w_q, m_xattn_w_kv, m_xattn_w_o, m_ffn_norm, m_ffn_w_up, m_ffn_conv_w, m_ffn_conv_b, m_ffn_w_down, m_final_norm, v_attn_norm, v_attn_w_qkv, v_attn_q_gain, v_attn_k_gain, v_attn_w_o, v_pool_norm, v_pool_w, v_pool_scale, v_xattn_norm, v_mem_norm, v_xattn_w_q, v_xattn_w_kv, v_xattn_w_o, v_ffn_norm, v_ffn_w_up, v_ffn_conv_w, v_ffn_conv_b, v_ffn_w_down, v_final_norm):
    names = ['attn_norm', 'attn_w_qkv', 'attn_q_gain', 'attn_k_gain', 'attn_w_o', 'pool_norm', 'pool_w', 'pool_scale',
             'xattn_norm', 'mem_norm', 'xattn_w_q', 'xattn_w_kv', 'xattn_w_o', 'ffn_norm', 'ffn_w_up', 'ffn_conv_w',
             'ffn_conv_b', 'ffn_w_down', 'final_norm']
    W = dict(zip(names, (attn_norm, attn_w_qkv, attn_q_gain, attn_k_gain, attn_w_o, pool_norm, pool_w, pool_scale,
                         xattn_norm, mem_norm, xattn_w_q, xattn_w_kv, xattn_w_o, ffn_norm, ffn_w_up, ffn_conv_w,
                         ffn_conv_b, ffn_w_down, final_norm)))
    Mo = dict(zip(names, (m_attn_norm, m_attn_w_qkv, m_attn_q_gain, m_attn_k_gain, m_attn_w_o, m_pool_norm, m_pool_w,
                          m_pool_scale, m_xattn_norm, m_mem_norm, m_xattn_w_q, m_xattn_w_kv, m_xattn_w_o, m_ffn_norm,
                          m_ffn_w_up, m_ffn_conv_w, m_ffn_conv_b, m_ffn_w_down, m_final_norm)))
    Vo = dict(zip(names, (v_attn_norm, v_attn_w_qkv, v_attn_q_gain, v_attn_k_gain, v_attn_w_o, v_pool_norm, v_pool_w,
                          v_pool_scale, v_xattn_norm, v_mem_norm, v_xattn_w_q, v_xattn_w_kv, v_xattn_w_o, v_ffn_norm,
                          v_ffn_w_up, v_ffn_conv_w, v_ffn_conv_b, v_ffn_w_down, v_final_norm)))

    S, D = x.shape[1], x.shape[2]
    n_layers = xattn_norm.shape[0]
    n_up = ffn_w_up.shape[2]
    qkv_w = attn_w_qkv.shape[2] * N_DEV
    n_heads = qkv_w // HEAD_DIM - 2 * N_KV_HEADS
    n_rot = (n_heads + N_KV_HEADS) * HEAD_DIM // LANES
    group_w = pool_w.shape[3]
    xs, mems, tgt = x[0], mem[0], loss_target[0]
    xi, yi, ci = _position()
    dev = 4 * xi + 2 * yi + ci
    pos = jnp.stack([ci, 2 * xi + yi]).astype(jnp.int32)

    layers = range(n_layers)
    n_groups = pool_w.shape[1]
    w_qkv, w_o, attn_token = _allgather_blocks("allgather_attn", [attn_w_qkv[0].astype(BF16), attn_w_o[0].astype(BF16)])
    w_qkv = w_qkv.transpose(1, 0, 2).reshape(D, qkv_w)
    w_o = w_o.reshape(-1, D)
    blocks = [pool_w.reshape(-1, group_w)] + [xattn_w_q[l] for l in layers] + [xattn_w_kv.reshape(n_layers * D, -1)]
    blocks += [xattn_w_o[l] for l in layers] + [ffn_w_up.reshape(n_layers * D, n_up)] + [ffn_w_down[l] for l in layers]
    blocks = [b.astype(BF16) for b in blocks]
    blocks[0] = _after(blocks[0], attn_token)
    n_blk = len(blocks)
    lands = [lax.dynamic_update_index_in_dim(lax.empty((N_DEV,) + b.shape, BF16), b, dev, 0) for b in blocks]
    plan_chips, plan_sibling = _plan_gather_chips(n_blk), _plan_gather_sibling(n_blk)
    gather_sems = _copies_start("gather_chips_start", blocks + lands, (4 * n_blk,), plan_chips)
    attn_norm_late = _after(attn_norm, gather_sems[3])

    small_vec = jnp.concatenate([pool_norm.reshape(-1), pool_scale.reshape(-1), ffn_conv_w.reshape(-1)])
    small_rows = _round_up(-(-small_vec.size // PACK_W), 8)
    small = _allgather_small("allgather_small", jnp.pad(small_vec, (0, small_rows * PACK_W - small_vec.size))
                             .reshape(small_rows, PACK_W)).reshape(N_DEV, -1)
    d_sh = pool_norm.shape[1]
    pool_norm_f = small[:, :d_sh].reshape(1, D)
    pool_scale_f = small[:, d_sh:2 * d_sh].reshape(1, D)
    conv_w_f = small[:, 2 * d_sh:2 * d_sh + ffn_conv_w.size].reshape(N_DEV, n_layers, 3, n_up)
    conv_b_f = ffn_conv_b.reshape(n_layers, N_DEV, 1, n_up)

    cos, sin = _rope_tables(S)
    bd = _head_mean_matrix()
    pad_w = qkv_w - (n_heads + N_KV_HEADS) * HEAD_DIM
    qk_gain = jnp.concatenate([jnp.tile(attn_q_gain[0], n_heads), jnp.tile(attn_k_gain[0], N_KV_HEADS),
                               jnp.ones((pad_w,), F32)]).reshape(1, qkv_w)
    qk_scale = jnp.concatenate([jnp.full((n_heads * HEAD_DIM,), HEAD_DIM ** -0.5, F32),
                                jnp.ones((qkv_w - n_heads * HEAD_DIM,), F32)]).reshape(1, qkv_w)

    saved = []

    def xattn_ffn_fwd(l, xin, hx=None):
        if hx is None:
            hx = _rmsnorm(f"xattn_norm{l}", xin, xattn_norm[l:l + 1], BF16)
        memn = _rmsnorm(f"mem_norm{l}", mems, mem_norm[l:l + 1], BF16)
        qx = _mm_nn(f"xattn_q{l}", hx, w_xq[l], BF16)
        kv = _mm_nn_bs(f"xattn_kv{l}", memn, w_xkv, BF16, l)
        ox = _xattn_fwd(f"xattn_fwd{l}", qx, kv)
        x2 = _mm_nn(f"xattn_o{l}", ox, w_xo[l], F32, res=xin)
        hf = _rmsnorm(f"ffn_norm{l}", x2, ffn_norm[l:l + 1], BF16)
        u = _mm_nn_bs(f"ffn_up{l}", hf, w_up, F32, l).reshape(2, N_DEV // 2, S, n_up)
        cw = conv_w_f[:, l].reshape(2, N_DEV // 2, 3, n_up)
        cb = conv_b_f[l].reshape(2, N_DEV // 2, 1, n_up)
        act = _ffn_act(f"ffn_act{l}", u, cw, cb)
        x3 = _mm_nn_as(f"ffn_down{l}", act, w_down[l], F32, x2)
        saved.append(dict(xin=xin, hx=hx, memn=memn, qx=qx, kv=kv, ox=ox, x2=x2, hf=hf, u=u, cw=cw, cb=cb, act=act))
        return x3

    h0 = _rmsnorm("attn_norm", xs, attn_norm_late, BF16)
    qkv = _mm_nn("attn_qkv", h0, w_qkv, F32)
    qkr = _heads_major(_qk_rope("qk_rope", qkv, qk_gain, qk_scale, cos, sin, bd, n_rot), S)
    q_hm, k_hm, v_hm = qkr[:n_heads], qkr[n_heads:n_heads + N_KV_HEADS], qkr[n_heads + N_KV_HEADS:]
    o_hm, lse = _attn_fwd("attn_fwd", q_hm, k_hm, v_hm)
    o_att = _heads_minor(o_hm)
    arrived = _copies_wait("gather_chips_wait", gather_sems[2], gather_sems[0], gather_sems[1], plan_chips, o_att)
    pass_sems = _copies_start("gather_sibling_start", arrived[n_blk:], (3 * n_blk,), plan_sibling)
    x1 = _mm_nn("attn_o", o_att, _after(w_o, pass_sems[3]), F32, res=xs)
    hx0 = _rmsnorm("xattn_norm0", x1, xattn_norm[0:1], BF16)
    gathered = iter(_copies_wait("gather_sibling_wait", pass_sems[2], pass_sems[0], pass_sems[1], plan_sibling, hx0))
    w_pool = (next(gathered).reshape(N_DEV, n_groups, -1, group_w).transpose(1, 0, 2, 3)
              .reshape(n_groups, group_w, group_w))
    w_xq = [next(gathered).reshape(D, D) for l in layers]
    w_xkv = next(gathered)
    w_xo = [next(gathered).reshape(D, D) for l in layers]
    w_up = next(gathered)
    w_down = [next(gathered).reshape(-1, D) for l in layers]
    x3 = xattn_ffn_fwd(0, x1, hx0)
    hp = _rmsnorm("pool_norm", x3, pool_norm_f, F32)
    mixed = _pool_window("pool_window", hp, group_w, False, BF16)
    x4 = _pool_proj("pool_proj", mixed, w_pool, pool_scale_f, x3)
    x6 = xattn_ffn_fwd(1, x4)

    G = {}
    g, d_final, lvec = _loss_head("loss_head", x6, final_norm.reshape(1, D), tgt)
    G['final_norm'] = d_final.reshape(D)
    loss = lax.psum(0.5 * jnp.sum(lvec) / D, ("x", "y", "c"))

    d_xn, d_mn, d_fn, d_xq, d_xkv, d_xo, d_up, d_cw, d_cb, d_down = ([None] * n_layers for _ in range(10))

    def xattn_ffn_bwd(l, g, conv_b_late=None, after_act=None):
        sv = saved[l]
        d_act = _mm_nt_bs(f"ffn_down_dx{l}", g, w_down[l], N_DEV // 2, F32)
        d_down[l] = _mm_tn_as(f"ffn_down_dw{l}", sv['act'], g, F32)
        du, st = _ffn_act_bwd(f"ffn_act_bwd{l}", sv['u'], d_act, sv['cw'],
                              sv['cb'] if conv_b_late is None else conv_b_late)
        du = du.reshape(N_DEV, S, n_up)
        ffn_gain = ffn_norm[l:l + 1] if after_act is None else _after(ffn_norm[l:l + 1], after_act(du))
        st = st.reshape(N_DEV, 8, n_up)
        d_cw[l], d_cb[l] = st[:, 0:3], st[:, 3].reshape(-1)
        d_up[l] = _mm_tn_bs(f"ffn_up_dw{l}", sv['hf'], du, F32)
        dhf = _mm_nt_abs(f"ffn_up_dx{l}", du, w_up, D, F32, l)
        g, d_fn[l] = _rmsnorm_bwd(f"ffn_norm_bwd{l}", sv['x2'], ffn_gain, dhf, g)
        d_xo[l] = _mm_tn(f"xattn_o_dw{l}", sv['ox'], g, F32)
        do = _mm_nt(f"xattn_o_dx{l}", g, w_xo[l], BF16)
        dq, dkv = _xattn_bwd(f"xattn_bwd{l}", sv['qx'], sv['kv'], do)
        d_xq[l] = _mm_tn(f"xattn_q_dw{l}", sv['hx'], dq, F32)
        d_xkv[l] = _mm_tn_bs(f"xattn_kv_dw{l}", sv['memn'], dkv, F32)
        dmemn = _mm_nt_abs(f"xattn_kv_dx{l}", dkv, w_xkv, D, F32, l)
        _, d_mn[l] = _rmsnorm_bwd(f"mem_norm_bwd{l}", mems, mem_norm[l:l + 1], dmemn)
        dhx = _mm_nt(f"xattn_q_dx{l}", dq, w_xq[l], F32)
        g, d_xn[l] = _rmsnorm_bwd(f"xattn_norm_bwd{l}", sv['xin'], xattn_norm[l:l + 1], dhx, g)
        return g

    def reduce_start(tag, bufs):
        n = len(bufs)
        g4s = [b.reshape((N_DEV // 2, 2) + b.shape[1:]) for b in bufs]
        lands = [lax.empty((N_DEV // 2,) + b.shape[1:], F32) for b in bufs]
        plan = _plan_reduce_sibling(n)
        return (n, plan) + _copies_start(f"reduce_sibling_start_{tag}", g4s + lands, (N_DEV // 2 * n,), plan)

    def reduce_between(tag, state, after):
        n, plan, send_sems, recv_sems, thru, _ = state
        got = _copies_wait(f"reduce_sibling_wait_{tag}", thru, send_sems, recv_sems, plan, after)
        sums = [_add_sibling(f"reduce_add_sibling_{tag}{i}", got[i], got[n + i], pos) for i in range(n)]
        lands = [lax.empty((3,) + tb.shape[1:], BF16) for tb, _ in sums]
        plan = _plan_reduce_chips(n)
        return (n, plan, [own for _, own in sums]) + _copies_start(f"reduce_chips_start_{tag}",
                                                                    [tb for tb, _ in sums] + lands, (3 * n,), plan)

    def reduce_finish(tag, state, after):
        n, plan, owns, send_sems, recv_sems, thru, _ = state
        got = _copies_wait(f"reduce_chips_wait_{tag}", thru, send_sems, recv_sems, plan, after)
        return [_add_chips(f"reduce_add_chips_{tag}{i}", owns[i], got[n + i]) for i in range(n)]

    def layer_bufs(l):
        return [d_xq[l].reshape(N_DEV, -1, D), d_xkv[l], d_xo[l].reshape(N_DEV, -1, D), d_up[l],
                d_down[l].reshape(N_DEV, -1, D)]

    g = xattn_ffn_bwd(1, g)
    d_mixed, d_pool_w, d_pool_scale = _pool_proj_bwd("pool_proj_bwd", g, mixed, w_pool, pool_scale_f)
    dhp = _pool_window("pool_window_bwd", d_mixed, group_w, True, F32)
    g, d_pool_norm = _rmsnorm_bwd("pool_norm_bwd", x3, pool_norm_f, dhp, g)
    upper = reduce_start("upper", [d_pool_w.reshape(n_groups, N_DEV, -1, group_w).transpose(1, 0, 2, 3)
                                   .reshape(N_DEV, -1, group_w)] + layer_bufs(1))
    between = []

    def upper_between(du):
        between.append(reduce_between("upper", upper, du))
        return between[0][-1]

    g = xattn_ffn_bwd(0, g, _after(saved[0]['cb'], upper[-1]), upper_between)
    lower = reduce_start("lower", layer_bufs(0))
    d_wo = _mm_tn("attn_o_dw", o_att, g, F32)
    do = _mm_nt("attn_o_dx", g, _after(w_o, lower[-1]), BF16)
    lower = reduce_between("lower", lower, do)
    do_hm = _heads_major(_after(do, lower[-1]), S)
    dq_hm, dk_hm, dv_hm = _attn_bwd("attn_bwd", q_hm, k_hm, v_hm, o_hm, lse, do_hm)
    red_lower = reduce_finish("lower", lower, dq_hm)
    d_qkr = _heads_minor(jnp.concatenate([dq_hm, dk_hm, dv_hm], axis=0))
    d_qkv, d_gain = _qk_rope_bwd("qk_rope_bwd", d_qkr, qkv, qk_gain, qk_scale, cos, sin, bd, n_rot)
    red_upper = reduce_finish("upper", between[0], d_qkv)
    d_wqkv = _mm_tn("attn_qkv_dw", h0, d_qkv, F32)
    dh0 = _mm_nt("attn_qkv_dx", d_qkv, w_qkv, F32)
    grad_x, d_attn_norm = _rmsnorm_bwd("attn_norm_bwd", xs, attn_norm, dh0, g)

    bufs = [d_wqkv.reshape(D, N_DEV, -1).transpose(1, 0, 2), d_wo.reshape(N_DEV, -1, D)]
    g4s = [b.reshape((N_DEV // 2, 2) + b.shape[1:]) for b in bufs]
    r1s = _swap_with_sibling("reduce_swap_sibling", g4s)
    sums = [_add_sibling(f"reduce_add_sibling{i}", g4, r1, pos) for i, (g4, r1) in enumerate(zip(g4s, r1s))]
    r2s = _swap_with_chips("reduce_swap_chips", [tb for tb, _ in sums])
    red = [_add_chips(f"reduce_add_chips{i}", own, r2) for i, ((_, own), r2) in enumerate(zip(sums, r2s))]
    G['attn_w_qkv'], G['attn_w_o'] = red[0][None], red[1][None]
    G['pool_w'] = red_upper[0].reshape(pool_w.shape)
    per_layer = [red_lower, red_upper[1:]]
    for i, n in enumerate(['xattn_w_q', 'xattn_w_kv', 'xattn_w_o', 'ffn_w_up', 'ffn_w_down']):
        G[n] = jnp.stack([per_layer[l][i] for l in layers])

    hq = n_heads * HEAD_DIM
    small_g = {'attn_norm': d_attn_norm, 'attn_q_gain': d_gain[0, :hq].reshape(n_heads, HEAD_DIM).sum(0),
               'attn_k_gain': d_gain[0, hq:hq + N_KV_HEADS * HEAD_DIM].reshape(N_KV_HEADS, HEAD_DIM).sum(0),
               'pool_norm': d_pool_norm, 'pool_scale': d_pool_scale,
               'xattn_norm': jnp.concatenate(d_xn), 'mem_norm': jnp.concatenate(d_mn), 'ffn_norm': jnp.concatenate(d_fn),
               'ffn_conv_w': jnp.stack(d_cw, axis=1), 'ffn_conv_b': jnp.stack(d_cb)}
    order = list(small_g)
    flat = jnp.concatenate([small_g[n].reshape(-1) for n in order] + [G['final_norm']])
    ar_rows = _round_up(-(-flat.size // PACK_W), 8)
    flat = jnp.pad(flat, (0, ar_rows * PACK_W - flat.size)).reshape(ar_rows, PACK_W)
    summed = _sum_slots("allreduce_sum", _allgather_small("allreduce_gather", flat)).reshape(-1)
    at = 0
    for n in order + ['final_norm']:
        size = G['final_norm'].size if n == 'final_norm' else small_g[n].size
        piece = summed[at:at + size]
        at += size
        if n in ('pool_norm', 'pool_scale'):
            piece = lax.dynamic_slice(piece, (dev * d_sh,), (d_sh,))
        elif n == 'ffn_conv_w':
            piece = lax.dynamic_index_in_dim(piece.reshape(N_DEV, n_layers, 3, n_up), dev, 0, keepdims=False)
        G[n] = piece.reshape(W[n].shape)

    deltas, new_m, new_v = [], [], []
    for n in names:
        d, nm, nv = _adamw(f"adamw_{n}", W[n], G[n], Mo[n], Vo[n])
        deltas.append(d)
        new_m.append(nm)
        new_v.append(nv)
    return (loss, grad_x[None], *[G[n] for n in names], *deltas, *new_m, *new_v)
```

```python
import jax
import jax.numpy as jnp
from jax import lax
from jax.experimental import pallas as pl
from jax.experimental.pallas import tpu as pltpu

F32 = jnp.float32
BF16 = jnp.bfloat16
MESH = pl.DeviceIdType.MESH

N_DEV = 8
EPS = 1e-6
HEAD_DIM = 64
N_KV_HEADS = 4
X_HEADS = 4
GRID_W = 64
ROPE_THETA = 10000.0
ROPE_PAIRS = HEAD_DIM // 4
POOL_WINDOWS = (2, 4, 8, 16)
POOL_PAD = 16
KEY_CHUNK = 1024
MM_ROWS = 1024
LANES = 128
PACK_W = 1024
ADAM_LR, ADAM_B1, ADAM_B2, ADAM_EPS, ADAM_WD, ADAM_STEP = 0.001, 0.9, 0.999, 1e-08, 0.01, 10

_NN = (((1,), (0,)), ((), ()))
_NT = (((1,), (1,)), ((), ()))
_TN = (((0,), (0,)), ((), ()))


def _pc(body, *, name, **kw):
    return pl.pallas_call(body, name=name, **kw)


def _sem(*kinds):
    return pltpu.CompilerParams(dimension_semantics=kinds)


def _tile(n, pref, mult):
    best = None
    for t in range(mult, min(n, pref) + 1, mult):
        if n % t == 0:
            best = t
    return n if best is None else best


def _round_up(n, m):
    return (n + m - 1) // m * m


def _mm_call(name, a, b, dims, grid, a_spec, b_spec, o_spec, out_shape, kaxis, res=None, res_spec=None):
    nk = grid[kaxis]
    acc_shape = tuple(d for d in o_spec.block_shape if d is not None)
    in_place = out_shape.dtype == F32 and res is None
    use_scratch = nk > 1 and not in_place

    def body(*refs):
        refs = list(refs)
        acc = refs.pop() if use_scratch else None
        a_ref, b_ref = refs[:2]
        r_ref = refs[2] if res is not None else None
        o_ref = refs[-1]
        prod = lax.dot_general(a_ref[...].astype(BF16), b_ref[...].astype(BF16), dims, preferred_element_type=F32)
        if nk == 1:
            if r_ref is not None:
                prod = prod + r_ref[...]
            o_ref[...] = prod.astype(o_ref.dtype)
            return
        k = pl.program_id(kaxis)
        tgt = o_ref if in_place else acc

        @pl.when(k == 0)
        def _():
            tgt[...] = prod

        @pl.when(k > 0)
        def _():
            tgt[...] += prod

        if not in_place:
            @pl.when(k == nk - 1)
            def _():
                r = acc[...]
                if r_ref is not None:
                    r = r + r_ref[...]
                o_ref[...] = r.astype(o_ref.dtype)

    sem = tuple("arbitrary" if ax == kaxis else "parallel" for ax in range(len(grid)))
    ins = [a, b] if res is None else [a, b, res]
    specs = [a_spec, b_spec] if res is None else [a_spec, b_spec, res_spec]
    return _pc(body, name=name, grid=grid, in_specs=specs, out_specs=o_spec, out_shape=out_shape,
               scratch_shapes=[pltpu.VMEM(acc_shape, F32)] if use_scratch else [],
               compiler_params=_sem(*sem))(*ins)


def _mm_nn(name, a, b, out_dtype, res=None):
    M, K = a.shape
    N = b.shape[1]
    tm, tn, tk = _tile(M, MM_ROWS, 16), _tile(N, 1024, LANES), _tile(K, 1024, LANES)
    return _mm_call(name, a, b, _NN, (M // tm, N // tn, K // tk),
                    pl.BlockSpec((tm, tk), lambda i, j, k: (i, k)),
                    pl.BlockSpec((tk, tn), lambda i, j, k: (k, j)),
                    pl.BlockSpec((tm, tn), lambda i, j, k: (i, j)),
                    jax.ShapeDtypeStruct((M, N), out_dtype), 2, res,
                    pl.BlockSpec((tm, tn), lambda i, j, k: (i, j)))


def _mm_nt(name, a, b, out_dtype):
    M, K = a.shape
    N = b.shape[0]
    tm, tn, tk = _tile(M, MM_ROWS, 16), _tile(N, 1024, LANES), _tile(K, 1024, LANES)
    return _mm_call(name, a, b, _NT, (M // tm, N // tn, K // tk),
                    pl.BlockSpec((tm, tk), lambda i, j, k: (i, k)),
                    pl.BlockSpec((tn, tk), lambda i, j, k: (j, k)),
                    pl.BlockSpec((tm, tn), lambda i, j, k: (i, j)),
                    jax.ShapeDtypeStruct((M, N), out_dtype), 2)


def _mm_tn(name, a, b, out_dtype):
    R, M = a.shape
    N = b.shape[1]
    tm, tn, tr = _tile(M, 1024, LANES), _tile(N, 1024, LANES), _tile(R, MM_ROWS, 16)
    return _mm_call(name, a, b, _TN, (M // tm, N // tn, R // tr),
                    pl.BlockSpec((tr, tm), lambda i, j, k: (k, i)),
                    pl.BlockSpec((tr, tn), lambda i, j, k: (k, j)),
                    pl.BlockSpec((tm, tn), lambda i, j, k: (i, j)),
                    jax.ShapeDtypeStruct((M, N), out_dtype), 2)


def _mm_nn_bs(name, a, b, out_dtype, layer=0):
    M, K = a.shape
    J, _, n = b.shape
    tm, tk = _tile(M, MM_ROWS, 16), _tile(K, 1024, LANES)
    first = layer * (K // tk)
    return _mm_call(name, a, b, _NN, (J, M // tm, K // tk),
                    pl.BlockSpec((tm, tk), lambda j, i, k: (i, k)),
                    pl.BlockSpec((None, tk, n), lambda j, i, k: (j, first + k, 0)),
                    pl.BlockSpec((None, tm, n), lambda j, i, k: (j, i, 0)),
                    jax.ShapeDtypeStruct((J, M, n), out_dtype), 2)


def _mm_nt_bs(name, a, b, J, out_dtype):
    M, K = a.shape
    n = b.shape[0] // J
    tm, tk = _tile(M, MM_ROWS, 16), _tile(K, 1024, LANES)
    return _mm_call(name, a, b, _NT, (J, M // tm, K // tk),
                    pl.BlockSpec((tm, tk), lambda j, i, k: (i, k)),
                    pl.BlockSpec((n, tk), lambda j, i, k: (j, k)),
                    pl.BlockSpec((None, tm, n), lambda j, i, k: (j, i, 0)),
                    jax.ShapeDtypeStruct((J, M, n), out_dtype), 2)


def _mm_nn_as(name, a, b, out_dtype, res):
    J, M, n = a.shape
    N = b.shape[1]
    tm, tn = _tile(M, MM_ROWS, 16), _tile(N, 1024, LANES)
    return _mm_call(name, a, b, _NN, (M // tm, N // tn, J),
                    pl.BlockSpec((None, tm, n), lambda i, j, k: (k, i, 0)),
                    pl.BlockSpec((n, tn), lambda i, j, k: (k, j)),
                    pl.BlockSpec((tm, tn), lambda i, j, k: (i, j)),
                    jax.ShapeDtypeStruct((M, N), out_dtype), 2, res,
                    pl.BlockSpec((tm, tn), lambda i, j, k: (i, j)))


def _mm_nt_abs(name, a, b, N, out_dtype, layer=0):
    J, M, n = a.shape
    tm, tn = _tile(M, MM_ROWS, 16), _tile(N, 1024, LANES)
    first = layer * (N // tn)
    return _mm_call(name, a, b, _NT, (M // tm, N // tn, J),
                    pl.BlockSpec((None, tm, n), lambda i, j, k: (k, i, 0)),
                    pl.BlockSpec((None, tn, n), lambda i, j, k: (k, first + j, 0)),
                    pl.BlockSpec((tm, tn), lambda i, j, k: (i, j)),
                    jax.ShapeDtypeStruct((M, N), out_dtype), 2)


def _mm_tn_bs(name, a, b, out_dtype):
    R, M = a.shape
    J, _, n = b.shape
    tm, tr = _tile(M, 1024, LANES), _tile(R, MM_ROWS, 16)
    return _mm_call(name, a, b, _TN, (J, M // tm, R // tr),
                    pl.BlockSpec((tr, tm), lambda j, i, k: (k, i)),
                    pl.BlockSpec((None, tr, n), lambda j, i, k: (j, k, 0)),
                    pl.BlockSpec((None, tm, n), lambda j, i, k: (j, i, 0)),
                    jax.ShapeDtypeStruct((J, M, n), out_dtype), 2)


def _mm_tn_as(name, a, b, out_dtype):
    J, R, n = a.shape
    N = b.shape[1]
    tn, tr = _tile(N, 1024, LANES), _tile(R, MM_ROWS, 16)
    return _mm_call(name, a, b, _TN, (J, N // tn, R // tr),
                    pl.BlockSpec((None, tr, n), lambda j, jn, k: (j, k, 0)),
                    pl.BlockSpec((tr, tn), lambda j, jn, k: (k, jn)),
                    pl.BlockSpec((n, tn), lambda j, jn, k: (j, jn)),
                    jax.ShapeDtypeStruct((J * n, N), out_dtype), 2)


def _rmsnorm(name, x, g, out_dtype):
    R, D = x.shape
    tm = _tile(R, 512, 16)

    def body(x_ref, g_ref, o_ref):
        xv = x_ref[...]
        r = lax.rsqrt(jnp.mean(xv * xv, axis=-1, keepdims=True) + EPS)
        o_ref[...] = (xv * r * g_ref[...]).astype(o_ref.dtype)

    return _pc(body, name=name, grid=(R // tm,),
               in_specs=[pl.BlockSpec((tm, D), lambda i: (i, 0)), pl.BlockSpec((1, D), lambda i: (0, 0))],
               out_specs=pl.BlockSpec((tm, D), lambda i: (i, 0)),
               out_shape=jax.ShapeDtypeStruct((R, D), out_dtype), compiler_params=_sem("parallel"))(x, g)


def _rmsnorm_bwd(name, x, g, dh, dres=None):
    R, D = x.shape
    tm = _tile(R, 512, 16)

    def body(*refs):
        if dres is None:
            x_ref, g_ref, dh_ref, dx_ref, dg_ref = refs
            dres_ref = None
        else:
            x_ref, g_ref, dh_ref, dres_ref, dx_ref, dg_ref = refs
        xv = x_ref[...]
        r = lax.rsqrt(jnp.mean(xv * xv, axis=-1, keepdims=True) + EPS)
        xh = xv * r
        dhv = dh_ref[...].astype(F32)

        @pl.when(pl.program_id(0) == 0)
        def _():
            dg_ref[...] = jnp.zeros_like(dg_ref)

        dg_ref[...] += jnp.sum(dhv * xh, axis=0, keepdims=True)
        dxh = dhv * g_ref[...]
        dx = r * (dxh - xh * jnp.mean(dxh * xh, axis=-1, keepdims=True))
        if dres_ref is not None:
            dx = dx + dres_ref[...]
        dx_ref[...] = dx

    row = pl.BlockSpec((tm, D), lambda i: (i, 0))
    vec = pl.BlockSpec((1, D), lambda i: (0, 0))
    ins = [x, g, dh] + ([] if dres is None else [dres])
    specs = [row, vec, row] + ([] if dres is None else [row])
    return _pc(body, name=name, grid=(R // tm,), in_specs=specs, out_specs=(row, vec),
               out_shape=(jax.ShapeDtypeStruct((R, D), F32), jax.ShapeDtypeStruct((1, D), F32)),
               compiler_params=_sem("arbitrary"))(*ins)


def _loss_head(name, x, g, tgt):
    R, D = x.shape
    tm = _tile(R, 512, 16)

    def body(x_ref, g_ref, t_ref, dx_ref, dg_ref, l_ref):
        xv = x_ref[...]
        r = lax.rsqrt(jnp.mean(xv * xv, axis=-1, keepdims=True) + EPS)
        xh = xv * r
        err = xh * g_ref[...] - t_ref[...]

        @pl.when(pl.program_id(0) == 0)
        def _():
            dg_ref[...] = jnp.zeros_like(dg_ref)
            l_ref[...] = jnp.zeros_like(l_ref)

        l_ref[...] += jnp.sum(err * err, axis=0, keepdims=True)
        dy = err * (1.0 / D)
        dg_ref[...] += jnp.sum(dy * xh, axis=0, keepdims=True)
        dxh = dy * g_ref[...]
        dx_ref[...] = r * (dxh - xh * jnp.mean(dxh * xh, axis=-1, keepdims=True))

    row = pl.BlockSpec((tm, D), lambda i: (i, 0))
    vec = pl.BlockSpec((1, D), lambda i: (0, 0))
    return _pc(body, name=name, grid=(R // tm,), in_specs=[row, vec, row], out_specs=(row, vec, vec),
               out_shape=(jax.ShapeDtypeStruct((R, D), F32), jax.ShapeDtypeStruct((1, D), F32),
                          jax.ShapeDtypeStruct((1, D), F32)),
               compiler_params=_sem("arbitrary"))(x, g, tgt)


def _rope_tables(S):
    n_rows = S // GRID_W
    row = jnp.repeat(jnp.arange(n_rows, dtype=F32), GRID_W)
    col = jnp.tile(jnp.arange(GRID_W, dtype=F32), n_rows)
    inv_freq = ROPE_THETA ** (-jnp.arange(ROPE_PAIRS, dtype=F32) / ROPE_PAIRS)
    ang = jnp.stack([row[:, None] * inv_freq, col[:, None] * inv_freq], axis=1)
    cos, sin = jnp.cos(ang), jnp.sin(ang)
    c = jnp.broadcast_to(cos[:, :, None, :], (S, 2, 2, ROPE_PAIRS)).reshape(S, HEAD_DIM)
    s = jnp.stack([-sin, sin], axis=2).reshape(S, HEAD_DIM)
    reps = LANES // HEAD_DIM
    return jnp.tile(c, (1, reps)), jnp.tile(s, (1, reps))


def _head_mean_matrix():
    h = jnp.arange(LANES) // HEAD_DIM
    m = jnp.where(h[:, None] == h[None, :], 1.0 / HEAD_DIM, 0.0).astype(BF16)
    return jnp.concatenate([m, m], axis=0)


def _head_mean(v, bd):
    hi = v.astype(BF16)
    lo = (v - hi.astype(F32)).astype(BF16)
    return jnp.dot(jnp.concatenate([hi, lo], axis=1), bd, preferred_element_type=F32)


def _swap_halves(y):
    lane = lax.broadcasted_iota(jnp.int32, y.shape, 1)
    return jnp.where(lane % 32 < 16, pltpu.roll(y, LANES - 16, 1), pltpu.roll(y, 16, 1))


def _qk_rope(name, qkv, gain, scale, cos, sin, bd, n_rot):
    S, W = qkv.shape
    tm = _tile(S, 2048, 16)

    def body(x_ref, g_ref, s_ref, c_ref, sn_ref, bd_ref, o_ref):
        j = pl.program_id(1)
        xv = x_ref[...]

        @pl.when(j < n_rot)
        def _():
            ms = _head_mean(xv * xv, bd_ref[...])
            y = xv * lax.rsqrt(ms + EPS) * g_ref[...] * s_ref[...]
            o_ref[...] = (y * c_ref[...] + _swap_halves(y) * sn_ref[...]).astype(BF16)

        @pl.when(j >= n_rot)
        def _():
            o_ref[...] = xv.astype(BF16)

    blk = pl.BlockSpec((tm, LANES), lambda i, j: (i, j))
    vec = pl.BlockSpec((1, LANES), lambda i, j: (0, j))
    tab = pl.BlockSpec((tm, LANES), lambda i, j: (i, 0))
    return _pc(body, name=name, grid=(S // tm, W // LANES),
               in_specs=[blk, vec, vec, tab, tab, pl.BlockSpec((2 * LANES, LANES), lambda i, j: (0, 0))],
               out_specs=blk, out_shape=jax.ShapeDtypeStruct((S, W), BF16),
               compiler_params=_sem("parallel", "parallel"))(qkv, gain, scale, cos, sin, bd)


def _qk_rope_bwd(name, d_out, qkv, gain, scale, cos, sin, bd, n_rot):
    S, W = qkv.shape
    tm = _tile(S, 2048, 16)

    def body(d_ref, x_ref, g_ref, s_ref, c_ref, sn_ref, bd_ref, dx_ref, dg_ref):
        j, i = pl.program_id(0), pl.program_id(1)
        dv = d_ref[...]

        @pl.when(i == 0)
        def _():
            dg_ref[...] = jnp.zeros_like(dg_ref)

        @pl.when(j < n_rot)
        def _():
            xv = x_ref[...]
            ms = _head_mean(xv * xv, bd_ref[...])
            r = lax.rsqrt(ms + EPS)
            z = xv * r
            dy = (dv * c_ref[...] - _swap_halves(dv) * sn_ref[...]) * s_ref[...]
            dg_ref[...] += jnp.sum(dy * z, axis=0, keepdims=True)
            dz = dy * g_ref[...]
            mz = _head_mean(dz * z, bd_ref[...])
            dx_ref[...] = (r * (dz - z * mz)).astype(BF16)

        @pl.when(j >= n_rot)
        def _():
            dx_ref[...] = dv.astype(BF16)

    blk = pl.BlockSpec((tm, LANES), lambda j, i: (i, j))
    vec = pl.BlockSpec((1, LANES), lambda j, i: (0, j))
    tab = pl.BlockSpec((tm, LANES), lambda j, i: (i, 0))
    return _pc(body, name=name, grid=(W // LANES, S // tm),
               in_specs=[blk, blk, vec, vec, tab, tab, pl.BlockSpec((2 * LANES, LANES), lambda j, i: (0, 0))],
               out_specs=(blk, vec),
               out_shape=(jax.ShapeDtypeStruct((S, W), BF16), jax.ShapeDtypeStruct((1, W), F32)),
               compiler_params=_sem("parallel", "arbitrary"))(d_out, qkv, gain, scale, cos, sin, bd)


def _softmax_rows(s):
    m = jnp.max(s, axis=-1, keepdims=True)
    p = jnp.exp(s - m)
    return p, jnp.sum(p, axis=-1, keepdims=True)


def _attn_fwd(name, q, k, v):
    H, S, dh = q.shape
    G = H // N_KV_HEADS
    tq = _tile(S, 128, 16)

    kc = _tile(S, KEY_CHUNK, LANES)
    R = G * tq

    def body(q_ref, k_ref, v_ref, o_ref, lse_ref):
        q = q_ref[...].reshape(R, dh)
        m = jnp.full((R, 1), -1e30, F32)
        l = jnp.zeros((R, 1), F32)
        acc = jnp.zeros((R, dh), F32)
        for c in range(S // kc):
            rows = slice(c * kc, (c + 1) * kc)
            s = lax.dot_general(q, k_ref[rows, :], _NT, preferred_element_type=F32)
            m_new = jnp.maximum(m, jnp.max(s, axis=-1, keepdims=True))
            alpha = jnp.exp(m - m_new)
            p = jnp.exp(s - m_new)
            l = alpha * l + jnp.sum(p, axis=-1, keepdims=True)
            acc = alpha * acc + jnp.dot(p.astype(BF16), v_ref[rows, :], preferred_element_type=F32)
            m = m_new
        o_ref[...] = (acc / l).astype(BF16).reshape(G, tq, dh)
        lse_ref[...] = (m + jnp.log(l)).reshape(G, tq, 1)

    qs = pl.BlockSpec((G, tq, dh), lambda kv, i: (kv, i, 0))
    ls = pl.BlockSpec((G, tq, 1), lambda kv, i: (kv, i, 0))
    ks = pl.BlockSpec((None, S, dh), lambda kv, i: (kv, 0, 0))
    return _pc(body, name=name, grid=(N_KV_HEADS, S // tq), in_specs=[qs, ks, ks], out_specs=(qs, ls),
               out_shape=(jax.ShapeDtypeStruct((H, S, dh), BF16), jax.ShapeDtypeStruct((H, S, 1), F32)),
               compiler_params=_sem("parallel", "parallel"))(q, k, v)


def _attn_bwd(name, q, k, v, o, lse, do):
    H, S, dh = q.shape
    G = H // N_KV_HEADS
    tq = _tile(S, 128, 16)
    kc = _tile(S, KEY_CHUNK, LANES)
    R = G * tq

    def body(q_ref, k_ref, v_ref, o_ref, lse_ref, do_ref, dq_ref, dk_ref, dv_ref):
        @pl.when(pl.program_id(1) == 0)
        def _():
            dk_ref[...] = jnp.zeros_like(dk_ref)
            dv_ref[...] = jnp.zeros_like(dv_ref)

        qq, dd = q_ref[...].reshape(R, dh), do_ref[...].reshape(R, dh)
        delta = jnp.sum(dd.astype(F32) * o_ref[...].reshape(R, dh).astype(F32), axis=-1, keepdims=True)
        lse = lse_ref[...].reshape(R, 1)
        dq = jnp.zeros((R, dh), F32)
        for c in range(S // kc):
            rows = slice(c * kc, (c + 1) * kc)
            kk, vv = k_ref[rows, :], v_ref[rows, :]
            p = jnp.exp(lax.dot_general(qq, kk, _NT, preferred_element_type=F32) - lse)
            dv_ref[rows, :] += lax.dot_general(p.astype(BF16), dd, _TN, preferred_element_type=F32)
            dp = lax.dot_general(dd, vv, _NT, preferred_element_type=F32)
            ds = (p * (dp - delta)).astype(BF16)
            dq = dq + jnp.dot(ds, kk, preferred_element_type=F32)
            dk_ref[rows, :] += lax.dot_general(ds, qq, _TN, preferred_element_type=F32)
        dq_ref[...] = dq.reshape(G, tq, dh)

    qs = pl.BlockSpec((G, tq, dh), lambda kv, i: (kv, i, 0))
    ls = pl.BlockSpec((G, tq, 1), lambda kv, i: (kv, i, 0))
    ks = pl.BlockSpec((None, S, dh), lambda kv, i: (kv, 0, 0))
    return _pc(body, name=name, grid=(N_KV_HEADS, S // tq), in_specs=[qs, ks, ks, qs, ls, qs],
               out_specs=(qs, ks, ks),
               out_shape=(jax.ShapeDtypeStruct((H, S, dh), F32), jax.ShapeDtypeStruct((N_KV_HEADS, S, dh), F32),
                          jax.ShapeDtypeStruct((N_KV_HEADS, S, dh), F32)),
               compiler_params=_sem("parallel", "arbitrary"))(q, k, v, o, lse, do)


def _xattn_fwd(name, q, kv):
    S, D = q.shape
    _, M, dh = kv.shape
    scale = dh ** -0.5
    tq = _tile(S, 256, 16)

    def body(q_ref, kv_ref, o_ref):
        for h in range(X_HEADS):
            lo, hi = h * dh, (h + 1) * dh
            s = lax.dot_general(q_ref[:, lo:hi], kv_ref[h], _NT, preferred_element_type=F32) * scale
            p, l = _softmax_rows(s)
            o = jnp.dot(p.astype(BF16), kv_ref[X_HEADS + h], preferred_element_type=F32)
            o_ref[:, lo:hi] = (o / l).astype(BF16)

    row = pl.BlockSpec((tq, D), lambda i: (i, 0))
    return _pc(body, name=name, grid=(S // tq,),
               in_specs=[row, pl.BlockSpec((2 * X_HEADS, M, dh), lambda i: (0, 0, 0))],
               out_specs=row, out_shape=jax.ShapeDtypeStruct((S, D), BF16),
               compiler_params=_sem("parallel"))(q, kv)


def _xattn_bwd(name, q, kv, do):
    S, D = q.shape
    _, M, dh = kv.shape
    scale = dh ** -0.5
    tq = _tile(S, 256, 16)

    def body(q_ref, kv_ref, do_ref, dq_ref, dkv_ref):
        @pl.when(pl.program_id(0) == 0)
        def _():
            dkv_ref[...] = jnp.zeros_like(dkv_ref)

        for h in range(X_HEADS):
            lo, hi = h * dh, (h + 1) * dh
            qh, kh, vh, doh = q_ref[:, lo:hi], kv_ref[h], kv_ref[X_HEADS + h], do_ref[:, lo:hi]
            s = lax.dot_general(qh, kh, _NT, preferred_element_type=F32) * scale
            p, l = _softmax_rows(s)
            pn = p / l
            dkv_ref[X_HEADS + h] += lax.dot_general(pn.astype(BF16), doh, _TN, preferred_element_type=F32)
            dp = lax.dot_general(doh, vh, _NT, preferred_element_type=F32)
            ds = (pn * (dp - jnp.sum(pn * dp, axis=-1, keepdims=True)) * scale).astype(BF16)
            dq_ref[:, lo:hi] = jnp.dot(ds, kh, preferred_element_type=F32).astype(BF16)
            dkv_ref[h] += lax.dot_general(ds, qh, _TN, preferred_element_type=F32)

    row = pl.BlockSpec((tq, D), lambda i: (i, 0))
    full = pl.BlockSpec((2 * X_HEADS, M, dh), lambda i: (0, 0, 0))
    return _pc(body, name=name, grid=(S // tq,), in_specs=[row, full, row], out_specs=(row, full),
               out_shape=(jax.ShapeDtypeStruct((S, D), BF16), jax.ShapeDtypeStruct((2 * X_HEADS, M, dh), F32)),
               compiler_params=_sem("arbitrary"))(q, kv, do)


def _sigmoid(x):
    return 1.0 / (1.0 + jnp.exp(-x))


def _halo_specs(tm, n, S):
    nb = tm // 8
    last8 = S // 8 - 1
    main = pl.BlockSpec((2, None, tm, n), lambda j, i: (0, j, i, 0))
    prev = pl.BlockSpec((2, None, 8, n), lambda j, i: (0, j, jnp.maximum(i * nb - 1, 0), 0))
    nxt = pl.BlockSpec((2, None, 8, n), lambda j, i: (0, j, jnp.minimum((i + 1) * nb, last8), 0))
    return main, prev, nxt


def _ffn_act(name, u, cw, cb):
    _, J, S, n = u.shape
    tm = _tile(S, 256, 16)
    nblk = S // tm

    def body(u_ref, up_ref, un_ref, w_ref, b_ref, a_ref):
        i = pl.program_id(1)
        row = lax.broadcasted_iota(jnp.int32, (tm, n), 0)
        c = []
        for half in range(2):
            main = u_ref[half]
            before = jnp.where(i > 0, up_ref[half, 7:8, :], 0.0)
            after = jnp.where(i < nblk - 1, un_ref[half, 0:1, :], 0.0)
            um = jnp.where(row == 0, before, pltpu.roll(main, 1, 0))
            up = jnp.where(row == tm - 1, after, pltpu.roll(main, tm - 1, 0))
            w = w_ref[half]
            c.append(um * w[0:1] + main * w[1:2] + up * w[2:3] + b_ref[half])
        a_ref[...] = (c[0] * _sigmoid(c[0]) * c[1]).astype(BF16)

    main, prev, nxt = _halo_specs(tm, n, S)
    return _pc(body, name=name, grid=(J, nblk),
               in_specs=[main, prev, nxt, pl.BlockSpec((2, None, 3, n), lambda j, i: (0, j, 0, 0)),
                         pl.BlockSpec((2, None, 1, n), lambda j, i: (0, j, 0, 0))],
               out_specs=pl.BlockSpec((None, tm, n), lambda j, i: (j, i, 0)),
               out_shape=jax.ShapeDtypeStruct((J, S, n), BF16),
               compiler_params=_sem("parallel", "parallel"))(u, u, u, cw, cb)


def _ffn_act_bwd(name, u, da, cw, cb):
    _, J, S, n = u.shape
    tm = _tile(S, 256, 16)
    nblk = S // tm
    te = tm + 16
    nb = tm // 8
    last8 = S // 8 - 1

    def body(u_ref, up_ref, un_ref, da_ref, dap_ref, dan_ref, w_ref, b_ref, du_ref, st_ref):
        i = pl.program_id(1)

        @pl.when(i == 0)
        def _():
            st_ref[...] = jnp.zeros_like(st_ref)

        def extended(before, main, after):
            return jnp.concatenate([jnp.where(i == 0, 0.0, before), main, jnp.where(i == nblk - 1, 0.0, after)], axis=0)

        mid = slice(8, tm + 8)
        da_e = extended(dap_ref[...], da_ref[...], dan_ref[...])
        ue, c = [], []
        for half in range(2):
            e = extended(up_ref[half], u_ref[half], un_ref[half])
            w = w_ref[half]
            ue.append((pltpu.roll(e, 1, 0), e, pltpu.roll(e, te - 1, 0)))
            c.append(ue[half][0] * w[0:1] + e * w[1:2] + ue[half][2] * w[2:3] + b_ref[half])
        sg = _sigmoid(c[0])
        dc = [da_e * c[1] * (sg * (1.0 + c[0] * (1.0 - sg))), da_e * (c[0] * sg)]
        r8 = lax.broadcasted_iota(jnp.int32, (8, n), 0)
        for half in range(2):
            w, d, (e_before, e, e_after) = w_ref[half], dc[half], ue[half]
            dm = d[mid]
            du = pltpu.roll(d, te - 1, 0)[mid] * w[0:1] + dm * w[1:2] + pltpu.roll(d, 1, 0)[mid] * w[2:3]
            du_ref[half] = du.astype(BF16)
            s0 = jnp.sum(dm * e_before[mid], axis=0, keepdims=True)
            s1 = jnp.sum(dm * e[mid], axis=0, keepdims=True)
            s2 = jnp.sum(dm * e_after[mid], axis=0, keepdims=True)
            s3 = jnp.sum(dm, axis=0, keepdims=True)
            st_ref[half] += jnp.where(r8 == 0, s0, jnp.where(r8 == 1, s1, jnp.where(r8 == 2, s2,
                                      jnp.where(r8 == 3, s3, 0.0))))

    main, prev, nxt = _halo_specs(tm, n, S)
    dmain = pl.BlockSpec((None, tm, n), lambda j, i: (j, i, 0))
    dprev = pl.BlockSpec((None, 8, n), lambda j, i: (j, jnp.maximum(i * nb - 1, 0), 0))
    dnxt = pl.BlockSpec((None, 8, n), lambda j, i: (j, jnp.minimum((i + 1) * nb, last8), 0))
    return _pc(body, name=name, grid=(J, nblk),
               in_specs=[main, prev, nxt, dmain, dprev, dnxt,
                         pl.BlockSpec((2, None, 3, n), lambda j, i: (0, j, 0, 0)),
                         pl.BlockSpec((2, None, 1, n), lambda j, i: (0, j, 0, 0))],
               out_specs=(main, pl.BlockSpec((2, None, 8, n), lambda j, i: (0, j, 0, 0))),
               out_shape=(jax.ShapeDtypeStruct((2, J, S, n), BF16), jax.ShapeDtypeStruct((2, J, 8, n), F32)),
               compiler_params=_sem("parallel", "arbitrary"))(u, u, u, da, da, da, cw, cb)


def _window_count(t, w, S):
    lo = jnp.maximum(t - w // 2, 0)
    hi = jnp.minimum(t + w - w // 2, S)
    return (hi - lo).astype(F32)


def _trailing_sums(x, w):
    k = 1
    while k < w:
        x = x + pltpu.roll(x, k, 0)
        k *= 2
    return x


def _pool_window(name, h, group_w, adjoint, out_dtype):
    S, D = h.shape
    SP = S + 2 * POOL_PAD
    per_group = group_w // LANES

    def body(h_ref, o_ref, xp):
        g = pl.program_id(0) // per_group
        t = lax.broadcasted_iota(jnp.int32, (S, LANES), 0)
        xp[0:POOL_PAD, :] = jnp.zeros((POOL_PAD, LANES), F32)
        xp[S + POOL_PAD:SP, :] = jnp.zeros((POOL_PAD, LANES), F32)
        for gi, w in enumerate(POOL_WINDOWS):
            @pl.when(g == gi)
            def _():
                hv = h_ref[...]
                cnt = _window_count(t, w, S)
                xp[POOL_PAD:S + POOL_PAD, :] = hv / cnt if adjoint else hv
                ahead = w // 2 if adjoint else w // 2 - 1
                sw = _trailing_sums(xp[...], w)
                if ahead:
                    sw = pltpu.roll(sw, SP - ahead, 0)
                win = sw[POOL_PAD:S + POOL_PAD]
                o_ref[...] = ((win if adjoint else win / cnt) - hv).astype(out_dtype)

    col = pl.BlockSpec((S, LANES), lambda j: (0, j))
    return _pc(body, name=name, grid=(D // LANES,), in_specs=[col], out_specs=col,
               out_shape=jax.ShapeDtypeStruct((S, D), out_dtype),
               scratch_shapes=[pltpu.VMEM((SP, LANES), F32)], compiler_params=_sem("parallel"))(h)


def _pool_proj(name, mixed, w, scale, res):
    S, D = mixed.shape
    G, gw, _ = w.shape
    tm = _tile(S, 512, 16)

    def body(m_ref, w_ref, s_ref, r_ref, o_ref):
        for g in range(G):
            lo, hi = g * gw, (g + 1) * gw
            y = jnp.dot(m_ref[:, lo:hi], w_ref[g], preferred_element_type=F32)
            o_ref[:, lo:hi] = r_ref[:, lo:hi] + y * s_ref[:, lo:hi]

    row = pl.BlockSpec((tm, D), lambda i: (i, 0))
    return _pc(body, name=name, grid=(S // tm,),
               in_specs=[row, pl.BlockSpec((G, gw, gw), lambda i: (0, 0, 0)), pl.BlockSpec((1, D), lambda i: (0, 0)), row],
               out_specs=row, out_shape=jax.ShapeDtypeStruct((S, D), F32),
               compiler_params=_sem("parallel"))(mixed, w, scale, res)


def _pool_proj_bwd(name, dy, mixed, w, scale):
    S, D = mixed.shape
    G, gw, _ = w.shape
    tm = _tile(S, 512, 16)

    def body(dy_ref, m_ref, w_ref, s_ref, dm_ref, dw_ref, ds_ref):
        @pl.when(pl.program_id(0) == 0)
        def _():
            dw_ref[...] = jnp.zeros_like(dw_ref)
            ds_ref[...] = jnp.zeros_like(ds_ref)

        for g in range(G):
            lo, hi = g * gw, (g + 1) * gw
            mg, dyg = m_ref[:, lo:hi], dy_ref[:, lo:hi]
            y = jnp.dot(mg, w_ref[g], preferred_element_type=F32)
            ds_ref[:, lo:hi] += jnp.sum(dyg * y, axis=0, keepdims=True)
            dyp = (dyg * s_ref[:, lo:hi]).astype(BF16)
            dm_ref[:, lo:hi] = lax.dot_general(dyp, w_ref[g], _NT, preferred_element_type=F32)
            dw_ref[g] += lax.dot_general(mg, dyp, _TN, preferred_element_type=F32)

    row = pl.BlockSpec((tm, D), lambda i: (i, 0))
    wsp = pl.BlockSpec((G, gw, gw), lambda i: (0, 0, 0))
    vec = pl.BlockSpec((1, D), lambda i: (0, 0))
    return _pc(body, name=name, grid=(S // tm,), in_specs=[row, row, wsp, vec], out_specs=(row, wsp, vec),
               out_shape=(jax.ShapeDtypeStruct((S, D), F32), jax.ShapeDtypeStruct((G, gw, gw), F32),
                          jax.ShapeDtypeStruct((1, D), F32)),
               compiler_params=_sem("arbitrary"))(dy, mixed, w, scale)


def _adamw(name, w, g, m, v):
    shape = w.shape
    C = shape[-1]
    R = w.size // C
    tm = _tile(R, 512, 8)

    def body(w_ref, g_ref, m_ref, v_ref, d_ref, nm_ref, nv_ref):
        gv = g_ref[...]
        nm = ADAM_B1 * m_ref[...] + (1.0 - ADAM_B1) * gv
        nv = ADAM_B2 * v_ref[...] + (1.0 - ADAM_B2) * (gv * gv)
        m_hat = nm / (1.0 - ADAM_B1 ** ADAM_STEP)
        v_hat = nv / (1.0 - ADAM_B2 ** ADAM_STEP)
        d_ref[...] = -ADAM_LR * (m_hat / (jnp.sqrt(v_hat) + ADAM_EPS) + ADAM_WD * w_ref[...])
        nm_ref[...] = nm
        nv_ref[...] = nv

    blk = pl.BlockSpec((tm, C), lambda i: (i, 0))
    sd = jax.ShapeDtypeStruct((R, C), F32)
    outs = _pc(body, name=name, grid=(R // tm,), in_specs=[blk] * 4, out_specs=(blk,) * 3, out_shape=(sd,) * 3,
               compiler_params=_sem("parallel"))(*(a.reshape(R, C) for a in (w, g, m, v)))
    return tuple(o.reshape(shape) for o in outs)


def _position():
    return lax.axis_index("x"), lax.axis_index("y"), lax.axis_index("c")


def _flip(v, bit):
    return 1 - v if bit else v


def _allgather_small(name, v):
    R, W = v.shape

    def body(v_ref, out_ref, send_sems, recv_sems):
        x, y, c = _position()
        me = 4 * x + 2 * y + c
        out_ref[me] = v_ref[...]
        sends = []
        for k in range(1, N_DEV):
            peer = (_flip(x, k & 4), _flip(y, k & 2), _flip(c, k & 1))
            cp = pltpu.make_async_remote_copy(src_ref=v_ref, dst_ref=out_ref.at[me], send_sem=send_sems.at[k - 1],
                                              recv_sem=recv_sems.at[k - 1], device_id=peer, device_id_type=MESH)
            cp.start()
            sends.append(cp)
        for k in range(1, N_DEV):
            peer = (_flip(x, k & 4), _flip(y, k & 2), _flip(c, k & 1))
            slot = 4 * peer[0] + 2 * peer[1] + peer[2]
            pltpu.make_async_remote_copy(src_ref=v_ref, dst_ref=out_ref.at[slot], send_sem=send_sems.at[k - 1],
                                         recv_sem=recv_sems.at[k - 1], device_id=peer, device_id_type=MESH).wait_recv()
        for cp in sends:
            cp.wait_send()

    vm = pl.BlockSpec(memory_space=pltpu.VMEM)
    return _pc(body, name=name, in_specs=[vm], out_specs=vm, out_shape=jax.ShapeDtypeStruct((N_DEV, R, W), F32),
               scratch_shapes=[pltpu.SemaphoreType.DMA((N_DEV - 1,)), pltpu.SemaphoreType.DMA((N_DEV - 1,))])(v)


def _sum_slots(name, a):
    n, R, W = a.shape

    def body(a_ref, o_ref):
        acc = a_ref[0]
        for s in range(1, n):
            acc = acc + a_ref[s]
        o_ref[...] = acc

    return _pc(body, name=name, grid=(1,), in_specs=[pl.BlockSpec((n, R, W), lambda i: (0, 0, 0))],
               out_specs=pl.BlockSpec((R, W), lambda i: (0, 0)), out_shape=jax.ShapeDtypeStruct((R, W), F32))(a)


def _allgather_blocks(name, blocks):
    n = len(blocks)

    def body(*refs):
        b_refs, out_refs, token = refs[:n], refs[n:2 * n], refs[2 * n]
        send_sems, recv_sems, local_sems = refs[2 * n + 1:]
        token[...] = jnp.zeros_like(token)
        x, y, c = _position()
        me, sibling = (x, y, c), (x, y, 1 - c)
        chips = [(1 - x, y), (x, 1 - y), (1 - x, 1 - y)]

        def slot(i, px, py, pc):
            return out_refs[i].at[4 * px + 2 * py + pc]

        def copy(i, k, block, to, src=None):
            return pltpu.make_async_remote_copy(src_ref=slot(i, *block) if src is None else src, dst_ref=slot(i, *block),
                                                send_sem=send_sems.at[k, i], recv_sem=recv_sems.at[k, i],
                                                device_id=to, device_id_type=MESH)

        mine = [pltpu.make_async_copy(b_refs[i], slot(i, *me), local_sems.at[i]) for i in range(n)]
        first = [copy(i, 1 + j, me, (*chip, c), src=b_refs[i]) for i in range(n) for j, chip in enumerate(chips)]
        first += [copy(i, 0, me, sibling, src=b_refs[i]) for i in range(n)]
        for cp in mine + first:
            cp.start()
        passed = []
        for j, chip in enumerate(chips):
            for i in range(n):
                copy(i, 1 + j, (*chip, c), me).wait_recv()
                passed.append(copy(i, 4 + j, (*chip, c), sibling))
                passed[-1].start()
        for i in range(n):
            copy(i, 0, sibling, me).wait_recv()
        for j, chip in enumerate(chips):
            for i in range(n):
                copy(i, 4 + j, (*chip, 1 - c), me).wait_recv()
        for cp in first + passed:
            cp.wait_send()
        for cp in mine:
            cp.wait()

    hbm = pl.BlockSpec(memory_space=pl.ANY)
    return _pc(body, name=name, in_specs=[hbm] * n, out_specs=[hbm] * n + [pl.BlockSpec(memory_space=pltpu.VMEM)],
               out_shape=[jax.ShapeDtypeStruct((N_DEV,) + b.shape, b.dtype) for b in blocks]
               + [jax.ShapeDtypeStruct((8, LANES), F32)],
               scratch_shapes=[pltpu.SemaphoreType.DMA((7, n)), pltpu.SemaphoreType.DMA((7, n)),
                               pltpu.SemaphoreType.DMA((n,))])(*blocks)


def _add_sibling(name, g4, r1, pos):
    n, _, L, W = g4.shape
    tl = _tile(L, 512, 16)

    def body(pos_ref, g_ref, r_ref, tb_ref, own_ref):
        t = g_ref[...] + r_ref[...]
        tb_ref[...] = t.astype(BF16)

        @pl.when(pl.program_id(1) == pos_ref[1])
        def _():
            own_ref[...] = t

    gs = pltpu.PrefetchScalarGridSpec(
        num_scalar_prefetch=1, grid=(L // tl, n),
        in_specs=[pl.BlockSpec((None, None, tl, W), lambda i, k, p: (k, p[0], i, 0)),
                  pl.BlockSpec((None, tl, W), lambda i, k, p: (k, i, 0))],
        out_specs=(pl.BlockSpec((None, tl, W), lambda i, k, p: (k, i, 0)),
                   pl.BlockSpec((tl, W), lambda i, k, p: (i, 0))))
    return _pc(body, name=name, grid_spec=gs,
               out_shape=(jax.ShapeDtypeStruct((n, L, W), BF16), jax.ShapeDtypeStruct((L, W), F32)),
               compiler_params=_sem("parallel", "arbitrary"))(pos, g4, r1)


def _add_chips(name, own, r2):
    L, W = own.shape
    tl = _tile(L, 512, 16)

    def body(o_ref, r_ref, out_ref):
        acc = o_ref[...]
        for j in range(3):
            acc = acc + r_ref[j].astype(F32)
        out_ref[...] = acc

    return _pc(body, name=name, grid=(L // tl,),
               in_specs=[pl.BlockSpec((tl, W), lambda i: (i, 0)), pl.BlockSpec((3, tl, W), lambda i: (0, i, 0))],
               out_specs=pl.BlockSpec((tl, W), lambda i: (i, 0)), out_shape=jax.ShapeDtypeStruct((L, W), F32),
               compiler_params=_sem("parallel"))(own, r2)


_HBM = pl.BlockSpec(memory_space=pltpu.HBM)
_SEM = pl.BlockSpec(memory_space=pltpu.SEMAPHORE)
_EFFECT = pltpu.SideEffectType.DATAFLOW_SIDE_EFFECTING


def _in_hbm(a):
    return pltpu.with_memory_space_constraint(a, pltpu.HBM)


def _after(x, token):
    return x + token[0, 0].astype(x.dtype)


def _copies_start(name, bufs, sem_shape, plan):
    nb = len(bufs)

    def body(*refs):
        for cp in plan(refs[:nb], refs[nb], refs[nb + 1]):
            cp.start()
        refs[-1][...] = jnp.zeros_like(refs[-1])

    out = _pc(body, name=name, in_specs=[_HBM] * nb,
              out_specs=(_SEM, _SEM, *[_HBM] * nb, pl.BlockSpec(memory_space=pltpu.VMEM)),
              out_shape=(pltpu.SemaphoreType.DMA(sem_shape), pltpu.SemaphoreType.DMA(sem_shape),
                         *[pltpu.HBM(b.shape, b.dtype) for b in bufs], jax.ShapeDtypeStruct((8, LANES), F32)),
              input_output_aliases={i: 2 + i for i in range(nb)},
              compiler_params=pltpu.CompilerParams(has_side_effects=_EFFECT))(*[_in_hbm(b) for b in bufs])
    return out[0], out[1], list(out[2:2 + nb]), out[-1]


def _copies_wait(name, bufs, send_sems, recv_sems, plan, after):
    nb = len(bufs)

    def body(*refs):
        for cp in plan(refs[:nb], refs[nb], refs[nb + 1]):
            cp.wait_send()
            cp.wait_recv()

    return list(_pc(body, name=name, in_specs=[_HBM] * nb + [_SEM, _SEM, pl.BlockSpec(memory_space=pl.ANY)],
                    out_specs=[_HBM] * nb, out_shape=[pltpu.HBM(b.shape, b.dtype) for b in bufs],
                    input_output_aliases={i: i for i in range(nb)},
                    compiler_params=pltpu.CompilerParams(has_side_effects=_EFFECT))(*bufs, send_sems, recv_sems, after))


def _plan_gather_chips(n):
    def plan(refs, send_sems, recv_sems):
        x, y, c = _position()
        peers = [(x, y, 1 - c), (1 - x, y, c), (x, 1 - y, c), (1 - x, 1 - y, c)]
        return [pltpu.make_async_remote_copy(src_ref=refs[i], dst_ref=refs[n + i].at[4 * x + 2 * y + c],
                                             send_sem=send_sems.at[k * n + i], recv_sem=recv_sems.at[k * n + i],
                                             device_id=peer, device_id_type=MESH)
                for i in range(n) for k, peer in enumerate(peers)]
    return plan


def _plan_gather_sibling(n):
    def plan(refs, send_sems, recv_sems):
        x, y, c = _position()
        slots = [4 * (1 - x) + 2 * y + c, 4 * x + 2 * (1 - y) + c, 4 * (1 - x) + 2 * (1 - y) + c]
        return [pltpu.make_async_remote_copy(src_ref=refs[i].at[s], dst_ref=refs[i].at[s],
                                             send_sem=send_sems.at[k * n + i], recv_sem=recv_sems.at[k * n + i],
                                             device_id=(x, y, 1 - c), device_id_type=MESH)
                for i in range(n) for k, s in enumerate(slots)]
    return plan


def _plan_reduce_sibling(n):
    def plan(refs, send_sems, recv_sems):
        x, y, c = _position()
        return [pltpu.make_async_remote_copy(src_ref=refs[i].at[k, 1 - c], dst_ref=refs[n + i].at[k],
                                             send_sem=send_sems.at[k * n + i], recv_sem=recv_sems.at[k * n + i],
                                             device_id=(x, y, 1 - c), device_id_type=MESH)
                for i in range(n) for k in range(N_DEV // 2)]
    return plan


def _plan_reduce_chips(n):
    def plan(refs, send_sems, recv_sems):
        x, y, c = _position()
        cps = []
        for i in range(n):
            for j in range(1, 4):
                px, py = _flip(x, j & 2), _flip(y, j & 1)
                sem = (j - 1) * n + i
                cps.append(pltpu.make_async_remote_copy(src_ref=refs[i].at[2 * px + py], dst_ref=refs[n + i].at[j - 1],
                                                        send_sem=send_sems.at[sem], recv_sem=recv_sems.at[sem],
                                                        device_id=(px, py, c), device_id_type=MESH))
        return cps
    return plan


def _heads_major(a, S):
    return a.reshape(S, -1, HEAD_DIM).transpose(1, 0, 2)


def _heads_minor(a):
    return a.transpose(1, 0, 2).reshape(a.shape[1], -1)


def kernel(x, mem, attn_norm, attn_w_qkv, attn_q_gain, attn_k_gain, attn_w_o, pool_norm, pool_w, pool_scale, xattn_norm, mem_norm, xattn_w_q, xattn_w_kv, xattn_w_o, ffn_norm, ffn_w_up, ffn_conv_w, ffn_conv_b, ffn_w_down, final_norm, loss_target, m_attn_norm, m_attn_w_qkv, m_attn_q_gain, m_attn_k_gain, m_attn_w_o, m_pool_norm, m_pool_w, m_pool_scale, m_xattn_norm, m_mem_norm, m_xattn_w_q, m_xattn_w_kv, m_xattn_w_o, m_ffn_norm, m_ffn_w_up, m_ffn_conv_w, m_ffn_conv_b, m_ffn_w_down, m_final_norm, v_attn_norm, v_attn_w_qkv, v_attn_q_gain, v_attn_k_gain, v_attn_w_o, v_pool_norm, v_pool_w, v_pool_scale, v_xattn_norm, v_mem_norm, v_xattn_w_q, v_xattn_w_kv, v_xattn_w_o, v_ffn_norm, v_ffn_w_up, v_ffn_conv_w, v_ffn_conv_b, v_ffn_w_down, v_final_norm):
    names = ['attn_norm', 'attn_w_qkv', 'attn_q_gain', 'attn_k_gain', 'attn_w_o', 'pool_norm', 'pool_w', 'pool_scale',
             'xattn_norm', 'mem_norm', 'xattn_w_q', 'xattn_w_kv', 'xattn_w_o', 'ffn_norm', 'ffn_w_up', 'ffn_conv_w',
             'ffn_conv_b', 'ffn_w_down', 'final_norm']
    W = dict(zip(names, (attn_norm, attn_w_qkv, attn_q_gain, attn_k_gain, attn_w_o, pool_norm, pool_w, pool_scale,
                         xattn_norm, mem_norm, xattn_w_q, xattn_w_kv, xattn_w_o, ffn_norm, ffn_w_up, ffn_conv_w,
                         ffn_conv_b, ffn_w_down, final_norm)))
    Mo = dict(zip(names, (m_attn_norm, m_attn_w_qkv, m_attn_q_gain, m_attn_k_gain, m_attn_w_o, m_pool_norm, m_pool_w,
                          m_pool_scale, m_xattn_norm, m_mem_norm, m_xattn_w_q, m_xattn_w_kv, m_xattn_w_o, m_ffn_norm,
                          m_ffn_w_up, m_ffn_conv_w, m_ffn_conv_b, m_ffn_w_down, m_final_norm)))
    Vo = dict(zip(names, (v_attn_norm, v_attn_w_qkv, v_attn_q_gain, v_attn_k_gain, v_attn_w_o, v_pool_norm, v_pool_w,
                          v_pool_scale, v_xattn_norm, v_mem_norm, v_xattn_w_q, v_xattn_w_kv, v_xattn_w_o, v_ffn_norm,
                          v_ffn_w_up, v_ffn_conv_w, v_ffn_conv_b, v_ffn_w_down, v_final_norm)))

    S, D = x.shape[1], x.shape[2]
    n_layers = xattn_norm.shape[0]
    n_up = ffn_w_up.shape[2]
    qkv_w = attn_w_qkv.shape[2] * N_DEV
    n_heads = qkv_w // HEAD_DIM - 2 * N_KV_HEADS
    n_rot = (n_heads + N_KV_HEADS) * HEAD_DIM // LANES
    group_w = pool_w.shape[3]
    xs, mems, tgt = x[0], mem[0], loss_target[0]
    xi, yi, ci = _position()
    dev = 4 * xi + 2 * yi + ci
    pos = jnp.stack([ci, 2 * xi + yi]).astype(jnp.int32)

    layers = range(n_layers)
    n_groups = pool_w.shape[1]
    small_vec = jnp.concatenate([pool_norm.reshape(-1), pool_scale.reshape(-1), ffn_conv_w.reshape(-1)])
    small_rows = _round_up(-(-small_vec.size // PACK_W), 8)
    small_vec = jnp.pad(small_vec, (0, small_rows * PACK_W - small_vec.size)).reshape(small_rows, PACK_W)
    w_qkv, w_o, small, attn_token = _allgather_blocks(
        "allgather_attn", [attn_w_qkv[0].astype(BF16), attn_w_o[0].astype(BF16), small_vec])
    small = small.reshape(N_DEV, -1)
    w_qkv = w_qkv.transpose(1, 0, 2).reshape(D, qkv_w)
    w_o = w_o.reshape(-1, D)
    blocks = [pool_w.reshape(-1, group_w)] + [xattn_w_q[l] for l in layers] + [xattn_w_kv.reshape(n_layers * D, -1)]
    blocks += [xattn_w_o[l] for l in layers] + [ffn_w_up.reshape(n_layers * D, n_up)] + [ffn_w_down[l] for l in layers]
    blocks = [b.astype(BF16) for b in blocks]
    blocks[0] = _after(blocks[0], attn_token)
    n_blk = len(blocks)
    lands = [lax.dynamic_update_index_in_dim(lax.empty((N_DEV,) + b.shape, BF16), b, dev, 0) for b in blocks]
    plan_chips, plan_sibling = _plan_gather_chips(n_blk), _plan_gather_sibling(n_blk)
    gather_sems = _copies_start("gather_chips_start", blocks + lands, (4 * n_blk,), plan_chips)
    attn_norm_late = _after(attn_norm, gather_sems[3])

    d_sh = pool_norm.shape[1]
    pool_norm_f = small[:, :d_sh].reshape(1, D)
    pool_scale_f = small[:, d_sh:2 * d_sh].reshape(1, D)
    conv_w_f = small[:, 2 * d_sh:2 * d_sh + ffn_conv_w.size].reshape(N_DEV, n_layers, 3, n_up)
    conv_b_f = ffn_conv_b.reshape(n_layers, N_DEV, 1, n_up)

    cos, sin = _rope_tables(S)
    bd = _head_mean_matrix()
    pad_w = qkv_w - (n_heads + N_KV_HEADS) * HEAD_DIM
    qk_gain = jnp.concatenate([jnp.tile(attn_q_gain[0], n_heads), jnp.tile(attn_k_gain[0], N_KV_HEADS),
                               jnp.ones((pad_w,), F32)]).reshape(1, qkv_w)
    qk_scale = jnp.concatenate([jnp.full((n_heads * HEAD_DIM,), HEAD_DIM ** -0.5, F32),
                                jnp.ones((qkv_w - n_heads * HEAD_DIM,), F32)]).reshape(1, qkv_w)

    saved = []

    def xattn_ffn_fwd(l, xin, hx=None):
        if hx is None:
            hx = _rmsnorm(f"xattn_norm{l}", xin, xattn_norm[l:l + 1], BF16)
        memn = _rmsnorm(f"mem_norm{l}", mems, mem_norm[l:l + 1], BF16)
        qx = _mm_nn(f"xattn_q{l}", hx, w_xq[l], BF16)
        kv = _mm_nn_bs(f"xattn_kv{l}", memn, w_xkv, BF16, l)
        ox = _xattn_fwd(f"xattn_fwd{l}", qx, kv)
        x2 = _mm_nn(f"xattn_o{l}", ox, w_xo[l], F32, res=xin)
        hf = _rmsnorm(f"ffn_norm{l}", x2, ffn_norm[l:l + 1], BF16)
        u = _mm_nn_bs(f"ffn_up{l}", hf, w_up, F32, l).reshape(2, N_DEV // 2, S, n_up)
        cw = conv_w_f[:, l].reshape(2, N_DEV // 2, 3, n_up)
        cb = conv_b_f[l].reshape(2, N_DEV // 2, 1, n_up)
        act = _ffn_act(f"ffn_act{l}", u, cw, cb)
        x3 = _mm_nn_as(f"ffn_down{l}", act, w_down[l], F32, x2)
        saved.append(dict(xin=xin, hx=hx, memn=memn, qx=qx, kv=kv, ox=ox, x2=x2, hf=hf, u=u, cw=cw, cb=cb, act=act))
        return x3

    h0 = _rmsnorm("attn_norm", xs, attn_norm_late, BF16)
    qkv = _mm_nn("attn_qkv", h0, w_qkv, F32)
    qkr = _heads_major(_qk_rope("qk_rope", qkv, qk_gain, qk_scale, cos, sin, bd, n_rot), S)
    q_hm, k_hm, v_hm = qkr[:n_heads], qkr[n_heads:n_heads + N_KV_HEADS], qkr[n_heads + N_KV_HEADS:]
    o_hm, lse = _attn_fwd("attn_fwd", q_hm, k_hm, v_hm)
    o_att = _heads_minor(o_hm)
    arrived = _copies_wait("gather_chips_wait", gather_sems[2], gather_sems[0], gather_sems[1], plan_chips, o_att)
    pass_sems = _copies_start("gather_sibling_start", arrived[n_blk:], (3 * n_blk,), plan_sibling)
    x1 = _mm_nn("attn_o", o_att, _after(w_o, pass_sems[3]), F32, res=xs)
    hx0 = _rmsnorm("xattn_norm0", x1, xattn_norm[0:1], BF16)
    gathered = iter(_copies_wait("gather_sibling_wait", pass_sems[2], pass_sems[0], pass_sems[1], plan_sibling, hx0))
    w_pool = (next(gathered).reshape(N_DEV, n_groups, -1, group_w).transpose(1, 0, 2, 3)
              .reshape(n_groups, group_w, group_w))
    w_xq = [next(gathered).reshape(D, D) for l in layers]
    w_xkv = next(gathered)
    w_xo = [next(gathered).reshape(D, D) for l in layers]
    w_up = next(gathered)
    w_down = [next(gathered).reshape(-1, D) for l in layers]
    x3 = xattn_ffn_fwd(0, x1, hx0)
    hp = _rmsnorm("pool_norm", x3, pool_norm_f, F32)
    mixed = _pool_window("pool_window", hp, group_w, False, BF16)
    x4 = _pool_proj("pool_proj", mixed, w_pool, pool_scale_f, x3)
    x6 = xattn_ffn_fwd(1, x4)

    G = {}
    g, d_final, lvec = _loss_head("loss_head", x6, final_norm.reshape(1, D), tgt)
    G['final_norm'] = d_final.reshape(D)
    loss_part = (0.5 * jnp.sum(lvec) / D).reshape(1)

    d_xn, d_mn, d_fn, d_xq, d_xkv, d_xo, d_up, d_cw, d_cb, d_down = ([None] * n_layers for _ in range(10))

    def xattn_ffn_bwd(l, g, conv_b_late=None, after_act=None):
        sv = saved[l]
        d_act = _mm_nt_bs(f"ffn_down_dx{l}", g, w_down[l], N_DEV // 2, F32)
        d_down[l] = _mm_tn_as(f"ffn_down_dw{l}", sv['act'], g, F32)
        du, st = _ffn_act_bwd(f"ffn_act_bwd{l}", sv['u'], d_act, sv['cw'],
                              sv['cb'] if conv_b_late is None else conv_b_late)
        du = du.reshape(N_DEV, S, n_up)
        ffn_gain = ffn_norm[l:l + 1] if after_act is None else _after(ffn_norm[l:l + 1], after_act(du))
        st = st.reshape(N_DEV, 8, n_up)
        d_cw[l], d_cb[l] = st[:, 0:3], st[:, 3].reshape(-1)
        d_up[l] = _mm_tn_bs(f"ffn_up_dw{l}", sv['hf'], du, F32)
        dhf = _mm_nt_abs(f"ffn_up_dx{l}", du, w_up, D, F32, l)
        g, d_fn[l] = _rmsnorm_bwd(f"ffn_norm_bwd{l}", sv['x2'], ffn_gain, dhf, g)
        d_xo[l] = _mm_tn(f"xattn_o_dw{l}", sv['ox'], g, F32)
        do = _mm_nt(f"xattn_o_dx{l}", g, w_xo[l], BF16)
        dq, dkv = _xattn_bwd(f"xattn_bwd{l}", sv['qx'], sv['kv'], do)
        d_xq[l] = _mm_tn(f"xattn_q_dw{l}", sv['hx'], dq, F32)
        d_xkv[l] = _mm_tn_bs(f"xattn_kv_dw{l}", sv['memn'], dkv, F32)
        dmemn = _mm_nt_abs(f"xattn_kv_dx{l}", dkv, w_xkv, D, F32, l)
        _, d_mn[l] = _rmsnorm_bwd(f"mem_norm_bwd{l}", mems, mem_norm[l:l + 1], dmemn)
        dhx = _mm_nt(f"xattn_q_dx{l}", dq, w_xq[l], F32)
        g, d_xn[l] = _rmsnorm_bwd(f"xattn_norm_bwd{l}", sv['xin'], xattn_norm[l:l + 1], dhx, g)
        return g

    def reduce_start(tag, bufs):
        n = len(bufs)
        g4s = [b.reshape((N_DEV // 2, 2) + b.shape[1:]) for b in bufs]
        lands = [lax.empty((N_DEV // 2,) + b.shape[1:], F32) for b in bufs]
        plan = _plan_reduce_sibling(n)
        return (n, plan) + _copies_start(f"reduce_sibling_start_{tag}", g4s + lands, (N_DEV // 2 * n,), plan)

    def reduce_between(tag, state, after):
        n, plan, send_sems, recv_sems, thru, _ = state
        got = _copies_wait(f"reduce_sibling_wait_{tag}", thru, send_sems, recv_sems, plan, after)
        sums = [_add_sibling(f"reduce_add_sibling_{tag}{i}", got[i], got[n + i], pos) for i in range(n)]
        lands = [lax.empty((3,) + tb.shape[1:], BF16) for tb, _ in sums]
        plan = _plan_reduce_chips(n)
        return (n, plan, [own for _, own in sums]) + _copies_start(f"reduce_chips_start_{tag}",
                                                                    [tb for tb, _ in sums] + lands, (3 * n,), plan)

    def reduce_finish(tag, state, after):
        n, plan, owns, send_sems, recv_sems, thru, _ = state
        got = _copies_wait(f"reduce_chips_wait_{tag}", thru, send_sems, recv_sems, plan, after)
        return [_add_chips(f"reduce_add_chips_{tag}{i}", owns[i], got[n + i]) for i in range(n)]

    def layer_bufs(l):
        return [d_xq[l].reshape(N_DEV, -1, D), d_xkv[l], d_xo[l].reshape(N_DEV, -1, D), d_up[l],
                d_down[l].reshape(N_DEV, -1, D)]

    g = xattn_ffn_bwd(1, g)
    d_mixed, d_pool_w, d_pool_scale = _pool_proj_bwd("pool_proj_bwd", g, mixed, w_pool, pool_scale_f)
    dhp = _pool_window("pool_window_bwd", d_mixed, group_w, True, F32)
    g, d_pool_norm = _rmsnorm_bwd("pool_norm_bwd", x3, pool_norm_f, dhp, g)
    upper = reduce_start("upper", [d_pool_w.reshape(n_groups, N_DEV, -1, group_w).transpose(1, 0, 2, 3)
                                   .reshape(N_DEV, -1, group_w)] + layer_bufs(1))
    between = []

    def upper_between(du):
        between.append(reduce_between("upper", upper, du))
        return between[0][-1]

    g = xattn_ffn_bwd(0, g, _after(saved[0]['cb'], upper[-1]), upper_between)
    d_wo = _mm_tn("attn_o_dw", o_att, g, F32)
    lower = reduce_start("lower", layer_bufs(0) + [d_wo.reshape(N_DEV, -1, D)])
    do = _mm_nt("attn_o_dx", g, _after(w_o, lower[-1]), BF16)
    lower = reduce_between("lower", lower, do)
    do_hm = _heads_major(_after(do, lower[-1]), S)
    dq_hm, dk_hm, dv_hm = _attn_bwd("attn_bwd", q_hm, k_hm, v_hm, o_hm, lse, do_hm)
    red_lower = reduce_finish("lower", lower, dq_hm)
    d_qkr = _heads_minor(jnp.concatenate([dq_hm, dk_hm, dv_hm], axis=0))
    d_qkv, d_gain = _qk_rope_bwd("qk_rope_bwd", d_qkr, qkv, qk_gain, qk_scale, cos, sin, bd, n_rot)
    red_upper = reduce_finish("upper", between[0], d_qkv)
    d_wqkv = _mm_tn("attn_qkv_dw", h0, d_qkv, F32)
    last = reduce_start("last", [d_wqkv.reshape(D, N_DEV, -1).transpose(1, 0, 2)])
    dh0 = _mm_nt("attn_qkv_dx", d_qkv, _after(w_qkv, last[-1]), F32)
    last = reduce_between("last", last, dh0)
    grad_x, d_attn_norm = _rmsnorm_bwd("attn_norm_bwd", xs, _after(attn_norm, last[-1]), dh0, g)
    G['attn_w_o'] = red_lower[-1][None]
    G['pool_w'] = red_upper[0].reshape(pool_w.shape)
    per_layer = [red_lower, red_upper[1:]]
    for i, n in enumerate(['xattn_w_q', 'xattn_w_kv', 'xattn_w_o', 'ffn_w_up', 'ffn_w_down']):
        G[n] = jnp.stack([per_layer[l][i] for l in layers])

    hq = n_heads * HEAD_DIM
    small_g = {'attn_norm': d_attn_norm, 'attn_q_gain': d_gain[0, :hq].reshape(n_heads, HEAD_DIM).sum(0),
               'attn_k_gain': d_gain[0, hq:hq + N_KV_HEADS * HEAD_DIM].reshape(N_KV_HEADS, HEAD_DIM).sum(0),
               'pool_norm': d_pool_norm, 'pool_scale': d_pool_scale,
               'xattn_norm': jnp.concatenate(d_xn), 'mem_norm': jnp.concatenate(d_mn), 'ffn_norm': jnp.concatenate(d_fn),
               'ffn_conv_w': jnp.stack(d_cw, axis=1), 'ffn_conv_b': jnp.stack(d_cb)}
    order = list(small_g)
    flat = jnp.concatenate([loss_part] + [small_g[n].reshape(-1) for n in order] + [G['final_norm']])
    ar_rows = _round_up(-(-flat.size // PACK_W), 8)
    flat = jnp.pad(flat, (0, ar_rows * PACK_W - flat.size)).reshape(ar_rows, PACK_W)
    summed = _sum_slots("allreduce_sum", _allgather_small("allreduce_gather", flat)).reshape(-1)
    loss = summed[0]
    G['attn_w_qkv'] = reduce_finish("last", last, summed)[0][None]
    at = 1
    for n in order + ['final_norm']:
        size = G['final_norm'].size if n == 'final_norm' else small_g[n].size
        piece = summed[at:at + size]
        at += size
        if n in ('pool_norm', 'pool_scale'):
            piece = lax.dynamic_slice(piece, (dev * d_sh,), (d_sh,))
        elif n == 'ffn_conv_w':
            piece = lax.dynamic_index_in_dim(piece.reshape(N_DEV, n_layers, 3, n_up), dev, 0, keepdims=False)
        G[n] = piece.reshape(W[n].shape)

    deltas, new_m, new_v = [], [], []
    for n in names:
        d, nm, nv = _adamw(f"adamw_{n}", W[n], G[n], Mo[n], Vo[n])
        deltas.append(d)
        new_m.append(nm)
        new_v.append(nv)
    return (loss, grad_x[None], *[G[n] for n in names], *deltas, *new_m, *new_v)
```

```python
import jax
import jax.numpy as jnp
from jax import lax
from jax.experimental import pallas as pl
from jax.experimental.pallas import tpu as pltpu

F32 = jnp.float32
BF16 = jnp.bfloat16
MESH = pl.DeviceIdType.MESH

N_DEV = 8
EPS = 1e-6
HEAD_DIM = 64
N_KV_HEADS = 4
X_HEADS = 4
GRID_W = 64
ROPE_THETA = 10000.0
ROPE_PAIRS = HEAD_DIM // 4
POOL_WINDOWS = (2, 4, 8, 16)
POOL_PAD = 16
KEY_CHUNK = 1024
MM_ROWS = 1024
LANES = 128
PACK_W = 1024
ADAM_LR, ADAM_B1, ADAM_B2, ADAM_EPS, ADAM_WD, ADAM_STEP = 0.001, 0.9, 0.999, 1e-08, 0.01, 10

_NN = (((1,), (0,)), ((), ()))
_NT = (((1,), (1,)), ((), ()))
_TN = (((0,), (0,)), ((), ()))


def _pc(body, *, name, **kw):
    return pl.pallas_call(body, name=name, **kw)


def _sem(*kinds):
    return pltpu.CompilerParams(dimension_semantics=kinds)


def _tile(n, pref, mult):
    best = None
    for t in range(mult, min(n, pref) + 1, mult):
        if n % t == 0:
            best = t
    return n if best is None else best


def _round_up(n, m):
    return (n + m - 1) // m * m


def _mm_call(name, a, b, dims, grid, a_spec, b_spec, o_spec, out_shape, kaxis, res=None, res_spec=None):
    nk = grid[kaxis]
    acc_shape = tuple(d for d in o_spec.block_shape if d is not None)
    in_place = out_shape.dtype == F32 and res is None
    use_scratch = nk > 1 and not in_place

    def body(*refs):
        refs = list(refs)
        acc = refs.pop() if use_scratch else None
        a_ref, b_ref = refs[:2]
        r_ref = refs[2] if res is not None else None
        o_ref = refs[-1]
        prod = lax.dot_general(a_ref[...].astype(BF16), b_ref[...].astype(BF16), dims, preferred_element_type=F32)
        if nk == 1:
            if r_ref is not None:
                prod = prod + r_ref[...]
            o_ref[...] = prod.astype(o_ref.dtype)
            return
        k = pl.program_id(kaxis)
        tgt = o_ref if in_place else acc

        @pl.when(k == 0)
        def _():
            tgt[...] = prod

        @pl.when(k > 0)
        def _():
            tgt[...] += prod

        if not in_place:
            @pl.when(k == nk - 1)
            def _():
                r = acc[...]
                if r_ref is not None:
                    r = r + r_ref[...]
                o_ref[...] = r.astype(o_ref.dtype)

    sem = tuple("arbitrary" if ax == kaxis else "parallel" for ax in range(len(grid)))
    ins = [a, b] if res is None else [a, b, res]
    specs = [a_spec, b_spec] if res is None else [a_spec, b_spec, res_spec]
    return _pc(body, name=name, grid=grid, in_specs=specs, out_specs=o_spec, out_shape=out_shape,
               scratch_shapes=[pltpu.VMEM(acc_shape, F32)] if use_scratch else [],
               compiler_params=_sem(*sem))(*ins)


def _mm_nn(name, a, b, out_dtype, res=None):
    M, K = a.shape
    N = b.shape[1]
    tm, tn, tk = _tile(M, MM_ROWS, 16), _tile(N, 1024, LANES), _tile(K, 1024, LANES)
    return _mm_call(name, a, b, _NN, (M // tm, N // tn, K // tk),
                    pl.BlockSpec((tm, tk), lambda i, j, k: (i, k)),
                    pl.BlockSpec((tk, tn), lambda i, j, k: (k, j)),
                    pl.BlockSpec((tm, tn), lambda i, j, k: (i, j)),
                    jax.ShapeDtypeStruct((M, N), out_dtype), 2, res,
                    pl.BlockSpec((tm, tn), lambda i, j, k: (i, j)))


def _mm_nt(name, a, b, out_dtype):
    M, K = a.shape
    N = b.shape[0]
    tm, tn, tk = _tile(M, MM_ROWS, 16), _tile(N, 1024, LANES), _tile(K, 1024, LANES)
    return _mm_call(name, a, b, _NT, (M // tm, N // tn, K // tk),
                    pl.BlockSpec((tm, tk), lambda i, j, k: (i, k)),
                    pl.BlockSpec((tn, tk), lambda i, j, k: (j, k)),
                    pl.BlockSpec((tm, tn), lambda i, j, k: (i, j)),
                    jax.ShapeDtypeStruct((M, N), out_dtype), 2)


def _mm_tn(name, a, b, out_dtype):
    R, M = a.shape
    N = b.shape[1]
    tm, tn, tr = _tile(M, 1024, LANES), _tile(N, 1024, LANES), _tile(R, MM_ROWS, 16)
    return _mm_call(name, a, b, _TN, (M // tm, N // tn, R // tr),
                    pl.BlockSpec((tr, tm), lambda i, j, k: (k, i)),
                    pl.BlockSpec((tr, tn), lambda i, j, k: (k, j)),
                    pl.BlockSpec((tm, tn), lambda i, j, k: (i, j)),
                    jax.ShapeDtypeStruct((M, N), out_dtype), 2)


def _mm_nn_bs(name, a, b, out_dtype, layer=0):
    M, K = a.shape
    J, _, n = b.shape
    tm, tk = _tile(M, MM_ROWS, 16), _tile(K, 1024, LANES)
    first = layer * (K // tk)
    return _mm_call(name, a, b, _NN, (J, M // tm, K // tk),
                    pl.BlockSpec((tm, tk), lambda j, i, k: (i, k)),
                    pl.BlockSpec((None, tk, n), lambda j, i, k: (j, first + k, 0)),
                    pl.BlockSpec((None, tm, n), lambda j, i, k: (j, i, 0)),
                    jax.ShapeDtypeStruct((J, M, n), out_dtype), 2)


def _mm_nt_bs(name, a, b, J, out_dtype):
    M, K = a.shape
    n = b.shape[0] // J
    tm, tk = _tile(M, MM_ROWS, 16), _tile(K, 1024, LANES)
    return _mm_call(name, a, b, _NT, (J, M // tm, K // tk),
                    pl.BlockSpec((tm, tk), lambda j, i, k: (i, k)),
                    pl.BlockSpec((n, tk), lambda j, i, k: (j, k)),
                    pl.BlockSpec((None, tm, n), lambda j, i, k: (j, i, 0)),
                    jax.ShapeDtypeStruct((J, M, n), out_dtype), 2)


def _mm_nn_as(name, a, b, out_dtype, res):
    J, M, n = a.shape
    N = b.shape[1]
    tm, tn = _tile(M, MM_ROWS, 16), _tile(N, 1024, LANES)
    return _mm_call(name, a, b, _NN, (M // tm, N // tn, J),
                    pl.BlockSpec((None, tm, n), lambda i, j, k: (k, i, 0)),
                    pl.BlockSpec((n, tn), lambda i, j, k: (k, j)),
                    pl.BlockSpec((tm, tn), lambda i, j, k: (i, j)),
                    jax.ShapeDtypeStruct((M, N), out_dtype), 2, res,
                    pl.BlockSpec((tm, tn), lambda i, j, k: (i, j)))


def _mm_nt_abs(name, a, b, N, out_dtype, layer=0):
    J, M, n = a.shape
    tm, tn = _tile(M, MM_ROWS, 16), _tile(N, 1024, LANES)
    first = layer * (N // tn)
    return _mm_call(name, a, b, _NT, (M // tm, N // tn, J),
                    pl.BlockSpec((None, tm, n), lambda i, j, k: (k, i, 0)),
                    pl.BlockSpec((None, tn, n), lambda i, j, k: (k, first + j, 0)),
                    pl.BlockSpec((tm, tn), lambda i, j, k: (i, j)),
                    jax.ShapeDtypeStruct((M, N), out_dtype), 2)


def _mm_tn_bs(name, a, b, out_dtype):
    R, M = a.shape
    J, _, n = b.shape
    tm, tr = _tile(M, 1024, LANES), _tile(R, MM_ROWS, 16)
    return _mm_call(name, a, b, _TN, (J, M // tm, R // tr),
                    pl.BlockSpec((tr, tm), lambda j, i, k: (k, i)),
                    pl.BlockSpec((None, tr, n), lambda j, i, k: (j, k, 0)),
                    pl.BlockSpec((None, tm, n), lambda j, i, k: (j, i, 0)),
                    jax.ShapeDtypeStruct((J, M, n), out_dtype), 2)


def _mm_tn_as(name, a, b, out_dtype):
    J, R, n = a.shape
    N = b.shape[1]
    tn, tr = _tile(N, 1024, LANES), _tile(R, MM_ROWS, 16)
    return _mm_call(name, a, b, _TN, (J, N // tn, R // tr),
                    pl.BlockSpec((None, tr, n), lambda j, jn, k: (j, k, 0)),
                    pl.BlockSpec((tr, tn), lambda j, jn, k: (k, jn)),
                    pl.BlockSpec((n, tn), lambda j, jn, k: (j, jn)),
                    jax.ShapeDtypeStruct((J * n, N), out_dtype), 2)


def _rmsnorm(name, x, g, out_dtype):
    R, D = x.shape
    tm = _tile(R, 512, 16)

    def body(x_ref, g_ref, o_ref):
        xv = x_ref[...]
        r = lax.rsqrt(jnp.mean(xv * xv, axis=-1, keepdims=True) + EPS)
        o_ref[...] = (xv * r * g_ref[...]).astype(o_ref.dtype)

    return _pc(body, name=name, grid=(R // tm,),
               in_specs=[pl.BlockSpec((tm, D), lambda i: (i, 0)), pl.BlockSpec((1, D), lambda i: (0, 0))],
               out_specs=pl.BlockSpec((tm, D), lambda i: (i, 0)),
               out_shape=jax.ShapeDtypeStruct((R, D), out_dtype), compiler_params=_sem("parallel"))(x, g)


def _rmsnorm_bwd(name, x, g, dh, dres=None):
    R, D = x.shape
    tm = _tile(R, 512, 16)

    def body(*refs):
        if dres is None:
            x_ref, g_ref, dh_ref, dx_ref, dg_ref = refs
            dres_ref = None
        else:
            x_ref, g_ref, dh_ref, dres_ref, dx_ref, dg_ref = refs
        xv = x_ref[...]
        r = lax.rsqrt(jnp.mean(xv * xv, axis=-1, keepdims=True) + EPS)
        xh = xv * r
        dhv = dh_ref[...].astype(F32)

        @pl.when(pl.program_id(0) == 0)
        def _():
            dg_ref[...] = jnp.zeros_like(dg_ref)

        dg_ref[...] += jnp.sum(dhv * xh, axis=0, keepdims=True)
        dxh = dhv * g_ref[...]
        dx = r * (dxh - xh * jnp.mean(dxh * xh, axis=-1, keepdims=True))
        if dres_ref is not None:
            dx = dx + dres_ref[...]
        dx_ref[...] = dx

    row = pl.BlockSpec((tm, D), lambda i: (i, 0))
    vec = pl.BlockSpec((1, D), lambda i: (0, 0))
    ins = [x, g, dh] + ([] if dres is None else [dres])
    specs = [row, vec, row] + ([] if dres is None else [row])
    return _pc(body, name=name, grid=(R // tm,), in_specs=specs, out_specs=(row, vec),
               out_shape=(jax.ShapeDtypeStruct((R, D), F32), jax.ShapeDtypeStruct((1, D), F32)),
               compiler_params=_sem("arbitrary"))(*ins)


def _loss_head(name, x, g, tgt):
    R, D = x.shape
    tm = _tile(R, 512, 16)

    def body(x_ref, g_ref, t_ref, dx_ref, dg_ref, l_ref):
        xv = x_ref[...]
        r = lax.rsqrt(jnp.mean(xv * xv, axis=-1, keepdims=True) + EPS)
        xh = xv * r
        err = xh * g_ref[...] - t_ref[...]

        @pl.when(pl.program_id(0) == 0)
        def _():
            dg_ref[...] = jnp.zeros_like(dg_ref)
            l_ref[...] = jnp.zeros_like(l_ref)

        l_ref[...] += jnp.sum(err * err, axis=0, keepdims=True)
        dy = err * (1.0 / D)
        dg_ref[...] += jnp.sum(dy * xh, axis=0, keepdims=True)
        dxh = dy * g_ref[...]
        dx_ref[...] = r * (dxh - xh * jnp.mean(dxh * xh, axis=-1, keepdims=True))

    row = pl.BlockSpec((tm, D), lambda i: (i, 0))
    vec = pl.BlockSpec((1, D), lambda i: (0, 0))
    return _pc(body, name=name, grid=(R // tm,), in_specs=[row, vec, row], out_specs=(row, vec, vec),
               out_shape=(jax.ShapeDtypeStruct((R, D), F32), jax.ShapeDtypeStruct((1, D), F32),
                          jax.ShapeDtypeStruct((1, D), F32)),
               compiler_params=_sem("arbitrary"))(x, g, tgt)


def _rope_tables(S):
    n_rows = S // GRID_W
    row = jnp.repeat(jnp.arange(n_rows, dtype=F32), GRID_W)
    col = jnp.tile(jnp.arange(GRID_W, dtype=F32), n_rows)
    inv_freq = ROPE_THETA ** (-jnp.arange(ROPE_PAIRS, dtype=F32) / ROPE_PAIRS)
    ang = jnp.stack([row[:, None] * inv_freq, col[:, None] * inv_freq], axis=1)
    cos, sin = jnp.cos(ang), jnp.sin(ang)
    c = jnp.broadcast_to(cos[:, :, None, :], (S, 2, 2, ROPE_PAIRS)).reshape(S, HEAD_DIM)
    s = jnp.stack([-sin, sin], axis=2).reshape(S, HEAD_DIM)
    reps = LANES // HEAD_DIM
    return jnp.tile(c, (1, reps)), jnp.tile(s, (1, reps))


def _head_mean_matrix():
    h = jnp.arange(LANES) // HEAD_DIM
    m = jnp.where(h[:, None] == h[None, :], 1.0 / HEAD_DIM, 0.0).astype(BF16)
    return jnp.concatenate([m, m], axis=0)


def _head_mean(v, bd):
    hi = v.astype(BF16)
    lo = (v - hi.astype(F32)).astype(BF16)
    return jnp.dot(jnp.concatenate([hi, lo], axis=1), bd, preferred_element_type=F32)


def _swap_halves(y):
    lane = lax.broadcasted_iota(jnp.int32, y.shape, 1)
    return jnp.where(lane % 32 < 16, pltpu.roll(y, LANES - 16, 1), pltpu.roll(y, 16, 1))


def _qk_rope(name, qkv, gain, scale, cos, sin, bd, n_rot):
    S, W = qkv.shape
    tm = _tile(S, 2048, 16)
    per = LANES // HEAD_DIM

    def body(x_ref, g_ref, s_ref, c_ref, sn_ref, bd_ref, o_ref):
        j = pl.program_id(1)
        xv = x_ref[...]

        def put(v):
            for h in range(per):
                o_ref[h] = v[:, h * HEAD_DIM:(h + 1) * HEAD_DIM].astype(BF16)

        @pl.when(j < n_rot)
        def _():
            ms = _head_mean(xv * xv, bd_ref[...])
            y = xv * lax.rsqrt(ms + EPS) * g_ref[...] * s_ref[...]
            put(y * c_ref[...] + _swap_halves(y) * sn_ref[...])

        @pl.when(j >= n_rot)
        def _():
            put(xv)

    blk = pl.BlockSpec((tm, LANES), lambda i, j: (i, j))
    vec = pl.BlockSpec((1, LANES), lambda i, j: (0, j))
    tab = pl.BlockSpec((tm, LANES), lambda i, j: (i, 0))
    return _pc(body, name=name, grid=(S // tm, W // LANES),
               in_specs=[blk, vec, vec, tab, tab, pl.BlockSpec((2 * LANES, LANES), lambda i, j: (0, 0))],
               out_specs=pl.BlockSpec((per, tm, HEAD_DIM), lambda i, j: (j, i, 0)),
               out_shape=jax.ShapeDtypeStruct((W // HEAD_DIM, S, HEAD_DIM), BF16),
               compiler_params=_sem("parallel", "parallel"))(qkv, gain, scale, cos, sin, bd)


def _qk_rope_bwd(name, dq, dk, dv, qkv, gain, scale, cos, sin, bd):
    S, W = qkv.shape
    tm = _tile(S, 2048, 16)
    per = LANES // HEAD_DIM
    nq, nk, nv = dq.shape[0] // per, dk.shape[0] // per, dv.shape[0] // per
    n_rot = nq + nk

    def body(dq_ref, dk_ref, dv_ref, x_ref, g_ref, s_ref, c_ref, sn_ref, bd_ref, dx_ref, dg_ref):
        j, i = pl.program_id(0), pl.program_id(1)

        @pl.when(i == 0)
        def _():
            dg_ref[...] = jnp.zeros_like(dg_ref)

        def rotate_back(d_ref):
            dv = jnp.concatenate([d_ref[h] for h in range(per)], axis=1)
            xv = x_ref[...]
            ms = _head_mean(xv * xv, bd_ref[...])
            r = lax.rsqrt(ms + EPS)
            z = xv * r
            dy = (dv * c_ref[...] - _swap_halves(dv) * sn_ref[...]) * s_ref[...]
            dg_ref[...] += jnp.sum(dy * z, axis=0, keepdims=True)
            dz = dy * g_ref[...]
            mz = _head_mean(dz * z, bd_ref[...])
            dx_ref[...] = (r * (dz - z * mz)).astype(BF16)

        @pl.when(j < nq)
        def _():
            rotate_back(dq_ref)

        @pl.when((j >= nq) & (j < n_rot))
        def _():
            rotate_back(dk_ref)

        @pl.when(j >= n_rot)
        def _():
            dx_ref[...] = jnp.concatenate([dv_ref[h] for h in range(per)], axis=1).astype(BF16)

    def part(first, count):
        return pl.BlockSpec((per, tm, HEAD_DIM), lambda j, i: (jnp.clip(j - first, 0, count - 1), i, 0))

    blk = pl.BlockSpec((tm, LANES), lambda j, i: (i, j))
    vec = pl.BlockSpec((1, LANES), lambda j, i: (0, j))
    tab = pl.BlockSpec((tm, LANES), lambda j, i: (i, 0))
    return _pc(body, name=name, grid=(W // LANES, S // tm),
               in_specs=[part(0, nq), part(nq, nk), part(n_rot, nv), blk, vec, vec, tab, tab,
                         pl.BlockSpec((2 * LANES, LANES), lambda j, i: (0, 0))],
               out_specs=(blk, vec),
               out_shape=(jax.ShapeDtypeStruct((S, W), BF16), jax.ShapeDtypeStruct((1, W), F32)),
               compiler_params=_sem("parallel", "arbitrary"))(dq, dk, dv, qkv, gain, scale, cos, sin, bd)


def _softmax_rows(s):
    m = jnp.max(s, axis=-1, keepdims=True)
    p = jnp.exp(s - m)
    return p, jnp.sum(p, axis=-1, keepdims=True)


def _attn_fwd(name, qkv, H):
    _, S, dh = qkv.shape
    G = H // N_KV_HEADS
    tq = _tile(S, 128, 16)

    kc = _tile(S, KEY_CHUNK, LANES)
    R = G * tq

    def body(q_ref, k_ref, v_ref, o_ref, lse_ref):
        q = q_ref[...].reshape(R, dh)
        m = jnp.full((R, 1), -1e30, F32)
        l = jnp.zeros((R, 1), F32)
        acc = jnp.zeros((R, dh), F32)
        for c in range(S // kc):
            rows = slice(c * kc, (c + 1) * kc)
            s = lax.dot_general(q, k_ref[rows, :], _NT, preferred_element_type=F32)
            m_new = jnp.maximum(m, jnp.max(s, axis=-1, keepdims=True))
            alpha = jnp.exp(m - m_new)
            p = jnp.exp(s - m_new)
            l = alpha * l + jnp.sum(p, axis=-1, keepdims=True)
            acc = alpha * acc + jnp.dot(p.astype(BF16), v_ref[rows, :], preferred_element_type=F32)
            m = m_new
        o_ref[...] = (acc / l).astype(BF16).reshape(G, tq, dh)
        lse_ref[...] = (m + jnp.log(l)).reshape(G, tq, 1)

    qs = pl.BlockSpec((G, tq, dh), lambda kv, i: (kv, i, 0))
    ls = pl.BlockSpec((G, tq, 1), lambda kv, i: (kv, i, 0))
    ks = pl.BlockSpec((None, S, dh), lambda kv, i: (H + kv, 0, 0))
    vs = pl.BlockSpec((None, S, dh), lambda kv, i: (H + N_KV_HEADS + kv, 0, 0))
    return _pc(body, name=name, grid=(N_KV_HEADS, S // tq), in_specs=[qs, ks, vs], out_specs=(qs, ls),
               out_shape=(jax.ShapeDtypeStruct((H, S, dh), BF16), jax.ShapeDtypeStruct((H, S, 1), F32)),
               compiler_params=_sem("parallel", "parallel"))(qkv, qkv, qkv)


def _attn_bwd(name, qkv, o, lse, do):
    H, S, dh = o.shape
    G = H // N_KV_HEADS
    tq = _tile(S, 128, 16)
    kc = _tile(S, KEY_CHUNK, LANES)
    R = G * tq

    def body(q_ref, k_ref, v_ref, o_ref, lse_ref, do_ref, dq_ref, dk_ref, dv_ref):
        @pl.when(pl.program_id(1) == 0)
        def _():
            dk_ref[...] = jnp.zeros_like(dk_ref)
            dv_ref[...] = jnp.zeros_like(dv_ref)

        qq, dd = q_ref[...].reshape(R, dh), do_ref[...].reshape(R, dh)
        delta = jnp.sum(dd.astype(F32) * o_ref[...].reshape(R, dh).astype(F32), axis=-1, keepdims=True)
        lse = lse_ref[...].reshape(R, 1)
        dq = jnp.zeros((R, dh), F32)
        for c in range(S // kc):
            rows = slice(c * kc, (c + 1) * kc)
            kk, vv = k_ref[rows, :], v_ref[rows, :]
            p = jnp.exp(lax.dot_general(qq, kk, _NT, preferred_element_type=F32) - lse)
            dv_ref[rows, :] += lax.dot_general(p.astype(BF16), dd, _TN, preferred_element_type=F32)
            dp = lax.dot_general(dd, vv, _NT, preferred_element_type=F32)
            ds = (p * (dp - delta)).astype(BF16)
            dq = dq + jnp.dot(ds, kk, preferred_element_type=F32)
            dk_ref[rows, :] += lax.dot_general(ds, qq, _TN, preferred_element_type=F32)
        dq_ref[...] = dq.reshape(G, tq, dh)

    qs = pl.BlockSpec((G, tq, dh), lambda kv, i: (kv, i, 0))
    ls = pl.BlockSpec((G, tq, 1), lambda kv, i: (kv, i, 0))
    ks = pl.BlockSpec((None, S, dh), lambda kv, i: (H + kv, 0, 0))
    vs = pl.BlockSpec((None, S, dh), lambda kv, i: (H + N_KV_HEADS + kv, 0, 0))
    acc = pl.BlockSpec((None, S, dh), lambda kv, i: (kv, 0, 0))
    return _pc(body, name=name, grid=(N_KV_HEADS, S // tq), in_specs=[qs, ks, vs, qs, ls, qs],
               out_specs=(qs, acc, acc),
               out_shape=(jax.ShapeDtypeStruct((H, S, dh), F32), jax.ShapeDtypeStruct((N_KV_HEADS, S, dh), F32),
                          jax.ShapeDtypeStruct((N_KV_HEADS, S, dh), F32)),
               compiler_params=_sem("parallel", "arbitrary"))(qkv, qkv, qkv, o, lse, do)


def _xattn_fwd(name, q, kv):
    S, D = q.shape
    _, M, dh = kv.shape
    scale = dh ** -0.5
    tq = _tile(S, 256, 16)

    def body(q_ref, kv_ref, o_ref):
        for h in range(X_HEADS):
            lo, hi = h * dh, (h + 1) * dh
            s = lax.dot_general(q_ref[:, lo:hi], kv_ref[h], _NT, preferred_element_type=F32) * scale
            p, l = _softmax_rows(s)
            o = jnp.dot(p.astype(BF16), kv_ref[X_HEADS + h], preferred_element_type=F32)
            o_ref[:, lo:hi] = (o / l).astype(BF16)

    row = pl.BlockSpec((tq, D), lambda i: (i, 0))
    return _pc(body, name=name, grid=(S // tq,),
               in_specs=[row, pl.BlockSpec((2 * X_HEADS, M, dh), lambda i: (0, 0, 0))],
               out_specs=row, out_shape=jax.ShapeDtypeStruct((S, D), BF16),
               compiler_params=_sem("parallel"))(q, kv)


def _xattn_bwd(name, q, kv, do):
    S, D = q.shape
    _, M, dh = kv.shape
    scale = dh ** -0.5
    tq = _tile(S, 256, 16)

    def body(q_ref, kv_ref, do_ref, dq_ref, dkv_ref):
        @pl.when(pl.program_id(0) == 0)
        def _():
            dkv_ref[...] = jnp.zeros_like(dkv_ref)

        for h in range(X_HEADS):
            lo, hi = h * dh, (h + 1) * dh
            qh, kh, vh, doh = q_ref[:, lo:hi], kv_ref[h], kv_ref[X_HEADS + h], do_ref[:, lo:hi]
            s = lax.dot_general(qh, kh, _NT, preferred_element_type=F32) * scale
            p, l = _softmax_rows(s)
            pn = p / l
            dkv_ref[X_HEADS + h] += lax.dot_general(pn.astype(BF16), doh, _TN, preferred_element_type=F32)
            dp = lax.dot_general(doh, vh, _NT, preferred_element_type=F32)
            ds = (pn * (dp - jnp.sum(pn * dp, axis=-1, keepdims=True)) * scale).astype(BF16)
            dq_ref[:, lo:hi] = jnp.dot(ds, kh, preferred_element_type=F32).astype(BF16)
            dkv_ref[h] += lax.dot_general(ds, qh, _TN, preferred_element_type=F32)

    row = pl.BlockSpec((tq, D), lambda i: (i, 0))
    full = pl.BlockSpec((2 * X_HEADS, M, dh), lambda i: (0, 0, 0))
    return _pc(body, name=name, grid=(S // tq,), in_specs=[row, full, row], out_specs=(row, full),
               out_shape=(jax.ShapeDtypeStruct((S, D), BF16), jax.ShapeDtypeStruct((2 * X_HEADS, M, dh), F32)),
               compiler_params=_sem("arbitrary"))(q, kv, do)


def _sigmoid(x):
    return 1.0 / (1.0 + jnp.exp(-x))


def _halo_specs(tm, n, S):
    nb = tm // 8
    last8 = S // 8 - 1
    main = pl.BlockSpec((2, None, tm, n), lambda j, i: (0, j, i, 0))
    prev = pl.BlockSpec((2, None, 8, n), lambda j, i: (0, j, jnp.maximum(i * nb - 1, 0), 0))
    nxt = pl.BlockSpec((2, None, 8, n), lambda j, i: (0, j, jnp.minimum((i + 1) * nb, last8), 0))
    return main, prev, nxt


def _ffn_act(name, u, cw, cb):
    _, J, S, n = u.shape
    tm = _tile(S, 256, 16)
    nblk = S // tm

    def body(u_ref, up_ref, un_ref, w_ref, b_ref, a_ref):
        i = pl.program_id(1)
        row = lax.broadcasted_iota(jnp.int32, (tm, n), 0)
        c = []
        for half in range(2):
            main = u_ref[half]
            before = jnp.where(i > 0, up_ref[half, 7:8, :], 0.0)
            after = jnp.where(i < nblk - 1, un_ref[half, 0:1, :], 0.0)
            um = jnp.where(row == 0, before, pltpu.roll(main, 1, 0))
            up = jnp.where(row == tm - 1, after, pltpu.roll(main, tm - 1, 0))
            w = w_ref[half]
            c.append(um * w[0:1] + main * w[1:2] + up * w[2:3] + b_ref[half])
        a_ref[...] = (c[0] * _sigmoid(c[0]) * c[1]).astype(BF16)

    main, prev, nxt = _halo_specs(tm, n, S)
    return _pc(body, name=name, grid=(J, nblk),
               in_specs=[main, prev, nxt, pl.BlockSpec((2, None, 3, n), lambda j, i: (0, j, 0, 0)),
                         pl.BlockSpec((2, None, 1, n), lambda j, i: (0, j, 0, 0))],
               out_specs=pl.BlockSpec((None, tm, n), lambda j, i: (j, i, 0)),
               out_shape=jax.ShapeDtypeStruct((J, S, n), BF16),
               compiler_params=_sem("parallel", "parallel"))(u, u, u, cw, cb)


def _ffn_act_bwd(name, u, da, cw, cb):
    _, J, S, n = u.shape
    tm = _tile(S, 256, 16)
    nblk = S // tm
    te = tm + 16
    nb = tm // 8
    last8 = S // 8 - 1

    def body(u_ref, up_ref, un_ref, da_ref, dap_ref, dan_ref, w_ref, b_ref, du_ref, st_ref):
        i = pl.program_id(1)

        @pl.when(i == 0)
        def _():
            st_ref[...] = jnp.zeros_like(st_ref)

        def extended(before, main, after):
            return jnp.concatenate([jnp.where(i == 0, 0.0, before), main, jnp.where(i == nblk - 1, 0.0, after)], axis=0)

        mid = slice(8, tm + 8)
        da_e = extended(dap_ref[...], da_ref[...], dan_ref[...])
        ue, c = [], []
        for half in range(2):
            e = extended(up_ref[half], u_ref[half], un_ref[half])
            w = w_ref[half]
            ue.append((pltpu.roll(e, 1, 0), e, pltpu.roll(e, te - 1, 0)))
            c.append(ue[half][0] * w[0:1] + e * w[1:2] + ue[half][2] * w[2:3] + b_ref[half])
        sg = _sigmoid(c[0])
        dc = [da_e * c[1] * (sg * (1.0 + c[0] * (1.0 - sg))), da_e * (c[0] * sg)]
        r8 = lax.broadcasted_iota(jnp.int32, (8, n), 0)
        for half in range(2):
            w, d, (e_before, e, e_after) = w_ref[half], dc[half], ue[half]
            dm = d[mid]
            du = pltpu.roll(d, te - 1, 0)[mid] * w[0:1] + dm * w[1:2] + pltpu.roll(d, 1, 0)[mid] * w[2:3]
            du_ref[half] = du.astype(BF16)
            s0 = jnp.sum(dm * e_before[mid], axis=0, keepdims=True)
            s1 = jnp.sum(dm * e[mid], axis=0, keepdims=True)
            s2 = jnp.sum(dm * e_after[mid], axis=0, keepdims=True)
            s3 = jnp.sum(dm, axis=0, keepdims=True)
            st_ref[half] += jnp.where(r8 == 0, s0, jnp.where(r8 == 1, s1, jnp.where(r8 == 2, s2,
                                      jnp.where(r8 == 3, s3, 0.0))))

    main, prev, nxt = _halo_specs(tm, n, S)
    dmain = pl.BlockSpec((None, tm, n), lambda j, i: (j, i, 0))
    dprev = pl.BlockSpec((None, 8, n), lambda j, i: (j, jnp.maximum(i * nb - 1, 0), 0))
    dnxt = pl.BlockSpec((None, 8, n), lambda j, i: (j, jnp.minimum((i + 1) * nb, last8), 0))
    return _pc(body, name=name, grid=(J, nblk),
               in_specs=[main, prev, nxt, dmain, dprev, dnxt,
                         pl.BlockSpec((2, None, 3, n), lambda j, i: (0, j, 0, 0)),
                         pl.BlockSpec((2, None, 1, n), lambda j, i: (0, j, 0, 0))],
               out_specs=(main, pl.BlockSpec((2, None, 8, n), lambda j, i: (0, j, 0, 0))),
               out_shape=(jax.ShapeDtypeStruct((2, J, S, n), BF16), jax.ShapeDtypeStruct((2, J, 8, n), F32)),
               compiler_params=_sem("parallel", "arbitrary"))(u, u, u, da, da, da, cw, cb)


def _window_count(t, w, S):
    lo = jnp.maximum(t - w // 2, 0)
    hi = jnp.minimum(t + w - w // 2, S)
    return (hi - lo).astype(F32)


def _trailing_sums(x, w):
    k = 1
    while k < w:
        x = x + pltpu.roll(x, k, 0)
        k *= 2
    return x


def _pool_window(name, h, group_w, adjoint, out_dtype):
    S, D = h.shape
    SP = S + 2 * POOL_PAD
    per_group = group_w // LANES

    def body(h_ref, o_ref, xp):
        g = pl.program_id(0) // per_group
        t = lax.broadcasted_iota(jnp.int32, (S, LANES), 0)
        xp[0:POOL_PAD, :] = jnp.zeros((POOL_PAD, LANES), F32)
        xp[S + POOL_PAD:SP, :] = jnp.zeros((POOL_PAD, LANES), F32)
        for gi, w in enumerate(POOL_WINDOWS):
            @pl.when(g == gi)
            def _():
                hv = h_ref[...]
                cnt = _window_count(t, w, S)
                xp[POOL_PAD:S + POOL_PAD, :] = hv / cnt if adjoint else hv
                ahead = w // 2 if adjoint else w // 2 - 1
                sw = _trailing_sums(xp[...], w)
                if ahead:
                    sw = pltpu.roll(sw, SP - ahead, 0)
                win = sw[POOL_PAD:S + POOL_PAD]
                o_ref[...] = ((win if adjoint else win / cnt) - hv).astype(out_dtype)

    col = pl.BlockSpec((S, LANES), lambda j: (0, j))
    return _pc(body, name=name, grid=(D // LANES,), in_specs=[col], out_specs=col,
               out_shape=jax.ShapeDtypeStruct((S, D), out_dtype),
               scratch_shapes=[pltpu.VMEM((SP, LANES), F32)], compiler_params=_sem("parallel"))(h)


def _pool_proj(name, mixed, w, scale, res):
    S, D = mixed.shape
    G, gw, _ = w.shape
    tm = _tile(S, 512, 16)

    def body(m_ref, w_ref, s_ref, r_ref, o_ref):
        for g in range(G):
            lo, hi = g * gw, (g + 1) * gw
            y = jnp.dot(m_ref[:, lo:hi], w_ref[g], preferred_element_type=F32)
            o_ref[:, lo:hi] = r_ref[:, lo:hi] + y * s_ref[:, lo:hi]

    row = pl.BlockSpec((tm, D), lambda i: (i, 0))
    return _pc(body, name=name, grid=(S // tm,),
               in_specs=[row, pl.BlockSpec((G, gw, gw), lambda i: (0, 0, 0)), pl.BlockSpec((1, D), lambda i: (0, 0)), row],
               out_specs=row, out_shape=jax.ShapeDtypeStruct((S, D), F32),
               compiler_params=_sem("parallel"))(mixed, w, scale, res)


def _pool_proj_bwd(name, dy, mixed, w, scale):
    S, D = mixed.shape
    G, gw, _ = w.shape
    tm = _tile(S, 512, 16)

    def body(dy_ref, m_ref, w_ref, s_ref, dm_ref, dw_ref, ds_ref):
        @pl.when(pl.program_id(0) == 0)
        def _():
            dw_ref[...] = jnp.zeros_like(dw_ref)
            ds_ref[...] = jnp.zeros_like(ds_ref)

        for g in range(G):
            lo, hi = g * gw, (g + 1) * gw
            mg, dyg = m_ref[:, lo:hi], dy_ref[:, lo:hi]
            y = jnp.dot(mg, w_ref[g], preferred_element_type=F32)
            ds_ref[:, lo:hi] += jnp.sum(dyg * y, axis=0, keepdims=True)
            dyp = (dyg * s_ref[:, lo:hi]).astype(BF16)
            dm_ref[:, lo:hi] = lax.dot_general(dyp, w_ref[g], _NT, preferred_element_type=F32)
            dw_ref[g] += lax.dot_general(mg, dyp, _TN, preferred_element_type=F32)

    row = pl.BlockSpec((tm, D), lambda i: (i, 0))
    wsp = pl.BlockSpec((G, gw, gw), lambda i: (0, 0, 0))
    vec = pl.BlockSpec((1, D), lambda i: (0, 0))
    return _pc(body, name=name, grid=(S // tm,), in_specs=[row, row, wsp, vec], out_specs=(row, wsp, vec),
               out_shape=(jax.ShapeDtypeStruct((S, D), F32), jax.ShapeDtypeStruct((G, gw, gw), F32),
                          jax.ShapeDtypeStruct((1, D), F32)),
               compiler_params=_sem("arbitrary"))(dy, mixed, w, scale)


def _adamw(name, w, g, m, v):
    shape = w.shape
    C = shape[-1]
    R = w.size // C
    tm = _tile(R, 512, 8)

    def body(w_ref, g_ref, m_ref, v_ref, d_ref, nm_ref, nv_ref):
        gv = g_ref[...]
        nm = ADAM_B1 * m_ref[...] + (1.0 - ADAM_B1) * gv
        nv = ADAM_B2 * v_ref[...] + (1.0 - ADAM_B2) * (gv * gv)
        m_hat = nm / (1.0 - ADAM_B1 ** ADAM_STEP)
        v_hat = nv / (1.0 - ADAM_B2 ** ADAM_STEP)
        d_ref[...] = -ADAM_LR * (m_hat / (jnp.sqrt(v_hat) + ADAM_EPS) + ADAM_WD * w_ref[...])
        nm_ref[...] = nm
        nv_ref[...] = nv

    blk = pl.BlockSpec((tm, C), lambda i: (i, 0))
    sd = jax.ShapeDtypeStruct((R, C), F32)
    outs = _pc(body, name=name, grid=(R // tm,), in_specs=[blk] * 4, out_specs=(blk,) * 3, out_shape=(sd,) * 3,
               compiler_params=_sem("parallel"))(*(a.reshape(R, C) for a in (w, g, m, v)))
    return tuple(o.reshape(shape) for o in outs)


def _position():
    return lax.axis_index("x"), lax.axis_index("y"), lax.axis_index("c")


def _flip(v, bit):
    return 1 - v if bit else v


def _allgather_small(name, v):
    R, W = v.shape

    def body(v_ref, out_ref, send_sems, recv_sems):
        x, y, c = _position()
        me = 4 * x + 2 * y + c
        out_ref[me] = v_ref[...]
        sends = []
        for k in range(1, N_DEV):
            peer = (_flip(x, k & 4), _flip(y, k & 2), _flip(c, k & 1))
            cp = pltpu.make_async_remote_copy(src_ref=v_ref, dst_ref=out_ref.at[me], send_sem=send_sems.at[k - 1],
                                              recv_sem=recv_sems.at[k - 1], device_id=peer, device_id_type=MESH)
            cp.start()
            sends.append(cp)
        for k in range(1, N_DEV):
            peer = (_flip(x, k & 4), _flip(y, k & 2), _flip(c, k & 1))
            slot = 4 * peer[0] + 2 * peer[1] + peer[2]
            pltpu.make_async_remote_copy(src_ref=v_ref, dst_ref=out_ref.at[slot], send_sem=send_sems.at[k - 1],
                                         recv_sem=recv_sems.at[k - 1], device_id=peer, device_id_type=MESH).wait_recv()
        for cp in sends:
            cp.wait_send()

    vm = pl.BlockSpec(memory_space=pltpu.VMEM)
    return _pc(body, name=name, in_specs=[vm], out_specs=vm, out_shape=jax.ShapeDtypeStruct((N_DEV, R, W), F32),
               scratch_shapes=[pltpu.SemaphoreType.DMA((N_DEV - 1,)), pltpu.SemaphoreType.DMA((N_DEV - 1,))])(v)


def _sum_slots(name, a):
    n, R, W = a.shape

    def body(a_ref, o_ref):
        acc = a_ref[0]
        for s in range(1, n):
            acc = acc + a_ref[s]
        o_ref[...] = acc

    return _pc(body, name=name, grid=(1,), in_specs=[pl.BlockSpec((n, R, W), lambda i: (0, 0, 0))],
               out_specs=pl.BlockSpec((R, W), lambda i: (0, 0)), out_shape=jax.ShapeDtypeStruct((R, W), F32))(a)


def _allgather_blocks(name, blocks):
    n = len(blocks)

    def body(*refs):
        b_refs, out_refs, token = refs[:n], refs[n:2 * n], refs[2 * n]
        send_sems, recv_sems, local_sems = refs[2 * n + 1:]
        token[...] = jnp.zeros_like(token)
        x, y, c = _position()
        me, sibling = (x, y, c), (x, y, 1 - c)
        chips = [(1 - x, y), (x, 1 - y), (1 - x, 1 - y)]

        def slot(i, px, py, pc):
            return out_refs[i].at[4 * px + 2 * py + pc]

        def copy(i, k, block, to, src=None):
            return pltpu.make_async_remote_copy(src_ref=slot(i, *block) if src is None else src, dst_ref=slot(i, *block),
                                                send_sem=send_sems.at[k, i], recv_sem=recv_sems.at[k, i],
                                                device_id=to, device_id_type=MESH)

        mine = [pltpu.make_async_copy(b_refs[i], slot(i, *me), local_sems.at[i]) for i in range(n)]
        first = [copy(i, 1 + j, me, (*chip, c), src=b_refs[i]) for i in range(n) for j, chip in enumerate(chips)]
        first += [copy(i, 0, me, sibling, src=b_refs[i]) for i in range(n)]
        for cp in mine + first:
            cp.start()
        passed = []
        for j, chip in enumerate(chips):
            for i in range(n):
                copy(i, 1 + j, (*chip, c), me).wait_recv()
                passed.append(copy(i, 4 + j, (*chip, c), sibling))
                passed[-1].start()
        for i in range(n):
            copy(i, 0, sibling, me).wait_recv()
        for j, chip in enumerate(chips):
            for i in range(n):
                copy(i, 4 + j, (*chip, 1 - c), me).wait_recv()
        for cp in first + passed:
            cp.wait_send()
        for cp in mine:
            cp.wait()

    hbm = pl.BlockSpec(memory_space=pl.ANY)
    return _pc(body, name=name, in_specs=[hbm] * n, out_specs=[hbm] * n + [pl.BlockSpec(memory_space=pltpu.VMEM)],
               out_shape=[jax.ShapeDtypeStruct((N_DEV,) + b.shape, b.dtype) for b in blocks]
               + [jax.ShapeDtypeStruct((8, LANES), F32)],
               scratch_shapes=[pltpu.SemaphoreType.DMA((7, n)), pltpu.SemaphoreType.DMA((7, n)),
                               pltpu.SemaphoreType.DMA((n,))])(*blocks)


def _add_sibling(name, g4, r1, pos):
    n, _, L, W = g4.shape
    tl = _tile(L, 512, 16)

    def body(pos_ref, g_ref, r_ref, tb_ref, own_ref):
        t = g_ref[...] + r_ref[...]
        tb_ref[...] = t.astype(BF16)

        @pl.when(pl.program_id(1) == pos_ref[1])
        def _():
            own_ref[...] = t

    gs = pltpu.PrefetchScalarGridSpec(
        num_scalar_prefetch=1, grid=(L // tl, n),
        in_specs=[pl.BlockSpec((None, None, tl, W), lambda i, k, p: (k, p[0], i, 0)),
                  pl.BlockSpec((None, tl, W), lambda i, k, p: (k, i, 0))],
        out_specs=(pl.BlockSpec((None, tl, W), lambda i, k, p: (k, i, 0)),
                   pl.BlockSpec((tl, W), lambda i, k, p: (i, 0))))
    return _pc(body, name=name, grid_spec=gs,
               out_shape=(jax.ShapeDtypeStruct((n, L, W), BF16), jax.ShapeDtypeStruct((L, W), F32)),
               compiler_params=_sem("parallel", "arbitrary"))(pos, g4, r1)


def _add_chips(name, own, r2):
    L, W = own.shape
    tl = _tile(L, 512, 16)

    def body(o_ref, r_ref, out_ref):
        acc = o_ref[...]
        for j in range(3):
            acc = acc + r_ref[j].astype(F32)
        out_ref[...] = acc

    return _pc(body, name=name, grid=(L // tl,),
               in_specs=[pl.BlockSpec((tl, W), lambda i: (i, 0)), pl.BlockSpec((3, tl, W), lambda i: (0, i, 0))],
               out_specs=pl.BlockSpec((tl, W), lambda i: (i, 0)), out_shape=jax.ShapeDtypeStruct((L, W), F32),
               compiler_params=_sem("parallel"))(own, r2)


_HBM = pl.BlockSpec(memory_space=pltpu.HBM)
_SEM = pl.BlockSpec(memory_space=pltpu.SEMAPHORE)
_EFFECT = pltpu.SideEffectType.DATAFLOW_SIDE_EFFECTING


def _in_hbm(a):
    return pltpu.with_memory_space_constraint(a, pltpu.HBM)


def _after(x, token):
    return x + token[0, 0].astype(x.dtype)


def _copies_start(name, bufs, sem_shape, plan):
    nb = len(bufs)

    def body(*refs):
        for cp in plan(refs[:nb], refs[nb], refs[nb + 1]):
            cp.start()
        refs[-1][...] = jnp.zeros_like(refs[-1])

    out = _pc(body, name=name, in_specs=[_HBM] * nb,
              out_specs=(_SEM, _SEM, *[_HBM] * nb, pl.BlockSpec(memory_space=pltpu.VMEM)),
              out_shape=(pltpu.SemaphoreType.DMA(sem_shape), pltpu.SemaphoreType.DMA(sem_shape),
                         *[pltpu.HBM(b.shape, b.dtype) for b in bufs], jax.ShapeDtypeStruct((8, LANES), F32)),
              input_output_aliases={i: 2 + i for i in range(nb)},
              compiler_params=pltpu.CompilerParams(has_side_effects=_EFFECT))(*[_in_hbm(b) for b in bufs])
    return out[0], out[1], list(out[2:2 + nb]), out[-1]


def _copies_wait(name, bufs, send_sems, recv_sems, plan, after):
    nb = len(bufs)

    def body(*refs):
        for cp in plan(refs[:nb], refs[nb], refs[nb + 1]):
            cp.wait_send()
            cp.wait_recv()

    return list(_pc(body, name=name, in_specs=[_HBM] * nb + [_SEM, _SEM, pl.BlockSpec(memory_space=pl.ANY)],
                    out_specs=[_HBM] * nb, out_shape=[pltpu.HBM(b.shape, b.dtype) for b in bufs],
                    input_output_aliases={i: i for i in range(nb)},
                    compiler_params=pltpu.CompilerParams(has_side_effects=_EFFECT))(*bufs, send_sems, recv_sems, after))


def _plan_gather_chips(n):
    def plan(refs, send_sems, recv_sems):
        x, y, c = _position()
        peers = [(x, y, 1 - c), (1 - x, y, c), (x, 1 - y, c), (1 - x, 1 - y, c)]
        return [pltpu.make_async_remote_copy(src_ref=refs[i], dst_ref=refs[n + i].at[4 * x + 2 * y + c],
                                             send_sem=send_sems.at[k * n + i], recv_sem=recv_sems.at[k * n + i],
                                             device_id=peer, device_id_type=MESH)
                for i in range(n) for k, peer in enumerate(peers)]
    return plan


def _plan_gather_sibling(n):
    def plan(refs, send_sems, recv_sems):
        x, y, c = _position()
        slots = [4 * (1 - x) + 2 * y + c, 4 * x + 2 * (1 - y) + c, 4 * (1 - x) + 2 * (1 - y) + c]
        return [pltpu.make_async_remote_copy(src_ref=refs[i].at[s], dst_ref=refs[i].at[s],
                                             send_sem=send_sems.at[k * n + i], recv_sem=recv_sems.at[k * n + i],
                                             device_id=(x, y, 1 - c), device_id_type=MESH)
                for i in range(n) for k, s in enumerate(slots)]
    return plan


def _plan_reduce_sibling(n):
    def plan(refs, send_sems, recv_sems):
        x, y, c = _position()
        return [pltpu.make_async_remote_copy(src_ref=refs[i].at[k, 1 - c], dst_ref=refs[n + i].at[k],
                                             send_sem=send_sems.at[k * n + i], recv_sem=recv_sems.at[k * n + i],
                                             device_id=(x, y, 1 - c), device_id_type=MESH)
                for i in range(n) for k in range(N_DEV // 2)]
    return plan


def _plan_reduce_chips(n):
    def plan(refs, send_sems, recv_sems):
        x, y, c = _position()
        cps = []
        for i in range(n):
            for j in range(1, 4):
                px, py = _flip(x, j & 2), _flip(y, j & 1)
                sem = (j - 1) * n + i
                cps.append(pltpu.make_async_remote_copy(src_ref=refs[i].at[2 * px + py], dst_ref=refs[n + i].at[j - 1],
                                                        send_sem=send_sems.at[sem], recv_sem=recv_sems.at[sem],
                                                        device_id=(px, py, c), device_id_type=MESH))
        return cps
    return plan


def _heads_major(name, a, after):
    S, W = a.shape
    H = W // HEAD_DIM
    tm = _tile(S, 512, 16)

    def body(a_ref, after_ref, o_ref):
        v = a_ref[...]
        for h in range(H):
            o_ref[h] = v[:, h * HEAD_DIM:(h + 1) * HEAD_DIM]

    return _pc(body, name=name, grid=(S // tm,),
               in_specs=[pl.BlockSpec((tm, W), lambda i: (i, 0)), pl.BlockSpec(memory_space=pl.ANY)],
               out_specs=pl.BlockSpec((H, tm, HEAD_DIM), lambda i: (0, i, 0)),
               out_shape=jax.ShapeDtypeStruct((H, S, HEAD_DIM), a.dtype), compiler_params=_sem("parallel"))(a, after)


def _heads_minor(name, a):
    H, S, _ = a.shape
    tm = _tile(S, 512, 16)

    def body(a_ref, o_ref):
        o_ref[...] = jnp.concatenate([a_ref[h] for h in range(H)], axis=1)

    return _pc(body, name=name, grid=(S // tm,), in_specs=[pl.BlockSpec((H, tm, HEAD_DIM), lambda i: (0, i, 0))],
               out_specs=pl.BlockSpec((tm, H * HEAD_DIM), lambda i: (i, 0)),
               out_shape=jax.ShapeDtypeStruct((S, H * HEAD_DIM), a.dtype), compiler_params=_sem("parallel"))(a)


def kernel(x, mem, attn_norm, attn_w_qkv, attn_q_gain, attn_k_gain, attn_w_o, pool_norm, pool_w, pool_scale, xattn_norm, mem_norm, xattn_w_q, xattn_w_kv, xattn_w_o, ffn_norm, ffn_w_up, ffn_conv_w, ffn_conv_b, ffn_w_down, final_norm, loss_target, m_attn_norm, m_attn_w_qkv, m_attn_q_gain, m_attn_k_gain, m_attn_w_o, m_pool_norm, m_pool_w, m_pool_scale, m_xattn_norm, m_mem_norm, m_xattn_w_q, m_xattn_w_kv, m_xattn_w_o, m_ffn_norm, m_ffn_w_up, m_ffn_conv_w, m_ffn_conv_b, m_ffn_w_down, m_final_norm, v_attn_norm, v_attn_w_qkv, v_attn_q_gain, v_attn_k_gain, v_attn_w_o, v_pool_norm, v_pool_w, v_pool_scale, v_xattn_norm, v_mem_norm, v_xattn_w_q, v_xattn_w_kv, v_xattn_w_o, v_ffn_norm, v_ffn_w_up, v_ffn_conv_w, v_ffn_conv_b, v_ffn_w_down, v_final_norm):
    names = ['attn_norm', 'attn_w_qkv', 'attn_q_gain', 'attn_k_gain', 'attn_w_o', 'pool_norm', 'pool_w', 'pool_scale',
             'xattn_norm', 'mem_norm', 'xattn_w_q', 'xattn_w_kv', 'xattn_w_o', 'ffn_norm', 'ffn_w_up', 'ffn_conv_w',
             'ffn_conv_b', 'ffn_w_down', 'final_norm']
    W = dict(zip(names, (attn_norm, attn_w_qkv, attn_q_gain, attn_k_gain, attn_w_o, pool_norm, pool_w, pool_scale,
                         xattn_norm, mem_norm, xattn_w_q, xattn_w_kv, xattn_w_o, ffn_norm, ffn_w_up, ffn_conv_w,
                         ffn_conv_b, ffn_w_down, final_norm)))
    Mo = dict(zip(names, (m_attn_norm, m_attn_w_qkv, m_attn_q_gain, m_attn_k_gain, m_attn_w_o, m_pool_norm, m_pool_w,
                          m_pool_scale, m_xattn_norm, m_mem_norm, m_xattn_w_q, m_xattn_w_kv, m_xattn_w_o, m_ffn_norm,
                          m_ffn_w_up, m_ffn_conv_w, m_ffn_conv_b, m_ffn_w_down, m_final_norm)))
    Vo = dict(zip(names, (v_attn_norm, v_attn_w_qkv, v_attn_q_gain, v_attn_k_gain, v_attn_w_o, v_pool_norm, v_pool_w,
                          v_pool_scale, v_xattn_norm, v_mem_norm, v_xattn_w_q, v_xattn_w_kv, v_xattn_w_o, v_ffn_norm,
                          v_ffn_w_up, v_ffn_conv_w, v_ffn_conv_b, v_ffn_w_down, v_final_norm)))

    S, D = x.shape[1], x.shape[2]
    n_layers = xattn_norm.shape[0]
    n_up = ffn_w_up.shape[2]
    qkv_w = attn_w_qkv.shape[2] * N_DEV
    n_heads = qkv_w // HEAD_DIM - 2 * N_KV_HEADS
    n_rot = (n_heads + N_KV_HEADS) * HEAD_DIM // LANES
    group_w = pool_w.shape[3]
    xs, mems, tgt = x[0], mem[0], loss_target[0]
    xi, yi, ci = _position()
    dev = 4 * xi + 2 * yi + ci
    pos = jnp.stack([ci, 2 * xi + yi]).astype(jnp.int32)

    layers = range(n_layers)
    n_groups = pool_w.shape[1]
    small_vec = jnp.concatenate([pool_norm.reshape(-1), pool_scale.reshape(-1), ffn_conv_w.reshape(-1)])
    small_rows = _round_up(-(-small_vec.size // PACK_W), 8)
    small_vec = jnp.pad(small_vec, (0, small_rows * PACK_W - small_vec.size)).reshape(small_rows, PACK_W)
    w_qkv, w_o, small, attn_token = _allgather_blocks(
        "allgather_attn", [attn_w_qkv[0].astype(BF16), attn_w_o[0].astype(BF16), small_vec])
    small = small.reshape(N_DEV, -1)
    w_qkv = w_qkv.transpose(1, 0, 2).reshape(D, qkv_w)
    w_o = w_o.reshape(-1, D)
    blocks = [pool_w.reshape(-1, group_w)] + [xattn_w_q[l] for l in layers] + [xattn_w_kv.reshape(n_layers * D, -1)]
    blocks += [xattn_w_o[l] for l in layers] + [ffn_w_up.reshape(n_layers * D, n_up)] + [ffn_w_down[l] for l in layers]
    blocks = [b.astype(BF16) for b in blocks]
    blocks[0] = _after(blocks[0], attn_token)
    n_blk = len(blocks)
    lands = [lax.dynamic_update_index_in_dim(lax.empty((N_DEV,) + b.shape, BF16), b, dev, 0) for b in blocks]
    plan_chips, plan_sibling = _plan_gather_chips(n_blk), _plan_gather_sibling(n_blk)
    gather_sems = _copies_start("gather_chips_start", blocks + lands, (4 * n_blk,), plan_chips)
    attn_norm_late = _after(attn_norm, gather_sems[3])

    d_sh = pool_norm.shape[1]
    pool_norm_f = small[:, :d_sh].reshape(1, D)
    pool_scale_f = small[:, d_sh:2 * d_sh].reshape(1, D)
    conv_w_f = small[:, 2 * d_sh:2 * d_sh + ffn_conv_w.size].reshape(N_DEV, n_layers, 3, n_up)
    conv_b_f = ffn_conv_b.reshape(n_layers, N_DEV, 1, n_up)

    cos, sin = _rope_tables(S)
    bd = _head_mean_matrix()
    pad_w = qkv_w - (n_heads + N_KV_HEADS) * HEAD_DIM
    qk_gain = jnp.concatenate([jnp.tile(attn_q_gain[0], n_heads), jnp.tile(attn_k_gain[0], N_KV_HEADS),
                               jnp.ones((pad_w,), F32)]).reshape(1, qkv_w)
    qk_scale = jnp.concatenate([jnp.full((n_heads * HEAD_DIM,), HEAD_DIM ** -0.5, F32),
                                jnp.ones((qkv_w - n_heads * HEAD_DIM,), F32)]).reshape(1, qkv_w)

    saved = []

    def xattn_ffn_fwd(l, xin, hx=None):
        if hx is None:
            hx = _rmsnorm(f"xattn_norm{l}", xin, xattn_norm[l:l + 1], BF16)
        memn = _rmsnorm(f"mem_norm{l}", mems, mem_norm[l:l + 1], BF16)
        qx = _mm_nn(f"xattn_q{l}", hx, w_xq[l], BF16)
        kv = _mm_nn_bs(f"xattn_kv{l}", memn, w_xkv, BF16, l)
        ox = _xattn_fwd(f"xattn_fwd{l}", qx, kv)
        x2 = _mm_nn(f"xattn_o{l}", ox, w_xo[l], F32, res=xin)
        hf = _rmsnorm(f"ffn_norm{l}", x2, ffn_norm[l:l + 1], BF16)
        u = _mm_nn_bs(f"ffn_up{l}", hf, w_up, F32, l).reshape(2, N_DEV // 2, S, n_up)
        cw = conv_w_f[:, l].reshape(2, N_DEV // 2, 3, n_up)
        cb = conv_b_f[l].reshape(2, N_DEV // 2, 1, n_up)
        act = _ffn_act(f"ffn_act{l}", u, cw, cb)
        x3 = _mm_nn_as(f"ffn_down{l}", act, w_down[l], F32, x2)
        saved.append(dict(xin=xin, hx=hx, memn=memn, qx=qx, kv=kv, ox=ox, x2=x2, hf=hf, u=u, cw=cw, cb=cb, act=act))
        return x3

    h0 = _rmsnorm("attn_norm", xs, attn_norm_late, BF16)
    qkv = _mm_nn("attn_qkv", h0, w_qkv, F32)
    qkr = _qk_rope("qk_rope", qkv, qk_gain, qk_scale, cos, sin, bd, n_rot)
    o_hm, lse = _attn_fwd("attn_fwd", qkr, n_heads)
    o_att = _heads_minor("attn_heads_minor", o_hm)
    arrived = _copies_wait("gather_chips_wait", gather_sems[2], gather_sems[0], gather_sems[1], plan_chips, o_att)
    pass_sems = _copies_start("gather_sibling_start", arrived[n_blk:], (3 * n_blk,), plan_sibling)
    x1 = _mm_nn("attn_o", o_att, _after(w_o, pass_sems[3]), F32, res=xs)
    hx0 = _rmsnorm("xattn_norm0", x1, xattn_norm[0:1], BF16)
    gathered = iter(_copies_wait("gather_sibling_wait", pass_sems[2], pass_sems[0], pass_sems[1], plan_sibling, hx0))
    w_pool = (next(gathered).reshape(N_DEV, n_groups, -1, group_w).transpose(1, 0, 2, 3)
              .reshape(n_groups, group_w, group_w))
    w_xq = [next(gathered).reshape(D, D) for l in layers]
    w_xkv = next(gathered)
    w_xo = [next(gathered).reshape(D, D) for l in layers]
    w_up = next(gathered)
    w_down = [next(gathered).reshape(-1, D) for l in layers]
    x3 = xattn_ffn_fwd(0, x1, hx0)
    hp = _rmsnorm("pool_norm", x3, pool_norm_f, F32)
    mixed = _pool_window("pool_window", hp, group_w, False, BF16)
    x4 = _pool_proj("pool_proj", mixed, w_pool, pool_scale_f, x3)
    x6 = xattn_ffn_fwd(1, x4)

    G = {}
    g, d_final, lvec = _loss_head("loss_head", x6, final_norm.reshape(1, D), tgt)
    G['final_norm'] = d_final.reshape(D)
    loss_part = (0.5 * jnp.sum(lvec) / D).reshape(1)

    d_xn, d_mn, d_fn, d_xq, d_xkv, d_xo, d_up, d_cw, d_cb, d_down = ([None] * n_layers for _ in range(10))

    def xattn_ffn_bwd(l, g, conv_b_late=None, after_act=None):
        sv = saved[l]
        d_act = _mm_nt_bs(f"ffn_down_dx{l}", g, w_down[l], N_DEV // 2, F32)
        d_down[l] = _mm_tn_as(f"ffn_down_dw{l}", sv['act'], g, F32)
        du, st = _ffn_act_bwd(f"ffn_act_bwd{l}", sv['u'], d_act, sv['cw'],
                              sv['cb'] if conv_b_late is None else conv_b_late)
        du = du.reshape(N_DEV, S, n_up)
        ffn_gain = ffn_norm[l:l + 1] if after_act is None else _after(ffn_norm[l:l + 1], after_act(du))
        st = st.reshape(N_DEV, 8, n_up)
        d_cw[l], d_cb[l] = st[:, 0:3], st[:, 3].reshape(-1)
        d_up[l] = _mm_tn_bs(f"ffn_up_dw{l}", sv['hf'], du, F32)
        dhf = _mm_nt_abs(f"ffn_up_dx{l}", du, w_up, D, F32, l)
        g, d_fn[l] = _rmsnorm_bwd(f"ffn_norm_bwd{l}", sv['x2'], ffn_gain, dhf, g)
        d_xo[l] = _mm_tn(f"xattn_o_dw{l}", sv['ox'], g, F32)
        do = _mm_nt(f"xattn_o_dx{l}", g, w_xo[l], BF16)
        dq, dkv = _xattn_bwd(f"xattn_bwd{l}", sv['qx'], sv['kv'], do)
        d_xq[l] = _mm_tn(f"xattn_q_dw{l}", sv['hx'], dq, F32)
        d_xkv[l] = _mm_tn_bs(f"xattn_kv_dw{l}", sv['memn'], dkv, F32)
        dmemn = _mm_nt_abs(f"xattn_kv_dx{l}", dkv, w_xkv, D, F32, l)
        _, d_mn[l] = _rmsnorm_bwd(f"mem_norm_bwd{l}", mems, mem_norm[l:l + 1], dmemn)
        dhx = _mm_nt(f"xattn_q_dx{l}", dq, w_xq[l], F32)
        g, d_xn[l] = _rmsnorm_bwd(f"xattn_norm_bwd{l}", sv['xin'], xattn_norm[l:l + 1], dhx, g)
        return g

    def reduce_start(tag, bufs):
        n = len(bufs)
        g4s = [b.reshape((N_DEV // 2, 2) + b.shape[1:]) for b in bufs]
        lands = [lax.empty((N_DEV // 2,) + b.shape[1:], F32) for b in bufs]
        plan = _plan_reduce_sibling(n)
        return (n, plan) + _copies_start(f"reduce_sibling_start_{tag}", g4s + lands, (N_DEV // 2 * n,), plan)

    def reduce_between(tag, state, after):
        n, plan, send_sems, recv_sems, thru, _ = state
        got = _copies_wait(f"reduce_sibling_wait_{tag}", thru, send_sems, recv_sems, plan, after)
        sums = [_add_sibling(f"reduce_add_sibling_{tag}{i}", got[i], got[n + i], pos) for i in range(n)]
        lands = [lax.empty((3,) + tb.shape[1:], BF16) for tb, _ in sums]
        plan = _plan_reduce_chips(n)
        return (n, plan, [own for _, own in sums]) + _copies_start(f"reduce_chips_start_{tag}",
                                                                    [tb for tb, _ in sums] + lands, (3 * n,), plan)

    def reduce_finish(tag, state, after):
        n, plan, owns, send_sems, recv_sems, thru, _ = state
        got = _copies_wait(f"reduce_chips_wait_{tag}", thru, send_sems, recv_sems, plan, after)
        return [_add_chips(f"reduce_add_chips_{tag}{i}", owns[i], got[n + i]) for i in range(n)]

    def layer_bufs(l):
        return [d_xq[l].reshape(N_DEV, -1, D), d_xkv[l], d_xo[l].reshape(N_DEV, -1, D), d_up[l],
                d_down[l].reshape(N_DEV, -1, D)]

    g = xattn_ffn_bwd(1, g)
    d_mixed, d_pool_w, d_pool_scale = _pool_proj_bwd("pool_proj_bwd", g, mixed, w_pool, pool_scale_f)
    dhp = _pool_window("pool_window_bwd", d_mixed, group_w, True, F32)
    g, d_pool_norm = _rmsnorm_bwd("pool_norm_bwd", x3, pool_norm_f, dhp, g)
    upper = reduce_start("upper", [d_pool_w.reshape(n_groups, N_DEV, -1, group_w).transpose(1, 0, 2, 3)
                                   .reshape(N_DEV, -1, group_w)] + layer_bufs(1))
    between = []

    def upper_between(du):
        between.append(reduce_between("upper", upper, du))
        return between[0][-1]

    g = xattn_ffn_bwd(0, g, _after(saved[0]['cb'], upper[-1]), upper_between)
    d_wo = _mm_tn("attn_o_dw", o_att, g, F32)
    lower = reduce_start("lower", layer_bufs(0) + [d_wo.reshape(N_DEV, -1, D)])
    do = _mm_nt("attn_o_dx", g, _after(w_o, lower[-1]), BF16)
    lower = reduce_between("lower", lower, do)
    do_hm = _heads_major("attn_heads_major", do, lower[-1])
    dq_hm, dk_hm, dv_hm = _attn_bwd("attn_bwd", qkr, o_hm, lse, do_hm)
    red_lower = reduce_finish("lower", lower, dq_hm)
    d_qkv, d_gain = _qk_rope_bwd("qk_rope_bwd", dq_hm, dk_hm, dv_hm, qkv, qk_gain, qk_scale, cos, sin, bd)
    red_upper = reduce_finish("upper", between[0], d_qkv)
    d_wqkv = _mm_tn("attn_qkv_dw", h0, d_qkv, F32)
    last = reduce_start("last", [d_wqkv.reshape(D, N_DEV, -1).transpose(1, 0, 2)])
    dh0 = _mm_nt("attn_qkv_dx", d_qkv, _after(w_qkv, last[-1]), F32)
    last = reduce_between("last", last, dh0)
    grad_x, d_attn_norm = _rmsnorm_bwd("attn_norm_bwd", xs, _after(attn_norm, last[-1]), dh0, g)
    G['attn_w_o'] = red_lower[-1][None]
    G['pool_w'] = red_upper[0].reshape(pool_w.shape)
    per_layer = [red_lower, red_upper[1:]]
    for i, n in enumerate(['xattn_w_q', 'xattn_w_kv', 'xattn_w_o', 'ffn_w_up', 'ffn_w_down']):
        G[n] = jnp.stack([per_layer[l][i] for l in layers])

    hq = n_heads * HEAD_DIM
    small_g = {'attn_norm': d_attn_norm, 'attn_q_gain': d_gain[0, :hq].reshape(n_heads, HEAD_DIM).sum(0),
               'attn_k_gain': d_gain[0, hq:hq + N_KV_HEADS * HEAD_DIM].reshape(N_KV_HEADS, HEAD_DIM).sum(0),
               'pool_norm': d_pool_norm, 'pool_scale': d_pool_scale,
               'xattn_norm': jnp.concatenate(d_xn), 'mem_norm': jnp.concatenate(d_mn), 'ffn_norm': jnp.concatenate(d_fn),
               'ffn_conv_w': jnp.stack(d_cw, axis=1), 'ffn_conv_b': jnp.stack(d_cb)}
    order = list(small_g)
    flat = jnp.concatenate([loss_part] + [small_g[n].reshape(-1) for n in order] + [G['final_norm']])
    ar_rows = _round_up(-(-flat.size // PACK_W), 8)
    flat = jnp.pad(flat, (0, ar_rows * PACK_W - flat.size)).reshape(ar_rows, PACK_W)
    summed = _sum_slots("allreduce_sum", _allgather_small("allreduce_gather", flat)).reshape(-1)
    loss = summed[0]
    G['attn_w_qkv'] = reduce_finish("last", last, summed)[0][None]
    at = 1
    for n in order + ['final_norm']:
        size = G['final_norm'].size if n == 'final_norm' else small_g[n].size
        piece = summed[at:at + size]
        at += size
        if n in ('pool_norm', 'pool_scale'):
            piece = lax.dynamic_slice(piece, (dev * d_sh,), (d_sh,))
        elif n == 'ffn_conv_w':
            piece = lax.dynamic_index_in_dim(piece.reshape(N_DEV, n_layers, 3, n_up), dev, 0, keepdims=False)
        G[n] = piece.reshape(W[n].shape)

    deltas, new_m, new_v = [], [], []
    for n in names:
        d, nm, nv = _adamw(f"adamw_{n}", W[n], G[n], Mo[n], Vo[n])
        deltas.append(d)
        new_m.append(nm)
        new_v.append(nv)
    return (loss, grad_x[None], *[G[n] for n in names], *deltas, *new_m, *new_v)
```

```python
import jax
import jax.numpy as jnp
from jax import lax
from jax.experimental import pallas as pl
from jax.experimental.pallas import tpu as pltpu

F32 = jnp.float32
BF16 = jnp.bfloat16
MESH = pl.DeviceIdType.MESH

N_DEV = 8
EPS = 1e-6
HEAD_DIM = 64
N_KV_HEADS = 4
X_HEADS = 4
GRID_W = 64
ROPE_THETA = 10000.0
ROPE_PAIRS = HEAD_DIM // 4
POOL_WINDOWS = (2, 4, 8, 16)
POOL_PAD = 16
KEY_CHUNK = 1024
MM_ROWS = 1024
FFN_HALO = 16
LANES = 128
PACK_W = 1024
ADAM_LR, ADAM_B1, ADAM_B2, ADAM_EPS, ADAM_WD, ADAM_STEP = 0.001, 0.9, 0.999, 1e-08, 0.01, 10

_NN = (((1,), (0,)), ((), ()))
_NT = (((1,), (1,)), ((), ()))
_TN = (((0,), (0,)), ((), ()))


def _pc(body, *, name, **kw):
    return pl.pallas_call(body, name=name, **kw)


def _sem(*kinds):
    return pltpu.CompilerParams(dimension_semantics=kinds)


def _tile(n, pref, mult):
    best = None
    for t in range(mult, min(n, pref) + 1, mult):
        if n % t == 0:
            best = t
    return n if best is None else best


def _round_up(n, m):
    return (n + m - 1) // m * m


def _mm_call(name, a, b, dims, grid, a_spec, b_spec, o_spec, out_shape, kaxis, res=None, res_spec=None):
    nk = grid[kaxis]
    acc_shape = tuple(d for d in o_spec.block_shape if d is not None)
    in_place = out_shape.dtype == F32 and res is None
    use_scratch = nk > 1 and not in_place

    def body(*refs):
        refs = list(refs)
        acc = refs.pop() if use_scratch else None
        a_ref, b_ref = refs[:2]
        r_ref = refs[2] if res is not None else None
        o_ref = refs[-1]
        prod = lax.dot_general(a_ref[...].astype(BF16), b_ref[...].astype(BF16), dims, preferred_element_type=F32)
        if nk == 1:
            if r_ref is not None:
                prod = prod + r_ref[...]
            o_ref[...] = prod.astype(o_ref.dtype)
            return
        k = pl.program_id(kaxis)
        tgt = o_ref if in_place else acc

        @pl.when(k == 0)
        def _():
            tgt[...] = prod

        @pl.when(k > 0)
        def _():
            tgt[...] += prod

        if not in_place:
            @pl.when(k == nk - 1)
            def _():
                r = acc[...]
                if r_ref is not None:
                    r = r + r_ref[...]
                o_ref[...] = r.astype(o_ref.dtype)

    sem = tuple("arbitrary" if ax == kaxis else "parallel" for ax in range(len(grid)))
    ins = [a, b] if res is None else [a, b, res]
    specs = [a_spec, b_spec] if res is None else [a_spec, b_spec, res_spec]
    return _pc(body, name=name, grid=grid, in_specs=specs, out_specs=o_spec, out_shape=out_shape,
               scratch_shapes=[pltpu.VMEM(acc_shape, F32)] if use_scratch else [],
               compiler_params=_sem(*sem))(*ins)


def _mm_nn(name, a, b, out_dtype, res=None):
    M, K = a.shape
    N = b.shape[1]
    tm, tn, tk = _tile(M, MM_ROWS, 16), _tile(N, 1024, LANES), _tile(K, 1024, LANES)
    return _mm_call(name, a, b, _NN, (M // tm, N // tn, K // tk),
                    pl.BlockSpec((tm, tk), lambda i, j, k: (i, k)),
                    pl.BlockSpec((tk, tn), lambda i, j, k: (k, j)),
                    pl.BlockSpec((tm, tn), lambda i, j, k: (i, j)),
                    jax.ShapeDtypeStruct((M, N), out_dtype), 2, res,
                    pl.BlockSpec((tm, tn), lambda i, j, k: (i, j)))


def _mm_nt(name, a, b, out_dtype):
    M, K = a.shape
    N = b.shape[0]
    tm, tn, tk = _tile(M, MM_ROWS, 16), _tile(N, 1024, LANES), _tile(K, 1024, LANES)
    return _mm_call(name, a, b, _NT, (M // tm, N // tn, K // tk),
                    pl.BlockSpec((tm, tk), lambda i, j, k: (i, k)),
                    pl.BlockSpec((tn, tk), lambda i, j, k: (j, k)),
                    pl.BlockSpec((tm, tn), lambda i, j, k: (i, j)),
                    jax.ShapeDtypeStruct((M, N), out_dtype), 2)


def _mm_tn(name, a, b, out_dtype):
    R, M = a.shape
    N = b.shape[1]
    tm, tn, tr = _tile(M, 1024, LANES), _tile(N, 1024, LANES), _tile(R, MM_ROWS, 16)
    return _mm_call(name, a, b, _TN, (M // tm, N // tn, R // tr),
                    pl.BlockSpec((tr, tm), lambda i, j, k: (k, i)),
                    pl.BlockSpec((tr, tn), lambda i, j, k: (k, j)),
                    pl.BlockSpec((tm, tn), lambda i, j, k: (i, j)),
                    jax.ShapeDtypeStruct((M, N), out_dtype), 2)


def _mm_nn_bs(name, a, b, out_dtype, layer=0):
    M, K = a.shape
    J, _, n = b.shape
    tm, tk = _tile(M, MM_ROWS, 16), _tile(K, 1024, LANES)
    first = layer * (K // tk)
    return _mm_call(name, a, b, _NN, (J, M // tm, K // tk),
                    pl.BlockSpec((tm, tk), lambda j, i, k: (i, k)),
                    pl.BlockSpec((None, tk, n), lambda j, i, k: (j, first + k, 0)),
                    pl.BlockSpec((None, tm, n), lambda j, i, k: (j, i, 0)),
                    jax.ShapeDtypeStruct((J, M, n), out_dtype), 2)


def _mm_nn_as(name, a, b, out_dtype, res):
    J, M, n = a.shape
    N = b.shape[1]
    tm, tn = _tile(M, MM_ROWS, 16), _tile(N, 1024, LANES)
    return _mm_call(name, a, b, _NN, (M // tm, N // tn, J),
                    pl.BlockSpec((None, tm, n), lambda i, j, k: (k, i, 0)),
                    pl.BlockSpec((n, tn), lambda i, j, k: (k, j)),
                    pl.BlockSpec((tm, tn), lambda i, j, k: (i, j)),
                    jax.ShapeDtypeStruct((M, N), out_dtype), 2, res,
                    pl.BlockSpec((tm, tn), lambda i, j, k: (i, j)))


def _mm_nt_abs(name, a, b, N, out_dtype, layer=0):
    J, M, n = a.shape
    tm, tn = _tile(M, MM_ROWS, 16), _tile(N, 1024, LANES)
    first = layer * (N // tn)
    return _mm_call(name, a, b, _NT, (M // tm, N // tn, J),
                    pl.BlockSpec((None, tm, n), lambda i, j, k: (k, i, 0)),
                    pl.BlockSpec((None, tn, n), lambda i, j, k: (k, first + j, 0)),
                    pl.BlockSpec((tm, tn), lambda i, j, k: (i, j)),
                    jax.ShapeDtypeStruct((M, N), out_dtype), 2)


def _mm_tn_bs(name, a, b, out_dtype):
    R, M = a.shape
    J, _, n = b.shape
    tm, tr = _tile(M, 1024, LANES), _tile(R, MM_ROWS, 16)
    return _mm_call(name, a, b, _TN, (J, M // tm, R // tr),
                    pl.BlockSpec((tr, tm), lambda j, i, k: (k, i)),
                    pl.BlockSpec((None, tr, n), lambda j, i, k: (j, k, 0)),
                    pl.BlockSpec((None, tm, n), lambda j, i, k: (j, i, 0)),
                    jax.ShapeDtypeStruct((J, M, n), out_dtype), 2)


def _mm_tn_as(name, a, b, out_dtype):
    J, R, n = a.shape
    N = b.shape[1]
    tn, tr = _tile(N, 1024, LANES), _tile(R, MM_ROWS, 16)
    return _mm_call(name, a, b, _TN, (J, N // tn, R // tr),
                    pl.BlockSpec((None, tr, n), lambda j, jn, k: (j, k, 0)),
                    pl.BlockSpec((tr, tn), lambda j, jn, k: (k, jn)),
                    pl.BlockSpec((n, tn), lambda j, jn, k: (j, jn)),
                    jax.ShapeDtypeStruct((J * n, N), out_dtype), 2)


def _rmsnorm(name, x, g, out_dtype):
    R, D = x.shape
    tm = _tile(R, 512, 16)

    def body(x_ref, g_ref, o_ref):
        xv = x_ref[...]
        r = lax.rsqrt(jnp.mean(xv * xv, axis=-1, keepdims=True) + EPS)
        o_ref[...] = (xv * r * g_ref[...]).astype(o_ref.dtype)

    return _pc(body, name=name, grid=(R // tm,),
               in_specs=[pl.BlockSpec((tm, D), lambda i: (i, 0)), pl.BlockSpec((1, D), lambda i: (0, 0))],
               out_specs=pl.BlockSpec((tm, D), lambda i: (i, 0)),
               out_shape=jax.ShapeDtypeStruct((R, D), out_dtype), compiler_params=_sem("parallel"))(x, g)


def _rmsnorm_bwd(name, x, g, dh, dres=None):
    R, D = x.shape
    tm = _tile(R, 512, 16)

    def body(*refs):
        if dres is None:
            x_ref, g_ref, dh_ref, dx_ref, dg_ref = refs
            dres_ref = None
        else:
            x_ref, g_ref, dh_ref, dres_ref, dx_ref, dg_ref = refs
        xv = x_ref[...]
        r = lax.rsqrt(jnp.mean(xv * xv, axis=-1, keepdims=True) + EPS)
        xh = xv * r
        dhv = dh_ref[...].astype(F32)

        @pl.when(pl.program_id(0) == 0)
        def _():
            dg_ref[...] = jnp.zeros_like(dg_ref)

        dg_ref[...] += jnp.sum(dhv * xh, axis=0, keepdims=True)
        dxh = dhv * g_ref[...]
        dx = r * (dxh - xh * jnp.mean(dxh * xh, axis=-1, keepdims=True))
        if dres_ref is not None:
            dx = dx + dres_ref[...]
        dx_ref[...] = dx

    row = pl.BlockSpec((tm, D), lambda i: (i, 0))
    vec = pl.BlockSpec((1, D), lambda i: (0, 0))
    ins = [x, g, dh] + ([] if dres is None else [dres])
    specs = [row, vec, row] + ([] if dres is None else [row])
    return _pc(body, name=name, grid=(R // tm,), in_specs=specs, out_specs=(row, vec),
               out_shape=(jax.ShapeDtypeStruct((R, D), F32), jax.ShapeDtypeStruct((1, D), F32)),
               compiler_params=_sem("arbitrary"))(*ins)


def _loss_head(name, x, g, tgt):
    R, D = x.shape
    tm = _tile(R, 512, 16)

    def body(x_ref, g_ref, t_ref, dx_ref, dg_ref, l_ref):
        xv = x_ref[...]
        r = lax.rsqrt(jnp.mean(xv * xv, axis=-1, keepdims=True) + EPS)
        xh = xv * r
        err = xh * g_ref[...] - t_ref[...]

        @pl.when(pl.program_id(0) == 0)
        def _():
            dg_ref[...] = jnp.zeros_like(dg_ref)
            l_ref[...] = jnp.zeros_like(l_ref)

        l_ref[...] += jnp.sum(err * err, axis=0, keepdims=True)
        dy = err * (1.0 / D)
        dg_ref[...] += jnp.sum(dy * xh, axis=0, keepdims=True)
        dxh = dy * g_ref[...]
        dx_ref[...] = r * (dxh - xh * jnp.mean(dxh * xh, axis=-1, keepdims=True))

    row = pl.BlockSpec((tm, D), lambda i: (i, 0))
    vec = pl.BlockSpec((1, D), lambda i: (0, 0))
    return _pc(body, name=name, grid=(R // tm,), in_specs=[row, vec, row], out_specs=(row, vec, vec),
               out_shape=(jax.ShapeDtypeStruct((R, D), F32), jax.ShapeDtypeStruct((1, D), F32),
                          jax.ShapeDtypeStruct((1, D), F32)),
               compiler_params=_sem("arbitrary"))(x, g, tgt)


def _rope_tables(S):
    n_rows = S // GRID_W
    row = jnp.repeat(jnp.arange(n_rows, dtype=F32), GRID_W)
    col = jnp.tile(jnp.arange(GRID_W, dtype=F32), n_rows)
    inv_freq = ROPE_THETA ** (-jnp.arange(ROPE_PAIRS, dtype=F32) / ROPE_PAIRS)
    ang = jnp.stack([row[:, None] * inv_freq, col[:, None] * inv_freq], axis=1)
    cos, sin = jnp.cos(ang), jnp.sin(ang)
    c = jnp.broadcast_to(cos[:, :, None, :], (S, 2, 2, ROPE_PAIRS)).reshape(S, HEAD_DIM)
    s = jnp.stack([-sin, sin], axis=2).reshape(S, HEAD_DIM)
    reps = LANES // HEAD_DIM
    return jnp.tile(c, (1, reps)), jnp.tile(s, (1, reps))


def _head_mean_matrix():
    h = jnp.arange(LANES) // HEAD_DIM
    m = jnp.where(h[:, None] == h[None, :], 1.0 / HEAD_DIM, 0.0).astype(BF16)
    return jnp.concatenate([m, m], axis=0)


def _head_mean(v, bd):
    hi = v.astype(BF16)
    lo = (v - hi.astype(F32)).astype(BF16)
    return jnp.dot(jnp.concatenate([hi, lo], axis=1), bd, preferred_element_type=F32)


def _swap_halves(y):
    lane = lax.broadcasted_iota(jnp.int32, y.shape, 1)
    return jnp.where(lane % 32 < 16, pltpu.roll(y, LANES - 16, 1), pltpu.roll(y, 16, 1))


def _qk_rope(name, qkv, gain, scale, cos, sin, bd, n_rot):
    S, W = qkv.shape
    tm = _tile(S, 2048, 16)
    per = LANES // HEAD_DIM

    def body(x_ref, g_ref, s_ref, c_ref, sn_ref, bd_ref, o_ref):
        j = pl.program_id(1)
        xv = x_ref[...]

        def put(v):
            for h in range(per):
                o_ref[h] = v[:, h * HEAD_DIM:(h + 1) * HEAD_DIM].astype(BF16)

        @pl.when(j < n_rot)
        def _():
            ms = _head_mean(xv * xv, bd_ref[...])
            y = xv * lax.rsqrt(ms + EPS) * g_ref[...] * s_ref[...]
            put(y * c_ref[...] + _swap_halves(y) * sn_ref[...])

        @pl.when(j >= n_rot)
        def _():
            put(xv)

    blk = pl.BlockSpec((tm, LANES), lambda i, j: (i, j))
    vec = pl.BlockSpec((1, LANES), lambda i, j: (0, j))
    tab = pl.BlockSpec((tm, LANES), lambda i, j: (i, 0))
    return _pc(body, name=name, grid=(S // tm, W // LANES),
               in_specs=[blk, vec, vec, tab, tab, pl.BlockSpec((2 * LANES, LANES), lambda i, j: (0, 0))],
               out_specs=pl.BlockSpec((per, tm, HEAD_DIM), lambda i, j: (j, i, 0)),
               out_shape=jax.ShapeDtypeStruct((W // HEAD_DIM, S, HEAD_DIM), BF16),
               compiler_params=_sem("parallel", "parallel"))(qkv, gain, scale, cos, sin, bd)


def _qk_rope_bwd(name, dq, dk, dv, qkv, gain, scale, cos, sin, bd):
    S, W = qkv.shape
    tm = _tile(S, 2048, 16)
    per = LANES // HEAD_DIM
    nq, nk, nv = dq.shape[0] // per, dk.shape[0] // per, dv.shape[0] // per
    n_rot = nq + nk

    def body(dq_ref, dk_ref, dv_ref, x_ref, g_ref, s_ref, c_ref, sn_ref, bd_ref, dx_ref, dg_ref):
        j, i = pl.program_id(0), pl.program_id(1)

        @pl.when(i == 0)
        def _():
            dg_ref[...] = jnp.zeros_like(dg_ref)

        def rotate_back(d_ref):
            dv = jnp.concatenate([d_ref[h] for h in range(per)], axis=1)
            xv = x_ref[...]
            ms = _head_mean(xv * xv, bd_ref[...])
            r = lax.rsqrt(ms + EPS)
            z = xv * r
            dy = (dv * c_ref[...] - _swap_halves(dv) * sn_ref[...]) * s_ref[...]
            dg_ref[...] += jnp.sum(dy * z, axis=0, keepdims=True)
            dz = dy * g_ref[...]
            mz = _head_mean(dz * z, bd_ref[...])
            dx_ref[...] = (r * (dz - z * mz)).astype(BF16)

        @pl.when(j < nq)
        def _():
            rotate_back(dq_ref)

        @pl.when((j >= nq) & (j < n_rot))
        def _():
            rotate_back(dk_ref)

        @pl.when(j >= n_rot)
        def _():
            dx_ref[...] = jnp.concatenate([dv_ref[h] for h in range(per)], axis=1).astype(BF16)

    def part(first, count):
        return pl.BlockSpec((per, tm, HEAD_DIM), lambda j, i: (jnp.clip(j - first, 0, count - 1), i, 0))

    blk = pl.BlockSpec((tm, LANES), lambda j, i: (i, j))
    vec = pl.BlockSpec((1, LANES), lambda j, i: (0, j))
    tab = pl.BlockSpec((tm, LANES), lambda j, i: (i, 0))
    return _pc(body, name=name, grid=(W // LANES, S // tm),
               in_specs=[part(0, nq), part(nq, nk), part(n_rot, nv), blk, vec, vec, tab, tab,
                         pl.BlockSpec((2 * LANES, LANES), lambda j, i: (0, 0))],
               out_specs=(blk, vec),
               out_shape=(jax.ShapeDtypeStruct((S, W), BF16), jax.ShapeDtypeStruct((1, W), F32)),
               compiler_params=_sem("parallel", "arbitrary"))(dq, dk, dv, qkv, gain, scale, cos, sin, bd)


def _softmax_rows(s):
    m = jnp.max(s, axis=-1, keepdims=True)
    p = jnp.exp(s - m)
    return p, jnp.sum(p, axis=-1, keepdims=True)


def _attn_fwd(name, qkv, H):
    _, S, dh = qkv.shape
    G = H // N_KV_HEADS
    tq = _tile(S, 128, 16)

    kc = _tile(S, KEY_CHUNK, LANES)
    R = G * tq

    def body(q_ref, k_ref, v_ref, o_ref, lse_ref):
        q = q_ref[...].reshape(R, dh)
        m = jnp.full((R, 1), -1e30, F32)
        l = jnp.zeros((R, 1), F32)
        acc = jnp.zeros((R, dh), F32)
        for c in range(S // kc):
            rows = slice(c * kc, (c + 1) * kc)
            s = lax.dot_general(q, k_ref[rows, :], _NT, preferred_element_type=F32)
            m_new = jnp.maximum(m, jnp.max(s, axis=-1, keepdims=True))
            alpha = jnp.exp(m - m_new)
            p = jnp.exp(s - m_new)
            l = alpha * l + jnp.sum(p, axis=-1, keepdims=True)
            acc = alpha * acc + jnp.dot(p.astype(BF16), v_ref[rows, :], preferred_element_type=F32)
            m = m_new
        o_ref[...] = (acc / l).astype(BF16).reshape(G, tq, dh)
        lse_ref[...] = (m + jnp.log(l)).reshape(G, tq, 1)

    qs = pl.BlockSpec((G, tq, dh), lambda kv, i: (kv, i, 0))
    ls = pl.BlockSpec((G, tq, 1), lambda kv, i: (kv, i, 0))
    ks = pl.BlockSpec((None, S, dh), lambda kv, i: (H + kv, 0, 0))
    vs = pl.BlockSpec((None, S, dh), lambda kv, i: (H + N_KV_HEADS + kv, 0, 0))
    return _pc(body, name=name, grid=(N_KV_HEADS, S // tq), in_specs=[qs, ks, vs], out_specs=(qs, ls),
               out_shape=(jax.ShapeDtypeStruct((H, S, dh), BF16), jax.ShapeDtypeStruct((H, S, 1), F32)),
               compiler_params=_sem("parallel", "parallel"))(qkv, qkv, qkv)


def _attn_bwd(name, qkv, o, lse, do):
    H, S, dh = o.shape
    G = H // N_KV_HEADS
    tq = _tile(S, 128, 16)
    kc = _tile(S, KEY_CHUNK, LANES)
    R = G * tq

    def body(q_ref, k_ref, v_ref, o_ref, lse_ref, do_ref, dq_ref, dk_ref, dv_ref):
        @pl.when(pl.program_id(1) == 0)
        def _():
            dk_ref[...] = jnp.zeros_like(dk_ref)
            dv_ref[...] = jnp.zeros_like(dv_ref)

        qq, dd = q_ref[...].reshape(R, dh), do_ref[...].reshape(R, dh)
        delta = jnp.sum(dd.astype(F32) * o_ref[...].reshape(R, dh).astype(F32), axis=-1, keepdims=True)
        lse = lse_ref[...].reshape(R, 1)
        dq = jnp.zeros((R, dh), F32)
        for c in range(S // kc):
            rows = slice(c * kc, (c + 1) * kc)
            kk, vv = k_ref[rows, :], v_ref[rows, :]
            p = jnp.exp(lax.dot_general(qq, kk, _NT, preferred_element_type=F32) - lse)
            dv_ref[rows, :] += lax.dot_general(p.astype(BF16), dd, _TN, preferred_element_type=F32)
            dp = lax.dot_general(dd, vv, _NT, preferred_element_type=F32)
            ds = (p * (dp - delta)).astype(BF16)
            dq = dq + jnp.dot(ds, kk, preferred_element_type=F32)
            dk_ref[rows, :] += lax.dot_general(ds, qq, _TN, preferred_element_type=F32)
        dq_ref[...] = dq.reshape(G, tq, dh)

    qs = pl.BlockSpec((G, tq, dh), lambda kv, i: (kv, i, 0))
    ls = pl.BlockSpec((G, tq, 1), lambda kv, i: (kv, i, 0))
    ks = pl.BlockSpec((None, S, dh), lambda kv, i: (H + kv, 0, 0))
    vs = pl.BlockSpec((None, S, dh), lambda kv, i: (H + N_KV_HEADS + kv, 0, 0))
    acc = pl.BlockSpec((None, S, dh), lambda kv, i: (kv, 0, 0))
    return _pc(body, name=name, grid=(N_KV_HEADS, S // tq), in_specs=[qs, ks, vs, qs, ls, qs],
               out_specs=(qs, acc, acc),
               out_shape=(jax.ShapeDtypeStruct((H, S, dh), F32), jax.ShapeDtypeStruct((N_KV_HEADS, S, dh), F32),
                          jax.ShapeDtypeStruct((N_KV_HEADS, S, dh), F32)),
               compiler_params=_sem("parallel", "arbitrary"))(qkv, qkv, qkv, o, lse, do)


def _xattn_fwd(name, q, kv):
    S, D = q.shape
    _, M, dh = kv.shape
    scale = dh ** -0.5
    tq = _tile(S, 256, 16)

    def body(q_ref, kv_ref, o_ref):
        for h in range(X_HEADS):
            lo, hi = h * dh, (h + 1) * dh
            s = lax.dot_general(q_ref[:, lo:hi], kv_ref[h], _NT, preferred_element_type=F32) * scale
            p, l = _softmax_rows(s)
            o = jnp.dot(p.astype(BF16), kv_ref[X_HEADS + h], preferred_element_type=F32)
            o_ref[:, lo:hi] = (o / l).astype(BF16)

    row = pl.BlockSpec((tq, D), lambda i: (i, 0))
    return _pc(body, name=name, grid=(S // tq,),
               in_specs=[row, pl.BlockSpec((2 * X_HEADS, M, dh), lambda i: (0, 0, 0))],
               out_specs=row, out_shape=jax.ShapeDtypeStruct((S, D), BF16),
               compiler_params=_sem("parallel"))(q, kv)


def _xattn_bwd(name, q, kv, do):
    S, D = q.shape
    _, M, dh = kv.shape
    scale = dh ** -0.5
    tq = _tile(S, 256, 16)

    def body(q_ref, kv_ref, do_ref, dq_ref, dkv_ref):
        @pl.when(pl.program_id(0) == 0)
        def _():
            dkv_ref[...] = jnp.zeros_like(dkv_ref)

        for h in range(X_HEADS):
            lo, hi = h * dh, (h + 1) * dh
            qh, kh, vh, doh = q_ref[:, lo:hi], kv_ref[h], kv_ref[X_HEADS + h], do_ref[:, lo:hi]
            s = lax.dot_general(qh, kh, _NT, preferred_element_type=F32) * scale
            p, l = _softmax_rows(s)
            pn = p / l
            dkv_ref[X_HEADS + h] += lax.dot_general(pn.astype(BF16), doh, _TN, preferred_element_type=F32)
            dp = lax.dot_general(doh, vh, _NT, preferred_element_type=F32)
            ds = (pn * (dp - jnp.sum(pn * dp, axis=-1, keepdims=True)) * scale).astype(BF16)
            dq_ref[:, lo:hi] = jnp.dot(ds, kh, preferred_element_type=F32).astype(BF16)
            dkv_ref[h] += lax.dot_general(ds, qh, _TN, preferred_element_type=F32)

    row = pl.BlockSpec((tq, D), lambda i: (i, 0))
    full = pl.BlockSpec((2 * X_HEADS, M, dh), lambda i: (0, 0, 0))
    return _pc(body, name=name, grid=(S // tq,), in_specs=[row, full, row], out_specs=(row, full),
               out_shape=(jax.ShapeDtypeStruct((S, D), BF16), jax.ShapeDtypeStruct((2 * X_HEADS, M, dh), F32)),
               compiler_params=_sem("arbitrary"))(q, kv, do)


def _sigmoid(x):
    return 1.0 / (1.0 + jnp.exp(-x))


def _halo_specs(tm, n, S):
    nb = tm // 8
    last8 = S // 8 - 1
    main = pl.BlockSpec((2, None, tm, n), lambda j, i: (0, j, i, 0))
    prev = pl.BlockSpec((2, None, 8, n), lambda j, i: (0, j, jnp.maximum(i * nb - 1, 0), 0))
    nxt = pl.BlockSpec((2, None, 8, n), lambda j, i: (0, j, jnp.minimum((i + 1) * nb, last8), 0))
    return main, prev, nxt


def _ffn_up_act(name, h, w, cw, cb, layer):
    S, K = h.shape
    _, J, _, n = w.shape
    tm = _tile(S, 512, FFN_HALO)
    nblk = S // tm
    hb, last = tm // FFN_HALO, S // FFN_HALO - 1
    te = tm + 2 * FFN_HALO

    def body(h_ref, hp_ref, hn_ref, w_ref, cw_ref, b_ref, u_ref, a_ref):
        i = pl.program_id(1)
        zero = jnp.zeros((FFN_HALO, K), BF16)
        he = jnp.concatenate([jnp.where(i == 0, zero, hp_ref[...]), h_ref[...],
                              jnp.where(i == nblk - 1, zero, hn_ref[...])], axis=0)
        mid = slice(FFN_HALO, tm + FFN_HALO)
        c = []
        for half in range(2):
            ue = jnp.dot(he, w_ref[half], preferred_element_type=F32)
            um = ue[mid]
            u_ref[half] = um
            k = cw_ref[half]
            c.append(pltpu.roll(ue, 1, 0)[mid] * k[0:1] + um * k[1:2] + pltpu.roll(ue, te - 1, 0)[mid] * k[2:3]
                     + b_ref[half])
        a_ref[...] = (c[0] * _sigmoid(c[0]) * c[1]).astype(BF16)

    return _pc(body, name=name, grid=(J, nblk),
               in_specs=[pl.BlockSpec((tm, K), lambda j, i: (i, 0)),
                         pl.BlockSpec((FFN_HALO, K), lambda j, i: (jnp.maximum(i * hb - 1, 0), 0)),
                         pl.BlockSpec((FFN_HALO, K), lambda j, i: (jnp.minimum((i + 1) * hb, last), 0)),
                         pl.BlockSpec((2, None, K, n), lambda j, i: (0, j, layer, 0)),
                         pl.BlockSpec((2, None, 3, n), lambda j, i: (0, j, 0, 0)),
                         pl.BlockSpec((2, None, 1, n), lambda j, i: (0, j, 0, 0))],
               out_specs=(pl.BlockSpec((2, None, tm, n), lambda j, i: (0, j, i, 0)),
                          pl.BlockSpec((None, tm, n), lambda j, i: (j, i, 0))),
               out_shape=(jax.ShapeDtypeStruct((2, J, S, n), F32), jax.ShapeDtypeStruct((J, S, n), BF16)),
               compiler_params=_sem("parallel", "parallel"))(h, h, h, w, cw, cb)


def _ffn_act_bwd(name, u, g, w_down, cw, cb):
    _, J, S, n = u.shape
    D = g.shape[1]
    tm = _tile(S, 256, 16)
    nblk = S // tm
    te = tm + 16
    nb = tm // 8
    last8 = S // 8 - 1

    def body(u_ref, up_ref, un_ref, g_ref, gp_ref, gn_ref, wd_ref, w_ref, b_ref, du_ref, st_ref):
        i = pl.program_id(1)

        @pl.when(i == 0)
        def _():
            st_ref[...] = jnp.zeros_like(st_ref)

        def extended(before, main, after):
            return jnp.concatenate([jnp.where(i == 0, 0.0, before), main, jnp.where(i == nblk - 1, 0.0, after)], axis=0)

        mid = slice(8, tm + 8)
        da_e = lax.dot_general(extended(gp_ref[...], g_ref[...], gn_ref[...]).astype(BF16), wd_ref[...], _NT,
                               preferred_element_type=F32)
        ue, c = [], []
        for half in range(2):
            e = extended(up_ref[half], u_ref[half], un_ref[half])
            w = w_ref[half]
            ue.append((pltpu.roll(e, 1, 0), e, pltpu.roll(e, te - 1, 0)))
            c.append(ue[half][0] * w[0:1] + e * w[1:2] + ue[half][2] * w[2:3] + b_ref[half])
        sg = _sigmoid(c[0])
        dc = [da_e * c[1] * (sg * (1.0 + c[0] * (1.0 - sg))), da_e * (c[0] * sg)]
        r8 = lax.broadcasted_iota(jnp.int32, (8, n), 0)
        for half in range(2):
            w, d, (e_before, e, e_after) = w_ref[half], dc[half], ue[half]
            dm = d[mid]
            du = pltpu.roll(d, te - 1, 0)[mid] * w[0:1] + dm * w[1:2] + pltpu.roll(d, 1, 0)[mid] * w[2:3]
            du_ref[half] = du.astype(BF16)
            s0 = jnp.sum(dm * e_before[mid], axis=0, keepdims=True)
            s1 = jnp.sum(dm * e[mid], axis=0, keepdims=True)
            s2 = jnp.sum(dm * e_after[mid], axis=0, keepdims=True)
            s3 = jnp.sum(dm, axis=0, keepdims=True)
            st_ref[half] += jnp.where(r8 == 0, s0, jnp.where(r8 == 1, s1, jnp.where(r8 == 2, s2,
                                      jnp.where(r8 == 3, s3, 0.0))))

    main, prev, nxt = _halo_specs(tm, n, S)
    gmain = pl.BlockSpec((tm, D), lambda j, i: (i, 0))
    gprev = pl.BlockSpec((8, D), lambda j, i: (jnp.maximum(i * nb - 1, 0), 0))
    gnxt = pl.BlockSpec((8, D), lambda j, i: (jnp.minimum((i + 1) * nb, last8), 0))
    return _pc(body, name=name, grid=(J, nblk),
               in_specs=[main, prev, nxt, gmain, gprev, gnxt, pl.BlockSpec((n, D), lambda j, i: (j, 0)),
                         pl.BlockSpec((2, None, 3, n), lambda j, i: (0, j, 0, 0)),
                         pl.BlockSpec((2, None, 1, n), lambda j, i: (0, j, 0, 0))],
               out_specs=(main, pl.BlockSpec((2, None, 8, n), lambda j, i: (0, j, 0, 0))),
               out_shape=(jax.ShapeDtypeStruct((2, J, S, n), BF16), jax.ShapeDtypeStruct((2, J, 8, n), F32)),
               compiler_params=_sem("parallel", "arbitrary"))(u, u, u, g, g, g, w_down, cw, cb)


def _window_count(t, w, S):
    lo = jnp.maximum(t - w // 2, 0)
    hi = jnp.minimum(t + w - w // 2, S)
    return (hi - lo).astype(F32)


def _trailing_sums(x, w):
    k = 1
    while k < w:
        x = x + pltpu.roll(x, k, 0)
        k *= 2
    return x


def _pool_window(name, h, group_w, adjoint, out_dtype):
    S, D = h.shape
    SP = S + 2 * POOL_PAD
    per_group = group_w // LANES

    def body(h_ref, o_ref, xp):
        g = pl.program_id(0) // per_group
        t = lax.broadcasted_iota(jnp.int32, (S, LANES), 0)
        xp[0:POOL_PAD, :] = jnp.zeros((POOL_PAD, LANES), F32)
        xp[S + POOL_PAD:SP, :] = jnp.zeros((POOL_PAD, LANES), F32)
        for gi, w in enumerate(POOL_WINDOWS):
            @pl.when(g == gi)
            def _():
                hv = h_ref[...]
                cnt = _window_count(t, w, S)
                xp[POOL_PAD:S + POOL_PAD, :] = hv / cnt if adjoint else hv
                ahead = w // 2 if adjoint else w // 2 - 1
                sw = _trailing_sums(xp[...], w)
                if ahead:
                    sw = pltpu.roll(sw, SP - ahead, 0)
                win = sw[POOL_PAD:S + POOL_PAD]
                o_ref[...] = ((win if adjoint else win / cnt) - hv).astype(out_dtype)

    col = pl.BlockSpec((S, LANES), lambda j: (0, j))
    return _pc(body, name=name, grid=(D // LANES,), in_specs=[col], out_specs=col,
               out_shape=jax.ShapeDtypeStruct((S, D), out_dtype),
               scratch_shapes=[pltpu.VMEM((SP, LANES), F32)], compiler_params=_sem("parallel"))(h)


def _pool_proj(name, mixed, w, scale, res):
    S, D = mixed.shape
    G, gw, _ = w.shape
    tm = _tile(S, 512, 16)

    def body(m_ref, w_ref, s_ref, r_ref, o_ref):
        for g in range(G):
            lo, hi = g * gw, (g + 1) * gw
            y = jnp.dot(m_ref[:, lo:hi], w_ref[g], preferred_element_type=F32)
            o_ref[:, lo:hi] = r_ref[:, lo:hi] + y * s_ref[:, lo:hi]

    row = pl.BlockSpec((tm, D), lambda i: (i, 0))
    return _pc(body, name=name, grid=(S // tm,),
               in_specs=[row, pl.BlockSpec((G, gw, gw), lambda i: (0, 0, 0)), pl.BlockSpec((1, D), lambda i: (0, 0)), row],
               out_specs=row, out_shape=jax.ShapeDtypeStruct((S, D), F32),
               compiler_params=_sem("parallel"))(mixed, w, scale, res)


def _pool_proj_bwd(name, dy, mixed, w, scale):
    S, D = mixed.shape
    G, gw, _ = w.shape
    tm = _tile(S, 512, 16)

    def body(dy_ref, m_ref, w_ref, s_ref, dm_ref, dw_ref, ds_ref):
        @pl.when(pl.program_id(0) == 0)
        def _():
            dw_ref[...] = jnp.zeros_like(dw_ref)
            ds_ref[...] = jnp.zeros_like(ds_ref)

        for g in range(G):
            lo, hi = g * gw, (g + 1) * gw
            mg, dyg = m_ref[:, lo:hi], dy_ref[:, lo:hi]
            y = jnp.dot(mg, w_ref[g], preferred_element_type=F32)
            ds_ref[:, lo:hi] += jnp.sum(dyg * y, axis=0, keepdims=True)
            dyp = (dyg * s_ref[:, lo:hi]).astype(BF16)
            dm_ref[:, lo:hi] = lax.dot_general(dyp, w_ref[g], _NT, preferred_element_type=F32)
            dw_ref[g] += lax.dot_general(mg, dyp, _TN, preferred_element_type=F32)

    row = pl.BlockSpec((tm, D), lambda i: (i, 0))
    wsp = pl.BlockSpec((G, gw, gw), lambda i: (0, 0, 0))
    vec = pl.BlockSpec((1, D), lambda i: (0, 0))
    return _pc(body, name=name, grid=(S // tm,), in_specs=[row, row, wsp, vec], out_specs=(row, wsp, vec),
               out_shape=(jax.ShapeDtypeStruct((S, D), F32), jax.ShapeDtypeStruct((G, gw, gw), F32),
                          jax.ShapeDtypeStruct((1, D), F32)),
               compiler_params=_sem("arbitrary"))(dy, mixed, w, scale)


def _adamw(name, w, g, m, v):
    shape = w.shape
    C = shape[-1]
    R = w.size // C
    tm = _tile(R, 512, 8)

    def body(w_ref, g_ref, m_ref, v_ref, d_ref, nm_ref, nv_ref):
        gv = g_ref[...]
        nm = ADAM_B1 * m_ref[...] + (1.0 - ADAM_B1) * gv
        nv = ADAM_B2 * v_ref[...] + (1.0 - ADAM_B2) * (gv * gv)
        m_hat = nm / (1.0 - ADAM_B1 ** ADAM_STEP)
        v_hat = nv / (1.0 - ADAM_B2 ** ADAM_STEP)
        d_ref[...] = -ADAM_LR * (m_hat / (jnp.sqrt(v_hat) + ADAM_EPS) + ADAM_WD * w_ref[...])
        nm_ref[...] = nm
        nv_ref[...] = nv

    blk = pl.BlockSpec((tm, C), lambda i: (i, 0))
    sd = jax.ShapeDtypeStruct((R, C), F32)
    outs = _pc(body, name=name, grid=(R // tm,), in_specs=[blk] * 4, out_specs=(blk,) * 3, out_shape=(sd,) * 3,
               compiler_params=_sem("parallel"))(*(a.reshape(R, C) for a in (w, g, m, v)))
    return tuple(o.reshape(shape) for o in outs)


def _position():
    return lax.axis_index("x"), lax.axis_index("y"), lax.axis_index("c")


def _flip(v, bit):
    return 1 - v if bit else v


def _allgather_small(name, v):
    R, W = v.shape

    def body(v_ref, out_ref, send_sems, recv_sems):
        x, y, c = _position()
        me = 4 * x + 2 * y + c
        out_ref[me] = v_ref[...]
        sends = []
        for k in range(1, N_DEV):
            peer = (_flip(x, k & 4), _flip(y, k & 2), _flip(c, k & 1))
            cp = pltpu.make_async_remote_copy(src_ref=v_ref, dst_ref=out_ref.at[me], send_sem=send_sems.at[k - 1],
                                              recv_sem=recv_sems.at[k - 1], device_id=peer, device_id_type=MESH)
            cp.start()
            sends.append(cp)
        for k in range(1, N_DEV):
            peer = (_flip(x, k & 4), _flip(y, k & 2), _flip(c, k & 1))
            slot = 4 * peer[0] + 2 * peer[1] + peer[2]
            pltpu.make_async_remote_copy(src_ref=v_ref, dst_ref=out_ref.at[slot], send_sem=send_sems.at[k - 1],
                                         recv_sem=recv_sems.at[k - 1], device_id=peer, device_id_type=MESH).wait_recv()
        for cp in sends:
            cp.wait_send()

    vm = pl.BlockSpec(memory_space=pltpu.VMEM)
    return _pc(body, name=name, in_specs=[vm], out_specs=vm, out_shape=jax.ShapeDtypeStruct((N_DEV, R, W), F32),
               scratch_shapes=[pltpu.SemaphoreType.DMA((N_DEV - 1,)), pltpu.SemaphoreType.DMA((N_DEV - 1,))])(v)


def _sum_slots(name, a):
    n, R, W = a.shape

    def body(a_ref, o_ref):
        acc = a_ref[0]
        for s in range(1, n):
            acc = acc + a_ref[s]
        o_ref[...] = acc

    return _pc(body, name=name, grid=(1,), in_specs=[pl.BlockSpec((n, R, W), lambda i: (0, 0, 0))],
               out_specs=pl.BlockSpec((R, W), lambda i: (0, 0)), out_shape=jax.ShapeDtypeStruct((R, W), F32))(a)


def _allgather_blocks(name, blocks):
    n = len(blocks)

    def body(*refs):
        b_refs, out_refs, token = refs[:n], refs[n:2 * n], refs[2 * n]
        send_sems, recv_sems, local_sems = refs[2 * n + 1:]
        token[...] = jnp.zeros_like(token)
        x, y, c = _position()
        me, sibling = (x, y, c), (x, y, 1 - c)
        chips = [(1 - x, y), (x, 1 - y), (1 - x, 1 - y)]

        def slot(i, px, py, pc):
            return out_refs[i].at[4 * px + 2 * py + pc]

        def copy(i, k, block, to, src=None):
            return pltpu.make_async_remote_copy(src_ref=slot(i, *block) if src is None else src, dst_ref=slot(i, *block),
                                                send_sem=send_sems.at[k, i], recv_sem=recv_sems.at[k, i],
                                                device_id=to, device_id_type=MESH)

        mine = [pltpu.make_async_copy(b_refs[i], slot(i, *me), local_sems.at[i]) for i in range(n)]
        first = [copy(i, 1 + j, me, (*chip, c), src=b_refs[i]) for i in range(n) for j, chip in enumerate(chips)]
        first += [copy(i, 0, me, sibling, src=b_refs[i]) for i in range(n)]
        for cp in mine + first:
            cp.start()
        passed = []
        for j, chip in enumerate(chips):
            for i in range(n):
                copy(i, 1 + j, (*chip, c), me).wait_recv()
                passed.append(copy(i, 4 + j, (*chip, c), sibling))
                passed[-1].start()
        for i in range(n):
            copy(i, 0, sibling, me).wait_recv()
        for j, chip in enumerate(chips):
            for i in range(n):
                copy(i, 4 + j, (*chip, 1 - c), me).wait_recv()
        for cp in first + passed:
            cp.wait_send()
        for cp in mine:
            cp.wait()

    hbm = pl.BlockSpec(memory_space=pl.ANY)
    return _pc(body, name=name, in_specs=[hbm] * n, out_specs=[hbm] * n + [pl.BlockSpec(memory_space=pltpu.VMEM)],
               out_shape=[jax.ShapeDtypeStruct((N_DEV,) + b.shape, b.dtype) for b in blocks]
               + [jax.ShapeDtypeStruct((8, LANES), F32)],
               scratch_shapes=[pltpu.SemaphoreType.DMA((7, n)), pltpu.SemaphoreType.DMA((7, n)),
                               pltpu.SemaphoreType.DMA((n,))])(*blocks)


def _add_sibling(name, g4, r1, pos):
    n, _, L, W = g4.shape
    tl = _tile(L, 512, 16)

    def body(pos_ref, g_ref, r_ref, tb_ref, own_ref):
        t = g_ref[...] + r_ref[...]
        tb_ref[...] = t.astype(BF16)

        @pl.when(pl.program_id(1) == pos_ref[1])
        def _():
            own_ref[...] = t

    gs = pltpu.PrefetchScalarGridSpec(
        num_scalar_prefetch=1, grid=(L // tl, n),
        in_specs=[pl.BlockSpec((None, None, tl, W), lambda i, k, p: (k, p[0], i, 0)),
                  pl.BlockSpec((None, tl, W), lambda i, k, p: (k, i, 0))],
        out_specs=(pl.BlockSpec((None, tl, W), lambda i, k, p: (k, i, 0)),
                   pl.BlockSpec((tl, W), lambda i, k, p: (i, 0))))
    return _pc(body, name=name, grid_spec=gs,
               out_shape=(jax.ShapeDtypeStruct((n, L, W), BF16), jax.ShapeDtypeStruct((L, W), F32)),
               compiler_params=_sem("parallel", "arbitrary"))(pos, g4, r1)


def _add_chips(name, own, r2):
    L, W = own.shape
    tl = _tile(L, 512, 16)

    def body(o_ref, r_ref, out_ref):
        acc = o_ref[...]
        for j in range(3):
            acc = acc + r_ref[j].astype(F32)
        out_ref[...] = acc

    return _pc(body, name=name, grid=(L // tl,),
               in_specs=[pl.BlockSpec((tl, W), lambda i: (i, 0)), pl.BlockSpec((3, tl, W), lambda i: (0, i, 0))],
               out_specs=pl.BlockSpec((tl, W), lambda i: (i, 0)), out_shape=jax.ShapeDtypeStruct((L, W), F32),
               compiler_params=_sem("parallel"))(own, r2)


_HBM = pl.BlockSpec(memory_space=pltpu.HBM)
_SEM = pl.BlockSpec(memory_space=pltpu.SEMAPHORE)
_EFFECT = pltpu.SideEffectType.DATAFLOW_SIDE_EFFECTING


def _in_hbm(a):
    return pltpu.with_memory_space_constraint(a, pltpu.HBM)


def _after(x, token):
    return x + token[0, 0].astype(x.dtype)


def _copies_start(name, bufs, sem_shape, plan):
    nb = len(bufs)

    def body(*refs):
        for cp in plan(refs[:nb], refs[nb], refs[nb + 1]):
            cp.start()
        refs[-1][...] = jnp.zeros_like(refs[-1])

    out = _pc(body, name=name, in_specs=[_HBM] * nb,
              out_specs=(_SEM, _SEM, *[_HBM] * nb, pl.BlockSpec(memory_space=pltpu.VMEM)),
              out_shape=(pltpu.SemaphoreType.DMA(sem_shape), pltpu.SemaphoreType.DMA(sem_shape),
                         *[pltpu.HBM(b.shape, b.dtype) for b in bufs], jax.ShapeDtypeStruct((8, LANES), F32)),
              input_output_aliases={i: 2 + i for i in range(nb)},
              compiler_params=pltpu.CompilerParams(has_side_effects=_EFFECT))(*[_in_hbm(b) for b in bufs])
    return out[0], out[1], list(out[2:2 + nb]), out[-1]


def _copies_wait(name, bufs, send_sems, recv_sems, plan, after):
    nb = len(bufs)

    def body(*refs):
        for cp in plan(refs[:nb], refs[nb], refs[nb + 1]):
            cp.wait_send()
            cp.wait_recv()

    return list(_pc(body, name=name, in_specs=[_HBM] * nb + [_SEM, _SEM, pl.BlockSpec(memory_space=pl.ANY)],
                    out_specs=[_HBM] * nb, out_shape=[pltpu.HBM(b.shape, b.dtype) for b in bufs],
                    input_output_aliases={i: i for i in range(nb)},
                    compiler_params=pltpu.CompilerParams(has_side_effects=_EFFECT))(*bufs, send_sems, recv_sems, after))


def _plan_gather_chips(n):
    def plan(refs, send_sems, recv_sems):
        x, y, c = _position()
        peers = [(x, y, 1 - c), (1 - x, y, c), (x, 1 - y, c), (1 - x, 1 - y, c)]
        return [pltpu.make_async_remote_copy(src_ref=refs[i], dst_ref=refs[n + i].at[4 * x + 2 * y + c],
                                             send_sem=send_sems.at[k * n + i], recv_sem=recv_sems.at[k * n + i],
                                             device_id=peer, device_id_type=MESH)
                for i in range(n) for k, peer in enumerate(peers)]
    return plan


def _plan_gather_sibling(n):
    def plan(refs, send_sems, recv_sems):
        x, y, c = _position()
        slots = [4 * (1 - x) + 2 * y + c, 4 * x + 2 * (1 - y) + c, 4 * (1 - x) + 2 * (1 - y) + c]
        return [pltpu.make_async_remote_copy(src_ref=refs[i].at[s], dst_ref=refs[i].at[s],
                                             send_sem=send_sems.at[k * n + i], recv_sem=recv_sems.at[k * n + i],
                                             device_id=(x, y, 1 - c), device_id_type=MESH)
                for i in range(n) for k, s in enumerate(slots)]
    return plan


def _plan_reduce_sibling(n):
    def plan(refs, send_sems, recv_sems):
        x, y, c = _position()
        return [pltpu.make_async_remote_copy(src_ref=refs[i].at[k, 1 - c], dst_ref=refs[n + i].at[k],
                                             send_sem=send_sems.at[k * n + i], recv_sem=recv_sems.at[k * n + i],
                                             device_id=(x, y, 1 - c), device_id_type=MESH)
                for i in range(n) for k in range(N_DEV // 2)]
    return plan


def _plan_reduce_chips(n):
    def plan(refs, send_sems, recv_sems):
        x, y, c = _position()
        cps = []
        for i in range(n):
            for j in range(1, 4):
                px, py = _flip(x, j & 2), _flip(y, j & 1)
                sem = (j - 1) * n + i
                cps.append(pltpu.make_async_remote_copy(src_ref=refs[i].at[2 * px + py], dst_ref=refs[n + i].at[j - 1],
                                                        send_sem=send_sems.at[sem], recv_sem=recv_sems.at[sem],
                                                        device_id=(px, py, c), device_id_type=MESH))
        return cps
    return plan


def _heads_major(name, a, after):
    S, W = a.shape
    H = W // HEAD_DIM
    tm = _tile(S, 512, 16)

    def body(a_ref, after_ref, o_ref):
        v = a_ref[...]
        for h in range(H):
            o_ref[h] = v[:, h * HEAD_DIM:(h + 1) * HEAD_DIM]

    return _pc(body, name=name, grid=(S // tm,),
               in_specs=[pl.BlockSpec((tm, W), lambda i: (i, 0)), pl.BlockSpec(memory_space=pl.ANY)],
               out_specs=pl.BlockSpec((H, tm, HEAD_DIM), lambda i: (0, i, 0)),
               out_shape=jax.ShapeDtypeStruct((H, S, HEAD_DIM), a.dtype), compiler_params=_sem("parallel"))(a, after)


def _heads_minor(name, a):
    H, S, _ = a.shape
    tm = _tile(S, 512, 16)

    def body(a_ref, o_ref):
        o_ref[...] = jnp.concatenate([a_ref[h] for h in range(H)], axis=1)

    return _pc(body, name=name, grid=(S // tm,), in_specs=[pl.BlockSpec((H, tm, HEAD_DIM), lambda i: (0, i, 0))],
               out_specs=pl.BlockSpec((tm, H * HEAD_DIM), lambda i: (i, 0)),
               out_shape=jax.ShapeDtypeStruct((S, H * HEAD_DIM), a.dtype), compiler_params=_sem("parallel"))(a)


def kernel(x, mem, attn_norm, attn_w_qkv, attn_q_gain, attn_k_gain, attn_w_o, pool_norm, pool_w, pool_scale, xattn_norm, mem_norm, xattn_w_q, xattn_w_kv, xattn_w_o, ffn_norm, ffn_w_up, ffn_conv_w, ffn_conv_b, ffn_w_down, final_norm, loss_target, m_attn_norm, m_attn_w_qkv, m_attn_q_gain, m_attn_k_gain, m_attn_w_o, m_pool_norm, m_pool_w, m_pool_scale, m_xattn_norm, m_mem_norm, m_xattn_w_q, m_xattn_w_kv, m_xattn_w_o, m_ffn_norm, m_ffn_w_up, m_ffn_conv_w, m_ffn_conv_b, m_ffn_w_down, m_final_norm, v_attn_norm, v_attn_w_qkv, v_attn_q_gain, v_attn_k_gain, v_attn_w_o, v_pool_norm, v_pool_w, v_pool_scale, v_xattn_norm, v_mem_norm, v_xattn_w_q, v_xattn_w_kv, v_xattn_w_o, v_ffn_norm, v_ffn_w_up, v_ffn_conv_w, v_ffn_conv_b, v_ffn_w_down, v_final_norm):
    names = ['attn_norm', 'attn_w_qkv', 'attn_q_gain', 'attn_k_gain', 'attn_w_o', 'pool_norm', 'pool_w', 'pool_scale',
             'xattn_norm', 'mem_norm', 'xattn_w_q', 'xattn_w_kv', 'xattn_w_o', 'ffn_norm', 'ffn_w_up', 'ffn_conv_w',
             'ffn_conv_b', 'ffn_w_down', 'final_norm']
    W = dict(zip(names, (attn_norm, attn_w_qkv, attn_q_gain, attn_k_gain, attn_w_o, pool_norm, pool_w, pool_scale,
                         xattn_norm, mem_norm, xattn_w_q, xattn_w_kv, xattn_w_o, ffn_norm, ffn_w_up, ffn_conv_w,
                         ffn_conv_b, ffn_w_down, final_norm)))
    Mo = dict(zip(names, (m_attn_norm, m_attn_w_qkv, m_attn_q_gain, m_attn_k_gain, m_attn_w_o, m_pool_norm, m_pool_w,
                          m_pool_scale, m_xattn_norm, m_mem_norm, m_xattn_w_q, m_xattn_w_kv, m_xattn_w_o, m_ffn_norm,
                          m_ffn_w_up, m_ffn_conv_w, m_ffn_conv_b, m_ffn_w_down, m_final_norm)))
    Vo = dict(zip(names, (v_attn_norm, v_attn_w_qkv, v_attn_q_gain, v_attn_k_gain, v_attn_w_o, v_pool_norm, v_pool_w,
                          v_pool_scale, v_xattn_norm, v_mem_norm, v_xattn_w_q, v_xattn_w_kv, v_xattn_w_o, v_ffn_norm,
                          v_ffn_w_up, v_ffn_conv_w, v_ffn_conv_b, v_ffn_w_down, v_final_norm)))

    S, D = x.shape[1], x.shape[2]
    n_layers = xattn_norm.shape[0]
    n_up = ffn_w_up.shape[2]
    qkv_w = attn_w_qkv.shape[2] * N_DEV
    n_heads = qkv_w // HEAD_DIM - 2 * N_KV_HEADS
    n_rot = (n_heads + N_KV_HEADS) * HEAD_DIM // LANES
    group_w = pool_w.shape[3]
    xs, mems, tgt = x[0], mem[0], loss_target[0]
    xi, yi, ci = _position()
    dev = 4 * xi + 2 * yi + ci
    pos = jnp.stack([ci, 2 * xi + yi]).astype(jnp.int32)

    layers = range(n_layers)
    n_groups = pool_w.shape[1]
    small_vec = jnp.concatenate([pool_norm.reshape(-1), pool_scale.reshape(-1), ffn_conv_w.reshape(-1)])
    small_rows = _round_up(-(-small_vec.size // PACK_W), 8)
    small_vec = jnp.pad(small_vec, (0, small_rows * PACK_W - small_vec.size)).reshape(small_rows, PACK_W)
    w_qkv, w_o, small, attn_token = _allgather_blocks(
        "allgather_attn", [attn_w_qkv[0].astype(BF16), attn_w_o[0].astype(BF16), small_vec])
    small = small.reshape(N_DEV, -1)
    w_qkv = w_qkv.transpose(1, 0, 2).reshape(D, qkv_w)
    w_o = w_o.reshape(-1, D)
    blocks = [pool_w.reshape(-1, group_w)] + [xattn_w_q[l] for l in layers] + [xattn_w_kv.reshape(n_layers * D, -1)]
    blocks += [xattn_w_o[l] for l in layers] + [ffn_w_up.reshape(n_layers * D, n_up)] + [ffn_w_down[l] for l in layers]
    blocks = [b.astype(BF16) for b in blocks]
    blocks[0] = _after(blocks[0], attn_token)
    n_blk = len(blocks)
    lands = [lax.dynamic_update_index_in_dim(lax.empty((N_DEV,) + b.shape, BF16), b, dev, 0) for b in blocks]
    plan_chips, plan_sibling = _plan_gather_chips(n_blk), _plan_gather_sibling(n_blk)
    gather_sems = _copies_start("gather_chips_start", blocks + lands, (4 * n_blk,), plan_chips)
    attn_norm_late = _after(attn_norm, gather_sems[3])

    d_sh = pool_norm.shape[1]
    pool_norm_f = small[:, :d_sh].reshape(1, D)
    pool_scale_f = small[:, d_sh:2 * d_sh].reshape(1, D)
    conv_w_f = small[:, 2 * d_sh:2 * d_sh + ffn_conv_w.size].reshape(N_DEV, n_layers, 3, n_up)
    conv_b_f = ffn_conv_b.reshape(n_layers, N_DEV, 1, n_up)

    cos, sin = _rope_tables(S)
    bd = _head_mean_matrix()
    pad_w = qkv_w - (n_heads + N_KV_HEADS) * HEAD_DIM
    qk_gain = jnp.concatenate([jnp.tile(attn_q_gain[0], n_heads), jnp.tile(attn_k_gain[0], N_KV_HEADS),
                               jnp.ones((pad_w,), F32)]).reshape(1, qkv_w)
    qk_scale = jnp.concatenate([jnp.full((n_heads * HEAD_DIM,), HEAD_DIM ** -0.5, F32),
                                jnp.ones((qkv_w - n_heads * HEAD_DIM,), F32)]).reshape(1, qkv_w)

    saved = []

    def xattn_ffn_fwd(l, xin, hx=None):
        if hx is None:
            hx = _rmsnorm(f"xattn_norm{l}", xin, xattn_norm[l:l + 1], BF16)
        memn = _rmsnorm(f"mem_norm{l}", mems, mem_norm[l:l + 1], BF16)
        qx = _mm_nn(f"xattn_q{l}", hx, w_xq[l], BF16)
        kv = _mm_nn_bs(f"xattn_kv{l}", memn, w_xkv, BF16, l)
        ox = _xattn_fwd(f"xattn_fwd{l}", qx, kv)
        x2 = _mm_nn(f"xattn_o{l}", ox, w_xo[l], F32, res=xin)
        hf = _rmsnorm(f"ffn_norm{l}", x2, ffn_norm[l:l + 1], BF16)
        cw = conv_w_f[:, l].reshape(2, N_DEV // 2, 3, n_up)
        cb = conv_b_f[l].reshape(2, N_DEV // 2, 1, n_up)
        u, act = _ffn_up_act(f"ffn_up_act{l}", hf, w_up.reshape(2, N_DEV // 2, n_layers * D, n_up), cw, cb, l)
        x3 = _mm_nn_as(f"ffn_down{l}", act, w_down[l], F32, x2)
        saved.append(dict(xin=xin, hx=hx, memn=memn, qx=qx, kv=kv, ox=ox, x2=x2, hf=hf, u=u, cw=cw, cb=cb, act=act))
        return x3

    h0 = _rmsnorm("attn_norm", xs, attn_norm_late, BF16)
    qkv = _mm_nn("attn_qkv", h0, w_qkv, F32)
    qkr = _qk_rope("qk_rope", qkv, qk_gain, qk_scale, cos, sin, bd, n_rot)
    o_hm, lse = _attn_fwd("attn_fwd", qkr, n_heads)
    o_att = _heads_minor("attn_heads_minor", o_hm)
    arrived = _copies_wait("gather_chips_wait", gather_sems[2], gather_sems[0], gather_sems[1], plan_chips, o_att)
    pass_sems = _copies_start("gather_sibling_start", arrived[n_blk:], (3 * n_blk,), plan_sibling)
    x1 = _mm_nn("attn_o", o_att, _after(w_o, pass_sems[3]), F32, res=xs)
    hx0 = _rmsnorm("xattn_norm0", x1, xattn_norm[0:1], BF16)
    gathered = iter(_copies_wait("gather_sibling_wait", pass_sems[2], pass_sems[0], pass_sems[1], plan_sibling, hx0))
    w_pool = (next(gathered).reshape(N_DEV, n_groups, -1, group_w).transpose(1, 0, 2, 3)
              .reshape(n_groups, group_w, group_w))
    w_xq = [next(gathered).reshape(D, D) for l in layers]
    w_xkv = next(gathered)
    w_xo = [next(gathered).reshape(D, D) for l in layers]
    w_up = next(gathered)
    w_down = [next(gathered).reshape(-1, D) for l in layers]
    x3 = xattn_ffn_fwd(0, x1, hx0)
    hp = _rmsnorm("pool_norm", x3, pool_norm_f, F32)
    mixed = _pool_window("pool_window", hp, group_w, False, BF16)
    x4 = _pool_proj("pool_proj", mixed, w_pool, pool_scale_f, x3)
    x6 = xattn_ffn_fwd(1, x4)

    G = {}
    g, d_final, lvec = _loss_head("loss_head", x6, final_norm.reshape(1, D), tgt)
    G['final_norm'] = d_final.reshape(D)
    loss_part = (0.5 * jnp.sum(lvec) / D).reshape(1)

    d_xn, d_mn, d_fn, d_xq, d_xkv, d_xo, d_up, d_cw, d_cb, d_down = ([None] * n_layers for _ in range(10))

    def xattn_ffn_bwd(l, g, conv_b_late=None, after_act=None):
        sv = saved[l]
        d_down[l] = _mm_tn_as(f"ffn_down_dw{l}", sv['act'], g, F32)
        du, st = _ffn_act_bwd(f"ffn_act_bwd{l}", sv['u'], g, w_down[l], sv['cw'],
                              sv['cb'] if conv_b_late is None else conv_b_late)
        du = du.reshape(N_DEV, S, n_up)
        ffn_gain = ffn_norm[l:l + 1] if after_act is None else _after(ffn_norm[l:l + 1], after_act(du))
        st = st.reshape(N_DEV, 8, n_up)
        d_cw[l], d_cb[l] = st[:, 0:3], st[:, 3].reshape(-1)
        d_up[l] = _mm_tn_bs(f"ffn_up_dw{l}", sv['hf'], du, F32)
        dhf = _mm_nt_abs(f"ffn_up_dx{l}", du, w_up, D, F32, l)
        g, d_fn[l] = _rmsnorm_bwd(f"ffn_norm_bwd{l}", sv['x2'], ffn_gain, dhf, g)
        d_xo[l] = _mm_tn(f"xattn_o_dw{l}", sv['ox'], g, F32)
        do = _mm_nt(f"xattn_o_dx{l}", g, w_xo[l], BF16)
        dq, dkv = _xattn_bwd(f"xattn_bwd{l}", sv['qx'], sv['kv'], do)
        d_xq[l] = _mm_tn(f"xattn_q_dw{l}", sv['hx'], dq, F32)
        d_xkv[l] = _mm_tn_bs(f"xattn_kv_dw{l}", sv['memn'], dkv, F32)
        dmemn = _mm_nt_abs(f"xattn_kv_dx{l}", dkv, w_xkv, D, F32, l)
        _, d_mn[l] = _rmsnorm_bwd(f"mem_norm_bwd{l}", mems, mem_norm[l:l + 1], dmemn)
        dhx = _mm_nt(f"xattn_q_dx{l}", dq, w_xq[l], F32)
        g, d_xn[l] = _rmsnorm_bwd(f"xattn_norm_bwd{l}", sv['xin'], xattn_norm[l:l + 1], dhx, g)
        return g

    def reduce_start(tag, bufs):
        n = len(bufs)
        g4s = [b.reshape((N_DEV // 2, 2) + b.shape[1:]) for b in bufs]
        lands = [lax.empty((N_DEV // 2,) + b.shape[1:], F32) for b in bufs]
        plan = _plan_reduce_sibling(n)
        return (n, plan) + _copies_start(f"reduce_sibling_start_{tag}", g4s + lands, (N_DEV // 2 * n,), plan)

    def reduce_between(tag, state, after):
        n, plan, send_sems, recv_sems, thru, _ = state
        got = _copies_wait(f"reduce_sibling_wait_{tag}", thru, send_sems, recv_sems, plan, after)
        sums = [_add_sibling(f"reduce_add_sibling_{tag}{i}", got[i], got[n + i], pos) for i in range(n)]
        lands = [lax.empty((3,) + tb.shape[1:], BF16) for tb, _ in sums]
        plan = _plan_reduce_chips(n)
        return (n, plan, [own for _, own in sums]) + _copies_start(f"reduce_chips_start_{tag}",
                                                                    [tb for tb, _ in sums] + lands, (3 * n,), plan)

    def reduce_finish(tag, state, after):
        n, plan, owns, send_sems, recv_sems, thru, _ = state
        got = _copies_wait(f"reduce_chips_wait_{tag}", thru, send_sems, recv_sems, plan, after)
        return [_add_chips(f"reduce_add_chips_{tag}{i}", owns[i], got[n + i]) for i in range(n)]

    def layer_bufs(l):
        return [d_xq[l].reshape(N_DEV, -1, D), d_xkv[l], d_xo[l].reshape(N_DEV, -1, D), d_up[l],
                d_down[l].reshape(N_DEV, -1, D)]

    g = xattn_ffn_bwd(1, g)
    d_mixed, d_pool_w, d_pool_scale = _pool_proj_bwd("pool_proj_bwd", g, mixed, w_pool, pool_scale_f)
    dhp = _pool_window("pool_window_bwd", d_mixed, group_w, True, F32)
    g, d_pool_norm = _rmsnorm_bwd("pool_norm_bwd", x3, pool_norm_f, dhp, g)
    upper = reduce_start("upper", [d_pool_w.reshape(n_groups, N_DEV, -1, group_w).transpose(1, 0, 2, 3)
                                   .reshape(N_DEV, -1, group_w)] + layer_bufs(1))
    between = []

    def upper_between(du):
        between.append(reduce_between("upper", upper, du))
        return between[0][-1]

    g = xattn_ffn_bwd(0, g, _after(saved[0]['cb'], upper[-1]), upper_between)
    d_wo = _mm_tn("attn_o_dw", o_att, g, F32)
    lower = reduce_start("lower", layer_bufs(0) + [d_wo.reshape(N_DEV, -1, D)])
    do = _mm_nt("attn_o_dx", g, _after(w_o, lower[-1]), BF16)
    lower = reduce_between("lower", lower, do)
    do_hm = _heads_major("attn_heads_major", do, lower[-1])
    dq_hm, dk_hm, dv_hm = _attn_bwd("attn_bwd", qkr, o_hm, lse, do_hm)
    red_lower = reduce_finish("lower", lower, dq_hm)
    d_qkv, d_gain = _qk_rope_bwd("qk_rope_bwd", dq_hm, dk_hm, dv_hm, qkv, qk_gain, qk_scale, cos, sin, bd)
    red_upper = reduce_finish("upper", between[0], d_qkv)
    d_wqkv = _mm_tn("attn_qkv_dw", h0, d_qkv, F32)
    last = reduce_start("last", [d_wqkv.reshape(D, N_DEV, -1).transpose(1, 0, 2)])
    dh0 = _mm_nt("attn_qkv_dx", d_qkv, _after(w_qkv, last[-1]), F32)
    last = reduce_between("last", last, dh0)
    grad_x, d_attn_norm = _rmsnorm_bwd("attn_norm_bwd", xs, _after(attn_norm, last[-1]), dh0, g)
    G['attn_w_o'] = red_lower[-1][None]
    G['pool_w'] = red_upper[0].reshape(pool_w.shape)
    per_layer = [red_lower, red_upper[1:]]
    for i, n in enumerate(['xattn_w_q', 'xattn_w_kv', 'xattn_w_o', 'ffn_w_up', 'ffn_w_down']):
        G[n] = jnp.stack([per_layer[l][i] for l in layers])

    hq = n_heads * HEAD_DIM
    small_g = {'attn_norm': d_attn_norm, 'attn_q_gain': d_gain[0, :hq].reshape(n_heads, HEAD_DIM).sum(0),
               'attn_k_gain': d_gain[0, hq:hq + N_KV_HEADS * HEAD_DIM].reshape(N_KV_HEADS, HEAD_DIM).sum(0),
               'pool_norm': d_pool_norm, 'pool_scale': d_pool_scale,
               'xattn_norm': jnp.concatenate(d_xn), 'mem_norm': jnp.concatenate(d_mn), 'ffn_norm': jnp.concatenate(d_fn),
               'ffn_conv_w': jnp.stack(d_cw, axis=1), 'ffn_conv_b': jnp.stack(d_cb)}
    order = list(small_g)
    flat = jnp.concatenate([loss_part] + [small_g[n].reshape(-1) for n in order] + [G['final_norm']])
    ar_rows = _round_up(-(-flat.size // PACK_W), 8)
    flat = jnp.pad(flat, (0, ar_rows * PACK_W - flat.size)).reshape(ar_rows, PACK_W)
    summed = _sum_slots("allreduce_sum", _allgather_small("allreduce_gather", flat)).reshape(-1)
    loss = summed[0]
    G['attn_w_qkv'] = reduce_finish("last", last, summed)[0][None]
    at = 1
    for n in order + ['final_norm']:
        size = G['final_norm'].size if n == 'final_norm' else small_g[n].size
        piece = summed[at:at + size]
        at += size
        if n in ('pool_norm', 'pool_scale'):
            piece = lax.dynamic_slice(piece, (dev * d_sh,), (d_sh,))
        elif n == 'ffn_conv_w':
            piece = lax.dynamic_index_in_dim(piece.reshape(N_DEV, n_layers, 3, n_up), dev, 0, keepdims=False)
        G[n] = piece.reshape(W[n].shape)

    deltas, new_m, new_v = [], [], []
    for n in names:
        d, nm, nv = _adamw(f"adamw_{n}", W[n], G[n], Mo[n], Vo[n])
        deltas.append(d)
        new_m.append(nm)
        new_v.append(nv)
    return (loss, grad_x[None], *[G[n] for n in names], *deltas, *new_m, *new_v)
```

```python
import jax
import jax.numpy as jnp
from jax import lax
from jax.experimental import pallas as pl
from jax.experimental.pallas import tpu as pltpu

F32 = jnp.float32
BF16 = jnp.bfloat16
MESH = pl.DeviceIdType.MESH

N_DEV = 8
EPS = 1e-6
HEAD_DIM = 64
N_KV_HEADS = 4
X_HEADS = 4
GRID_W = 64
ROPE_THETA = 10000.0
ROPE_PAIRS = HEAD_DIM // 4
POOL_WINDOWS = (2, 4, 8, 16)
POOL_PAD = 16
KEY_CHUNK = 1024
MM_ROWS = 1024
FFN_HALO = 16
LANES = 128
PACK_W = 1024
ADAM_LR, ADAM_B1, ADAM_B2, ADAM_EPS, ADAM_WD, ADAM_STEP = 0.001, 0.9, 0.999, 1e-08, 0.01, 10

_NN = (((1,), (0,)), ((), ()))
_NT = (((1,), (1,)), ((), ()))
_TN = (((0,), (0,)), ((), ()))


def _pc(body, *, name, **kw):
    return pl.pallas_call(body, name=name, **kw)


def _sem(*kinds):
    return pltpu.CompilerParams(dimension_semantics=kinds)


def _tile(n, pref, mult):
    best = None
    for t in range(mult, min(n, pref) + 1, mult):
        if n % t == 0:
            best = t
    return n if best is None else best


def _round_up(n, m):
    return (n + m - 1) // m * m


def _mm_call(name, a, b, dims, grid, a_spec, b_spec, o_spec, out_shape, kaxis, res=None, res_spec=None):
    nk = grid[kaxis]
    acc_shape = tuple(d for d in o_spec.block_shape if d is not None)
    in_place = out_shape.dtype == F32 and res is None
    use_scratch = nk > 1 and not in_place

    def body(*refs):
        refs = list(refs)
        acc = refs.pop() if use_scratch else None
        a_ref, b_ref = refs[:2]
        r_ref = refs[2] if res is not None else None
        o_ref = refs[-1]
        prod = lax.dot_general(a_ref[...].astype(BF16), b_ref[...].astype(BF16), dims, preferred_element_type=F32)
        if nk == 1:
            if r_ref is not None:
                prod = prod + r_ref[...]
            o_ref[...] = prod.astype(o_ref.dtype)
            return
        k = pl.program_id(kaxis)
        tgt = o_ref if in_place else acc

        @pl.when(k == 0)
        def _():
            tgt[...] = prod

        @pl.when(k > 0)
        def _():
            tgt[...] += prod

        if not in_place:
            @pl.when(k == nk - 1)
            def _():
                r = acc[...]
                if r_ref is not None:
                    r = r + r_ref[...]
                o_ref[...] = r.astype(o_ref.dtype)

    sem = tuple("arbitrary" if ax == kaxis else "parallel" for ax in range(len(grid)))
    ins = [a, b] if res is None else [a, b, res]
    specs = [a_spec, b_spec] if res is None else [a_spec, b_spec, res_spec]
    return _pc(body, name=name, grid=grid, in_specs=specs, out_specs=o_spec, out_shape=out_shape,
               scratch_shapes=[pltpu.VMEM(acc_shape, F32)] if use_scratch else [],
               compiler_params=_sem(*sem))(*ins)


def _mm_nn(name, a, b, out_dtype, res=None):
    M, K = a.shape
    N = b.shape[1]
    tm, tn, tk = _tile(M, MM_ROWS, 16), _tile(N, 1024, LANES), _tile(K, 1024, LANES)
    return _mm_call(name, a, b, _NN, (M // tm, N // tn, K // tk),
                    pl.BlockSpec((tm, tk), lambda i, j, k: (i, k)),
                    pl.BlockSpec((tk, tn), lambda i, j, k: (k, j)),
                    pl.BlockSpec((tm, tn), lambda i, j, k: (i, j)),
                    jax.ShapeDtypeStruct((M, N), out_dtype), 2, res,
                    pl.BlockSpec((tm, tn), lambda i, j, k: (i, j)))


def _mm_nt(name, a, b, out_dtype):
    M, K = a.shape
    N = b.shape[0]
    tm, tn, tk = _tile(M, MM_ROWS, 16), _tile(N, 1024, LANES), _tile(K, 1024, LANES)
    return _mm_call(name, a, b, _NT, (M // tm, N // tn, K // tk),
                    pl.BlockSpec((tm, tk), lambda i, j, k: (i, k)),
                    pl.BlockSpec((tn, tk), lambda i, j, k: (j, k)),
                    pl.BlockSpec((tm, tn), lambda i, j, k: (i, j)),
                    jax.ShapeDtypeStruct((M, N), out_dtype), 2)


def _mm_tn(name, a, b, out_dtype):
    R, M = a.shape
    N = b.shape[1]
    tm, tn, tr = _tile(M, 1024, LANES), _tile(N, 1024, LANES), _tile(R, MM_ROWS, 16)
    return _mm_call(name, a, b, _TN, (M // tm, N // tn, R // tr),
                    pl.BlockSpec((tr, tm), lambda i, j, k: (k, i)),
                    pl.BlockSpec((tr, tn), lambda i, j, k: (k, j)),
                    pl.BlockSpec((tm, tn), lambda i, j, k: (i, j)),
                    jax.ShapeDtypeStruct((M, N), out_dtype), 2)


def _mm_nn_bs(name, a, b, out_dtype, layer=0):
    M, K = a.shape
    J, _, n = b.shape
    tm, tk = _tile(M, MM_ROWS, 16), _tile(K, 1024, LANES)
    first = layer * (K // tk)
    return _mm_call(name, a, b, _NN, (J, M // tm, K // tk),
                    pl.BlockSpec((tm, tk), lambda j, i, k: (i, k)),
                    pl.BlockSpec((None, tk, n), lambda j, i, k: (j, first + k, 0)),
                    pl.BlockSpec((None, tm, n), lambda j, i, k: (j, i, 0)),
                    jax.ShapeDtypeStruct((J, M, n), out_dtype), 2)


def _mm_nn_as(name, a, b, out_dtype, res):
    J, M, n = a.shape
    N = b.shape[1]
    tm, tn = _tile(M, MM_ROWS, 16), _tile(N, 1024, LANES)
    return _mm_call(name, a, b, _NN, (M // tm, N // tn, J),
                    pl.BlockSpec((None, tm, n), lambda i, j, k: (k, i, 0)),
                    pl.BlockSpec((n, tn), lambda i, j, k: (k, j)),
                    pl.BlockSpec((tm, tn), lambda i, j, k: (i, j)),
                    jax.ShapeDtypeStruct((M, N), out_dtype), 2, res,
                    pl.BlockSpec((tm, tn), lambda i, j, k: (i, j)))


def _mm_tn_as(name, a, b, out_dtype):
    J, R, n = a.shape
    N = b.shape[1]
    tn, tr = _tile(N, 1024, LANES), _tile(R, MM_ROWS, 16)
    return _mm_call(name, a, b, _TN, (J, N // tn, R // tr),
                    pl.BlockSpec((None, tr, n), lambda j, jn, k: (j, k, 0)),
                    pl.BlockSpec((tr, tn), lambda j, jn, k: (k, jn)),
                    pl.BlockSpec((n, tn), lambda j, jn, k: (j, jn)),
                    jax.ShapeDtypeStruct((J * n, N), out_dtype), 2)


def _mm_nt_abs(name, a, b, N, out_dtype, layer=0):
    J, M, n = a.shape
    tm, tn = _tile(M, MM_ROWS, 16), _tile(N, 1024, LANES)
    first = layer * (N // tn)
    return _mm_call(name, a, b, _NT, (M // tm, N // tn, J),
                    pl.BlockSpec((None, tm, n), lambda i, j, k: (k, i, 0)),
                    pl.BlockSpec((None, tn, n), lambda i, j, k: (k, first + j, 0)),
                    pl.BlockSpec((tm, tn), lambda i, j, k: (i, j)),
                    jax.ShapeDtypeStruct((M, N), out_dtype), 2)


def _mm_tn_bs(name, a, b, out_dtype):
    R, M = a.shape
    J, _, n = b.shape
    tm, tr = _tile(M, 1024, LANES), _tile(R, MM_ROWS, 16)
    return _mm_call(name, a, b, _TN, (J, M // tm, R // tr),
                    pl.BlockSpec((tr, tm), lambda j, i, k: (k, i)),
                    pl.BlockSpec((None, tr, n), lambda j, i, k: (j, k, 0)),
                    pl.BlockSpec((None, tm, n), lambda j, i, k: (j, i, 0)),
                    jax.ShapeDtypeStruct((J, M, n), out_dtype), 2)


def _rmsnorm(name, x, g, out_dtype):
    R, D = x.shape
    tm = _tile(R, 512, 16)

    def body(x_ref, g_ref, o_ref):
        xv = x_ref[...]
        r = lax.rsqrt(jnp.mean(xv * xv, axis=-1, keepdims=True) + EPS)
        o_ref[...] = (xv * r * g_ref[...]).astype(o_ref.dtype)

    return _pc(body, name=name, grid=(R // tm,),
               in_specs=[pl.BlockSpec((tm, D), lambda i: (i, 0)), pl.BlockSpec((1, D), lambda i: (0, 0))],
               out_specs=pl.BlockSpec((tm, D), lambda i: (i, 0)),
               out_shape=jax.ShapeDtypeStruct((R, D), out_dtype), compiler_params=_sem("parallel"))(x, g)


def _rmsnorm_bwd(name, x, g, dh, dres=None):
    R, D = x.shape
    tm = _tile(R, 512, 16)

    def body(*refs):
        if dres is None:
            x_ref, g_ref, dh_ref, dx_ref, dg_ref = refs
            dres_ref = None
        else:
            x_ref, g_ref, dh_ref, dres_ref, dx_ref, dg_ref = refs
        xv = x_ref[...]
        r = lax.rsqrt(jnp.mean(xv * xv, axis=-1, keepdims=True) + EPS)
        xh = xv * r
        dhv = dh_ref[...].astype(F32)

        @pl.when(pl.program_id(0) == 0)
        def _():
            dg_ref[...] = jnp.zeros_like(dg_ref)

        dg_ref[...] += jnp.sum(dhv * xh, axis=0, keepdims=True)
        dxh = dhv * g_ref[...]
        dx = r * (dxh - xh * jnp.mean(dxh * xh, axis=-1, keepdims=True))
        if dres_ref is not None:
            dx = dx + dres_ref[...]
        dx_ref[...] = dx

    row = pl.BlockSpec((tm, D), lambda i: (i, 0))
    vec = pl.BlockSpec((1, D), lambda i: (0, 0))
    ins = [x, g, dh] + ([] if dres is None else [dres])
    specs = [row, vec, row] + ([] if dres is None else [row])
    return _pc(body, name=name, grid=(R // tm,), in_specs=specs, out_specs=(row, vec),
               out_shape=(jax.ShapeDtypeStruct((R, D), F32), jax.ShapeDtypeStruct((1, D), F32)),
               compiler_params=_sem("arbitrary"))(*ins)


def _mm_norm_bwd(name, a, b, grid, a_spec, b_spec, tm, x, gain, dres, after=None):
    M, D = x.shape
    nk = grid[2]

    def body(*refs):
        a_ref, b_ref, x_ref, g_ref, r_ref = refs[:5]
        dx_ref, dg_ref = refs[-3 if nk > 1 else -2:][:2]
        acc = refs[-1] if nk > 1 else None
        i, k = pl.program_id(0), pl.program_id(2)
        prod = lax.dot_general(a_ref[...].astype(BF16), b_ref[...].astype(BF16), _NT, preferred_element_type=F32)

        @pl.when((i == 0) & (k == 0))
        def _():
            dg_ref[...] = jnp.zeros_like(dg_ref)

        def finish(dh):
            xv = x_ref[...]
            r = lax.rsqrt(jnp.mean(xv * xv, axis=-1, keepdims=True) + EPS)
            xh = xv * r
            dg_ref[...] += jnp.sum(dh * xh, axis=0, keepdims=True)
            dxh = dh * g_ref[...]
            dx_ref[...] = r * (dxh - xh * jnp.mean(dxh * xh, axis=-1, keepdims=True)) + r_ref[...]

        if nk == 1:
            finish(prod)
            return

        @pl.when(k == 0)
        def _():
            acc[...] = prod

        @pl.when(k > 0)
        def _():
            acc[...] += prod

        @pl.when(k == nk - 1)
        def _():
            finish(acc[...])

    row = pl.BlockSpec((tm, D), lambda i, j, k: (i, 0))
    vec = pl.BlockSpec((1, D), lambda i, j, k: (0, 0))
    ins = [a, b, x, gain, dres] + ([] if after is None else [after])
    specs = [a_spec, b_spec, row, vec, row] + ([] if after is None else [pl.BlockSpec(memory_space=pl.ANY)])
    return _pc(body, name=name, grid=grid, in_specs=specs, out_specs=(row, vec),
               out_shape=(jax.ShapeDtypeStruct((M, D), F32), jax.ShapeDtypeStruct((1, D), F32)),
               scratch_shapes=[pltpu.VMEM((tm, D), F32)] if nk > 1 else [],
               compiler_params=_sem("arbitrary", "arbitrary", "arbitrary"))(*ins)


def _mm_nt_norm_bwd(name, a, b, x, gain, dres, after=None):
    M, K = a.shape
    D = b.shape[0]
    tm, tk = _tile(M, 512, 16), _tile(K, 1024, LANES)
    return _mm_norm_bwd(name, a, b, (M // tm, 1, K // tk),
                        pl.BlockSpec((tm, tk), lambda i, j, k: (i, k)),
                        pl.BlockSpec((D, tk), lambda i, j, k: (0, k)), tm, x, gain, dres, after)


def _mm_nt_abs_norm_bwd(name, a, b, x, gain, dres, layer=0):
    J, M, n = a.shape
    D = x.shape[1]
    tm = _tile(M, 512, 16)
    return _mm_norm_bwd(name, a, b, (M // tm, 1, J),
                        pl.BlockSpec((None, tm, n), lambda i, j, k: (k, i, 0)),
                        pl.BlockSpec((None, D, n), lambda i, j, k: (k, layer, 0)), tm, x, gain, dres)


def _loss_head(name, x, g, tgt):
    R, D = x.shape
    tm = _tile(R, 512, 16)

    def body(x_ref, g_ref, t_ref, dx_ref, dg_ref, l_ref):
        xv = x_ref[...]
        r = lax.rsqrt(jnp.mean(xv * xv, axis=-1, keepdims=True) + EPS)
        xh = xv * r
        err = xh * g_ref[...] - t_ref[...]

        @pl.when(pl.program_id(0) == 0)
        def _():
            dg_ref[...] = jnp.zeros_like(dg_ref)
            l_ref[...] = jnp.zeros_like(l_ref)

        l_ref[...] += jnp.sum(err * err, axis=0, keepdims=True)
        dy = err * (1.0 / D)
        dg_ref[...] += jnp.sum(dy * xh, axis=0, keepdims=True)
        dxh = dy * g_ref[...]
        dx_ref[...] = r * (dxh - xh * jnp.mean(dxh * xh, axis=-1, keepdims=True))

    row = pl.BlockSpec((tm, D), lambda i: (i, 0))
    vec = pl.BlockSpec((1, D), lambda i: (0, 0))
    return _pc(body, name=name, grid=(R // tm,), in_specs=[row, vec, row], out_specs=(row, vec, vec),
               out_shape=(jax.ShapeDtypeStruct((R, D), F32), jax.ShapeDtypeStruct((1, D), F32),
                          jax.ShapeDtypeStruct((1, D), F32)),
               compiler_params=_sem("arbitrary"))(x, g, tgt)


def _rope_tables(S):
    n_rows = S // GRID_W
    row = jnp.repeat(jnp.arange(n_rows, dtype=F32), GRID_W)
    col = jnp.tile(jnp.arange(GRID_W, dtype=F32), n_rows)
    inv_freq = ROPE_THETA ** (-jnp.arange(ROPE_PAIRS, dtype=F32) / ROPE_PAIRS)
    ang = jnp.stack([row[:, None] * inv_freq, col[:, None] * inv_freq], axis=1)
    cos, sin = jnp.cos(ang), jnp.sin(ang)
    c = jnp.broadcast_to(cos[:, :, None, :], (S, 2, 2, ROPE_PAIRS)).reshape(S, HEAD_DIM)
    s = jnp.stack([-sin, sin], axis=2).reshape(S, HEAD_DIM)
    reps = LANES // HEAD_DIM
    return jnp.tile(c, (1, reps)), jnp.tile(s, (1, reps))


def _head_mean_matrix():
    h = jnp.arange(LANES) // HEAD_DIM
    m = jnp.where(h[:, None] == h[None, :], 1.0 / HEAD_DIM, 0.0).astype(BF16)
    return jnp.concatenate([m, m], axis=0)


def _head_mean(v, bd):
    hi = v.astype(BF16)
    lo = (v - hi.astype(F32)).astype(BF16)
    return jnp.dot(jnp.concatenate([hi, lo], axis=1), bd, preferred_element_type=F32)


def _swap_halves(y):
    lane = lax.broadcasted_iota(jnp.int32, y.shape, 1)
    return jnp.where(lane % 32 < 16, pltpu.roll(y, LANES - 16, 1), pltpu.roll(y, 16, 1))


def _qk_rope(name, qkv, gain, scale, cos, sin, bd, n_rot):
    S, W = qkv.shape
    tm = _tile(S, 2048, 16)
    per = LANES // HEAD_DIM

    def body(x_ref, g_ref, s_ref, c_ref, sn_ref, bd_ref, o_ref):
        j = pl.program_id(1)
        xv = x_ref[...]

        def put(v):
            for h in range(per):
                o_ref[h] = v[:, h * HEAD_DIM:(h + 1) * HEAD_DIM].astype(BF16)

        @pl.when(j < n_rot)
        def _():
            ms = _head_mean(xv * xv, bd_ref[...])
            y = xv * lax.rsqrt(ms + EPS) * g_ref[...] * s_ref[...]
            put(y * c_ref[...] + _swap_halves(y) * sn_ref[...])

        @pl.when(j >= n_rot)
        def _():
            put(xv)

    blk = pl.BlockSpec((tm, LANES), lambda i, j: (i, j))
    vec = pl.BlockSpec((1, LANES), lambda i, j: (0, j))
    tab = pl.BlockSpec((tm, LANES), lambda i, j: (i, 0))
    return _pc(body, name=name, grid=(S // tm, W // LANES),
               in_specs=[blk, vec, vec, tab, tab, pl.BlockSpec((2 * LANES, LANES), lambda i, j: (0, 0))],
               out_specs=pl.BlockSpec((per, tm, HEAD_DIM), lambda i, j: (j, i, 0)),
               out_shape=jax.ShapeDtypeStruct((W // HEAD_DIM, S, HEAD_DIM), BF16),
               compiler_params=_sem("parallel", "parallel"))(qkv, gain, scale, cos, sin, bd)


def _qk_rope_bwd(name, dq, dk, dv, qkv, gain, scale, cos, sin, bd):
    S, W = qkv.shape
    tm = _tile(S, 2048, 16)
    per = LANES // HEAD_DIM
    nq, nk, nv = dq.shape[0] // per, dk.shape[0] // per, dv.shape[0] // per
    n_rot = nq + nk

    def body(dq_ref, dk_ref, dv_ref, x_ref, g_ref, s_ref, c_ref, sn_ref, bd_ref, dx_ref, dg_ref):
        j, i = pl.program_id(0), pl.program_id(1)

        @pl.when(i == 0)
        def _():
            dg_ref[...] = jnp.zeros_like(dg_ref)

        def rotate_back(d_ref):
            dv = jnp.concatenate([d_ref[h] for h in range(per)], axis=1)
            xv = x_ref[...]
            ms = _head_mean(xv * xv, bd_ref[...])
            r = lax.rsqrt(ms + EPS)
            z = xv * r
            dy = (dv * c_ref[...] - _swap_halves(dv) * sn_ref[...]) * s_ref[...]
            dg_ref[...] += jnp.sum(dy * z, axis=0, keepdims=True)
            dz = dy * g_ref[...]
            mz = _head_mean(dz * z, bd_ref[...])
            dx_ref[...] = (r * (dz - z * mz)).astype(BF16)

        @pl.when(j < nq)
        def _():
            rotate_back(dq_ref)

        @pl.when((j >= nq) & (j < n_rot))
        def _():
            rotate_back(dk_ref)

        @pl.when(j >= n_rot)
        def _():
            dx_ref[...] = jnp.concatenate([dv_ref[h] for h in range(per)], axis=1).astype(BF16)

    def part(first, count):
        return pl.BlockSpec((per, tm, HEAD_DIM), lambda j, i: (jnp.clip(j - first, 0, count - 1), i, 0))

    blk = pl.BlockSpec((tm, LANES), lambda j, i: (i, j))
    vec = pl.BlockSpec((1, LANES), lambda j, i: (0, j))
    tab = pl.BlockSpec((tm, LANES), lambda j, i: (i, 0))
    return _pc(body, name=name, grid=(W // LANES, S // tm),
               in_specs=[part(0, nq), part(nq, nk), part(n_rot, nv), blk, vec, vec, tab, tab,
                         pl.BlockSpec((2 * LANES, LANES), lambda j, i: (0, 0))],
               out_specs=(blk, vec),
               out_shape=(jax.ShapeDtypeStruct((S, W), BF16), jax.ShapeDtypeStruct((1, W), F32)),
               compiler_params=_sem("parallel", "arbitrary"))(dq, dk, dv, qkv, gain, scale, cos, sin, bd)


def _softmax_rows(s):
    m = jnp.max(s, axis=-1, keepdims=True)
    p = jnp.exp(s - m)
    return p, jnp.sum(p, axis=-1, keepdims=True)


def _attn_fwd(name, qkv, H):
    _, S, dh = qkv.shape
    G = H // N_KV_HEADS
    tq = _tile(S, 256, 16)
    kc = _tile(S, KEY_CHUNK, LANES)
    R = G * tq

    def body(q_ref, k_ref, v_ref, o_ref, lse_ref):
        q = q_ref[...].reshape(R, dh)
        m = jnp.full((R, 1), -1e30, F32)
        l = jnp.zeros((R, 1), F32)
        acc = jnp.zeros((R, dh), F32)
        for c in range(S // kc):
            rows = slice(c * kc, (c + 1) * kc)
            s = lax.dot_general(q, k_ref[rows, :], _NT, preferred_element_type=F32)
            m_new = jnp.maximum(m, jnp.max(s, axis=-1, keepdims=True))
            alpha = jnp.exp(m - m_new)
            p = jnp.exp(s - m_new)
            l = alpha * l + jnp.sum(p, axis=-1, keepdims=True)
            acc = alpha * acc + jnp.dot(p.astype(BF16), v_ref[rows, :], preferred_element_type=F32)
            m = m_new
        o_ref[...] = (acc / l).astype(BF16).reshape(G, tq, dh)
        lse_ref[...] = (m + jnp.log(l)).reshape(G, tq, 1)

    qs = pl.BlockSpec((G, tq, dh), lambda kv, i: (kv, i, 0))
    ls = pl.BlockSpec((G, tq, 1), lambda kv, i: (kv, i, 0))
    ks = pl.BlockSpec((None, S, dh), lambda kv, i: (H + kv, 0, 0))
    vs = pl.BlockSpec((None, S, dh), lambda kv, i: (H + N_KV_HEADS + kv, 0, 0))
    return _pc(body, name=name, grid=(N_KV_HEADS, S // tq), in_specs=[qs, ks, vs], out_specs=(qs, ls),
               out_shape=(jax.ShapeDtypeStruct((H, S, dh), BF16), jax.ShapeDtypeStruct((H, S, 1), F32)),
               compiler_params=_sem("parallel", "parallel"))(qkv, qkv, qkv)


def _attn_bwd(name, qkv, o, lse, do):
    H, S, dh = o.shape
    G = H // N_KV_HEADS
    tq = _tile(S, 128, 16)
    kc = _tile(S, KEY_CHUNK, LANES)
    R = G * tq

    def body(q_ref, k_ref, v_ref, o_ref, lse_ref, do_ref, dq_ref, dk_ref, dv_ref):
        @pl.when(pl.program_id(1) == 0)
        def _():
            dk_ref[...] = jnp.zeros_like(dk_ref)
            dv_ref[...] = jnp.zeros_like(dv_ref)

        qq, dd = q_ref[...].reshape(R, dh), do_ref[...].reshape(R, dh)
        delta = jnp.sum(dd.astype(F32) * o_ref[...].reshape(R, dh).astype(F32), axis=-1, keepdims=True)
        lse = lse_ref[...].reshape(R, 1)
        dq = jnp.zeros((R, dh), F32)
        for c in range(S // kc):
            rows = slice(c * kc, (c + 1) * kc)
            kk, vv = k_ref[rows, :], v_ref[rows, :]
            p = jnp.exp(lax.dot_general(qq, kk, _NT, preferred_element_type=F32) - lse)
            dv_ref[rows, :] += lax.dot_general(p.astype(BF16), dd, _TN, preferred_element_type=F32)
            dp = lax.dot_general(dd, vv, _NT, preferred_element_type=F32)
            ds = (p * (dp - delta)).astype(BF16)
            dq = dq + jnp.dot(ds, kk, preferred_element_type=F32)
            dk_ref[rows, :] += lax.dot_general(ds, qq, _TN, preferred_element_type=F32)
        dq_ref[...] = dq.reshape(G, tq, dh)

    qs = pl.BlockSpec((G, tq, dh), lambda kv, i: (kv, i, 0))
    ls = pl.BlockSpec((G, tq, 1), lambda kv, i: (kv, i, 0))
    ks = pl.BlockSpec((None, S, dh), lambda kv, i: (H + kv, 0, 0))
    vs = pl.BlockSpec((None, S, dh), lambda kv, i: (H + N_KV_HEADS + kv, 0, 0))
    acc = pl.BlockSpec((None, S, dh), lambda kv, i: (kv, 0, 0))
    return _pc(body, name=name, grid=(N_KV_HEADS, S // tq), in_specs=[qs, ks, vs, qs, ls, qs],
               out_specs=(qs, acc, acc),
               out_shape=(jax.ShapeDtypeStruct((H, S, dh), F32), jax.ShapeDtypeStruct((N_KV_HEADS, S, dh), F32),
                          jax.ShapeDtypeStruct((N_KV_HEADS, S, dh), F32)),
               compiler_params=_sem("parallel", "arbitrary"))(qkv, qkv, qkv, o, lse, do)


def _xattn_fwd(name, q, kv):
    S, D = q.shape
    _, M, dh = kv.shape
    scale = dh ** -0.5
    tq = _tile(S, 256, 16)

    def body(q_ref, kv_ref, o_ref):
        for h in range(X_HEADS):
            lo, hi = h * dh, (h + 1) * dh
            s = lax.dot_general(q_ref[:, lo:hi], kv_ref[h], _NT, preferred_element_type=F32) * scale
            p, l = _softmax_rows(s)
            o = jnp.dot(p.astype(BF16), kv_ref[X_HEADS + h], preferred_element_type=F32)
            o_ref[:, lo:hi] = (o / l).astype(BF16)

    row = pl.BlockSpec((tq, D), lambda i: (i, 0))
    return _pc(body, name=name, grid=(S // tq,),
               in_specs=[row, pl.BlockSpec((2 * X_HEADS, M, dh), lambda i: (0, 0, 0))],
               out_specs=row, out_shape=jax.ShapeDtypeStruct((S, D), BF16),
               compiler_params=_sem("parallel"))(q, kv)


def _xattn_bwd(name, q, kv, do):
    S, D = q.shape
    _, M, dh = kv.shape
    scale = dh ** -0.5
    tq = _tile(S, 256, 16)

    def body(q_ref, kv_ref, do_ref, dq_ref, dkv_ref):
        @pl.when(pl.program_id(0) == 0)
        def _():
            dkv_ref[...] = jnp.zeros_like(dkv_ref)

        for h in range(X_HEADS):
            lo, hi = h * dh, (h + 1) * dh
            qh, kh, vh, doh = q_ref[:, lo:hi], kv_ref[h], kv_ref[X_HEADS + h], do_ref[:, lo:hi]
            s = lax.dot_general(qh, kh, _NT, preferred_element_type=F32) * scale
            p, l = _softmax_rows(s)
            pn = p / l
            dkv_ref[X_HEADS + h] += lax.dot_general(pn.astype(BF16), doh, _TN, preferred_element_type=F32)
            dp = lax.dot_general(doh, vh, _NT, preferred_element_type=F32)
            ds = (pn * (dp - jnp.sum(pn * dp, axis=-1, keepdims=True)) * scale).astype(BF16)
            dq_ref[:, lo:hi] = jnp.dot(ds, kh, preferred_element_type=F32).astype(BF16)
            dkv_ref[h] += lax.dot_general(ds, qh, _TN, preferred_element_type=F32)

    row = pl.BlockSpec((tq, D), lambda i: (i, 0))
    full = pl.BlockSpec((2 * X_HEADS, M, dh), lambda i: (0, 0, 0))
    return _pc(body, name=name, grid=(S // tq,), in_specs=[row, full, row], out_specs=(row, full),
               out_shape=(jax.ShapeDtypeStruct((S, D), BF16), jax.ShapeDtypeStruct((2 * X_HEADS, M, dh), F32)),
               compiler_params=_sem("arbitrary"))(q, kv, do)


def _sigmoid(x):
    return 1.0 / (1.0 + jnp.exp(-x))


def _halo_specs(tm, n, S):
    nb = tm // 8
    last8 = S // 8 - 1
    main = pl.BlockSpec((2, None, tm, n), lambda j, i: (0, j, i, 0))
    prev = pl.BlockSpec((2, None, 8, n), lambda j, i: (0, j, jnp.maximum(i * nb - 1, 0), 0))
    nxt = pl.BlockSpec((2, None, 8, n), lambda j, i: (0, j, jnp.minimum((i + 1) * nb, last8), 0))
    return main, prev, nxt


def _ffn_up_act(name, h, w, cw, cb, layer):
    S, K = h.shape
    _, J, _, n = w.shape
    tm = _tile(S, 512, FFN_HALO)
    nblk = S // tm
    hb, last = tm // FFN_HALO, S // FFN_HALO - 1
    te = tm + 2 * FFN_HALO

    def body(h_ref, hp_ref, hn_ref, w_ref, cw_ref, b_ref, u_ref, a_ref):
        i = pl.program_id(1)
        zero = jnp.zeros((FFN_HALO, K), BF16)
        he = jnp.concatenate([jnp.where(i == 0, zero, hp_ref[...]), h_ref[...],
                              jnp.where(i == nblk - 1, zero, hn_ref[...])], axis=0)
        mid = slice(FFN_HALO, tm + FFN_HALO)
        c = []
        for half in range(2):
            ue = jnp.dot(he, w_ref[half], preferred_element_type=F32)
            um = ue[mid]
            u_ref[half] = um
            k = cw_ref[half]
            c.append(pltpu.roll(ue, 1, 0)[mid] * k[0:1] + um * k[1:2] + pltpu.roll(ue, te - 1, 0)[mid] * k[2:3]
                     + b_ref[half])
        a_ref[...] = (c[0] * _sigmoid(c[0]) * c[1]).astype(BF16)

    return _pc(body, name=name, grid=(J, nblk),
               in_specs=[pl.BlockSpec((tm, K), lambda j, i: (i, 0)),
                         pl.BlockSpec((FFN_HALO, K), lambda j, i: (jnp.maximum(i * hb - 1, 0), 0)),
                         pl.BlockSpec((FFN_HALO, K), lambda j, i: (jnp.minimum((i + 1) * hb, last), 0)),
                         pl.BlockSpec((2, None, K, n), lambda j, i: (0, j, layer, 0)),
                         pl.BlockSpec((2, None, 3, n), lambda j, i: (0, j, 0, 0)),
                         pl.BlockSpec((2, None, 1, n), lambda j, i: (0, j, 0, 0))],
               out_specs=(pl.BlockSpec((2, None, tm, n), lambda j, i: (0, j, i, 0)),
                          pl.BlockSpec((None, tm, n), lambda j, i: (j, i, 0))),
               out_shape=(jax.ShapeDtypeStruct((2, J, S, n), F32), jax.ShapeDtypeStruct((J, S, n), BF16)),
               compiler_params=_sem("parallel", "parallel"))(h, h, h, w, cw, cb)


def _ffn_act_bwd(name, u, g, w_down, cw, cb):
    _, J, S, n = u.shape
    D = g.shape[1]
    tm = _tile(S, 256, 16)
    nblk = S // tm
    te = tm + 16
    nb = tm // 8
    last8 = S // 8 - 1

    def body(u_ref, up_ref, un_ref, g_ref, gp_ref, gn_ref, wd_ref, w_ref, b_ref, du_ref, st_ref):
        i = pl.program_id(1)

        @pl.when(i == 0)
        def _():
            st_ref[...] = jnp.zeros_like(st_ref)

        def extended(before, main, after):
            return jnp.concatenate([jnp.where(i == 0, 0.0, before), main, jnp.where(i == nblk - 1, 0.0, after)], axis=0)

        mid = slice(8, tm + 8)
        da_e = lax.dot_general(extended(gp_ref[...], g_ref[...], gn_ref[...]).astype(BF16), wd_ref[...], _NT,
                               preferred_element_type=F32)
        ue, c = [], []
        for half in range(2):
            e = extended(up_ref[half], u_ref[half], un_ref[half])
            w = w_ref[half]
            ue.append((pltpu.roll(e, 1, 0), e, pltpu.roll(e, te - 1, 0)))
            c.append(ue[half][0] * w[0:1] + e * w[1:2] + ue[half][2] * w[2:3] + b_ref[half])
        sg = _sigmoid(c[0])
        dc = [da_e * c[1] * (sg * (1.0 + c[0] * (1.0 - sg))), da_e * (c[0] * sg)]
        r8 = lax.broadcasted_iota(jnp.int32, (8, n), 0)
        for half in range(2):
            w, d, (e_before, e, e_after) = w_ref[half], dc[half], ue[half]
            dm = d[mid]
            du = pltpu.roll(d, te - 1, 0)[mid] * w[0:1] + dm * w[1:2] + pltpu.roll(d, 1, 0)[mid] * w[2:3]
            du_ref[half] = du.astype(BF16)
            s0 = jnp.sum(dm * e_before[mid], axis=0, keepdims=True)
            s1 = jnp.sum(dm * e[mid], axis=0, keepdims=True)
            s2 = jnp.sum(dm * e_after[mid], axis=0, keepdims=True)
            s3 = jnp.sum(dm, axis=0, keepdims=True)
            st_ref[half] += jnp.where(r8 == 0, s0, jnp.where(r8 == 1, s1, jnp.where(r8 == 2, s2,
                                      jnp.where(r8 == 3, s3, 0.0))))

    main, prev, nxt = _halo_specs(tm, n, S)
    gmain = pl.BlockSpec((tm, D), lambda j, i: (i, 0))
    gprev = pl.BlockSpec((8, D), lambda j, i: (jnp.maximum(i * nb - 1, 0), 0))
    gnxt = pl.BlockSpec((8, D), lambda j, i: (jnp.minimum((i + 1) * nb, last8), 0))
    return _pc(body, name=name, grid=(J, nblk),
               in_specs=[main, prev, nxt, gmain, gprev, gnxt, pl.BlockSpec((n, D), lambda j, i: (j, 0)),
                         pl.BlockSpec((2, None, 3, n), lambda j, i: (0, j, 0, 0)),
                         pl.BlockSpec((2, None, 1, n), lambda j, i: (0, j, 0, 0))],
               out_specs=(main, pl.BlockSpec((2, None, 8, n), lambda j, i: (0, j, 0, 0))),
               out_shape=(jax.ShapeDtypeStruct((2, J, S, n), BF16), jax.ShapeDtypeStruct((2, J, 8, n), F32)),
               compiler_params=_sem("parallel", "arbitrary"))(u, u, u, g, g, g, w_down, cw, cb)


def _window_count(t, w, S):
    lo = jnp.maximum(t - w // 2, 0)
    hi = jnp.minimum(t + w - w // 2, S)
    return (hi - lo).astype(F32)


def _trailing_sums(x, w):
    k = 1
    while k < w:
        x = x + pltpu.roll(x, k, 0)
        k *= 2
    return x


def _pool_window(name, h, group_w, adjoint, out_dtype):
    S, D = h.shape
    SP = S + 2 * POOL_PAD
    per_group = group_w // LANES

    def body(h_ref, o_ref, xp):
        g = pl.program_id(0) // per_group
        t = lax.broadcasted_iota(jnp.int32, (S, LANES), 0)
        xp[0:POOL_PAD, :] = jnp.zeros((POOL_PAD, LANES), F32)
        xp[S + POOL_PAD:SP, :] = jnp.zeros((POOL_PAD, LANES), F32)
        for gi, w in enumerate(POOL_WINDOWS):
            @pl.when(g == gi)
            def _():
                hv = h_ref[...]
                cnt = _window_count(t, w, S)
                xp[POOL_PAD:S + POOL_PAD, :] = hv / cnt if adjoint else hv
                ahead = w // 2 if adjoint else w // 2 - 1
                sw = _trailing_sums(xp[...], w)
                if ahead:
                    sw = pltpu.roll(sw, SP - ahead, 0)
                win = sw[POOL_PAD:S + POOL_PAD]
                o_ref[...] = ((win if adjoint else win / cnt) - hv).astype(out_dtype)

    col = pl.BlockSpec((S, LANES), lambda j: (0, j))
    return _pc(body, name=name, grid=(D // LANES,), in_specs=[col], out_specs=col,
               out_shape=jax.ShapeDtypeStruct((S, D), out_dtype),
               scratch_shapes=[pltpu.VMEM((SP, LANES), F32)], compiler_params=_sem("parallel"))(h)


def _pool_proj(name, mixed, w, scale, res):
    S, D = mixed.shape
    G, gw, _ = w.shape
    tm = _tile(S, 512, 16)

    def body(m_ref, w_ref, s_ref, r_ref, o_ref):
        for g in range(G):
            lo, hi = g * gw, (g + 1) * gw
            y = jnp.dot(m_ref[:, lo:hi], w_ref[g], preferred_element_type=F32)
            o_ref[:, lo:hi] = r_ref[:, lo:hi] + y * s_ref[:, lo:hi]

    row = pl.BlockSpec((tm, D), lambda i: (i, 0))
    return _pc(body, name=name, grid=(S // tm,),
               in_specs=[row, pl.BlockSpec((G, gw, gw), lambda i: (0, 0, 0)), pl.BlockSpec((1, D), lambda i: (0, 0)), row],
               out_specs=row, out_shape=jax.ShapeDtypeStruct((S, D), F32),
               compiler_params=_sem("parallel"))(mixed, w, scale, res)


def _pool_proj_bwd(name, dy, mixed, w, scale):
    S, D = mixed.shape
    G, gw, _ = w.shape
    tm = _tile(S, 512, 16)

    def body(dy_ref, m_ref, w_ref, s_ref, dm_ref, dw_ref, ds_ref):
        @pl.when(pl.program_id(0) == 0)
        def _():
            dw_ref[...] = jnp.zeros_like(dw_ref)
            ds_ref[...] = jnp.zeros_like(ds_ref)

        for g in range(G):
            lo, hi = g * gw, (g + 1) * gw
            mg, dyg = m_ref[:, lo:hi], dy_ref[:, lo:hi]
            y = jnp.dot(mg, w_ref[g], preferred_element_type=F32)
            ds_ref[:, lo:hi] += jnp.sum(dyg * y, axis=0, keepdims=True)
            dyp = (dyg * s_ref[:, lo:hi]).astype(BF16)
            dm_ref[:, lo:hi] = lax.dot_general(dyp, w_ref[g], _NT, preferred_element_type=F32)
            dw_ref[g] += lax.dot_general(mg, dyp, _TN, preferred_element_type=F32)

    row = pl.BlockSpec((tm, D), lambda i: (i, 0))
    wsp = pl.BlockSpec((G, gw, gw), lambda i: (0, 0, 0))
    vec = pl.BlockSpec((1, D), lambda i: (0, 0))
    return _pc(body, name=name, grid=(S // tm,), in_specs=[row, row, wsp, vec], out_specs=(row, wsp, vec),
               out_shape=(jax.ShapeDtypeStruct((S, D), F32), jax.ShapeDtypeStruct((G, gw, gw), F32),
                          jax.ShapeDtypeStruct((1, D), F32)),
               compiler_params=_sem("arbitrary"))(dy, mixed, w, scale)


def _adamw(name, w, g, m, v):
    shape = w.shape
    C = shape[-1]
    R = w.size // C
    tm = _tile(R, 512, 8)

    def body(w_ref, g_ref, m_ref, v_ref, d_ref, nm_ref, nv_ref):
        gv = g_ref[...]
        nm = ADAM_B1 * m_ref[...] + (1.0 - ADAM_B1) * gv
        nv = ADAM_B2 * v_ref[...] + (1.0 - ADAM_B2) * (gv * gv)
        m_hat = nm / (1.0 - ADAM_B1 ** ADAM_STEP)
        v_hat = nv / (1.0 - ADAM_B2 ** ADAM_STEP)
        d_ref[...] = -ADAM_LR * (m_hat / (jnp.sqrt(v_hat) + ADAM_EPS) + ADAM_WD * w_ref[...])
        nm_ref[...] = nm
        nv_ref[...] = nv

    blk = pl.BlockSpec((tm, C), lambda i: (i, 0))
    sd = jax.ShapeDtypeStruct((R, C), F32)
    outs = _pc(body, name=name, grid=(R // tm,), in_specs=[blk] * 4, out_specs=(blk,) * 3, out_shape=(sd,) * 3,
               compiler_params=_sem("parallel"))(*(a.reshape(R, C) for a in (w, g, m, v)))
    return tuple(o.reshape(shape) for o in outs)


def _position():
    return lax.axis_index("x"), lax.axis_index("y"), lax.axis_index("c")


def _flip(v, bit):
    return 1 - v if bit else v


def _allgather_small(name, v):
    R, W = v.shape

    def body(v_ref, out_ref, send_sems, recv_sems):
        x, y, c = _position()
        me = 4 * x + 2 * y + c
        out_ref[me] = v_ref[...]
        sends = []
        for k in range(1, N_DEV):
            peer = (_flip(x, k & 4), _flip(y, k & 2), _flip(c, k & 1))
            cp = pltpu.make_async_remote_copy(src_ref=v_ref, dst_ref=out_ref.at[me], send_sem=send_sems.at[k - 1],
                                              recv_sem=recv_sems.at[k - 1], device_id=peer, device_id_type=MESH)
            cp.start()
            sends.append(cp)
        for k in range(1, N_DEV):
            peer = (_flip(x, k & 4), _flip(y, k & 2), _flip(c, k & 1))
            slot = 4 * peer[0] + 2 * peer[1] + peer[2]
            pltpu.make_async_remote_copy(src_ref=v_ref, dst_ref=out_ref.at[slot], send_sem=send_sems.at[k - 1],
                                         recv_sem=recv_sems.at[k - 1], device_id=peer, device_id_type=MESH).wait_recv()
        for cp in sends:
            cp.wait_send()

    vm = pl.BlockSpec(memory_space=pltpu.VMEM)
    return _pc(body, name=name, in_specs=[vm], out_specs=vm, out_shape=jax.ShapeDtypeStruct((N_DEV, R, W), F32),
               scratch_shapes=[pltpu.SemaphoreType.DMA((N_DEV - 1,)), pltpu.SemaphoreType.DMA((N_DEV - 1,))])(v)


def _sum_slots(name, a):
    n, R, W = a.shape

    def body(a_ref, o_ref):
        acc = a_ref[0]
        for s in range(1, n):
            acc = acc + a_ref[s]
        o_ref[...] = acc

    return _pc(body, name=name, grid=(1,), in_specs=[pl.BlockSpec((n, R, W), lambda i: (0, 0, 0))],
               out_specs=pl.BlockSpec((R, W), lambda i: (0, 0)), out_shape=jax.ShapeDtypeStruct((R, W), F32))(a)


def _allgather_blocks(name, blocks):
    n = len(blocks)

    def body(*refs):
        b_refs, out_refs, token = refs[:n], refs[n:2 * n], refs[2 * n]
        send_sems, recv_sems, local_sems = refs[2 * n + 1:]
        token[...] = jnp.zeros_like(token)
        x, y, c = _position()
        me, sibling = (x, y, c), (x, y, 1 - c)
        chips = [(1 - x, y), (x, 1 - y), (1 - x, 1 - y)]

        def slot(i, px, py, pc):
            return out_refs[i].at[4 * px + 2 * py + pc]

        def copy(i, k, block, to, src=None):
            return pltpu.make_async_remote_copy(src_ref=slot(i, *block) if src is None else src, dst_ref=slot(i, *block),
                                                send_sem=send_sems.at[k, i], recv_sem=recv_sems.at[k, i],
                                                device_id=to, device_id_type=MESH)

        mine = [pltpu.make_async_copy(b_refs[i], slot(i, *me), local_sems.at[i]) for i in range(n)]
        first = [copy(i, 1 + j, me, (*chip, c), src=b_refs[i]) for i in range(n) for j, chip in enumerate(chips)]
        first += [copy(i, 0, me, sibling, src=b_refs[i]) for i in range(n)]
        for cp in mine + first:
            cp.start()
        passed = []
        for j, chip in enumerate(chips):
            for i in range(n):
                copy(i, 1 + j, (*chip, c), me).wait_recv()
                passed.append(copy(i, 4 + j, (*chip, c), sibling))
                passed[-1].start()
        for i in range(n):
            copy(i, 0, sibling, me).wait_recv()
        for j, chip in enumerate(chips):
            for i in range(n):
                copy(i, 4 + j, (*chip, 1 - c), me).wait_recv()
        for cp in first + passed:
            cp.wait_send()
        for cp in mine:
            cp.wait()

    hbm = pl.BlockSpec(memory_space=pl.ANY)
    return _pc(body, name=name, in_specs=[hbm] * n, out_specs=[hbm] * n + [pl.BlockSpec(memory_space=pltpu.VMEM)],
               out_shape=[jax.ShapeDtypeStruct((N_DEV,) + b.shape, b.dtype) for b in blocks]
               + [jax.ShapeDtypeStruct((8, LANES), F32)],
               scratch_shapes=[pltpu.SemaphoreType.DMA((7, n)), pltpu.SemaphoreType.DMA((7, n)),
                               pltpu.SemaphoreType.DMA((n,))])(*blocks)


def _add_sibling(name, g4, r1, pos):
    n, _, L, W = g4.shape
    tl = _tile(L, 512, 16)

    def body(pos_ref, g_ref, r_ref, tb_ref, own_ref):
        t = g_ref[...] + r_ref[...]
        tb_ref[...] = t.astype(BF16)

        @pl.when(pl.program_id(1) == pos_ref[1])
        def _():
            own_ref[...] = t

    gs = pltpu.PrefetchScalarGridSpec(
        num_scalar_prefetch=1, grid=(L // tl, n),
        in_specs=[pl.BlockSpec((None, None, tl, W), lambda i, k, p: (k, p[0], i, 0)),
                  pl.BlockSpec((None, tl, W), lambda i, k, p: (k, i, 0))],
        out_specs=(pl.BlockSpec((None, tl, W), lambda i, k, p: (k, i, 0)),
                   pl.BlockSpec((tl, W), lambda i, k, p: (i, 0))))
    return _pc(body, name=name, grid_spec=gs,
               out_shape=(jax.ShapeDtypeStruct((n, L, W), BF16), jax.ShapeDtypeStruct((L, W), F32)),
               compiler_params=_sem("parallel", "arbitrary"))(pos, g4, r1)


def _add_chips(name, own, r2):
    L, W = own.shape
    tl = _tile(L, 512, 16)

    def body(o_ref, r_ref, out_ref):
        acc = o_ref[...]
        for j in range(3):
            acc = acc + r_ref[j].astype(F32)
        out_ref[...] = acc

    return _pc(body, name=name, grid=(L // tl,),
               in_specs=[pl.BlockSpec((tl, W), lambda i: (i, 0)), pl.BlockSpec((3, tl, W), lambda i: (0, i, 0))],
               out_specs=pl.BlockSpec((tl, W), lambda i: (i, 0)), out_shape=jax.ShapeDtypeStruct((L, W), F32),
               compiler_params=_sem("parallel"))(own, r2)


_HBM = pl.BlockSpec(memory_space=pltpu.HBM)
_SEM = pl.BlockSpec(memory_space=pltpu.SEMAPHORE)
_EFFECT = pltpu.SideEffectType.DATAFLOW_SIDE_EFFECTING


def _in_hbm(a):
    return pltpu.with_memory_space_constraint(a, pltpu.HBM)


def _after(x, token):
    return x + token[0, 0].astype(x.dtype)


def _copies_start(name, bufs, sem_shape, plan):
    nb = len(bufs)

    def body(*refs):
        for cp in plan(refs[:nb], refs[nb], refs[nb + 1]):
            cp.start()
        refs[-1][...] = jnp.zeros_like(refs[-1])

    out = _pc(body, name=name, in_specs=[_HBM] * nb,
              out_specs=(_SEM, _SEM, *[_HBM] * nb, pl.BlockSpec(memory_space=pltpu.VMEM)),
              out_shape=(pltpu.SemaphoreType.DMA(sem_shape), pltpu.SemaphoreType.DMA(sem_shape),
                         *[pltpu.HBM(b.shape, b.dtype) for b in bufs], jax.ShapeDtypeStruct((8, LANES), F32)),
              input_output_aliases={i: 2 + i for i in range(nb)},
              compiler_params=pltpu.CompilerParams(has_side_effects=_EFFECT))(*[_in_hbm(b) for b in bufs])
    return out[0], out[1], list(out[2:2 + nb]), out[-1]


def _copies_wait(name, bufs, send_sems, recv_sems, plan, after):
    nb = len(bufs)

    def body(*refs):
        for cp in plan(refs[:nb], refs[nb], refs[nb + 1]):
            cp.wait_send()
            cp.wait_recv()

    return list(_pc(body, name=name, in_specs=[_HBM] * nb + [_SEM, _SEM, pl.BlockSpec(memory_space=pl.ANY)],
                    out_specs=[_HBM] * nb, out_shape=[pltpu.HBM(b.shape, b.dtype) for b in bufs],
                    input_output_aliases={i: i for i in range(nb)},
                    compiler_params=pltpu.CompilerParams(has_side_effects=_EFFECT))(*bufs, send_sems, recv_sems, after))


def _plan_gather_chips(n):
    def plan(refs, send_sems, recv_sems):
        x, y, c = _position()
        peers = [(x, y, 1 - c), (1 - x, y, c), (x, 1 - y, c), (1 - x, 1 - y, c)]
        return [pltpu.make_async_remote_copy(src_ref=refs[i], dst_ref=refs[n + i].at[4 * x + 2 * y + c],
                                             send_sem=send_sems.at[k * n + i], recv_sem=recv_sems.at[k * n + i],
                                             device_id=peer, device_id_type=MESH)
                for i in range(n) for k, peer in enumerate(peers)]
    return plan


def _plan_gather_sibling(n):
    def plan(refs, send_sems, recv_sems):
        x, y, c = _position()
        slots = [4 * (1 - x) + 2 * y + c, 4 * x + 2 * (1 - y) + c, 4 * (1 - x) + 2 * (1 - y) + c]
        return [pltpu.make_async_remote_copy(src_ref=refs[i].at[s], dst_ref=refs[i].at[s],
                                             send_sem=send_sems.at[k * n + i], recv_sem=recv_sems.at[k * n + i],
                                             device_id=(x, y, 1 - c), device_id_type=MESH)
                for i in range(n) for k, s in enumerate(slots)]
    return plan


def _plan_reduce_sibling(n):
    def plan(refs, send_sems, recv_sems):
        x, y, c = _position()
        return [pltpu.make_async_remote_copy(src_ref=refs[i].at[k, 1 - c], dst_ref=refs[n + i].at[k],
                                             send_sem=send_sems.at[k * n + i], recv_sem=recv_sems.at[k * n + i],
                                             device_id=(x, y, 1 - c), device_id_type=MESH)
                for i in range(n) for k in range(N_DEV // 2)]
    return plan


def _plan_reduce_chips(n):
    def plan(refs, send_sems, recv_sems):
        x, y, c = _position()
        cps = []
        for i in range(n):
            for j in range(1, 4):
                px, py = _flip(x, j & 2), _flip(y, j & 1)
                sem = (j - 1) * n + i
                cps.append(pltpu.make_async_remote_copy(src_ref=refs[i].at[2 * px + py], dst_ref=refs[n + i].at[j - 1],
                                                        send_sem=send_sems.at[sem], recv_sem=recv_sems.at[sem],
                                                        device_id=(px, py, c), device_id_type=MESH))
        return cps
    return plan


def _heads_major(name, a, after):
    S, W = a.shape
    H = W // HEAD_DIM
    tm = _tile(S, 512, 16)

    def body(a_ref, after_ref, o_ref):
        v = a_ref[...]
        for h in range(H):
            o_ref[h] = v[:, h * HEAD_DIM:(h + 1) * HEAD_DIM]

    return _pc(body, name=name, grid=(S // tm,),
               in_specs=[pl.BlockSpec((tm, W), lambda i: (i, 0)), pl.BlockSpec(memory_space=pl.ANY)],
               out_specs=pl.BlockSpec((H, tm, HEAD_DIM), lambda i: (0, i, 0)),
               out_shape=jax.ShapeDtypeStruct((H, S, HEAD_DIM), a.dtype), compiler_params=_sem("parallel"))(a, after)


def _heads_minor(name, a):
    H, S, _ = a.shape
    tm = _tile(S, 512, 16)

    def body(a_ref, o_ref):
        o_ref[...] = jnp.concatenate([a_ref[h] for h in range(H)], axis=1)

    return _pc(body, name=name, grid=(S // tm,), in_specs=[pl.BlockSpec((H, tm, HEAD_DIM), lambda i: (0, i, 0))],
               out_specs=pl.BlockSpec((tm, H * HEAD_DIM), lambda i: (i, 0)),
               out_shape=jax.ShapeDtypeStruct((S, H * HEAD_DIM), a.dtype), compiler_params=_sem("parallel"))(a)


def kernel(x, mem, attn_norm, attn_w_qkv, attn_q_gain, attn_k_gain, attn_w_o, pool_norm, pool_w, pool_scale, xattn_norm, mem_norm, xattn_w_q, xattn_w_kv, xattn_w_o, ffn_norm, ffn_w_up, ffn_conv_w, ffn_conv_b, ffn_w_down, final_norm, loss_target, m_attn_norm, m_attn_w_qkv, m_attn_q_gain, m_attn_k_gain, m_attn_w_o, m_pool_norm, m_pool_w, m_pool_scale, m_xattn_norm, m_mem_norm, m_xattn_w_q, m_xattn_w_kv, m_xattn_w_o, m_ffn_norm, m_ffn_w_up, m_ffn_conv_w, m_ffn_conv_b, m_ffn_w_down, m_final_norm, v_attn_norm, v_attn_w_qkv, v_attn_q_gain, v_attn_k_gain, v_attn_w_o, v_pool_norm, v_pool_w, v_pool_scale, v_xattn_norm, v_mem_norm, v_xattn_w_q, v_xattn_w_kv, v_xattn_w_o, v_ffn_norm, v_ffn_w_up, v_ffn_conv_w, v_ffn_conv_b, v_ffn_w_down, v_final_norm):
    names = ['attn_norm', 'attn_w_qkv', 'attn_q_gain', 'attn_k_gain', 'attn_w_o', 'pool_norm', 'pool_w', 'pool_scale',
             'xattn_norm', 'mem_norm', 'xattn_w_q', 'xattn_w_kv', 'xattn_w_o', 'ffn_norm', 'ffn_w_up', 'ffn_conv_w',
             'ffn_conv_b', 'ffn_w_down', 'final_norm']
    W = dict(zip(names, (attn_norm, attn_w_qkv, attn_q_gain, attn_k_gain, attn_w_o, pool_norm, pool_w, pool_scale,
                         xattn_norm, mem_norm, xattn_w_q, xattn_w_kv, xattn_w_o, ffn_norm, ffn_w_up, ffn_conv_w,
                         ffn_conv_b, ffn_w_down, final_norm)))
    Mo = dict(zip(names, (m_attn_norm, m_attn_w_qkv, m_attn_q_gain, m_attn_k_gain, m_attn_w_o, m_pool_norm, m_pool_w,
                          m_pool_scale, m_xattn_norm, m_mem_norm, m_xattn_w_q, m_xattn_w_kv, m_xattn_w_o, m_ffn_norm,
                          m_ffn_w_up, m_ffn_conv_w, m_ffn_conv_b, m_ffn_w_down, m_final_norm)))
    Vo = dict(zip(names, (v_attn_norm, v_attn_w_qkv, v_attn_q_gain, v_attn_k_gain, v_attn_w_o, v_pool_norm, v_pool_w,
                          v_pool_scale, v_xattn_norm, v_mem_norm, v_xattn_w_q, v_xattn_w_kv, v_xattn_w_o, v_ffn_norm,
                          v_ffn_w_up, v_ffn_conv_w, v_ffn_conv_b, v_ffn_w_down, v_final_norm)))

    S, D = x.shape[1], x.shape[2]
    n_layers = xattn_norm.shape[0]
    n_up = ffn_w_up.shape[2]
    qkv_w = attn_w_qkv.shape[2] * N_DEV
    n_heads = qkv_w // HEAD_DIM - 2 * N_KV_HEADS
    n_rot = (n_heads + N_KV_HEADS) * HEAD_DIM // LANES
    group_w = pool_w.shape[3]
    xs, mems, tgt = x[0], mem[0], loss_target[0]
    xi, yi, ci = _position()
    dev = 4 * xi + 2 * yi + ci
    pos = jnp.stack([ci, 2 * xi + yi]).astype(jnp.int32)

    layers = range(n_layers)
    n_groups = pool_w.shape[1]
    small_vec = jnp.concatenate([pool_norm.reshape(-1), pool_scale.reshape(-1), ffn_conv_w.reshape(-1)])
    small_rows = _round_up(-(-small_vec.size // PACK_W), 8)
    small_vec = jnp.pad(small_vec, (0, small_rows * PACK_W - small_vec.size)).reshape(small_rows, PACK_W)
    w_qkv, w_o, small, attn_token = _allgather_blocks(
        "allgather_attn", [attn_w_qkv[0].astype(BF16), attn_w_o[0].astype(BF16), small_vec])
    small = small.reshape(N_DEV, -1)
    w_qkv = w_qkv.transpose(1, 0, 2).reshape(D, qkv_w)
    w_o = w_o.reshape(-1, D)
    blocks = [pool_w.reshape(-1, group_w)] + [xattn_w_q[l] for l in layers] + [xattn_w_kv.reshape(n_layers * D, -1)]
    blocks += [xattn_w_o[l] for l in layers] + [ffn_w_up.reshape(n_layers * D, n_up)] + [ffn_w_down[l] for l in layers]
    blocks = [b.astype(BF16) for b in blocks]
    blocks[0] = _after(blocks[0], attn_token)
    n_blk = len(blocks)
    lands = [lax.dynamic_update_index_in_dim(lax.empty((N_DEV,) + b.shape, BF16), b, dev, 0) for b in blocks]
    plan_chips, plan_sibling = _plan_gather_chips(n_blk), _plan_gather_sibling(n_blk)
    gather_sems = _copies_start("gather_chips_start", blocks + lands, (4 * n_blk,), plan_chips)
    attn_norm_late = _after(attn_norm, gather_sems[3])

    d_sh = pool_norm.shape[1]
    pool_norm_f = small[:, :d_sh].reshape(1, D)
    pool_scale_f = small[:, d_sh:2 * d_sh].reshape(1, D)
    conv_w_f = small[:, 2 * d_sh:2 * d_sh + ffn_conv_w.size].reshape(N_DEV, n_layers, 3, n_up)
    conv_b_f = ffn_conv_b.reshape(n_layers, N_DEV, 1, n_up)

    cos, sin = _rope_tables(S)
    bd = _head_mean_matrix()
    pad_w = qkv_w - (n_heads + N_KV_HEADS) * HEAD_DIM
    qk_gain = jnp.concatenate([jnp.tile(attn_q_gain[0], n_heads), jnp.tile(attn_k_gain[0], N_KV_HEADS),
                               jnp.ones((pad_w,), F32)]).reshape(1, qkv_w)
    qk_scale = jnp.concatenate([jnp.full((n_heads * HEAD_DIM,), HEAD_DIM ** -0.5, F32),
                                jnp.ones((qkv_w - n_heads * HEAD_DIM,), F32)]).reshape(1, qkv_w)

    saved = []

    def xattn_ffn_fwd(l, xin, hx=None):
        if hx is None:
            hx = _rmsnorm(f"xattn_norm{l}", xin, xattn_norm[l:l + 1], BF16)
        memn = _rmsnorm(f"mem_norm{l}", mems, mem_norm[l:l + 1], BF16)
        qx = _mm_nn(f"xattn_q{l}", hx, w_xq[l], BF16)
        kv = _mm_nn_bs(f"xattn_kv{l}", memn, w_xkv, BF16, l)
        ox = _xattn_fwd(f"xattn_fwd{l}", qx, kv)
        x2 = _mm_nn(f"xattn_o{l}", ox, w_xo[l], F32, res=xin)
        hf = _rmsnorm(f"ffn_norm{l}", x2, ffn_norm[l:l + 1], BF16)
        cw = conv_w_f[:, l].reshape(2, N_DEV // 2, 3, n_up)
        cb = conv_b_f[l].reshape(2, N_DEV // 2, 1, n_up)
        u, act = _ffn_up_act(f"ffn_up_act{l}", hf, w_up.reshape(2, N_DEV // 2, n_layers * D, n_up), cw, cb, l)
        x3 = _mm_nn_as(f"ffn_down{l}", act, w_down[l], F32, x2)
        saved.append(dict(xin=xin, hx=hx, memn=memn, qx=qx, kv=kv, ox=ox, x2=x2, hf=hf, u=u, cw=cw, cb=cb, act=act))
        return x3

    h0 = _rmsnorm("attn_norm", xs, attn_norm_late, BF16)
    qkv = _mm_nn("attn_qkv", h0, w_qkv, F32)
    qkr = _qk_rope("qk_rope", qkv, qk_gain, qk_scale, cos, sin, bd, n_rot)
    o_hm, lse = _attn_fwd("attn_fwd", qkr, n_heads)
    o_att = _heads_minor("attn_heads_minor", o_hm)
    arrived = _copies_wait("gather_chips_wait", gather_sems[2], gather_sems[0], gather_sems[1], plan_chips, o_att)
    pass_sems = _copies_start("gather_sibling_start", arrived[n_blk:], (3 * n_blk,), plan_sibling)
    x1 = _mm_nn("attn_o", o_att, _after(w_o, pass_sems[3]), F32, res=xs)
    hx0 = _rmsnorm("xattn_norm0", x1, xattn_norm[0:1], BF16)
    gathered = iter(_copies_wait("gather_sibling_wait", pass_sems[2], pass_sems[0], pass_sems[1], plan_sibling, hx0))
    w_pool = (next(gathered).reshape(N_DEV, n_groups, -1, group_w).transpose(1, 0, 2, 3)
              .reshape(n_groups, group_w, group_w))
    w_xq = [next(gathered).reshape(D, D) for l in layers]
    w_xkv = next(gathered)
    w_xo = [next(gathered).reshape(D, D) for l in layers]
    w_up = next(gathered)
    w_down = [next(gathered).reshape(-1, D) for l in layers]
    x3 = xattn_ffn_fwd(0, x1, hx0)
    hp = _rmsnorm("pool_norm", x3, pool_norm_f, F32)
    mixed = _pool_window("pool_window", hp, group_w, False, BF16)
    x4 = _pool_proj("pool_proj", mixed, w_pool, pool_scale_f, x3)
    x6 = xattn_ffn_fwd(1, x4)

    G = {}
    g, d_final, lvec = _loss_head("loss_head", x6, final_norm.reshape(1, D), tgt)
    G['final_norm'] = d_final.reshape(D)
    loss_part = (0.5 * jnp.sum(lvec) / D).reshape(1)

    d_xn, d_mn, d_fn, d_xq, d_xkv, d_xo, d_up, d_cw, d_cb, d_down = ([None] * n_layers for _ in range(10))

    def xattn_ffn_bwd(l, g, conv_b_late=None, after_act=None):
        sv = saved[l]
        d_down[l] = _mm_tn_as(f"ffn_down_dw{l}", sv['act'], g, F32)
        du, st = _ffn_act_bwd(f"ffn_act_bwd{l}", sv['u'], g, w_down[l], sv['cw'],
                              sv['cb'] if conv_b_late is None else conv_b_late)
        du = du.reshape(N_DEV, S, n_up)
        ffn_gain = ffn_norm[l:l + 1] if after_act is None else _after(ffn_norm[l:l + 1], after_act(du))
        st = st.reshape(N_DEV, 8, n_up)
        d_cw[l], d_cb[l] = st[:, 0:3], st[:, 3].reshape(-1)
        d_up[l] = _mm_tn_bs(f"ffn_up_dw{l}", sv['hf'], du, F32)
        g, d_fn[l] = _mm_nt_abs_norm_bwd(f"ffn_up_dx_norm_bwd{l}", du, w_up, sv['x2'], ffn_gain, g, l)
        d_xo[l] = _mm_tn(f"xattn_o_dw{l}", sv['ox'], g, F32)
        do = _mm_nt(f"xattn_o_dx{l}", g, w_xo[l], BF16)
        dq, dkv = _xattn_bwd(f"xattn_bwd{l}", sv['qx'], sv['kv'], do)
        d_xq[l] = _mm_tn(f"xattn_q_dw{l}", sv['hx'], dq, F32)
        d_xkv[l] = _mm_tn_bs(f"xattn_kv_dw{l}", sv['memn'], dkv, F32)
        dmemn = _mm_nt_abs(f"xattn_kv_dx{l}", dkv, w_xkv, D, F32, l)
        _, d_mn[l] = _rmsnorm_bwd(f"mem_norm_bwd{l}", mems, mem_norm[l:l + 1], dmemn)
        g, d_xn[l] = _mm_nt_norm_bwd(f"xattn_q_dx_norm_bwd{l}", dq, w_xq[l], sv['xin'], xattn_norm[l:l + 1], g)
        return g

    def reduce_start(tag, bufs):
        n = len(bufs)
        g4s = [b.reshape((N_DEV // 2, 2) + b.shape[1:]) for b in bufs]
        lands = [lax.empty((N_DEV // 2,) + b.shape[1:], F32) for b in bufs]
        plan = _plan_reduce_sibling(n)
        return (n, plan) + _copies_start(f"reduce_sibling_start_{tag}", g4s + lands, (N_DEV // 2 * n,), plan)

    def reduce_between(tag, state, after):
        n, plan, send_sems, recv_sems, thru, _ = state
        got = _copies_wait(f"reduce_sibling_wait_{tag}", thru, send_sems, recv_sems, plan, after)
        sums = [_add_sibling(f"reduce_add_sibling_{tag}{i}", got[i], got[n + i], pos) for i in range(n)]
        lands = [lax.empty((3,) + tb.shape[1:], BF16) for tb, _ in sums]
        plan = _plan_reduce_chips(n)
        return (n, plan, [own for _, own in sums]) + _copies_start(f"reduce_chips_start_{tag}",
                                                                    [tb for tb, _ in sums] + lands, (3 * n,), plan)

    def reduce_finish(tag, state, after):
        n, plan, owns, send_sems, recv_sems, thru, _ = state
        got = _copies_wait(f"reduce_chips_wait_{tag}", thru, send_sems, recv_sems, plan, after)
        return [_add_chips(f"reduce_add_chips_{tag}{i}", owns[i], got[n + i]) for i in range(n)]

    def layer_bufs(l):
        return [d_xq[l].reshape(N_DEV, -1, D), d_xkv[l], d_xo[l].reshape(N_DEV, -1, D), d_up[l],
                d_down[l].reshape(N_DEV, -1, D)]

    g = xattn_ffn_bwd(1, g)
    d_mixed, d_pool_w, d_pool_scale = _pool_proj_bwd("pool_proj_bwd", g, mixed, w_pool, pool_scale_f)
    dhp = _pool_window("pool_window_bwd", d_mixed, group_w, True, F32)
    g, d_pool_norm = _rmsnorm_bwd("pool_norm_bwd", x3, pool_norm_f, dhp, g)
    upper = reduce_start("upper", [d_pool_w.reshape(n_groups, N_DEV, -1, group_w).transpose(1, 0, 2, 3)
                                   .reshape(N_DEV, -1, group_w)] + layer_bufs(1))
    between = []

    def upper_between(du):
        between.append(reduce_between("upper", upper, du))
        return between[0][-1]

    g = xattn_ffn_bwd(0, g, _after(saved[0]['cb'], upper[-1]), upper_between)
    lower = reduce_start("lower", layer_bufs(0))
    d_wo = _mm_tn("attn_o_dw", o_att, g, F32)
    do = _mm_nt("attn_o_dx", g, _after(w_o, lower[-1]), BF16)
    lower = reduce_between("lower", lower, do)
    do_hm = _heads_major("attn_heads_major", do, lower[-1])
    dq_hm, dk_hm, dv_hm = _attn_bwd("attn_bwd", qkr, o_hm, lse, do_hm)
    red_lower = reduce_finish("lower", lower, dq_hm)
    d_qkv, d_gain = _qk_rope_bwd("qk_rope_bwd", dq_hm, dk_hm, dv_hm, qkv, qk_gain, qk_scale, cos, sin, bd)
    red_upper = reduce_finish("upper", between[0], d_qkv)
    d_wqkv = _mm_tn("attn_qkv_dw", h0, d_qkv, F32)
    last = reduce_start("last", [d_wqkv.reshape(D, N_DEV, -1).transpose(1, 0, 2), d_wo.reshape(N_DEV, -1, D)])
    grad_x, d_attn_norm = _mm_nt_norm_bwd("attn_qkv_dx_norm_bwd", d_qkv, w_qkv, xs, attn_norm, g, last[-1])
    last = reduce_between("last", last, grad_x)
    G['pool_w'] = red_upper[0].reshape(pool_w.shape)
    per_layer = [red_lower, red_upper[1:]]
    for i, n in enumerate(['xattn_w_q', 'xattn_w_kv', 'xattn_w_o', 'ffn_w_up', 'ffn_w_down']):
        G[n] = jnp.stack([per_layer[l][i] for l in layers])

    hq = n_heads * HEAD_DIM
    small_g = {'attn_norm': d_attn_norm, 'attn_q_gain': d_gain[0, :hq].reshape(n_heads, HEAD_DIM).sum(0),
               'attn_k_gain': d_gain[0, hq:hq + N_KV_HEADS * HEAD_DIM].reshape(N_KV_HEADS, HEAD_DIM).sum(0),
               'pool_norm': d_pool_norm, 'pool_scale': d_pool_scale,
               'xattn_norm': jnp.concatenate(d_xn), 'mem_norm': jnp.concatenate(d_mn), 'ffn_norm': jnp.concatenate(d_fn),
               'ffn_conv_w': jnp.stack(d_cw, axis=1), 'ffn_conv_b': jnp.stack(d_cb)}
    order = list(small_g)
    flat = jnp.concatenate([loss_part] + [small_g[n].reshape(-1) for n in order] + [G['final_norm']])
    ar_rows = _round_up(-(-flat.size // PACK_W), 8)
    flat = jnp.pad(flat, (0, ar_rows * PACK_W - flat.size)).reshape(ar_rows, PACK_W)
    summed = _sum_slots("allreduce_sum", _allgather_small("allreduce_gather", flat)).reshape(-1)
    loss = summed[0]
    red_last = reduce_finish("last", last, summed)
    G['attn_w_qkv'], G['attn_w_o'] = red_last[0][None], red_last[1][None]
    at = 1
    for n in order + ['final_norm']:
        size = G['final_norm'].size if n == 'final_norm' else small_g[n].size
        piece = summed[at:at + size]
        at += size
        if n in ('pool_norm', 'pool_scale'):
            piece = lax.dynamic_slice(piece, (dev * d_sh,), (d_sh,))
        elif n == 'ffn_conv_w':
            piece = lax.dynamic_index_in_dim(piece.reshape(N_DEV, n_layers, 3, n_up), dev, 0, keepdims=False)
        G[n] = piece.reshape(W[n].shape)

    deltas, new_m, new_v = [], [], []
    for n in names:
        d, nm, nv = _adamw(f"adamw_{n}", W[n], G[n], Mo[n], Vo[n])
        deltas.append(d)
        new_m.append(nm)
        new_v.append(nv)
    return (loss, grad_x[None], *[G[n] for n in names], *deltas, *new_m, *new_v)
```

```python
import jax
import jax.numpy as jnp
from jax import lax
from jax.experimental import pallas as pl
from jax.experimental.pallas import tpu as pltpu

F32 = jnp.float32
BF16 = jnp.bfloat16
MESH = pl.DeviceIdType.MESH

N_DEV = 8
EPS = 1e-6
HEAD_DIM = 64
N_KV_HEADS = 4
X_HEADS = 4
GRID_W = 64
ROPE_THETA = 10000.0
ROPE_PAIRS = HEAD_DIM // 4
POOL_WINDOWS = (2, 4, 8, 16)
POOL_PAD = 16
KEY_CHUNK = 1024
MM_ROWS = 1024
FFN_HALO = 16
LANES = 128
PACK_W = 1024
ADAM_LR, ADAM_B1, ADAM_B2, ADAM_EPS, ADAM_WD, ADAM_STEP = 0.001, 0.9, 0.999, 1e-08, 0.01, 10

_NN = (((1,), (0,)), ((), ()))
_NT = (((1,), (1,)), ((), ()))
_TN = (((0,), (0,)), ((), ()))


def _pc(body, *, name, **kw):
    return pl.pallas_call(body, name=name, **kw)


def _sem(*kinds):
    return pltpu.CompilerParams(dimension_semantics=kinds)


def _tile(n, pref, mult):
    best = None
    for t in range(mult, min(n, pref) + 1, mult):
        if n % t == 0:
            best = t
    return n if best is None else best


def _round_up(n, m):
    return (n + m - 1) // m * m


def _mm_call(name, a, b, dims, grid, a_spec, b_spec, o_spec, out_shape, kaxis, res=None, res_spec=None):
    nk = grid[kaxis]
    acc_shape = tuple(d for d in o_spec.block_shape if d is not None)
    in_place = out_shape.dtype == F32 and res is None
    use_scratch = nk > 1 and not in_place

    def body(*refs):
        refs = list(refs)
        acc = refs.pop() if use_scratch else None
        a_ref, b_ref = refs[:2]
        r_ref = refs[2] if res is not None else None
        o_ref = refs[-1]
        prod = lax.dot_general(a_ref[...].astype(BF16), b_ref[...].astype(BF16), dims, preferred_element_type=F32)
        if nk == 1:
            if r_ref is not None:
                prod = prod + r_ref[...]
            o_ref[...] = prod.astype(o_ref.dtype)
            return
        k = pl.program_id(kaxis)
        tgt = o_ref if in_place else acc

        @pl.when(k == 0)
        def _():
            tgt[...] = prod

        @pl.when(k > 0)
        def _():
            tgt[...] += prod

        if not in_place:
            @pl.when(k == nk - 1)
            def _():
                r = acc[...]
                if r_ref is not None:
                    r = r + r_ref[...]
                o_ref[...] = r.astype(o_ref.dtype)

    sem = tuple("arbitrary" if ax == kaxis else "parallel" for ax in range(len(grid)))
    ins = [a, b] if res is None else [a, b, res]
    specs = [a_spec, b_spec] if res is None else [a_spec, b_spec, res_spec]
    return _pc(body, name=name, grid=grid, in_specs=specs, out_specs=o_spec, out_shape=out_shape,
               scratch_shapes=[pltpu.VMEM(acc_shape, F32)] if use_scratch else [],
               compiler_params=_sem(*sem))(*ins)


def _mm_nn(name, a, b, out_dtype, res=None):
    M, K = a.shape
    N = b.shape[1]
    tm, tn, tk = _tile(M, MM_ROWS, 16), _tile(N, 1024, LANES), _tile(K, 1024, LANES)
    return _mm_call(name, a, b, _NN, (M // tm, N // tn, K // tk),
                    pl.BlockSpec((tm, tk), lambda i, j, k: (i, k)),
                    pl.BlockSpec((tk, tn), lambda i, j, k: (k, j)),
                    pl.BlockSpec((tm, tn), lambda i, j, k: (i, j)),
                    jax.ShapeDtypeStruct((M, N), out_dtype), 2, res,
                    pl.BlockSpec((tm, tn), lambda i, j, k: (i, j)))


def _mm_nt(name, a, b, out_dtype):
    M, K = a.shape
    N = b.shape[0]
    tm, tn, tk = _tile(M, MM_ROWS, 16), _tile(N, 1024, LANES), _tile(K, 1024, LANES)
    return _mm_call(name, a, b, _NT, (M // tm, N // tn, K // tk),
                    pl.BlockSpec((tm, tk), lambda i, j, k: (i, k)),
                    pl.BlockSpec((tn, tk), lambda i, j, k: (j, k)),
                    pl.BlockSpec((tm, tn), lambda i, j, k: (i, j)),
                    jax.ShapeDtypeStruct((M, N), out_dtype), 2)


def _mm_tn(name, a, b, out_dtype):
    R, M = a.shape
    N = b.shape[1]
    tm, tn, tr = _tile(M, 1024, LANES), _tile(N, 1024, LANES), _tile(R, MM_ROWS, 16)
    return _mm_call(name, a, b, _TN, (M // tm, N // tn, R // tr),
                    pl.BlockSpec((tr, tm), lambda i, j, k: (k, i)),
                    pl.BlockSpec((tr, tn), lambda i, j, k: (k, j)),
                    pl.BlockSpec((tm, tn), lambda i, j, k: (i, j)),
                    jax.ShapeDtypeStruct((M, N), out_dtype), 2)


def _mm_nn_bs(name, a, b, out_dtype, layer=0):
    M, K = a.shape
    J, _, n = b.shape
    tm, tk = _tile(M, MM_ROWS, 16), _tile(K, 1024, LANES)
    first = layer * (K // tk)
    return _mm_call(name, a, b, _NN, (J, M // tm, K // tk),
                    pl.BlockSpec((tm, tk), lambda j, i, k: (i, k)),
                    pl.BlockSpec((None, tk, n), lambda j, i, k: (j, first + k, 0)),
                    pl.BlockSpec((None, tm, n), lambda j, i, k: (j, i, 0)),
                    jax.ShapeDtypeStruct((J, M, n), out_dtype), 2)


def _mm_nn_as(name, a, b, out_dtype, res):
    J, M, n = a.shape
    N = b.shape[1]
    tm, tn = _tile(M, MM_ROWS, 16), _tile(N, 1024, LANES)
    return _mm_call(name, a, b, _NN, (M // tm, N // tn, J),
                    pl.BlockSpec((None, tm, n), lambda i, j, k: (k, i, 0)),
                    pl.BlockSpec((n, tn), lambda i, j, k: (k, j)),
                    pl.BlockSpec((tm, tn), lambda i, j, k: (i, j)),
                    jax.ShapeDtypeStruct((M, N), out_dtype), 2, res,
                    pl.BlockSpec((tm, tn), lambda i, j, k: (i, j)))


def _mm_tn_as(name, a, b, out_dtype):
    J, R, n = a.shape
    N = b.shape[1]
    tn, tr = _tile(N, 1024, LANES), _tile(R, MM_ROWS, 16)
    return _mm_call(name, a, b, _TN, (J, N // tn, R // tr),
                    pl.BlockSpec((None, tr, n), lambda j, jn, k: (j, k, 0)),
                    pl.BlockSpec((tr, tn), lambda j, jn, k: (k, jn)),
                    pl.BlockSpec((n, tn), lambda j, jn, k: (j, jn)),
                    jax.ShapeDtypeStruct((J * n, N), out_dtype), 2)


def _mm_nt_abs(name, a, b, N, out_dtype, layer=0):
    J, M, n = a.shape
    tm, tn = _tile(M, MM_ROWS, 16), _tile(N, 1024, LANES)
    first = layer * (N // tn)
    return _mm_call(name, a, b, _NT, (M // tm, N // tn, J),
                    pl.BlockSpec((None, tm, n), lambda i, j, k: (k, i, 0)),
                    pl.BlockSpec((None, tn, n), lambda i, j, k: (k, first + j, 0)),
                    pl.BlockSpec((tm, tn), lambda i, j, k: (i, j)),
                    jax.ShapeDtypeStruct((M, N), out_dtype), 2)


def _mm_tn_bs(name, a, b, out_dtype):
    R, M = a.shape
    J, _, n = b.shape
    tm, tr = _tile(M, 1024, LANES), _tile(R, MM_ROWS, 16)
    return _mm_call(name, a, b, _TN, (J, M // tm, R // tr),
                    pl.BlockSpec((tr, tm), lambda j, i, k: (k, i)),
                    pl.BlockSpec((None, tr, n), lambda j, i, k: (j, k, 0)),
                    pl.BlockSpec((None, tm, n), lambda j, i, k: (j, i, 0)),
                    jax.ShapeDtypeStruct((J, M, n), out_dtype), 2)


def _rmsnorm(name, x, g, out_dtype):
    R, D = x.shape
    tm = _tile(R, 512, 16)

    def body(x_ref, g_ref, o_ref):
        xv = x_ref[...]
        r = lax.rsqrt(jnp.mean(xv * xv, axis=-1, keepdims=True) + EPS)
        o_ref[...] = (xv * r * g_ref[...]).astype(o_ref.dtype)

    return _pc(body, name=name, grid=(R // tm,),
               in_specs=[pl.BlockSpec((tm, D), lambda i: (i, 0)), pl.BlockSpec((1, D), lambda i: (0, 0))],
               out_specs=pl.BlockSpec((tm, D), lambda i: (i, 0)),
               out_shape=jax.ShapeDtypeStruct((R, D), out_dtype), compiler_params=_sem("parallel"))(x, g)


def _rmsnorm_bwd(name, x, g, dh, dres=None):
    R, D = x.shape
    tm = _tile(R, 512, 16)

    def body(*refs):
        if dres is None:
            x_ref, g_ref, dh_ref, dx_ref, dg_ref = refs
            dres_ref = None
        else:
            x_ref, g_ref, dh_ref, dres_ref, dx_ref, dg_ref = refs
        xv = x_ref[...]
        r = lax.rsqrt(jnp.mean(xv * xv, axis=-1, keepdims=True) + EPS)
        xh = xv * r
        dhv = dh_ref[...].astype(F32)

        @pl.when(pl.program_id(0) == 0)
        def _():
            dg_ref[...] = jnp.zeros_like(dg_ref)

        dg_ref[...] += jnp.sum(dhv * xh, axis=0, keepdims=True)
        dxh = dhv * g_ref[...]
        dx = r * (dxh - xh * jnp.mean(dxh * xh, axis=-1, keepdims=True))
        if dres_ref is not None:
            dx = dx + dres_ref[...]
        dx_ref[...] = dx

    row = pl.BlockSpec((tm, D), lambda i: (i, 0))
    vec = pl.BlockSpec((1, D), lambda i: (0, 0))
    ins = [x, g, dh] + ([] if dres is None else [dres])
    specs = [row, vec, row] + ([] if dres is None else [row])
    return _pc(body, name=name, grid=(R // tm,), in_specs=specs, out_specs=(row, vec),
               out_shape=(jax.ShapeDtypeStruct((R, D), F32), jax.ShapeDtypeStruct((1, D), F32)),
               compiler_params=_sem("arbitrary"))(*ins)


def _mm_norm_bwd(name, a, b, grid, a_spec, b_spec, tm, x, gain, dres, after=None):
    M, D = x.shape
    nk = grid[2]

    def body(*refs):
        a_ref, b_ref, x_ref, g_ref, r_ref = refs[:5]
        dx_ref, dg_ref = refs[-3 if nk > 1 else -2:][:2]
        acc = refs[-1] if nk > 1 else None
        i, k = pl.program_id(0), pl.program_id(2)
        prod = lax.dot_general(a_ref[...].astype(BF16), b_ref[...].astype(BF16), _NT, preferred_element_type=F32)

        @pl.when((i == 0) & (k == 0))
        def _():
            dg_ref[...] = jnp.zeros_like(dg_ref)

        def finish(dh):
            xv = x_ref[...]
            r = lax.rsqrt(jnp.mean(xv * xv, axis=-1, keepdims=True) + EPS)
            xh = xv * r
            dg_ref[...] += jnp.sum(dh * xh, axis=0, keepdims=True)
            dxh = dh * g_ref[...]
            dx_ref[...] = r * (dxh - xh * jnp.mean(dxh * xh, axis=-1, keepdims=True)) + r_ref[...]

        if nk == 1:
            finish(prod)
            return

        @pl.when(k == 0)
        def _():
            acc[...] = prod

        @pl.when(k > 0)
        def _():
            acc[...] += prod

        @pl.when(k == nk - 1)
        def _():
            finish(acc[...])

    row = pl.BlockSpec((tm, D), lambda i, j, k: (i, 0))
    vec = pl.BlockSpec((1, D), lambda i, j, k: (0, 0))
    ins = [a, b, x, gain, dres] + ([] if after is None else [after])
    specs = [a_spec, b_spec, row, vec, row] + ([] if after is None else [pl.BlockSpec(memory_space=pl.ANY)])
    return _pc(body, name=name, grid=grid, in_specs=specs, out_specs=(row, vec),
               out_shape=(jax.ShapeDtypeStruct((M, D), F32), jax.ShapeDtypeStruct((1, D), F32)),
               scratch_shapes=[pltpu.VMEM((tm, D), F32)] if nk > 1 else [],
               compiler_params=_sem("arbitrary", "arbitrary", "arbitrary"))(*ins)


def _mm_nt_norm_bwd(name, a, b, x, gain, dres, after=None):
    M, K = a.shape
    D = b.shape[0]
    tm, tk = _tile(M, MM_ROWS, 16), _tile(K, 1024, LANES)
    return _mm_norm_bwd(name, a, b, (M // tm, 1, K // tk),
                        pl.BlockSpec((tm, tk), lambda i, j, k: (i, k)),
                        pl.BlockSpec((D, tk), lambda i, j, k: (0, k)), tm, x, gain, dres, after)


def _mm_nt_abs_norm_bwd(name, a, b, x, gain, dres, layer=0):
    J, M, n = a.shape
    D = x.shape[1]
    tm = _tile(M, MM_ROWS, 16)
    return _mm_norm_bwd(name, a, b, (M // tm, 1, J),
                        pl.BlockSpec((None, tm, n), lambda i, j, k: (k, i, 0)),
                        pl.BlockSpec((None, D, n), lambda i, j, k: (k, layer, 0)), tm, x, gain, dres)


def _loss_head(name, x, g, tgt):
    R, D = x.shape
    tm = _tile(R, 512, 16)

    def body(x_ref, g_ref, t_ref, dx_ref, dg_ref, l_ref):
        xv = x_ref[...]
        r = lax.rsqrt(jnp.mean(xv * xv, axis=-1, keepdims=True) + EPS)
        xh = xv * r
        err = xh * g_ref[...] - t_ref[...]

        @pl.when(pl.program_id(0) == 0)
        def _():
            dg_ref[...] = jnp.zeros_like(dg_ref)
            l_ref[...] = jnp.zeros_like(l_ref)

        l_ref[...] += jnp.sum(err * err, axis=0, keepdims=True)
        dy = err * (1.0 / D)
        dg_ref[...] += jnp.sum(dy * xh, axis=0, keepdims=True)
        dxh = dy * g_ref[...]
        dx_ref[...] = r * (dxh - xh * jnp.mean(dxh * xh, axis=-1, keepdims=True))

    row = pl.BlockSpec((tm, D), lambda i: (i, 0))
    vec = pl.BlockSpec((1, D), lambda i: (0, 0))
    return _pc(body, name=name, grid=(R // tm,), in_specs=[row, vec, row], out_specs=(row, vec, vec),
               out_shape=(jax.ShapeDtypeStruct((R, D), F32), jax.ShapeDtypeStruct((1, D), F32),
                          jax.ShapeDtypeStruct((1, D), F32)),
               compiler_params=_sem("arbitrary"))(x, g, tgt)


def _rope_tables(S):
    n_rows = S // GRID_W
    row = jnp.repeat(jnp.arange(n_rows, dtype=F32), GRID_W)
    col = jnp.tile(jnp.arange(GRID_W, dtype=F32), n_rows)
    inv_freq = ROPE_THETA ** (-jnp.arange(ROPE_PAIRS, dtype=F32) / ROPE_PAIRS)
    ang = jnp.stack([row[:, None] * inv_freq, col[:, None] * inv_freq], axis=1)
    cos, sin = jnp.cos(ang), jnp.sin(ang)
    c = jnp.broadcast_to(cos[:, :, None, :], (S, 2, 2, ROPE_PAIRS)).reshape(S, HEAD_DIM)
    s = jnp.stack([-sin, sin], axis=2).reshape(S, HEAD_DIM)
    reps = LANES // HEAD_DIM
    return jnp.tile(c, (1, reps)), jnp.tile(s, (1, reps))


def _head_mean_matrix():
    h = jnp.arange(LANES) // HEAD_DIM
    m = jnp.where(h[:, None] == h[None, :], 1.0 / HEAD_DIM, 0.0).astype(BF16)
    return jnp.concatenate([m, m], axis=0)


def _head_mean(v, bd):
    hi = v.astype(BF16)
    lo = (v - hi.astype(F32)).astype(BF16)
    return jnp.dot(jnp.concatenate([hi, lo], axis=1), bd, preferred_element_type=F32)


def _swap_halves(y):
    lane = lax.broadcasted_iota(jnp.int32, y.shape, 1)
    return jnp.where(lane % 32 < 16, pltpu.roll(y, LANES - 16, 1), pltpu.roll(y, 16, 1))


def _qk_rope(name, qkv, gain, scale, cos, sin, bd, n_rot):
    S, W = qkv.shape
    tm = _tile(S, 2048, 16)
    per = LANES // HEAD_DIM

    def body(x_ref, g_ref, s_ref, c_ref, sn_ref, bd_ref, o_ref):
        j = pl.program_id(1)
        xv = x_ref[...]

        def put(v):
            for h in range(per):
                o_ref[h] = v[:, h * HEAD_DIM:(h + 1) * HEAD_DIM].astype(BF16)

        @pl.when(j < n_rot)
        def _():
            ms = _head_mean(xv * xv, bd_ref[...])
            y = xv * lax.rsqrt(ms + EPS) * g_ref[...] * s_ref[...]
            put(y * c_ref[...] + _swap_halves(y) * sn_ref[...])

        @pl.when(j >= n_rot)
        def _():
            put(xv)

    blk = pl.BlockSpec((tm, LANES), lambda i, j: (i, j))
    vec = pl.BlockSpec((1, LANES), lambda i, j: (0, j))
    tab = pl.BlockSpec((tm, LANES), lambda i, j: (i, 0))
    return _pc(body, name=name, grid=(S // tm, W // LANES),
               in_specs=[blk, vec, vec, tab, tab, pl.BlockSpec((2 * LANES, LANES), lambda i, j: (0, 0))],
               out_specs=pl.BlockSpec((per, tm, HEAD_DIM), lambda i, j: (j, i, 0)),
               out_shape=jax.ShapeDtypeStruct((W // HEAD_DIM, S, HEAD_DIM), BF16),
               compiler_params=_sem("parallel", "parallel"))(qkv, gain, scale, cos, sin, bd)


def _qk_rope_bwd(name, dq, dk, dv, qkv, gain, scale, cos, sin, bd):
    S, W = qkv.shape
    tm = _tile(S, 2048, 16)
    per = LANES // HEAD_DIM
    nq, nk, nv = dq.shape[0] // per, dk.shape[0] // per, dv.shape[0] // per
    n_rot = nq + nk

    def body(dq_ref, dk_ref, dv_ref, x_ref, g_ref, s_ref, c_ref, sn_ref, bd_ref, dx_ref, dg_ref):
        j, i = pl.program_id(0), pl.program_id(1)

        @pl.when(i == 0)
        def _():
            dg_ref[...] = jnp.zeros_like(dg_ref)

        def rotate_back(d_ref):
            dv = jnp.concatenate([d_ref[h] for h in range(per)], axis=1)
            xv = x_ref[...]
            ms = _head_mean(xv * xv, bd_ref[...])
            r = lax.rsqrt(ms + EPS)
            z = xv * r
            dy = (dv * c_ref[...] - _swap_halves(dv) * sn_ref[...]) * s_ref[...]
            dg_ref[...] += jnp.sum(dy * z, axis=0, keepdims=True)
            dz = dy * g_ref[...]
            mz = _head_mean(dz * z, bd_ref[...])
            dx_ref[...] = (r * (dz - z * mz)).astype(BF16)

        @pl.when(j < nq)
        def _():
            rotate_back(dq_ref)

        @pl.when((j >= nq) & (j < n_rot))
        def _():
            rotate_back(dk_ref)

        @pl.when(j >= n_rot)
        def _():
            dx_ref[...] = jnp.concatenate([dv_ref[h] for h in range(per)], axis=1).astype(BF16)

    def part(first, count):
        return pl.BlockSpec((per, tm, HEAD_DIM), lambda j, i: (jnp.clip(j - first, 0, count - 1), i, 0))

    blk = pl.BlockSpec((tm, LANES), lambda j, i: (i, j))
    vec = pl.BlockSpec((1, LANES), lambda j, i: (0, j))
    tab = pl.BlockSpec((tm, LANES), lambda j, i: (i, 0))
    return _pc(body, name=name, grid=(W // LANES, S // tm),
               in_specs=[part(0, nq), part(nq, nk), part(n_rot, nv), blk, vec, vec, tab, tab,
                         pl.BlockSpec((2 * LANES, LANES), lambda j, i: (0, 0))],
               out_specs=(blk, vec),
               out_shape=(jax.ShapeDtypeStruct((S, W), BF16), jax.ShapeDtypeStruct((1, W), F32)),
               compiler_params=_sem("parallel", "arbitrary"))(dq, dk, dv, qkv, gain, scale, cos, sin, bd)


def _softmax_rows(s):
    m = jnp.max(s, axis=-1, keepdims=True)
    p = jnp.exp(s - m)
    return p, jnp.sum(p, axis=-1, keepdims=True)


def _attn_fwd(name, qkv, H):
    _, S, dh = qkv.shape
    G = H // N_KV_HEADS
    tq = _tile(S, 256, 16)
    kc = _tile(S, KEY_CHUNK, LANES)
    R = G * tq

    def body(q_ref, k_ref, v_ref, o_ref, lse_ref):
        q = q_ref[...].reshape(R, dh)
        m = jnp.full((R, 1), -1e30, F32)
        l = jnp.zeros((R, 1), F32)
        acc = jnp.zeros((R, dh), F32)
        for c in range(S // kc):
            rows = slice(c * kc, (c + 1) * kc)
            s = lax.dot_general(q, k_ref[rows, :], _NT, preferred_element_type=F32)
            m_new = jnp.maximum(m, jnp.max(s, axis=-1, keepdims=True))
            alpha = jnp.exp(m - m_new)
            p = jnp.exp(s - m_new)
            l = alpha * l + jnp.sum(p, axis=-1, keepdims=True)
            acc = alpha * acc + jnp.dot(p.astype(BF16), v_ref[rows, :], preferred_element_type=F32)
            m = m_new
        o_ref[...] = (acc / l).astype(BF16).reshape(G, tq, dh)
        lse_ref[...] = (m + jnp.log(l)).reshape(G, tq, 1)

    qs = pl.BlockSpec((G, tq, dh), lambda kv, i: (kv, i, 0))
    ls = pl.BlockSpec((G, tq, 1), lambda kv, i: (kv, i, 0))
    ks = pl.BlockSpec((None, S, dh), lambda kv, i: (H + kv, 0, 0))
    vs = pl.BlockSpec((None, S, dh), lambda kv, i: (H + N_KV_HEADS + kv, 0, 0))
    return _pc(body, name=name, grid=(N_KV_HEADS, S // tq), in_specs=[qs, ks, vs], out_specs=(qs, ls),
               out_shape=(jax.ShapeDtypeStruct((H, S, dh), BF16), jax.ShapeDtypeStruct((H, S, 1), F32)),
               compiler_params=_sem("parallel", "parallel"))(qkv, qkv, qkv)


def _attn_bwd(name, qkv, o, lse, do):
    H, S, dh = o.shape
    G = H // N_KV_HEADS
    tq = _tile(S, 128, 16)
    kc = _tile(S, KEY_CHUNK, LANES)
    R = G * tq

    def body(q_ref, k_ref, v_ref, o_ref, lse_ref, do_ref, dq_ref, dk_ref, dv_ref):
        @pl.when(pl.program_id(1) == 0)
        def _():
            dk_ref[...] = jnp.zeros_like(dk_ref)
            dv_ref[...] = jnp.zeros_like(dv_ref)

        qq, dd = q_ref[...].reshape(R, dh), do_ref[...].reshape(R, dh)
        delta = jnp.sum(dd.astype(F32) * o_ref[...].reshape(R, dh).astype(F32), axis=-1, keepdims=True)
        lse = lse_ref[...].reshape(R, 1)
        dq = jnp.zeros((R, dh), F32)
        for c in range(S // kc):
            rows = slice(c * kc, (c + 1) * kc)
            kk, vv = k_ref[rows, :], v_ref[rows, :]
            p = jnp.exp(lax.dot_general(qq, kk, _NT, preferred_element_type=F32) - lse)
            dv_ref[rows, :] += lax.dot_general(p.astype(BF16), dd, _TN, preferred_element_type=F32)
            dp = lax.dot_general(dd, vv, _NT, preferred_element_type=F32)
            ds = (p * (dp - delta)).astype(BF16)
            dq = dq + jnp.dot(ds, kk, preferred_element_type=F32)
            dk_ref[rows, :] += lax.dot_general(ds, qq, _TN, preferred_element_type=F32)
        dq_ref[...] = dq.reshape(G, tq, dh)

    qs = pl.BlockSpec((G, tq, dh), lambda kv, i: (kv, i, 0))
    ls = pl.BlockSpec((G, tq, 1), lambda kv, i: (kv, i, 0))
    ks = pl.BlockSpec((None, S, dh), lambda kv, i: (H + kv, 0, 0))
    vs = pl.BlockSpec((None, S, dh), lambda kv, i: (H + N_KV_HEADS + kv, 0, 0))
    acc = pl.BlockSpec((None, S, dh), lambda kv, i: (kv, 0, 0))
    return _pc(body, name=name, grid=(N_KV_HEADS, S // tq), in_specs=[qs, ks, vs, qs, ls, qs],
               out_specs=(qs, acc, acc),
               out_shape=(jax.ShapeDtypeStruct((H, S, dh), F32), jax.ShapeDtypeStruct((N_KV_HEADS, S, dh), F32),
                          jax.ShapeDtypeStruct((N_KV_HEADS, S, dh), F32)),
               compiler_params=_sem("parallel", "arbitrary"))(qkv, qkv, qkv, o, lse, do)


def _xattn_fwd(name, q, kv):
    S, D = q.shape
    _, M, dh = kv.shape
    scale = dh ** -0.5
    tq = _tile(S, 512, 16)

    def body(q_ref, kv_ref, o_ref):
        for h in range(X_HEADS):
            lo, hi = h * dh, (h + 1) * dh
            s = lax.dot_general(q_ref[:, lo:hi], kv_ref[h], _NT, preferred_element_type=F32) * scale
            p, l = _softmax_rows(s)
            o = jnp.dot(p.astype(BF16), kv_ref[X_HEADS + h], preferred_element_type=F32)
            o_ref[:, lo:hi] = (o / l).astype(BF16)

    row = pl.BlockSpec((tq, D), lambda i: (i, 0))
    return _pc(body, name=name, grid=(S // tq,),
               in_specs=[row, pl.BlockSpec((2 * X_HEADS, M, dh), lambda i: (0, 0, 0))],
               out_specs=row, out_shape=jax.ShapeDtypeStruct((S, D), BF16),
               compiler_params=_sem("parallel"))(q, kv)


def _xattn_bwd(name, q, kv, do):
    S, D = q.shape
    _, M, dh = kv.shape
    scale = dh ** -0.5
    tq = _tile(S, 512, 16)

    def body(q_ref, kv_ref, do_ref, dq_ref, dkv_ref):
        @pl.when(pl.program_id(0) == 0)
        def _():
            dkv_ref[...] = jnp.zeros_like(dkv_ref)

        for h in range(X_HEADS):
            lo, hi = h * dh, (h + 1) * dh
            qh, kh, vh, doh = q_ref[:, lo:hi], kv_ref[h], kv_ref[X_HEADS + h], do_ref[:, lo:hi]
            s = lax.dot_general(qh, kh, _NT, preferred_element_type=F32) * scale
            p, l = _softmax_rows(s)
            pn = p / l
            dkv_ref[X_HEADS + h] += lax.dot_general(pn.astype(BF16), doh, _TN, preferred_element_type=F32)
            dp = lax.dot_general(doh, vh, _NT, preferred_element_type=F32)
            ds = (pn * (dp - jnp.sum(pn * dp, axis=-1, keepdims=True)) * scale).astype(BF16)
            dq_ref[:, lo:hi] = jnp.dot(ds, kh, preferred_element_type=F32).astype(BF16)
            dkv_ref[h] += lax.dot_general(ds, qh, _TN, preferred_element_type=F32)

    row = pl.BlockSpec((tq, D), lambda i: (i, 0))
    full = pl.BlockSpec((2 * X_HEADS, M, dh), lambda i: (0, 0, 0))
    return _pc(body, name=name, grid=(S // tq,), in_specs=[row, full, row], out_specs=(row, full),
               out_shape=(jax.ShapeDtypeStruct((S, D), BF16), jax.ShapeDtypeStruct((2 * X_HEADS, M, dh), F32)),
               compiler_params=_sem("arbitrary"))(q, kv, do)


def _sigmoid(x):
    return 1.0 / (1.0 + jnp.exp(-x))


def _halo_specs(tm, n, S):
    nb = tm // 8
    last8 = S // 8 - 1
    main = pl.BlockSpec((2, None, tm, n), lambda j, i: (0, j, i, 0))
    prev = pl.BlockSpec((2, None, 8, n), lambda j, i: (0, j, jnp.maximum(i * nb - 1, 0), 0))
    nxt = pl.BlockSpec((2, None, 8, n), lambda j, i: (0, j, jnp.minimum((i + 1) * nb, last8), 0))
    return main, prev, nxt


def _ffn_up_act(name, h, w, cw, cb, layer):
    S, K = h.shape
    _, J, _, n = w.shape
    tm = _tile(S, 512, FFN_HALO)
    nblk = S // tm
    hb, last = tm // FFN_HALO, S // FFN_HALO - 1
    te = tm + 2 * FFN_HALO

    def body(h_ref, hp_ref, hn_ref, w_ref, cw_ref, b_ref, u_ref, a_ref):
        i = pl.program_id(1)
        zero = jnp.zeros((FFN_HALO, K), BF16)
        he = jnp.concatenate([jnp.where(i == 0, zero, hp_ref[...]), h_ref[...],
                              jnp.where(i == nblk - 1, zero, hn_ref[...])], axis=0)
        mid = slice(FFN_HALO, tm + FFN_HALO)
        c = []
        for half in range(2):
            ue = jnp.dot(he, w_ref[half], preferred_element_type=F32)
            um = ue[mid]
            u_ref[half] = um
            k = cw_ref[half]
            c.append(pltpu.roll(ue, 1, 0)[mid] * k[0:1] + um * k[1:2] + pltpu.roll(ue, te - 1, 0)[mid] * k[2:3]
                     + b_ref[half])
        a_ref[...] = (c[0] * _sigmoid(c[0]) * c[1]).astype(BF16)

    return _pc(body, name=name, grid=(J, nblk),
               in_specs=[pl.BlockSpec((tm, K), lambda j, i: (i, 0)),
                         pl.BlockSpec((FFN_HALO, K), lambda j, i: (jnp.maximum(i * hb - 1, 0), 0)),
                         pl.BlockSpec((FFN_HALO, K), lambda j, i: (jnp.minimum((i + 1) * hb, last), 0)),
                         pl.BlockSpec((2, None, K, n), lambda j, i: (0, j, layer, 0)),
                         pl.BlockSpec((2, None, 3, n), lambda j, i: (0, j, 0, 0)),
                         pl.BlockSpec((2, None, 1, n), lambda j, i: (0, j, 0, 0))],
               out_specs=(pl.BlockSpec((2, None, tm, n), lambda j, i: (0, j, i, 0)),
                          pl.BlockSpec((None, tm, n), lambda j, i: (j, i, 0))),
               out_shape=(jax.ShapeDtypeStruct((2, J, S, n), F32), jax.ShapeDtypeStruct((J, S, n), BF16)),
               compiler_params=_sem("parallel", "parallel"))(h, h, h, w, cw, cb)


def _ffn_act_bwd(name, u, g, w_down, cw, cb):
    _, J, S, n = u.shape
    D = g.shape[1]
    tm = _tile(S, 256, 16)
    nblk = S // tm
    te = tm + 16
    nb = tm // 8
    last8 = S // 8 - 1

    def body(u_ref, up_ref, un_ref, g_ref, gp_ref, gn_ref, wd_ref, w_ref, b_ref, du_ref, st_ref):
        i = pl.program_id(1)

        @pl.when(i == 0)
        def _():
            st_ref[...] = jnp.zeros_like(st_ref)

        def extended(before, main, after):
            return jnp.concatenate([jnp.where(i == 0, 0.0, before), main, jnp.where(i == nblk - 1, 0.0, after)], axis=0)

        mid = slice(8, tm + 8)
        da_e = lax.dot_general(extended(gp_ref[...], g_ref[...], gn_ref[...]).astype(BF16), wd_ref[...], _NT,
                               preferred_element_type=F32)
        ue, c = [], []
        for half in range(2):
            e = extended(up_ref[half], u_ref[half], un_ref[half])
            w = w_ref[half]
            ue.append((pltpu.roll(e, 1, 0), e, pltpu.roll(e, te - 1, 0)))
            c.append(ue[half][0] * w[0:1] + e * w[1:2] + ue[half][2] * w[2:3] + b_ref[half])
        sg = _sigmoid(c[0])
        dc = [da_e * c[1] * (sg * (1.0 + c[0] * (1.0 - sg))), da_e * (c[0] * sg)]
        r8 = lax.broadcasted_iota(jnp.int32, (8, n), 0)
        for half in range(2):
            w, d, (e_before, e, e_after) = w_ref[half], dc[half], ue[half]
            dm = d[mid]
            du = pltpu.roll(d, te - 1, 0)[mid] * w[0:1] + dm * w[1:2] + pltpu.roll(d, 1, 0)[mid] * w[2:3]
            du_ref[half] = du.astype(BF16)
            s0 = jnp.sum(dm * e_before[mid], axis=0, keepdims=True)
            s1 = jnp.sum(dm * e[mid], axis=0, keepdims=True)
            s2 = jnp.sum(dm * e_after[mid], axis=0, keepdims=True)
            s3 = jnp.sum(dm, axis=0, keepdims=True)
            st_ref[half] += jnp.where(r8 == 0, s0, jnp.where(r8 == 1, s1, jnp.where(r8 == 2, s2,
                                      jnp.where(r8 == 3, s3, 0.0))))

    main, prev, nxt = _halo_specs(tm, n, S)
    gmain = pl.BlockSpec((tm, D), lambda j, i: (i, 0))
    gprev = pl.BlockSpec((8, D), lambda j, i: (jnp.maximum(i * nb - 1, 0), 0))
    gnxt = pl.BlockSpec((8, D), lambda j, i: (jnp.minimum((i + 1) * nb, last8), 0))
    return _pc(body, name=name, grid=(J, nblk),
               in_specs=[main, prev, nxt, gmain, gprev, gnxt, pl.BlockSpec((n, D), lambda j, i: (j, 0)),
                         pl.BlockSpec((2, None, 3, n), lambda j, i: (0, j, 0, 0)),
                         pl.BlockSpec((2, None, 1, n), lambda j, i: (0, j, 0, 0))],
               out_specs=(main, pl.BlockSpec((2, None, 8, n), lambda j, i: (0, j, 0, 0))),
               out_shape=(jax.ShapeDtypeStruct((2, J, S, n), BF16), jax.ShapeDtypeStruct((2, J, 8, n), F32)),
               compiler_params=_sem("parallel", "arbitrary"))(u, u, u, g, g, g, w_down, cw, cb)


def _window_count(t, w, S):
    lo = jnp.maximum(t - w // 2, 0)
    hi = jnp.minimum(t + w - w // 2, S)
    return (hi - lo).astype(F32)


def _trailing_sums(x, w):
    k = 1
    while k < w:
        x = x + pltpu.roll(x, k, 0)
        k *= 2
    return x


def _pool_window(name, h, group_w, adjoint, out_dtype):
    S, D = h.shape
    SP = S + 2 * POOL_PAD
    per_group = group_w // LANES

    def body(h_ref, o_ref, xp):
        g = pl.program_id(0) // per_group
        t = lax.broadcasted_iota(jnp.int32, (S, LANES), 0)
        xp[0:POOL_PAD, :] = jnp.zeros((POOL_PAD, LANES), F32)
        xp[S + POOL_PAD:SP, :] = jnp.zeros((POOL_PAD, LANES), F32)
        for gi, w in enumerate(POOL_WINDOWS):
            @pl.when(g == gi)
            def _():
                hv = h_ref[...]
                cnt = _window_count(t, w, S)
                xp[POOL_PAD:S + POOL_PAD, :] = hv / cnt if adjoint else hv
                ahead = w // 2 if adjoint else w // 2 - 1
                sw = _trailing_sums(xp[...], w)
                if ahead:
                    sw = pltpu.roll(sw, SP - ahead, 0)
                win = sw[POOL_PAD:S + POOL_PAD]
                o_ref[...] = ((win if adjoint else win / cnt) - hv).astype(out_dtype)

    col = pl.BlockSpec((S, LANES), lambda j: (0, j))
    return _pc(body, name=name, grid=(D // LANES,), in_specs=[col], out_specs=col,
               out_shape=jax.ShapeDtypeStruct((S, D), out_dtype),
               scratch_shapes=[pltpu.VMEM((SP, LANES), F32)], compiler_params=_sem("parallel"))(h)


def _pool_proj(name, mixed, w, scale, res):
    S, D = mixed.shape
    G, gw, _ = w.shape
    tm = _tile(S, 512, 16)

    def body(m_ref, w_ref, s_ref, r_ref, o_ref):
        for g in range(G):
            lo, hi = g * gw, (g + 1) * gw
            y = jnp.dot(m_ref[:, lo:hi], w_ref[g], preferred_element_type=F32)
            o_ref[:, lo:hi] = r_ref[:, lo:hi] + y * s_ref[:, lo:hi]

    row = pl.BlockSpec((tm, D), lambda i: (i, 0))
    return _pc(body, name=name, grid=(S // tm,),
               in_specs=[row, pl.BlockSpec((G, gw, gw), lambda i: (0, 0, 0)), pl.BlockSpec((1, D), lambda i: (0, 0)), row],
               out_specs=row, out_shape=jax.ShapeDtypeStruct((S, D), F32),
               compiler_params=_sem("parallel"))(mixed, w, scale, res)


def _pool_proj_bwd(name, dy, mixed, w, scale):
    S, D = mixed.shape
    G, gw, _ = w.shape
    tm = _tile(S, 512, 16)

    def body(dy_ref, m_ref, w_ref, s_ref, dm_ref, dw_ref, ds_ref):
        @pl.when(pl.program_id(0) == 0)
        def _():
            dw_ref[...] = jnp.zeros_like(dw_ref)
            ds_ref[...] = jnp.zeros_like(ds_ref)

        for g in range(G):
            lo, hi = g * gw, (g + 1) * gw
            mg, dyg = m_ref[:, lo:hi], dy_ref[:, lo:hi]
            y = jnp.dot(mg, w_ref[g], preferred_element_type=F32)
            ds_ref[:, lo:hi] += jnp.sum(dyg * y, axis=0, keepdims=True)
            dyp = (dyg * s_ref[:, lo:hi]).astype(BF16)
            dm_ref[:, lo:hi] = lax.dot_general(dyp, w_ref[g], _NT, preferred_element_type=F32)
            dw_ref[g] += lax.dot_general(mg, dyp, _TN, preferred_element_type=F32)

    row = pl.BlockSpec((tm, D), lambda i: (i, 0))
    wsp = pl.BlockSpec((G, gw, gw), lambda i: (0, 0, 0))
    vec = pl.BlockSpec((1, D), lambda i: (0, 0))
    return _pc(body, name=name, grid=(S // tm,), in_specs=[row, row, wsp, vec], out_specs=(row, wsp, vec),
               out_shape=(jax.ShapeDtypeStruct((S, D), F32), jax.ShapeDtypeStruct((G, gw, gw), F32),
                          jax.ShapeDtypeStruct((1, D), F32)),
               compiler_params=_sem("arbitrary"))(dy, mixed, w, scale)


def _adamw(name, w, g, m, v):
    shape = w.shape
    C = shape[-1]
    R = w.size // C
    tm = _tile(R, 512, 8)

    def body(w_ref, g_ref, m_ref, v_ref, d_ref, nm_ref, nv_ref):
        gv = g_ref[...]
        nm = ADAM_B1 * m_ref[...] + (1.0 - ADAM_B1) * gv
        nv = ADAM_B2 * v_ref[...] + (1.0 - ADAM_B2) * (gv * gv)
        m_hat = nm / (1.0 - ADAM_B1 ** ADAM_STEP)
        v_hat = nv / (1.0 - ADAM_B2 ** ADAM_STEP)
        d_ref[...] = -ADAM_LR * (m_hat / (jnp.sqrt(v_hat) + ADAM_EPS) + ADAM_WD * w_ref[...])
        nm_ref[...] = nm
        nv_ref[...] = nv

    blk = pl.BlockSpec((tm, C), lambda i: (i, 0))
    sd = jax.ShapeDtypeStruct((R, C), F32)
    outs = _pc(body, name=name, grid=(R // tm,), in_specs=[blk] * 4, out_specs=(blk,) * 3, out_shape=(sd,) * 3,
               compiler_params=_sem("parallel"))(*(a.reshape(R, C) for a in (w, g, m, v)))
    return tuple(o.reshape(shape) for o in outs)


def _position():
    return lax.axis_index("x"), lax.axis_index("y"), lax.axis_index("c")


def _flip(v, bit):
    return 1 - v if bit else v


def _allgather_small(name, v):
    R, W = v.shape

    def body(v_ref, out_ref, send_sems, recv_sems):
        x, y, c = _position()
        me = 4 * x + 2 * y + c
        out_ref[me] = v_ref[...]
        sends = []
        for k in range(1, N_DEV):
            peer = (_flip(x, k & 4), _flip(y, k & 2), _flip(c, k & 1))
            cp = pltpu.make_async_remote_copy(src_ref=v_ref, dst_ref=out_ref.at[me], send_sem=send_sems.at[k - 1],
                                              recv_sem=recv_sems.at[k - 1], device_id=peer, device_id_type=MESH)
            cp.start()
            sends.append(cp)
        for k in range(1, N_DEV):
            peer = (_flip(x, k & 4), _flip(y, k & 2), _flip(c, k & 1))
            slot = 4 * peer[0] + 2 * peer[1] + peer[2]
            pltpu.make_async_remote_copy(src_ref=v_ref, dst_ref=out_ref.at[slot], send_sem=send_sems.at[k - 1],
                                         recv_sem=recv_sems.at[k - 1], device_id=peer, device_id_type=MESH).wait_recv()
        for cp in sends:
            cp.wait_send()

    vm = pl.BlockSpec(memory_space=pltpu.VMEM)
    return _pc(body, name=name, in_specs=[vm], out_specs=vm, out_shape=jax.ShapeDtypeStruct((N_DEV, R, W), F32),
               scratch_shapes=[pltpu.SemaphoreType.DMA((N_DEV - 1,)), pltpu.SemaphoreType.DMA((N_DEV - 1,))])(v)


def _sum_slots(name, a):
    n, R, W = a.shape

    def body(a_ref, o_ref):
        acc = a_ref[0]
        for s in range(1, n):
            acc = acc + a_ref[s]
        o_ref[...] = acc

    return _pc(body, name=name, grid=(1,), in_specs=[pl.BlockSpec((n, R, W), lambda i: (0, 0, 0))],
               out_specs=pl.BlockSpec((R, W), lambda i: (0, 0)), out_shape=jax.ShapeDtypeStruct((R, W), F32))(a)


def _allgather_blocks(name, blocks):
    n = len(blocks)

    def body(*refs):
        b_refs, out_refs, token = refs[:n], refs[n:2 * n], refs[2 * n]
        send_sems, recv_sems, local_sems = refs[2 * n + 1:]
        token[...] = jnp.zeros_like(token)
        x, y, c = _position()
        me, sibling = (x, y, c), (x, y, 1 - c)
        chips = [(1 - x, y), (x, 1 - y), (1 - x, 1 - y)]

        def slot(i, px, py, pc):
            return out_refs[i].at[4 * px + 2 * py + pc]

        def copy(i, k, block, to, src=None):
            return pltpu.make_async_remote_copy(src_ref=slot(i, *block) if src is None else src, dst_ref=slot(i, *block),
                                                send_sem=send_sems.at[k, i], recv_sem=recv_sems.at[k, i],
                                                device_id=to, device_id_type=MESH)

        mine = [pltpu.make_async_copy(b_refs[i], slot(i, *me), local_sems.at[i]) for i in range(n)]
        first = [copy(i, 1 + j, me, (*chip, c), src=b_refs[i]) for i in range(n) for j, chip in enumerate(chips)]
        first += [copy(i, 0, me, sibling, src=b_refs[i]) for i in range(n)]
        for cp in mine + first:
            cp.start()
        passed = []
        for j, chip in enumerate(chips):
            for i in range(n):
                copy(i, 1 + j, (*chip, c), me).wait_recv()
                passed.append(copy(i, 4 + j, (*chip, c), sibling))
                passed[-1].start()
        for i in range(n):
            copy(i, 0, sibling, me).wait_recv()
        for j, chip in enumerate(chips):
            for i in range(n):
                copy(i, 4 + j, (*chip, 1 - c), me).wait_recv()
        for cp in first + passed:
            cp.wait_send()
        for cp in mine:
            cp.wait()

    hbm = pl.BlockSpec(memory_space=pl.ANY)
    return _pc(body, name=name, in_specs=[hbm] * n, out_specs=[hbm] * n + [pl.BlockSpec(memory_space=pltpu.VMEM)],
               out_shape=[jax.ShapeDtypeStruct((N_DEV,) + b.shape, b.dtype) for b in blocks]
               + [jax.ShapeDtypeStruct((8, LANES), F32)],
               scratch_shapes=[pltpu.SemaphoreType.DMA((7, n)), pltpu.SemaphoreType.DMA((7, n)),
                               pltpu.SemaphoreType.DMA((n,))])(*blocks)


def _add_sibling(name, g4, r1, pos):
    n, _, L, W = g4.shape
    tl = _tile(L, 512, 16)

    def body(pos_ref, g_ref, r_ref, tb_ref, own_ref):
        t = g_ref[...] + r_ref[...]
        tb_ref[...] = t.astype(BF16)

        @pl.when(pl.program_id(1) == pos_ref[1])
        def _():
            own_ref[...] = t

    gs = pltpu.PrefetchScalarGridSpec(
        num_scalar_prefetch=1, grid=(L // tl, n),
        in_specs=[pl.BlockSpec((None, None, tl, W), lambda i, k, p: (k, p[0], i, 0)),
                  pl.BlockSpec((None, tl, W), lambda i, k, p: (k, i, 0))],
        out_specs=(pl.BlockSpec((None, tl, W), lambda i, k, p: (k, i, 0)),
                   pl.BlockSpec((tl, W), lambda i, k, p: (i, 0))))
    return _pc(body, name=name, grid_spec=gs,
               out_shape=(jax.ShapeDtypeStruct((n, L, W), BF16), jax.ShapeDtypeStruct((L, W), F32)),
               compiler_params=_sem("parallel", "arbitrary"))(pos, g4, r1)


def _add_chips(name, own, r2):
    L, W = own.shape
    tl = _tile(L, 512, 16)

    def body(o_ref, r_ref, out_ref):
        acc = o_ref[...]
        for j in range(3):
            acc = acc + r_ref[j].astype(F32)
        out_ref[...] = acc

    return _pc(body, name=name, grid=(L // tl,),
               in_specs=[pl.BlockSpec((tl, W), lambda i: (i, 0)), pl.BlockSpec((3, tl, W), lambda i: (0, i, 0))],
               out_specs=pl.BlockSpec((tl, W), lambda i: (i, 0)), out_shape=jax.ShapeDtypeStruct((L, W), F32),
               compiler_params=_sem("parallel"))(own, r2)


_HBM = pl.BlockSpec(memory_space=pltpu.HBM)
_SEM = pl.BlockSpec(memory_space=pltpu.SEMAPHORE)
_EFFECT = pltpu.SideEffectType.DATAFLOW_SIDE_EFFECTING


def _in_hbm(a):
    return pltpu.with_memory_space_constraint(a, pltpu.HBM)


def _after(x, token):
    return x + token[0, 0].astype(x.dtype)


def _copies_start(name, bufs, sem_shape, plan):
    nb = len(bufs)

    def body(*refs):
        for cp in plan(refs[:nb], refs[nb], refs[nb + 1]):
            cp.start()
        refs[-1][...] = jnp.zeros_like(refs[-1])

    out = _pc(body, name=name, in_specs=[_HBM] * nb,
              out_specs=(_SEM, _SEM, *[_HBM] * nb, pl.BlockSpec(memory_space=pltpu.VMEM)),
              out_shape=(pltpu.SemaphoreType.DMA(sem_shape), pltpu.SemaphoreType.DMA(sem_shape),
                         *[pltpu.HBM(b.shape, b.dtype) for b in bufs], jax.ShapeDtypeStruct((8, LANES), F32)),
              input_output_aliases={i: 2 + i for i in range(nb)},
              compiler_params=pltpu.CompilerParams(has_side_effects=_EFFECT))(*[_in_hbm(b) for b in bufs])
    return out[0], out[1], list(out[2:2 + nb]), out[-1]


def _copies_wait(name, bufs, send_sems, recv_sems, plan, after):
    nb = len(bufs)

    def body(*refs):
        for cp in plan(refs[:nb], refs[nb], refs[nb + 1]):
            cp.wait_send()
            cp.wait_recv()

    return list(_pc(body, name=name, in_specs=[_HBM] * nb + [_SEM, _SEM, pl.BlockSpec(memory_space=pl.ANY)],
                    out_specs=[_HBM] * nb, out_shape=[pltpu.HBM(b.shape, b.dtype) for b in bufs],
                    input_output_aliases={i: i for i in range(nb)},
                    compiler_params=pltpu.CompilerParams(has_side_effects=_EFFECT))(*bufs, send_sems, recv_sems, after))


def _plan_gather_chips(n):
    def plan(refs, send_sems, recv_sems):
        x, y, c = _position()
        peers = [(x, y, 1 - c), (1 - x, y, c), (x, 1 - y, c), (1 - x, 1 - y, c)]
        return [pltpu.make_async_remote_copy(src_ref=refs[i], dst_ref=refs[n + i].at[4 * x + 2 * y + c],
                                             send_sem=send_sems.at[k * n + i], recv_sem=recv_sems.at[k * n + i],
                                             device_id=peer, device_id_type=MESH)
                for i in range(n) for k, peer in enumerate(peers)]
    return plan


def _plan_gather_sibling(n):
    def plan(refs, send_sems, recv_sems):
        x, y, c = _position()
        slots = [4 * (1 - x) + 2 * y + c, 4 * x + 2 * (1 - y) + c, 4 * (1 - x) + 2 * (1 - y) + c]
        return [pltpu.make_async_remote_copy(src_ref=refs[i].at[s], dst_ref=refs[i].at[s],
                                             send_sem=send_sems.at[k * n + i], recv_sem=recv_sems.at[k * n + i],
                                             device_id=(x, y, 1 - c), device_id_type=MESH)
                for i in range(n) for k, s in enumerate(slots)]
    return plan


def _plan_reduce_sibling(n):
    def plan(refs, send_sems, recv_sems):
        x, y, c = _position()
        return [pltpu.make_async_remote_copy(src_ref=refs[i].at[k, 1 - c], dst_ref=refs[n + i].at[k],
                                             send_sem=send_sems.at[k * n + i], recv_sem=recv_sems.at[k * n + i],
                                             device_id=(x, y, 1 - c), device_id_type=MESH)
                for i in range(n) for k in range(N_DEV // 2)]
    return plan


def _plan_reduce_chips(n):
    def plan(refs, send_sems, recv_sems):
        x, y, c = _position()
        cps = []
        for i in range(n):
            for j in range(1, 4):
                px, py = _flip(x, j & 2), _flip(y, j & 1)
                sem = (j - 1) * n + i
                cps.append(pltpu.make_async_remote_copy(src_ref=refs[i].at[2 * px + py], dst_ref=refs[n + i].at[j - 1],
                                                        send_sem=send_sems.at[sem], recv_sem=recv_sems.at[sem],
                                                        device_id=(px, py, c), device_id_type=MESH))
        return cps
    return plan


def _heads_major(name, a, after):
    S, W = a.shape
    H = W // HEAD_DIM
    tm = _tile(S, 512, 16)

    def body(a_ref, after_ref, o_ref):
        v = a_ref[...]
        for h in range(H):
            o_ref[h] = v[:, h * HEAD_DIM:(h + 1) * HEAD_DIM]

    return _pc(body, name=name, grid=(S // tm,),
               in_specs=[pl.BlockSpec((tm, W), lambda i: (i, 0)), pl.BlockSpec(memory_space=pl.ANY)],
               out_specs=pl.BlockSpec((H, tm, HEAD_DIM), lambda i: (0, i, 0)),
               out_shape=jax.ShapeDtypeStruct((H, S, HEAD_DIM), a.dtype), compiler_params=_sem("parallel"))(a, after)


def _heads_minor(name, a):
    H, S, _ = a.shape
    tm = _tile(S, 512, 16)

    def body(a_ref, o_ref):
        o_ref[...] = jnp.concatenate([a_ref[h] for h in range(H)], axis=1)

    return _pc(body, name=name, grid=(S // tm,), in_specs=[pl.BlockSpec((H, tm, HEAD_DIM), lambda i: (0, i, 0))],
               out_specs=pl.BlockSpec((tm, H * HEAD_DIM), lambda i: (i, 0)),
               out_shape=jax.ShapeDtypeStruct((S, H * HEAD_DIM), a.dtype), compiler_params=_sem("parallel"))(a)


def kernel(x, mem, attn_norm, attn_w_qkv, attn_q_gain, attn_k_gain, attn_w_o, pool_norm, pool_w, pool_scale, xattn_norm, mem_norm, xattn_w_q, xattn_w_kv, xattn_w_o, ffn_norm, ffn_w_up, ffn_conv_w, ffn_conv_b, ffn_w_down, final_norm, loss_target, m_attn_norm, m_attn_w_qkv, m_attn_q_gain, m_attn_k_gain, m_attn_w_o, m_pool_norm, m_pool_w, m_pool_scale, m_xattn_norm, m_mem_norm, m_xattn_w_q, m_xattn_w_kv, m_xattn_w_o, m_ffn_norm, m_ffn_w_up, m_ffn_conv_w, m_ffn_conv_b, m_ffn_w_down, m_final_norm, v_attn_norm, v_attn_w_qkv, v_attn_q_gain, v_attn_k_gain, v_attn_w_o, v_pool_norm, v_pool_w, v_pool_scale, v_xattn_norm, v_mem_norm, v_xattn_w_q, v_xattn_w_kv, v_xattn_w_o, v_ffn_norm, v_ffn_w_up, v_ffn_conv_w, v_ffn_conv_b, v_ffn_w_down, v_final_norm):
    names = ['attn_norm', 'attn_w_qkv', 'attn_q_gain', 'attn_k_gain', 'attn_w_o', 'pool_norm', 'pool_w', 'pool_scale',
             'xattn_norm', 'mem_norm', 'xattn_w_q', 'xattn_w_kv', 'xattn_w_o', 'ffn_norm', 'ffn_w_up', 'ffn_conv_w',
             'ffn_conv_b', 'ffn_w_down', 'final_norm']
    W = dict(zip(names, (attn_norm, attn_w_qkv, attn_q_gain, attn_k_gain, attn_w_o, pool_norm, pool_w, pool_scale,
                         xattn_norm, mem_norm, xattn_w_q, xattn_w_kv, xattn_w_o, ffn_norm, ffn_w_up, ffn_conv_w,
                         ffn_conv_b, ffn_w_down, final_norm)))
    Mo = dict(zip(names, (m_attn_norm, m_attn_w_qkv, m_attn_q_gain, m_attn_k_gain, m_attn_w_o, m_pool_norm, m_pool_w,
                          m_pool_scale, m_xattn_norm, m_mem_norm, m_xattn_w_q, m_xattn_w_kv, m_xattn_w_o, m_ffn_norm,
                          m_ffn_w_up, m_ffn_conv_w, m_ffn_conv_b, m_ffn_w_down, m_final_norm)))
    Vo = dict(zip(names, (v_attn_norm, v_attn_w_qkv, v_attn_q_gain, v_attn_k_gain, v_attn_w_o, v_pool_norm, v_pool_w,
                          v_pool_scale, v_xattn_norm, v_mem_norm, v_xattn_w_q, v_xattn_w_kv, v_xattn_w_o, v_ffn_norm,
                          v_ffn_w_up, v_ffn_conv_w, v_ffn_conv_b, v_ffn_w_down, v_final_norm)))

    S, D = x.shape[1], x.shape[2]
    n_layers = xattn_norm.shape[0]
    n_up = ffn_w_up.shape[2]
    qkv_w = attn_w_qkv.shape[2] * N_DEV
    n_heads = qkv_w // HEAD_DIM - 2 * N_KV_HEADS
    n_rot = (n_heads + N_KV_HEADS) * HEAD_DIM // LANES
    group_w = pool_w.shape[3]
    xs, mems, tgt = x[0], mem[0], loss_target[0]
    xi, yi, ci = _position()
    dev = 4 * xi + 2 * yi + ci
    pos = jnp.stack([ci, 2 * xi + yi]).astype(jnp.int32)

    layers = range(n_layers)
    n_groups = pool_w.shape[1]
    small_vec = jnp.concatenate([pool_norm.reshape(-1), pool_scale.reshape(-1), ffn_conv_w.reshape(-1)])
    small_rows = _round_up(-(-small_vec.size // PACK_W), 8)
    small_vec = jnp.pad(small_vec, (0, small_rows * PACK_W - small_vec.size)).reshape(small_rows, PACK_W)
    w_qkv, w_o, small, attn_token = _allgather_blocks(
        "allgather_attn", [attn_w_qkv[0].astype(BF16), attn_w_o[0].astype(BF16), small_vec])
    small = small.reshape(N_DEV, -1)
    w_qkv = w_qkv.transpose(1, 0, 2).reshape(D, qkv_w)
    w_o = w_o.reshape(-1, D)
    blocks = [pool_w.reshape(-1, group_w)] + [xattn_w_q[l] for l in layers] + [xattn_w_kv.reshape(n_layers * D, -1)]
    blocks += [xattn_w_o[l] for l in layers] + [ffn_w_up.reshape(n_layers * D, n_up)] + [ffn_w_down[l] for l in layers]
    blocks = [b.astype(BF16) for b in blocks]
    blocks[0] = _after(blocks[0], attn_token)
    n_blk = len(blocks)
    lands = [lax.dynamic_update_index_in_dim(lax.empty((N_DEV,) + b.shape, BF16), b, dev, 0) for b in blocks]
    plan_chips, plan_sibling = _plan_gather_chips(n_blk), _plan_gather_sibling(n_blk)
    gather_sems = _copies_start("gather_chips_start", blocks + lands, (4 * n_blk,), plan_chips)
    attn_norm_late = _after(attn_norm, gather_sems[3])

    d_sh = pool_norm.shape[1]
    pool_norm_f = small[:, :d_sh].reshape(1, D)
    pool_scale_f = small[:, d_sh:2 * d_sh].reshape(1, D)
    conv_w_f = small[:, 2 * d_sh:2 * d_sh + ffn_conv_w.size].reshape(N_DEV, n_layers, 3, n_up)
    conv_b_f = ffn_conv_b.reshape(n_layers, N_DEV, 1, n_up)

    cos, sin = _rope_tables(S)
    bd = _head_mean_matrix()
    pad_w = qkv_w - (n_heads + N_KV_HEADS) * HEAD_DIM
    qk_gain = jnp.concatenate([jnp.tile(attn_q_gain[0], n_heads), jnp.tile(attn_k_gain[0], N_KV_HEADS),
                               jnp.ones((pad_w,), F32)]).reshape(1, qkv_w)
    qk_scale = jnp.concatenate([jnp.full((n_heads * HEAD_DIM,), HEAD_DIM ** -0.5, F32),
                                jnp.ones((qkv_w - n_heads * HEAD_DIM,), F32)]).reshape(1, qkv_w)

    saved = []

    def xattn_ffn_fwd(l, xin, hx=None):
        if hx is None:
            hx = _rmsnorm(f"xattn_norm{l}", xin, xattn_norm[l:l + 1], BF16)
        memn = _rmsnorm(f"mem_norm{l}", mems, mem_norm[l:l + 1], BF16)
        qx = _mm_nn(f"xattn_q{l}", hx, w_xq[l], BF16)
        kv = _mm_nn_bs(f"xattn_kv{l}", memn, w_xkv, BF16, l)
        ox = _xattn_fwd(f"xattn_fwd{l}", qx, kv)
        x2 = _mm_nn(f"xattn_o{l}", ox, w_xo[l], F32, res=xin)
        hf = _rmsnorm(f"ffn_norm{l}", x2, ffn_norm[l:l + 1], BF16)
        cw = conv_w_f[:, l].reshape(2, N_DEV // 2, 3, n_up)
        cb = conv_b_f[l].reshape(2, N_DEV // 2, 1, n_up)
        u, act = _ffn_up_act(f"ffn_up_act{l}", hf, w_up.reshape(2, N_DEV // 2, n_layers * D, n_up), cw, cb, l)
        x3 = _mm_nn_as(f"ffn_down{l}", act, w_down[l], F32, x2)
        saved.append(dict(xin=xin, hx=hx, memn=memn, qx=qx, kv=kv, ox=ox, x2=x2, hf=hf, u=u, cw=cw, cb=cb, act=act))
        return x3

    h0 = _rmsnorm("attn_norm", xs, attn_norm_late, BF16)
    qkv = _mm_nn("attn_qkv", h0, w_qkv, F32)
    qkr = _qk_rope("qk_rope", qkv, qk_gain, qk_scale, cos, sin, bd, n_rot)
    o_hm, lse = _attn_fwd("attn_fwd", qkr, n_heads)
    o_att = _heads_minor("attn_heads_minor", o_hm)
    arrived = _copies_wait("gather_chips_wait", gather_sems[2], gather_sems[0], gather_sems[1], plan_chips, o_att)
    pass_sems = _copies_start("gather_sibling_start", arrived[n_blk:], (3 * n_blk,), plan_sibling)
    x1 = _mm_nn("attn_o", o_att, _after(w_o, pass_sems[3]), F32, res=xs)
    hx0 = _rmsnorm("xattn_norm0", x1, xattn_norm[0:1], BF16)
    gathered = iter(_copies_wait("gather_sibling_wait", pass_sems[2], pass_sems[0], pass_sems[1], plan_sibling, hx0))
    w_pool = (next(gathered).reshape(N_DEV, n_groups, -1, group_w).transpose(1, 0, 2, 3)
              .reshape(n_groups, group_w, group_w))
    w_xq = [next(gathered).reshape(D, D) for l in layers]
    w_xkv = next(gathered)
    w_xo = [next(gathered).reshape(D, D) for l in layers]
    w_up = next(gathered)
    w_down = [next(gathered).reshape(-1, D) for l in layers]
    x3 = xattn_ffn_fwd(0, x1, hx0)
    hp = _rmsnorm("pool_norm", x3, pool_norm_f, F32)
    mixed = _pool_window("pool_window", hp, group_w, False, BF16)
    x4 = _pool_proj("pool_proj", mixed, w_pool, pool_scale_f, x3)
    x6 = xattn_ffn_fwd(1, x4)

    G = {}
    g, d_final, lvec = _loss_head("loss_head", x6, final_norm.reshape(1, D), tgt)
    G['final_norm'] = d_final.reshape(D)
    loss_part = (0.5 * jnp.sum(lvec) / D).reshape(1)

    d_xn, d_mn, d_fn, d_xq, d_xkv, d_xo, d_up, d_cw, d_cb, d_down = ([None] * n_layers for _ in range(10))

    def xattn_ffn_bwd(l, g, conv_b_late=None, after_act=None):
        sv = saved[l]
        d_down[l] = _mm_tn_as(f"ffn_down_dw{l}", sv['act'], g, F32)
        du, st = _ffn_act_bwd(f"ffn_act_bwd{l}", sv['u'], g, w_down[l], sv['cw'],
                              sv['cb'] if conv_b_late is None else conv_b_late)
        du = du.reshape(N_DEV, S, n_up)
        ffn_gain = ffn_norm[l:l + 1] if after_act is None else _after(ffn_norm[l:l + 1], after_act(du))
        st = st.reshape(N_DEV, 8, n_up)
        d_cw[l], d_cb[l] = st[:, 0:3], st[:, 3].reshape(-1)
        d_up[l] = _mm_tn_bs(f"ffn_up_dw{l}", sv['hf'], du, F32)
        g, d_fn[l] = _mm_nt_abs_norm_bwd(f"ffn_up_dx_norm_bwd{l}", du, w_up, sv['x2'], ffn_gain, g, l)
        d_xo[l] = _mm_tn(f"xattn_o_dw{l}", sv['ox'], g, F32)
        do = _mm_nt(f"xattn_o_dx{l}", g, w_xo[l], BF16)
        dq, dkv = _xattn_bwd(f"xattn_bwd{l}", sv['qx'], sv['kv'], do)
        d_xq[l] = _mm_tn(f"xattn_q_dw{l}", sv['hx'], dq, F32)
        d_xkv[l] = _mm_tn_bs(f"xattn_kv_dw{l}", sv['memn'], dkv, F32)
        dmemn = _mm_nt_abs(f"xattn_kv_dx{l}", dkv, w_xkv, D, F32, l)
        _, d_mn[l] = _rmsnorm_bwd(f"mem_norm_bwd{l}", mems, mem_norm[l:l + 1], dmemn)
        g, d_xn[l] = _mm_nt_norm_bwd(f"xattn_q_dx_norm_bwd{l}", dq, w_xq[l], sv['xin'], xattn_norm[l:l + 1], g)
        return g

    def reduce_start(tag, bufs):
        n = len(bufs)
        g4s = [b.reshape((N_DEV // 2, 2) + b.shape[1:]) for b in bufs]
        lands = [lax.empty((N_DEV // 2,) + b.shape[1:], F32) for b in bufs]
        plan = _plan_reduce_sibling(n)
        return (n, plan) + _copies_start(f"reduce_sibling_start_{tag}", g4s + lands, (N_DEV // 2 * n,), plan)

    def reduce_between(tag, state, after):
        n, plan, send_sems, recv_sems, thru, _ = state
        got = _copies_wait(f"reduce_sibling_wait_{tag}", thru, send_sems, recv_sems, plan, after)
        sums = [_add_sibling(f"reduce_add_sibling_{tag}{i}", got[i], got[n + i], pos) for i in range(n)]
        lands = [lax.empty((3,) + tb.shape[1:], BF16) for tb, _ in sums]
        plan = _plan_reduce_chips(n)
        return (n, plan, [own for _, own in sums]) + _copies_start(f"reduce_chips_start_{tag}",
                                                                    [tb for tb, _ in sums] + lands, (3 * n,), plan)

    def reduce_finish(tag, state, after):
        n, plan, owns, send_sems, recv_sems, thru, _ = state
        got = _copies_wait(f"reduce_chips_wait_{tag}", thru, send_sems, recv_sems, plan, after)
        return [_add_chips(f"reduce_add_chips_{tag}{i}", owns[i], got[n + i]) for i in range(n)]

    def layer_bufs(l):
        return [d_xq[l].reshape(N_DEV, -1, D), d_xkv[l], d_xo[l].reshape(N_DEV, -1, D), d_up[l],
                d_down[l].reshape(N_DEV, -1, D)]

    g = xattn_ffn_bwd(1, g)
    d_mixed, d_pool_w, d_pool_scale = _pool_proj_bwd("pool_proj_bwd", g, mixed, w_pool, pool_scale_f)
    dhp = _pool_window("pool_window_bwd", d_mixed, group_w, True, F32)
    g, d_pool_norm = _rmsnorm_bwd("pool_norm_bwd", x3, pool_norm_f, dhp, g)
    upper = reduce_start("upper", [d_pool_w.reshape(n_groups, N_DEV, -1, group_w).transpose(1, 0, 2, 3)
                                   .reshape(N_DEV, -1, group_w)] + layer_bufs(1))
    between = []

    def upper_between(du):
        between.append(reduce_between("upper", upper, du))
        return between[0][-1]

    g = xattn_ffn_bwd(0, g, _after(saved[0]['cb'], upper[-1]), upper_between)
    lower = reduce_start("lower", layer_bufs(0))
    d_wo = _mm_tn("attn_o_dw", o_att, g, F32)
    do = _mm_nt("attn_o_dx", g, _after(w_o, lower[-1]), BF16)
    lower = reduce_between("lower", lower, do)
    do_hm = _heads_major("attn_heads_major", do, lower[-1])
    dq_hm, dk_hm, dv_hm = _attn_bwd("attn_bwd", qkr, o_hm, lse, do_hm)
    red_lower = reduce_finish("lower", lower, dq_hm)
    d_qkv, d_gain = _qk_rope_bwd("qk_rope_bwd", dq_hm, dk_hm, dv_hm, qkv, qk_gain, qk_scale, cos, sin, bd)
    red_upper = reduce_finish("upper", between[0], d_qkv)
    d_wqkv = _mm_tn("attn_qkv_dw", h0, d_qkv, F32)
    last = reduce_start("last", [d_wqkv.reshape(D, N_DEV, -1).transpose(1, 0, 2), d_wo.reshape(N_DEV, -1, D)])
    last = reduce_between("last", last, d_wqkv)
    grad_x, d_attn_norm = _mm_nt_norm_bwd("attn_qkv_dx_norm_bwd", d_qkv, w_qkv, xs, attn_norm, g, last[-1])
    G['pool_w'] = red_upper[0].reshape(pool_w.shape)
    per_layer = [red_lower, red_upper[1:]]
    for i, n in enumerate(['xattn_w_q', 'xattn_w_kv', 'xattn_w_o', 'ffn_w_up', 'ffn_w_down']):
        G[n] = jnp.stack([per_layer[l][i] for l in layers])

    hq = n_heads * HEAD_DIM
    small_g = {'attn_norm': d_attn_norm, 'attn_q_gain': d_gain[0, :hq].reshape(n_heads, HEAD_DIM).sum(0),
               'attn_k_gain': d_gain[0, hq:hq + N_KV_HEADS * HEAD_DIM].reshape(N_KV_HEADS, HEAD_DIM).sum(0),
               'pool_norm': d_pool_norm, 'pool_scale': d_pool_scale,
               'xattn_norm': jnp.concatenate(d_xn), 'mem_norm': jnp.concatenate(d_mn), 'ffn_norm': jnp.concatenate(d_fn),
               'ffn_conv_w': jnp.stack(d_cw, axis=1), 'ffn_conv_b': jnp.stack(d_cb)}
    order = list(small_g)
    flat = jnp.concatenate([loss_part] + [small_g[n].reshape(-1) for n in order] + [G['final_norm']])
    ar_rows = _round_up(-(-flat.size // PACK_W), 8)
    flat = jnp.pad(flat, (0, ar_rows * PACK_W - flat.size)).reshape(ar_rows, PACK_W)
    summed = _sum_slots("allreduce_sum", _allgather_small("allreduce_gather", flat)).reshape(-1)
    loss = summed[0]
    red_last = reduce_finish("last", last, summed)
    G['attn_w_qkv'], G['attn_w_o'] = red_last[0][None], red_last[1][None]
    at = 1
    for n in order + ['final_norm']:
        size = G['final_norm'].size if n == 'final_norm' else small_g[n].size
        piece = summed[at:at + size]
        at += size
        if n in ('pool_norm', 'pool_scale'):
            piece = lax.dynamic_slice(piece, (dev * d_sh,), (d_sh,))
        elif n == 'ffn_conv_w':
            piece = lax.dynamic_index_in_dim(piece.reshape(N_DEV, n_layers, 3, n_up), dev, 0, keepdims=False)
        G[n] = piece.reshape(W[n].shape)

    deltas, new_m, new_v = [], [], []
    for n in names:
        d, nm, nv = _adamw(f"adamw_{n}", W[n], G[n], Mo[n], Vo[n])
        deltas.append(d)
        new_m.append(nm)
        new_v.append(nv)
    return (loss, grad_x[None], *[G[n] for n in names], *deltas, *new_m, *new_v)
```

```python
import jax
import jax.numpy as jnp
from jax import lax
from jax.experimental import pallas as pl
from jax.experimental.pallas import tpu as pltpu

F32 = jnp.float32
BF16 = jnp.bfloat16
MESH = pl.DeviceIdType.MESH

N_DEV = 8
EPS = 1e-6
HEAD_DIM = 64
N_KV_HEADS = 4
X_HEADS = 4
GRID_W = 64
ROPE_THETA = 10000.0
ROPE_PAIRS = HEAD_DIM // 4
POOL_WINDOWS = (2, 4, 8, 16)
POOL_PAD = 16
KEY_CHUNK = 1024
MM_ROWS = 1024
REDUCE_ROWS = 2048
FFN_HALO = 16
LANES = 128
PACK_W = 1024
ADAM_LR, ADAM_B1, ADAM_B2, ADAM_EPS, ADAM_WD, ADAM_STEP = 0.001, 0.9, 0.999, 1e-08, 0.01, 10

_NN = (((1,), (0,)), ((), ()))
_NT = (((1,), (1,)), ((), ()))
_TN = (((0,), (0,)), ((), ()))


def _pc(body, *, name, **kw):
    return pl.pallas_call(body, name=name, **kw)


def _sem(*kinds):
    return pltpu.CompilerParams(dimension_semantics=kinds)


def _tile(n, pref, mult):
    best = None
    for t in range(mult, min(n, pref) + 1, mult):
        if n % t == 0:
            best = t
    return n if best is None else best


def _round_up(n, m):
    return (n + m - 1) // m * m


def _mm_call(name, a, b, dims, grid, a_spec, b_spec, o_spec, out_shape, kaxis, res=None, res_spec=None):
    nk = grid[kaxis]
    acc_shape = tuple(d for d in o_spec.block_shape if d is not None)
    in_place = out_shape.dtype == F32
    use_scratch = nk > 1 and not in_place

    def body(*refs):
        refs = list(refs)
        acc = refs.pop() if use_scratch else None
        a_ref, b_ref = refs[:2]
        r_ref = refs[2] if res is not None else None
        o_ref = refs[-1]
        prod = lax.dot_general(a_ref[...].astype(BF16), b_ref[...].astype(BF16), dims, preferred_element_type=F32)
        if nk == 1:
            if r_ref is not None:
                prod = prod + r_ref[...]
            o_ref[...] = prod.astype(o_ref.dtype)
            return
        k = pl.program_id(kaxis)
        tgt = o_ref if in_place else acc

        @pl.when(k == 0)
        def _():
            tgt[...] = prod + r_ref[...] if (in_place and r_ref is not None) else prod

        @pl.when(k > 0)
        def _():
            tgt[...] += prod

        if not in_place:
            @pl.when(k == nk - 1)
            def _():
                r = acc[...]
                if r_ref is not None:
                    r = r + r_ref[...]
                o_ref[...] = r.astype(o_ref.dtype)

    sem = tuple("arbitrary" if ax == kaxis else "parallel" for ax in range(len(grid)))
    ins = [a, b] if res is None else [a, b, res]
    specs = [a_spec, b_spec] if res is None else [a_spec, b_spec, res_spec]
    return _pc(body, name=name, grid=grid, in_specs=specs, out_specs=o_spec, out_shape=out_shape,
               scratch_shapes=[pltpu.VMEM(acc_shape, F32)] if use_scratch else [],
               compiler_params=_sem(*sem))(*ins)


def _mm_nn(name, a, b, out_dtype, res=None):
    M, K = a.shape
    N = b.shape[1]
    tm, tn, tk = _tile(M, MM_ROWS, 16), _tile(N, 1024, LANES), _tile(K, 1024, LANES)
    return _mm_call(name, a, b, _NN, (M // tm, N // tn, K // tk),
                    pl.BlockSpec((tm, tk), lambda i, j, k: (i, k)),
                    pl.BlockSpec((tk, tn), lambda i, j, k: (k, j)),
                    pl.BlockSpec((tm, tn), lambda i, j, k: (i, j)),
                    jax.ShapeDtypeStruct((M, N), out_dtype), 2, res,
                    pl.BlockSpec((tm, tn), lambda i, j, k: (i, j)))


def _mm_nt(name, a, b, out_dtype):
    M, K = a.shape
    N = b.shape[0]
    tm, tn, tk = _tile(M, MM_ROWS, 16), _tile(N, 1024, LANES), _tile(K, 1024, LANES)
    return _mm_call(name, a, b, _NT, (M // tm, N // tn, K // tk),
                    pl.BlockSpec((tm, tk), lambda i, j, k: (i, k)),
                    pl.BlockSpec((tn, tk), lambda i, j, k: (j, k)),
                    pl.BlockSpec((tm, tn), lambda i, j, k: (i, j)),
                    jax.ShapeDtypeStruct((M, N), out_dtype), 2)


def _mm_tn(name, a, b, out_dtype):
    R, M = a.shape
    N = b.shape[1]
    tm, tn, tr = _tile(M, 1024, LANES), _tile(N, 1024, LANES), _tile(R, REDUCE_ROWS, 16)
    return _mm_call(name, a, b, _TN, (M // tm, N // tn, R // tr),
                    pl.BlockSpec((tr, tm), lambda i, j, k: (k, i)),
                    pl.BlockSpec((tr, tn), lambda i, j, k: (k, j)),
                    pl.BlockSpec((tm, tn), lambda i, j, k: (i, j)),
                    jax.ShapeDtypeStruct((M, N), out_dtype), 2)


def _mm_nn_bs(name, a, b, out_dtype, layer=0):
    M, K = a.shape
    J, _, n = b.shape
    tm, tk = _tile(M, MM_ROWS, 16), _tile(K, 1024, LANES)
    first = layer * (K // tk)
    return _mm_call(name, a, b, _NN, (J, M // tm, K // tk),
                    pl.BlockSpec((tm, tk), lambda j, i, k: (i, k)),
                    pl.BlockSpec((None, tk, n), lambda j, i, k: (j, first + k, 0)),
                    pl.BlockSpec((None, tm, n), lambda j, i, k: (j, i, 0)),
                    jax.ShapeDtypeStruct((J, M, n), out_dtype), 2)


def _mm_nn_as(name, a, b, out_dtype, res):
    J, M, n = a.shape
    N = b.shape[1]
    tm, tn = _tile(M, MM_ROWS, 16), _tile(N, 1024, LANES)
    return _mm_call(name, a, b, _NN, (M // tm, N // tn, J),
                    pl.BlockSpec((None, tm, n), lambda i, j, k: (k, i, 0)),
                    pl.BlockSpec((n, tn), lambda i, j, k: (k, j)),
                    pl.BlockSpec((tm, tn), lambda i, j, k: (i, j)),
                    jax.ShapeDtypeStruct((M, N), out_dtype), 2, res,
                    pl.BlockSpec((tm, tn), lambda i, j, k: (i, j)))


def _mm_tn_as(name, a, b, out_dtype):
    J, R, n = a.shape
    N = b.shape[1]
    tn, tr = _tile(N, 1024, LANES), _tile(R, REDUCE_ROWS, 16)
    return _mm_call(name, a, b, _TN, (J, N // tn, R // tr),
                    pl.BlockSpec((None, tr, n), lambda j, jn, k: (j, k, 0)),
                    pl.BlockSpec((tr, tn), lambda j, jn, k: (k, jn)),
                    pl.BlockSpec((n, tn), lambda j, jn, k: (j, jn)),
                    jax.ShapeDtypeStruct((J * n, N), out_dtype), 2)


def _mm_nt_abs(name, a, b, N, out_dtype, layer=0):
    J, M, n = a.shape
    tm, tn = _tile(M, MM_ROWS, 16), _tile(N, 1024, LANES)
    first = layer * (N // tn)
    return _mm_call(name, a, b, _NT, (M // tm, N // tn, J),
                    pl.BlockSpec((None, tm, n), lambda i, j, k: (k, i, 0)),
                    pl.BlockSpec((None, tn, n), lambda i, j, k: (k, first + j, 0)),
                    pl.BlockSpec((tm, tn), lambda i, j, k: (i, j)),
                    jax.ShapeDtypeStruct((M, N), out_dtype), 2)


def _mm_tn_bs(name, a, b, out_dtype):
    R, M = a.shape
    J, _, n = b.shape
    tm, tr = _tile(M, 1024, LANES), _tile(R, REDUCE_ROWS, 16)
    return _mm_call(name, a, b, _TN, (J, M // tm, R // tr),
                    pl.BlockSpec((tr, tm), lambda j, i, k: (k, i)),
                    pl.BlockSpec((None, tr, n), lambda j, i, k: (j, k, 0)),
                    pl.BlockSpec((None, tm, n), lambda j, i, k: (j, i, 0)),
                    jax.ShapeDtypeStruct((J, M, n), out_dtype), 2)


def _rmsnorm(name, x, g, out_dtype):
    R, D = x.shape
    tm = _tile(R, 512, 16)

    def body(x_ref, g_ref, o_ref):
        xv = x_ref[...]
        r = lax.rsqrt(jnp.mean(xv * xv, axis=-1, keepdims=True) + EPS)
        o_ref[...] = (xv * r * g_ref[...]).astype(o_ref.dtype)

    return _pc(body, name=name, grid=(R // tm,),
               in_specs=[pl.BlockSpec((tm, D), lambda i: (i, 0)), pl.BlockSpec((1, D), lambda i: (0, 0))],
               out_specs=pl.BlockSpec((tm, D), lambda i: (i, 0)),
               out_shape=jax.ShapeDtypeStruct((R, D), out_dtype), compiler_params=_sem("parallel"))(x, g)


def _rmsnorm_bwd(name, x, g, dh, dres=None):
    R, D = x.shape
    tm = _tile(R, 512, 16)

    def body(*refs):
        if dres is None:
            x_ref, g_ref, dh_ref, dx_ref, dg_ref = refs
            dres_ref = None
        else:
            x_ref, g_ref, dh_ref, dres_ref, dx_ref, dg_ref = refs
        xv = x_ref[...]
        r = lax.rsqrt(jnp.mean(xv * xv, axis=-1, keepdims=True) + EPS)
        xh = xv * r
        dhv = dh_ref[...].astype(F32)

        @pl.when(pl.program_id(0) == 0)
        def _():
            dg_ref[...] = jnp.zeros_like(dg_ref)

        dg_ref[...] += jnp.sum(dhv * xh, axis=0, keepdims=True)
        dxh = dhv * g_ref[...]
        dx = r * (dxh - xh * jnp.mean(dxh * xh, axis=-1, keepdims=True))
        if dres_ref is not None:
            dx = dx + dres_ref[...]
        dx_ref[...] = dx

    row = pl.BlockSpec((tm, D), lambda i: (i, 0))
    vec = pl.BlockSpec((1, D), lambda i: (0, 0))
    ins = [x, g, dh] + ([] if dres is None else [dres])
    specs = [row, vec, row] + ([] if dres is None else [row])
    return _pc(body, name=name, grid=(R // tm,), in_specs=specs, out_specs=(row, vec),
               out_shape=(jax.ShapeDtypeStruct((R, D), F32), jax.ShapeDtypeStruct((1, D), F32)),
               compiler_params=_sem("arbitrary"))(*ins)


def _mm_norm_bwd(name, a, b, grid, a_spec, b_spec, tm, x, gain, dres, after=None):
    M, D = x.shape
    nk = grid[2]

    def body(*refs):
        a_ref, b_ref, x_ref, g_ref, r_ref = refs[:5]
        dx_ref, dg_ref = refs[-3 if nk > 1 else -2:][:2]
        acc = refs[-1] if nk > 1 else None
        i, k = pl.program_id(0), pl.program_id(2)
        prod = lax.dot_general(a_ref[...].astype(BF16), b_ref[...].astype(BF16), _NT, preferred_element_type=F32)

        @pl.when((i == 0) & (k == 0))
        def _():
            dg_ref[...] = jnp.zeros_like(dg_ref)

        def finish(dh):
            xv = x_ref[...]
            r = lax.rsqrt(jnp.mean(xv * xv, axis=-1, keepdims=True) + EPS)
            xh = xv * r
            dg_ref[...] += jnp.sum(dh * xh, axis=0, keepdims=True)
            dxh = dh * g_ref[...]
            dx_ref[...] = r * (dxh - xh * jnp.mean(dxh * xh, axis=-1, keepdims=True)) + r_ref[...]

        if nk == 1:
            finish(prod)
            return

        @pl.when(k == 0)
        def _():
            acc[...] = prod

        @pl.when(k > 0)
        def _():
            acc[...] += prod

        @pl.when(k == nk - 1)
        def _():
            finish(acc[...])

    row = pl.BlockSpec((tm, D), lambda i, j, k: (i, 0))
    vec = pl.BlockSpec((1, D), lambda i, j, k: (0, 0))
    ins = [a, b, x, gain, dres] + ([] if after is None else [after])
    specs = [a_spec, b_spec, row, vec, row] + ([] if after is None else [pl.BlockSpec(memory_space=pl.ANY)])
    return _pc(body, name=name, grid=grid, in_specs=specs, out_specs=(row, vec),
               out_shape=(jax.ShapeDtypeStruct((M, D), F32), jax.ShapeDtypeStruct((1, D), F32)),
               scratch_shapes=[pltpu.VMEM((tm, D), F32)] if nk > 1 else [],
               compiler_params=_sem("arbitrary", "arbitrary", "arbitrary"))(*ins)


def _mm_nt_norm_bwd(name, a, b, x, gain, dres, after=None):
    M, K = a.shape
    D = b.shape[0]
    tm, tk = _tile(M, MM_ROWS, 16), _tile(K, 1024, LANES)
    return _mm_norm_bwd(name, a, b, (M // tm, 1, K // tk),
                        pl.BlockSpec((tm, tk), lambda i, j, k: (i, k)),
                        pl.BlockSpec((D, tk), lambda i, j, k: (0, k)), tm, x, gain, dres, after)


def _mm_nt_abs_norm_bwd(name, a, b, x, gain, dres, layer=0):
    J, M, n = a.shape
    D = x.shape[1]
    tm = _tile(M, MM_ROWS, 16)
    return _mm_norm_bwd(name, a, b, (M // tm, 1, J),
                        pl.BlockSpec((None, tm, n), lambda i, j, k: (k, i, 0)),
                        pl.BlockSpec((None, D, n), lambda i, j, k: (k, layer, 0)), tm, x, gain, dres)


def _loss_head(name, x, g, tgt):
    R, D = x.shape
    tm = _tile(R, 512, 16)

    def body(x_ref, g_ref, t_ref, dx_ref, dg_ref, l_ref):
        xv = x_ref[...]
        r = lax.rsqrt(jnp.mean(xv * xv, axis=-1, keepdims=True) + EPS)
        xh = xv * r
        err = xh * g_ref[...] - t_ref[...]

        @pl.when(pl.program_id(0) == 0)
        def _():
            dg_ref[...] = jnp.zeros_like(dg_ref)
            l_ref[...] = jnp.zeros_like(l_ref)

        l_ref[...] += jnp.sum(err * err, axis=0, keepdims=True)
        dy = err * (1.0 / D)
        dg_ref[...] += jnp.sum(dy * xh, axis=0, keepdims=True)
        dxh = dy * g_ref[...]
        dx_ref[...] = r * (dxh - xh * jnp.mean(dxh * xh, axis=-1, keepdims=True))

    row = pl.BlockSpec((tm, D), lambda i: (i, 0))
    vec = pl.BlockSpec((1, D), lambda i: (0, 0))
    return _pc(body, name=name, grid=(R // tm,), in_specs=[row, vec, row], out_specs=(row, vec, vec),
               out_shape=(jax.ShapeDtypeStruct((R, D), F32), jax.ShapeDtypeStruct((1, D), F32),
                          jax.ShapeDtypeStruct((1, D), F32)),
               compiler_params=_sem("arbitrary"))(x, g, tgt)


def _rope_tables(S):
    n_rows = S // GRID_W
    row = jnp.repeat(jnp.arange(n_rows, dtype=F32), GRID_W)
    col = jnp.tile(jnp.arange(GRID_W, dtype=F32), n_rows)
    inv_freq = ROPE_THETA ** (-jnp.arange(ROPE_PAIRS, dtype=F32) / ROPE_PAIRS)
    ang = jnp.stack([row[:, None] * inv_freq, col[:, None] * inv_freq], axis=1)
    cos, sin = jnp.cos(ang), jnp.sin(ang)
    c = jnp.broadcast_to(cos[:, :, None, :], (S, 2, 2, ROPE_PAIRS)).reshape(S, HEAD_DIM)
    s = jnp.stack([-sin, sin], axis=2).reshape(S, HEAD_DIM)
    reps = LANES // HEAD_DIM
    return jnp.tile(c, (1, reps)), jnp.tile(s, (1, reps))


def _head_mean_matrix():
    h = jnp.arange(LANES) // HEAD_DIM
    m = jnp.where(h[:, None] == h[None, :], 1.0 / HEAD_DIM, 0.0).astype(BF16)
    return jnp.concatenate([m, m], axis=0)


def _head_mean(v, bd):
    hi = v.astype(BF16)
    lo = (v - hi.astype(F32)).astype(BF16)
    return jnp.dot(jnp.concatenate([hi, lo], axis=1), bd, preferred_element_type=F32)


def _swap_halves(y):
    lane = lax.broadcasted_iota(jnp.int32, y.shape, 1)
    return jnp.where(lane % 32 < 16, pltpu.roll(y, LANES - 16, 1), pltpu.roll(y, 16, 1))


def _qk_rope(name, qkv, gain, scale, cos, sin, bd, n_rot):
    S, W = qkv.shape
    tm = _tile(S, 2048, 16)
    per = LANES // HEAD_DIM

    def body(x_ref, g_ref, s_ref, c_ref, sn_ref, bd_ref, o_ref):
        j = pl.program_id(1)
        xv = x_ref[...]

        def put(v):
            for h in range(per):
                o_ref[h] = v[:, h * HEAD_DIM:(h + 1) * HEAD_DIM].astype(BF16)

        @pl.when(j < n_rot)
        def _():
            ms = _head_mean(xv * xv, bd_ref[...])
            y = xv * lax.rsqrt(ms + EPS) * g_ref[...] * s_ref[...]
            put(y * c_ref[...] + _swap_halves(y) * sn_ref[...])

        @pl.when(j >= n_rot)
        def _():
            put(xv)

    blk = pl.BlockSpec((tm, LANES), lambda i, j: (i, j))
    vec = pl.BlockSpec((1, LANES), lambda i, j: (0, j))
    tab = pl.BlockSpec((tm, LANES), lambda i, j: (i, 0))
    return _pc(body, name=name, grid=(S // tm, W // LANES),
               in_specs=[blk, vec, vec, tab, tab, pl.BlockSpec((2 * LANES, LANES), lambda i, j: (0, 0))],
               out_specs=pl.BlockSpec((per, tm, HEAD_DIM), lambda i, j: (j, i, 0)),
               out_shape=jax.ShapeDtypeStruct((W // HEAD_DIM, S, HEAD_DIM), BF16),
               compiler_params=_sem("parallel", "parallel"))(qkv, gain, scale, cos, sin, bd)


def _qk_rope_bwd(name, dq, dk, dv, qkv, gain, scale, cos, sin, bd):
    S, W = qkv.shape
    tm = _tile(S, 2048, 16)
    per = LANES // HEAD_DIM
    nq, nk, nv = dq.shape[0] // per, dk.shape[0] // per, dv.shape[0] // per
    n_rot = nq + nk

    def body(dq_ref, dk_ref, dv_ref, x_ref, g_ref, s_ref, c_ref, sn_ref, bd_ref, dx_ref, dg_ref):
        j, i = pl.program_id(0), pl.program_id(1)

        @pl.when(i == 0)
        def _():
            dg_ref[...] = jnp.zeros_like(dg_ref)

        def rotate_back(d_ref):
            dv = jnp.concatenate([d_ref[h] for h in range(per)], axis=1)
            xv = x_ref[...]
            ms = _head_mean(xv * xv, bd_ref[...])
            r = lax.rsqrt(ms + EPS)
            z = xv * r
            dy = (dv * c_ref[...] - _swap_halves(dv) * sn_ref[...]) * s_ref[...]
            dg_ref[...] += jnp.sum(dy * z, axis=0, keepdims=True)
            dz = dy * g_ref[...]
            mz = _head_mean(dz * z, bd_ref[...])
            dx_ref[...] = (r * (dz - z * mz)).astype(BF16)

        @pl.when(j < nq)
        def _():
            rotate_back(dq_ref)

        @pl.when((j >= nq) & (j < n_rot))
        def _():
            rotate_back(dk_ref)

        @pl.when(j >= n_rot)
        def _():
            dx_ref[...] = jnp.concatenate([dv_ref[h] for h in range(per)], axis=1).astype(BF16)

    def part(first, count):
        return pl.BlockSpec((per, tm, HEAD_DIM), lambda j, i: (jnp.clip(j - first, 0, count - 1), i, 0))

    blk = pl.BlockSpec((tm, LANES), lambda j, i: (i, j))
    vec = pl.BlockSpec((1, LANES), lambda j, i: (0, j))
    tab = pl.BlockSpec((tm, LANES), lambda j, i: (i, 0))
    return _pc(body, name=name, grid=(W // LANES, S // tm),
               in_specs=[part(0, nq), part(nq, nk), part(n_rot, nv), blk, vec, vec, tab, tab,
                         pl.BlockSpec((2 * LANES, LANES), lambda j, i: (0, 0))],
               out_specs=(blk, vec),
               out_shape=(jax.ShapeDtypeStruct((S, W), BF16), jax.ShapeDtypeStruct((1, W), F32)),
               compiler_params=_sem("parallel", "arbitrary"))(dq, dk, dv, qkv, gain, scale, cos, sin, bd)


def _softmax_rows(s):
    m = jnp.max(s, axis=-1, keepdims=True)
    p = jnp.exp(s - m)
    return p, jnp.sum(p, axis=-1, keepdims=True)


def _attn_fwd(name, qkv, H):
    _, S, dh = qkv.shape
    G = H // N_KV_HEADS
    tq = _tile(S, 256, 16)
    kc = _tile(S, KEY_CHUNK, LANES)
    R = G * tq

    def body(q_ref, k_ref, v_ref, o_ref, lse_ref):
        q = q_ref[...].reshape(R, dh)
        m = jnp.full((R, 1), -1e30, F32)
        l = jnp.zeros((R, 1), F32)
        acc = jnp.zeros((R, dh), F32)
        for c in range(S // kc):
            rows = slice(c * kc, (c + 1) * kc)
            s = lax.dot_general(q, k_ref[rows, :], _NT, preferred_element_type=F32)
            m_new = jnp.maximum(m, jnp.max(s, axis=-1, keepdims=True))
            alpha = jnp.exp(m - m_new)
            p = jnp.exp(s - m_new)
            l = alpha * l + jnp.sum(p, axis=-1, keepdims=True)
            acc = alpha * acc + jnp.dot(p.astype(BF16), v_ref[rows, :], preferred_element_type=F32)
            m = m_new
        o_ref[...] = (acc / l).astype(BF16).reshape(G, tq, dh)
        lse_ref[...] = (m + jnp.log(l)).reshape(G, tq, 1)

    qs = pl.BlockSpec((G, tq, dh), lambda kv, i: (kv, i, 0))
    ls = pl.BlockSpec((G, tq, 1), lambda kv, i: (kv, i, 0))
    ks = pl.BlockSpec((None, S, dh), lambda kv, i: (H + kv, 0, 0))
    vs = pl.BlockSpec((None, S, dh), lambda kv, i: (H + N_KV_HEADS + kv, 0, 0))
    return _pc(body, name=name, grid=(N_KV_HEADS, S // tq), in_specs=[qs, ks, vs], out_specs=(qs, ls),
               out_shape=(jax.ShapeDtypeStruct((H, S, dh), BF16), jax.ShapeDtypeStruct((H, S, 1), F32)),
               compiler_params=_sem("parallel", "parallel"))(qkv, qkv, qkv)


def _attn_bwd(name, qkv, o, lse, do):
    H, S, dh = o.shape
    G = H // N_KV_HEADS
    tq = _tile(S, 128, 16)
    kc = _tile(S, KEY_CHUNK, LANES)
    R = G * tq

    def body(q_ref, k_ref, v_ref, o_ref, lse_ref, do_ref, dq_ref, dk_ref, dv_ref):
        @pl.when(pl.program_id(1) == 0)
        def _():
            dk_ref[...] = jnp.zeros_like(dk_ref)
            dv_ref[...] = jnp.zeros_like(dv_ref)

        qq, dd = q_ref[...].reshape(R, dh), do_ref[...].reshape(R, dh)
        delta = jnp.sum(dd.astype(F32) * o_ref[...].reshape(R, dh).astype(F32), axis=-1, keepdims=True)
        lse = lse_ref[...].reshape(R, 1)
        dq = jnp.zeros((R, dh), F32)
        for c in range(S // kc):
            rows = slice(c * kc, (c + 1) * kc)
            kk, vv = k_ref[rows, :], v_ref[rows, :]
            p = jnp.exp(lax.dot_general(qq, kk, _NT, preferred_element_type=F32) - lse)
            dv_ref[rows, :] += lax.dot_general(p.astype(BF16), dd, _TN, preferred_element_type=F32)
            dp = lax.dot_general(dd, vv, _NT, preferred_element_type=F32)
            ds = (p * (dp - delta)).astype(BF16)
            dq = dq + jnp.dot(ds, kk, preferred_element_type=F32)
            dk_ref[rows, :] += lax.dot_general(ds, qq, _TN, preferred_element_type=F32)
        dq_ref[...] = dq.reshape(G, tq, dh)

    qs = pl.BlockSpec((G, tq, dh), lambda kv, i: (kv, i, 0))
    ls = pl.BlockSpec((G, tq, 1), lambda kv, i: (kv, i, 0))
    ks = pl.BlockSpec((None, S, dh), lambda kv, i: (H + kv, 0, 0))
    vs = pl.BlockSpec((None, S, dh), lambda kv, i: (H + N_KV_HEADS + kv, 0, 0))
    acc = pl.BlockSpec((None, S, dh), lambda kv, i: (kv, 0, 0))
    return _pc(body, name=name, grid=(N_KV_HEADS, S // tq), in_specs=[qs, ks, vs, qs, ls, qs],
               out_specs=(qs, acc, acc),
               out_shape=(jax.ShapeDtypeStruct((H, S, dh), F32), jax.ShapeDtypeStruct((N_KV_HEADS, S, dh), F32),
                          jax.ShapeDtypeStruct((N_KV_HEADS, S, dh), F32)),
               compiler_params=_sem("parallel", "arbitrary"))(qkv, qkv, qkv, o, lse, do)


def _xattn_fwd(name, q, kv):
    S, D = q.shape
    _, M, dh = kv.shape
    scale = dh ** -0.5
    tq = _tile(S, 512, 16)

    def body(q_ref, kv_ref, o_ref):
        for h in range(X_HEADS):
            lo, hi = h * dh, (h + 1) * dh
            s = lax.dot_general(q_ref[:, lo:hi], kv_ref[h], _NT, preferred_element_type=F32) * scale
            p, l = _softmax_rows(s)
            o = jnp.dot(p.astype(BF16), kv_ref[X_HEADS + h], preferred_element_type=F32)
            o_ref[:, lo:hi] = (o / l).astype(BF16)

    row = pl.BlockSpec((tq, D), lambda i: (i, 0))
    return _pc(body, name=name, grid=(S // tq,),
               in_specs=[row, pl.BlockSpec((2 * X_HEADS, M, dh), lambda i: (0, 0, 0))],
               out_specs=row, out_shape=jax.ShapeDtypeStruct((S, D), BF16),
               compiler_params=_sem("parallel"))(q, kv)


def _xattn_bwd(name, q, kv, do):
    S, D = q.shape
    _, M, dh = kv.shape
    scale = dh ** -0.5
    tq = _tile(S, 512, 16)

    def body(q_ref, kv_ref, do_ref, dq_ref, dkv_ref):
        @pl.when(pl.program_id(0) == 0)
        def _():
            dkv_ref[...] = jnp.zeros_like(dkv_ref)

        for h in range(X_HEADS):
            lo, hi = h * dh, (h + 1) * dh
            qh, kh, vh, doh = q_ref[:, lo:hi], kv_ref[h], kv_ref[X_HEADS + h], do_ref[:, lo:hi]
            s = lax.dot_general(qh, kh, _NT, preferred_element_type=F32) * scale
            p, l = _softmax_rows(s)
            pn = p / l
            dkv_ref[X_HEADS + h] += lax.dot_general(pn.astype(BF16), doh, _TN, preferred_element_type=F32)
            dp = lax.dot_general(doh, vh, _NT, preferred_element_type=F32)
            ds = (pn * (dp - jnp.sum(pn * dp, axis=-1, keepdims=True)) * scale).astype(BF16)
            dq_ref[:, lo:hi] = jnp.dot(ds, kh, preferred_element_type=F32).astype(BF16)
            dkv_ref[h] += lax.dot_general(ds, qh, _TN, preferred_element_type=F32)

    row = pl.BlockSpec((tq, D), lambda i: (i, 0))
    full = pl.BlockSpec((2 * X_HEADS, M, dh), lambda i: (0, 0, 0))
    return _pc(body, name=name, grid=(S // tq,), in_specs=[row, full, row], out_specs=(row, full),
               out_shape=(jax.ShapeDtypeStruct((S, D), BF16), jax.ShapeDtypeStruct((2 * X_HEADS, M, dh), F32)),
               compiler_params=_sem("arbitrary"))(q, kv, do)


def _sigmoid(x):
    return 1.0 / (1.0 + jnp.exp(-x))


def _halo_specs(tm, n, S):
    nb = tm // 8
    last8 = S // 8 - 1
    main = pl.BlockSpec((2, None, tm, n), lambda j, i: (0, j, i, 0))
    prev = pl.BlockSpec((2, None, 8, n), lambda j, i: (0, j, jnp.maximum(i * nb - 1, 0), 0))
    nxt = pl.BlockSpec((2, None, 8, n), lambda j, i: (0, j, jnp.minimum((i + 1) * nb, last8), 0))
    return main, prev, nxt


def _ffn_up_act(name, h, w, cw, cb, layer):
    S, K = h.shape
    _, J, _, n = w.shape
    tm = _tile(S, 512, FFN_HALO)
    nblk = S // tm
    hb, last = tm // FFN_HALO, S // FFN_HALO - 1
    te = tm + 2 * FFN_HALO

    def body(h_ref, hp_ref, hn_ref, w_ref, cw_ref, b_ref, u_ref, a_ref):
        i = pl.program_id(1)
        zero = jnp.zeros((FFN_HALO, K), BF16)
        he = jnp.concatenate([jnp.where(i == 0, zero, hp_ref[...]), h_ref[...],
                              jnp.where(i == nblk - 1, zero, hn_ref[...])], axis=0)
        mid = slice(FFN_HALO, tm + FFN_HALO)
        c = []
        for half in range(2):
            ue = jnp.dot(he, w_ref[half], preferred_element_type=F32)
            um = ue[mid]
            u_ref[half] = um
            k = cw_ref[half]
            c.append(pltpu.roll(ue, 1, 0)[mid] * k[0:1] + um * k[1:2] + pltpu.roll(ue, te - 1, 0)[mid] * k[2:3]
                     + b_ref[half])
        a_ref[...] = (c[0] * _sigmoid(c[0]) * c[1]).astype(BF16)

    return _pc(body, name=name, grid=(J, nblk),
               in_specs=[pl.BlockSpec((tm, K), lambda j, i: (i, 0)),
                         pl.BlockSpec((FFN_HALO, K), lambda j, i: (jnp.maximum(i * hb - 1, 0), 0)),
                         pl.BlockSpec((FFN_HALO, K), lambda j, i: (jnp.minimum((i + 1) * hb, last), 0)),
                         pl.BlockSpec((2, None, K, n), lambda j, i: (0, j, layer, 0)),
                         pl.BlockSpec((2, None, 3, n), lambda j, i: (0, j, 0, 0)),
                         pl.BlockSpec((2, None, 1, n), lambda j, i: (0, j, 0, 0))],
               out_specs=(pl.BlockSpec((2, None, tm, n), lambda j, i: (0, j, i, 0)),
                          pl.BlockSpec((None, tm, n), lambda j, i: (j, i, 0))),
               out_shape=(jax.ShapeDtypeStruct((2, J, S, n), F32), jax.ShapeDtypeStruct((J, S, n), BF16)),
               compiler_params=_sem("parallel", "parallel"))(h, h, h, w, cw, cb)


def _ffn_act_bwd(name, u, g, w_down, cw, cb):
    _, J, S, n = u.shape
    D = g.shape[1]
    tm = _tile(S, 256, 16)
    nblk = S // tm
    te = tm + 16
    nb = tm // 8
    last8 = S // 8 - 1

    def body(u_ref, up_ref, un_ref, g_ref, gp_ref, gn_ref, wd_ref, w_ref, b_ref, du_ref, st_ref):
        i = pl.program_id(1)

        @pl.when(i == 0)
        def _():
            st_ref[...] = jnp.zeros_like(st_ref)

        def extended(before, main, after):
            return jnp.concatenate([jnp.where(i == 0, 0.0, before), main, jnp.where(i == nblk - 1, 0.0, after)], axis=0)

        mid = slice(8, tm + 8)
        da_e = lax.dot_general(extended(gp_ref[...], g_ref[...], gn_ref[...]).astype(BF16), wd_ref[...], _NT,
                               preferred_element_type=F32)
        ue, c = [], []
        for half in range(2):
            e = extended(up_ref[half], u_ref[half], un_ref[half])
            w = w_ref[half]
            ue.append((pltpu.roll(e, 1, 0), e, pltpu.roll(e, te - 1, 0)))
            c.append(ue[half][0] * w[0:1] + e * w[1:2] + ue[half][2] * w[2:3] + b_ref[half])
        sg = _sigmoid(c[0])
        dc = [da_e * c[1] * (sg * (1.0 + c[0] * (1.0 - sg))), da_e * (c[0] * sg)]
        r8 = lax.broadcasted_iota(jnp.int32, (8, n), 0)
        for half in range(2):
            w, d, (e_before, e, e_after) = w_ref[half], dc[half], ue[half]
            dm = d[mid]
            du = pltpu.roll(d, te - 1, 0)[mid] * w[0:1] + dm * w[1:2] + pltpu.roll(d, 1, 0)[mid] * w[2:3]
            du_ref[half] = du.astype(BF16)
            s0 = jnp.sum(dm * e_before[mid], axis=0, keepdims=True)
            s1 = jnp.sum(dm * e[mid], axis=0, keepdims=True)
            s2 = jnp.sum(dm * e_after[mid], axis=0, keepdims=True)
            s3 = jnp.sum(dm, axis=0, keepdims=True)
            st_ref[half] += jnp.where(r8 == 0, s0, jnp.where(r8 == 1, s1, jnp.where(r8 == 2, s2,
                                      jnp.where(r8 == 3, s3, 0.0))))

    main, prev, nxt = _halo_specs(tm, n, S)
    gmain = pl.BlockSpec((tm, D), lambda j, i: (i, 0))
    gprev = pl.BlockSpec((8, D), lambda j, i: (jnp.maximum(i * nb - 1, 0), 0))
    gnxt = pl.BlockSpec((8, D), lambda j, i: (jnp.minimum((i + 1) * nb, last8), 0))
    return _pc(body, name=name, grid=(J, nblk),
               in_specs=[main, prev, nxt, gmain, gprev, gnxt, pl.BlockSpec((n, D), lambda j, i: (j, 0)),
                         pl.BlockSpec((2, None, 3, n), lambda j, i: (0, j, 0, 0)),
                         pl.BlockSpec((2, None, 1, n), lambda j, i: (0, j, 0, 0))],
               out_specs=(main, pl.BlockSpec((2, None, 8, n), lambda j, i: (0, j, 0, 0))),
               out_shape=(jax.ShapeDtypeStruct((2, J, S, n), BF16), jax.ShapeDtypeStruct((2, J, 8, n), F32)),
               compiler_params=_sem("parallel", "arbitrary"))(u, u, u, g, g, g, w_down, cw, cb)


def _window_count(t, w, S):
    lo = jnp.maximum(t - w // 2, 0)
    hi = jnp.minimum(t + w - w // 2, S)
    return (hi - lo).astype(F32)


def _trailing_sums(x, w):
    k = 1
    while k < w:
        x = x + pltpu.roll(x, k, 0)
        k *= 2
    return x


def _pool_window(name, h, group_w, adjoint, out_dtype):
    S, D = h.shape
    SP = S + 2 * POOL_PAD
    per_group = group_w // LANES

    def body(h_ref, o_ref, xp):
        g = pl.program_id(0) // per_group
        t = lax.broadcasted_iota(jnp.int32, (S, LANES), 0)
        xp[0:POOL_PAD, :] = jnp.zeros((POOL_PAD, LANES), F32)
        xp[S + POOL_PAD:SP, :] = jnp.zeros((POOL_PAD, LANES), F32)
        for gi, w in enumerate(POOL_WINDOWS):
            @pl.when(g == gi)
            def _():
                hv = h_ref[...]
                cnt = _window_count(t, w, S)
                xp[POOL_PAD:S + POOL_PAD, :] = hv / cnt if adjoint else hv
                ahead = w // 2 if adjoint else w // 2 - 1
                sw = _trailing_sums(xp[...], w)
                if ahead:
                    sw = pltpu.roll(sw, SP - ahead, 0)
                win = sw[POOL_PAD:S + POOL_PAD]
                o_ref[...] = ((win if adjoint else win / cnt) - hv).astype(out_dtype)

    col = pl.BlockSpec((S, LANES), lambda j: (0, j))
    return _pc(body, name=name, grid=(D // LANES,), in_specs=[col], out_specs=col,
               out_shape=jax.ShapeDtypeStruct((S, D), out_dtype),
               scratch_shapes=[pltpu.VMEM((SP, LANES), F32)], compiler_params=_sem("parallel"))(h)


def _pool_proj(name, mixed, w, scale, res):
    S, D = mixed.shape
    G, gw, _ = w.shape
    tm = _tile(S, 512, 16)

    def body(m_ref, w_ref, s_ref, r_ref, o_ref):
        for g in range(G):
            lo, hi = g * gw, (g + 1) * gw
            y = jnp.dot(m_ref[:, lo:hi], w_ref[g], preferred_element_type=F32)
            o_ref[:, lo:hi] = r_ref[:, lo:hi] + y * s_ref[:, lo:hi]

    row = pl.BlockSpec((tm, D), lambda i: (i, 0))
    return _pc(body, name=name, grid=(S // tm,),
               in_specs=[row, pl.BlockSpec((G, gw, gw), lambda i: (0, 0, 0)), pl.BlockSpec((1, D), lambda i: (0, 0)), row],
               out_specs=row, out_shape=jax.ShapeDtypeStruct((S, D), F32),
               compiler_params=_sem("parallel"))(mixed, w, scale, res)


def _pool_proj_bwd(name, dy, mixed, w, scale):
    S, D = mixed.shape
    G, gw, _ = w.shape
    tm = _tile(S, 512, 16)

    def body(dy_ref, m_ref, w_ref, s_ref, dm_ref, dw_ref, ds_ref):
        @pl.when(pl.program_id(0) == 0)
        def _():
            dw_ref[...] = jnp.zeros_like(dw_ref)
            ds_ref[...] = jnp.zeros_like(ds_ref)

        for g in range(G):
            lo, hi = g * gw, (g + 1) * gw
            mg, dyg = m_ref[:, lo:hi], dy_ref[:, lo:hi]
            y = jnp.dot(mg, w_ref[g], preferred_element_type=F32)
            ds_ref[:, lo:hi] += jnp.sum(dyg * y, axis=0, keepdims=True)
            dyp = (dyg * s_ref[:, lo:hi]).astype(BF16)
            dm_ref[:, lo:hi] = lax.dot_general(dyp, w_ref[g], _NT, preferred_element_type=F32)
            dw_ref[g] += lax.dot_general(mg, dyp, _TN, preferred_element_type=F32)

    row = pl.BlockSpec((tm, D), lambda i: (i, 0))
    wsp = pl.BlockSpec((G, gw, gw), lambda i: (0, 0, 0))
    vec = pl.BlockSpec((1, D), lambda i: (0, 0))
    return _pc(body, name=name, grid=(S // tm,), in_specs=[row, row, wsp, vec], out_specs=(row, wsp, vec),
               out_shape=(jax.ShapeDtypeStruct((S, D), F32), jax.ShapeDtypeStruct((G, gw, gw), F32),
                          jax.ShapeDtypeStruct((1, D), F32)),
               compiler_params=_sem("arbitrary"))(dy, mixed, w, scale)


def _adamw(name, w, g, m, v):
    shape = w.shape
    C = shape[-1]
    R = w.size // C
    tm = _tile(R, 512, 8)

    def body(w_ref, g_ref, m_ref, v_ref, d_ref, nm_ref, nv_ref):
        gv = g_ref[...]
        nm = ADAM_B1 * m_ref[...] + (1.0 - ADAM_B1) * gv
        nv = ADAM_B2 * v_ref[...] + (1.0 - ADAM_B2) * (gv * gv)
        m_hat = nm / (1.0 - ADAM_B1 ** ADAM_STEP)
        v_hat = nv / (1.0 - ADAM_B2 ** ADAM_STEP)
        d_ref[...] = -ADAM_LR * (m_hat / (jnp.sqrt(v_hat) + ADAM_EPS) + ADAM_WD * w_ref[...])
        nm_ref[...] = nm
        nv_ref[...] = nv

    blk = pl.BlockSpec((tm, C), lambda i: (i, 0))
    sd = jax.ShapeDtypeStruct((R, C), F32)
    outs = _pc(body, name=name, grid=(R // tm,), in_specs=[blk] * 4, out_specs=(blk,) * 3, out_shape=(sd,) * 3,
               compiler_params=_sem("parallel"))(*(a.reshape(R, C) for a in (w, g, m, v)))
    return tuple(o.reshape(shape) for o in outs)


def _position():
    return lax.axis_index("x"), lax.axis_index("y"), lax.axis_index("c")


def _flip(v, bit):
    return 1 - v if bit else v


def _allgather_small(name, v):
    R, W = v.shape

    def body(v_ref, out_ref, send_sems, recv_sems):
        x, y, c = _position()
        me = 4 * x + 2 * y + c
        out_ref[me] = v_ref[...]
        sends = []
        for k in range(1, N_DEV):
            peer = (_flip(x, k & 4), _flip(y, k & 2), _flip(c, k & 1))
            cp = pltpu.make_async_remote_copy(src_ref=v_ref, dst_ref=out_ref.at[me], send_sem=send_sems.at[k - 1],
                                              recv_sem=recv_sems.at[k - 1], device_id=peer, device_id_type=MESH)
            cp.start()
            sends.append(cp)
        for k in range(1, N_DEV):
            peer = (_flip(x, k & 4), _flip(y, k & 2), _flip(c, k & 1))
            slot = 4 * peer[0] + 2 * peer[1] + peer[2]
            pltpu.make_async_remote_copy(src_ref=v_ref, dst_ref=out_ref.at[slot], send_sem=send_sems.at[k - 1],
                                         recv_sem=recv_sems.at[k - 1], device_id=peer, device_id_type=MESH).wait_recv()
        for cp in sends:
            cp.wait_send()

    vm = pl.BlockSpec(memory_space=pltpu.VMEM)
    return _pc(body, name=name, in_specs=[vm], out_specs=vm, out_shape=jax.ShapeDtypeStruct((N_DEV, R, W), F32),
               scratch_shapes=[pltpu.SemaphoreType.DMA((N_DEV - 1,)), pltpu.SemaphoreType.DMA((N_DEV - 1,))])(v)


def _sum_slots(name, a):
    n, R, W = a.shape

    def body(a_ref, o_ref):
        acc = a_ref[0]
        for s in range(1, n):
            acc = acc + a_ref[s]
        o_ref[...] = acc

    return _pc(body, name=name, grid=(1,), in_specs=[pl.BlockSpec((n, R, W), lambda i: (0, 0, 0))],
               out_specs=pl.BlockSpec((R, W), lambda i: (0, 0)), out_shape=jax.ShapeDtypeStruct((R, W), F32))(a)


def _allgather_blocks(name, blocks):
    n = len(blocks)

    def body(*refs):
        b_refs, out_refs, token = refs[:n], refs[n:2 * n], refs[2 * n]
        send_sems, recv_sems, local_sems = refs[2 * n + 1:]
        token[...] = jnp.zeros_like(token)
        x, y, c = _position()
        me, sibling = (x, y, c), (x, y, 1 - c)
        chips = [(1 - x, y), (x, 1 - y), (1 - x, 1 - y)]

        def slot(i, px, py, pc):
            return out_refs[i].at[4 * px + 2 * py + pc]

        def copy(i, k, block, to, src=None):
            return pltpu.make_async_remote_copy(src_ref=slot(i, *block) if src is None else src, dst_ref=slot(i, *block),
                                                send_sem=send_sems.at[k, i], recv_sem=recv_sems.at[k, i],
                                                device_id=to, device_id_type=MESH)

        mine = [pltpu.make_async_copy(b_refs[i], slot(i, *me), local_sems.at[i]) for i in range(n)]
        first = [copy(i, 1 + j, me, (*chip, c), src=b_refs[i]) for i in range(n) for j, chip in enumerate(chips)]
        first += [copy(i, 0, me, sibling, src=b_refs[i]) for i in range(n)]
        for cp in mine + first:
            cp.start()
        passed = []
        for j, chip in enumerate(chips):
            for i in range(n):
                copy(i, 1 + j, (*chip, c), me).wait_recv()
                passed.append(copy(i, 4 + j, (*chip, c), sibling))
                passed[-1].start()
        for i in range(n):
            copy(i, 0, sibling, me).wait_recv()
        for j, chip in enumerate(chips):
            for i in range(n):
                copy(i, 4 + j, (*chip, 1 - c), me).wait_recv()
        for cp in first + passed:
            cp.wait_send()
        for cp in mine:
            cp.wait()

    hbm = pl.BlockSpec(memory_space=pl.ANY)
    return _pc(body, name=name, in_specs=[hbm] * n, out_specs=[hbm] * n + [pl.BlockSpec(memory_space=pltpu.VMEM)],
               out_shape=[jax.ShapeDtypeStruct((N_DEV,) + b.shape, b.dtype) for b in blocks]
               + [jax.ShapeDtypeStruct((8, LANES), F32)],
               scratch_shapes=[pltpu.SemaphoreType.DMA((7, n)), pltpu.SemaphoreType.DMA((7, n)),
                               pltpu.SemaphoreType.DMA((n,))])(*blocks)


def _add_sibling(name, g4, r1, pos):
    n, _, L, W = g4.shape
    tl = _tile(L, 512, 16)

    def body(pos_ref, g_ref, r_ref, tb_ref, own_ref):
        t = g_ref[...] + r_ref[...]
        tb_ref[...] = t.astype(BF16)

        @pl.when(pl.program_id(1) == pos_ref[1])
        def _():
            own_ref[...] = t

    gs = pltpu.PrefetchScalarGridSpec(
        num_scalar_prefetch=1, grid=(L // tl, n),
        in_specs=[pl.BlockSpec((None, None, tl, W), lambda i, k, p: (k, p[0], i, 0)),
                  pl.BlockSpec((None, tl, W), lambda i, k, p: (k, i, 0))],
        out_specs=(pl.BlockSpec((None, tl, W), lambda i, k, p: (k, i, 0)),
                   pl.BlockSpec((tl, W), lambda i, k, p: (i, 0))))
    return _pc(body, name=name, grid_spec=gs,
               out_shape=(jax.ShapeDtypeStruct((n, L, W), BF16), jax.ShapeDtypeStruct((L, W), F32)),
               compiler_params=_sem("parallel", "arbitrary"))(pos, g4, r1)


def _add_chips(name, own, r2):
    L, W = own.shape
    tl = _tile(L, 512, 16)

    def body(o_ref, r_ref, out_ref):
        acc = o_ref[...]
        for j in range(3):
            acc = acc + r_ref[j].astype(F32)
        out_ref[...] = acc

    return _pc(body, name=name, grid=(L // tl,),
               in_specs=[pl.BlockSpec((tl, W), lambda i: (i, 0)), pl.BlockSpec((3, tl, W), lambda i: (0, i, 0))],
               out_specs=pl.BlockSpec((tl, W), lambda i: (i, 0)), out_shape=jax.ShapeDtypeStruct((L, W), F32),
               compiler_params=_sem("parallel"))(own, r2)


_HBM = pl.BlockSpec(memory_space=pltpu.HBM)
_SEM = pl.BlockSpec(memory_space=pltpu.SEMAPHORE)
_EFFECT = pltpu.SideEffectType.DATAFLOW_SIDE_EFFECTING


def _in_hbm(a):
    return pltpu.with_memory_space_constraint(a, pltpu.HBM)


def _after(x, token):
    return x + token[0, 0].astype(x.dtype)


def _copies_start(name, bufs, sem_shape, plan):
    nb = len(bufs)

    def body(*refs):
        for cp in plan(refs[:nb], refs[nb], refs[nb + 1]):
            cp.start()
        refs[-1][...] = jnp.zeros_like(refs[-1])

    out = _pc(body, name=name, in_specs=[_HBM] * nb,
              out_specs=(_SEM, _SEM, *[_HBM] * nb, pl.BlockSpec(memory_space=pltpu.VMEM)),
              out_shape=(pltpu.SemaphoreType.DMA(sem_shape), pltpu.SemaphoreType.DMA(sem_shape),
                         *[pltpu.HBM(b.shape, b.dtype) for b in bufs], jax.ShapeDtypeStruct((8, LANES), F32)),
              input_output_aliases={i: 2 + i for i in range(nb)},
              compiler_params=pltpu.CompilerParams(has_side_effects=_EFFECT))(*[_in_hbm(b) for b in bufs])
    return out[0], out[1], list(out[2:2 + nb]), out[-1]


def _copies_wait(name, bufs, send_sems, recv_sems, plan, after):
    nb = len(bufs)

    def body(*refs):
        for cp in plan(refs[:nb], refs[nb], refs[nb + 1]):
            cp.wait_send()
            cp.wait_recv()

    return list(_pc(body, name=name, in_specs=[_HBM] * nb + [_SEM, _SEM, pl.BlockSpec(memory_space=pl.ANY)],
                    out_specs=[_HBM] * nb, out_shape=[pltpu.HBM(b.shape, b.dtype) for b in bufs],
                    input_output_aliases={i: i for i in range(nb)},
                    compiler_params=pltpu.CompilerParams(has_side_effects=_EFFECT))(*bufs, send_sems, recv_sems, after))


def _plan_gather_chips(n):
    def plan(refs, send_sems, recv_sems):
        x, y, c = _position()
        peers = [(x, y, 1 - c), (1 - x, y, c), (x, 1 - y, c), (1 - x, 1 - y, c)]
        return [pltpu.make_async_remote_copy(src_ref=refs[i], dst_ref=refs[n + i].at[4 * x + 2 * y + c],
                                             send_sem=send_sems.at[k * n + i], recv_sem=recv_sems.at[k * n + i],
                                             device_id=peer, device_id_type=MESH)
                for i in range(n) for k, peer in enumerate(peers)]
    return plan


def _plan_gather_sibling(n):
    def plan(refs, send_sems, recv_sems):
        x, y, c = _position()
        slots = [4 * (1 - x) + 2 * y + c, 4 * x + 2 * (1 - y) + c, 4 * (1 - x) + 2 * (1 - y) + c]
        return [pltpu.make_async_remote_copy(src_ref=refs[i].at[s], dst_ref=refs[i].at[s],
                                             send_sem=send_sems.at[k * n + i], recv_sem=recv_sems.at[k * n + i],
                                             device_id=(x, y, 1 - c), device_id_type=MESH)
                for i in range(n) for k, s in enumerate(slots)]
    return plan


def _plan_reduce_sibling(n):
    def plan(refs, send_sems, recv_sems):
        x, y, c = _position()
        return [pltpu.make_async_remote_copy(src_ref=refs[i].at[k, 1 - c], dst_ref=refs[n + i].at[k],
                                             send_sem=send_sems.at[k * n + i], recv_sem=recv_sems.at[k * n + i],
                                             device_id=(x, y, 1 - c), device_id_type=MESH)
                for i in range(n) for k in range(N_DEV // 2)]
    return plan


def _plan_reduce_chips(n):
    def plan(refs, send_sems, recv_sems):
        x, y, c = _position()
        cps = []
        for i in range(n):
            for j in range(1, 4):
                px, py = _flip(x, j & 2), _flip(y, j & 1)
                sem = (j - 1) * n + i
                cps.append(pltpu.make_async_remote_copy(src_ref=refs[i].at[2 * px + py], dst_ref=refs[n + i].at[j - 1],
                                                        send_sem=send_sems.at[sem], recv_sem=recv_sems.at[sem],
                                                        device_id=(px, py, c), device_id_type=MESH))
        return cps
    return plan


def _heads_major(name, a, after):
    S, W = a.shape
    H = W // HEAD_DIM
    tm = _tile(S, 512, 16)

    def body(a_ref, after_ref, o_ref):
        v = a_ref[...]
        for h in range(H):
            o_ref[h] = v[:, h * HEAD_DIM:(h + 1) * HEAD_DIM]

    return _pc(body, name=name, grid=(S // tm,),
               in_specs=[pl.BlockSpec((tm, W), lambda i: (i, 0)), pl.BlockSpec(memory_space=pl.ANY)],
               out_specs=pl.BlockSpec((H, tm, HEAD_DIM), lambda i: (0, i, 0)),
               out_shape=jax.ShapeDtypeStruct((H, S, HEAD_DIM), a.dtype), compiler_params=_sem("parallel"))(a, after)


def _heads_minor(name, a):
    H, S, _ = a.shape
    tm = _tile(S, 512, 16)

    def body(a_ref, o_ref):
        o_ref[...] = jnp.concatenate([a_ref[h] for h in range(H)], axis=1)

    return _pc(body, name=name, grid=(S // tm,), in_specs=[pl.BlockSpec((H, tm, HEAD_DIM), lambda i: (0, i, 0))],
               out_specs=pl.BlockSpec((tm, H * HEAD_DIM), lambda i: (i, 0)),
               out_shape=jax.ShapeDtypeStruct((S, H * HEAD_DIM), a.dtype), compiler_params=_sem("parallel"))(a)


def kernel(x, mem, attn_norm, attn_w_qkv, attn_q_gain, attn_k_gain, attn_w_o, pool_norm, pool_w, pool_scale, xattn_norm, mem_norm, xattn_w_q, xattn_w_kv, xattn_w_o, ffn_norm, ffn_w_up, ffn_conv_w, ffn_conv_b, ffn_w_down, final_norm, loss_target, m_attn_norm, m_attn_w_qkv, m_attn_q_gain, m_attn_k_gain, m_attn_w_o, m_pool_norm, m_pool_w, m_pool_scale, m_xattn_norm, m_mem_norm, m_xattn_w_q, m_xattn_w_kv, m_xattn_w_o, m_ffn_norm, m_ffn_w_up, m_ffn_conv_w, m_ffn_conv_b, m_ffn_w_down, m_final_norm, v_attn_norm, v_attn_w_qkv, v_attn_q_gain, v_attn_k_gain, v_attn_w_o, v_pool_norm, v_pool_w, v_pool_scale, v_xattn_norm, v_mem_norm, v_xattn_w_q, v_xattn_w_kv, v_xattn_w_o, v_ffn_norm, v_ffn_w_up, v_ffn_conv_w, v_ffn_conv_b, v_ffn_w_down, v_final_norm):
    names = ['attn_norm', 'attn_w_qkv', 'attn_q_gain', 'attn_k_gain', 'attn_w_o', 'pool_norm', 'pool_w', 'pool_scale',
             'xattn_norm', 'mem_norm', 'xattn_w_q', 'xattn_w_kv', 'xattn_w_o', 'ffn_norm', 'ffn_w_up', 'ffn_conv_w',
             'ffn_conv_b', 'ffn_w_down', 'final_norm']
    W = dict(zip(names, (attn_norm, attn_w_qkv, attn_q_gain, attn_k_gain, attn_w_o, pool_norm, pool_w, pool_scale,
                         xattn_norm, mem_norm, xattn_w_q, xattn_w_kv, xattn_w_o, ffn_norm, ffn_w_up, ffn_conv_w,
                         ffn_conv_b, ffn_w_down, final_norm)))
    Mo = dict(zip(names, (m_attn_norm, m_attn_w_qkv, m_attn_q_gain, m_attn_k_gain, m_attn_w_o, m_pool_norm, m_pool_w,
                          m_pool_scale, m_xattn_norm, m_mem_norm, m_xattn_w_q, m_xattn_w_kv, m_xattn_w_o, m_ffn_norm,
                          m_ffn_w_up, m_ffn_conv_w, m_ffn_conv_b, m_ffn_w_down, m_final_norm)))
    Vo = dict(zip(names, (v_attn_norm, v_attn_w_qkv, v_attn_q_gain, v_attn_k_gain, v_attn_w_o, v_pool_norm, v_pool_w,
                          v_pool_scale, v_xattn_norm, v_mem_norm, v_xattn_w_q, v_xattn_w_kv, v_xattn_w_o, v_ffn_norm,
                          v_ffn_w_up, v_ffn_conv_w, v_ffn_conv_b, v_ffn_w_down, v_final_norm)))

    S, D = x.shape[1], x.shape[2]
    n_layers = xattn_norm.shape[0]
    n_up = ffn_w_up.shape[2]
    qkv_w = attn_w_qkv.shape[2] * N_DEV
    n_heads = qkv_w // HEAD_DIM - 2 * N_KV_HEADS
    n_rot = (n_heads + N_KV_HEADS) * HEAD_DIM // LANES
    group_w = pool_w.shape[3]
    xs, mems, tgt = x[0], mem[0], loss_target[0]
    xi, yi, ci = _position()
    dev = 4 * xi + 2 * yi + ci
    pos = jnp.stack([ci, 2 * xi + yi]).astype(jnp.int32)

    layers = range(n_layers)
    n_groups = pool_w.shape[1]
    small_vec = jnp.concatenate([pool_norm.reshape(-1), pool_scale.reshape(-1), ffn_conv_w.reshape(-1)])
    small_rows = _round_up(-(-small_vec.size // PACK_W), 8)
    small_vec = jnp.pad(small_vec, (0, small_rows * PACK_W - small_vec.size)).reshape(small_rows, PACK_W)
    w_qkv, w_o, small, attn_token = _allgather_blocks(
        "allgather_attn", [attn_w_qkv[0].astype(BF16), attn_w_o[0].astype(BF16), small_vec])
    small = small.reshape(N_DEV, -1)
    w_qkv = w_qkv.transpose(1, 0, 2).reshape(D, qkv_w)
    w_o = w_o.reshape(-1, D)
    blocks = [pool_w.reshape(-1, group_w)] + [xattn_w_q[l] for l in layers] + [xattn_w_kv.reshape(n_layers * D, -1)]
    blocks += [xattn_w_o[l] for l in layers] + [ffn_w_up.reshape(n_layers * D, n_up)] + [ffn_w_down[l] for l in layers]
    blocks = [b.astype(BF16) for b in blocks]
    blocks[0] = _after(blocks[0], attn_token)
    n_blk = len(blocks)
    lands = [lax.dynamic_update_index_in_dim(lax.empty((N_DEV,) + b.shape, BF16), b, dev, 0) for b in blocks]
    plan_chips, plan_sibling = _plan_gather_chips(n_blk), _plan_gather_sibling(n_blk)
    gather_sems = _copies_start("gather_chips_start", blocks + lands, (4 * n_blk,), plan_chips)
    attn_norm_late = _after(attn_norm, gather_sems[3])

    d_sh = pool_norm.shape[1]
    pool_norm_f = small[:, :d_sh].reshape(1, D)
    pool_scale_f = small[:, d_sh:2 * d_sh].reshape(1, D)
    conv_w_f = small[:, 2 * d_sh:2 * d_sh + ffn_conv_w.size].reshape(N_DEV, n_layers, 3, n_up)
    conv_b_f = ffn_conv_b.reshape(n_layers, N_DEV, 1, n_up)

    cos, sin = _rope_tables(S)
    bd = _head_mean_matrix()
    pad_w = qkv_w - (n_heads + N_KV_HEADS) * HEAD_DIM
    qk_gain = jnp.concatenate([jnp.tile(attn_q_gain[0], n_heads), jnp.tile(attn_k_gain[0], N_KV_HEADS),
                               jnp.ones((pad_w,), F32)]).reshape(1, qkv_w)
    qk_scale = jnp.concatenate([jnp.full((n_heads * HEAD_DIM,), HEAD_DIM ** -0.5, F32),
                                jnp.ones((qkv_w - n_heads * HEAD_DIM,), F32)]).reshape(1, qkv_w)

    saved = []

    def xattn_ffn_fwd(l, xin, hx=None):
        if hx is None:
            hx = _rmsnorm(f"xattn_norm{l}", xin, xattn_norm[l:l + 1], BF16)
        memn = _rmsnorm(f"mem_norm{l}", mems, mem_norm[l:l + 1], BF16)
        qx = _mm_nn(f"xattn_q{l}", hx, w_xq[l], BF16)
        kv = _mm_nn_bs(f"xattn_kv{l}", memn, w_xkv, BF16, l)
        ox = _xattn_fwd(f"xattn_fwd{l}", qx, kv)
        x2 = _mm_nn(f"xattn_o{l}", ox, w_xo[l], F32, res=xin)
        hf = _rmsnorm(f"ffn_norm{l}", x2, ffn_norm[l:l + 1], BF16)
        cw = conv_w_f[:, l].reshape(2, N_DEV // 2, 3, n_up)
        cb = conv_b_f[l].reshape(2, N_DEV // 2, 1, n_up)
        u, act = _ffn_up_act(f"ffn_up_act{l}", hf, w_up.reshape(2, N_DEV // 2, n_layers * D, n_up), cw, cb, l)
        x3 = _mm_nn_as(f"ffn_down{l}", act, w_down[l], F32, x2)
        saved.append(dict(xin=xin, hx=hx, memn=memn, qx=qx, kv=kv, ox=ox, x2=x2, hf=hf, u=u, cw=cw, cb=cb, act=act))
        return x3

    h0 = _rmsnorm("attn_norm", xs, attn_norm_late, BF16)
    qkv = _mm_nn("attn_qkv", h0, w_qkv, F32)
    qkr = _qk_rope("qk_rope", qkv, qk_gain, qk_scale, cos, sin, bd, n_rot)
    o_hm, lse = _attn_fwd("attn_fwd", qkr, n_heads)
    o_att = _heads_minor("attn_heads_minor", o_hm)
    arrived = _copies_wait("gather_chips_wait", gather_sems[2], gather_sems[0], gather_sems[1], plan_chips, o_att)
    pass_sems = _copies_start("gather_sibling_start", arrived[n_blk:], (3 * n_blk,), plan_sibling)
    x1 = _mm_nn("attn_o", o_att, _after(w_o, pass_sems[3]), F32, res=xs)
    hx0 = _rmsnorm("xattn_norm0", x1, xattn_norm[0:1], BF16)
    gathered = iter(_copies_wait("gather_sibling_wait", pass_sems[2], pass_sems[0], pass_sems[1], plan_sibling, hx0))
    w_pool = (next(gathered).reshape(N_DEV, n_groups, -1, group_w).transpose(1, 0, 2, 3)
              .reshape(n_groups, group_w, group_w))
    w_xq = [next(gathered).reshape(D, D) for l in layers]
    w_xkv = next(gathered)
    w_xo = [next(gathered).reshape(D, D) for l in layers]
    w_up = next(gathered)
    w_down = [next(gathered).reshape(-1, D) for l in layers]
    x3 = xattn_ffn_fwd(0, x1, hx0)
    hp = _rmsnorm("pool_norm", x3, pool_norm_f, F32)
    mixed = _pool_window("pool_window", hp, group_w, False, BF16)
    x4 = _pool_proj("pool_proj", mixed, w_pool, pool_scale_f, x3)
    x6 = xattn_ffn_fwd(1, x4)

    G = {}
    g, d_final, lvec = _loss_head("loss_head", x6, final_norm.reshape(1, D), tgt)
    G['final_norm'] = d_final.reshape(D)
    loss_part = (0.5 * jnp.sum(lvec) / D).reshape(1)

    d_xn, d_mn, d_fn, d_xq, d_xkv, d_xo, d_up, d_cw, d_cb, d_down = ([None] * n_layers for _ in range(10))

    def xattn_ffn_bwd(l, g, conv_b_late=None, after_act=None):
        sv = saved[l]
        d_down[l] = _mm_tn_as(f"ffn_down_dw{l}", sv['act'], g, F32)
        du, st = _ffn_act_bwd(f"ffn_act_bwd{l}", sv['u'], g, w_down[l], sv['cw'],
                              sv['cb'] if conv_b_late is None else conv_b_late)
        du = du.reshape(N_DEV, S, n_up)
        ffn_gain = ffn_norm[l:l + 1] if after_act is None else _after(ffn_norm[l:l + 1], after_act(du))
        st = st.reshape(N_DEV, 8, n_up)
        d_cw[l], d_cb[l] = st[:, 0:3], st[:, 3].reshape(-1)
        d_up[l] = _mm_tn_bs(f"ffn_up_dw{l}", sv['hf'], du, F32)
        g, d_fn[l] = _mm_nt_abs_norm_bwd(f"ffn_up_dx_norm_bwd{l}", du, w_up, sv['x2'], ffn_gain, g, l)
        d_xo[l] = _mm_tn(f"xattn_o_dw{l}", sv['ox'], g, F32)
        do = _mm_nt(f"xattn_o_dx{l}", g, w_xo[l], BF16)
        dq, dkv = _xattn_bwd(f"xattn_bwd{l}", sv['qx'], sv['kv'], do)
        d_xq[l] = _mm_tn(f"xattn_q_dw{l}", sv['hx'], dq, F32)
        d_xkv[l] = _mm_tn_bs(f"xattn_kv_dw{l}", sv['memn'], dkv, F32)
        dmemn = _mm_nt_abs(f"xattn_kv_dx{l}", dkv, w_xkv, D, F32, l)
        _, d_mn[l] = _rmsnorm_bwd(f"mem_norm_bwd{l}", mems, mem_norm[l:l + 1], dmemn)
        g, d_xn[l] = _mm_nt_norm_bwd(f"xattn_q_dx_norm_bwd{l}", dq, w_xq[l], sv['xin'], xattn_norm[l:l + 1], g)
        return g

    def reduce_start(tag, bufs):
        n = len(bufs)
        g4s = [b.reshape((N_DEV // 2, 2) + b.shape[1:]) for b in bufs]
        lands = [lax.empty((N_DEV // 2,) + b.shape[1:], F32) for b in bufs]
        plan = _plan_reduce_sibling(n)
        return (n, plan) + _copies_start(f"reduce_sibling_start_{tag}", g4s + lands, (N_DEV // 2 * n,), plan)

    def reduce_between(tag, state, after):
        n, plan, send_sems, recv_sems, thru, _ = state
        got = _copies_wait(f"reduce_sibling_wait_{tag}", thru, send_sems, recv_sems, plan, after)
        sums = [_add_sibling(f"reduce_add_sibling_{tag}{i}", got[i], got[n + i], pos) for i in range(n)]
        lands = [lax.empty((3,) + tb.shape[1:], BF16) for tb, _ in sums]
        plan = _plan_reduce_chips(n)
        return (n, plan, [own for _, own in sums]) + _copies_start(f"reduce_chips_start_{tag}",
                                                                    [tb for tb, _ in sums] + lands, (3 * n,), plan)

    def reduce_finish(tag, state, after):
        n, plan, owns, send_sems, recv_sems, thru, _ = state
        got = _copies_wait(f"reduce_chips_wait_{tag}", thru, send_sems, recv_sems, plan, after)
        return [_add_chips(f"reduce_add_chips_{tag}{i}", owns[i], got[n + i]) for i in range(n)]

    def layer_bufs(l):
        return [d_xq[l].reshape(N_DEV, -1, D), d_xkv[l], d_xo[l].reshape(N_DEV, -1, D), d_up[l],
                d_down[l].reshape(N_DEV, -1, D)]

    g = xattn_ffn_bwd(1, g)
    d_mixed, d_pool_w, d_pool_scale = _pool_proj_bwd("pool_proj_bwd", g, mixed, w_pool, pool_scale_f)
    dhp = _pool_window("pool_window_bwd", d_mixed, group_w, True, F32)
    g, d_pool_norm = _rmsnorm_bwd("pool_norm_bwd", x3, pool_norm_f, dhp, g)
    upper = reduce_start("upper", [d_pool_w.reshape(n_groups, N_DEV, -1, group_w).transpose(1, 0, 2, 3)
                                   .reshape(N_DEV, -1, group_w)] + layer_bufs(1))
    between = []

    def upper_between(du):
        between.append(reduce_between("upper", upper, du))
        return between[0][-1]

    g = xattn_ffn_bwd(0, g, _after(saved[0]['cb'], upper[-1]), upper_between)
    lower = reduce_start("lower", layer_bufs(0))
    d_wo = _mm_tn("attn_o_dw", o_att, g, F32)
    do = _mm_nt("attn_o_dx", g, _after(w_o, lower[-1]), BF16)
    lower = reduce_between("lower", lower, do)
    do_hm = _heads_major("attn_heads_major", do, lower[-1])
    dq_hm, dk_hm, dv_hm = _attn_bwd("attn_bwd", qkr, o_hm, lse, do_hm)
    red_lower = reduce_finish("lower", lower, dq_hm)
    d_qkv, d_gain = _qk_rope_bwd("qk_rope_bwd", dq_hm, dk_hm, dv_hm, qkv, qk_gain, qk_scale, cos, sin, bd)
    red_upper = reduce_finish("upper", between[0], d_qkv)
    d_wqkv = _mm_tn("attn_qkv_dw", h0, d_qkv, F32)
    last = reduce_start("last", [d_wqkv.reshape(D, N_DEV, -1).transpose(1, 0, 2), d_wo.reshape(N_DEV, -1, D)])
    last = reduce_between("last", last, d_wqkv)
    grad_x, d_attn_norm = _mm_nt_norm_bwd("attn_qkv_dx_norm_bwd", d_qkv, w_qkv, xs, attn_norm, g, last[-1])
    G['pool_w'] = red_upper[0].reshape(pool_w.shape)
    per_layer = [red_lower, red_upper[1:]]
    for i, n in enumerate(['xattn_w_q', 'xattn_w_kv', 'xattn_w_o', 'ffn_w_up', 'ffn_w_down']):
        G[n] = jnp.stack([per_layer[l][i] for l in layers])

    hq = n_heads * HEAD_DIM
    small_g = {'attn_norm': d_attn_norm, 'attn_q_gain': d_gain[0, :hq].reshape(n_heads, HEAD_DIM).sum(0),
               'attn_k_gain': d_gain[0, hq:hq + N_KV_HEADS * HEAD_DIM].reshape(N_KV_HEADS, HEAD_DIM).sum(0),
               'pool_norm': d_pool_norm, 'pool_scale': d_pool_scale,
               'xattn_norm': jnp.concatenate(d_xn), 'mem_norm': jnp.concatenate(d_mn), 'ffn_norm': jnp.concatenate(d_fn),
               'ffn_conv_w': jnp.stack(d_cw, axis=1), 'ffn_conv_b': jnp.stack(d_cb)}
    order = list(small_g)
    flat = jnp.concatenate([loss_part] + [small_g[n].reshape(-1) for n in order] + [G['final_norm']])
    ar_rows = _round_up(-(-flat.size // PACK_W), 8)
    flat = jnp.pad(flat, (0, ar_rows * PACK_W - flat.size)).reshape(ar_rows, PACK_W)
    summed = _sum_slots("allreduce_sum", _allgather_small("allreduce_gather", flat)).reshape(-1)
    loss = summed[0]
    red_last = reduce_finish("last", last, summed)
    G['attn_w_qkv'], G['attn_w_o'] = red_last[0][None], red_last[1][None]
    at = 1
    for n in order + ['final_norm']:
        size = G['final_norm'].size if n == 'final_norm' else small_g[n].size
        piece = summed[at:at + size]
        at += size
        if n in ('pool_norm', 'pool_scale'):
            piece = lax.dynamic_slice(piece, (dev * d_sh,), (d_sh,))
        elif n == 'ffn_conv_w':
            piece = lax.dynamic_index_in_dim(piece.reshape(N_DEV, n_layers, 3, n_up), dev, 0, keepdims=False)
        G[n] = piece.reshape(W[n].shape)

    deltas, new_m, new_v = [], [], []
    for n in names:
        d, nm, nv = _adamw(f"adamw_{n}", W[n], G[n], Mo[n], Vo[n])
        deltas.append(d)
        new_m.append(nm)
        new_v.append(nv)
    return (loss, grad_x[None], *[G[n] for n in names], *deltas, *new_m, *new_v)
```

```python
import jax
import jax.numpy as jnp
from jax import lax
from jax.experimental import pallas as pl
from jax.experimental.pallas import tpu as pltpu

F32 = jnp.float32
BF16 = jnp.bfloat16
MESH = pl.DeviceIdType.MESH

N_DEV = 8
EPS = 1e-6
HEAD_DIM = 64
N_KV_HEADS = 4
X_HEADS = 4
GRID_W = 64
ROPE_THETA = 10000.0
ROPE_PAIRS = HEAD_DIM // 4
POOL_WINDOWS = (2, 4, 8, 16)
POOL_PAD = 16
KEY_CHUNK = 1024
MM_ROWS = 1024
REDUCE_ROWS = 2048
FFN_HALO = 16
LANES = 128
PACK_W = 1024
ADAM_LR, ADAM_B1, ADAM_B2, ADAM_EPS, ADAM_WD, ADAM_STEP = 0.001, 0.9, 0.999, 1e-08, 0.01, 10

_NN = (((1,), (0,)), ((), ()))
_NT = (((1,), (1,)), ((), ()))
_TN = (((0,), (0,)), ((), ()))


def _pc(body, *, name, **kw):
    return pl.pallas_call(body, name=name, **kw)


def _sem(*kinds):
    return pltpu.CompilerParams(dimension_semantics=kinds)


def _tile(n, pref, mult):
    best = None
    for t in range(mult, min(n, pref) + 1, mult):
        if n % t == 0:
            best = t
    return n if best is None else best


def _round_up(n, m):
    return (n + m - 1) // m * m


def _mm_call(name, a, b, dims, grid, a_spec, b_spec, o_spec, out_shape, kaxis, res=None, res_spec=None):
    nk = grid[kaxis]
    acc_shape = tuple(d for d in o_spec.block_shape if d is not None)
    in_place = out_shape.dtype == F32
    use_scratch = nk > 1 and not in_place

    def body(*refs):
        refs = list(refs)
        acc = refs.pop() if use_scratch else None
        a_ref, b_ref = refs[:2]
        r_ref = refs[2] if res is not None else None
        o_ref = refs[-1]
        if len(a_ref.shape) == 3:
            n = a_ref.shape[2]
            prod = sum(jnp.dot(a_ref[s].astype(BF16), b_ref[s * n:(s + 1) * n, :].astype(BF16),
                               preferred_element_type=F32) for s in range(a_ref.shape[0]))
        else:
            prod = lax.dot_general(a_ref[...].astype(BF16), b_ref[...].astype(BF16), dims, preferred_element_type=F32)
        if nk == 1:
            if r_ref is not None:
                prod = prod + r_ref[...]
            o_ref[...] = prod.astype(o_ref.dtype)
            return
        k = pl.program_id(kaxis)
        tgt = o_ref if in_place else acc

        @pl.when(k == 0)
        def _():
            tgt[...] = prod + r_ref[...] if (in_place and r_ref is not None) else prod

        @pl.when(k > 0)
        def _():
            tgt[...] += prod

        if not in_place:
            @pl.when(k == nk - 1)
            def _():
                r = acc[...]
                if r_ref is not None:
                    r = r + r_ref[...]
                o_ref[...] = r.astype(o_ref.dtype)

    sem = tuple("arbitrary" if ax == kaxis else "parallel" for ax in range(len(grid)))
    ins = [a, b] if res is None else [a, b, res]
    specs = [a_spec, b_spec] if res is None else [a_spec, b_spec, res_spec]
    return _pc(body, name=name, grid=grid, in_specs=specs, out_specs=o_spec, out_shape=out_shape,
               scratch_shapes=[pltpu.VMEM(acc_shape, F32)] if use_scratch else [],
               compiler_params=_sem(*sem))(*ins)


def _mm_nn(name, a, b, out_dtype, res=None):
    M, K = a.shape
    N = b.shape[1]
    tm, tn, tk = _tile(M, MM_ROWS, 16), _tile(N, 1024, LANES), _tile(K, 1024, LANES)
    return _mm_call(name, a, b, _NN, (M // tm, N // tn, K // tk),
                    pl.BlockSpec((tm, tk), lambda i, j, k: (i, k)),
                    pl.BlockSpec((tk, tn), lambda i, j, k: (k, j)),
                    pl.BlockSpec((tm, tn), lambda i, j, k: (i, j)),
                    jax.ShapeDtypeStruct((M, N), out_dtype), 2, res,
                    pl.BlockSpec((tm, tn), lambda i, j, k: (i, j)))


def _mm_nt(name, a, b, out_dtype):
    M, K = a.shape
    N = b.shape[0]
    tm, tn, tk = _tile(M, MM_ROWS, 16), _tile(N, 1024, LANES), _tile(K, 1024, LANES)
    return _mm_call(name, a, b, _NT, (M // tm, N // tn, K // tk),
                    pl.BlockSpec((tm, tk), lambda i, j, k: (i, k)),
                    pl.BlockSpec((tn, tk), lambda i, j, k: (j, k)),
                    pl.BlockSpec((tm, tn), lambda i, j, k: (i, j)),
                    jax.ShapeDtypeStruct((M, N), out_dtype), 2)


def _mm_tn(name, a, b, out_dtype):
    R, M = a.shape
    N = b.shape[1]
    tm, tn, tr = _tile(M, 1024, LANES), _tile(N, 1024, LANES), _tile(R, REDUCE_ROWS, 16)
    return _mm_call(name, a, b, _TN, (M // tm, N // tn, R // tr),
                    pl.BlockSpec((tr, tm), lambda i, j, k: (k, i)),
                    pl.BlockSpec((tr, tn), lambda i, j, k: (k, j)),
                    pl.BlockSpec((tm, tn), lambda i, j, k: (i, j)),
                    jax.ShapeDtypeStruct((M, N), out_dtype), 2)


def _mm_nn_bs(name, a, b, out_dtype, layer=0):
    M, K = a.shape
    J, _, n = b.shape
    tm, tk = _tile(M, MM_ROWS, 16), _tile(K, 1024, LANES)
    first = layer * (K // tk)
    return _mm_call(name, a, b, _NN, (J, M // tm, K // tk),
                    pl.BlockSpec((tm, tk), lambda j, i, k: (i, k)),
                    pl.BlockSpec((None, tk, n), lambda j, i, k: (j, first + k, 0)),
                    pl.BlockSpec((None, tm, n), lambda j, i, k: (j, i, 0)),
                    jax.ShapeDtypeStruct((J, M, n), out_dtype), 2)


def _mm_nn_as(name, a, b, out_dtype, res):
    J, M, n = a.shape
    N = b.shape[1]
    tm, tn = _tile(M, MM_ROWS, 16), _tile(N, 1024, LANES)
    per = 2 if J % 2 == 0 else 1
    return _mm_call(name, a, b, _NN, (M // tm, N // tn, J // per),
                    pl.BlockSpec((per, tm, n), lambda i, j, k: (k, i, 0)),
                    pl.BlockSpec((per * n, tn), lambda i, j, k: (k, j)),
                    pl.BlockSpec((tm, tn), lambda i, j, k: (i, j)),
                    jax.ShapeDtypeStruct((M, N), out_dtype), 2, res,
                    pl.BlockSpec((tm, tn), lambda i, j, k: (i, j)))


def _mm_tn_as(name, a, b, out_dtype):
    J, R, n = a.shape
    N = b.shape[1]
    tn, tr = _tile(N, 1024, LANES), _tile(R, REDUCE_ROWS, 16)
    return _mm_call(name, a, b, _TN, (J, N // tn, R // tr),
                    pl.BlockSpec((None, tr, n), lambda j, jn, k: (j, k, 0)),
                    pl.BlockSpec((tr, tn), lambda j, jn, k: (k, jn)),
                    pl.BlockSpec((n, tn), lambda j, jn, k: (j, jn)),
                    jax.ShapeDtypeStruct((J * n, N), out_dtype), 2)


def _mm_nt_abs(name, a, b, N, out_dtype, layer=0):
    J, M, n = a.shape
    tm, tn = _tile(M, MM_ROWS, 16), _tile(N, 1024, LANES)
    first = layer * (N // tn)
    return _mm_call(name, a, b, _NT, (M // tm, N // tn, J),
                    pl.BlockSpec((None, tm, n), lambda i, j, k: (k, i, 0)),
                    pl.BlockSpec((None, tn, n), lambda i, j, k: (k, first + j, 0)),
                    pl.BlockSpec((tm, tn), lambda i, j, k: (i, j)),
                    jax.ShapeDtypeStruct((M, N), out_dtype), 2)


def _mm_tn_bs(name, a, b, out_dtype):
    R, M = a.shape
    J, _, n = b.shape
    tm, tr = _tile(M, 1024, LANES), _tile(R, REDUCE_ROWS, 16)
    return _mm_call(name, a, b, _TN, (J, M // tm, R // tr),
                    pl.BlockSpec((tr, tm), lambda j, i, k: (k, i)),
                    pl.BlockSpec((None, tr, n), lambda j, i, k: (j, k, 0)),
                    pl.BlockSpec((None, tm, n), lambda j, i, k: (j, i, 0)),
                    jax.ShapeDtypeStruct((J, M, n), out_dtype), 2)


def _rmsnorm(name, x, g, out_dtype):
    R, D = x.shape
    tm = _tile(R, 512, 16)

    def body(x_ref, g_ref, o_ref):
        xv = x_ref[...]
        r = lax.rsqrt(jnp.mean(xv * xv, axis=-1, keepdims=True) + EPS)
        o_ref[...] = (xv * r * g_ref[...]).astype(o_ref.dtype)

    return _pc(body, name=name, grid=(R // tm,),
               in_specs=[pl.BlockSpec((tm, D), lambda i: (i, 0)), pl.BlockSpec((1, D), lambda i: (0, 0))],
               out_specs=pl.BlockSpec((tm, D), lambda i: (i, 0)),
               out_shape=jax.ShapeDtypeStruct((R, D), out_dtype), compiler_params=_sem("parallel"))(x, g)


def _rmsnorm_bwd(name, x, g, dh, dres=None):
    R, D = x.shape
    tm = _tile(R, 512, 16)

    def body(*refs):
        if dres is None:
            x_ref, g_ref, dh_ref, dx_ref, dg_ref = refs
            dres_ref = None
        else:
            x_ref, g_ref, dh_ref, dres_ref, dx_ref, dg_ref = refs
        xv = x_ref[...]
        r = lax.rsqrt(jnp.mean(xv * xv, axis=-1, keepdims=True) + EPS)
        xh = xv * r
        dhv = dh_ref[...].astype(F32)

        @pl.when(pl.program_id(0) == 0)
        def _():
            dg_ref[...] = jnp.zeros_like(dg_ref)

        dg_ref[...] += jnp.sum(dhv * xh, axis=0, keepdims=True)
        dxh = dhv * g_ref[...]
        dx = r * (dxh - xh * jnp.mean(dxh * xh, axis=-1, keepdims=True))
        if dres_ref is not None:
            dx = dx + dres_ref[...]
        dx_ref[...] = dx

    row = pl.BlockSpec((tm, D), lambda i: (i, 0))
    vec = pl.BlockSpec((1, D), lambda i: (0, 0))
    ins = [x, g, dh] + ([] if dres is None else [dres])
    specs = [row, vec, row] + ([] if dres is None else [row])
    return _pc(body, name=name, grid=(R // tm,), in_specs=specs, out_specs=(row, vec),
               out_shape=(jax.ShapeDtypeStruct((R, D), F32), jax.ShapeDtypeStruct((1, D), F32)),
               compiler_params=_sem("arbitrary"))(*ins)


def _mm_norm_bwd(name, a, b, grid, a_spec, b_spec, tm, x, gain, dres, after=None):
    M, D = x.shape
    nk = grid[2]

    def body(*refs):
        a_ref, b_ref, x_ref, g_ref, r_ref = refs[:5]
        dx_ref, dg_ref = refs[-3 if nk > 1 else -2:][:2]
        acc = refs[-1] if nk > 1 else None
        i, k = pl.program_id(0), pl.program_id(2)
        if len(a_ref.shape) == 3:
            prod = sum(lax.dot_general(a_ref[s].astype(BF16), b_ref[s].astype(BF16), _NT, preferred_element_type=F32)
                       for s in range(a_ref.shape[0]))
        else:
            prod = lax.dot_general(a_ref[...].astype(BF16), b_ref[...].astype(BF16), _NT, preferred_element_type=F32)

        @pl.when((i == 0) & (k == 0))
        def _():
            dg_ref[...] = jnp.zeros_like(dg_ref)

        def finish(dh):
            xv = x_ref[...]
            r = lax.rsqrt(jnp.mean(xv * xv, axis=-1, keepdims=True) + EPS)
            xh = xv * r
            dg_ref[...] += jnp.sum(dh * xh, axis=0, keepdims=True)
            dxh = dh * g_ref[...]
            dx_ref[...] = r * (dxh - xh * jnp.mean(dxh * xh, axis=-1, keepdims=True)) + r_ref[...]

        if nk == 1:
            finish(prod)
            return

        @pl.when(k == 0)
        def _():
            acc[...] = prod

        @pl.when(k > 0)
        def _():
            acc[...] += prod

        @pl.when(k == nk - 1)
        def _():
            finish(acc[...])

    row = pl.BlockSpec((tm, D), lambda i, j, k: (i, 0))
    vec = pl.BlockSpec((1, D), lambda i, j, k: (0, 0))
    ins = [a, b, x, gain, dres] + ([] if after is None else [after])
    specs = [a_spec, b_spec, row, vec, row] + ([] if after is None else [pl.BlockSpec(memory_space=pl.ANY)])
    return _pc(body, name=name, grid=grid, in_specs=specs, out_specs=(row, vec),
               out_shape=(jax.ShapeDtypeStruct((M, D), F32), jax.ShapeDtypeStruct((1, D), F32)),
               scratch_shapes=[pltpu.VMEM((tm, D), F32)] if nk > 1 else [],
               compiler_params=_sem("arbitrary", "arbitrary", "arbitrary"))(*ins)


def _mm_nt_norm_bwd(name, a, b, x, gain, dres, after=None):
    M, K = a.shape
    D = b.shape[0]
    tm, tk = _tile(M, MM_ROWS, 16), _tile(K, 1024, LANES)
    return _mm_norm_bwd(name, a, b, (M // tm, 1, K // tk),
                        pl.BlockSpec((tm, tk), lambda i, j, k: (i, k)),
                        pl.BlockSpec((D, tk), lambda i, j, k: (0, k)), tm, x, gain, dres, after)


def _mm_nt_abs_norm_bwd(name, a, b, x, gain, dres, layer=0):
    J, M, n = a.shape
    D = x.shape[1]
    tm = _tile(M, MM_ROWS, 16)
    per = 2 if J % 2 == 0 else 1
    return _mm_norm_bwd(name, a, b, (M // tm, 1, J // per),
                        pl.BlockSpec((per, tm, n), lambda i, j, k: (k, i, 0)),
                        pl.BlockSpec((per, D, n), lambda i, j, k: (k, layer, 0)), tm, x, gain, dres)


def _loss_head(name, x, g, tgt):
    R, D = x.shape
    tm = _tile(R, 512, 16)

    def body(x_ref, g_ref, t_ref, dx_ref, dg_ref, l_ref):
        xv = x_ref[...]
        r = lax.rsqrt(jnp.mean(xv * xv, axis=-1, keepdims=True) + EPS)
        xh = xv * r
        err = xh * g_ref[...] - t_ref[...]

        @pl.when(pl.program_id(0) == 0)
        def _():
            dg_ref[...] = jnp.zeros_like(dg_ref)
            l_ref[...] = jnp.zeros_like(l_ref)

        l_ref[...] += jnp.sum(err * err, axis=0, keepdims=True)
        dy = err * (1.0 / D)
        dg_ref[...] += jnp.sum(dy * xh, axis=0, keepdims=True)
        dxh = dy * g_ref[...]
        dx_ref[...] = r * (dxh - xh * jnp.mean(dxh * xh, axis=-1, keepdims=True))

    row = pl.BlockSpec((tm, D), lambda i: (i, 0))
    vec = pl.BlockSpec((1, D), lambda i: (0, 0))
    return _pc(body, name=name, grid=(R // tm,), in_specs=[row, vec, row], out_specs=(row, vec, vec),
               out_shape=(jax.ShapeDtypeStruct((R, D), F32), jax.ShapeDtypeStruct((1, D), F32),
                          jax.ShapeDtypeStruct((1, D), F32)),
               compiler_params=_sem("arbitrary"))(x, g, tgt)


def _rope_tables(S):
    n_rows = S // GRID_W
    row = jnp.repeat(jnp.arange(n_rows, dtype=F32), GRID_W)
    col = jnp.tile(jnp.arange(GRID_W, dtype=F32), n_rows)
    inv_freq = ROPE_THETA ** (-jnp.arange(ROPE_PAIRS, dtype=F32) / ROPE_PAIRS)
    ang = jnp.stack([row[:, None] * inv_freq, col[:, None] * inv_freq], axis=1)
    cos, sin = jnp.cos(ang), jnp.sin(ang)
    c = jnp.broadcast_to(cos[:, :, None, :], (S, 2, 2, ROPE_PAIRS)).reshape(S, HEAD_DIM)
    s = jnp.stack([-sin, sin], axis=2).reshape(S, HEAD_DIM)
    reps = LANES // HEAD_DIM
    return jnp.tile(c, (1, reps)), jnp.tile(s, (1, reps))


def _head_mean_matrix():
    h = jnp.arange(LANES) // HEAD_DIM
    m = jnp.where(h[:, None] == h[None, :], 1.0 / HEAD_DIM, 0.0).astype(BF16)
    return jnp.concatenate([m, m], axis=0)


def _head_mean(v, bd):
    hi = v.astype(BF16)
    lo = (v - hi.astype(F32)).astype(BF16)
    return jnp.dot(jnp.concatenate([hi, lo], axis=1), bd, preferred_element_type=F32)


def _swap_halves(y):
    lane = lax.broadcasted_iota(jnp.int32, y.shape, 1)
    return jnp.where(lane % 32 < 16, pltpu.roll(y, LANES - 16, 1), pltpu.roll(y, 16, 1))


def _qk_rope(name, qkv, gain, scale, cos, sin, bd, n_rot):
    S, W = qkv.shape
    tm = _tile(S, 2048, 16)
    per = LANES // HEAD_DIM

    def body(x_ref, g_ref, s_ref, c_ref, sn_ref, bd_ref, o_ref):
        j = pl.program_id(1)
        xv = x_ref[...]

        def put(v):
            for h in range(per):
                o_ref[h] = v[:, h * HEAD_DIM:(h + 1) * HEAD_DIM].astype(BF16)

        @pl.when(j < n_rot)
        def _():
            ms = _head_mean(xv * xv, bd_ref[...])
            y = xv * lax.rsqrt(ms + EPS) * g_ref[...] * s_ref[...]
            put(y * c_ref[...] + _swap_halves(y) * sn_ref[...])

        @pl.when(j >= n_rot)
        def _():
            put(xv)

    blk = pl.BlockSpec((tm, LANES), lambda i, j: (i, j))
    vec = pl.BlockSpec((1, LANES), lambda i, j: (0, j))
    tab = pl.BlockSpec((tm, LANES), lambda i, j: (i, 0))
    return _pc(body, name=name, grid=(S // tm, W // LANES),
               in_specs=[blk, vec, vec, tab, tab, pl.BlockSpec((2 * LANES, LANES), lambda i, j: (0, 0))],
               out_specs=pl.BlockSpec((per, tm, HEAD_DIM), lambda i, j: (j, i, 0)),
               out_shape=jax.ShapeDtypeStruct((W // HEAD_DIM, S, HEAD_DIM), BF16),
               compiler_params=_sem("parallel", "parallel"))(qkv, gain, scale, cos, sin, bd)


def _qk_rope_bwd(name, dq, dk, dv, qkv, gain, scale, cos, sin, bd):
    S, W = qkv.shape
    tm = _tile(S, 2048, 16)
    per = LANES // HEAD_DIM
    nq, nk, nv = dq.shape[0] // per, dk.shape[0] // per, dv.shape[0] // per
    n_rot = nq + nk

    def body(dq_ref, dk_ref, dv_ref, x_ref, g_ref, s_ref, c_ref, sn_ref, bd_ref, dx_ref, dg_ref):
        j, i = pl.program_id(0), pl.program_id(1)

        @pl.when(i == 0)
        def _():
            dg_ref[...] = jnp.zeros_like(dg_ref)

        def rotate_back(d_ref):
            dv = jnp.concatenate([d_ref[h] for h in range(per)], axis=1)
            xv = x_ref[...]
            ms = _head_mean(xv * xv, bd_ref[...])
            r = lax.rsqrt(ms + EPS)
            z = xv * r
            dy = (dv * c_ref[...] - _swap_halves(dv) * sn_ref[...]) * s_ref[...]
            dg_ref[...] += jnp.sum(dy * z, axis=0, keepdims=True)
            dz = dy * g_ref[...]
            mz = _head_mean(dz * z, bd_ref[...])
            dx_ref[...] = (r * (dz - z * mz)).astype(BF16)

        @pl.when(j < nq)
        def _():
            rotate_back(dq_ref)

        @pl.when((j >= nq) & (j < n_rot))
        def _():
            rotate_back(dk_ref)

        @pl.when(j >= n_rot)
        def _():
            dx_ref[...] = jnp.concatenate([dv_ref[h] for h in range(per)], axis=1).astype(BF16)

    def part(first, count):
        return pl.BlockSpec((per, tm, HEAD_DIM), lambda j, i: (jnp.clip(j - first, 0, count - 1), i, 0))

    blk = pl.BlockSpec((tm, LANES), lambda j, i: (i, j))
    vec = pl.BlockSpec((1, LANES), lambda j, i: (0, j))
    tab = pl.BlockSpec((tm, LANES), lambda j, i: (i, 0))
    return _pc(body, name=name, grid=(W // LANES, S // tm),
               in_specs=[part(0, nq), part(nq, nk), part(n_rot, nv), blk, vec, vec, tab, tab,
                         pl.BlockSpec((2 * LANES, LANES), lambda j, i: (0, 0))],
               out_specs=(blk, vec),
               out_shape=(jax.ShapeDtypeStruct((S, W), BF16), jax.ShapeDtypeStruct((1, W), F32)),
               compiler_params=_sem("parallel", "arbitrary"))(dq, dk, dv, qkv, gain, scale, cos, sin, bd)


def _softmax_rows(s):
    m = jnp.max(s, axis=-1, keepdims=True)
    p = jnp.exp(s - m)
    return p, jnp.sum(p, axis=-1, keepdims=True)


def _attn_fwd(name, qkv, H):
    _, S, dh = qkv.shape
    G = H // N_KV_HEADS
    tq = _tile(S, 256, 16)
    kc = _tile(S, KEY_CHUNK, LANES)
    R = G * tq

    def body(q_ref, k_ref, v_ref, o_ref, lse_ref):
        q = q_ref[...].reshape(R, dh)
        m = jnp.full((R, 1), -1e30, F32)
        l = jnp.zeros((R, 1), F32)
        acc = jnp.zeros((R, dh), F32)
        for c in range(S // kc):
            rows = slice(c * kc, (c + 1) * kc)
            s = lax.dot_general(q, k_ref[rows, :], _NT, preferred_element_type=F32)
            m_new = jnp.maximum(m, jnp.max(s, axis=-1, keepdims=True))
            alpha = jnp.exp(m - m_new)
            p = jnp.exp(s - m_new)
            l = alpha * l + jnp.sum(p, axis=-1, keepdims=True)
            acc = alpha * acc + jnp.dot(p.astype(BF16), v_ref[rows, :], preferred_element_type=F32)
            m = m_new
        o_ref[...] = (acc / l).astype(BF16).reshape(G, tq, dh)
        lse_ref[...] = (m + jnp.log(l)).reshape(G, tq, 1)

    qs = pl.BlockSpec((G, tq, dh), lambda kv, i: (kv, i, 0))
    ls = pl.BlockSpec((G, tq, 1), lambda kv, i: (kv, i, 0))
    ks = pl.BlockSpec((None, S, dh), lambda kv, i: (H + kv, 0, 0))
    vs = pl.BlockSpec((None, S, dh), lambda kv, i: (H + N_KV_HEADS + kv, 0, 0))
    return _pc(body, name=name, grid=(N_KV_HEADS, S // tq), in_specs=[qs, ks, vs], out_specs=(qs, ls),
               out_shape=(jax.ShapeDtypeStruct((H, S, dh), BF16), jax.ShapeDtypeStruct((H, S, 1), F32)),
               compiler_params=_sem("parallel", "parallel"))(qkv, qkv, qkv)


def _attn_bwd(name, qkv, o, lse, do):
    H, S, dh = o.shape
    G = H // N_KV_HEADS
    tq = _tile(S, 128, 16)
    kc = _tile(S, KEY_CHUNK, LANES)
    R = G * tq

    def body(q_ref, k_ref, v_ref, o_ref, lse_ref, do_ref, dq_ref, dk_ref, dv_ref):
        @pl.when(pl.program_id(1) == 0)
        def _():
            dk_ref[...] = jnp.zeros_like(dk_ref)
            dv_ref[...] = jnp.zeros_like(dv_ref)

        qq, dd = q_ref[...].reshape(R, dh), do_ref[...].reshape(R, dh)
        delta = jnp.sum(dd.astype(F32) * o_ref[...].reshape(R, dh).astype(F32), axis=-1, keepdims=True)
        lse = lse_ref[...].reshape(R, 1)
        dq = jnp.zeros((R, dh), F32)
        for c in range(S // kc):
            rows = slice(c * kc, (c + 1) * kc)
            kk, vv = k_ref[rows, :], v_ref[rows, :]
            p = jnp.exp(lax.dot_general(qq, kk, _NT, preferred_element_type=F32) - lse)
            dv_ref[rows, :] += lax.dot_general(p.astype(BF16), dd, _TN, preferred_element_type=F32)
            dp = lax.dot_general(dd, vv, _NT, preferred_element_type=F32)
            ds = (p * (dp - delta)).astype(BF16)
            dq = dq + jnp.dot(ds, kk, preferred_element_type=F32)
            dk_ref[rows, :] += lax.dot_general(ds, qq, _TN, preferred_element_type=F32)
        dq_ref[...] = dq.reshape(G, tq, dh)

    qs = pl.BlockSpec((G, tq, dh), lambda kv, i: (kv, i, 0))
    ls = pl.BlockSpec((G, tq, 1), lambda kv, i: (kv, i, 0))
    ks = pl.BlockSpec((None, S, dh), lambda kv, i: (H + kv, 0, 0))
    vs = pl.BlockSpec((None, S, dh), lambda kv, i: (H + N_KV_HEADS + kv, 0, 0))
    acc = pl.BlockSpec((None, S, dh), lambda kv, i: (kv, 0, 0))
    return _pc(body, name=name, grid=(N_KV_HEADS, S // tq), in_specs=[qs, ks, vs, qs, ls, qs],
               out_specs=(qs, acc, acc),
               out_shape=(jax.ShapeDtypeStruct((H, S, dh), F32), jax.ShapeDtypeStruct((N_KV_HEADS, S, dh), F32),
                          jax.ShapeDtypeStruct((N_KV_HEADS, S, dh), F32)),
               compiler_params=_sem("parallel", "arbitrary"))(qkv, qkv, qkv, o, lse, do)


def _xattn_fwd(name, q, kv):
    S, D = q.shape
    _, M, dh = kv.shape
    scale = dh ** -0.5
    tq = _tile(S, 512, 16)

    def body(q_ref, kv_ref, o_ref):
        for h in range(X_HEADS):
            lo, hi = h * dh, (h + 1) * dh
            s = lax.dot_general(q_ref[:, lo:hi], kv_ref[h], _NT, preferred_element_type=F32) * scale
            p, l = _softmax_rows(s)
            o = jnp.dot(p.astype(BF16), kv_ref[X_HEADS + h], preferred_element_type=F32)
            o_ref[:, lo:hi] = (o / l).astype(BF16)

    row = pl.BlockSpec((tq, D), lambda i: (i, 0))
    return _pc(body, name=name, grid=(S // tq,),
               in_specs=[row, pl.BlockSpec((2 * X_HEADS, M, dh), lambda i: (0, 0, 0))],
               out_specs=row, out_shape=jax.ShapeDtypeStruct((S, D), BF16),
               compiler_params=_sem("parallel"))(q, kv)


def _xattn_bwd(name, q, kv, do):
    S, D = q.shape
    _, M, dh = kv.shape
    scale = dh ** -0.5
    tq = _tile(S, 512, 16)

    def body(q_ref, kv_ref, do_ref, dq_ref, dkv_ref):
        @pl.when(pl.program_id(0) == 0)
        def _():
            dkv_ref[...] = jnp.zeros_like(dkv_ref)

        for h in range(X_HEADS):
            lo, hi = h * dh, (h + 1) * dh
            qh, kh, vh, doh = q_ref[:, lo:hi], kv_ref[h], kv_ref[X_HEADS + h], do_ref[:, lo:hi]
            s = lax.dot_general(qh, kh, _NT, preferred_element_type=F32) * scale
            p, l = _softmax_rows(s)
            pn = p / l
            dkv_ref[X_HEADS + h] += lax.dot_general(pn.astype(BF16), doh, _TN, preferred_element_type=F32)
            dp = lax.dot_general(doh, vh, _NT, preferred_element_type=F32)
            ds = (pn * (dp - jnp.sum(pn * dp, axis=-1, keepdims=True)) * scale).astype(BF16)
            dq_ref[:, lo:hi] = jnp.dot(ds, kh, preferred_element_type=F32).astype(BF16)
            dkv_ref[h] += lax.dot_general(ds, qh, _TN, preferred_element_type=F32)

    row = pl.BlockSpec((tq, D), lambda i: (i, 0))
    full = pl.BlockSpec((2 * X_HEADS, M, dh), lambda i: (0, 0, 0))
    return _pc(body, name=name, grid=(S // tq,), in_specs=[row, full, row], out_specs=(row, full),
               out_shape=(jax.ShapeDtypeStruct((S, D), BF16), jax.ShapeDtypeStruct((2 * X_HEADS, M, dh), F32)),
               compiler_params=_sem("arbitrary"))(q, kv, do)


def _sigmoid(x):
    return 1.0 / (1.0 + jnp.exp(-x))


def _halo_specs(tm, n, S):
    nb = tm // 8
    last8 = S // 8 - 1
    main = pl.BlockSpec((2, None, tm, n), lambda j, i: (0, j, i, 0))
    prev = pl.BlockSpec((2, None, 8, n), lambda j, i: (0, j, jnp.maximum(i * nb - 1, 0), 0))
    nxt = pl.BlockSpec((2, None, 8, n), lambda j, i: (0, j, jnp.minimum((i + 1) * nb, last8), 0))
    return main, prev, nxt


def _ffn_up_act(name, h, w, cw, cb, layer):
    S, K = h.shape
    _, J, _, n = w.shape
    tm = _tile(S, 512, FFN_HALO)
    nblk = S // tm
    hb, last = tm // FFN_HALO, S // FFN_HALO - 1
    te = tm + 2 * FFN_HALO

    def body(h_ref, hp_ref, hn_ref, w_ref, cw_ref, b_ref, u_ref, a_ref):
        i = pl.program_id(1)
        zero = jnp.zeros((FFN_HALO, K), BF16)
        he = jnp.concatenate([jnp.where(i == 0, zero, hp_ref[...]), h_ref[...],
                              jnp.where(i == nblk - 1, zero, hn_ref[...])], axis=0)
        mid = slice(FFN_HALO, tm + FFN_HALO)
        c = []
        for half in range(2):
            ue = jnp.dot(he, w_ref[half], preferred_element_type=F32)
            um = ue[mid]
            u_ref[half] = um
            k = cw_ref[half]
            c.append(pltpu.roll(ue, 1, 0)[mid] * k[0:1] + um * k[1:2] + pltpu.roll(ue, te - 1, 0)[mid] * k[2:3]
                     + b_ref[half])
        a_ref[...] = (c[0] * _sigmoid(c[0]) * c[1]).astype(BF16)

    return _pc(body, name=name, grid=(J, nblk),
               in_specs=[pl.BlockSpec((tm, K), lambda j, i: (i, 0)),
                         pl.BlockSpec((FFN_HALO, K), lambda j, i: (jnp.maximum(i * hb - 1, 0), 0)),
                         pl.BlockSpec((FFN_HALO, K), lambda j, i: (jnp.minimum((i + 1) * hb, last), 0)),
                         pl.BlockSpec((2, None, K, n), lambda j, i: (0, j, layer, 0)),
                         pl.BlockSpec((2, None, 3, n), lambda j, i: (0, j, 0, 0)),
                         pl.BlockSpec((2, None, 1, n), lambda j, i: (0, j, 0, 0))],
               out_specs=(pl.BlockSpec((2, None, tm, n), lambda j, i: (0, j, i, 0)),
                          pl.BlockSpec((None, tm, n), lambda j, i: (j, i, 0))),
               out_shape=(jax.ShapeDtypeStruct((2, J, S, n), F32), jax.ShapeDtypeStruct((J, S, n), BF16)),
               compiler_params=_sem("parallel", "parallel"))(h, h, h, w, cw, cb)


def _ffn_act_bwd(name, u, g, w_down, cw, cb):
    _, J, S, n = u.shape
    D = g.shape[1]
    tm = _tile(S, 256, 16)
    nblk = S // tm
    te = tm + 16
    nb = tm // 8
    last8 = S // 8 - 1

    def body(u_ref, up_ref, un_ref, g_ref, gp_ref, gn_ref, wd_ref, w_ref, b_ref, du_ref, st_ref):
        i = pl.program_id(1)

        @pl.when(i == 0)
        def _():
            st_ref[...] = jnp.zeros_like(st_ref)

        def extended(before, main, after):
            return jnp.concatenate([jnp.where(i == 0, 0.0, before), main, jnp.where(i == nblk - 1, 0.0, after)], axis=0)

        mid = slice(8, tm + 8)
        da_e = lax.dot_general(extended(gp_ref[...], g_ref[...], gn_ref[...]).astype(BF16), wd_ref[...], _NT,
                               preferred_element_type=F32)
        ue, c = [], []
        for half in range(2):
            e = extended(up_ref[half], u_ref[half], un_ref[half])
            w = w_ref[half]
            ue.append((pltpu.roll(e, 1, 0), e, pltpu.roll(e, te - 1, 0)))
            c.append(ue[half][0] * w[0:1] + e * w[1:2] + ue[half][2] * w[2:3] + b_ref[half])
        sg = _sigmoid(c[0])
        dc = [da_e * c[1] * (sg * (1.0 + c[0] * (1.0 - sg))), da_e * (c[0] * sg)]
        r8 = lax.broadcasted_iota(jnp.int32, (8, n), 0)
        for half in range(2):
            w, d, (e_before, e, e_after) = w_ref[half], dc[half], ue[half]
            dm = d[mid]
            du = pltpu.roll(d, te - 1, 0)[mid] * w[0:1] + dm * w[1:2] + pltpu.roll(d, 1, 0)[mid] * w[2:3]
            du_ref[half] = du.astype(BF16)
            s0 = jnp.sum(dm * e_before[mid], axis=0, keepdims=True)
            s1 = jnp.sum(dm * e[mid], axis=0, keepdims=True)
            s2 = jnp.sum(dm * e_after[mid], axis=0, keepdims=True)
            s3 = jnp.sum(dm, axis=0, keepdims=True)
            st_ref[half] += jnp.where(r8 == 0, s0, jnp.where(r8 == 1, s1, jnp.where(r8 == 2, s2,
                                      jnp.where(r8 == 3, s3, 0.0))))

    main, prev, nxt = _halo_specs(tm, n, S)
    gmain = pl.BlockSpec((tm, D), lambda j, i: (i, 0))
    gprev = pl.BlockSpec((8, D), lambda j, i: (jnp.maximum(i * nb - 1, 0), 0))
    gnxt = pl.BlockSpec((8, D), lambda j, i: (jnp.minimum((i + 1) * nb, last8), 0))
    return _pc(body, name=name, grid=(J, nblk),
               in_specs=[main, prev, nxt, gmain, gprev, gnxt, pl.BlockSpec((n, D), lambda j, i: (j, 0)),
                         pl.BlockSpec((2, None, 3, n), lambda j, i: (0, j, 0, 0)),
                         pl.BlockSpec((2, None, 1, n), lambda j, i: (0, j, 0, 0))],
               out_specs=(main, pl.BlockSpec((2, None, 8, n), lambda j, i: (0, j, 0, 0))),
               out_shape=(jax.ShapeDtypeStruct((2, J, S, n), BF16), jax.ShapeDtypeStruct((2, J, 8, n), F32)),
               compiler_params=_sem("parallel", "arbitrary"))(u, u, u, g, g, g, w_down, cw, cb)


def _window_count(t, w, S):
    lo = jnp.maximum(t - w // 2, 0)
    hi = jnp.minimum(t + w - w // 2, S)
    return (hi - lo).astype(F32)


def _trailing_sums(x, w):
    k = 1
    while k < w:
        x = x + pltpu.roll(x, k, 0)
        k *= 2
    return x


def _pool_window(name, h, group_w, adjoint, out_dtype):
    S, D = h.shape
    SP = S + 2 * POOL_PAD
    per_group = group_w // LANES

    def body(h_ref, o_ref, xp):
        g = pl.program_id(0) // per_group
        t = lax.broadcasted_iota(jnp.int32, (S, LANES), 0)
        xp[0:POOL_PAD, :] = jnp.zeros((POOL_PAD, LANES), F32)
        xp[S + POOL_PAD:SP, :] = jnp.zeros((POOL_PAD, LANES), F32)
        for gi, w in enumerate(POOL_WINDOWS):
            @pl.when(g == gi)
            def _():
                hv = h_ref[...]
                cnt = _window_count(t, w, S)
                xp[POOL_PAD:S + POOL_PAD, :] = hv / cnt if adjoint else hv
                ahead = w // 2 if adjoint else w // 2 - 1
                sw = _trailing_sums(xp[...], w)
                if ahead:
                    sw = pltpu.roll(sw, SP - ahead, 0)
                win = sw[POOL_PAD:S + POOL_PAD]
                o_ref[...] = ((win if adjoint else win / cnt) - hv).astype(out_dtype)

    col = pl.BlockSpec((S, LANES), lambda j: (0, j))
    return _pc(body, name=name, grid=(D // LANES,), in_specs=[col], out_specs=col,
               out_shape=jax.ShapeDtypeStruct((S, D), out_dtype),
               scratch_shapes=[pltpu.VMEM((SP, LANES), F32)], compiler_params=_sem("parallel"))(h)


def _pool_proj(name, mixed, w, scale, res):
    S, D = mixed.shape
    G, gw, _ = w.shape
    tm = _tile(S, 512, 16)

    def body(m_ref, w_ref, s_ref, r_ref, o_ref):
        for g in range(G):
            lo, hi = g * gw, (g + 1) * gw
            y = jnp.dot(m_ref[:, lo:hi], w_ref[g], preferred_element_type=F32)
            o_ref[:, lo:hi] = r_ref[:, lo:hi] + y * s_ref[:, lo:hi]

    row = pl.BlockSpec((tm, D), lambda i: (i, 0))
    return _pc(body, name=name, grid=(S // tm,),
               in_specs=[row, pl.BlockSpec((G, gw, gw), lambda i: (0, 0, 0)), pl.BlockSpec((1, D), lambda i: (0, 0)), row],
               out_specs=row, out_shape=jax.ShapeDtypeStruct((S, D), F32),
               compiler_params=_sem("parallel"))(mixed, w, scale, res)


def _pool_proj_bwd(name, dy, mixed, w, scale):
    S, D = mixed.shape
    G, gw, _ = w.shape
    tm = _tile(S, 512, 16)

    def body(dy_ref, m_ref, w_ref, s_ref, dm_ref, dw_ref, ds_ref):
        @pl.when(pl.program_id(0) == 0)
        def _():
            dw_ref[...] = jnp.zeros_like(dw_ref)
            ds_ref[...] = jnp.zeros_like(ds_ref)

        for g in range(G):
            lo, hi = g * gw, (g + 1) * gw
            mg, dyg = m_ref[:, lo:hi], dy_ref[:, lo:hi]
            y = jnp.dot(mg, w_ref[g], preferred_element_type=F32)
            ds_ref[:, lo:hi] += jnp.sum(dyg * y, axis=0, keepdims=True)
            dyp = (dyg * s_ref[:, lo:hi]).astype(BF16)
            dm_ref[:, lo:hi] = lax.dot_general(dyp, w_ref[g], _NT, preferred_element_type=F32)
            dw_ref[g] += lax.dot_general(mg, dyp, _TN, preferred_element_type=F32)

    row = pl.BlockSpec((tm, D), lambda i: (i, 0))
    wsp = pl.BlockSpec((G, gw, gw), lambda i: (0, 0, 0))
    vec = pl.BlockSpec((1, D), lambda i: (0, 0))
    return _pc(body, name=name, grid=(S // tm,), in_specs=[row, row, wsp, vec], out_specs=(row, wsp, vec),
               out_shape=(jax.ShapeDtypeStruct((S, D), F32), jax.ShapeDtypeStruct((G, gw, gw), F32),
                          jax.ShapeDtypeStruct((1, D), F32)),
               compiler_params=_sem("arbitrary"))(dy, mixed, w, scale)


def _adamw(name, w, g, m, v):
    shape = w.shape
    C = shape[-1]
    R = w.size // C
    tm = _tile(R, 512, 8)

    def body(w_ref, g_ref, m_ref, v_ref, d_ref, nm_ref, nv_ref):
        gv = g_ref[...]
        nm = ADAM_B1 * m_ref[...] + (1.0 - ADAM_B1) * gv
        nv = ADAM_B2 * v_ref[...] + (1.0 - ADAM_B2) * (gv * gv)
        m_hat = nm / (1.0 - ADAM_B1 ** ADAM_STEP)
        v_hat = nv / (1.0 - ADAM_B2 ** ADAM_STEP)
        d_ref[...] = -ADAM_LR * (m_hat / (jnp.sqrt(v_hat) + ADAM_EPS) + ADAM_WD * w_ref[...])
        nm_ref[...] = nm
        nv_ref[...] = nv

    blk = pl.BlockSpec((tm, C), lambda i: (i, 0))
    sd = jax.ShapeDtypeStruct((R, C), F32)
    outs = _pc(body, name=name, grid=(R // tm,), in_specs=[blk] * 4, out_specs=(blk,) * 3, out_shape=(sd,) * 3,
               compiler_params=_sem("parallel"))(*(a.reshape(R, C) for a in (w, g, m, v)))
    return tuple(o.reshape(shape) for o in outs)


def _position():
    return lax.axis_index("x"), lax.axis_index("y"), lax.axis_index("c")


def _flip(v, bit):
    return 1 - v if bit else v


def _allgather_small(name, v):
    R, W = v.shape

    def body(v_ref, out_ref, send_sems, recv_sems):
        x, y, c = _position()
        me = 4 * x + 2 * y + c
        out_ref[me] = v_ref[...]
        sends = []
        for k in range(1, N_DEV):
            peer = (_flip(x, k & 4), _flip(y, k & 2), _flip(c, k & 1))
            cp = pltpu.make_async_remote_copy(src_ref=v_ref, dst_ref=out_ref.at[me], send_sem=send_sems.at[k - 1],
                                              recv_sem=recv_sems.at[k - 1], device_id=peer, device_id_type=MESH)
            cp.start()
            sends.append(cp)
        for k in range(1, N_DEV):
            peer = (_flip(x, k & 4), _flip(y, k & 2), _flip(c, k & 1))
            slot = 4 * peer[0] + 2 * peer[1] + peer[2]
            pltpu.make_async_remote_copy(src_ref=v_ref, dst_ref=out_ref.at[slot], send_sem=send_sems.at[k - 1],
                                         recv_sem=recv_sems.at[k - 1], device_id=peer, device_id_type=MESH).wait_recv()
        for cp in sends:
            cp.wait_send()

    vm = pl.BlockSpec(memory_space=pltpu.VMEM)
    return _pc(body, name=name, in_specs=[vm], out_specs=vm, out_shape=jax.ShapeDtypeStruct((N_DEV, R, W), F32),
               scratch_shapes=[pltpu.SemaphoreType.DMA((N_DEV - 1,)), pltpu.SemaphoreType.DMA((N_DEV - 1,))])(v)


def _sum_slots(name, a):
    n, R, W = a.shape

    def body(a_ref, o_ref):
        acc = a_ref[0]
        for s in range(1, n):
            acc = acc + a_ref[s]
        o_ref[...] = acc

    return _pc(body, name=name, grid=(1,), in_specs=[pl.BlockSpec((n, R, W), lambda i: (0, 0, 0))],
               out_specs=pl.BlockSpec((R, W), lambda i: (0, 0)), out_shape=jax.ShapeDtypeStruct((R, W), F32))(a)


def _allgather_blocks(name, blocks):
    n = len(blocks)

    def body(*refs):
        b_refs, out_refs, token = refs[:n], refs[n:2 * n], refs[2 * n]
        send_sems, recv_sems, local_sems = refs[2 * n + 1:]
        token[...] = jnp.zeros_like(token)
        x, y, c = _position()
        me, sibling = (x, y, c), (x, y, 1 - c)
        chips = [(1 - x, y), (x, 1 - y), (1 - x, 1 - y)]

        def slot(i, px, py, pc):
            return out_refs[i].at[4 * px + 2 * py + pc]

        def copy(i, k, block, to, src=None):
            return pltpu.make_async_remote_copy(src_ref=slot(i, *block) if src is None else src, dst_ref=slot(i, *block),
                                                send_sem=send_sems.at[k, i], recv_sem=recv_sems.at[k, i],
                                                device_id=to, device_id_type=MESH)

        mine = [pltpu.make_async_copy(b_refs[i], slot(i, *me), local_sems.at[i]) for i in range(n)]
        first = [copy(i, 1 + j, me, (*chip, c), src=b_refs[i]) for i in range(n) for j, chip in enumerate(chips)]
        first += [copy(i, 0, me, sibling, src=b_refs[i]) for i in range(n)]
        for cp in mine + first:
            cp.start()
        passed = []
        for j, chip in enumerate(chips):
            for i in range(n):
                copy(i, 1 + j, (*chip, c), me).wait_recv()
                passed.append(copy(i, 4 + j, (*chip, c), sibling))
                passed[-1].start()
        for i in range(n):
            copy(i, 0, sibling, me).wait_recv()
        for j, chip in enumerate(chips):
            for i in range(n):
                copy(i, 4 + j, (*chip, 1 - c), me).wait_recv()
        for cp in first + passed:
            cp.wait_send()
        for cp in mine:
            cp.wait()

    hbm = pl.BlockSpec(memory_space=pl.ANY)
    return _pc(body, name=name, in_specs=[hbm] * n, out_specs=[hbm] * n + [pl.BlockSpec(memory_space=pltpu.VMEM)],
               out_shape=[jax.ShapeDtypeStruct((N_DEV,) + b.shape, b.dtype) for b in blocks]
               + [jax.ShapeDtypeStruct((8, LANES), F32)],
               scratch_shapes=[pltpu.SemaphoreType.DMA((7, n)), pltpu.SemaphoreType.DMA((7, n)),
                               pltpu.SemaphoreType.DMA((n,))])(*blocks)


def _add_sibling(name, g4, r1, pos):
    n, _, L, W = g4.shape
    tl = _tile(L, 512, 16)

    def body(pos_ref, g_ref, r_ref, tb_ref, own_ref):
        t = g_ref[...] + r_ref[...]
        tb_ref[...] = t.astype(BF16)

        @pl.when(pl.program_id(1) == pos_ref[1])
        def _():
            own_ref[...] = t

    gs = pltpu.PrefetchScalarGridSpec(
        num_scalar_prefetch=1, grid=(L // tl, n),
        in_specs=[pl.BlockSpec((None, None, tl, W), lambda i, k, p: (k, p[0], i, 0)),
                  pl.BlockSpec((None, tl, W), lambda i, k, p: (k, i, 0))],
        out_specs=(pl.BlockSpec((None, tl, W), lambda i, k, p: (k, i, 0)),
                   pl.BlockSpec((tl, W), lambda i, k, p: (i, 0))))
    return _pc(body, name=name, grid_spec=gs,
               out_shape=(jax.ShapeDtypeStruct((n, L, W), BF16), jax.ShapeDtypeStruct((L, W), F32)),
               compiler_params=_sem("parallel", "arbitrary"))(pos, g4, r1)


def _add_chips(name, own, r2):
    L, W = own.shape
    tl = _tile(L, 512, 16)

    def body(o_ref, r_ref, out_ref):
        acc = o_ref[...]
        for j in range(3):
            acc = acc + r_ref[j].astype(F32)
        out_ref[...] = acc

    return _pc(body, name=name, grid=(L // tl,),
               in_specs=[pl.BlockSpec((tl, W), lambda i: (i, 0)), pl.BlockSpec((3, tl, W), lambda i: (0, i, 0))],
               out_specs=pl.BlockSpec((tl, W), lambda i: (i, 0)), out_shape=jax.ShapeDtypeStruct((L, W), F32),
               compiler_params=_sem("parallel"))(own, r2)


_HBM = pl.BlockSpec(memory_space=pltpu.HBM)
_SEM = pl.BlockSpec(memory_space=pltpu.SEMAPHORE)
_EFFECT = pltpu.SideEffectType.DATAFLOW_SIDE_EFFECTING


def _in_hbm(a):
    return pltpu.with_memory_space_constraint(a, pltpu.HBM)


def _after(x, token):
    return x + token[0, 0].astype(x.dtype)


def _copies_start(name, bufs, sem_shape, plan):
    nb = len(bufs)

    def body(*refs):
        for cp in plan(refs[:nb], refs[nb], refs[nb + 1]):
            cp.start()
        refs[-1][...] = jnp.zeros_like(refs[-1])

    out = _pc(body, name=name, in_specs=[_HBM] * nb,
              out_specs=(_SEM, _SEM, *[_HBM] * nb, pl.BlockSpec(memory_space=pltpu.VMEM)),
              out_shape=(pltpu.SemaphoreType.DMA(sem_shape), pltpu.SemaphoreType.DMA(sem_shape),
                         *[pltpu.HBM(b.shape, b.dtype) for b in bufs], jax.ShapeDtypeStruct((8, LANES), F32)),
              input_output_aliases={i: 2 + i for i in range(nb)},
              compiler_params=pltpu.CompilerParams(has_side_effects=_EFFECT))(*[_in_hbm(b) for b in bufs])
    return out[0], out[1], list(out[2:2 + nb]), out[-1]


def _copies_wait(name, bufs, send_sems, recv_sems, plan, after):
    nb = len(bufs)

    def body(*refs):
        for cp in plan(refs[:nb], refs[nb], refs[nb + 1]):
            cp.wait_send()
            cp.wait_recv()

    return list(_pc(body, name=name, in_specs=[_HBM] * nb + [_SEM, _SEM, pl.BlockSpec(memory_space=pl.ANY)],
                    out_specs=[_HBM] * nb, out_shape=[pltpu.HBM(b.shape, b.dtype) for b in bufs],
                    input_output_aliases={i: i for i in range(nb)},
                    compiler_params=pltpu.CompilerParams(has_side_effects=_EFFECT))(*bufs, send_sems, recv_sems, after))


def _plan_gather_chips(n):
    def plan(refs, send_sems, recv_sems):
        x, y, c = _position()
        peers = [(x, y, 1 - c), (1 - x, y, c), (x, 1 - y, c), (1 - x, 1 - y, c)]
        return [pltpu.make_async_remote_copy(src_ref=refs[i], dst_ref=refs[n + i].at[4 * x + 2 * y + c],
                                             send_sem=send_sems.at[k * n + i], recv_sem=recv_sems.at[k * n + i],
                                             device_id=peer, device_id_type=MESH)
                for i in range(n) for k, peer in enumerate(peers)]
    return plan


def _plan_gather_sibling(n):
    def plan(refs, send_sems, recv_sems):
        x, y, c = _position()
        slots = [4 * (1 - x) + 2 * y + c, 4 * x + 2 * (1 - y) + c, 4 * (1 - x) + 2 * (1 - y) + c]
        return [pltpu.make_async_remote_copy(src_ref=refs[i].at[s], dst_ref=refs[i].at[s],
                                             send_sem=send_sems.at[k * n + i], recv_sem=recv_sems.at[k * n + i],
                                             device_id=(x, y, 1 - c), device_id_type=MESH)
                for i in range(n) for k, s in enumerate(slots)]
    return plan


def _plan_reduce_sibling(n):
    def plan(refs, send_sems, recv_sems):
        x, y, c = _position()
        return [pltpu.make_async_remote_copy(src_ref=refs[i].at[k, 1 - c], dst_ref=refs[n + i].at[k],
                                             send_sem=send_sems.at[k * n + i], recv_sem=recv_sems.at[k * n + i],
                                             device_id=(x, y, 1 - c), device_id_type=MESH)
                for i in range(n) for k in range(N_DEV // 2)]
    return plan


def _plan_reduce_chips(n):
    def plan(refs, send_sems, recv_sems):
        x, y, c = _position()
        cps = []
        for i in range(n):
            for j in range(1, 4):
                px, py = _flip(x, j & 2), _flip(y, j & 1)
                sem = (j - 1) * n + i
                cps.append(pltpu.make_async_remote_copy(src_ref=refs[i].at[2 * px + py], dst_ref=refs[n + i].at[j - 1],
                                                        send_sem=send_sems.at[sem], recv_sem=recv_sems.at[sem],
                                                        device_id=(px, py, c), device_id_type=MESH))
        return cps
    return plan


def _heads_major(name, a, after):
    S, W = a.shape
    H = W // HEAD_DIM
    tm = _tile(S, 512, 16)

    def body(a_ref, after_ref, o_ref):
        v = a_ref[...]
        for h in range(H):
            o_ref[h] = v[:, h * HEAD_DIM:(h + 1) * HEAD_DIM]

    return _pc(body, name=name, grid=(S // tm,),
               in_specs=[pl.BlockSpec((tm, W), lambda i: (i, 0)), pl.BlockSpec(memory_space=pl.ANY)],
               out_specs=pl.BlockSpec((H, tm, HEAD_DIM), lambda i: (0, i, 0)),
               out_shape=jax.ShapeDtypeStruct((H, S, HEAD_DIM), a.dtype), compiler_params=_sem("parallel"))(a, after)


def _heads_minor(name, a):
    H, S, _ = a.shape
    tm = _tile(S, 512, 16)

    def body(a_ref, o_ref):
        o_ref[...] = jnp.concatenate([a_ref[h] for h in range(H)], axis=1)

    return _pc(body, name=name, grid=(S // tm,), in_specs=[pl.BlockSpec((H, tm, HEAD_DIM), lambda i: (0, i, 0))],
               out_specs=pl.BlockSpec((tm, H * HEAD_DIM), lambda i: (i, 0)),
               out_shape=jax.ShapeDtypeStruct((S, H * HEAD_DIM), a.dtype), compiler_params=_sem("parallel"))(a)


def kernel(x, mem, attn_norm, attn_w_qkv, attn_q_gain, attn_k_gain, attn_w_o, pool_norm, pool_w, pool_scale, xattn_norm, mem_norm, xattn_w_q, xattn_w_kv, xattn_w_o, ffn_norm, ffn_w_up, ffn_conv_w, ffn_conv_b, ffn_w_down, final_norm, loss_target, m_attn_norm, m_attn_w_qkv, m_attn_q_gain, m_attn_k_gain, m_attn_w_o, m_pool_norm, m_pool_w, m_pool_scale, m_xattn_norm, m_mem_norm, m_xattn_w_q, m_xattn_w_kv, m_xattn_w_o, m_ffn_norm, m_ffn_w_up, m_ffn_conv_w, m_ffn_conv_b, m_ffn_w_down, m_final_norm, v_attn_norm, v_attn_w_qkv, v_attn_q_gain, v_attn_k_gain, v_attn_w_o, v_pool_norm, v_pool_w, v_pool_scale, v_xattn_norm, v_mem_norm, v_xattn_w_q, v_xattn_w_kv, v_xattn_w_o, v_ffn_norm, v_ffn_w_up, v_ffn_conv_w, v_ffn_conv_b, v_ffn_w_down, v_final_norm):
    names = ['attn_norm', 'attn_w_qkv', 'attn_q_gain', 'attn_k_gain', 'attn_w_o', 'pool_norm', 'pool_w', 'pool_scale',
             'xattn_norm', 'mem_norm', 'xattn_w_q', 'xattn_w_kv', 'xattn_w_o', 'ffn_norm', 'ffn_w_up', 'ffn_conv_w',
             'ffn_conv_b', 'ffn_w_down', 'final_norm']
    W = dict(zip(names, (attn_norm, attn_w_qkv, attn_q_gain, attn_k_gain, attn_w_o, pool_norm, pool_w, pool_scale,
                         xattn_norm, mem_norm, xattn_w_q, xattn_w_kv, xattn_w_o, ffn_norm, ffn_w_up, ffn_conv_w,
                         ffn_conv_b, ffn_w_down, final_norm)))
    Mo = dict(zip(names, (m_attn_norm, m_attn_w_qkv, m_attn_q_gain, m_attn_k_gain, m_attn_w_o, m_pool_norm, m_pool_w,
                          m_pool_scale, m_xattn_norm, m_mem_norm, m_xattn_w_q, m_xattn_w_kv, m_xattn_w_o, m_ffn_norm,
                          m_ffn_w_up, m_ffn_conv_w, m_ffn_conv_b, m_ffn_w_down, m_final_norm)))
    Vo = dict(zip(names, (v_attn_norm, v_attn_w_qkv, v_attn_q_gain, v_attn_k_gain, v_attn_w_o, v_pool_norm, v_pool_w,
                          v_pool_scale, v_xattn_norm, v_mem_norm, v_xattn_w_q, v_xattn_w_kv, v_xattn_w_o, v_ffn_norm,
                          v_ffn_w_up, v_ffn_conv_w, v_ffn_conv_b, v_ffn_w_down, v_final_norm)))

    S, D = x.shape[1], x.shape[2]
    n_layers = xattn_norm.shape[0]
    n_up = ffn_w_up.shape[2]
    qkv_w = attn_w_qkv.shape[2] * N_DEV
    n_heads = qkv_w // HEAD_DIM - 2 * N_KV_HEADS
    n_rot = (n_heads + N_KV_HEADS) * HEAD_DIM // LANES
    group_w = pool_w.shape[3]
    xs, mems, tgt = x[0], mem[0], loss_target[0]
    xi, yi, ci = _position()
    dev = 4 * xi + 2 * yi + ci
    pos = jnp.stack([ci, 2 * xi + yi]).astype(jnp.int32)

    layers = range(n_layers)
    n_groups = pool_w.shape[1]
    small_vec = jnp.concatenate([pool_norm.reshape(-1), pool_scale.reshape(-1), ffn_conv_w.reshape(-1)])
    small_rows = _round_up(-(-small_vec.size // PACK_W), 8)
    small_vec = jnp.pad(small_vec, (0, small_rows * PACK_W - small_vec.size)).reshape(small_rows, PACK_W)
    w_qkv, w_o, small, attn_token = _allgather_blocks(
        "allgather_attn", [attn_w_qkv[0].astype(BF16), attn_w_o[0].astype(BF16), small_vec])
    small = small.reshape(N_DEV, -1)
    w_qkv = w_qkv.transpose(1, 0, 2).reshape(D, qkv_w)
    w_o = w_o.reshape(-1, D)
    blocks = [pool_w.reshape(-1, group_w)] + [xattn_w_q[l] for l in layers] + [xattn_w_kv.reshape(n_layers * D, -1)]
    blocks += [xattn_w_o[l] for l in layers] + [ffn_w_up.reshape(n_layers * D, n_up)] + [ffn_w_down[l] for l in layers]
    blocks = [b.astype(BF16) for b in blocks]
    blocks[0] = _after(blocks[0], attn_token)
    n_blk = len(blocks)
    lands = [lax.dynamic_update_index_in_dim(lax.empty((N_DEV,) + b.shape, BF16), b, dev, 0) for b in blocks]
    plan_chips, plan_sibling = _plan_gather_chips(n_blk), _plan_gather_sibling(n_blk)
    gather_sems = _copies_start("gather_chips_start", blocks + lands, (4 * n_blk,), plan_chips)
    attn_norm_late = _after(attn_norm, gather_sems[3])

    d_sh = pool_norm.shape[1]
    pool_norm_f = small[:, :d_sh].reshape(1, D)
    pool_scale_f = small[:, d_sh:2 * d_sh].reshape(1, D)
    conv_w_f = small[:, 2 * d_sh:2 * d_sh + ffn_conv_w.size].reshape(N_DEV, n_layers, 3, n_up)
    conv_b_f = ffn_conv_b.reshape(n_layers, N_DEV, 1, n_up)

    cos, sin = _rope_tables(S)
    bd = _head_mean_matrix()
    pad_w = qkv_w - (n_heads + N_KV_HEADS) * HEAD_DIM
    qk_gain = jnp.concatenate([jnp.tile(attn_q_gain[0], n_heads), jnp.tile(attn_k_gain[0], N_KV_HEADS),
                               jnp.ones((pad_w,), F32)]).reshape(1, qkv_w)
    qk_scale = jnp.concatenate([jnp.full((n_heads * HEAD_DIM,), HEAD_DIM ** -0.5, F32),
                                jnp.ones((qkv_w - n_heads * HEAD_DIM,), F32)]).reshape(1, qkv_w)

    saved = []

    def xattn_ffn_fwd(l, xin, hx=None):
        if hx is None:
            hx = _rmsnorm(f"xattn_norm{l}", xin, xattn_norm[l:l + 1], BF16)
        memn = _rmsnorm(f"mem_norm{l}", mems, mem_norm[l:l + 1], BF16)
        qx = _mm_nn(f"xattn_q{l}", hx, w_xq[l], BF16)
        kv = _mm_nn_bs(f"xattn_kv{l}", memn, w_xkv, BF16, l)
        ox = _xattn_fwd(f"xattn_fwd{l}", qx, kv)
        x2 = _mm_nn(f"xattn_o{l}", ox, w_xo[l], F32, res=xin)
        hf = _rmsnorm(f"ffn_norm{l}", x2, ffn_norm[l:l + 1], BF16)
        cw = conv_w_f[:, l].reshape(2, N_DEV // 2, 3, n_up)
        cb = conv_b_f[l].reshape(2, N_DEV // 2, 1, n_up)
        u, act = _ffn_up_act(f"ffn_up_act{l}", hf, w_up.reshape(2, N_DEV // 2, n_layers * D, n_up), cw, cb, l)
        x3 = _mm_nn_as(f"ffn_down{l}", act, w_down[l], F32, x2)
        saved.append(dict(xin=xin, hx=hx, memn=memn, qx=qx, kv=kv, ox=ox, x2=x2, hf=hf, u=u, cw=cw, cb=cb, act=act))
        return x3

    h0 = _rmsnorm("attn_norm", xs, attn_norm_late, BF16)
    qkv = _mm_nn("attn_qkv", h0, w_qkv, F32)
    qkr = _qk_rope("qk_rope", qkv, qk_gain, qk_scale, cos, sin, bd, n_rot)
    o_hm, lse = _attn_fwd("attn_fwd", qkr, n_heads)
    o_att = _heads_minor("attn_heads_minor", o_hm)
    arrived = _copies_wait("gather_chips_wait", gather_sems[2], gather_sems[0], gather_sems[1], plan_chips, o_att)
    pass_sems = _copies_start("gather_sibling_start", arrived[n_blk:], (3 * n_blk,), plan_sibling)
    x1 = _mm_nn("attn_o", o_att, _after(w_o, pass_sems[3]), F32, res=xs)
    hx0 = _rmsnorm("xattn_norm0", x1, xattn_norm[0:1], BF16)
    gathered = iter(_copies_wait("gather_sibling_wait", pass_sems[2], pass_sems[0], pass_sems[1], plan_sibling, hx0))
    w_pool = (next(gathered).reshape(N_DEV, n_groups, -1, group_w).transpose(1, 0, 2, 3)
              .reshape(n_groups, group_w, group_w))
    w_xq = [next(gathered).reshape(D, D) for l in layers]
    w_xkv = next(gathered)
    w_xo = [next(gathered).reshape(D, D) for l in layers]
    w_up = next(gathered)
    w_down = [next(gathered).reshape(-1, D) for l in layers]
    x3 = xattn_ffn_fwd(0, x1, hx0)
    hp = _rmsnorm("pool_norm", x3, pool_norm_f, F32)
    mixed = _pool_window("pool_window", hp, group_w, False, BF16)
    x4 = _pool_proj("pool_proj", mixed, w_pool, pool_scale_f, x3)
    x6 = xattn_ffn_fwd(1, x4)

    G = {}
    g, d_final, lvec = _loss_head("loss_head", x6, final_norm.reshape(1, D), tgt)
    G['final_norm'] = d_final.reshape(D)
    loss_part = (0.5 * jnp.sum(lvec) / D).reshape(1)

    d_xn, d_mn, d_fn, d_xq, d_xkv, d_xo, d_up, d_cw, d_cb, d_down = ([None] * n_layers for _ in range(10))

    def xattn_ffn_bwd(l, g, conv_b_late=None, after_act=None):
        sv = saved[l]
        d_down[l] = _mm_tn_as(f"ffn_down_dw{l}", sv['act'], g, F32)
        du, st = _ffn_act_bwd(f"ffn_act_bwd{l}", sv['u'], g, w_down[l], sv['cw'],
                              sv['cb'] if conv_b_late is None else conv_b_late)
        du = du.reshape(N_DEV, S, n_up)
        ffn_gain = ffn_norm[l:l + 1] if after_act is None else _after(ffn_norm[l:l + 1], after_act(du))
        st = st.reshape(N_DEV, 8, n_up)
        d_cw[l], d_cb[l] = st[:, 0:3], st[:, 3].reshape(-1)
        d_up[l] = _mm_tn_bs(f"ffn_up_dw{l}", sv['hf'], du, F32)
        g, d_fn[l] = _mm_nt_abs_norm_bwd(f"ffn_up_dx_norm_bwd{l}", du, w_up, sv['x2'], ffn_gain, g, l)
        d_xo[l] = _mm_tn(f"xattn_o_dw{l}", sv['ox'], g, F32)
        do = _mm_nt(f"xattn_o_dx{l}", g, w_xo[l], BF16)
        dq, dkv = _xattn_bwd(f"xattn_bwd{l}", sv['qx'], sv['kv'], do)
        d_xq[l] = _mm_tn(f"xattn_q_dw{l}", sv['hx'], dq, F32)
        d_xkv[l] = _mm_tn_bs(f"xattn_kv_dw{l}", sv['memn'], dkv, F32)
        dmemn = _mm_nt_abs(f"xattn_kv_dx{l}", dkv, w_xkv, D, F32, l)
        _, d_mn[l] = _rmsnorm_bwd(f"mem_norm_bwd{l}", mems, mem_norm[l:l + 1], dmemn)
        g, d_xn[l] = _mm_nt_norm_bwd(f"xattn_q_dx_norm_bwd{l}", dq, w_xq[l], sv['xin'], xattn_norm[l:l + 1], g)
        return g

    def reduce_start(tag, bufs):
        n = len(bufs)
        g4s = [b.reshape((N_DEV // 2, 2) + b.shape[1:]) for b in bufs]
        lands = [lax.empty((N_DEV // 2,) + b.shape[1:], F32) for b in bufs]
        plan = _plan_reduce_sibling(n)
        return (n, plan) + _copies_start(f"reduce_sibling_start_{tag}", g4s + lands, (N_DEV // 2 * n,), plan)

    def reduce_between(tag, state, after):
        n, plan, send_sems, recv_sems, thru, _ = state
        got = _copies_wait(f"reduce_sibling_wait_{tag}", thru, send_sems, recv_sems, plan, after)
        sums = [_add_sibling(f"reduce_add_sibling_{tag}{i}", got[i], got[n + i], pos) for i in range(n)]
        lands = [lax.empty((3,) + tb.shape[1:], BF16) for tb, _ in sums]
        plan = _plan_reduce_chips(n)
        return (n, plan, [own for _, own in sums]) + _copies_start(f"reduce_chips_start_{tag}",
                                                                    [tb for tb, _ in sums] + lands, (3 * n,), plan)

    def reduce_finish(tag, state, after):
        n, plan, owns, send_sems, recv_sems, thru, _ = state
        got = _copies_wait(f"reduce_chips_wait_{tag}", thru, send_sems, recv_sems, plan, after)
        return [_add_chips(f"reduce_add_chips_{tag}{i}", owns[i], got[n + i]) for i in range(n)]

    def layer_bufs(l):
        return [d_xq[l].reshape(N_DEV, -1, D), d_xkv[l], d_xo[l].reshape(N_DEV, -1, D), d_up[l],
                d_down[l].reshape(N_DEV, -1, D)]

    g = xattn_ffn_bwd(1, g)
    d_mixed, d_pool_w, d_pool_scale = _pool_proj_bwd("pool_proj_bwd", g, mixed, w_pool, pool_scale_f)
    dhp = _pool_window("pool_window_bwd", d_mixed, group_w, True, F32)
    g, d_pool_norm = _rmsnorm_bwd("pool_norm_bwd", x3, pool_norm_f, dhp, g)
    upper = reduce_start("upper", [d_pool_w.reshape(n_groups, N_DEV, -1, group_w).transpose(1, 0, 2, 3)
                                   .reshape(N_DEV, -1, group_w)] + layer_bufs(1))
    between = []

    def upper_between(du):
        between.append(reduce_between("upper", upper, du))
        return between[0][-1]

    g = xattn_ffn_bwd(0, g, _after(saved[0]['cb'], upper[-1]), upper_between)
    lower = reduce_start("lower", layer_bufs(0))
    d_wo = _mm_tn("attn_o_dw", o_att, g, F32)
    do = _mm_nt("attn_o_dx", g, _after(w_o, lower[-1]), BF16)
    lower = reduce_between("lower", lower, do)
    do_hm = _heads_major("attn_heads_major", do, lower[-1])
    dq_hm, dk_hm, dv_hm = _attn_bwd("attn_bwd", qkr, o_hm, lse, do_hm)
    red_lower = reduce_finish("lower", lower, dq_hm)
    d_qkv, d_gain = _qk_rope_bwd("qk_rope_bwd", dq_hm, dk_hm, dv_hm, qkv, qk_gain, qk_scale, cos, sin, bd)
    red_upper = reduce_finish("upper", between[0], d_qkv)
    d_wqkv = _mm_tn("attn_qkv_dw", h0, d_qkv, F32)
    last = reduce_start("last", [d_wqkv.reshape(D, N_DEV, -1).transpose(1, 0, 2), d_wo.reshape(N_DEV, -1, D)])
    last = reduce_between("last", last, d_wqkv)
    grad_x, d_attn_norm = _mm_nt_norm_bwd("attn_qkv_dx_norm_bwd", d_qkv, w_qkv, xs, attn_norm, g, last[-1])
    G['pool_w'] = red_upper[0].reshape(pool_w.shape)
    per_layer = [red_lower, red_upper[1:]]
    for i, n in enumerate(['xattn_w_q', 'xattn_w_kv', 'xattn_w_o', 'ffn_w_up', 'ffn_w_down']):
        G[n] = jnp.stack([per_layer[l][i] for l in layers])

    hq = n_heads * HEAD_DIM
    small_g = {'attn_norm': d_attn_norm, 'attn_q_gain': d_gain[0, :hq].reshape(n_heads, HEAD_DIM).sum(0),
               'attn_k_gain': d_gain[0, hq:hq + N_KV_HEADS * HEAD_DIM].reshape(N_KV_HEADS, HEAD_DIM).sum(0),
               'pool_norm': d_pool_norm, 'pool_scale': d_pool_scale,
               'xattn_norm': jnp.concatenate(d_xn), 'mem_norm': jnp.concatenate(d_mn), 'ffn_norm': jnp.concatenate(d_fn),
               'ffn_conv_w': jnp.stack(d_cw, axis=1), 'ffn_conv_b': jnp.stack(d_cb)}
    order = list(small_g)
    flat = jnp.concatenate([loss_part] + [small_g[n].reshape(-1) for n in order] + [G['final_norm']])
    ar_rows = _round_up(-(-flat.size // PACK_W), 8)
    flat = jnp.pad(flat, (0, ar_rows * PACK_W - flat.size)).reshape(ar_rows, PACK_W)
    summed = _sum_slots("allreduce_sum", _allgather_small("allreduce_gather", flat)).reshape(-1)
    loss = summed[0]
    red_last = reduce_finish("last", last, summed)
    G['attn_w_qkv'], G['attn_w_o'] = red_last[0][None], red_last[1][None]
    at = 1
    for n in order + ['final_norm']:
        size = G['final_norm'].size if n == 'final_norm' else small_g[n].size
        piece = summed[at:at + size]
        at += size
        if n in ('pool_norm', 'pool_scale'):
            piece = lax.dynamic_slice(piece, (dev * d_sh,), (d_sh,))
        elif n == 'ffn_conv_w':
            piece = lax.dynamic_index_in_dim(piece.reshape(N_DEV, n_layers, 3, n_up), dev, 0, keepdims=False)
        G[n] = piece.reshape(W[n].shape)

    deltas, new_m, new_v = [], [], []
    for n in names:
        d, nm, nv = _adamw(f"adamw_{n}", W[n], G[n], Mo[n], Vo[n])
        deltas.append(d)
        new_m.append(nm)
        new_v.append(nv)
    return (loss, grad_x[None], *[G[n] for n in names], *deltas, *new_m, *new_v)
```

```python
import jax
import jax.numpy as jnp
from jax import lax
from jax.experimental import pallas as pl
from jax.experimental.pallas import tpu as pltpu

F32 = jnp.float32
BF16 = jnp.bfloat16
MESH = pl.DeviceIdType.MESH

N_DEV = 8
EPS = 1e-6
HEAD_DIM = 64
N_KV_HEADS = 4
X_HEADS = 4
GRID_W = 64
ROPE_THETA = 10000.0
ROPE_PAIRS = HEAD_DIM // 4
POOL_WINDOWS = (2, 4, 8, 16)
POOL_PAD = 16
KEY_CHUNK = 1024
MM_ROWS = 1024
REDUCE_ROWS = 2048
FFN_HALO = 16
LANES = 128
PACK_W = 1024
ADAM_LR, ADAM_B1, ADAM_B2, ADAM_EPS, ADAM_WD, ADAM_STEP = 0.001, 0.9, 0.999, 1e-08, 0.01, 10

_NN = (((1,), (0,)), ((), ()))
_NT = (((1,), (1,)), ((), ()))
_TN = (((0,), (0,)), ((), ()))


def _pc(body, *, name, **kw):
    return pl.pallas_call(body, name=name, **kw)


def _sem(*kinds):
    return pltpu.CompilerParams(dimension_semantics=kinds)


def _tile(n, pref, mult):
    best = None
    for t in range(mult, min(n, pref) + 1, mult):
        if n % t == 0:
            best = t
    return n if best is None else best


def _round_up(n, m):
    return (n + m - 1) // m * m


def _mm_call(name, a, b, dims, grid, a_spec, b_spec, o_spec, out_shape, kaxis, res=None, res_spec=None):
    nk = grid[kaxis]
    acc_shape = tuple(d for d in o_spec.block_shape if d is not None)
    in_place = out_shape.dtype == F32
    use_scratch = nk > 1 and not in_place

    def body(*refs):
        refs = list(refs)
        acc = refs.pop() if use_scratch else None
        a_ref, b_ref = refs[:2]
        r_ref = refs[2] if res is not None else None
        o_ref = refs[-1]
        if len(a_ref.shape) == 3:
            n = a_ref.shape[2]
            prod = sum(jnp.dot(a_ref[s].astype(BF16), b_ref[s * n:(s + 1) * n, :].astype(BF16),
                               preferred_element_type=F32) for s in range(a_ref.shape[0]))
        else:
            prod = lax.dot_general(a_ref[...].astype(BF16), b_ref[...].astype(BF16), dims, preferred_element_type=F32)
        if nk == 1:
            if r_ref is not None:
                prod = prod + r_ref[...]
            o_ref[...] = prod.astype(o_ref.dtype)
            return
        k = pl.program_id(kaxis)
        tgt = o_ref if in_place else acc

        @pl.when(k == 0)
        def _():
            tgt[...] = prod + r_ref[...] if (in_place and r_ref is not None) else prod

        @pl.when(k > 0)
        def _():
            tgt[...] += prod

        if not in_place:
            @pl.when(k == nk - 1)
            def _():
                r = acc[...]
                if r_ref is not None:
                    r = r + r_ref[...]
                o_ref[...] = r.astype(o_ref.dtype)

    sem = tuple("arbitrary" if ax == kaxis else "parallel" for ax in range(len(grid)))
    ins = [a, b] if res is None else [a, b, res]
    specs = [a_spec, b_spec] if res is None else [a_spec, b_spec, res_spec]
    return _pc(body, name=name, grid=grid, in_specs=specs, out_specs=o_spec, out_shape=out_shape,
               scratch_shapes=[pltpu.VMEM(acc_shape, F32)] if use_scratch else [],
               compiler_params=_sem(*sem))(*ins)


def _reduce_rows(a, b):
    return REDUCE_ROWS * (2 if a.dtype == BF16 and b.dtype == BF16 else 1)


def _mm_nn(name, a, b, out_dtype, res=None):
    M, K = a.shape
    N = b.shape[1]
    tm, tn, tk = _tile(M, MM_ROWS, 16), _tile(N, 1024, LANES), _tile(K, 1024, LANES)
    return _mm_call(name, a, b, _NN, (M // tm, N // tn, K // tk),
                    pl.BlockSpec((tm, tk), lambda i, j, k: (i, k)),
                    pl.BlockSpec((tk, tn), lambda i, j, k: (k, j)),
                    pl.BlockSpec((tm, tn), lambda i, j, k: (i, j)),
                    jax.ShapeDtypeStruct((M, N), out_dtype), 2, res,
                    pl.BlockSpec((tm, tn), lambda i, j, k: (i, j)))


def _mm_nt(name, a, b, out_dtype):
    M, K = a.shape
    N = b.shape[0]
    tm, tn, tk = _tile(M, MM_ROWS, 16), _tile(N, 1024, LANES), _tile(K, 1024, LANES)
    return _mm_call(name, a, b, _NT, (M // tm, N // tn, K // tk),
                    pl.BlockSpec((tm, tk), lambda i, j, k: (i, k)),
                    pl.BlockSpec((tn, tk), lambda i, j, k: (j, k)),
                    pl.BlockSpec((tm, tn), lambda i, j, k: (i, j)),
                    jax.ShapeDtypeStruct((M, N), out_dtype), 2)


def _mm_tn(name, a, b, out_dtype):
    R, M = a.shape
    N = b.shape[1]
    tm, tn, tr = _tile(M, 1024, LANES), _tile(N, 1024, LANES), _tile(R, _reduce_rows(a, b), 16)
    return _mm_call(name, a, b, _TN, (M // tm, N // tn, R // tr),
                    pl.BlockSpec((tr, tm), lambda i, j, k: (k, i)),
                    pl.BlockSpec((tr, tn), lambda i, j, k: (k, j)),
                    pl.BlockSpec((tm, tn), lambda i, j, k: (i, j)),
                    jax.ShapeDtypeStruct((M, N), out_dtype), 2)


def _mm_nn_bs(name, a, b, out_dtype, layer=0):
    M, K = a.shape
    J, _, n = b.shape
    tm, tk = _tile(M, MM_ROWS, 16), _tile(K, 1024, LANES)
    first = layer * (K // tk)
    return _mm_call(name, a, b, _NN, (J, M // tm, K // tk),
                    pl.BlockSpec((tm, tk), lambda j, i, k: (i, k)),
                    pl.BlockSpec((None, tk, n), lambda j, i, k: (j, first + k, 0)),
                    pl.BlockSpec((None, tm, n), lambda j, i, k: (j, i, 0)),
                    jax.ShapeDtypeStruct((J, M, n), out_dtype), 2)


def _mm_nn_as(name, a, b, out_dtype, res):
    J, M, n = a.shape
    N = b.shape[1]
    tm, tn = _tile(M, MM_ROWS, 16), _tile(N, 1024, LANES)
    per = J if J <= 4 else (2 if J % 2 == 0 else 1)
    return _mm_call(name, a, b, _NN, (M // tm, N // tn, J // per),
                    pl.BlockSpec((per, tm, n), lambda i, j, k: (k, i, 0)),
                    pl.BlockSpec((per * n, tn), lambda i, j, k: (k, j)),
                    pl.BlockSpec((tm, tn), lambda i, j, k: (i, j)),
                    jax.ShapeDtypeStruct((M, N), out_dtype), 2, res,
                    pl.BlockSpec((tm, tn), lambda i, j, k: (i, j)))


def _mm_tn_as(name, a, b, out_dtype):
    J, R, n = a.shape
    N = b.shape[1]
    tn, tr = _tile(N, 1024, LANES), _tile(R, REDUCE_ROWS, 16)
    return _mm_call(name, a, b, _TN, (J, N // tn, R // tr),
                    pl.BlockSpec((None, tr, n), lambda j, jn, k: (j, k, 0)),
                    pl.BlockSpec((tr, tn), lambda j, jn, k: (k, jn)),
                    pl.BlockSpec((n, tn), lambda j, jn, k: (j, jn)),
                    jax.ShapeDtypeStruct((J * n, N), out_dtype), 2)


def _mm_nt_abs(name, a, b, N, out_dtype, layer=0):
    J, M, n = a.shape
    tm, tn = _tile(M, MM_ROWS, 16), _tile(N, 1024, LANES)
    first = layer * (N // tn)
    return _mm_call(name, a, b, _NT, (M // tm, N // tn, J),
                    pl.BlockSpec((None, tm, n), lambda i, j, k: (k, i, 0)),
                    pl.BlockSpec((None, tn, n), lambda i, j, k: (k, first + j, 0)),
                    pl.BlockSpec((tm, tn), lambda i, j, k: (i, j)),
                    jax.ShapeDtypeStruct((M, N), out_dtype), 2)


def _mm_tn_bs(name, a, b, out_dtype):
    R, M = a.shape
    J, _, n = b.shape
    tm, tr = _tile(M, 1024, LANES), _tile(R, _reduce_rows(a, b), 16)
    return _mm_call(name, a, b, _TN, (J, M // tm, R // tr),
                    pl.BlockSpec((tr, tm), lambda j, i, k: (k, i)),
                    pl.BlockSpec((None, tr, n), lambda j, i, k: (j, k, 0)),
                    pl.BlockSpec((None, tm, n), lambda j, i, k: (j, i, 0)),
                    jax.ShapeDtypeStruct((J, M, n), out_dtype), 2)


def _rmsnorm(name, x, g, out_dtype):
    R, D = x.shape
    tm = _tile(R, 512, 16)

    def body(x_ref, g_ref, o_ref):
        xv = x_ref[...]
        r = lax.rsqrt(jnp.mean(xv * xv, axis=-1, keepdims=True) + EPS)
        o_ref[...] = (xv * r * g_ref[...]).astype(o_ref.dtype)

    return _pc(body, name=name, grid=(R // tm,),
               in_specs=[pl.BlockSpec((tm, D), lambda i: (i, 0)), pl.BlockSpec((1, D), lambda i: (0, 0))],
               out_specs=pl.BlockSpec((tm, D), lambda i: (i, 0)),
               out_shape=jax.ShapeDtypeStruct((R, D), out_dtype), compiler_params=_sem("parallel"))(x, g)


def _rmsnorm_bwd(name, x, g, dh, dres=None):
    R, D = x.shape
    tm = _tile(R, 512, 16)

    def body(*refs):
        if dres is None:
            x_ref, g_ref, dh_ref, dx_ref, dg_ref = refs
            dres_ref = None
        else:
            x_ref, g_ref, dh_ref, dres_ref, dx_ref, dg_ref = refs
        xv = x_ref[...]
        r = lax.rsqrt(jnp.mean(xv * xv, axis=-1, keepdims=True) + EPS)
        xh = xv * r
        dhv = dh_ref[...].astype(F32)

        @pl.when(pl.program_id(0) == 0)
        def _():
            dg_ref[...] = jnp.zeros_like(dg_ref)

        dg_ref[...] += jnp.sum(dhv * xh, axis=0, keepdims=True)
        dxh = dhv * g_ref[...]
        dx = r * (dxh - xh * jnp.mean(dxh * xh, axis=-1, keepdims=True))
        if dres_ref is not None:
            dx = dx + dres_ref[...]
        dx_ref[...] = dx

    row = pl.BlockSpec((tm, D), lambda i: (i, 0))
    vec = pl.BlockSpec((1, D), lambda i: (0, 0))
    ins = [x, g, dh] + ([] if dres is None else [dres])
    specs = [row, vec, row] + ([] if dres is None else [row])
    return _pc(body, name=name, grid=(R // tm,), in_specs=specs, out_specs=(row, vec),
               out_shape=(jax.ShapeDtypeStruct((R, D), F32), jax.ShapeDtypeStruct((1, D), F32)),
               compiler_params=_sem("arbitrary"))(*ins)


def _mm_norm_bwd(name, a, b, grid, a_spec, b_spec, tm, x, gain, dres, after=None):
    M, D = x.shape
    nk = grid[2]

    def body(*refs):
        a_ref, b_ref, x_ref, g_ref, r_ref = refs[:5]
        dx_ref, dg_ref = refs[-3 if nk > 1 else -2:][:2]
        acc = refs[-1] if nk > 1 else None
        i, k = pl.program_id(0), pl.program_id(2)
        if len(a_ref.shape) == 3:
            prod = sum(lax.dot_general(a_ref[s].astype(BF16), b_ref[s].astype(BF16), _NT, preferred_element_type=F32)
                       for s in range(a_ref.shape[0]))
        else:
            prod = lax.dot_general(a_ref[...].astype(BF16), b_ref[...].astype(BF16), _NT, preferred_element_type=F32)

        @pl.when((i == 0) & (k == 0))
        def _():
            dg_ref[...] = jnp.zeros_like(dg_ref)

        def finish(dh):
            xv = x_ref[...]
            r = lax.rsqrt(jnp.mean(xv * xv, axis=-1, keepdims=True) + EPS)
            xh = xv * r
            dg_ref[...] += jnp.sum(dh * xh, axis=0, keepdims=True)
            dxh = dh * g_ref[...]
            dx_ref[...] = r * (dxh - xh * jnp.mean(dxh * xh, axis=-1, keepdims=True)) + r_ref[...]

        if nk == 1:
            finish(prod)
            return

        @pl.when(k == 0)
        def _():
            acc[...] = prod

        @pl.when(k > 0)
        def _():
            acc[...] += prod

        @pl.when(k == nk - 1)
        def _():
            finish(acc[...])

    row = pl.BlockSpec((tm, D), lambda i, j, k: (i, 0))
    vec = pl.BlockSpec((1, D), lambda i, j, k: (0, 0))
    ins = [a, b, x, gain, dres] + ([] if after is None else [after])
    specs = [a_spec, b_spec, row, vec, row] + ([] if after is None else [pl.BlockSpec(memory_space=pl.ANY)])
    return _pc(body, name=name, grid=grid, in_specs=specs, out_specs=(row, vec),
               out_shape=(jax.ShapeDtypeStruct((M, D), F32), jax.ShapeDtypeStruct((1, D), F32)),
               scratch_shapes=[pltpu.VMEM((tm, D), F32)] if nk > 1 else [],
               compiler_params=_sem("arbitrary", "arbitrary", "arbitrary"))(*ins)


def _mm_nt_norm_bwd(name, a, b, x, gain, dres, after=None):
    M, K = a.shape
    D = b.shape[0]
    tm, tk = _tile(M, MM_ROWS, 16), _tile(K, 1024, LANES)
    return _mm_norm_bwd(name, a, b, (M // tm, 1, K // tk),
                        pl.BlockSpec((tm, tk), lambda i, j, k: (i, k)),
                        pl.BlockSpec((D, tk), lambda i, j, k: (0, k)), tm, x, gain, dres, after)


def _mm_nt_abs_norm_bwd(name, a, b, x, gain, dres, layer=0):
    J, M, n = a.shape
    D = x.shape[1]
    tm = _tile(M, MM_ROWS, 16)
    per = 2 if J % 2 == 0 else 1
    return _mm_norm_bwd(name, a, b, (M // tm, 1, J // per),
                        pl.BlockSpec((per, tm, n), lambda i, j, k: (k, i, 0)),
                        pl.BlockSpec((per, D, n), lambda i, j, k: (k, layer, 0)), tm, x, gain, dres)


def _loss_head(name, x, g, tgt):
    R, D = x.shape
    tm = _tile(R, 512, 16)

    def body(x_ref, g_ref, t_ref, dx_ref, dg_ref, l_ref):
        xv = x_ref[...]
        r = lax.rsqrt(jnp.mean(xv * xv, axis=-1, keepdims=True) + EPS)
        xh = xv * r
        err = xh * g_ref[...] - t_ref[...]

        @pl.when(pl.program_id(0) == 0)
        def _():
            dg_ref[...] = jnp.zeros_like(dg_ref)
            l_ref[...] = jnp.zeros_like(l_ref)

        l_ref[...] += jnp.sum(err * err, axis=0, keepdims=True)
        dy = err * (1.0 / D)
        dg_ref[...] += jnp.sum(dy * xh, axis=0, keepdims=True)
        dxh = dy * g_ref[...]
        dx_ref[...] = r * (dxh - xh * jnp.mean(dxh * xh, axis=-1, keepdims=True))

    row = pl.BlockSpec((tm, D), lambda i: (i, 0))
    vec = pl.BlockSpec((1, D), lambda i: (0, 0))
    return _pc(body, name=name, grid=(R // tm,), in_specs=[row, vec, row], out_specs=(row, vec, vec),
               out_shape=(jax.ShapeDtypeStruct((R, D), F32), jax.ShapeDtypeStruct((1, D), F32),
                          jax.ShapeDtypeStruct((1, D), F32)),
               compiler_params=_sem("arbitrary"))(x, g, tgt)


def _rope_tables(S):
    n_rows = S // GRID_W
    row = jnp.repeat(jnp.arange(n_rows, dtype=F32), GRID_W)
    col = jnp.tile(jnp.arange(GRID_W, dtype=F32), n_rows)
    inv_freq = ROPE_THETA ** (-jnp.arange(ROPE_PAIRS, dtype=F32) / ROPE_PAIRS)
    ang = jnp.stack([row[:, None] * inv_freq, col[:, None] * inv_freq], axis=1)
    cos, sin = jnp.cos(ang), jnp.sin(ang)
    c = jnp.broadcast_to(cos[:, :, None, :], (S, 2, 2, ROPE_PAIRS)).reshape(S, HEAD_DIM)
    s = jnp.stack([-sin, sin], axis=2).reshape(S, HEAD_DIM)
    reps = LANES // HEAD_DIM
    return jnp.tile(c, (1, reps)), jnp.tile(s, (1, reps))


def _head_mean_matrix():
    h = jnp.arange(LANES) // HEAD_DIM
    m = jnp.where(h[:, None] == h[None, :], 1.0 / HEAD_DIM, 0.0).astype(BF16)
    return jnp.concatenate([m, m], axis=0)


def _head_mean(v, bd):
    hi = v.astype(BF16)
    lo = (v - hi.astype(F32)).astype(BF16)
    return jnp.dot(jnp.concatenate([hi, lo], axis=1), bd, preferred_element_type=F32)


def _swap_halves(y):
    lane = lax.broadcasted_iota(jnp.int32, y.shape, 1)
    return jnp.where(lane % 32 < 16, pltpu.roll(y, LANES - 16, 1), pltpu.roll(y, 16, 1))


def _qk_rope(name, qkv, gain, scale, cos, sin, bd, n_rot):
    S, W = qkv.shape
    tm = _tile(S, 2048, 16)
    per = LANES // HEAD_DIM

    def body(x_ref, g_ref, s_ref, c_ref, sn_ref, bd_ref, o_ref):
        j = pl.program_id(1)
        xv = x_ref[...]

        def put(v):
            for h in range(per):
                o_ref[h] = v[:, h * HEAD_DIM:(h + 1) * HEAD_DIM].astype(BF16)

        @pl.when(j < n_rot)
        def _():
            ms = _head_mean(xv * xv, bd_ref[...])
            y = xv * lax.rsqrt(ms + EPS) * g_ref[...] * s_ref[...]
            put(y * c_ref[...] + _swap_halves(y) * sn_ref[...])

        @pl.when(j >= n_rot)
        def _():
            put(xv)

    blk = pl.BlockSpec((tm, LANES), lambda i, j: (i, j))
    vec = pl.BlockSpec((1, LANES), lambda i, j: (0, j))
    tab = pl.BlockSpec((tm, LANES), lambda i, j: (i, 0))
    return _pc(body, name=name, grid=(S // tm, W // LANES),
               in_specs=[blk, vec, vec, tab, tab, pl.BlockSpec((2 * LANES, LANES), lambda i, j: (0, 0))],
               out_specs=pl.BlockSpec((per, tm, HEAD_DIM), lambda i, j: (j, i, 0)),
               out_shape=jax.ShapeDtypeStruct((W // HEAD_DIM, S, HEAD_DIM), BF16),
               compiler_params=_sem("parallel", "parallel"))(qkv, gain, scale, cos, sin, bd)


def _qk_rope_bwd(name, dq, dk, dv, qkv, gain, scale, cos, sin, bd):
    S, W = qkv.shape
    tm = _tile(S, 2048, 16)
    per = LANES // HEAD_DIM
    nq, nk, nv = dq.shape[0] // per, dk.shape[0] // per, dv.shape[0] // per
    n_rot = nq + nk

    def body(dq_ref, dk_ref, dv_ref, x_ref, g_ref, s_ref, c_ref, sn_ref, bd_ref, dx_ref, dg_ref):
        j, i = pl.program_id(0), pl.program_id(1)

        @pl.when(i == 0)
        def _():
            dg_ref[...] = jnp.zeros_like(dg_ref)

        def rotate_back(d_ref):
            dv = jnp.concatenate([d_ref[h] for h in range(per)], axis=1)
            xv = x_ref[...]
            ms = _head_mean(xv * xv, bd_ref[...])
            r = lax.rsqrt(ms + EPS)
            z = xv * r
            dy = (dv * c_ref[...] - _swap_halves(dv) * sn_ref[...]) * s_ref[...]
            dg_ref[...] += jnp.sum(dy * z, axis=0, keepdims=True)
            dz = dy * g_ref[...]
            mz = _head_mean(dz * z, bd_ref[...])
            dx_ref[...] = (r * (dz - z * mz)).astype(BF16)

        @pl.when(j < nq)
        def _():
            rotate_back(dq_ref)

        @pl.when((j >= nq) & (j < n_rot))
        def _():
            rotate_back(dk_ref)

        @pl.when(j >= n_rot)
        def _():
            dx_ref[...] = jnp.concatenate([dv_ref[h] for h in range(per)], axis=1).astype(BF16)

    def part(first, count):
        return pl.BlockSpec((per, tm, HEAD_DIM), lambda j, i: (jnp.clip(j - first, 0, count - 1), i, 0))

    blk = pl.BlockSpec((tm, LANES), lambda j, i: (i, j))
    vec = pl.BlockSpec((1, LANES), lambda j, i: (0, j))
    tab = pl.BlockSpec((tm, LANES), lambda j, i: (i, 0))
    return _pc(body, name=name, grid=(W // LANES, S // tm),
               in_specs=[part(0, nq), part(nq, nk), part(n_rot, nv), blk, vec, vec, tab, tab,
                         pl.BlockSpec((2 * LANES, LANES), lambda j, i: (0, 0))],
               out_specs=(blk, vec),
               out_shape=(jax.ShapeDtypeStruct((S, W), BF16), jax.ShapeDtypeStruct((1, W), F32)),
               compiler_params=_sem("parallel", "arbitrary"))(dq, dk, dv, qkv, gain, scale, cos, sin, bd)


def _softmax_rows(s):
    m = jnp.max(s, axis=-1, keepdims=True)
    p = jnp.exp(s - m)
    return p, jnp.sum(p, axis=-1, keepdims=True)


def _attn_fwd(name, qkv, H):
    _, S, dh = qkv.shape
    G = H // N_KV_HEADS
    tq = _tile(S, 256, 16)
    kc = _tile(S, KEY_CHUNK, LANES)
    R = G * tq

    def body(q_ref, k_ref, v_ref, o_ref, lse_ref):
        q = q_ref[...].reshape(R, dh)
        m = jnp.full((R, 1), -1e30, F32)
        l = jnp.zeros((R, 1), F32)
        acc = jnp.zeros((R, dh), F32)
        for c in range(S // kc):
            rows = slice(c * kc, (c + 1) * kc)
            s = lax.dot_general(q, k_ref[rows, :], _NT, preferred_element_type=F32)
            m_new = jnp.maximum(m, jnp.max(s, axis=-1, keepdims=True))
            alpha = jnp.exp(m - m_new)
            p = jnp.exp(s - m_new)
            l = alpha * l + jnp.sum(p, axis=-1, keepdims=True)
            acc = alpha * acc + jnp.dot(p.astype(BF16), v_ref[rows, :], preferred_element_type=F32)
            m = m_new
        o_ref[...] = (acc / l).astype(BF16).reshape(G, tq, dh)
        lse_ref[...] = (m + jnp.log(l)).reshape(G, tq, 1)

    qs = pl.BlockSpec((G, tq, dh), lambda kv, i: (kv, i, 0))
    ls = pl.BlockSpec((G, tq, 1), lambda kv, i: (kv, i, 0))
    ks = pl.BlockSpec((None, S, dh), lambda kv, i: (H + kv, 0, 0))
    vs = pl.BlockSpec((None, S, dh), lambda kv, i: (H + N_KV_HEADS + kv, 0, 0))
    return _pc(body, name=name, grid=(N_KV_HEADS, S // tq), in_specs=[qs, ks, vs], out_specs=(qs, ls),
               out_shape=(jax.ShapeDtypeStruct((H, S, dh), BF16), jax.ShapeDtypeStruct((H, S, 1), F32)),
               compiler_params=_sem("parallel", "parallel"))(qkv, qkv, qkv)


def _attn_bwd(name, qkv, o, lse, do):
    H, S, dh = o.shape
    G = H // N_KV_HEADS
    tq = _tile(S, 128, 16)
    kc = _tile(S, KEY_CHUNK, LANES)
    R = G * tq

    def body(q_ref, k_ref, v_ref, o_ref, lse_ref, do_ref, dq_ref, dk_ref, dv_ref):
        @pl.when(pl.program_id(1) == 0)
        def _():
            dk_ref[...] = jnp.zeros_like(dk_ref)
            dv_ref[...] = jnp.zeros_like(dv_ref)

        qq, dd = q_ref[...].reshape(R, dh), do_ref[...].reshape(R, dh)
        delta = jnp.sum(dd.astype(F32) * o_ref[...].reshape(R, dh).astype(F32), axis=-1, keepdims=True)
        lse = lse_ref[...].reshape(R, 1)
        dq = jnp.zeros((R, dh), F32)
        for c in range(S // kc):
            rows = slice(c * kc, (c + 1) * kc)
            kk, vv = k_ref[rows, :], v_ref[rows, :]
            p = jnp.exp(lax.dot_general(qq, kk, _NT, preferred_element_type=F32) - lse)
            dv_ref[rows, :] += lax.dot_general(p.astype(BF16), dd, _TN, preferred_element_type=F32)
            dp = lax.dot_general(dd, vv, _NT, preferred_element_type=F32)
            ds = (p * (dp - delta)).astype(BF16)
            dq = dq + jnp.dot(ds, kk, preferred_element_type=F32)
            dk_ref[rows, :] += lax.dot_general(ds, qq, _TN, preferred_element_type=F32)
        dq_ref[...] = dq.reshape(G, tq, dh)

    qs = pl.BlockSpec((G, tq, dh), lambda kv, i: (kv, i, 0))
    ls = pl.BlockSpec((G, tq, 1), lambda kv, i: (kv, i, 0))
    ks = pl.BlockSpec((None, S, dh), lambda kv, i: (H + kv, 0, 0))
    vs = pl.BlockSpec((None, S, dh), lambda kv, i: (H + N_KV_HEADS + kv, 0, 0))
    acc = pl.BlockSpec((None, S, dh), lambda kv, i: (kv, 0, 0))
    return _pc(body, name=name, grid=(N_KV_HEADS, S // tq), in_specs=[qs, ks, vs, qs, ls, qs],
               out_specs=(qs, acc, acc),
               out_shape=(jax.ShapeDtypeStruct((H, S, dh), F32), jax.ShapeDtypeStruct((N_KV_HEADS, S, dh), F32),
                          jax.ShapeDtypeStruct((N_KV_HEADS, S, dh), F32)),
               compiler_params=_sem("parallel", "arbitrary"))(qkv, qkv, qkv, o, lse, do)


def _xattn_fwd(name, q, kv):
    S, D = q.shape
    _, M, dh = kv.shape
    scale = dh ** -0.5
    tq = _tile(S, 512, 16)

    def body(q_ref, kv_ref, o_ref):
        for h in range(X_HEADS):
            lo, hi = h * dh, (h + 1) * dh
            s = lax.dot_general(q_ref[:, lo:hi], kv_ref[h], _NT, preferred_element_type=F32) * scale
            p, l = _softmax_rows(s)
            o = jnp.dot(p.astype(BF16), kv_ref[X_HEADS + h], preferred_element_type=F32)
            o_ref[:, lo:hi] = (o / l).astype(BF16)

    row = pl.BlockSpec((tq, D), lambda i: (i, 0))
    return _pc(body, name=name, grid=(S // tq,),
               in_specs=[row, pl.BlockSpec((2 * X_HEADS, M, dh), lambda i: (0, 0, 0))],
               out_specs=row, out_shape=jax.ShapeDtypeStruct((S, D), BF16),
               compiler_params=_sem("parallel"))(q, kv)


def _xattn_bwd(name, q, kv, do):
    S, D = q.shape
    _, M, dh = kv.shape
    scale = dh ** -0.5
    tq = _tile(S, 512, 16)

    def body(q_ref, kv_ref, do_ref, dq_ref, dkv_ref):
        @pl.when(pl.program_id(0) == 0)
        def _():
            dkv_ref[...] = jnp.zeros_like(dkv_ref)

        for h in range(X_HEADS):
            lo, hi = h * dh, (h + 1) * dh
            qh, kh, vh, doh = q_ref[:, lo:hi], kv_ref[h], kv_ref[X_HEADS + h], do_ref[:, lo:hi]
            s = lax.dot_general(qh, kh, _NT, preferred_element_type=F32) * scale
            p, l = _softmax_rows(s)
            pn = p / l
            dkv_ref[X_HEADS + h] += lax.dot_general(pn.astype(BF16), doh, _TN, preferred_element_type=F32)
            dp = lax.dot_general(doh, vh, _NT, preferred_element_type=F32)
            ds = (pn * (dp - jnp.sum(pn * dp, axis=-1, keepdims=True)) * scale).astype(BF16)
            dq_ref[:, lo:hi] = jnp.dot(ds, kh, preferred_element_type=F32).astype(BF16)
            dkv_ref[h] += lax.dot_general(ds, qh, _TN, preferred_element_type=F32)

    row = pl.BlockSpec((tq, D), lambda i: (i, 0))
    full = pl.BlockSpec((2 * X_HEADS, M, dh), lambda i: (0, 0, 0))
    return _pc(body, name=name, grid=(S // tq,), in_specs=[row, full, row], out_specs=(row, full),
               out_shape=(jax.ShapeDtypeStruct((S, D), BF16), jax.ShapeDtypeStruct((2 * X_HEADS, M, dh), F32)),
               compiler_params=_sem("arbitrary"))(q, kv, do)


def _sigmoid(x):
    return 1.0 / (1.0 + jnp.exp(-x))


def _halo_specs(tm, n, S):
    nb = tm // 8
    last8 = S // 8 - 1
    main = pl.BlockSpec((2, None, tm, n), lambda j, i: (0, j, i, 0))
    prev = pl.BlockSpec((2, None, 8, n), lambda j, i: (0, j, jnp.maximum(i * nb - 1, 0), 0))
    nxt = pl.BlockSpec((2, None, 8, n), lambda j, i: (0, j, jnp.minimum((i + 1) * nb, last8), 0))
    return main, prev, nxt


def _ffn_up_act(name, h, w, cw, cb, layer):
    S, K = h.shape
    _, J, _, n = w.shape
    tm = _tile(S, 512, FFN_HALO)
    nblk = S // tm
    hb, last = tm // FFN_HALO, S // FFN_HALO - 1
    te = tm + 2 * FFN_HALO

    def body(h_ref, hp_ref, hn_ref, w_ref, cw_ref, b_ref, u_ref, a_ref):
        i = pl.program_id(1)
        zero = jnp.zeros((FFN_HALO, K), BF16)
        he = jnp.concatenate([jnp.where(i == 0, zero, hp_ref[...]), h_ref[...],
                              jnp.where(i == nblk - 1, zero, hn_ref[...])], axis=0)
        mid = slice(FFN_HALO, tm + FFN_HALO)
        c = []
        for half in range(2):
            ue = jnp.dot(he, w_ref[half], preferred_element_type=F32)
            um = ue[mid]
            u_ref[half] = um
            k = cw_ref[half]
            c.append(pltpu.roll(ue, 1, 0)[mid] * k[0:1] + um * k[1:2] + pltpu.roll(ue, te - 1, 0)[mid] * k[2:3]
                     + b_ref[half])
        a_ref[...] = (c[0] * _sigmoid(c[0]) * c[1]).astype(BF16)

    return _pc(body, name=name, grid=(J, nblk),
               in_specs=[pl.BlockSpec((tm, K), lambda j, i: (i, 0)),
                         pl.BlockSpec((FFN_HALO, K), lambda j, i: (jnp.maximum(i * hb - 1, 0), 0)),
                         pl.BlockSpec((FFN_HALO, K), lambda j, i: (jnp.minimum((i + 1) * hb, last), 0)),
                         pl.BlockSpec((2, None, K, n), lambda j, i: (0, j, layer, 0)),
                         pl.BlockSpec((2, None, 3, n), lambda j, i: (0, j, 0, 0)),
                         pl.BlockSpec((2, None, 1, n), lambda j, i: (0, j, 0, 0))],
               out_specs=(pl.BlockSpec((2, None, tm, n), lambda j, i: (0, j, i, 0)),
                          pl.BlockSpec((None, tm, n), lambda j, i: (j, i, 0))),
               out_shape=(jax.ShapeDtypeStruct((2, J, S, n), F32), jax.ShapeDtypeStruct((J, S, n), BF16)),
               compiler_params=_sem("parallel", "parallel"))(h, h, h, w, cw, cb)


def _ffn_act_bwd(name, u, g, w_down, cw, cb):
    _, J, S, n = u.shape
    D = g.shape[1]
    tm = _tile(S, 256, 16)
    nblk = S // tm
    te = tm + 16
    nb = tm // 8
    last8 = S // 8 - 1

    def body(u_ref, up_ref, un_ref, g_ref, gp_ref, gn_ref, wd_ref, w_ref, b_ref, du_ref, st_ref):
        i = pl.program_id(1)

        @pl.when(i == 0)
        def _():
            st_ref[...] = jnp.zeros_like(st_ref)

        def extended(before, main, after):
            return jnp.concatenate([jnp.where(i == 0, 0.0, before), main, jnp.where(i == nblk - 1, 0.0, after)], axis=0)

        mid = slice(8, tm + 8)
        da_e = lax.dot_general(extended(gp_ref[...], g_ref[...], gn_ref[...]).astype(BF16), wd_ref[...], _NT,
                               preferred_element_type=F32)
        ue, c = [], []
        for half in range(2):
            e = extended(up_ref[half], u_ref[half], un_ref[half])
            w = w_ref[half]
            ue.append((pltpu.roll(e, 1, 0), e, pltpu.roll(e, te - 1, 0)))
            c.append(ue[half][0] * w[0:1] + e * w[1:2] + ue[half][2] * w[2:3] + b_ref[half])
        sg = _sigmoid(c[0])
        dc = [da_e * c[1] * (sg * (1.0 + c[0] * (1.0 - sg))), da_e * (c[0] * sg)]
        r8 = lax.broadcasted_iota(jnp.int32, (8, n), 0)
        for half in range(2):
            w, d, (e_before, e, e_after) = w_ref[half], dc[half], ue[half]
            dm = d[mid]
            du = pltpu.roll(d, te - 1, 0)[mid] * w[0:1] + dm * w[1:2] + pltpu.roll(d, 1, 0)[mid] * w[2:3]
            du_ref[half] = du.astype(BF16)
            s0 = jnp.sum(dm * e_before[mid], axis=0, keepdims=True)
            s1 = jnp.sum(dm * e[mid], axis=0, keepdims=True)
            s2 = jnp.sum(dm * e_after[mid], axis=0, keepdims=True)
            s3 = jnp.sum(dm, axis=0, keepdims=True)
            st_ref[half] += jnp.where(r8 == 0, s0, jnp.where(r8 == 1, s1, jnp.where(r8 == 2, s2,
                                      jnp.where(r8 == 3, s3, 0.0))))

    main, prev, nxt = _halo_specs(tm, n, S)
    gmain = pl.BlockSpec((tm, D), lambda j, i: (i, 0))
    gprev = pl.BlockSpec((8, D), lambda j, i: (jnp.maximum(i * nb - 1, 0), 0))
    gnxt = pl.BlockSpec((8, D), lambda j, i: (jnp.minimum((i + 1) * nb, last8), 0))
    return _pc(body, name=name, grid=(J, nblk),
               in_specs=[main, prev, nxt, gmain, gprev, gnxt, pl.BlockSpec((n, D), lambda j, i: (j, 0)),
                         pl.BlockSpec((2, None, 3, n), lambda j, i: (0, j, 0, 0)),
                         pl.BlockSpec((2, None, 1, n), lambda j, i: (0, j, 0, 0))],
               out_specs=(main, pl.BlockSpec((2, None, 8, n), lambda j, i: (0, j, 0, 0))),
               out_shape=(jax.ShapeDtypeStruct((2, J, S, n), BF16), jax.ShapeDtypeStruct((2, J, 8, n), F32)),
               compiler_params=_sem("parallel", "arbitrary"))(u, u, u, g, g, g, w_down, cw, cb)


def _window_count(t, w, S):
    lo = jnp.maximum(t - w // 2, 0)
    hi = jnp.minimum(t + w - w // 2, S)
    return (hi - lo).astype(F32)


def _trailing_sums(x, w):
    k = 1
    while k < w:
        x = x + pltpu.roll(x, k, 0)
        k *= 2
    return x


def _pool_window(name, h, group_w, adjoint, out_dtype):
    S, D = h.shape
    SP = S + 2 * POOL_PAD
    per_group = group_w // LANES

    def body(h_ref, o_ref, xp):
        g = pl.program_id(0) // per_group
        t = lax.broadcasted_iota(jnp.int32, (S, LANES), 0)
        xp[0:POOL_PAD, :] = jnp.zeros((POOL_PAD, LANES), F32)
        xp[S + POOL_PAD:SP, :] = jnp.zeros((POOL_PAD, LANES), F32)
        for gi, w in enumerate(POOL_WINDOWS):
            @pl.when(g == gi)
            def _():
                hv = h_ref[...]
                cnt = _window_count(t, w, S)
                xp[POOL_PAD:S + POOL_PAD, :] = hv / cnt if adjoint else hv
                ahead = w // 2 if adjoint else w // 2 - 1
                sw = _trailing_sums(xp[...], w)
                if ahead:
                    sw = pltpu.roll(sw, SP - ahead, 0)
                win = sw[POOL_PAD:S + POOL_PAD]
                o_ref[...] = ((win if adjoint else win / cnt) - hv).astype(out_dtype)

    col = pl.BlockSpec((S, LANES), lambda j: (0, j))
    return _pc(body, name=name, grid=(D // LANES,), in_specs=[col], out_specs=col,
               out_shape=jax.ShapeDtypeStruct((S, D), out_dtype),
               scratch_shapes=[pltpu.VMEM((SP, LANES), F32)], compiler_params=_sem("parallel"))(h)


def _pool_proj(name, mixed, w, scale, res):
    S, D = mixed.shape
    G, gw, _ = w.shape
    tm = _tile(S, 512, 16)

    def body(m_ref, w_ref, s_ref, r_ref, o_ref):
        for g in range(G):
            lo, hi = g * gw, (g + 1) * gw
            y = jnp.dot(m_ref[:, lo:hi], w_ref[g], preferred_element_type=F32)
            o_ref[:, lo:hi] = r_ref[:, lo:hi] + y * s_ref[:, lo:hi]

    row = pl.BlockSpec((tm, D), lambda i: (i, 0))
    return _pc(body, name=name, grid=(S // tm,),
               in_specs=[row, pl.BlockSpec((G, gw, gw), lambda i: (0, 0, 0)), pl.BlockSpec((1, D), lambda i: (0, 0)), row],
               out_specs=row, out_shape=jax.ShapeDtypeStruct((S, D), F32),
               compiler_params=_sem("parallel"))(mixed, w, scale, res)


def _pool_proj_bwd(name, dy, mixed, w, scale):
    S, D = mixed.shape
    G, gw, _ = w.shape
    tm = _tile(S, 512, 16)

    def body(dy_ref, m_ref, w_ref, s_ref, dm_ref, dw_ref, ds_ref):
        @pl.when(pl.program_id(0) == 0)
        def _():
            dw_ref[...] = jnp.zeros_like(dw_ref)
            ds_ref[...] = jnp.zeros_like(ds_ref)

        for g in range(G):
            lo, hi = g * gw, (g + 1) * gw
            mg, dyg = m_ref[:, lo:hi], dy_ref[:, lo:hi]
            y = jnp.dot(mg, w_ref[g], preferred_element_type=F32)
            ds_ref[:, lo:hi] += jnp.sum(dyg * y, axis=0, keepdims=True)
            dyp = (dyg * s_ref[:, lo:hi]).astype(BF16)
            dm_ref[:, lo:hi] = lax.dot_general(dyp, w_ref[g], _NT, preferred_element_type=F32)
            dw_ref[g] += lax.dot_general(mg, dyp, _TN, preferred_element_type=F32)

    row = pl.BlockSpec((tm, D), lambda i: (i, 0))
    wsp = pl.BlockSpec((G, gw, gw), lambda i: (0, 0, 0))
    vec = pl.BlockSpec((1, D), lambda i: (0, 0))
    return _pc(body, name=name, grid=(S // tm,), in_specs=[row, row, wsp, vec], out_specs=(row, wsp, vec),
               out_shape=(jax.ShapeDtypeStruct((S, D), F32), jax.ShapeDtypeStruct((G, gw, gw), F32),
                          jax.ShapeDtypeStruct((1, D), F32)),
               compiler_params=_sem("arbitrary"))(dy, mixed, w, scale)


def _adamw(name, w, g, m, v):
    shape = w.shape
    C = shape[-1]
    R = w.size // C
    tm = _tile(R, 512, 8)

    def body(w_ref, g_ref, m_ref, v_ref, d_ref, nm_ref, nv_ref):
        gv = g_ref[...]
        nm = ADAM_B1 * m_ref[...] + (1.0 - ADAM_B1) * gv
        nv = ADAM_B2 * v_ref[...] + (1.0 - ADAM_B2) * (gv * gv)
        m_hat = nm / (1.0 - ADAM_B1 ** ADAM_STEP)
        v_hat = nv / (1.0 - ADAM_B2 ** ADAM_STEP)
        d_ref[...] = -ADAM_LR * (m_hat / (jnp.sqrt(v_hat) + ADAM_EPS) + ADAM_WD * w_ref[...])
        nm_ref[...] = nm
        nv_ref[...] = nv

    blk = pl.BlockSpec((tm, C), lambda i: (i, 0))
    sd = jax.ShapeDtypeStruct((R, C), F32)
    outs = _pc(body, name=name, grid=(R // tm,), in_specs=[blk] * 4, out_specs=(blk,) * 3, out_shape=(sd,) * 3,
               compiler_params=_sem("parallel"))(*(a.reshape(R, C) for a in (w, g, m, v)))
    return tuple(o.reshape(shape) for o in outs)


def _position():
    return lax.axis_index("x"), lax.axis_index("y"), lax.axis_index("c")


def _flip(v, bit):
    return 1 - v if bit else v


def _allgather_small(name, v):
    R, W = v.shape

    def body(v_ref, out_ref, send_sems, recv_sems):
        x, y, c = _position()
        me = 4 * x + 2 * y + c
        out_ref[me] = v_ref[...]
        sends = []
        for k in range(1, N_DEV):
            peer = (_flip(x, k & 4), _flip(y, k & 2), _flip(c, k & 1))
            cp = pltpu.make_async_remote_copy(src_ref=v_ref, dst_ref=out_ref.at[me], send_sem=send_sems.at[k - 1],
                                              recv_sem=recv_sems.at[k - 1], device_id=peer, device_id_type=MESH)
            cp.start()
            sends.append(cp)
        for k in range(1, N_DEV):
            peer = (_flip(x, k & 4), _flip(y, k & 2), _flip(c, k & 1))
            slot = 4 * peer[0] + 2 * peer[1] + peer[2]
            pltpu.make_async_remote_copy(src_ref=v_ref, dst_ref=out_ref.at[slot], send_sem=send_sems.at[k - 1],
                                         recv_sem=recv_sems.at[k - 1], device_id=peer, device_id_type=MESH).wait_recv()
        for cp in sends:
            cp.wait_send()

    vm = pl.BlockSpec(memory_space=pltpu.VMEM)
    return _pc(body, name=name, in_specs=[vm], out_specs=vm, out_shape=jax.ShapeDtypeStruct((N_DEV, R, W), F32),
               scratch_shapes=[pltpu.SemaphoreType.DMA((N_DEV - 1,)), pltpu.SemaphoreType.DMA((N_DEV - 1,))])(v)


def _sum_slots(name, a):
    n, R, W = a.shape

    def body(a_ref, o_ref):
        acc = a_ref[0]
        for s in range(1, n):
            acc = acc + a_ref[s]
        o_ref[...] = acc

    return _pc(body, name=name, grid=(1,), in_specs=[pl.BlockSpec((n, R, W), lambda i: (0, 0, 0))],
               out_specs=pl.BlockSpec((R, W), lambda i: (0, 0)), out_shape=jax.ShapeDtypeStruct((R, W), F32))(a)


def _allgather_blocks(name, blocks):
    n = len(blocks)

    def body(*refs):
        b_refs, out_refs, token = refs[:n], refs[n:2 * n], refs[2 * n]
        send_sems, recv_sems, local_sems = refs[2 * n + 1:]
        token[...] = jnp.zeros_like(token)
        x, y, c = _position()
        me, sibling = (x, y, c), (x, y, 1 - c)
        chips = [(1 - x, y), (x, 1 - y), (1 - x, 1 - y)]

        def slot(i, px, py, pc):
            return out_refs[i].at[4 * px + 2 * py + pc]

        def copy(i, k, block, to, src=None):
            return pltpu.make_async_remote_copy(src_ref=slot(i, *block) if src is None else src, dst_ref=slot(i, *block),
                                                send_sem=send_sems.at[k, i], recv_sem=recv_sems.at[k, i],
                                                device_id=to, device_id_type=MESH)

        mine = [pltpu.make_async_copy(b_refs[i], slot(i, *me), local_sems.at[i]) for i in range(n)]
        first = [copy(i, 1 + j, me, (*chip, c), src=b_refs[i]) for i in range(n) for j, chip in enumerate(chips)]
        first += [copy(i, 0, me, sibling, src=b_refs[i]) for i in range(n)]
        for cp in mine + first:
            cp.start()
        passed = []
        for j, chip in enumerate(chips):
            for i in range(n):
                copy(i, 1 + j, (*chip, c), me).wait_recv()
                passed.append(copy(i, 4 + j, (*chip, c), sibling))
                passed[-1].start()
        for i in range(n):
            copy(i, 0, sibling, me).wait_recv()
        for j, chip in enumerate(chips):
            for i in range(n):
                copy(i, 4 + j, (*chip, 1 - c), me).wait_recv()
        for cp in first + passed:
            cp.wait_send()
        for cp in mine:
            cp.wait()

    hbm = pl.BlockSpec(memory_space=pl.ANY)
    return _pc(body, name=name, in_specs=[hbm] * n, out_specs=[hbm] * n + [pl.BlockSpec(memory_space=pltpu.VMEM)],
               out_shape=[jax.ShapeDtypeStruct((N_DEV,) + b.shape, b.dtype) for b in blocks]
               + [jax.ShapeDtypeStruct((8, LANES), F32)],
               scratch_shapes=[pltpu.SemaphoreType.DMA((7, n)), pltpu.SemaphoreType.DMA((7, n)),
                               pltpu.SemaphoreType.DMA((n,))])(*blocks)


def _add_sibling(name, g4, r1, pos):
    n, _, L, W = g4.shape
    tl = _tile(L, 512, 16)

    def body(pos_ref, g_ref, r_ref, tb_ref, own_ref):
        t = g_ref[...] + r_ref[...]
        tb_ref[...] = t.astype(BF16)

        @pl.when(pl.program_id(1) == pos_ref[1])
        def _():
            own_ref[...] = t

    gs = pltpu.PrefetchScalarGridSpec(
        num_scalar_prefetch=1, grid=(L // tl, n),
        in_specs=[pl.BlockSpec((None, None, tl, W), lambda i, k, p: (k, p[0], i, 0)),
                  pl.BlockSpec((None, tl, W), lambda i, k, p: (k, i, 0))],
        out_specs=(pl.BlockSpec((None, tl, W), lambda i, k, p: (k, i, 0)),
                   pl.BlockSpec((tl, W), lambda i, k, p: (i, 0))))
    return _pc(body, name=name, grid_spec=gs,
               out_shape=(jax.ShapeDtypeStruct((n, L, W), BF16), jax.ShapeDtypeStruct((L, W), F32)),
               compiler_params=_sem("parallel", "arbitrary"))(pos, g4, r1)


def _add_chips(name, own, r2):
    L, W = own.shape
    tl = _tile(L, 512, 16)

    def body(o_ref, r_ref, out_ref):
        acc = o_ref[...]
        for j in range(3):
            acc = acc + r_ref[j].astype(F32)
        out_ref[...] = acc

    return _pc(body, name=name, grid=(L // tl,),
               in_specs=[pl.BlockSpec((tl, W), lambda i: (i, 0)), pl.BlockSpec((3, tl, W), lambda i: (0, i, 0))],
               out_specs=pl.BlockSpec((tl, W), lambda i: (i, 0)), out_shape=jax.ShapeDtypeStruct((L, W), F32),
               compiler_params=_sem("parallel"))(own, r2)


_HBM = pl.BlockSpec(memory_space=pltpu.HBM)
_SEM = pl.BlockSpec(memory_space=pltpu.SEMAPHORE)
_EFFECT = pltpu.SideEffectType.DATAFLOW_SIDE_EFFECTING


def _in_hbm(a):
    return pltpu.with_memory_space_constraint(a, pltpu.HBM)


def _after(x, token):
    return x + token[0, 0].astype(x.dtype)


def _copies_start(name, bufs, sem_shape, plan):
    nb = len(bufs)

    def body(*refs):
        for cp in plan(refs[:nb], refs[nb], refs[nb + 1]):
            cp.start()
        refs[-1][...] = jnp.zeros_like(refs[-1])

    out = _pc(body, name=name, in_specs=[_HBM] * nb,
              out_specs=(_SEM, _SEM, *[_HBM] * nb, pl.BlockSpec(memory_space=pltpu.VMEM)),
              out_shape=(pltpu.SemaphoreType.DMA(sem_shape), pltpu.SemaphoreType.DMA(sem_shape),
                         *[pltpu.HBM(b.shape, b.dtype) for b in bufs], jax.ShapeDtypeStruct((8, LANES), F32)),
              input_output_aliases={i: 2 + i for i in range(nb)},
              compiler_params=pltpu.CompilerParams(has_side_effects=_EFFECT))(*[_in_hbm(b) for b in bufs])
    return out[0], out[1], list(out[2:2 + nb]), out[-1]


def _copies_wait(name, bufs, send_sems, recv_sems, plan, after):
    nb = len(bufs)

    def body(*refs):
        for cp in plan(refs[:nb], refs[nb], refs[nb + 1]):
            cp.wait_send()
            cp.wait_recv()

    return list(_pc(body, name=name, in_specs=[_HBM] * nb + [_SEM, _SEM, pl.BlockSpec(memory_space=pl.ANY)],
                    out_specs=[_HBM] * nb, out_shape=[pltpu.HBM(b.shape, b.dtype) for b in bufs],
                    input_output_aliases={i: i for i in range(nb)},
                    compiler_params=pltpu.CompilerParams(has_side_effects=_EFFECT))(*bufs, send_sems, recv_sems, after))


def _plan_gather_chips(n):
    def plan(refs, send_sems, recv_sems):
        x, y, c = _position()
        peers = [(x, y, 1 - c), (1 - x, y, c), (x, 1 - y, c), (1 - x, 1 - y, c)]
        return [pltpu.make_async_remote_copy(src_ref=refs[i], dst_ref=refs[n + i].at[4 * x + 2 * y + c],
                                             send_sem=send_sems.at[k * n + i], recv_sem=recv_sems.at[k * n + i],
                                             device_id=peer, device_id_type=MESH)
                for i in range(n) for k, peer in enumerate(peers)]
    return plan


def _plan_gather_sibling(n):
    def plan(refs, send_sems, recv_sems):
        x, y, c = _position()
        slots = [4 * (1 - x) + 2 * y + c, 4 * x + 2 * (1 - y) + c, 4 * (1 - x) + 2 * (1 - y) + c]
        return [pltpu.make_async_remote_copy(src_ref=refs[i].at[s], dst_ref=refs[i].at[s],
                                             send_sem=send_sems.at[k * n + i], recv_sem=recv_sems.at[k * n + i],
                                             device_id=(x, y, 1 - c), device_id_type=MESH)
                for i in range(n) for k, s in enumerate(slots)]
    return plan


def _plan_reduce_sibling(n):
    def plan(refs, send_sems, recv_sems):
        x, y, c = _position()
        return [pltpu.make_async_remote_copy(src_ref=refs[i].at[k, 1 - c], dst_ref=refs[n + i].at[k],
                                             send_sem=send_sems.at[k * n + i], recv_sem=recv_sems.at[k * n + i],
                                             device_id=(x, y, 1 - c), device_id_type=MESH)
                for i in range(n) for k in range(N_DEV // 2)]
    return plan


def _plan_reduce_chips(n):
    def plan(refs, send_sems, recv_sems):
        x, y, c = _position()
        cps = []
        for i in range(n):
            for j in range(1, 4):
                px, py = _flip(x, j & 2), _flip(y, j & 1)
                sem = (j - 1) * n + i
                cps.append(pltpu.make_async_remote_copy(src_ref=refs[i].at[2 * px + py], dst_ref=refs[n + i].at[j - 1],
                                                        send_sem=send_sems.at[sem], recv_sem=recv_sems.at[sem],
                                                        device_id=(px, py, c), device_id_type=MESH))
        return cps
    return plan


def _heads_major(name, a, after):
    S, W = a.shape
    H = W // HEAD_DIM
    tm = _tile(S, 512, 16)

    def body(a_ref, after_ref, o_ref):
        v = a_ref[...]
        for h in range(H):
            o_ref[h] = v[:, h * HEAD_DIM:(h + 1) * HEAD_DIM]

    return _pc(body, name=name, grid=(S // tm,),
               in_specs=[pl.BlockSpec((tm, W), lambda i: (i, 0)), pl.BlockSpec(memory_space=pl.ANY)],
               out_specs=pl.BlockSpec((H, tm, HEAD_DIM), lambda i: (0, i, 0)),
               out_shape=jax.ShapeDtypeStruct((H, S, HEAD_DIM), a.dtype), compiler_params=_sem("parallel"))(a, after)


def _heads_minor(name, a):
    H, S, _ = a.shape
    tm = _tile(S, 512, 16)

    def body(a_ref, o_ref):
        o_ref[...] = jnp.concatenate([a_ref[h] for h in range(H)], axis=1)

    return _pc(body, name=name, grid=(S // tm,), in_specs=[pl.BlockSpec((H, tm, HEAD_DIM), lambda i: (0, i, 0))],
               out_specs=pl.BlockSpec((tm, H * HEAD_DIM), lambda i: (i, 0)),
               out_shape=jax.ShapeDtypeStruct((S, H * HEAD_DIM), a.dtype), compiler_params=_sem("parallel"))(a)


def kernel(x, mem, attn_norm, attn_w_qkv, attn_q_gain, attn_k_gain, attn_w_o, pool_norm, pool_w, pool_scale, xattn_norm, mem_norm, xattn_w_q, xattn_w_kv, xattn_w_o, ffn_norm, ffn_w_up, ffn_conv_w, ffn_conv_b, ffn_w_down, final_norm, loss_target, m_attn_norm, m_attn_w_qkv, m_attn_q_gain, m_attn_k_gain, m_attn_w_o, m_pool_norm, m_pool_w, m_pool_scale, m_xattn_norm, m_mem_norm, m_xattn_w_q, m_xattn_w_kv, m_xattn_w_o, m_ffn_norm, m_ffn_w_up, m_ffn_conv_w, m_ffn_conv_b, m_ffn_w_down, m_final_norm, v_attn_norm, v_attn_w_qkv, v_attn_q_gain, v_attn_k_gain, v_attn_w_o, v_pool_norm, v_pool_w, v_pool_scale, v_xattn_norm, v_mem_norm, v_xattn_w_q, v_xattn_w_kv, v_xattn_w_o, v_ffn_norm, v_ffn_w_up, v_ffn_conv_w, v_ffn_conv_b, v_ffn_w_down, v_final_norm):
    names = ['attn_norm', 'attn_w_qkv', 'attn_q_gain', 'attn_k_gain', 'attn_w_o', 'pool_norm', 'pool_w', 'pool_scale',
             'xattn_norm', 'mem_norm', 'xattn_w_q', 'xattn_w_kv', 'xattn_w_o', 'ffn_norm', 'ffn_w_up', 'ffn_conv_w',
             'ffn_conv_b', 'ffn_w_down', 'final_norm']
    W = dict(zip(names, (attn_norm, attn_w_qkv, attn_q_gain, attn_k_gain, attn_w_o, pool_norm, pool_w, pool_scale,
                         xattn_norm, mem_norm, xattn_w_q, xattn_w_kv, xattn_w_o, ffn_norm, ffn_w_up, ffn_conv_w,
                         ffn_conv_b, ffn_w_down, final_norm)))
    Mo = dict(zip(names, (m_attn_norm, m_attn_w_qkv, m_attn_q_gain, m_attn_k_gain, m_attn_w_o, m_pool_norm, m_pool_w,
                          m_pool_scale, m_xattn_norm, m_mem_norm, m_xattn_w_q, m_xattn_w_kv, m_xattn_w_o, m_ffn_norm,
                          m_ffn_w_up, m_ffn_conv_w, m_ffn_conv_b, m_ffn_w_down, m_final_norm)))
    Vo = dict(zip(names, (v_attn_norm, v_attn_w_qkv, v_attn_q_gain, v_attn_k_gain, v_attn_w_o, v_pool_norm, v_pool_w,
                          v_pool_scale, v_xattn_norm, v_mem_norm, v_xattn_w_q, v_xattn_w_kv, v_xattn_w_o, v_ffn_norm,
                          v_ffn_w_up, v_ffn_conv_w, v_ffn_conv_b, v_ffn_w_down, v_final_norm)))

    S, D = x.shape[1], x.shape[2]
    n_layers = xattn_norm.shape[0]
    n_up = ffn_w_up.shape[2]
    qkv_w = attn_w_qkv.shape[2] * N_DEV
    n_heads = qkv_w // HEAD_DIM - 2 * N_KV_HEADS
    n_rot = (n_heads + N_KV_HEADS) * HEAD_DIM // LANES
    group_w = pool_w.shape[3]
    xs, mems, tgt = x[0], mem[0], loss_target[0]
    xi, yi, ci = _position()
    dev = 4 * xi + 2 * yi + ci
    pos = jnp.stack([ci, 2 * xi + yi]).astype(jnp.int32)

    layers = range(n_layers)
    n_groups = pool_w.shape[1]
    small_vec = jnp.concatenate([pool_norm.reshape(-1), pool_scale.reshape(-1), ffn_conv_w.reshape(-1)])
    small_rows = _round_up(-(-small_vec.size // PACK_W), 8)
    small_vec = jnp.pad(small_vec, (0, small_rows * PACK_W - small_vec.size)).reshape(small_rows, PACK_W)
    w_qkv, w_o, small, attn_token = _allgather_blocks(
        "allgather_attn", [attn_w_qkv[0].astype(BF16), attn_w_o[0].astype(BF16), small_vec])
    small = small.reshape(N_DEV, -1)
    w_qkv = w_qkv.transpose(1, 0, 2).reshape(D, qkv_w)
    w_o = w_o.reshape(-1, D)
    blocks = [pool_w.reshape(-1, group_w)] + [xattn_w_q[l] for l in layers] + [xattn_w_kv.reshape(n_layers * D, -1)]
    blocks += [xattn_w_o[l] for l in layers] + [ffn_w_up.reshape(n_layers * D, n_up)] + [ffn_w_down[l] for l in layers]
    blocks = [b.astype(BF16) for b in blocks]
    blocks[0] = _after(blocks[0], attn_token)
    n_blk = len(blocks)
    lands = [lax.dynamic_update_index_in_dim(lax.empty((N_DEV,) + b.shape, BF16), b, dev, 0) for b in blocks]
    plan_chips, plan_sibling = _plan_gather_chips(n_blk), _plan_gather_sibling(n_blk)
    gather_sems = _copies_start("gather_chips_start", blocks + lands, (4 * n_blk,), plan_chips)
    attn_norm_late = _after(attn_norm, gather_sems[3])

    d_sh = pool_norm.shape[1]
    pool_norm_f = small[:, :d_sh].reshape(1, D)
    pool_scale_f = small[:, d_sh:2 * d_sh].reshape(1, D)
    conv_w_f = small[:, 2 * d_sh:2 * d_sh + ffn_conv_w.size].reshape(N_DEV, n_layers, 3, n_up)
    conv_b_f = ffn_conv_b.reshape(n_layers, N_DEV, 1, n_up)

    cos, sin = _rope_tables(S)
    bd = _head_mean_matrix()
    pad_w = qkv_w - (n_heads + N_KV_HEADS) * HEAD_DIM
    qk_gain = jnp.concatenate([jnp.tile(attn_q_gain[0], n_heads), jnp.tile(attn_k_gain[0], N_KV_HEADS),
                               jnp.ones((pad_w,), F32)]).reshape(1, qkv_w)
    qk_scale = jnp.concatenate([jnp.full((n_heads * HEAD_DIM,), HEAD_DIM ** -0.5, F32),
                                jnp.ones((qkv_w - n_heads * HEAD_DIM,), F32)]).reshape(1, qkv_w)

    saved = []

    def xattn_ffn_fwd(l, xin, hx=None):
        if hx is None:
            hx = _rmsnorm(f"xattn_norm{l}", xin, xattn_norm[l:l + 1], BF16)
        memn = _rmsnorm(f"mem_norm{l}", mems, mem_norm[l:l + 1], BF16)
        qx = _mm_nn(f"xattn_q{l}", hx, w_xq[l], BF16)
        kv = _mm_nn_bs(f"xattn_kv{l}", memn, w_xkv, BF16, l)
        ox = _xattn_fwd(f"xattn_fwd{l}", qx, kv)
        x2 = _mm_nn(f"xattn_o{l}", ox, w_xo[l], F32, res=xin)
        hf = _rmsnorm(f"ffn_norm{l}", x2, ffn_norm[l:l + 1], BF16)
        cw = conv_w_f[:, l].reshape(2, N_DEV // 2, 3, n_up)
        cb = conv_b_f[l].reshape(2, N_DEV // 2, 1, n_up)
        u, act = _ffn_up_act(f"ffn_up_act{l}", hf, w_up.reshape(2, N_DEV // 2, n_layers * D, n_up), cw, cb, l)
        x3 = _mm_nn_as(f"ffn_down{l}", act, w_down[l], F32, x2)
        saved.append(dict(xin=xin, hx=hx, memn=memn, qx=qx, kv=kv, ox=ox, x2=x2, hf=hf, u=u, cw=cw, cb=cb, act=act))
        return x3

    h0 = _rmsnorm("attn_norm", xs, attn_norm_late, BF16)
    qkv = _mm_nn("attn_qkv", h0, w_qkv, F32)
    qkr = _qk_rope("qk_rope", qkv, qk_gain, qk_scale, cos, sin, bd, n_rot)
    o_hm, lse = _attn_fwd("attn_fwd", qkr, n_heads)
    o_att = _heads_minor("attn_heads_minor", o_hm)
    arrived = _copies_wait("gather_chips_wait", gather_sems[2], gather_sems[0], gather_sems[1], plan_chips, o_att)
    pass_sems = _copies_start("gather_sibling_start", arrived[n_blk:], (3 * n_blk,), plan_sibling)
    x1 = _mm_nn("attn_o", o_att, _after(w_o, pass_sems[3]), F32, res=xs)
    hx0 = _rmsnorm("xattn_norm0", x1, xattn_norm[0:1], BF16)
    gathered = iter(_copies_wait("gather_sibling_wait", pass_sems[2], pass_sems[0], pass_sems[1], plan_sibling, hx0))
    w_pool = (next(gathered).reshape(N_DEV, n_groups, -1, group_w).transpose(1, 0, 2, 3)
              .reshape(n_groups, group_w, group_w))
    w_xq = [next(gathered).reshape(D, D) for l in layers]
    w_xkv = next(gathered)
    w_xo = [next(gathered).reshape(D, D) for l in layers]
    w_up = next(gathered)
    w_down = [next(gathered).reshape(-1, D) for l in layers]
    x3 = xattn_ffn_fwd(0, x1, hx0)
    hp = _rmsnorm("pool_norm", x3, pool_norm_f, F32)
    mixed = _pool_window("pool_window", hp, group_w, False, BF16)
    x4 = _pool_proj("pool_proj", mixed, w_pool, pool_scale_f, x3)
    x6 = xattn_ffn_fwd(1, x4)

    G = {}
    g, d_final, lvec = _loss_head("loss_head", x6, final_norm.reshape(1, D), tgt)
    G['final_norm'] = d_final.reshape(D)
    loss_part = (0.5 * jnp.sum(lvec) / D).reshape(1)

    d_xn, d_mn, d_fn, d_xq, d_xkv, d_xo, d_up, d_cw, d_cb, d_down = ([None] * n_layers for _ in range(10))

    def xattn_ffn_bwd(l, g, conv_b_late=None, after_act=None):
        sv = saved[l]
        d_down[l] = _mm_tn_as(f"ffn_down_dw{l}", sv['act'], g, F32)
        du, st = _ffn_act_bwd(f"ffn_act_bwd{l}", sv['u'], g, w_down[l], sv['cw'],
                              sv['cb'] if conv_b_late is None else conv_b_late)
        du = du.reshape(N_DEV, S, n_up)
        ffn_gain = ffn_norm[l:l + 1] if after_act is None else _after(ffn_norm[l:l + 1], after_act(du))
        st = st.reshape(N_DEV, 8, n_up)
        d_cw[l], d_cb[l] = st[:, 0:3], st[:, 3].reshape(-1)
        d_up[l] = _mm_tn_bs(f"ffn_up_dw{l}", sv['hf'], du, F32)
        g, d_fn[l] = _mm_nt_abs_norm_bwd(f"ffn_up_dx_norm_bwd{l}", du, w_up, sv['x2'], ffn_gain, g, l)
        d_xo[l] = _mm_tn(f"xattn_o_dw{l}", sv['ox'], g, F32)
        do = _mm_nt(f"xattn_o_dx{l}", g, w_xo[l], BF16)
        dq, dkv = _xattn_bwd(f"xattn_bwd{l}", sv['qx'], sv['kv'], do)
        d_xq[l] = _mm_tn(f"xattn_q_dw{l}", sv['hx'], dq, F32)
        d_xkv[l] = _mm_tn_bs(f"xattn_kv_dw{l}", sv['memn'], dkv, F32)
        dmemn = _mm_nt_abs(f"xattn_kv_dx{l}", dkv, w_xkv, D, F32, l)
        _, d_mn[l] = _rmsnorm_bwd(f"mem_norm_bwd{l}", mems, mem_norm[l:l + 1], dmemn)
        g, d_xn[l] = _mm_nt_norm_bwd(f"xattn_q_dx_norm_bwd{l}", dq, w_xq[l], sv['xin'], xattn_norm[l:l + 1], g)
        return g

    def reduce_start(tag, bufs):
        n = len(bufs)
        g4s = [b.reshape((N_DEV // 2, 2) + b.shape[1:]) for b in bufs]
        lands = [lax.empty((N_DEV // 2,) + b.shape[1:], F32) for b in bufs]
        plan = _plan_reduce_sibling(n)
        return (n, plan) + _copies_start(f"reduce_sibling_start_{tag}", g4s + lands, (N_DEV // 2 * n,), plan)

    def reduce_between(tag, state, after):
        n, plan, send_sems, recv_sems, thru, _ = state
        got = _copies_wait(f"reduce_sibling_wait_{tag}", thru, send_sems, recv_sems, plan, after)
        sums = [_add_sibling(f"reduce_add_sibling_{tag}{i}", got[i], got[n + i], pos) for i in range(n)]
        lands = [lax.empty((3,) + tb.shape[1:], BF16) for tb, _ in sums]
        plan = _plan_reduce_chips(n)
        return (n, plan, [own for _, own in sums]) + _copies_start(f"reduce_chips_start_{tag}",
                                                                    [tb for tb, _ in sums] + lands, (3 * n,), plan)

    def reduce_finish(tag, state, after):
        n, plan, owns, send_sems, recv_sems, thru, _ = state
        got = _copies_wait(f"reduce_chips_wait_{tag}", thru, send_sems, recv_sems, plan, after)
        return [_add_chips(f"reduce_add_chips_{tag}{i}", owns[i], got[n + i]) for i in range(n)]

    def layer_bufs(l):
        return [d_xq[l].reshape(N_DEV, -1, D), d_xkv[l], d_xo[l].reshape(N_DEV, -1, D), d_up[l],
                d_down[l].reshape(N_DEV, -1, D)]

    g = xattn_ffn_bwd(1, g)
    d_mixed, d_pool_w, d_pool_scale = _pool_proj_bwd("pool_proj_bwd", g, mixed, w_pool, pool_scale_f)
    dhp = _pool_window("pool_window_bwd", d_mixed, group_w, True, F32)
    g, d_pool_norm = _rmsnorm_bwd("pool_norm_bwd", x3, pool_norm_f, dhp, g)
    upper = reduce_start("upper", [d_pool_w.reshape(n_groups, N_DEV, -1, group_w).transpose(1, 0, 2, 3)
                                   .reshape(N_DEV, -1, group_w)] + layer_bufs(1))
    between = []

    def upper_between(du):
        between.append(reduce_between("upper", upper, du))
        return between[0][-1]

    g = xattn_ffn_bwd(0, g, _after(saved[0]['cb'], upper[-1]), upper_between)
    lower = reduce_start("lower", layer_bufs(0))
    d_wo = _mm_tn("attn_o_dw", o_att, g, F32)
    do = _mm_nt("attn_o_dx", g, _after(w_o, lower[-1]), BF16)
    lower = reduce_between("lower", lower, do)
    do_hm = _heads_major("attn_heads_major", do, lower[-1])
    dq_hm, dk_hm, dv_hm = _attn_bwd("attn_bwd", qkr, o_hm, lse, do_hm)
    red_lower = reduce_finish("lower", lower, dq_hm)
    d_qkv, d_gain = _qk_rope_bwd("qk_rope_bwd", dq_hm, dk_hm, dv_hm, qkv, qk_gain, qk_scale, cos, sin, bd)
    red_upper = reduce_finish("upper", between[0], d_qkv)
    d_wqkv = _mm_tn("attn_qkv_dw", h0, d_qkv, F32)
    last = reduce_start("last", [d_wqkv.reshape(D, N_DEV, -1).transpose(1, 0, 2), d_wo.reshape(N_DEV, -1, D)])
    last = reduce_between("last", last, d_wqkv)
    grad_x, d_attn_norm = _mm_nt_norm_bwd("attn_qkv_dx_norm_bwd", d_qkv, w_qkv, xs, attn_norm, g, last[-1])
    G['pool_w'] = red_upper[0].reshape(pool_w.shape)
    per_layer = [red_lower, red_upper[1:]]
    for i, n in enumerate(['xattn_w_q', 'xattn_w_kv', 'xattn_w_o', 'ffn_w_up', 'ffn_w_down']):
        G[n] = jnp.stack([per_layer[l][i] for l in layers])

    hq = n_heads * HEAD_DIM
    small_g = {'attn_norm': d_attn_norm, 'attn_q_gain': d_gain[0, :hq].reshape(n_heads, HEAD_DIM).sum(0),
               'attn_k_gain': d_gain[0, hq:hq + N_KV_HEADS * HEAD_DIM].reshape(N_KV_HEADS, HEAD_DIM).sum(0),
               'pool_norm': d_pool_norm, 'pool_scale': d_pool_scale,
               'xattn_norm': jnp.concatenate(d_xn), 'mem_norm': jnp.concatenate(d_mn), 'ffn_norm': jnp.concatenate(d_fn),
               'ffn_conv_w': jnp.stack(d_cw, axis=1), 'ffn_conv_b': jnp.stack(d_cb)}
    order = list(small_g)
    flat = jnp.concatenate([loss_part] + [small_g[n].reshape(-1) for n in order] + [G['final_norm']])
    ar_rows = _round_up(-(-flat.size // PACK_W), 8)
    flat = jnp.pad(flat, (0, ar_rows * PACK_W - flat.size)).reshape(ar_rows, PACK_W)
    summed = _sum_slots("allreduce_sum", _allgather_small("allreduce_gather", flat)).reshape(-1)
    loss = summed[0]
    red_last = reduce_finish("last", last, summed)
    G['attn_w_qkv'], G['attn_w_o'] = red_last[0][None], red_last[1][None]
    at = 1
    for n in order + ['final_norm']:
        size = G['final_norm'].size if n == 'final_norm' else small_g[n].size
        piece = summed[at:at + size]
        at += size
        if n in ('pool_norm', 'pool_scale'):
            piece = lax.dynamic_slice(piece, (dev * d_sh,), (d_sh,))
        elif n == 'ffn_conv_w':
            piece = lax.dynamic_index_in_dim(piece.reshape(N_DEV, n_layers, 3, n_up), dev, 0, keepdims=False)
        G[n] = piece.reshape(W[n].shape)

    deltas, new_m, new_v = [], [], []
    for n in names:
        d, nm, nv = _adamw(f"adamw_{n}", W[n], G[n], Mo[n], Vo[n])
        deltas.append(d)
        new_m.append(nm)
        new_v.append(nv)
    return (loss, grad_x[None], *[G[n] for n in names], *deltas, *new_m, *new_v)
```

```python
import jax
import jax.numpy as jnp
from jax import lax
from jax.experimental import pallas as pl
from jax.experimental.pallas import tpu as pltpu

F32 = jnp.float32
BF16 = jnp.bfloat16
MESH = pl.DeviceIdType.MESH

N_DEV = 8
EPS = 1e-6
HEAD_DIM = 64
N_KV_HEADS = 4
X_HEADS = 4
GRID_W = 64
ROPE_THETA = 10000.0
ROPE_PAIRS = HEAD_DIM // 4
POOL_WINDOWS = (2, 4, 8, 16)
POOL_PAD = 16
KEY_CHUNK = 1024
MM_ROWS = 1024
REDUCE_ROWS = 2048
FFN_HALO = 16
LANES = 128
PACK_W = 1024
ADAM_LR, ADAM_B1, ADAM_B2, ADAM_EPS, ADAM_WD, ADAM_STEP = 0.001, 0.9, 0.999, 1e-08, 0.01, 10

_NN = (((1,), (0,)), ((), ()))
_NT = (((1,), (1,)), ((), ()))
_TN = (((0,), (0,)), ((), ()))


def _pc(body, *, name, **kw):
    return pl.pallas_call(body, name=name, **kw)


def _sem(*kinds):
    return pltpu.CompilerParams(dimension_semantics=kinds)


def _tile(n, pref, mult):
    best = None
    for t in range(mult, min(n, pref) + 1, mult):
        if n % t == 0:
            best = t
    return n if best is None else best


def _round_up(n, m):
    return (n + m - 1) // m * m


def _mm_call(name, a, b, dims, grid, a_spec, b_spec, o_spec, out_shape, kaxis, res=None, res_spec=None):
    nk = grid[kaxis]
    acc_shape = tuple(d for d in o_spec.block_shape if d is not None)
    in_place = out_shape.dtype == F32
    use_scratch = nk > 1 and not in_place

    def body(*refs):
        refs = list(refs)
        acc = refs.pop() if use_scratch else None
        a_ref, b_ref = refs[:2]
        r_ref = refs[2] if res is not None else None
        o_ref = refs[-1]
        if len(a_ref.shape) == 3:
            n = a_ref.shape[2]
            prod = sum(jnp.dot(a_ref[s].astype(BF16), b_ref[s * n:(s + 1) * n, :].astype(BF16),
                               preferred_element_type=F32) for s in range(a_ref.shape[0]))
        else:
            prod = lax.dot_general(a_ref[...].astype(BF16), b_ref[...].astype(BF16), dims, preferred_element_type=F32)
        if nk == 1:
            if r_ref is not None:
                prod = prod + r_ref[...]
            o_ref[...] = prod.astype(o_ref.dtype)
            return
        k = pl.program_id(kaxis)
        tgt = o_ref if in_place else acc

        @pl.when(k == 0)
        def _():
            tgt[...] = prod + r_ref[...] if (in_place and r_ref is not None) else prod

        @pl.when(k > 0)
        def _():
            tgt[...] += prod

        if not in_place:
            @pl.when(k == nk - 1)
            def _():
                r = acc[...]
                if r_ref is not None:
                    r = r + r_ref[...]
                o_ref[...] = r.astype(o_ref.dtype)

    sem = tuple("arbitrary" if ax == kaxis else "parallel" for ax in range(len(grid)))
    ins = [a, b] if res is None else [a, b, res]
    specs = [a_spec, b_spec] if res is None else [a_spec, b_spec, res_spec]
    return _pc(body, name=name, grid=grid, in_specs=specs, out_specs=o_spec, out_shape=out_shape,
               scratch_shapes=[pltpu.VMEM(acc_shape, F32)] if use_scratch else [],
               compiler_params=_sem(*sem))(*ins)


def _reduce_rows(a, b):
    return REDUCE_ROWS * (2 if a.dtype == BF16 and b.dtype == BF16 else 1)


def _mm_nn(name, a, b, out_dtype, res=None):
    M, K = a.shape
    N = b.shape[1]
    tm, tn, tk = _tile(M, MM_ROWS, 16), _tile(N, 1024, LANES), _tile(K, 1024, LANES)
    return _mm_call(name, a, b, _NN, (M // tm, N // tn, K // tk),
                    pl.BlockSpec((tm, tk), lambda i, j, k: (i, k)),
                    pl.BlockSpec((tk, tn), lambda i, j, k: (k, j)),
                    pl.BlockSpec((tm, tn), lambda i, j, k: (i, j)),
                    jax.ShapeDtypeStruct((M, N), out_dtype), 2, res,
                    pl.BlockSpec((tm, tn), lambda i, j, k: (i, j)))


def _mm_nt(name, a, b, out_dtype):
    M, K = a.shape
    N = b.shape[0]
    tm, tn, tk = _tile(M, MM_ROWS, 16), _tile(N, 1024, LANES), _tile(K, 1024, LANES)
    return _mm_call(name, a, b, _NT, (M // tm, N // tn, K // tk),
                    pl.BlockSpec((tm, tk), lambda i, j, k: (i, k)),
                    pl.BlockSpec((tn, tk), lambda i, j, k: (j, k)),
                    pl.BlockSpec((tm, tn), lambda i, j, k: (i, j)),
                    jax.ShapeDtypeStruct((M, N), out_dtype), 2)


def _mm_tn(name, a, b, out_dtype):
    R, M = a.shape
    N = b.shape[1]
    tm, tn, tr = _tile(M, 1024, LANES), _tile(N, 1024, LANES), _tile(R, _reduce_rows(a, b), 16)
    return _mm_call(name, a, b, _TN, (M // tm, N // tn, R // tr),
                    pl.BlockSpec((tr, tm), lambda i, j, k: (k, i)),
                    pl.BlockSpec((tr, tn), lambda i, j, k: (k, j)),
                    pl.BlockSpec((tm, tn), lambda i, j, k: (i, j)),
                    jax.ShapeDtypeStruct((M, N), out_dtype), 2)


def _mm_nn_bs(name, a, b, out_dtype, layer=0):
    M, K = a.shape
    J, _, n = b.shape
    tm, tk = _tile(M, MM_ROWS, 16), _tile(K, 1024, LANES)
    first = layer * (K // tk)
    return _mm_call(name, a, b, _NN, (J, M // tm, K // tk),
                    pl.BlockSpec((tm, tk), lambda j, i, k: (i, k)),
                    pl.BlockSpec((None, tk, n), lambda j, i, k: (j, first + k, 0)),
                    pl.BlockSpec((None, tm, n), lambda j, i, k: (j, i, 0)),
                    jax.ShapeDtypeStruct((J, M, n), out_dtype), 2)


def _mm_nn_as(name, a, b, out_dtype, res):
    J, M, n = a.shape
    N = b.shape[1]
    tm, tn = _tile(M, MM_ROWS, 16), _tile(N, 1024, LANES)
    per = J if J <= 4 else (2 if J % 2 == 0 else 1)
    return _mm_call(name, a, b, _NN, (M // tm, N // tn, J // per),
                    pl.BlockSpec((per, tm, n), lambda i, j, k: (k, i, 0)),
                    pl.BlockSpec((per * n, tn), lambda i, j, k: (k, j)),
                    pl.BlockSpec((tm, tn), lambda i, j, k: (i, j)),
                    jax.ShapeDtypeStruct((M, N), out_dtype), 2, res,
                    pl.BlockSpec((tm, tn), lambda i, j, k: (i, j)))


def _mm_tn_as(name, a, b, out_dtype):
    J, R, n = a.shape
    N = b.shape[1]
    tn, tr = _tile(N, 1024, LANES), _tile(R, REDUCE_ROWS, 16)
    return _mm_call(name, a, b, _TN, (J, N // tn, R // tr),
                    pl.BlockSpec((None, tr, n), lambda j, jn, k: (j, k, 0)),
                    pl.BlockSpec((tr, tn), lambda j, jn, k: (k, jn)),
                    pl.BlockSpec((n, tn), lambda j, jn, k: (j, jn)),
                    jax.ShapeDtypeStruct((J * n, N), out_dtype), 2)


def _mm_nt_abs(name, a, b, N, out_dtype, layer=0):
    J, M, n = a.shape
    tm, tn = _tile(M, MM_ROWS, 16), _tile(N, 1024, LANES)
    first = layer * (N // tn)
    return _mm_call(name, a, b, _NT, (M // tm, N // tn, J),
                    pl.BlockSpec((None, tm, n), lambda i, j, k: (k, i, 0)),
                    pl.BlockSpec((None, tn, n), lambda i, j, k: (k, first + j, 0)),
                    pl.BlockSpec((tm, tn), lambda i, j, k: (i, j)),
                    jax.ShapeDtypeStruct((M, N), out_dtype), 2)


def _mm_tn_bs(name, a, b, out_dtype):
    R, M = a.shape
    J, _, n = b.shape
    tm, tr = _tile(M, 1024, LANES), _tile(R, _reduce_rows(a, b), 16)
    return _mm_call(name, a, b, _TN, (J, M // tm, R // tr),
                    pl.BlockSpec((tr, tm), lambda j, i, k: (k, i)),
                    pl.BlockSpec((None, tr, n), lambda j, i, k: (j, k, 0)),
                    pl.BlockSpec((None, tm, n), lambda j, i, k: (j, i, 0)),
                    jax.ShapeDtypeStruct((J, M, n), out_dtype), 2)


def _rmsnorm(name, x, g, out_dtype):
    R, D = x.shape
    tm = _tile(R, 512, 16)

    def body(x_ref, g_ref, o_ref):
        xv = x_ref[...]
        r = lax.rsqrt(jnp.mean(xv * xv, axis=-1, keepdims=True) + EPS)
        o_ref[...] = (xv * r * g_ref[...]).astype(o_ref.dtype)

    return _pc(body, name=name, grid=(R // tm,),
               in_specs=[pl.BlockSpec((tm, D), lambda i: (i, 0)), pl.BlockSpec((1, D), lambda i: (0, 0))],
               out_specs=pl.BlockSpec((tm, D), lambda i: (i, 0)),
               out_shape=jax.ShapeDtypeStruct((R, D), out_dtype), compiler_params=_sem("parallel"))(x, g)


def _rmsnorm_bwd(name, x, g, dh, dres=None):
    R, D = x.shape
    tm = _tile(R, 512, 16)

    def body(*refs):
        if dres is None:
            x_ref, g_ref, dh_ref, dx_ref, dg_ref = refs
            dres_ref = None
        else:
            x_ref, g_ref, dh_ref, dres_ref, dx_ref, dg_ref = refs
        xv = x_ref[...]
        r = lax.rsqrt(jnp.mean(xv * xv, axis=-1, keepdims=True) + EPS)
        xh = xv * r
        dhv = dh_ref[...].astype(F32)

        @pl.when(pl.program_id(0) == 0)
        def _():
            dg_ref[...] = jnp.zeros_like(dg_ref)

        dg_ref[...] += jnp.sum(dhv * xh, axis=0, keepdims=True)
        dxh = dhv * g_ref[...]
        dx = r * (dxh - xh * jnp.mean(dxh * xh, axis=-1, keepdims=True))
        if dres_ref is not None:
            dx = dx + dres_ref[...]
        dx_ref[...] = dx

    row = pl.BlockSpec((tm, D), lambda i: (i, 0))
    vec = pl.BlockSpec((1, D), lambda i: (0, 0))
    ins = [x, g, dh] + ([] if dres is None else [dres])
    specs = [row, vec, row] + ([] if dres is None else [row])
    return _pc(body, name=name, grid=(R // tm,), in_specs=specs, out_specs=(row, vec),
               out_shape=(jax.ShapeDtypeStruct((R, D), F32), jax.ShapeDtypeStruct((1, D), F32)),
               compiler_params=_sem("arbitrary"))(*ins)


def _mm_norm_bwd(name, a, b, grid, a_spec, b_spec, tm, x, gain, dres, after=None):
    M, D = x.shape
    nk = grid[2]

    def body(*refs):
        a_ref, b_ref, x_ref, g_ref, r_ref = refs[:5]
        dx_ref, dg_ref = refs[-3 if nk > 1 else -2:][:2]
        acc = refs[-1] if nk > 1 else None
        i, k = pl.program_id(0), pl.program_id(2)
        if len(a_ref.shape) == 3:
            prod = sum(lax.dot_general(a_ref[s].astype(BF16), b_ref[s].astype(BF16), _NT, preferred_element_type=F32)
                       for s in range(a_ref.shape[0]))
        else:
            prod = lax.dot_general(a_ref[...].astype(BF16), b_ref[...].astype(BF16), _NT, preferred_element_type=F32)

        @pl.when((i == 0) & (k == 0))
        def _():
            dg_ref[...] = jnp.zeros_like(dg_ref)

        def finish(dh):
            xv = x_ref[...]
            r = lax.rsqrt(jnp.mean(xv * xv, axis=-1, keepdims=True) + EPS)
            xh = xv * r
            dg_ref[...] += jnp.sum(dh * xh, axis=0, keepdims=True)
            dxh = dh * g_ref[...]
            dx_ref[...] = r * (dxh - xh * jnp.mean(dxh * xh, axis=-1, keepdims=True)) + r_ref[...]

        if nk == 1:
            finish(prod)
            return

        @pl.when(k == 0)
        def _():
            acc[...] = prod

        @pl.when(k > 0)
        def _():
            acc[...] += prod

        @pl.when(k == nk - 1)
        def _():
            finish(acc[...])

    row = pl.BlockSpec((tm, D), lambda i, j, k: (i, 0))
    vec = pl.BlockSpec((1, D), lambda i, j, k: (0, 0))
    ins = [a, b, x, gain, dres] + ([] if after is None else [after])
    specs = [a_spec, b_spec, row, vec, row] + ([] if after is None else [pl.BlockSpec(memory_space=pl.ANY)])
    return _pc(body, name=name, grid=grid, in_specs=specs, out_specs=(row, vec),
               out_shape=(jax.ShapeDtypeStruct((M, D), F32), jax.ShapeDtypeStruct((1, D), F32)),
               scratch_shapes=[pltpu.VMEM((tm, D), F32)] if nk > 1 else [],
               compiler_params=_sem("arbitrary", "arbitrary", "arbitrary"))(*ins)


def _mm_nt_norm_bwd(name, a, b, x, gain, dres, after=None):
    M, K = a.shape
    D = b.shape[0]
    tm, tk = _tile(M, MM_ROWS, 16), _tile(K, 1024, LANES)
    return _mm_norm_bwd(name, a, b, (M // tm, 1, K // tk),
                        pl.BlockSpec((tm, tk), lambda i, j, k: (i, k)),
                        pl.BlockSpec((D, tk), lambda i, j, k: (0, k)), tm, x, gain, dres, after)


def _mm_nt_abs_norm_bwd(name, a, b, x, gain, dres, layer=0):
    J, M, n = a.shape
    D = x.shape[1]
    tm = _tile(M, MM_ROWS, 16)
    per = 2 if J % 2 == 0 else 1
    return _mm_norm_bwd(name, a, b, (M // tm, 1, J // per),
                        pl.BlockSpec((per, tm, n), lambda i, j, k: (k, i, 0)),
                        pl.BlockSpec((per, D, n), lambda i, j, k: (k, layer, 0)), tm, x, gain, dres)


def _loss_head(name, x, g, tgt):
    R, D = x.shape
    tm = _tile(R, 512, 16)

    def body(x_ref, g_ref, t_ref, dx_ref, dg_ref, l_ref):
        xv = x_ref[...]
        r = lax.rsqrt(jnp.mean(xv * xv, axis=-1, keepdims=True) + EPS)
        xh = xv * r
        err = xh * g_ref[...] - t_ref[...]

        @pl.when(pl.program_id(0) == 0)
        def _():
            dg_ref[...] = jnp.zeros_like(dg_ref)
            l_ref[...] = jnp.zeros_like(l_ref)

        l_ref[...] += jnp.sum(err * err, axis=0, keepdims=True)
        dy = err * (1.0 / D)
        dg_ref[...] += jnp.sum(dy * xh, axis=0, keepdims=True)
        dxh = dy * g_ref[...]
        dx_ref[...] = r * (dxh - xh * jnp.mean(dxh * xh, axis=-1, keepdims=True))

    row = pl.BlockSpec((tm, D), lambda i: (i, 0))
    vec = pl.BlockSpec((1, D), lambda i: (0, 0))
    return _pc(body, name=name, grid=(R // tm,), in_specs=[row, vec, row], out_specs=(row, vec, vec),
               out_shape=(jax.ShapeDtypeStruct((R, D), F32), jax.ShapeDtypeStruct((1, D), F32),
                          jax.ShapeDtypeStruct((1, D), F32)),
               compiler_params=_sem("arbitrary"))(x, g, tgt)


def _rope_tables(S):
    n_rows = S // GRID_W
    row = jnp.repeat(jnp.arange(n_rows, dtype=F32), GRID_W)
    col = jnp.tile(jnp.arange(GRID_W, dtype=F32), n_rows)
    inv_freq = ROPE_THETA ** (-jnp.arange(ROPE_PAIRS, dtype=F32) / ROPE_PAIRS)
    ang = jnp.stack([row[:, None] * inv_freq, col[:, None] * inv_freq], axis=1)
    cos, sin = jnp.cos(ang), jnp.sin(ang)
    c = jnp.broadcast_to(cos[:, :, None, :], (S, 2, 2, ROPE_PAIRS)).reshape(S, HEAD_DIM)
    s = jnp.stack([-sin, sin], axis=2).reshape(S, HEAD_DIM)
    reps = LANES // HEAD_DIM
    return jnp.tile(c, (1, reps)), jnp.tile(s, (1, reps))


def _head_mean_matrix():
    h = jnp.arange(LANES) // HEAD_DIM
    m = jnp.where(h[:, None] == h[None, :], 1.0 / HEAD_DIM, 0.0).astype(BF16)
    return jnp.concatenate([m, m], axis=0)


def _head_mean(v, bd):
    hi = v.astype(BF16)
    lo = (v - hi.astype(F32)).astype(BF16)
    return jnp.dot(jnp.concatenate([hi, lo], axis=1), bd, preferred_element_type=F32)


def _swap_halves(y):
    lane = lax.broadcasted_iota(jnp.int32, y.shape, 1)
    return jnp.where(lane % 32 < 16, pltpu.roll(y, LANES - 16, 1), pltpu.roll(y, 16, 1))


def _qk_rope(name, qkv, gain, scale, cos, sin, bd, n_rot):
    S, W = qkv.shape
    tm = _tile(S, 2048, 16)
    per = LANES // HEAD_DIM

    def body(x_ref, g_ref, s_ref, c_ref, sn_ref, bd_ref, o_ref):
        j = pl.program_id(1)
        xv = x_ref[...]

        def put(v):
            for h in range(per):
                o_ref[h] = v[:, h * HEAD_DIM:(h + 1) * HEAD_DIM].astype(BF16)

        @pl.when(j < n_rot)
        def _():
            ms = _head_mean(xv * xv, bd_ref[...])
            y = xv * lax.rsqrt(ms + EPS) * g_ref[...] * s_ref[...]
            put(y * c_ref[...] + _swap_halves(y) * sn_ref[...])

        @pl.when(j >= n_rot)
        def _():
            put(xv)

    blk = pl.BlockSpec((tm, LANES), lambda i, j: (i, j))
    vec = pl.BlockSpec((1, LANES), lambda i, j: (0, j))
    tab = pl.BlockSpec((tm, LANES), lambda i, j: (i, 0))
    return _pc(body, name=name, grid=(S // tm, W // LANES),
               in_specs=[blk, vec, vec, tab, tab, pl.BlockSpec((2 * LANES, LANES), lambda i, j: (0, 0))],
               out_specs=pl.BlockSpec((per, tm, HEAD_DIM), lambda i, j: (j, i, 0)),
               out_shape=jax.ShapeDtypeStruct((W // HEAD_DIM, S, HEAD_DIM), BF16),
               compiler_params=_sem("parallel", "parallel"))(qkv, gain, scale, cos, sin, bd)


def _qk_rope_bwd(name, dq, dk, dv, qkv, gain, scale, cos, sin, bd):
    S, W = qkv.shape
    tm = _tile(S, 2048, 16)
    per = LANES // HEAD_DIM
    nq, nk, nv = dq.shape[0] // per, dk.shape[0] // per, dv.shape[0] // per
    n_rot = nq + nk

    def body(dq_ref, dk_ref, dv_ref, x_ref, g_ref, s_ref, c_ref, sn_ref, bd_ref, dx_ref, dg_ref):
        j, i = pl.program_id(0), pl.program_id(1)

        @pl.when(i == 0)
        def _():
            dg_ref[...] = jnp.zeros_like(dg_ref)

        def rotate_back(d_ref):
            dv = jnp.concatenate([d_ref[h] for h in range(per)], axis=1)
            xv = x_ref[...]
            ms = _head_mean(xv * xv, bd_ref[...])
            r = lax.rsqrt(ms + EPS)
            z = xv * r
            dy = (dv * c_ref[...] - _swap_halves(dv) * sn_ref[...]) * s_ref[...]
            dg_ref[...] += jnp.sum(dy * z, axis=0, keepdims=True)
            dz = dy * g_ref[...]
            mz = _head_mean(dz * z, bd_ref[...])
            dx_ref[...] = (r * (dz - z * mz)).astype(BF16)

        @pl.when(j < nq)
        def _():
            rotate_back(dq_ref)

        @pl.when((j >= nq) & (j < n_rot))
        def _():
            rotate_back(dk_ref)

        @pl.when(j >= n_rot)
        def _():
            dx_ref[...] = jnp.concatenate([dv_ref[h] for h in range(per)], axis=1).astype(BF16)

    def part(first, count):
        return pl.BlockSpec((per, tm, HEAD_DIM), lambda j, i: (jnp.clip(j - first, 0, count - 1), i, 0))

    blk = pl.BlockSpec((tm, LANES), lambda j, i: (i, j))
    vec = pl.BlockSpec((1, LANES), lambda j, i: (0, j))
    tab = pl.BlockSpec((tm, LANES), lambda j, i: (i, 0))
    return _pc(body, name=name, grid=(W // LANES, S // tm),
               in_specs=[part(0, nq), part(nq, nk), part(n_rot, nv), blk, vec, vec, tab, tab,
                         pl.BlockSpec((2 * LANES, LANES), lambda j, i: (0, 0))],
               out_specs=(blk, vec),
               out_shape=(jax.ShapeDtypeStruct((S, W), BF16), jax.ShapeDtypeStruct((1, W), F32)),
               compiler_params=_sem("parallel", "arbitrary"))(dq, dk, dv, qkv, gain, scale, cos, sin, bd)


def _softmax_rows(s):
    m = jnp.max(s, axis=-1, keepdims=True)
    p = jnp.exp(s - m)
    return p, jnp.sum(p, axis=-1, keepdims=True)


def _attn_fwd(name, qkv, H):
    _, S, dh = qkv.shape
    G = H // N_KV_HEADS
    tq = _tile(S, 256, 16)
    kc = _tile(S, KEY_CHUNK, LANES)
    R = G * tq

    def body(q_ref, k_ref, v_ref, o_ref, lse_ref):
        q = q_ref[...].reshape(R, dh)
        m = jnp.full((R, 1), -1e30, F32)
        l = jnp.zeros((R, 1), F32)
        acc = jnp.zeros((R, dh), F32)
        for c in range(S // kc):
            rows = slice(c * kc, (c + 1) * kc)
            s = lax.dot_general(q, k_ref[rows, :], _NT, preferred_element_type=F32)
            m_new = jnp.maximum(m, jnp.max(s, axis=-1, keepdims=True))
            alpha = jnp.exp(m - m_new)
            p = jnp.exp(s - m_new)
            l = alpha * l + jnp.sum(p, axis=-1, keepdims=True)
            acc = alpha * acc + jnp.dot(p.astype(BF16), v_ref[rows, :], preferred_element_type=F32)
            m = m_new
        o_ref[...] = (acc / l).astype(BF16).reshape(G, tq, dh)
        lse_ref[...] = (m + jnp.log(l)).reshape(G, tq, 1)

    qs = pl.BlockSpec((G, tq, dh), lambda kv, i: (kv, i, 0))
    ls = pl.BlockSpec((G, tq, 1), lambda kv, i: (kv, i, 0))
    ks = pl.BlockSpec((None, S, dh), lambda kv, i: (H + kv, 0, 0))
    vs = pl.BlockSpec((None, S, dh), lambda kv, i: (H + N_KV_HEADS + kv, 0, 0))
    return _pc(body, name=name, grid=(N_KV_HEADS, S // tq), in_specs=[qs, ks, vs], out_specs=(qs, ls),
               out_shape=(jax.ShapeDtypeStruct((H, S, dh), BF16), jax.ShapeDtypeStruct((H, S, 1), F32)),
               compiler_params=_sem("parallel", "parallel"))(qkv, qkv, qkv)


def _attn_bwd(name, qkv, o, lse, do):
    H, S, dh = o.shape
    G = H // N_KV_HEADS
    tq = _tile(S, 128, 16)
    kc = _tile(S, KEY_CHUNK, LANES)
    R = G * tq

    def body(q_ref, k_ref, v_ref, o_ref, lse_ref, do_ref, dq_ref, dk_ref, dv_ref):
        @pl.when(pl.program_id(1) == 0)
        def _():
            dk_ref[...] = jnp.zeros_like(dk_ref)
            dv_ref[...] = jnp.zeros_like(dv_ref)

        qq, dd = q_ref[...].reshape(R, dh), do_ref[...].reshape(R, dh)
        delta = jnp.sum(dd.astype(F32) * o_ref[...].reshape(R, dh).astype(F32), axis=-1, keepdims=True)
        lse = lse_ref[...].reshape(R, 1)
        dq = jnp.zeros((R, dh), F32)
        for c in range(S // kc):
            rows = slice(c * kc, (c + 1) * kc)
            kk, vv = k_ref[rows, :], v_ref[rows, :]
            p = jnp.exp(lax.dot_general(qq, kk, _NT, preferred_element_type=F32) - lse)
            dv_ref[rows, :] += lax.dot_general(p.astype(BF16), dd, _TN, preferred_element_type=F32)
            dp = lax.dot_general(dd, vv, _NT, preferred_element_type=F32)
            ds = (p * (dp - delta)).astype(BF16)
            dq = dq + jnp.dot(ds, kk, preferred_element_type=F32)
            dk_ref[rows, :] += lax.dot_general(ds, qq, _TN, preferred_element_type=F32)
        dq_ref[...] = dq.reshape(G, tq, dh)

    qs = pl.BlockSpec((G, tq, dh), lambda kv, i: (kv, i, 0))
    ls = pl.BlockSpec((G, tq, 1), lambda kv, i: (kv, i, 0))
    ks = pl.BlockSpec((None, S, dh), lambda kv, i: (H + kv, 0, 0))
    vs = pl.BlockSpec((None, S, dh), lambda kv, i: (H + N_KV_HEADS + kv, 0, 0))
    acc = pl.BlockSpec((None, S, dh), lambda kv, i: (kv, 0, 0))
    return _pc(body, name=name, grid=(N_KV_HEADS, S // tq), in_specs=[qs, ks, vs, qs, ls, qs],
               out_specs=(qs, acc, acc),
               out_shape=(jax.ShapeDtypeStruct((H, S, dh), F32), jax.ShapeDtypeStruct((N_KV_HEADS, S, dh), F32),
                          jax.ShapeDtypeStruct((N_KV_HEADS, S, dh), F32)),
               compiler_params=_sem("parallel", "arbitrary"))(qkv, qkv, qkv, o, lse, do)


def _xattn_fwd(name, q, kv):
    S, D = q.shape
    _, M, dh = kv.shape
    scale = dh ** -0.5
    tq = _tile(S, 512, 16)

    def body(q_ref, kv_ref, o_ref):
        for h in range(X_HEADS):
            lo, hi = h * dh, (h + 1) * dh
            s = lax.dot_general(q_ref[:, lo:hi], kv_ref[h], _NT, preferred_element_type=F32) * scale
            p, l = _softmax_rows(s)
            o = jnp.dot(p.astype(BF16), kv_ref[X_HEADS + h], preferred_element_type=F32)
            o_ref[:, lo:hi] = (o / l).astype(BF16)

    row = pl.BlockSpec((tq, D), lambda i: (i, 0))
    return _pc(body, name=name, grid=(S // tq,),
               in_specs=[row, pl.BlockSpec((2 * X_HEADS, M, dh), lambda i: (0, 0, 0))],
               out_specs=row, out_shape=jax.ShapeDtypeStruct((S, D), BF16),
               compiler_params=_sem("parallel"))(q, kv)


def _xattn_bwd(name, q, kv, do):
    S, D = q.shape
    _, M, dh = kv.shape
    scale = dh ** -0.5
    tq = _tile(S, 512, 16)

    def body(q_ref, kv_ref, do_ref, dq_ref, dkv_ref):
        @pl.when(pl.program_id(0) == 0)
        def _():
            dkv_ref[...] = jnp.zeros_like(dkv_ref)

        for h in range(X_HEADS):
            lo, hi = h * dh, (h + 1) * dh
            qh, kh, vh, doh = q_ref[:, lo:hi], kv_ref[h], kv_ref[X_HEADS + h], do_ref[:, lo:hi]
            s = lax.dot_general(qh, kh, _NT, preferred_element_type=F32) * scale
            p, l = _softmax_rows(s)
            pn = p / l
            dkv_ref[X_HEADS + h] += lax.dot_general(pn.astype(BF16), doh, _TN, preferred_element_type=F32)
            dp = lax.dot_general(doh, vh, _NT, preferred_element_type=F32)
            ds = (pn * (dp - jnp.sum(pn * dp, axis=-1, keepdims=True)) * scale).astype(BF16)
            dq_ref[:, lo:hi] = jnp.dot(ds, kh, preferred_element_type=F32).astype(BF16)
            dkv_ref[h] += lax.dot_general(ds, qh, _TN, preferred_element_type=F32)

    row = pl.BlockSpec((tq, D), lambda i: (i, 0))
    full = pl.BlockSpec((2 * X_HEADS, M, dh), lambda i: (0, 0, 0))
    return _pc(body, name=name, grid=(S // tq,), in_specs=[row, full, row], out_specs=(row, full),
               out_shape=(jax.ShapeDtypeStruct((S, D), BF16), jax.ShapeDtypeStruct((2 * X_HEADS, M, dh), F32)),
               compiler_params=_sem("arbitrary"))(q, kv, do)


def _sigmoid(x):
    return 1.0 / (1.0 + jnp.exp(-x))


def _halo_specs(tm, n, S):
    nb = tm // 8
    last8 = S // 8 - 1
    main = pl.BlockSpec((2, None, tm, n), lambda j, i: (0, j, i, 0))
    prev = pl.BlockSpec((2, None, 8, n), lambda j, i: (0, j, jnp.maximum(i * nb - 1, 0), 0))
    nxt = pl.BlockSpec((2, None, 8, n), lambda j, i: (0, j, jnp.minimum((i + 1) * nb, last8), 0))
    return main, prev, nxt


def _ffn_up_act(name, h, w, cw, cb, layer):
    S, K = h.shape
    _, J, _, n = w.shape
    tm = _tile(S, 512, FFN_HALO)
    nblk = S // tm
    hb, last = tm // FFN_HALO, S // FFN_HALO - 1
    te = tm + 2 * FFN_HALO

    def body(h_ref, hp_ref, hn_ref, w_ref, cw_ref, b_ref, u_ref, a_ref):
        i = pl.program_id(1)
        zero = jnp.zeros((FFN_HALO, K), BF16)
        he = jnp.concatenate([jnp.where(i == 0, zero, hp_ref[...]), h_ref[...],
                              jnp.where(i == nblk - 1, zero, hn_ref[...])], axis=0)
        mid = slice(FFN_HALO, tm + FFN_HALO)
        c = []
        for half in range(2):
            ue = jnp.dot(he, w_ref[half], preferred_element_type=F32)
            um = ue[mid]
            u_ref[half] = um
            k = cw_ref[half]
            c.append(pltpu.roll(ue, 1, 0)[mid] * k[0:1] + um * k[1:2] + pltpu.roll(ue, te - 1, 0)[mid] * k[2:3]
                     + b_ref[half])
        a_ref[...] = (c[0] * _sigmoid(c[0]) * c[1]).astype(BF16)

    return _pc(body, name=name, grid=(J, nblk),
               in_specs=[pl.BlockSpec((tm, K), lambda j, i: (i, 0)),
                         pl.BlockSpec((FFN_HALO, K), lambda j, i: (jnp.maximum(i * hb - 1, 0), 0)),
                         pl.BlockSpec((FFN_HALO, K), lambda j, i: (jnp.minimum((i + 1) * hb, last), 0)),
                         pl.BlockSpec((2, None, K, n), lambda j, i: (0, j, layer, 0)),
                         pl.BlockSpec((2, None, 3, n), lambda j, i: (0, j, 0, 0)),
                         pl.BlockSpec((2, None, 1, n), lambda j, i: (0, j, 0, 0))],
               out_specs=(pl.BlockSpec((2, None, tm, n), lambda j, i: (0, j, i, 0)),
                          pl.BlockSpec((None, tm, n), lambda j, i: (j, i, 0))),
               out_shape=(jax.ShapeDtypeStruct((2, J, S, n), F32), jax.ShapeDtypeStruct((J, S, n), BF16)),
               compiler_params=_sem("parallel", "parallel"))(h, h, h, w, cw, cb)


def _ffn_act_bwd(name, u, g, w_down, cw, cb):
    _, J, S, n = u.shape
    D = g.shape[1]
    tm = _tile(S, 256, 16)
    nblk = S // tm
    te = tm + 16
    nb = tm // 8
    last8 = S // 8 - 1

    def body(u_ref, up_ref, un_ref, g_ref, gp_ref, gn_ref, wd_ref, w_ref, b_ref, du_ref, st_ref):
        i = pl.program_id(1)

        @pl.when(i == 0)
        def _():
            st_ref[...] = jnp.zeros_like(st_ref)

        def extended(before, main, after):
            return jnp.concatenate([jnp.where(i == 0, 0.0, before), main, jnp.where(i == nblk - 1, 0.0, after)], axis=0)

        mid = slice(8, tm + 8)
        da_e = lax.dot_general(extended(gp_ref[...], g_ref[...], gn_ref[...]).astype(BF16), wd_ref[...], _NT,
                               preferred_element_type=F32)
        ue, c = [], []
        for half in range(2):
            e = extended(up_ref[half], u_ref[half], un_ref[half])
            w = w_ref[half]
            ue.append((pltpu.roll(e, 1, 0), e, pltpu.roll(e, te - 1, 0)))
            c.append(ue[half][0] * w[0:1] + e * w[1:2] + ue[half][2] * w[2:3] + b_ref[half])
        sg = _sigmoid(c[0])
        dc = [da_e * c[1] * (sg * (1.0 + c[0] * (1.0 - sg))), da_e * (c[0] * sg)]
        r8 = lax.broadcasted_iota(jnp.int32, (8, n), 0)
        for half in range(2):
            w, d, (e_before, e, e_after) = w_ref[half], dc[half], ue[half]
            dm = d[mid]
            du = pltpu.roll(d, te - 1, 0)[mid] * w[0:1] + dm * w[1:2] + pltpu.roll(d, 1, 0)[mid] * w[2:3]
            du_ref[half] = du.astype(BF16)
            s0 = jnp.sum(dm * e_before[mid], axis=0, keepdims=True)
            s1 = jnp.sum(dm * e[mid], axis=0, keepdims=True)
            s2 = jnp.sum(dm * e_after[mid], axis=0, keepdims=True)
            s3 = jnp.sum(dm, axis=0, keepdims=True)
            st_ref[half] += jnp.where(r8 == 0, s0, jnp.where(r8 == 1, s1, jnp.where(r8 == 2, s2,
                                      jnp.where(r8 == 3, s3, 0.0))))

    main, prev, nxt = _halo_specs(tm, n, S)
    gmain = pl.BlockSpec((tm, D), lambda j, i: (i, 0))
    gprev = pl.BlockSpec((8, D), lambda j, i: (jnp.maximum(i * nb - 1, 0), 0))
    gnxt = pl.BlockSpec((8, D), lambda j, i: (jnp.minimum((i + 1) * nb, last8), 0))
    return _pc(body, name=name, grid=(J, nblk),
               in_specs=[main, prev, nxt, gmain, gprev, gnxt, pl.BlockSpec((n, D), lambda j, i: (j, 0)),
                         pl.BlockSpec((2, None, 3, n), lambda j, i: (0, j, 0, 0)),
                         pl.BlockSpec((2, None, 1, n), lambda j, i: (0, j, 0, 0))],
               out_specs=(main, pl.BlockSpec((2, None, 8, n), lambda j, i: (0, j, 0, 0))),
               out_shape=(jax.ShapeDtypeStruct((2, J, S, n), BF16), jax.ShapeDtypeStruct((2, J, 8, n), F32)),
               compiler_params=_sem("parallel", "arbitrary"))(u, u, u, g, g, g, w_down, cw, cb)


def _window_count(t, w, S):
    lo = jnp.maximum(t - w // 2, 0)
    hi = jnp.minimum(t + w - w // 2, S)
    return (hi - lo).astype(F32)


def _trailing_sums(x, w):
    k = 1
    while k < w:
        x = x + pltpu.roll(x, k, 0)
        k *= 2
    return x


def _pool_window(name, h, group_w, adjoint, out_dtype):
    S, D = h.shape
    SP = S + 2 * POOL_PAD
    per_group = group_w // LANES

    def body(h_ref, o_ref, xp):
        g = pl.program_id(0) // per_group
        t = lax.broadcasted_iota(jnp.int32, (S, LANES), 0)
        xp[0:POOL_PAD, :] = jnp.zeros((POOL_PAD, LANES), F32)
        xp[S + POOL_PAD:SP, :] = jnp.zeros((POOL_PAD, LANES), F32)
        for gi, w in enumerate(POOL_WINDOWS):
            @pl.when(g == gi)
            def _():
                hv = h_ref[...]
                cnt = _window_count(t, w, S)
                xp[POOL_PAD:S + POOL_PAD, :] = hv / cnt if adjoint else hv
                ahead = w // 2 if adjoint else w // 2 - 1
                sw = _trailing_sums(xp[...], w)
                if ahead:
                    sw = pltpu.roll(sw, SP - ahead, 0)
                win = sw[POOL_PAD:S + POOL_PAD]
                o_ref[...] = ((win if adjoint else win / cnt) - hv).astype(out_dtype)

    col = pl.BlockSpec((S, LANES), lambda j: (0, j))
    return _pc(body, name=name, grid=(D // LANES,), in_specs=[col], out_specs=col,
               out_shape=jax.ShapeDtypeStruct((S, D), out_dtype),
               scratch_shapes=[pltpu.VMEM((SP, LANES), F32)], compiler_params=_sem("parallel"))(h)


def _pool_proj(name, mixed, w, scale, res):
    S, D = mixed.shape
    G, gw, _ = w.shape
    tm = _tile(S, 512, 16)

    def body(m_ref, w_ref, s_ref, r_ref, o_ref):
        for g in range(G):
            lo, hi = g * gw, (g + 1) * gw
            y = jnp.dot(m_ref[:, lo:hi], w_ref[g], preferred_element_type=F32)
            o_ref[:, lo:hi] = r_ref[:, lo:hi] + y * s_ref[:, lo:hi]

    row = pl.BlockSpec((tm, D), lambda i: (i, 0))
    return _pc(body, name=name, grid=(S // tm,),
               in_specs=[row, pl.BlockSpec((G, gw, gw), lambda i: (0, 0, 0)), pl.BlockSpec((1, D), lambda i: (0, 0)), row],
               out_specs=row, out_shape=jax.ShapeDtypeStruct((S, D), F32),
               compiler_params=_sem("parallel"))(mixed, w, scale, res)


def _pool_proj_bwd(name, dy, mixed, w, scale):
    S, D = mixed.shape
    G, gw, _ = w.shape
    tm = _tile(S, 512, 16)

    def body(dy_ref, m_ref, w_ref, s_ref, dm_ref, dw_ref, ds_ref):
        @pl.when(pl.program_id(0) == 0)
        def _():
            dw_ref[...] = jnp.zeros_like(dw_ref)
            ds_ref[...] = jnp.zeros_like(ds_ref)

        for g in range(G):
            lo, hi = g * gw, (g + 1) * gw
            mg, dyg = m_ref[:, lo:hi], dy_ref[:, lo:hi]
            y = jnp.dot(mg, w_ref[g], preferred_element_type=F32)
            ds_ref[:, lo:hi] += jnp.sum(dyg * y, axis=0, keepdims=True)
            dyp = (dyg * s_ref[:, lo:hi]).astype(BF16)
            dm_ref[:, lo:hi] = lax.dot_general(dyp, w_ref[g], _NT, preferred_element_type=F32)
            dw_ref[g] += lax.dot_general(mg, dyp, _TN, preferred_element_type=F32)

    row = pl.BlockSpec((tm, D), lambda i: (i, 0))
    wsp = pl.BlockSpec((G, gw, gw), lambda i: (0, 0, 0))
    vec = pl.BlockSpec((1, D), lambda i: (0, 0))
    return _pc(body, name=name, grid=(S // tm,), in_specs=[row, row, wsp, vec], out_specs=(row, wsp, vec),
               out_shape=(jax.ShapeDtypeStruct((S, D), F32), jax.ShapeDtypeStruct((G, gw, gw), F32),
                          jax.ShapeDtypeStruct((1, D), F32)),
               compiler_params=_sem("arbitrary"))(dy, mixed, w, scale)


def _adamw(name, w, g, m, v):
    shape = w.shape
    C = shape[-1]
    R = w.size // C
    tm = _tile(R, 512, 8)

    def body(w_ref, g_ref, m_ref, v_ref, d_ref, nm_ref, nv_ref):
        gv = g_ref[...]
        nm = ADAM_B1 * m_ref[...] + (1.0 - ADAM_B1) * gv
        nv = ADAM_B2 * v_ref[...] + (1.0 - ADAM_B2) * (gv * gv)
        m_hat = nm / (1.0 - ADAM_B1 ** ADAM_STEP)
        v_hat = nv / (1.0 - ADAM_B2 ** ADAM_STEP)
        d_ref[...] = -ADAM_LR * (m_hat / (jnp.sqrt(v_hat) + ADAM_EPS) + ADAM_WD * w_ref[...])
        nm_ref[...] = nm
        nv_ref[...] = nv

    blk = pl.BlockSpec((tm, C), lambda i: (i, 0))
    sd = jax.ShapeDtypeStruct((R, C), F32)
    outs = _pc(body, name=name, grid=(R // tm,), in_specs=[blk] * 4, out_specs=(blk,) * 3, out_shape=(sd,) * 3,
               compiler_params=_sem("parallel"))(*(a.reshape(R, C) for a in (w, g, m, v)))
    return tuple(o.reshape(shape) for o in outs)


def _position():
    return lax.axis_index("x"), lax.axis_index("y"), lax.axis_index("c")


def _flip(v, bit):
    return 1 - v if bit else v


def _allgather_small(name, v):
    R, W = v.shape

    def body(v_ref, out_ref, send_sems, recv_sems):
        x, y, c = _position()
        me = 4 * x + 2 * y + c
        out_ref[me] = v_ref[...]
        sends = []
        for k in range(1, N_DEV):
            peer = (_flip(x, k & 4), _flip(y, k & 2), _flip(c, k & 1))
            cp = pltpu.make_async_remote_copy(src_ref=v_ref, dst_ref=out_ref.at[me], send_sem=send_sems.at[k - 1],
                                              recv_sem=recv_sems.at[k - 1], device_id=peer, device_id_type=MESH)
            cp.start()
            sends.append(cp)
        for k in range(1, N_DEV):
            peer = (_flip(x, k & 4), _flip(y, k & 2), _flip(c, k & 1))
            slot = 4 * peer[0] + 2 * peer[1] + peer[2]
            pltpu.make_async_remote_copy(src_ref=v_ref, dst_ref=out_ref.at[slot], send_sem=send_sems.at[k - 1],
                                         recv_sem=recv_sems.at[k - 1], device_id=peer, device_id_type=MESH).wait_recv()
        for cp in sends:
            cp.wait_send()

    vm = pl.BlockSpec(memory_space=pltpu.VMEM)
    return _pc(body, name=name, in_specs=[vm], out_specs=vm, out_shape=jax.ShapeDtypeStruct((N_DEV, R, W), F32),
               scratch_shapes=[pltpu.SemaphoreType.DMA((N_DEV - 1,)), pltpu.SemaphoreType.DMA((N_DEV - 1,))])(v)


def _sum_slots(name, a):
    n, R, W = a.shape

    def body(a_ref, o_ref):
        acc = a_ref[0]
        for s in range(1, n):
            acc = acc + a_ref[s]
        o_ref[...] = acc

    return _pc(body, name=name, grid=(1,), in_specs=[pl.BlockSpec((n, R, W), lambda i: (0, 0, 0))],
               out_specs=pl.BlockSpec((R, W), lambda i: (0, 0)), out_shape=jax.ShapeDtypeStruct((R, W), F32))(a)


def _allgather_blocks(name, blocks):
    n = len(blocks)

    def body(*refs):
        b_refs, out_refs, token = refs[:n], refs[n:2 * n], refs[2 * n]
        send_sems, recv_sems, local_sems = refs[2 * n + 1:]
        token[...] = jnp.zeros_like(token)
        x, y, c = _position()
        me, sibling = (x, y, c), (x, y, 1 - c)
        chips = [(1 - x, y), (x, 1 - y), (1 - x, 1 - y)]

        def slot(i, px, py, pc):
            return out_refs[i].at[4 * px + 2 * py + pc]

        def copy(i, k, block, to, src=None):
            return pltpu.make_async_remote_copy(src_ref=slot(i, *block) if src is None else src, dst_ref=slot(i, *block),
                                                send_sem=send_sems.at[k, i], recv_sem=recv_sems.at[k, i],
                                                device_id=to, device_id_type=MESH)

        mine = [pltpu.make_async_copy(b_refs[i], slot(i, *me), local_sems.at[i]) for i in range(n)]
        first = [copy(i, 1 + j, me, (*chip, c), src=b_refs[i]) for i in range(n) for j, chip in enumerate(chips)]
        first += [copy(i, 0, me, sibling, src=b_refs[i]) for i in range(n)]
        for cp in mine + first:
            cp.start()
        passed = []
        for j, chip in enumerate(chips):
            for i in range(n):
                copy(i, 1 + j, (*chip, c), me).wait_recv()
                passed.append(copy(i, 4 + j, (*chip, c), sibling))
                passed[-1].start()
        for i in range(n):
            copy(i, 0, sibling, me).wait_recv()
        for j, chip in enumerate(chips):
            for i in range(n):
                copy(i, 4 + j, (*chip, 1 - c), me).wait_recv()
        for cp in first + passed:
            cp.wait_send()
        for cp in mine:
            cp.wait()

    hbm = pl.BlockSpec(memory_space=pl.ANY)
    return _pc(body, name=name, in_specs=[hbm] * n, out_specs=[hbm] * n + [pl.BlockSpec(memory_space=pltpu.VMEM)],
               out_shape=[jax.ShapeDtypeStruct((N_DEV,) + b.shape, b.dtype) for b in blocks]
               + [jax.ShapeDtypeStruct((8, LANES), F32)],
               scratch_shapes=[pltpu.SemaphoreType.DMA((7, n)), pltpu.SemaphoreType.DMA((7, n)),
                               pltpu.SemaphoreType.DMA((n,))])(*blocks)


def _add_sibling(name, g4, r1, pos):
    n, _, L, W = g4.shape
    tl = _tile(L, MM_ROWS, 16)

    def body(pos_ref, g_ref, r_ref, tb_ref, own_ref):
        t = g_ref[...] + r_ref[...]
        tb_ref[...] = t.astype(BF16)

        @pl.when(pl.program_id(1) == pos_ref[1])
        def _():
            own_ref[...] = t

    gs = pltpu.PrefetchScalarGridSpec(
        num_scalar_prefetch=1, grid=(L // tl, n),
        in_specs=[pl.BlockSpec((None, None, tl, W), lambda i, k, p: (k, p[0], i, 0)),
                  pl.BlockSpec((None, tl, W), lambda i, k, p: (k, i, 0))],
        out_specs=(pl.BlockSpec((None, tl, W), lambda i, k, p: (k, i, 0)),
                   pl.BlockSpec((tl, W), lambda i, k, p: (i, 0))))
    return _pc(body, name=name, grid_spec=gs,
               out_shape=(jax.ShapeDtypeStruct((n, L, W), BF16), jax.ShapeDtypeStruct((L, W), F32)),
               compiler_params=_sem("parallel", "arbitrary"))(pos, g4, r1)


def _add_chips(name, own, r2):
    L, W = own.shape
    tl = _tile(L, MM_ROWS, 16)

    def body(o_ref, r_ref, out_ref):
        acc = o_ref[...]
        for j in range(3):
            acc = acc + r_ref[j].astype(F32)
        out_ref[...] = acc

    return _pc(body, name=name, grid=(L // tl,),
               in_specs=[pl.BlockSpec((tl, W), lambda i: (i, 0)), pl.BlockSpec((3, tl, W), lambda i: (0, i, 0))],
               out_specs=pl.BlockSpec((tl, W), lambda i: (i, 0)), out_shape=jax.ShapeDtypeStruct((L, W), F32),
               compiler_params=_sem("parallel"))(own, r2)


_HBM = pl.BlockSpec(memory_space=pltpu.HBM)
_SEM = pl.BlockSpec(memory_space=pltpu.SEMAPHORE)
_EFFECT = pltpu.SideEffectType.DATAFLOW_SIDE_EFFECTING


def _in_hbm(a):
    return pltpu.with_memory_space_constraint(a, pltpu.HBM)


def _after(x, token):
    return x + token[0, 0].astype(x.dtype)


def _copies_start(name, bufs, sem_shape, plan):
    nb = len(bufs)

    def body(*refs):
        for cp in plan(refs[:nb], refs[nb], refs[nb + 1]):
            cp.start()
        refs[-1][...] = jnp.zeros_like(refs[-1])

    out = _pc(body, name=name, in_specs=[_HBM] * nb,
              out_specs=(_SEM, _SEM, *[_HBM] * nb, pl.BlockSpec(memory_space=pltpu.VMEM)),
              out_shape=(pltpu.SemaphoreType.DMA(sem_shape), pltpu.SemaphoreType.DMA(sem_shape),
                         *[pltpu.HBM(b.shape, b.dtype) for b in bufs], jax.ShapeDtypeStruct((8, LANES), F32)),
              input_output_aliases={i: 2 + i for i in range(nb)},
              compiler_params=pltpu.CompilerParams(has_side_effects=_EFFECT))(*[_in_hbm(b) for b in bufs])
    return out[0], out[1], list(out[2:2 + nb]), out[-1]


def _copies_wait(name, bufs, send_sems, recv_sems, plan, after):
    nb = len(bufs)

    def body(*refs):
        for cp in plan(refs[:nb], refs[nb], refs[nb + 1]):
            cp.wait_send()
            cp.wait_recv()

    return list(_pc(body, name=name, in_specs=[_HBM] * nb + [_SEM, _SEM, pl.BlockSpec(memory_space=pl.ANY)],
                    out_specs=[_HBM] * nb, out_shape=[pltpu.HBM(b.shape, b.dtype) for b in bufs],
                    input_output_aliases={i: i for i in range(nb)},
                    compiler_params=pltpu.CompilerParams(has_side_effects=_EFFECT))(*bufs, send_sems, recv_sems, after))


def _plan_gather_chips(n):
    def plan(refs, send_sems, recv_sems):
        x, y, c = _position()
        peers = [(x, y, 1 - c), (1 - x, y, c), (x, 1 - y, c), (1 - x, 1 - y, c)]
        return [pltpu.make_async_remote_copy(src_ref=refs[i], dst_ref=refs[n + i].at[4 * x + 2 * y + c],
                                             send_sem=send_sems.at[k * n + i], recv_sem=recv_sems.at[k * n + i],
                                             device_id=peer, device_id_type=MESH)
                for i in range(n) for k, peer in enumerate(peers)]
    return plan


def _plan_gather_sibling(n):
    def plan(refs, send_sems, recv_sems):
        x, y, c = _position()
        slots = [4 * (1 - x) + 2 * y + c, 4 * x + 2 * (1 - y) + c, 4 * (1 - x) + 2 * (1 - y) + c]
        return [pltpu.make_async_remote_copy(src_ref=refs[i].at[s], dst_ref=refs[i].at[s],
                                             send_sem=send_sems.at[k * n + i], recv_sem=recv_sems.at[k * n + i],
                                             device_id=(x, y, 1 - c), device_id_type=MESH)
                for i in range(n) for k, s in enumerate(slots)]
    return plan


def _plan_reduce_sibling(n):
    def plan(refs, send_sems, recv_sems):
        x, y, c = _position()
        return [pltpu.make_async_remote_copy(src_ref=refs[i].at[k, 1 - c], dst_ref=refs[n + i].at[k],
                                             send_sem=send_sems.at[k * n + i], recv_sem=recv_sems.at[k * n + i],
                                             device_id=(x, y, 1 - c), device_id_type=MESH)
                for i in range(n) for k in range(N_DEV // 2)]
    return plan


def _plan_reduce_chips(n):
    def plan(refs, send_sems, recv_sems):
        x, y, c = _position()
        cps = []
        for i in range(n):
            for j in range(1, 4):
                px, py = _flip(x, j & 2), _flip(y, j & 1)
                sem = (j - 1) * n + i
                cps.append(pltpu.make_async_remote_copy(src_ref=refs[i].at[2 * px + py], dst_ref=refs[n + i].at[j - 1],
                                                        send_sem=send_sems.at[sem], recv_sem=recv_sems.at[sem],
                                                        device_id=(px, py, c), device_id_type=MESH))
        return cps
    return plan


def _heads_major(name, a, after):
    S, W = a.shape
    H = W // HEAD_DIM
    tm = _tile(S, 512, 16)

    def body(a_ref, after_ref, o_ref):
        v = a_ref[...]
        for h in range(H):
            o_ref[h] = v[:, h * HEAD_DIM:(h + 1) * HEAD_DIM]

    return _pc(body, name=name, grid=(S // tm,),
               in_specs=[pl.BlockSpec((tm, W), lambda i: (i, 0)), pl.BlockSpec(memory_space=pl.ANY)],
               out_specs=pl.BlockSpec((H, tm, HEAD_DIM), lambda i: (0, i, 0)),
               out_shape=jax.ShapeDtypeStruct((H, S, HEAD_DIM), a.dtype), compiler_params=_sem("parallel"))(a, after)


def _heads_minor(name, a):
    H, S, _ = a.shape
    tm = _tile(S, 512, 16)

    def body(a_ref, o_ref):
        o_ref[...] = jnp.concatenate([a_ref[h] for h in range(H)], axis=1)

    return _pc(body, name=name, grid=(S // tm,), in_specs=[pl.BlockSpec((H, tm, HEAD_DIM), lambda i: (0, i, 0))],
               out_specs=pl.BlockSpec((tm, H * HEAD_DIM), lambda i: (i, 0)),
               out_shape=jax.ShapeDtypeStruct((S, H * HEAD_DIM), a.dtype), compiler_params=_sem("parallel"))(a)


def kernel(x, mem, attn_norm, attn_w_qkv, attn_q_gain, attn_k_gain, attn_w_o, pool_norm, pool_w, pool_scale, xattn_norm, mem_norm, xattn_w_q, xattn_w_kv, xattn_w_o, ffn_norm, ffn_w_up, ffn_conv_w, ffn_conv_b, ffn_w_down, final_norm, loss_target, m_attn_norm, m_attn_w_qkv, m_attn_q_gain, m_attn_k_gain, m_attn_w_o, m_pool_norm, m_pool_w, m_pool_scale, m_xattn_norm, m_mem_norm, m_xattn_w_q, m_xattn_w_kv, m_xattn_w_o, m_ffn_norm, m_ffn_w_up, m_ffn_conv_w, m_ffn_conv_b, m_ffn_w_down, m_final_norm, v_attn_norm, v_attn_w_qkv, v_attn_q_gain, v_attn_k_gain, v_attn_w_o, v_pool_norm, v_pool_w, v_pool_scale, v_xattn_norm, v_mem_norm, v_xattn_w_q, v_xattn_w_kv, v_xattn_w_o, v_ffn_norm, v_ffn_w_up, v_ffn_conv_w, v_ffn_conv_b, v_ffn_w_down, v_final_norm):
    names = ['attn_norm', 'attn_w_qkv', 'attn_q_gain', 'attn_k_gain', 'attn_w_o', 'pool_norm', 'pool_w', 'pool_scale',
             'xattn_norm', 'mem_norm', 'xattn_w_q', 'xattn_w_kv', 'xattn_w_o', 'ffn_norm', 'ffn_w_up', 'ffn_conv_w',
             'ffn_conv_b', 'ffn_w_down', 'final_norm']
    W = dict(zip(names, (attn_norm, attn_w_qkv, attn_q_gain, attn_k_gain, attn_w_o, pool_norm, pool_w, pool_scale,
                         xattn_norm, mem_norm, xattn_w_q, xattn_w_kv, xattn_w_o, ffn_norm, ffn_w_up, ffn_conv_w,
                         ffn_conv_b, ffn_w_down, final_norm)))
    Mo = dict(zip(names, (m_attn_norm, m_attn_w_qkv, m_attn_q_gain, m_attn_k_gain, m_attn_w_o, m_pool_norm, m_pool_w,
                          m_pool_scale, m_xattn_norm, m_mem_norm, m_xattn_w_q, m_xattn_w_kv, m_xattn_w_o, m_ffn_norm,
                          m_ffn_w_up, m_ffn_conv_w, m_ffn_conv_b, m_ffn_w_down, m_final_norm)))
    Vo = dict(zip(names, (v_attn_norm, v_attn_w_qkv, v_attn_q_gain, v_attn_k_gain, v_attn_w_o, v_pool_norm, v_pool_w,
                          v_pool_scale, v_xattn_norm, v_mem_norm, v_xattn_w_q, v_xattn_w_kv, v_xattn_w_o, v_ffn_norm,
                          v_ffn_w_up, v_ffn_conv_w, v_ffn_conv_b, v_ffn_w_down, v_final_norm)))

    S, D = x.shape[1], x.shape[2]
    n_layers = xattn_norm.shape[0]
    n_up = ffn_w_up.shape[2]
    qkv_w = attn_w_qkv.shape[2] * N_DEV
    n_heads = qkv_w // HEAD_DIM - 2 * N_KV_HEADS
    n_rot = (n_heads + N_KV_HEADS) * HEAD_DIM // LANES
    group_w = pool_w.shape[3]
    xs, mems, tgt = x[0], mem[0], loss_target[0]
    xi, yi, ci = _position()
    dev = 4 * xi + 2 * yi + ci
    pos = jnp.stack([ci, 2 * xi + yi]).astype(jnp.int32)

    layers = range(n_layers)
    n_groups = pool_w.shape[1]
    small_vec = jnp.concatenate([pool_norm.reshape(-1), pool_scale.reshape(-1), ffn_conv_w.reshape(-1)])
    small_rows = _round_up(-(-small_vec.size // PACK_W), 8)
    small_vec = jnp.pad(small_vec, (0, small_rows * PACK_W - small_vec.size)).reshape(small_rows, PACK_W)
    w_qkv, w_o, small, attn_token = _allgather_blocks(
        "allgather_attn", [attn_w_qkv[0].astype(BF16), attn_w_o[0].astype(BF16), small_vec])
    small = small.reshape(N_DEV, -1)
    w_qkv = w_qkv.transpose(1, 0, 2).reshape(D, qkv_w)
    w_o = w_o.reshape(-1, D)
    blocks = [pool_w.reshape(-1, group_w)] + [xattn_w_q[l] for l in layers] + [xattn_w_kv.reshape(n_layers * D, -1)]
    blocks += [xattn_w_o[l] for l in layers] + [ffn_w_up.reshape(n_layers * D, n_up)] + [ffn_w_down[l] for l in layers]
    blocks = [b.astype(BF16) for b in blocks]
    blocks[0] = _after(blocks[0], attn_token)
    n_blk = len(blocks)
    lands = [lax.dynamic_update_index_in_dim(lax.empty((N_DEV,) + b.shape, BF16), b, dev, 0) for b in blocks]
    plan_chips, plan_sibling = _plan_gather_chips(n_blk), _plan_gather_sibling(n_blk)
    gather_sems = _copies_start("gather_chips_start", blocks + lands, (4 * n_blk,), plan_chips)
    attn_norm_late = _after(attn_norm, gather_sems[3])

    d_sh = pool_norm.shape[1]
    pool_norm_f = small[:, :d_sh].reshape(1, D)
    pool_scale_f = small[:, d_sh:2 * d_sh].reshape(1, D)
    conv_w_f = small[:, 2 * d_sh:2 * d_sh + ffn_conv_w.size].reshape(N_DEV, n_layers, 3, n_up)
    conv_b_f = ffn_conv_b.reshape(n_layers, N_DEV, 1, n_up)

    cos, sin = _rope_tables(S)
    bd = _head_mean_matrix()
    pad_w = qkv_w - (n_heads + N_KV_HEADS) * HEAD_DIM
    qk_gain = jnp.concatenate([jnp.tile(attn_q_gain[0], n_heads), jnp.tile(attn_k_gain[0], N_KV_HEADS),
                               jnp.ones((pad_w,), F32)]).reshape(1, qkv_w)
    qk_scale = jnp.concatenate([jnp.full((n_heads * HEAD_DIM,), HEAD_DIM ** -0.5, F32),
                                jnp.ones((qkv_w - n_heads * HEAD_DIM,), F32)]).reshape(1, qkv_w)

    saved = []

    def xattn_ffn_fwd(l, xin, hx=None):
        if hx is None:
            hx = _rmsnorm(f"xattn_norm{l}", xin, xattn_norm[l:l + 1], BF16)
        memn = _rmsnorm(f"mem_norm{l}", mems, mem_norm[l:l + 1], BF16)
        qx = _mm_nn(f"xattn_q{l}", hx, w_xq[l], BF16)
        kv = _mm_nn_bs(f"xattn_kv{l}", memn, w_xkv, BF16, l)
        ox = _xattn_fwd(f"xattn_fwd{l}", qx, kv)
        x2 = _mm_nn(f"xattn_o{l}", ox, w_xo[l], F32, res=xin)
        hf = _rmsnorm(f"ffn_norm{l}", x2, ffn_norm[l:l + 1], BF16)
        cw = conv_w_f[:, l].reshape(2, N_DEV // 2, 3, n_up)
        cb = conv_b_f[l].reshape(2, N_DEV // 2, 1, n_up)
        u, act = _ffn_up_act(f"ffn_up_act{l}", hf, w_up.reshape(2, N_DEV // 2, n_layers * D, n_up), cw, cb, l)
        x3 = _mm_nn_as(f"ffn_down{l}", act, w_down[l], F32, x2)
        saved.append(dict(xin=xin, hx=hx, memn=memn, qx=qx, kv=kv, ox=ox, x2=x2, hf=hf, u=u, cw=cw, cb=cb, act=act))
        return x3

    h0 = _rmsnorm("attn_norm", xs, attn_norm_late, BF16)
    qkv = _mm_nn("attn_qkv", h0, w_qkv, F32)
    qkr = _qk_rope("qk_rope", qkv, qk_gain, qk_scale, cos, sin, bd, n_rot)
    o_hm, lse = _attn_fwd("attn_fwd", qkr, n_heads)
    o_att = _heads_minor("attn_heads_minor", o_hm)
    arrived = _copies_wait("gather_chips_wait", gather_sems[2], gather_sems[0], gather_sems[1], plan_chips, o_att)
    pass_sems = _copies_start("gather_sibling_start", arrived[n_blk:], (3 * n_blk,), plan_sibling)
    x1 = _mm_nn("attn_o", o_att, _after(w_o, pass_sems[3]), F32, res=xs)
    hx0 = _rmsnorm("xattn_norm0", x1, xattn_norm[0:1], BF16)
    gathered = iter(_copies_wait("gather_sibling_wait", pass_sems[2], pass_sems[0], pass_sems[1], plan_sibling, hx0))
    w_pool = (next(gathered).reshape(N_DEV, n_groups, -1, group_w).transpose(1, 0, 2, 3)
              .reshape(n_groups, group_w, group_w))
    w_xq = [next(gathered).reshape(D, D) for l in layers]
    w_xkv = next(gathered)
    w_xo = [next(gathered).reshape(D, D) for l in layers]
    w_up = next(gathered)
    w_down = [next(gathered).reshape(-1, D) for l in layers]
    x3 = xattn_ffn_fwd(0, x1, hx0)
    hp = _rmsnorm("pool_norm", x3, pool_norm_f, F32)
    mixed = _pool_window("pool_window", hp, group_w, False, BF16)
    x4 = _pool_proj("pool_proj", mixed, w_pool, pool_scale_f, x3)
    x6 = xattn_ffn_fwd(1, x4)

    G = {}
    g, d_final, lvec = _loss_head("loss_head", x6, final_norm.reshape(1, D), tgt)
    G['final_norm'] = d_final.reshape(D)
    loss_part = (0.5 * jnp.sum(lvec) / D).reshape(1)

    d_xn, d_mn, d_fn, d_xq, d_xkv, d_xo, d_up, d_cw, d_cb, d_down = ([None] * n_layers for _ in range(10))

    def xattn_ffn_bwd(l, g, conv_b_late=None, after_act=None):
        sv = saved[l]
        d_down[l] = _mm_tn_as(f"ffn_down_dw{l}", sv['act'], g, F32)
        du, st = _ffn_act_bwd(f"ffn_act_bwd{l}", sv['u'], g, w_down[l], sv['cw'],
                              sv['cb'] if conv_b_late is None else conv_b_late)
        du = du.reshape(N_DEV, S, n_up)
        ffn_gain = ffn_norm[l:l + 1] if after_act is None else _after(ffn_norm[l:l + 1], after_act(du))
        st = st.reshape(N_DEV, 8, n_up)
        d_cw[l], d_cb[l] = st[:, 0:3], st[:, 3].reshape(-1)
        d_up[l] = _mm_tn_bs(f"ffn_up_dw{l}", sv['hf'], du, F32)
        g, d_fn[l] = _mm_nt_abs_norm_bwd(f"ffn_up_dx_norm_bwd{l}", du, w_up, sv['x2'], ffn_gain, g, l)
        d_xo[l] = _mm_tn(f"xattn_o_dw{l}", sv['ox'], g, F32)
        do = _mm_nt(f"xattn_o_dx{l}", g, w_xo[l], BF16)
        dq, dkv = _xattn_bwd(f"xattn_bwd{l}", sv['qx'], sv['kv'], do)
        d_xq[l] = _mm_tn(f"xattn_q_dw{l}", sv['hx'], dq, F32)
        d_xkv[l] = _mm_tn_bs(f"xattn_kv_dw{l}", sv['memn'], dkv, F32)
        dmemn = _mm_nt_abs(f"xattn_kv_dx{l}", dkv, w_xkv, D, F32, l)
        _, d_mn[l] = _rmsnorm_bwd(f"mem_norm_bwd{l}", mems, mem_norm[l:l + 1], dmemn)
        g, d_xn[l] = _mm_nt_norm_bwd(f"xattn_q_dx_norm_bwd{l}", dq, w_xq[l], sv['xin'], xattn_norm[l:l + 1], g)
        return g

    def reduce_start(tag, bufs):
        n = len(bufs)
        g4s = [b.reshape((N_DEV // 2, 2) + b.shape[1:]) for b in bufs]
        lands = [lax.empty((N_DEV // 2,) + b.shape[1:], F32) for b in bufs]
        plan = _plan_reduce_sibling(n)
        return (n, plan) + _copies_start(f"reduce_sibling_start_{tag}", g4s + lands, (N_DEV // 2 * n,), plan)

    def reduce_between(tag, state, after):
        n, plan, send_sems, recv_sems, thru, _ = state
        got = _copies_wait(f"reduce_sibling_wait_{tag}", thru, send_sems, recv_sems, plan, after)
        sums = [_add_sibling(f"reduce_add_sibling_{tag}{i}", got[i], got[n + i], pos) for i in range(n)]
        lands = [lax.empty((3,) + tb.shape[1:], BF16) for tb, _ in sums]
        plan = _plan_reduce_chips(n)
        return (n, plan, [own for _, own in sums]) + _copies_start(f"reduce_chips_start_{tag}",
                                                                    [tb for tb, _ in sums] + lands, (3 * n,), plan)

    def reduce_finish(tag, state, after):
        n, plan, owns, send_sems, recv_sems, thru, _ = state
        got = _copies_wait(f"reduce_chips_wait_{tag}", thru, send_sems, recv_sems, plan, after)
        return [_add_chips(f"reduce_add_chips_{tag}{i}", owns[i], got[n + i]) for i in range(n)]

    def layer_bufs(l):
        return [d_xq[l].reshape(N_DEV, -1, D), d_xkv[l], d_xo[l].reshape(N_DEV, -1, D), d_up[l],
                d_down[l].reshape(N_DEV, -1, D)]

    g = xattn_ffn_bwd(1, g)
    d_mixed, d_pool_w, d_pool_scale = _pool_proj_bwd("pool_proj_bwd", g, mixed, w_pool, pool_scale_f)
    dhp = _pool_window("pool_window_bwd", d_mixed, group_w, True, F32)
    g, d_pool_norm = _rmsnorm_bwd("pool_norm_bwd", x3, pool_norm_f, dhp, g)
    upper = reduce_start("upper", [d_pool_w.reshape(n_groups, N_DEV, -1, group_w).transpose(1, 0, 2, 3)
                                   .reshape(N_DEV, -1, group_w)] + layer_bufs(1))
    between = []

    def upper_between(du):
        between.append(reduce_between("upper", upper, du))
        return between[0][-1]

    g = xattn_ffn_bwd(0, g, _after(saved[0]['cb'], upper[-1]), upper_between)
    lower = reduce_start("lower", layer_bufs(0))
    d_wo = _mm_tn("attn_o_dw", o_att, g, F32)
    do = _mm_nt("attn_o_dx", g, _after(w_o, lower[-1]), BF16)
    lower = reduce_between("lower", lower, do)
    do_hm = _heads_major("attn_heads_major", do, lower[-1])
    dq_hm, dk_hm, dv_hm = _attn_bwd("attn_bwd", qkr, o_hm, lse, do_hm)
    red_lower = reduce_finish("lower", lower, dq_hm)
    d_qkv, d_gain = _qk_rope_bwd("qk_rope_bwd", dq_hm, dk_hm, dv_hm, qkv, qk_gain, qk_scale, cos, sin, bd)
    red_upper = reduce_finish("upper", between[0], d_qkv)
    d_wqkv = _mm_tn("attn_qkv_dw", h0, d_qkv, F32)
    last = reduce_start("last", [d_wqkv.reshape(D, N_DEV, -1).transpose(1, 0, 2), d_wo.reshape(N_DEV, -1, D)])
    last = reduce_between("last", last, d_wqkv)
    grad_x, d_attn_norm = _mm_nt_norm_bwd("attn_qkv_dx_norm_bwd", d_qkv, w_qkv, xs, attn_norm, g, last[-1])
    G['pool_w'] = red_upper[0].reshape(pool_w.shape)
    per_layer = [red_lower, red_upper[1:]]
    for i, n in enumerate(['xattn_w_q', 'xattn_w_kv', 'xattn_w_o', 'ffn_w_up', 'ffn_w_down']):
        G[n] = jnp.stack([per_layer[l][i] for l in layers])

    hq = n_heads * HEAD_DIM
    small_g = {'attn_norm': d_attn_norm, 'attn_q_gain': d_gain[0, :hq].reshape(n_heads, HEAD_DIM).sum(0),
               'attn_k_gain': d_gain[0, hq:hq + N_KV_HEADS * HEAD_DIM].reshape(N_KV_HEADS, HEAD_DIM).sum(0),
               'pool_norm': d_pool_norm, 'pool_scale': d_pool_scale,
               'xattn_norm': jnp.concatenate(d_xn), 'mem_norm': jnp.concatenate(d_mn), 'ffn_norm': jnp.concatenate(d_fn),
               'ffn_conv_w': jnp.stack(d_cw, axis=1), 'ffn_conv_b': jnp.stack(d_cb)}
    order = list(small_g)
    flat = jnp.concatenate([loss_part] + [small_g[n].reshape(-1) for n in order] + [G['final_norm']])
    ar_rows = _round_up(-(-flat.size // PACK_W), 8)
    flat = jnp.pad(flat, (0, ar_rows * PACK_W - flat.size)).reshape(ar_rows, PACK_W)
    summed = _sum_slots("allreduce_sum", _allgather_small("allreduce_gather", flat)).reshape(-1)
    loss = summed[0]
    red_last = reduce_finish("last", last, summed)
    G['attn_w_qkv'], G['attn_w_o'] = red_last[0][None], red_last[1][None]
    at = 1
    for n in order + ['final_norm']:
        size = G['final_norm'].size if n == 'final_norm' else small_g[n].size
        piece = summed[at:at + size]
        at += size
        if n in ('pool_norm', 'pool_scale'):
            piece = lax.dynamic_slice(piece, (dev * d_sh,), (d_sh,))
        elif n == 'ffn_conv_w':
            piece = lax.dynamic_index_in_dim(piece.reshape(N_DEV, n_layers, 3, n_up), dev, 0, keepdims=False)
        G[n] = piece.reshape(W[n].shape)

    deltas, new_m, new_v = [], [], []
    for n in names:
        d, nm, nv = _adamw(f"adamw_{n}", W[n], G[n], Mo[n], Vo[n])
        deltas.append(d)
        new_m.append(nm)
        new_v.append(nv)
    return (loss, grad_x[None], *[G[n] for n in names], *deltas, *new_m, *new_v)
```

```python
import jax
import jax.numpy as jnp
from jax import lax
from jax.experimental import pallas as pl
from jax.experimental.pallas import tpu as pltpu

F32 = jnp.float32
BF16 = jnp.bfloat16
MESH = pl.DeviceIdType.MESH

N_DEV = 8
EPS = 1e-6
HEAD_DIM = 64
N_KV_HEADS = 4
X_HEADS = 4
GRID_W = 64
ROPE_THETA = 10000.0
ROPE_PAIRS = HEAD_DIM // 4
POOL_WINDOWS = (2, 4, 8, 16)
POOL_PAD = 16
KEY_CHUNK = 1024
MM_ROWS = 1024
REDUCE_ROWS = 2048
FFN_HALO = 16
LANES = 128
PACK_W = 1024
ADAM_LR, ADAM_B1, ADAM_B2, ADAM_EPS, ADAM_WD, ADAM_STEP = 0.001, 0.9, 0.999, 1e-08, 0.01, 10

_NN = (((1,), (0,)), ((), ()))
_NT = (((1,), (1,)), ((), ()))
_TN = (((0,), (0,)), ((), ()))


def _pc(body, *, name, **kw):
    return pl.pallas_call(body, name=name, **kw)


def _sem(*kinds):
    return pltpu.CompilerParams(dimension_semantics=kinds)


def _tile(n, pref, mult):
    best = None
    for t in range(mult, min(n, pref) + 1, mult):
        if n % t == 0:
            best = t
    return n if best is None else best


def _round_up(n, m):
    return (n + m - 1) // m * m


def _mm_call(name, a, b, dims, grid, a_spec, b_spec, o_spec, out_shape, kaxis, res=None, res_spec=None):
    nk = grid[kaxis]
    acc_shape = tuple(d for d in o_spec.block_shape if d is not None)
    in_place = out_shape.dtype == F32
    use_scratch = nk > 1 and not in_place

    def body(*refs):
        refs = list(refs)
        acc = refs.pop() if use_scratch else None
        a_ref, b_ref = refs[:2]
        r_ref = refs[2] if res is not None else None
        o_ref = refs[-1]
        if len(a_ref.shape) == 3:
            n = a_ref.shape[2]
            prod = sum(jnp.dot(a_ref[s].astype(BF16), b_ref[s * n:(s + 1) * n, :].astype(BF16),
                               preferred_element_type=F32) for s in range(a_ref.shape[0]))
        else:
            prod = lax.dot_general(a_ref[...].astype(BF16), b_ref[...].astype(BF16), dims, preferred_element_type=F32)
        if nk == 1:
            if r_ref is not None:
                prod = prod + r_ref[...]
            o_ref[...] = prod.astype(o_ref.dtype)
            return
        k = pl.program_id(kaxis)
        tgt = o_ref if in_place else acc

        @pl.when(k == 0)
        def _():
            tgt[...] = prod + r_ref[...] if (in_place and r_ref is not None) else prod

        @pl.when(k > 0)
        def _():
            tgt[...] += prod

        if not in_place:
            @pl.when(k == nk - 1)
            def _():
                r = acc[...]
                if r_ref is not None:
                    r = r + r_ref[...]
                o_ref[...] = r.astype(o_ref.dtype)

    sem = tuple("arbitrary" if ax == kaxis else "parallel" for ax in range(len(grid)))
    ins = [a, b] if res is None else [a, b, res]
    specs = [a_spec, b_spec] if res is None else [a_spec, b_spec, res_spec]
    return _pc(body, name=name, grid=grid, in_specs=specs, out_specs=o_spec, out_shape=out_shape,
               scratch_shapes=[pltpu.VMEM(acc_shape, F32)] if use_scratch else [],
               compiler_params=_sem(*sem))(*ins)


def _reduce_rows(a, b):
    return REDUCE_ROWS * (2 if a.dtype == BF16 and b.dtype == BF16 else 1)


def _mm_nn(name, a, b, out_dtype, res=None):
    M, K = a.shape
    N = b.shape[1]
    tm, tn, tk = _tile(M, MM_ROWS, 16), _tile(N, 1024, LANES), _tile(K, 1024, LANES)
    return _mm_call(name, a, b, _NN, (M // tm, N // tn, K // tk),
                    pl.BlockSpec((tm, tk), lambda i, j, k: (i, k)),
                    pl.BlockSpec((tk, tn), lambda i, j, k: (k, j)),
                    pl.BlockSpec((tm, tn), lambda i, j, k: (i, j)),
                    jax.ShapeDtypeStruct((M, N), out_dtype), 2, res,
                    pl.BlockSpec((tm, tn), lambda i, j, k: (i, j)))


def _mm_nt(name, a, b, out_dtype):
    M, K = a.shape
    N = b.shape[0]
    tm, tn, tk = _tile(M, MM_ROWS, 16), _tile(N, 1024, LANES), _tile(K, 1024, LANES)
    return _mm_call(name, a, b, _NT, (M // tm, N // tn, K // tk),
                    pl.BlockSpec((tm, tk), lambda i, j, k: (i, k)),
                    pl.BlockSpec((tn, tk), lambda i, j, k: (j, k)),
                    pl.BlockSpec((tm, tn), lambda i, j, k: (i, j)),
                    jax.ShapeDtypeStruct((M, N), out_dtype), 2)


def _mm_tn(name, a, b, out_dtype):
    R, M = a.shape
    N = b.shape[1]
    tm, tn, tr = _tile(M, 1024, LANES), _tile(N, 1024, LANES), _tile(R, _reduce_rows(a, b), 16)
    return _mm_call(name, a, b, _TN, (M // tm, N // tn, R // tr),
                    pl.BlockSpec((tr, tm), lambda i, j, k: (k, i)),
                    pl.BlockSpec((tr, tn), lambda i, j, k: (k, j)),
                    pl.BlockSpec((tm, tn), lambda i, j, k: (i, j)),
                    jax.ShapeDtypeStruct((M, N), out_dtype), 2)


def _mm_nn_bs(name, a, b, out_dtype, layer=0):
    M, K = a.shape
    J, _, n = b.shape
    tm, tk = _tile(M, MM_ROWS, 16), _tile(K, 1024, LANES)
    first = layer * (K // tk)
    return _mm_call(name, a, b, _NN, (J, M // tm, K // tk),
                    pl.BlockSpec((tm, tk), lambda j, i, k: (i, k)),
                    pl.BlockSpec((None, tk, n), lambda j, i, k: (j, first + k, 0)),
                    pl.BlockSpec((None, tm, n), lambda j, i, k: (j, i, 0)),
                    jax.ShapeDtypeStruct((J, M, n), out_dtype), 2)


def _mm_nn_as(name, a, b, out_dtype, res):
    J, M, n = a.shape
    N = b.shape[1]
    tm, tn = _tile(M, MM_ROWS, 16), _tile(N, 1024, LANES)
    per = J if J <= 4 else (2 if J % 2 == 0 else 1)
    return _mm_call(name, a, b, _NN, (M // tm, N // tn, J // per),
                    pl.BlockSpec((per, tm, n), lambda i, j, k: (k, i, 0)),
                    pl.BlockSpec((per * n, tn), lambda i, j, k: (k, j)),
                    pl.BlockSpec((tm, tn), lambda i, j, k: (i, j)),
                    jax.ShapeDtypeStruct((M, N), out_dtype), 2, res,
                    pl.BlockSpec((tm, tn), lambda i, j, k: (i, j)))


def _mm_tn_as(name, a, b, out_dtype):
    J, R, n = a.shape
    N = b.shape[1]
    tn, tr = _tile(N, 1024, LANES), _tile(R, REDUCE_ROWS, 16)
    return _mm_call(name, a, b, _TN, (J, N // tn, R // tr),
                    pl.BlockSpec((None, tr, n), lambda j, jn, k: (j, k, 0)),
                    pl.BlockSpec((tr, tn), lambda j, jn, k: (k, jn)),
                    pl.BlockSpec((n, tn), lambda j, jn, k: (j, jn)),
                    jax.ShapeDtypeStruct((J * n, N), out_dtype), 2)


def _mm_nt_abs(name, a, b, N, out_dtype, layer=0):
    J, M, n = a.shape
    tm, tn = _tile(M, MM_ROWS, 16), _tile(N, 1024, LANES)
    first = layer * (N // tn)
    return _mm_call(name, a, b, _NT, (M // tm, N // tn, J),
                    pl.BlockSpec((None, tm, n), lambda i, j, k: (k, i, 0)),
                    pl.BlockSpec((None, tn, n), lambda i, j, k: (k, first + j, 0)),
                    pl.BlockSpec((tm, tn), lambda i, j, k: (i, j)),
                    jax.ShapeDtypeStruct((M, N), out_dtype), 2)


def _mm_tn_bs(name, a, b, out_dtype):
    R, M = a.shape
    J, _, n = b.shape
    tm, tr = _tile(M, 1024, LANES), _tile(R, _reduce_rows(a, b), 16)
    return _mm_call(name, a, b, _TN, (J, M // tm, R // tr),
                    pl.BlockSpec((tr, tm), lambda j, i, k: (k, i)),
                    pl.BlockSpec((None, tr, n), lambda j, i, k: (j, k, 0)),
                    pl.BlockSpec((None, tm, n), lambda j, i, k: (j, i, 0)),
                    jax.ShapeDtypeStruct((J, M, n), out_dtype), 2)


def _rmsnorm(name, x, g, out_dtype):
    R, D = x.shape
    tm = _tile(R, 512, 16)

    def body(x_ref, g_ref, o_ref):
        xv = x_ref[...]
        r = lax.rsqrt(jnp.mean(xv * xv, axis=-1, keepdims=True) + EPS)
        o_ref[...] = (xv * r * g_ref[...]).astype(o_ref.dtype)

    return _pc(body, name=name, grid=(R // tm,),
               in_specs=[pl.BlockSpec((tm, D), lambda i: (i, 0)), pl.BlockSpec((1, D), lambda i: (0, 0))],
               out_specs=pl.BlockSpec((tm, D), lambda i: (i, 0)),
               out_shape=jax.ShapeDtypeStruct((R, D), out_dtype), compiler_params=_sem("parallel"))(x, g)


def _rmsnorm_bwd(name, x, g, dh, dres=None):
    R, D = x.shape
    tm = _tile(R, 512, 16)

    def body(*refs):
        if dres is None:
            x_ref, g_ref, dh_ref, dx_ref, dg_ref = refs
            dres_ref = None
        else:
            x_ref, g_ref, dh_ref, dres_ref, dx_ref, dg_ref = refs
        xv = x_ref[...]
        r = lax.rsqrt(jnp.mean(xv * xv, axis=-1, keepdims=True) + EPS)
        xh = xv * r
        dhv = dh_ref[...].astype(F32)

        @pl.when(pl.program_id(0) == 0)
        def _():
            dg_ref[...] = jnp.zeros_like(dg_ref)

        dg_ref[...] += jnp.sum(dhv * xh, axis=0, keepdims=True)
        dxh = dhv * g_ref[...]
        dx = r * (dxh - xh * jnp.mean(dxh * xh, axis=-1, keepdims=True))
        if dres_ref is not None:
            dx = dx + dres_ref[...]
        dx_ref[...] = dx

    row = pl.BlockSpec((tm, D), lambda i: (i, 0))
    vec = pl.BlockSpec((1, D), lambda i: (0, 0))
    ins = [x, g, dh] + ([] if dres is None else [dres])
    specs = [row, vec, row] + ([] if dres is None else [row])
    return _pc(body, name=name, grid=(R // tm,), in_specs=specs, out_specs=(row, vec),
               out_shape=(jax.ShapeDtypeStruct((R, D), F32), jax.ShapeDtypeStruct((1, D), F32)),
               compiler_params=_sem("arbitrary"))(*ins)


def _mm_norm_bwd(name, a, b, grid, a_spec, b_spec, tm, x, gain, dres, after=None):
    M, D = x.shape
    nk = grid[2]

    def body(*refs):
        a_ref, b_ref, x_ref, g_ref, r_ref = refs[:5]
        dx_ref, dg_ref = refs[-3 if nk > 1 else -2:][:2]
        acc = refs[-1] if nk > 1 else None
        i, k = pl.program_id(0), pl.program_id(2)
        if len(a_ref.shape) == 3:
            prod = sum(lax.dot_general(a_ref[s].astype(BF16), b_ref[s].astype(BF16), _NT, preferred_element_type=F32)
                       for s in range(a_ref.shape[0]))
        else:
            prod = lax.dot_general(a_ref[...].astype(BF16), b_ref[...].astype(BF16), _NT, preferred_element_type=F32)

        @pl.when((i == 0) & (k == 0))
        def _():
            dg_ref[...] = jnp.zeros_like(dg_ref)

        def finish(dh):
            xv = x_ref[...]
            r = lax.rsqrt(jnp.mean(xv * xv, axis=-1, keepdims=True) + EPS)
            xh = xv * r
            dg_ref[...] += jnp.sum(dh * xh, axis=0, keepdims=True)
            dxh = dh * g_ref[...]
            dx_ref[...] = r * (dxh - xh * jnp.mean(dxh * xh, axis=-1, keepdims=True)) + r_ref[...]

        if nk == 1:
            finish(prod)
            return

        @pl.when(k == 0)
        def _():
            acc[...] = prod

        @pl.when(k > 0)
        def _():
            acc[...] += prod

        @pl.when(k == nk - 1)
        def _():
            finish(acc[...])

    row = pl.BlockSpec((tm, D), lambda i, j, k: (i, 0))
    vec = pl.BlockSpec((1, D), lambda i, j, k: (0, 0))
    ins = [a, b, x, gain, dres] + ([] if after is None else [after])
    specs = [a_spec, b_spec, row, vec, row] + ([] if after is None else [pl.BlockSpec(memory_space=pl.ANY)])
    return _pc(body, name=name, grid=grid, in_specs=specs, out_specs=(row, vec),
               out_shape=(jax.ShapeDtypeStruct((M, D), F32), jax.ShapeDtypeStruct((1, D), F32)),
               scratch_shapes=[pltpu.VMEM((tm, D), F32)] if nk > 1 else [],
               compiler_params=_sem("arbitrary", "arbitrary", "arbitrary"))(*ins)


def _mm_nt_norm_bwd(name, a, b, x, gain, dres, after=None):
    M, K = a.shape
    D = b.shape[0]
    tm, tk = _tile(M, MM_ROWS, 16), _tile(K, 1024, LANES)
    return _mm_norm_bwd(name, a, b, (M // tm, 1, K // tk),
                        pl.BlockSpec((tm, tk), lambda i, j, k: (i, k)),
                        pl.BlockSpec((D, tk), lambda i, j, k: (0, k)), tm, x, gain, dres, after)


def _mm_nt_abs_norm_bwd(name, a, b, x, gain, dres, layer=0):
    J, M, n = a.shape
    D = x.shape[1]
    tm = _tile(M, MM_ROWS, 16)
    per = 2 if J % 2 == 0 else 1
    return _mm_norm_bwd(name, a, b, (M // tm, 1, J // per),
                        pl.BlockSpec((per, tm, n), lambda i, j, k: (k, i, 0)),
                        pl.BlockSpec((per, D, n), lambda i, j, k: (k, layer, 0)), tm, x, gain, dres)


def _loss_head(name, x, g, tgt):
    R, D = x.shape
    tm = _tile(R, 512, 16)

    def body(x_ref, g_ref, t_ref, dx_ref, dg_ref, l_ref):
        xv = x_ref[...]
        r = lax.rsqrt(jnp.mean(xv * xv, axis=-1, keepdims=True) + EPS)
        xh = xv * r
        err = xh * g_ref[...] - t_ref[...]

        @pl.when(pl.program_id(0) == 0)
        def _():
            dg_ref[...] = jnp.zeros_like(dg_ref)
            l_ref[...] = jnp.zeros_like(l_ref)

        l_ref[...] += jnp.sum(err * err, axis=0, keepdims=True)
        dy = err * (1.0 / D)
        dg_ref[...] += jnp.sum(dy * xh, axis=0, keepdims=True)
        dxh = dy * g_ref[...]
        dx_ref[...] = r * (dxh - xh * jnp.mean(dxh * xh, axis=-1, keepdims=True))

    row = pl.BlockSpec((tm, D), lambda i: (i, 0))
    vec = pl.BlockSpec((1, D), lambda i: (0, 0))
    return _pc(body, name=name, grid=(R // tm,), in_specs=[row, vec, row], out_specs=(row, vec, vec),
               out_shape=(jax.ShapeDtypeStruct((R, D), F32), jax.ShapeDtypeStruct((1, D), F32),
                          jax.ShapeDtypeStruct((1, D), F32)),
               compiler_params=_sem("arbitrary"))(x, g, tgt)


def _rope_tables(S):
    n_rows = S // GRID_W
    row = jnp.repeat(jnp.arange(n_rows, dtype=F32), GRID_W)
    col = jnp.tile(jnp.arange(GRID_W, dtype=F32), n_rows)
    inv_freq = ROPE_THETA ** (-jnp.arange(ROPE_PAIRS, dtype=F32) / ROPE_PAIRS)
    ang = jnp.stack([row[:, None] * inv_freq, col[:, None] * inv_freq], axis=1)
    cos, sin = jnp.cos(ang), jnp.sin(ang)
    c = jnp.broadcast_to(cos[:, :, None, :], (S, 2, 2, ROPE_PAIRS)).reshape(S, HEAD_DIM)
    s = jnp.stack([-sin, sin], axis=2).reshape(S, HEAD_DIM)
    reps = LANES // HEAD_DIM
    return jnp.tile(c, (1, reps)), jnp.tile(s, (1, reps))


def _head_mean_matrix():
    h = jnp.arange(LANES) // HEAD_DIM
    m = jnp.where(h[:, None] == h[None, :], 1.0 / HEAD_DIM, 0.0).astype(BF16)
    return jnp.concatenate([m, m], axis=0)


def _head_mean(v, bd):
    hi = v.astype(BF16)
    lo = (v - hi.astype(F32)).astype(BF16)
    return jnp.dot(jnp.concatenate([hi, lo], axis=1), bd, preferred_element_type=F32)


def _swap_halves(y):
    lane = lax.broadcasted_iota(jnp.int32, y.shape, 1)
    return jnp.where(lane % 32 < 16, pltpu.roll(y, LANES - 16, 1), pltpu.roll(y, 16, 1))


def _qk_rope(name, qkv, gain, scale, cos, sin, bd, n_rot):
    S, W = qkv.shape
    tm = _tile(S, 2048, 16)
    per = LANES // HEAD_DIM

    def body(x_ref, g_ref, s_ref, c_ref, sn_ref, bd_ref, o_ref):
        j = pl.program_id(1)
        xv = x_ref[...]

        def put(v):
            for h in range(per):
                o_ref[h] = v[:, h * HEAD_DIM:(h + 1) * HEAD_DIM].astype(BF16)

        @pl.when(j < n_rot)
        def _():
            ms = _head_mean(xv * xv, bd_ref[...])
            y = xv * lax.rsqrt(ms + EPS) * g_ref[...] * s_ref[...]
            put(y * c_ref[...] + _swap_halves(y) * sn_ref[...])

        @pl.when(j >= n_rot)
        def _():
            put(xv)

    blk = pl.BlockSpec((tm, LANES), lambda i, j: (i, j))
    vec = pl.BlockSpec((1, LANES), lambda i, j: (0, j))
    tab = pl.BlockSpec((tm, LANES), lambda i, j: (i, 0))
    return _pc(body, name=name, grid=(S // tm, W // LANES),
               in_specs=[blk, vec, vec, tab, tab, pl.BlockSpec((2 * LANES, LANES), lambda i, j: (0, 0))],
               out_specs=pl.BlockSpec((per, tm, HEAD_DIM), lambda i, j: (j, i, 0)),
               out_shape=jax.ShapeDtypeStruct((W // HEAD_DIM, S, HEAD_DIM), BF16),
               compiler_params=_sem("parallel", "parallel"))(qkv, gain, scale, cos, sin, bd)


def _qk_rope_bwd(name, dq, dk, dv, qkv, gain, scale, cos, sin, bd):
    S, W = qkv.shape
    tm = _tile(S, 2048, 16)
    per = LANES // HEAD_DIM
    nq, nk, nv = dq.shape[0] // per, dk.shape[0] // per, dv.shape[0] // per
    n_rot = nq + nk

    def body(dq_ref, dk_ref, dv_ref, x_ref, g_ref, s_ref, c_ref, sn_ref, bd_ref, dx_ref, dg_ref):
        j, i = pl.program_id(0), pl.program_id(1)

        @pl.when(i == 0)
        def _():
            dg_ref[...] = jnp.zeros_like(dg_ref)

        def rotate_back(d_ref):
            dv = jnp.concatenate([d_ref[h] for h in range(per)], axis=1)
            xv = x_ref[...]
            ms = _head_mean(xv * xv, bd_ref[...])
            r = lax.rsqrt(ms + EPS)
            z = xv * r
            dy = (dv * c_ref[...] - _swap_halves(dv) * sn_ref[...]) * s_ref[...]
            dg_ref[...] += jnp.sum(dy * z, axis=0, keepdims=True)
            dz = dy * g_ref[...]
            mz = _head_mean(dz * z, bd_ref[...])
            dx_ref[...] = (r * (dz - z * mz)).astype(BF16)

        @pl.when(j < nq)
        def _():
            rotate_back(dq_ref)

        @pl.when((j >= nq) & (j < n_rot))
        def _():
            rotate_back(dk_ref)

        @pl.when(j >= n_rot)
        def _():
            dx_ref[...] = jnp.concatenate([dv_ref[h] for h in range(per)], axis=1).astype(BF16)

    def part(first, count):
        return pl.BlockSpec((per, tm, HEAD_DIM), lambda j, i: (jnp.clip(j - first, 0, count - 1), i, 0))

    blk = pl.BlockSpec((tm, LANES), lambda j, i: (i, j))
    vec = pl.BlockSpec((1, LANES), lambda j, i: (0, j))
    tab = pl.BlockSpec((tm, LANES), lambda j, i: (i, 0))
    return _pc(body, name=name, grid=(W // LANES, S // tm),
               in_specs=[part(0, nq), part(nq, nk), part(n_rot, nv), blk, vec, vec, tab, tab,
                         pl.BlockSpec((2 * LANES, LANES), lambda j, i: (0, 0))],
               out_specs=(blk, vec),
               out_shape=(jax.ShapeDtypeStruct((S, W), BF16), jax.ShapeDtypeStruct((1, W), F32)),
               compiler_params=_sem("parallel", "arbitrary"))(dq, dk, dv, qkv, gain, scale, cos, sin, bd)


def _softmax_rows(s):
    m = jnp.max(s, axis=-1, keepdims=True)
    p = jnp.exp(s - m)
    return p, jnp.sum(p, axis=-1, keepdims=True)


def _attn_fwd(name, qkv, H):
    _, S, dh = qkv.shape
    G = H // N_KV_HEADS
    tq = _tile(S, 256, 16)
    kc = _tile(S, KEY_CHUNK, LANES)
    R = G * tq

    def body(q_ref, k_ref, v_ref, o_ref, lse_ref):
        q = q_ref[...].reshape(R, dh)
        m = jnp.full((R, 1), -1e30, F32)
        l = jnp.zeros((R, 1), F32)
        acc = jnp.zeros((R, dh), F32)
        for c in range(S // kc):
            rows = slice(c * kc, (c + 1) * kc)
            s = lax.dot_general(q, k_ref[rows, :], _NT, preferred_element_type=F32)
            m_new = jnp.maximum(m, jnp.max(s, axis=-1, keepdims=True))
            alpha = jnp.exp(m - m_new)
            p = jnp.exp(s - m_new)
            l = alpha * l + jnp.sum(p, axis=-1, keepdims=True)
            acc = alpha * acc + jnp.dot(p.astype(BF16), v_ref[rows, :], preferred_element_type=F32)
            m = m_new
        o_ref[...] = (acc / l).astype(BF16).reshape(G, tq, dh)
        lse_ref[...] = (m + jnp.log(l)).reshape(G, tq, 1)

    qs = pl.BlockSpec((G, tq, dh), lambda kv, i: (kv, i, 0))
    ls = pl.BlockSpec((G, tq, 1), lambda kv, i: (kv, i, 0))
    ks = pl.BlockSpec((None, S, dh), lambda kv, i: (H + kv, 0, 0))
    vs = pl.BlockSpec((None, S, dh), lambda kv, i: (H + N_KV_HEADS + kv, 0, 0))
    return _pc(body, name=name, grid=(N_KV_HEADS, S // tq), in_specs=[qs, ks, vs], out_specs=(qs, ls),
               out_shape=(jax.ShapeDtypeStruct((H, S, dh), BF16), jax.ShapeDtypeStruct((H, S, 1), F32)),
               compiler_params=_sem("parallel", "parallel"))(qkv, qkv, qkv)


def _attn_bwd(name, qkv, o, lse, do):
    H, S, dh = o.shape
    G = H // N_KV_HEADS
    tq = _tile(S, 128, 16)
    kc = _tile(S, KEY_CHUNK, LANES)
    R = G * tq

    def body(q_ref, k_ref, v_ref, o_ref, lse_ref, do_ref, dq_ref, dk_ref, dv_ref):
        @pl.when(pl.program_id(1) == 0)
        def _():
            dk_ref[...] = jnp.zeros_like(dk_ref)
            dv_ref[...] = jnp.zeros_like(dv_ref)

        qq, dd = q_ref[...].reshape(R, dh), do_ref[...].reshape(R, dh)
        delta = jnp.sum(dd.astype(F32) * o_ref[...].reshape(R, dh).astype(F32), axis=-1, keepdims=True)
        lse = lse_ref[...].reshape(R, 1)
        dq = jnp.zeros((R, dh), F32)
        for c in range(S // kc):
            rows = slice(c * kc, (c + 1) * kc)
            kk, vv = k_ref[rows, :], v_ref[rows, :]
            p = jnp.exp(lax.dot_general(qq, kk, _NT, preferred_element_type=F32) - lse)
            dv_ref[rows, :] += lax.dot_general(p.astype(BF16), dd, _TN, preferred_element_type=F32)
            dp = lax.dot_general(dd, vv, _NT, preferred_element_type=F32)
            ds = (p * (dp - delta)).astype(BF16)
            dq = dq + jnp.dot(ds, kk, preferred_element_type=F32)
            dk_ref[rows, :] += lax.dot_general(ds, qq, _TN, preferred_element_type=F32)
        dq_ref[...] = dq.reshape(G, tq, dh)

    qs = pl.BlockSpec((G, tq, dh), lambda kv, i: (kv, i, 0))
    ls = pl.BlockSpec((G, tq, 1), lambda kv, i: (kv, i, 0))
    ks = pl.BlockSpec((None, S, dh), lambda kv, i: (H + kv, 0, 0))
    vs = pl.BlockSpec((None, S, dh), lambda kv, i: (H + N_KV_HEADS + kv, 0, 0))
    acc = pl.BlockSpec((None, S, dh), lambda kv, i: (kv, 0, 0))
    return _pc(body, name=name, grid=(N_KV_HEADS, S // tq), in_specs=[qs, ks, vs, qs, ls, qs],
               out_specs=(qs, acc, acc),
               out_shape=(jax.ShapeDtypeStruct((H, S, dh), F32), jax.ShapeDtypeStruct((N_KV_HEADS, S, dh), F32),
                          jax.ShapeDtypeStruct((N_KV_HEADS, S, dh), F32)),
               compiler_params=_sem("parallel", "arbitrary"))(qkv, qkv, qkv, o, lse, do)


def _xattn_fwd(name, q, kv):
    S, D = q.shape
    _, M, dh = kv.shape
    scale = dh ** -0.5
    tq = _tile(S, 512, 16)

    def body(q_ref, kv_ref, o_ref):
        for h in range(X_HEADS):
            lo, hi = h * dh, (h + 1) * dh
            s = lax.dot_general(q_ref[:, lo:hi], kv_ref[h], _NT, preferred_element_type=F32) * scale
            p, l = _softmax_rows(s)
            o = jnp.dot(p.astype(BF16), kv_ref[X_HEADS + h], preferred_element_type=F32)
            o_ref[:, lo:hi] = (o / l).astype(BF16)

    row = pl.BlockSpec((tq, D), lambda i: (i, 0))
    return _pc(body, name=name, grid=(S // tq,),
               in_specs=[row, pl.BlockSpec((2 * X_HEADS, M, dh), lambda i: (0, 0, 0))],
               out_specs=row, out_shape=jax.ShapeDtypeStruct((S, D), BF16),
               compiler_params=_sem("parallel"))(q, kv)


def _xattn_bwd(name, q, kv, do):
    S, D = q.shape
    _, M, dh = kv.shape
    scale = dh ** -0.5
    tq = _tile(S, 512, 16)

    def body(q_ref, kv_ref, do_ref, dq_ref, dkv_ref):
        @pl.when(pl.program_id(0) == 0)
        def _():
            dkv_ref[...] = jnp.zeros_like(dkv_ref)

        for h in range(X_HEADS):
            lo, hi = h * dh, (h + 1) * dh
            qh, kh, vh, doh = q_ref[:, lo:hi], kv_ref[h], kv_ref[X_HEADS + h], do_ref[:, lo:hi]
            s = lax.dot_general(qh, kh, _NT, preferred_element_type=F32) * scale
            p, l = _softmax_rows(s)
            pn = p / l
            dkv_ref[X_HEADS + h] += lax.dot_general(pn.astype(BF16), doh, _TN, preferred_element_type=F32)
            dp = lax.dot_general(doh, vh, _NT, preferred_element_type=F32)
            ds = (pn * (dp - jnp.sum(pn * dp, axis=-1, keepdims=True)) * scale).astype(BF16)
            dq_ref[:, lo:hi] = jnp.dot(ds, kh, preferred_element_type=F32).astype(BF16)
            dkv_ref[h] += lax.dot_general(ds, qh, _TN, preferred_element_type=F32)

    row = pl.BlockSpec((tq, D), lambda i: (i, 0))
    full = pl.BlockSpec((2 * X_HEADS, M, dh), lambda i: (0, 0, 0))
    return _pc(body, name=name, grid=(S // tq,), in_specs=[row, full, row], out_specs=(row, full),
               out_shape=(jax.ShapeDtypeStruct((S, D), BF16), jax.ShapeDtypeStruct((2 * X_HEADS, M, dh), F32)),
               compiler_params=_sem("arbitrary"))(q, kv, do)


def _sigmoid(x):
    return 1.0 / (1.0 + jnp.exp(-x))


def _halo_specs(tm, n, S):
    nb = tm // 8
    last8 = S // 8 - 1
    main = pl.BlockSpec((2, None, tm, n), lambda j, i: (0, j, i, 0))
    prev = pl.BlockSpec((2, None, 8, n), lambda j, i: (0, j, jnp.maximum(i * nb - 1, 0), 0))
    nxt = pl.BlockSpec((2, None, 8, n), lambda j, i: (0, j, jnp.minimum((i + 1) * nb, last8), 0))
    return main, prev, nxt


def _ffn_up_act(name, h, w, cw, cb, layer):
    S, K = h.shape
    _, J, _, n = w.shape
    tm = _tile(S, 512, FFN_HALO)
    nblk = S // tm
    hb, last = tm // FFN_HALO, S // FFN_HALO - 1
    te = tm + 2 * FFN_HALO

    def body(h_ref, hp_ref, hn_ref, w_ref, cw_ref, b_ref, u_ref, a_ref):
        i = pl.program_id(1)
        zero = jnp.zeros((FFN_HALO, K), BF16)
        he = jnp.concatenate([jnp.where(i == 0, zero, hp_ref[...]), h_ref[...],
                              jnp.where(i == nblk - 1, zero, hn_ref[...])], axis=0)
        mid = slice(FFN_HALO, tm + FFN_HALO)
        c = []
        for half in range(2):
            ue = jnp.dot(he, w_ref[half], preferred_element_type=F32)
            um = ue[mid]
            u_ref[half] = um
            k = cw_ref[half]
            c.append(pltpu.roll(ue, 1, 0)[mid] * k[0:1] + um * k[1:2] + pltpu.roll(ue, te - 1, 0)[mid] * k[2:3]
                     + b_ref[half])
        a_ref[...] = (c[0] * _sigmoid(c[0]) * c[1]).astype(BF16)

    return _pc(body, name=name, grid=(J, nblk),
               in_specs=[pl.BlockSpec((tm, K), lambda j, i: (i, 0)),
                         pl.BlockSpec((FFN_HALO, K), lambda j, i: (jnp.maximum(i * hb - 1, 0), 0)),
                         pl.BlockSpec((FFN_HALO, K), lambda j, i: (jnp.minimum((i + 1) * hb, last), 0)),
                         pl.BlockSpec((2, None, K, n), lambda j, i: (0, j, layer, 0)),
                         pl.BlockSpec((2, None, 3, n), lambda j, i: (0, j, 0, 0)),
                         pl.BlockSpec((2, None, 1, n), lambda j, i: (0, j, 0, 0))],
               out_specs=(pl.BlockSpec((2, None, tm, n), lambda j, i: (0, j, i, 0)),
                          pl.BlockSpec((None, tm, n), lambda j, i: (j, i, 0))),
               out_shape=(jax.ShapeDtypeStruct((2, J, S, n), F32), jax.ShapeDtypeStruct((J, S, n), BF16)),
               compiler_params=_sem("parallel", "parallel"))(h, h, h, w, cw, cb)


def _ffn_act_bwd(name, u, g, w_down, cw, cb):
    _, J, S, n = u.shape
    D = g.shape[1]
    tm = _tile(S, 512, 16)
    nblk = S // tm
    te = tm + 16
    nb = tm // 8
    last8 = S // 8 - 1

    def body(u_ref, up_ref, un_ref, g_ref, gp_ref, gn_ref, wd_ref, w_ref, b_ref, du_ref, st_ref):
        i = pl.program_id(1)

        @pl.when(i == 0)
        def _():
            st_ref[...] = jnp.zeros_like(st_ref)

        def extended(before, main, after):
            return jnp.concatenate([jnp.where(i == 0, 0.0, before), main, jnp.where(i == nblk - 1, 0.0, after)], axis=0)

        mid = slice(8, tm + 8)
        da_e = lax.dot_general(extended(gp_ref[...], g_ref[...], gn_ref[...]).astype(BF16), wd_ref[...], _NT,
                               preferred_element_type=F32)
        ue, c = [], []
        for half in range(2):
            e = extended(up_ref[half], u_ref[half], un_ref[half])
            w = w_ref[half]
            ue.append((pltpu.roll(e, 1, 0), e, pltpu.roll(e, te - 1, 0)))
            c.append(ue[half][0] * w[0:1] + e * w[1:2] + ue[half][2] * w[2:3] + b_ref[half])
        sg = _sigmoid(c[0])
        dc = [da_e * c[1] * (sg * (1.0 + c[0] * (1.0 - sg))), da_e * (c[0] * sg)]
        r8 = lax.broadcasted_iota(jnp.int32, (8, n), 0)
        for half in range(2):
            w, d, (e_before, e, e_after) = w_ref[half], dc[half], ue[half]
            dm = d[mid]
            du = pltpu.roll(d, te - 1, 0)[mid] * w[0:1] + dm * w[1:2] + pltpu.roll(d, 1, 0)[mid] * w[2:3]
            du_ref[half] = du.astype(BF16)
            s0 = jnp.sum(dm * e_before[mid], axis=0, keepdims=True)
            s1 = jnp.sum(dm * e[mid], axis=0, keepdims=True)
            s2 = jnp.sum(dm * e_after[mid], axis=0, keepdims=True)
            s3 = jnp.sum(dm, axis=0, keepdims=True)
            st_ref[half] += jnp.where(r8 == 0, s0, jnp.where(r8 == 1, s1, jnp.where(r8 == 2, s2,
                                      jnp.where(r8 == 3, s3, 0.0))))

    main, prev, nxt = _halo_specs(tm, n, S)
    gmain = pl.BlockSpec((tm, D), lambda j, i: (i, 0))
    gprev = pl.BlockSpec((8, D), lambda j, i: (jnp.maximum(i * nb - 1, 0), 0))
    gnxt = pl.BlockSpec((8, D), lambda j, i: (jnp.minimum((i + 1) * nb, last8), 0))
    return _pc(body, name=name, grid=(J, nblk),
               in_specs=[main, prev, nxt, gmain, gprev, gnxt, pl.BlockSpec((n, D), lambda j, i: (j, 0)),
                         pl.BlockSpec((2, None, 3, n), lambda j, i: (0, j, 0, 0)),
                         pl.BlockSpec((2, None, 1, n), lambda j, i: (0, j, 0, 0))],
               out_specs=(main, pl.BlockSpec((2, None, 8, n), lambda j, i: (0, j, 0, 0))),
               out_shape=(jax.ShapeDtypeStruct((2, J, S, n), BF16), jax.ShapeDtypeStruct((2, J, 8, n), F32)),
               compiler_params=_sem("parallel", "arbitrary"))(u, u, u, g, g, g, w_down, cw, cb)


def _window_count(t, w, S):
    lo = jnp.maximum(t - w // 2, 0)
    hi = jnp.minimum(t + w - w // 2, S)
    return (hi - lo).astype(F32)


def _trailing_sums(x, w):
    k = 1
    while k < w:
        x = x + pltpu.roll(x, k, 0)
        k *= 2
    return x


def _pool_window(name, h, group_w, adjoint, out_dtype):
    S, D = h.shape
    SP = S + 2 * POOL_PAD
    per_group = group_w // LANES

    def body(h_ref, o_ref, xp):
        g = pl.program_id(0) // per_group
        t = lax.broadcasted_iota(jnp.int32, (S, LANES), 0)
        xp[0:POOL_PAD, :] = jnp.zeros((POOL_PAD, LANES), F32)
        xp[S + POOL_PAD:SP, :] = jnp.zeros((POOL_PAD, LANES), F32)
        for gi, w in enumerate(POOL_WINDOWS):
            @pl.when(g == gi)
            def _():
                hv = h_ref[...]
                cnt = _window_count(t, w, S)
                xp[POOL_PAD:S + POOL_PAD, :] = hv / cnt if adjoint else hv
                ahead = w // 2 if adjoint else w // 2 - 1
                sw = _trailing_sums(xp[...], w)
                if ahead:
                    sw = pltpu.roll(sw, SP - ahead, 0)
                win = sw[POOL_PAD:S + POOL_PAD]
                o_ref[...] = ((win if adjoint else win / cnt) - hv).astype(out_dtype)

    col = pl.BlockSpec((S, LANES), lambda j: (0, j))
    return _pc(body, name=name, grid=(D // LANES,), in_specs=[col], out_specs=col,
               out_shape=jax.ShapeDtypeStruct((S, D), out_dtype),
               scratch_shapes=[pltpu.VMEM((SP, LANES), F32)], compiler_params=_sem("parallel"))(h)


def _pool_proj(name, mixed, w, scale, res):
    S, D = mixed.shape
    G, gw, _ = w.shape
    tm = _tile(S, 512, 16)

    def body(m_ref, w_ref, s_ref, r_ref, o_ref):
        for g in range(G):
            lo, hi = g * gw, (g + 1) * gw
            y = jnp.dot(m_ref[:, lo:hi], w_ref[g], preferred_element_type=F32)
            o_ref[:, lo:hi] = r_ref[:, lo:hi] + y * s_ref[:, lo:hi]

    row = pl.BlockSpec((tm, D), lambda i: (i, 0))
    return _pc(body, name=name, grid=(S // tm,),
               in_specs=[row, pl.BlockSpec((G, gw, gw), lambda i: (0, 0, 0)), pl.BlockSpec((1, D), lambda i: (0, 0)), row],
               out_specs=row, out_shape=jax.ShapeDtypeStruct((S, D), F32),
               compiler_params=_sem("parallel"))(mixed, w, scale, res)


def _pool_proj_bwd(name, dy, mixed, w, scale):
    S, D = mixed.shape
    G, gw, _ = w.shape
    tm = _tile(S, 512, 16)

    def body(dy_ref, m_ref, w_ref, s_ref, dm_ref, dw_ref, ds_ref):
        @pl.when(pl.program_id(0) == 0)
        def _():
            dw_ref[...] = jnp.zeros_like(dw_ref)
            ds_ref[...] = jnp.zeros_like(ds_ref)

        for g in range(G):
            lo, hi = g * gw, (g + 1) * gw
            mg, dyg = m_ref[:, lo:hi], dy_ref[:, lo:hi]
            y = jnp.dot(mg, w_ref[g], preferred_element_type=F32)
            ds_ref[:, lo:hi] += jnp.sum(dyg * y, axis=0, keepdims=True)
            dyp = (dyg * s_ref[:, lo:hi]).astype(BF16)
            dm_ref[:, lo:hi] = lax.dot_general(dyp, w_ref[g], _NT, preferred_element_type=F32)
            dw_ref[g] += lax.dot_general(mg, dyp, _TN, preferred_element_type=F32)

    row = pl.BlockSpec((tm, D), lambda i: (i, 0))
    wsp = pl.BlockSpec((G, gw, gw), lambda i: (0, 0, 0))
    vec = pl.BlockSpec((1, D), lambda i: (0, 0))
    return _pc(body, name=name, grid=(S // tm,), in_specs=[row, row, wsp, vec], out_specs=(row, wsp, vec),
               out_shape=(jax.ShapeDtypeStruct((S, D), F32), jax.ShapeDtypeStruct((G, gw, gw), F32),
                          jax.ShapeDtypeStruct((1, D), F32)),
               compiler_params=_sem("arbitrary"))(dy, mixed, w, scale)


def _adamw(name, w, g, m, v):
    shape = w.shape
    C = shape[-1]
    R = w.size // C
    tm = _tile(R, 512, 8)

    def body(w_ref, g_ref, m_ref, v_ref, d_ref, nm_ref, nv_ref):
        gv = g_ref[...]
        nm = ADAM_B1 * m_ref[...] + (1.0 - ADAM_B1) * gv
        nv = ADAM_B2 * v_ref[...] + (1.0 - ADAM_B2) * (gv * gv)
        m_hat = nm / (1.0 - ADAM_B1 ** ADAM_STEP)
        v_hat = nv / (1.0 - ADAM_B2 ** ADAM_STEP)
        d_ref[...] = -ADAM_LR * (m_hat / (jnp.sqrt(v_hat) + ADAM_EPS) + ADAM_WD * w_ref[...])
        nm_ref[...] = nm
        nv_ref[...] = nv

    blk = pl.BlockSpec((tm, C), lambda i: (i, 0))
    sd = jax.ShapeDtypeStruct((R, C), F32)
    outs = _pc(body, name=name, grid=(R // tm,), in_specs=[blk] * 4, out_specs=(blk,) * 3, out_shape=(sd,) * 3,
               compiler_params=_sem("parallel"))(*(a.reshape(R, C) for a in (w, g, m, v)))
    return tuple(o.reshape(shape) for o in outs)


def _position():
    return lax.axis_index("x"), lax.axis_index("y"), lax.axis_index("c")


def _flip(v, bit):
    return 1 - v if bit else v


def _allgather_small(name, v):
    R, W = v.shape

    def body(v_ref, out_ref, send_sems, recv_sems):
        x, y, c = _position()
        me = 4 * x + 2 * y + c
        out_ref[me] = v_ref[...]
        sends = []
        for k in range(1, N_DEV):
            peer = (_flip(x, k & 4), _flip(y, k & 2), _flip(c, k & 1))
            cp = pltpu.make_async_remote_copy(src_ref=v_ref, dst_ref=out_ref.at[me], send_sem=send_sems.at[k - 1],
                                              recv_sem=recv_sems.at[k - 1], device_id=peer, device_id_type=MESH)
            cp.start()
            sends.append(cp)
        for k in range(1, N_DEV):
            peer = (_flip(x, k & 4), _flip(y, k & 2), _flip(c, k & 1))
            slot = 4 * peer[0] + 2 * peer[1] + peer[2]
            pltpu.make_async_remote_copy(src_ref=v_ref, dst_ref=out_ref.at[slot], send_sem=send_sems.at[k - 1],
                                         recv_sem=recv_sems.at[k - 1], device_id=peer, device_id_type=MESH).wait_recv()
        for cp in sends:
            cp.wait_send()

    vm = pl.BlockSpec(memory_space=pltpu.VMEM)
    return _pc(body, name=name, in_specs=[vm], out_specs=vm, out_shape=jax.ShapeDtypeStruct((N_DEV, R, W), F32),
               scratch_shapes=[pltpu.SemaphoreType.DMA((N_DEV - 1,)), pltpu.SemaphoreType.DMA((N_DEV - 1,))])(v)


def _sum_slots(name, a):
    n, R, W = a.shape

    def body(a_ref, o_ref):
        acc = a_ref[0]
        for s in range(1, n):
            acc = acc + a_ref[s]
        o_ref[...] = acc

    return _pc(body, name=name, grid=(1,), in_specs=[pl.BlockSpec((n, R, W), lambda i: (0, 0, 0))],
               out_specs=pl.BlockSpec((R, W), lambda i: (0, 0)), out_shape=jax.ShapeDtypeStruct((R, W), F32))(a)


def _allgather_blocks(name, blocks):
    n = len(blocks)

    def body(*refs):
        b_refs, out_refs, token = refs[:n], refs[n:2 * n], refs[2 * n]
        send_sems, recv_sems, local_sems = refs[2 * n + 1:]
        token[...] = jnp.zeros_like(token)
        x, y, c = _position()
        me, sibling = (x, y, c), (x, y, 1 - c)
        chips = [(1 - x, y), (x, 1 - y), (1 - x, 1 - y)]

        def slot(i, px, py, pc):
            return out_refs[i].at[4 * px + 2 * py + pc]

        def copy(i, k, block, to, src=None):
            return pltpu.make_async_remote_copy(src_ref=slot(i, *block) if src is None else src, dst_ref=slot(i, *block),
                                                send_sem=send_sems.at[k, i], recv_sem=recv_sems.at[k, i],
                                                device_id=to, device_id_type=MESH)

        mine = [pltpu.make_async_copy(b_refs[i], slot(i, *me), local_sems.at[i]) for i in range(n)]
        first = [copy(i, 1 + j, me, (*chip, c), src=b_refs[i]) for i in range(n) for j, chip in enumerate(chips)]
        first += [copy(i, 0, me, sibling, src=b_refs[i]) for i in range(n)]
        for cp in mine + first:
            cp.start()
        passed = []
        for j, chip in enumerate(chips):
            for i in range(n):
                copy(i, 1 + j, (*chip, c), me).wait_recv()
                passed.append(copy(i, 4 + j, (*chip, c), sibling))
                passed[-1].start()
        for i in range(n):
            copy(i, 0, sibling, me).wait_recv()
        for j, chip in enumerate(chips):
            for i in range(n):
                copy(i, 4 + j, (*chip, 1 - c), me).wait_recv()
        for cp in first + passed:
            cp.wait_send()
        for cp in mine:
            cp.wait()

    hbm = pl.BlockSpec(memory_space=pl.ANY)
    return _pc(body, name=name, in_specs=[hbm] * n, out_specs=[hbm] * n + [pl.BlockSpec(memory_space=pltpu.VMEM)],
               out_shape=[jax.ShapeDtypeStruct((N_DEV,) + b.shape, b.dtype) for b in blocks]
               + [jax.ShapeDtypeStruct((8, LANES), F32)],
               scratch_shapes=[pltpu.SemaphoreType.DMA((7, n)), pltpu.SemaphoreType.DMA((7, n)),
                               pltpu.SemaphoreType.DMA((n,))])(*blocks)


def _add_sibling(name, g4, r1, pos):
    n, _, L, W = g4.shape
    tl = _tile(L, MM_ROWS, 16)

    def body(pos_ref, g_ref, r_ref, tb_ref, own_ref):
        t = g_ref[...] + r_ref[...]
        tb_ref[...] = t.astype(BF16)

        @pl.when(pl.program_id(1) == pos_ref[1])
        def _():
            own_ref[...] = t

    gs = pltpu.PrefetchScalarGridSpec(
        num_scalar_prefetch=1, grid=(L // tl, n),
        in_specs=[pl.BlockSpec((None, None, tl, W), lambda i, k, p: (k, p[0], i, 0)),
                  pl.BlockSpec((None, tl, W), lambda i, k, p: (k, i, 0))],
        out_specs=(pl.BlockSpec((None, tl, W), lambda i, k, p: (k, i, 0)),
                   pl.BlockSpec((tl, W), lambda i, k, p: (i, 0))))
    return _pc(body, name=name, grid_spec=gs,
               out_shape=(jax.ShapeDtypeStruct((n, L, W), BF16), jax.ShapeDtypeStruct((L, W), F32)),
               compiler_params=_sem("parallel", "arbitrary"))(pos, g4, r1)


def _add_chips(name, own, r2):
    L, W = own.shape
    tl = _tile(L, MM_ROWS, 16)

    def body(o_ref, r_ref, out_ref):
        acc = o_ref[...]
        for j in range(3):
            acc = acc + r_ref[j].astype(F32)
        out_ref[...] = acc

    return _pc(body, name=name, grid=(L // tl,),
               in_specs=[pl.BlockSpec((tl, W), lambda i: (i, 0)), pl.BlockSpec((3, tl, W), lambda i: (0, i, 0))],
               out_specs=pl.BlockSpec((tl, W), lambda i: (i, 0)), out_shape=jax.ShapeDtypeStruct((L, W), F32),
               compiler_params=_sem("parallel"))(own, r2)


_HBM = pl.BlockSpec(memory_space=pltpu.HBM)
_SEM = pl.BlockSpec(memory_space=pltpu.SEMAPHORE)
_EFFECT = pltpu.SideEffectType.DATAFLOW_SIDE_EFFECTING


def _in_hbm(a):
    return pltpu.with_memory_space_constraint(a, pltpu.HBM)


def _after(x, token):
    return x + token[0, 0].astype(x.dtype)


def _copies_start(name, bufs, sem_shape, plan):
    nb = len(bufs)

    def body(*refs):
        for cp in plan(refs[:nb], refs[nb], refs[nb + 1]):
            cp.start()
        refs[-1][...] = jnp.zeros_like(refs[-1])

    out = _pc(body, name=name, in_specs=[_HBM] * nb,
              out_specs=(_SEM, _SEM, *[_HBM] * nb, pl.BlockSpec(memory_space=pltpu.VMEM)),
              out_shape=(pltpu.SemaphoreType.DMA(sem_shape), pltpu.SemaphoreType.DMA(sem_shape),
                         *[pltpu.HBM(b.shape, b.dtype) for b in bufs], jax.ShapeDtypeStruct((8, LANES), F32)),
              input_output_aliases={i: 2 + i for i in range(nb)},
              compiler_params=pltpu.CompilerParams(has_side_effects=_EFFECT))(*[_in_hbm(b) for b in bufs])
    return out[0], out[1], list(out[2:2 + nb]), out[-1]


def _copies_wait(name, bufs, send_sems, recv_sems, plan, after):
    nb = len(bufs)

    def body(*refs):
        for cp in plan(refs[:nb], refs[nb], refs[nb + 1]):
            cp.wait_send()
            cp.wait_recv()

    return list(_pc(body, name=name, in_specs=[_HBM] * nb + [_SEM, _SEM, pl.BlockSpec(memory_space=pl.ANY)],
                    out_specs=[_HBM] * nb, out_shape=[pltpu.HBM(b.shape, b.dtype) for b in bufs],
                    input_output_aliases={i: i for i in range(nb)},
                    compiler_params=pltpu.CompilerParams(has_side_effects=_EFFECT))(*bufs, send_sems, recv_sems, after))


def _plan_gather_chips(n):
    def plan(refs, send_sems, recv_sems):
        x, y, c = _position()
        peers = [(x, y, 1 - c), (1 - x, y, c), (x, 1 - y, c), (1 - x, 1 - y, c)]
        return [pltpu.make_async_remote_copy(src_ref=refs[i], dst_ref=refs[n + i].at[4 * x + 2 * y + c],
                                             send_sem=send_sems.at[k * n + i], recv_sem=recv_sems.at[k * n + i],
                                             device_id=peer, device_id_type=MESH)
                for i in range(n) for k, peer in enumerate(peers)]
    return plan


def _plan_gather_sibling(n):
    def plan(refs, send_sems, recv_sems):
        x, y, c = _position()
        slots = [4 * (1 - x) + 2 * y + c, 4 * x + 2 * (1 - y) + c, 4 * (1 - x) + 2 * (1 - y) + c]
        return [pltpu.make_async_remote_copy(src_ref=refs[i].at[s], dst_ref=refs[i].at[s],
                                             send_sem=send_sems.at[k * n + i], recv_sem=recv_sems.at[k * n + i],
                                             device_id=(x, y, 1 - c), device_id_type=MESH)
                for i in range(n) for k, s in enumerate(slots)]
    return plan


def _plan_reduce_sibling(n):
    def plan(refs, send_sems, recv_sems):
        x, y, c = _position()
        return [pltpu.make_async_remote_copy(src_ref=refs[i].at[k, 1 - c], dst_ref=refs[n + i].at[k],
                                             send_sem=send_sems.at[k * n + i], recv_sem=recv_sems.at[k * n + i],
                                             device_id=(x, y, 1 - c), device_id_type=MESH)
                for i in range(n) for k in range(N_DEV // 2)]
    return plan


def _plan_reduce_chips(n):
    def plan(refs, send_sems, recv_sems):
        x, y, c = _position()
        cps = []
        for i in range(n):
            for j in range(1, 4):
                px, py = _flip(x, j & 2), _flip(y, j & 1)
                sem = (j - 1) * n + i
                cps.append(pltpu.make_async_remote_copy(src_ref=refs[i].at[2 * px + py], dst_ref=refs[n + i].at[j - 1],
                                                        send_sem=send_sems.at[sem], recv_sem=recv_sems.at[sem],
                                                        device_id=(px, py, c), device_id_type=MESH))
        return cps
    return plan


def _heads_major(name, a, after):
    S, W = a.shape
    H = W // HEAD_DIM
    tm = _tile(S, 512, 16)

    def body(a_ref, after_ref, o_ref):
        v = a_ref[...]
        for h in range(H):
            o_ref[h] = v[:, h * HEAD_DIM:(h + 1) * HEAD_DIM]

    return _pc(body, name=name, grid=(S // tm,),
               in_specs=[pl.BlockSpec((tm, W), lambda i: (i, 0)), pl.BlockSpec(memory_space=pl.ANY)],
               out_specs=pl.BlockSpec((H, tm, HEAD_DIM), lambda i: (0, i, 0)),
               out_shape=jax.ShapeDtypeStruct((H, S, HEAD_DIM), a.dtype), compiler_params=_sem("parallel"))(a, after)


def _heads_minor(name, a):
    H, S, _ = a.shape
    tm = _tile(S, 512, 16)

    def body(a_ref, o_ref):
        o_ref[...] = jnp.concatenate([a_ref[h] for h in range(H)], axis=1)

    return _pc(body, name=name, grid=(S // tm,), in_specs=[pl.BlockSpec((H, tm, HEAD_DIM), lambda i: (0, i, 0))],
               out_specs=pl.BlockSpec((tm, H * HEAD_DIM), lambda i: (i, 0)),
               out_shape=jax.ShapeDtypeStruct((S, H * HEAD_DIM), a.dtype), compiler_params=_sem("parallel"))(a)


def kernel(x, mem, attn_norm, attn_w_qkv, attn_q_gain, attn_k_gain, attn_w_o, pool_norm, pool_w, pool_scale, xattn_norm, mem_norm, xattn_w_q, xattn_w_kv, xattn_w_o, ffn_norm, ffn_w_up, ffn_conv_w, ffn_conv_b, ffn_w_down, final_norm, loss_target, m_attn_norm, m_attn_w_qkv, m_attn_q_gain, m_attn_k_gain, m_attn_w_o, m_pool_norm, m_pool_w, m_pool_scale, m_xattn_norm, m_mem_norm, m_xattn_w_q, m_xattn_w_kv, m_xattn_w_o, m_ffn_norm, m_ffn_w_up, m_ffn_conv_w, m_ffn_conv_b, m_ffn_w_down, m_final_norm, v_attn_norm, v_attn_w_qkv, v_attn_q_gain, v_attn_k_gain, v_attn_w_o, v_pool_norm, v_pool_w, v_pool_scale, v_xattn_norm, v_mem_norm, v_xattn_w_q, v_xattn_w_kv, v_xattn_w_o, v_ffn_norm, v_ffn_w_up, v_ffn_conv_w, v_ffn_conv_b, v_ffn_w_down, v_final_norm):
    names = ['attn_norm', 'attn_w_qkv', 'attn_q_gain', 'attn_k_gain', 'attn_w_o', 'pool_norm', 'pool_w', 'pool_scale',
             'xattn_norm', 'mem_norm', 'xattn_w_q', 'xattn_w_kv', 'xattn_w_o', 'ffn_norm', 'ffn_w_up', 'ffn_conv_w',
             'ffn_conv_b', 'ffn_w_down', 'final_norm']
    W = dict(zip(names, (attn_norm, attn_w_qkv, attn_q_gain, attn_k_gain, attn_w_o, pool_norm, pool_w, pool_scale,
                         xattn_norm, mem_norm, xattn_w_q, xattn_w_kv, xattn_w_o, ffn_norm, ffn_w_up, ffn_conv_w,
                         ffn_conv_b, ffn_w_down, final_norm)))
    Mo = dict(zip(names, (m_attn_norm, m_attn_w_qkv, m_attn_q_gain, m_attn_k_gain, m_attn_w_o, m_pool_norm, m_pool_w,
                          m_pool_scale, m_xattn_norm, m_mem_norm, m_xattn_w_q, m_xattn_w_kv, m_xattn_w_o, m_ffn_norm,
                          m_ffn_w_up, m_ffn_conv_w, m_ffn_conv_b, m_ffn_w_down, m_final_norm)))
    Vo = dict(zip(names, (v_attn_norm, v_attn_w_qkv, v_attn_q_gain, v_attn_k_gain, v_attn_w_o, v_pool_norm, v_pool_w,
                          v_pool_scale, v_xattn_norm, v_mem_norm, v_xattn_w_q, v_xattn_w_kv, v_xattn_w_o, v_ffn_norm,
                          v_ffn_w_up, v_ffn_conv_w, v_ffn_conv_b, v_ffn_w_down, v_final_norm)))

    S, D = x.shape[1], x.shape[2]
    n_layers = xattn_norm.shape[0]
    n_up = ffn_w_up.shape[2]
    qkv_w = attn_w_qkv.shape[2] * N_DEV
    n_heads = qkv_w // HEAD_DIM - 2 * N_KV_HEADS
    n_rot = (n_heads + N_KV_HEADS) * HEAD_DIM // LANES
    group_w = pool_w.shape[3]
    xs, mems, tgt = x[0], mem[0], loss_target[0]
    xi, yi, ci = _position()
    dev = 4 * xi + 2 * yi + ci
    pos = jnp.stack([ci, 2 * xi + yi]).astype(jnp.int32)

    layers = range(n_layers)
    n_groups = pool_w.shape[1]
    small_vec = jnp.concatenate([pool_norm.reshape(-1), pool_scale.reshape(-1), ffn_conv_w.reshape(-1)])
    small_rows = _round_up(-(-small_vec.size // PACK_W), 8)
    small_vec = jnp.pad(small_vec, (0, small_rows * PACK_W - small_vec.size)).reshape(small_rows, PACK_W)
    w_qkv, w_o, small, attn_token = _allgather_blocks(
        "allgather_attn", [attn_w_qkv[0].astype(BF16), attn_w_o[0].astype(BF16), small_vec])
    small = small.reshape(N_DEV, -1)
    w_qkv = w_qkv.transpose(1, 0, 2).reshape(D, qkv_w)
    w_o = w_o.reshape(-1, D)
    blocks = [pool_w.reshape(-1, group_w)] + [xattn_w_q[l] for l in layers] + [xattn_w_kv.reshape(n_layers * D, -1)]
    blocks += [xattn_w_o[l] for l in layers] + [ffn_w_up.reshape(n_layers * D, n_up)] + [ffn_w_down[l] for l in layers]
    blocks = [b.astype(BF16) for b in blocks]
    blocks[0] = _after(blocks[0], attn_token)
    n_blk = len(blocks)
    lands = [lax.dynamic_update_index_in_dim(lax.empty((N_DEV,) + b.shape, BF16), b, dev, 0) for b in blocks]
    plan_chips, plan_sibling = _plan_gather_chips(n_blk), _plan_gather_sibling(n_blk)
    gather_sems = _copies_start("gather_chips_start", blocks + lands, (4 * n_blk,), plan_chips)
    attn_norm_late = _after(attn_norm, gather_sems[3])

    d_sh = pool_norm.shape[1]
    pool_norm_f = small[:, :d_sh].reshape(1, D)
    pool_scale_f = small[:, d_sh:2 * d_sh].reshape(1, D)
    conv_w_f = small[:, 2 * d_sh:2 * d_sh + ffn_conv_w.size].reshape(N_DEV, n_layers, 3, n_up)
    conv_b_f = ffn_conv_b.reshape(n_layers, N_DEV, 1, n_up)

    cos, sin = _rope_tables(S)
    bd = _head_mean_matrix()
    pad_w = qkv_w - (n_heads + N_KV_HEADS) * HEAD_DIM
    qk_gain = jnp.concatenate([jnp.tile(attn_q_gain[0], n_heads), jnp.tile(attn_k_gain[0], N_KV_HEADS),
                               jnp.ones((pad_w,), F32)]).reshape(1, qkv_w)
    qk_scale = jnp.concatenate([jnp.full((n_heads * HEAD_DIM,), HEAD_DIM ** -0.5, F32),
                                jnp.ones((qkv_w - n_heads * HEAD_DIM,), F32)]).reshape(1, qkv_w)

    saved = []

    def xattn_ffn_fwd(l, xin, hx=None):
        if hx is None:
            hx = _rmsnorm(f"xattn_norm{l}", xin, xattn_norm[l:l + 1], BF16)
        memn = _rmsnorm(f"mem_norm{l}", mems, mem_norm[l:l + 1], BF16)
        qx = _mm_nn(f"xattn_q{l}", hx, w_xq[l], BF16)
        kv = _mm_nn_bs(f"xattn_kv{l}", memn, w_xkv, BF16, l)
        ox = _xattn_fwd(f"xattn_fwd{l}", qx, kv)
        x2 = _mm_nn(f"xattn_o{l}", ox, w_xo[l], F32, res=xin)
        hf = _rmsnorm(f"ffn_norm{l}", x2, ffn_norm[l:l + 1], BF16)
        cw = conv_w_f[:, l].reshape(2, N_DEV // 2, 3, n_up)
        cb = conv_b_f[l].reshape(2, N_DEV // 2, 1, n_up)
        u, act = _ffn_up_act(f"ffn_up_act{l}", hf, w_up.reshape(2, N_DEV // 2, n_layers * D, n_up), cw, cb, l)
        x3 = _mm_nn_as(f"ffn_down{l}", act, w_down[l], F32, x2)
        saved.append(dict(xin=xin, hx=hx, memn=memn, qx=qx, kv=kv, ox=ox, x2=x2, hf=hf, u=u, cw=cw, cb=cb, act=act))
        return x3

    h0 = _rmsnorm("attn_norm", xs, attn_norm_late, BF16)
    qkv = _mm_nn("attn_qkv", h0, w_qkv, F32)
    qkr = _qk_rope("qk_rope", qkv, qk_gain, qk_scale, cos, sin, bd, n_rot)
    o_hm, lse = _attn_fwd("attn_fwd", qkr, n_heads)
    o_att = _heads_minor("attn_heads_minor", o_hm)
    arrived = _copies_wait("gather_chips_wait", gather_sems[2], gather_sems[0], gather_sems[1], plan_chips, o_att)
    pass_sems = _copies_start("gather_sibling_start", arrived[n_blk:], (3 * n_blk,), plan_sibling)
    x1 = _mm_nn("attn_o", o_att, _after(w_o, pass_sems[3]), F32, res=xs)
    hx0 = _rmsnorm("xattn_norm0", x1, xattn_norm[0:1], BF16)
    gathered = iter(_copies_wait("gather_sibling_wait", pass_sems[2], pass_sems[0], pass_sems[1], plan_sibling, hx0))
    w_pool = (next(gathered).reshape(N_DEV, n_groups, -1, group_w).transpose(1, 0, 2, 3)
              .reshape(n_groups, group_w, group_w))
    w_xq = [next(gathered).reshape(D, D) for l in layers]
    w_xkv = next(gathered)
    w_xo = [next(gathered).reshape(D, D) for l in layers]
    w_up = next(gathered)
    w_down = [next(gathered).reshape(-1, D) for l in layers]
    x3 = xattn_ffn_fwd(0, x1, hx0)
    hp = _rmsnorm("pool_norm", x3, pool_norm_f, F32)
    mixed = _pool_window("pool_window", hp, group_w, False, BF16)
    x4 = _pool_proj("pool_proj", mixed, w_pool, pool_scale_f, x3)
    x6 = xattn_ffn_fwd(1, x4)

    G = {}
    g, d_final, lvec = _loss_head("loss_head", x6, final_norm.reshape(1, D), tgt)
    G['final_norm'] = d_final.reshape(D)
    loss_part = (0.5 * jnp.sum(lvec) / D).reshape(1)

    d_xn, d_mn, d_fn, d_xq, d_xkv, d_xo, d_up, d_cw, d_cb, d_down = ([None] * n_layers for _ in range(10))

    def xattn_ffn_bwd(l, g, conv_b_late=None, after_act=None):
        sv = saved[l]
        d_down[l] = _mm_tn_as(f"ffn_down_dw{l}", sv['act'], g, F32)
        du, st = _ffn_act_bwd(f"ffn_act_bwd{l}", sv['u'], g, w_down[l], sv['cw'],
                              sv['cb'] if conv_b_late is None else conv_b_late)
        du = du.reshape(N_DEV, S, n_up)
        ffn_gain = ffn_norm[l:l + 1] if after_act is None else _after(ffn_norm[l:l + 1], after_act(du))
        st = st.reshape(N_DEV, 8, n_up)
        d_cw[l], d_cb[l] = st[:, 0:3], st[:, 3].reshape(-1)
        d_up[l] = _mm_tn_bs(f"ffn_up_dw{l}", sv['hf'], du, F32)
        g, d_fn[l] = _mm_nt_abs_norm_bwd(f"ffn_up_dx_norm_bwd{l}", du, w_up, sv['x2'], ffn_gain, g, l)
        d_xo[l] = _mm_tn(f"xattn_o_dw{l}", sv['ox'], g, F32)
        do = _mm_nt(f"xattn_o_dx{l}", g, w_xo[l], BF16)
        dq, dkv = _xattn_bwd(f"xattn_bwd{l}", sv['qx'], sv['kv'], do)
        d_xq[l] = _mm_tn(f"xattn_q_dw{l}", sv['hx'], dq, F32)
        d_xkv[l] = _mm_tn_bs(f"xattn_kv_dw{l}", sv['memn'], dkv, F32)
        dmemn = _mm_nt_abs(f"xattn_kv_dx{l}", dkv, w_xkv, D, F32, l)
        _, d_mn[l] = _rmsnorm_bwd(f"mem_norm_bwd{l}", mems, mem_norm[l:l + 1], dmemn)
        g, d_xn[l] = _mm_nt_norm_bwd(f"xattn_q_dx_norm_bwd{l}", dq, w_xq[l], sv['xin'], xattn_norm[l:l + 1], g)
        return g

    def reduce_start(tag, bufs):
        n = len(bufs)
        g4s = [b.reshape((N_DEV // 2, 2) + b.shape[1:]) for b in bufs]
        lands = [lax.empty((N_DEV // 2,) + b.shape[1:], F32) for b in bufs]
        plan = _plan_reduce_sibling(n)
        return (n, plan) + _copies_start(f"reduce_sibling_start_{tag}", g4s + lands, (N_DEV // 2 * n,), plan)

    def reduce_between(tag, state, after):
        n, plan, send_sems, recv_sems, thru, _ = state
        got = _copies_wait(f"reduce_sibling_wait_{tag}", thru, send_sems, recv_sems, plan, after)
        sums = [_add_sibling(f"reduce_add_sibling_{tag}{i}", got[i], got[n + i], pos) for i in range(n)]
        lands = [lax.empty((3,) + tb.shape[1:], BF16) for tb, _ in sums]
        plan = _plan_reduce_chips(n)
        return (n, plan, [own for _, own in sums]) + _copies_start(f"reduce_chips_start_{tag}",
                                                                    [tb for tb, _ in sums] + lands, (3 * n,), plan)

    def reduce_finish(tag, state, after):
        n, plan, owns, send_sems, recv_sems, thru, _ = state
        got = _copies_wait(f"reduce_chips_wait_{tag}", thru, send_sems, recv_sems, plan, after)
        return [_add_chips(f"reduce_add_chips_{tag}{i}", owns[i], got[n + i]) for i in range(n)]

    def layer_bufs(l):
        return [d_xq[l].reshape(N_DEV, -1, D), d_xkv[l], d_xo[l].reshape(N_DEV, -1, D), d_up[l],
                d_down[l].reshape(N_DEV, -1, D)]

    g = xattn_ffn_bwd(1, g)
    d_mixed, d_pool_w, d_pool_scale = _pool_proj_bwd("pool_proj_bwd", g, mixed, w_pool, pool_scale_f)
    dhp = _pool_window("pool_window_bwd", d_mixed, group_w, True, F32)
    g, d_pool_norm = _rmsnorm_bwd("pool_norm_bwd", x3, pool_norm_f, dhp, g)
    upper = reduce_start("upper", [d_pool_w.reshape(n_groups, N_DEV, -1, group_w).transpose(1, 0, 2, 3)
                                   .reshape(N_DEV, -1, group_w)] + layer_bufs(1))
    between = []

    def upper_between(du):
        between.append(reduce_between("upper", upper, du))
        return between[0][-1]

    g = xattn_ffn_bwd(0, g, _after(saved[0]['cb'], upper[-1]), upper_between)
    lower = reduce_start("lower", layer_bufs(0))
    d_wo = _mm_tn("attn_o_dw", o_att, g, F32)
    do = _mm_nt("attn_o_dx", g, _after(w_o, lower[-1]), BF16)
    lower = reduce_between("lower", lower, do)
    do_hm = _heads_major("attn_heads_major", do, lower[-1])
    dq_hm, dk_hm, dv_hm = _attn_bwd("attn_bwd", qkr, o_hm, lse, do_hm)
    red_lower = reduce_finish("lower", lower, dq_hm)
    d_qkv, d_gain = _qk_rope_bwd("qk_rope_bwd", dq_hm, dk_hm, dv_hm, qkv, qk_gain, qk_scale, cos, sin, bd)
    red_upper = reduce_finish("upper", between[0], d_qkv)
    d_wqkv = _mm_tn("attn_qkv_dw", h0, d_qkv, F32)
    last = reduce_start("last", [d_wqkv.reshape(D, N_DEV, -1).transpose(1, 0, 2), d_wo.reshape(N_DEV, -1, D)])
    last = reduce_between("last", last, d_wqkv)
    grad_x, d_attn_norm = _mm_nt_norm_bwd("attn_qkv_dx_norm_bwd", d_qkv, w_qkv, xs, attn_norm, g, last[-1])
    G['pool_w'] = red_upper[0].reshape(pool_w.shape)
    per_layer = [red_lower, red_upper[1:]]
    for i, n in enumerate(['xattn_w_q', 'xattn_w_kv', 'xattn_w_o', 'ffn_w_up', 'ffn_w_down']):
        G[n] = jnp.stack([per_layer[l][i] for l in layers])

    hq = n_heads * HEAD_DIM
    small_g = {'attn_norm': d_attn_norm, 'attn_q_gain': d_gain[0, :hq].reshape(n_heads, HEAD_DIM).sum(0),
               'attn_k_gain': d_gain[0, hq:hq + N_KV_HEADS * HEAD_DIM].reshape(N_KV_HEADS, HEAD_DIM).sum(0),
               'pool_norm': d_pool_norm, 'pool_scale': d_pool_scale,
               'xattn_norm': jnp.concatenate(d_xn), 'mem_norm': jnp.concatenate(d_mn), 'ffn_norm': jnp.concatenate(d_fn),
               'ffn_conv_w': jnp.stack(d_cw, axis=1), 'ffn_conv_b': jnp.stack(d_cb)}
    order = list(small_g)
    flat = jnp.concatenate([loss_part] + [small_g[n].reshape(-1) for n in order] + [G['final_norm']])
    ar_rows = _round_up(-(-flat.size // PACK_W), 8)
    flat = jnp.pad(flat, (0, ar_rows * PACK_W - flat.size)).reshape(ar_rows, PACK_W)
    summed = _sum_slots("allreduce_sum", _allgather_small("allreduce_gather", flat)).reshape(-1)
    loss = summed[0]
    red_last = reduce_finish("last", last, summed)
    G['attn_w_qkv'], G['attn_w_o'] = red_last[0][None], red_last[1][None]
    at = 1
    for n in order + ['final_norm']:
        size = G['final_norm'].size if n == 'final_norm' else small_g[n].size
        piece = summed[at:at + size]
        at += size
        if n in ('pool_norm', 'pool_scale'):
            piece = lax.dynamic_slice(piece, (dev * d_sh,), (d_sh,))
        elif n == 'ffn_conv_w':
            piece = lax.dynamic_index_in_dim(piece.reshape(N_DEV, n_layers, 3, n_up), dev, 0, keepdims=False)
        G[n] = piece.reshape(W[n].shape)

    deltas, new_m, new_v = [], [], []
    for n in names:
        d, nm, nv = _adamw(f"adamw_{n}", W[n], G[n], Mo[n], Vo[n])
        deltas.append(d)
        new_m.append(nm)
        new_v.append(nv)
    return (loss, grad_x[None], *[G[n] for n in names], *deltas, *new_m, *new_v)
```

```python
import jax
import jax.numpy as jnp
from jax import lax
from jax.experimental import pallas as pl
from jax.experimental.pallas import tpu as pltpu

F32 = jnp.float32
BF16 = jnp.bfloat16
MESH = pl.DeviceIdType.MESH

N_DEV = 8
EPS = 1e-6
HEAD_DIM = 64
N_KV_HEADS = 4
X_HEADS = 4
GRID_W = 64
ROPE_THETA = 10000.0
ROPE_PAIRS = HEAD_DIM // 4
POOL_WINDOWS = (2, 4, 8, 16)
POOL_PAD = 16
KEY_CHUNK = 1024
MM_ROWS = 1024
REDUCE_ROWS = 2048
FFN_HALO = 16
LANES = 128
PACK_W = 1024
ADAM_LR, ADAM_B1, ADAM_B2, ADAM_EPS, ADAM_WD, ADAM_STEP = 0.001, 0.9, 0.999, 1e-08, 0.01, 10

_NN = (((1,), (0,)), ((), ()))
_NT = (((1,), (1,)), ((), ()))
_TN = (((0,), (0,)), ((), ()))


def _pc(body, *, name, **kw):
    return pl.pallas_call(body, name=name, **kw)


def _sem(*kinds):
    return pltpu.CompilerParams(dimension_semantics=kinds)


def _tile(n, pref, mult):
    best = None
    for t in range(mult, min(n, pref) + 1, mult):
        if n % t == 0:
            best = t
    return n if best is None else best


def _round_up(n, m):
    return (n + m - 1) // m * m


def _mm_call(name, a, b, dims, grid, a_spec, b_spec, o_spec, out_shape, kaxis, res=None, res_spec=None):
    nk = grid[kaxis]
    acc_shape = tuple(d for d in o_spec.block_shape if d is not None)
    in_place = out_shape.dtype == F32
    use_scratch = nk > 1 and not in_place

    def body(*refs):
        refs = list(refs)
        acc = refs.pop() if use_scratch else None
        a_ref, b_ref = refs[:2]
        r_ref = refs[2] if res is not None else None
        o_ref = refs[-1]
        if len(a_ref.shape) == 3:
            n = a_ref.shape[2]
            prod = sum(jnp.dot(a_ref[s].astype(BF16), b_ref[s * n:(s + 1) * n, :].astype(BF16),
                               preferred_element_type=F32) for s in range(a_ref.shape[0]))
        else:
            prod = lax.dot_general(a_ref[...].astype(BF16), b_ref[...].astype(BF16), dims, preferred_element_type=F32)
        if nk == 1:
            if r_ref is not None:
                prod = prod + r_ref[...]
            o_ref[...] = prod.astype(o_ref.dtype)
            return
        k = pl.program_id(kaxis)
        tgt = o_ref if in_place else acc

        @pl.when(k == 0)
        def _():
            tgt[...] = prod + r_ref[...] if (in_place and r_ref is not None) else prod

        @pl.when(k > 0)
        def _():
            tgt[...] += prod

        if not in_place:
            @pl.when(k == nk - 1)
            def _():
                r = acc[...]
                if r_ref is not None:
                    r = r + r_ref[...]
                o_ref[...] = r.astype(o_ref.dtype)

    sem = tuple("arbitrary" if ax == kaxis else "parallel" for ax in range(len(grid)))
    ins = [a, b] if res is None else [a, b, res]
    specs = [a_spec, b_spec] if res is None else [a_spec, b_spec, res_spec]
    return _pc(body, name=name, grid=grid, in_specs=specs, out_specs=o_spec, out_shape=out_shape,
               scratch_shapes=[pltpu.VMEM(acc_shape, F32)] if use_scratch else [],
               compiler_params=_sem(*sem))(*ins)


def _reduce_rows(a, b):
    return REDUCE_ROWS * (2 if a.dtype == BF16 and b.dtype == BF16 else 1)


def _mm_nn(name, a, b, out_dtype, res=None):
    M, K = a.shape
    N = b.shape[1]
    tm, tn, tk = _tile(M, MM_ROWS, 16), _tile(N, 1024, LANES), _tile(K, 1024, LANES)
    return _mm_call(name, a, b, _NN, (M // tm, N // tn, K // tk),
                    pl.BlockSpec((tm, tk), lambda i, j, k: (i, k)),
                    pl.BlockSpec((tk, tn), lambda i, j, k: (k, j)),
                    pl.BlockSpec((tm, tn), lambda i, j, k: (i, j)),
                    jax.ShapeDtypeStruct((M, N), out_dtype), 2, res,
                    pl.BlockSpec((tm, tn), lambda i, j, k: (i, j)))


def _mm_nt(name, a, b, out_dtype):
    M, K = a.shape
    N = b.shape[0]
    tm, tn, tk = _tile(M, MM_ROWS, 16), _tile(N, 1024, LANES), _tile(K, 1024, LANES)
    return _mm_call(name, a, b, _NT, (M // tm, N // tn, K // tk),
                    pl.BlockSpec((tm, tk), lambda i, j, k: (i, k)),
                    pl.BlockSpec((tn, tk), lambda i, j, k: (j, k)),
                    pl.BlockSpec((tm, tn), lambda i, j, k: (i, j)),
                    jax.ShapeDtypeStruct((M, N), out_dtype), 2)


def _mm_tn(name, a, b, out_dtype):
    R, M = a.shape
    N = b.shape[1]
    tm, tn, tr = _tile(M, 1024, LANES), _tile(N, 1024, LANES), _tile(R, _reduce_rows(a, b), 16)
    return _mm_call(name, a, b, _TN, (M // tm, N // tn, R // tr),
                    pl.BlockSpec((tr, tm), lambda i, j, k: (k, i)),
                    pl.BlockSpec((tr, tn), lambda i, j, k: (k, j)),
                    pl.BlockSpec((tm, tn), lambda i, j, k: (i, j)),
                    jax.ShapeDtypeStruct((M, N), out_dtype), 2)


def _mm_nn_bs(name, a, b, out_dtype, layer=0):
    M, K = a.shape
    J, _, n = b.shape
    tm, tk = _tile(M, MM_ROWS, 16), _tile(K, 1024, LANES)
    first = layer * (K // tk)
    return _mm_call(name, a, b, _NN, (J, M // tm, K // tk),
                    pl.BlockSpec((tm, tk), lambda j, i, k: (i, k)),
                    pl.BlockSpec((None, tk, n), lambda j, i, k: (j, first + k, 0)),
                    pl.BlockSpec((None, tm, n), lambda j, i, k: (j, i, 0)),
                    jax.ShapeDtypeStruct((J, M, n), out_dtype), 2)


def _mm_nn_as(name, a, b, out_dtype, res):
    J, M, n = a.shape
    N = b.shape[1]
    tm, tn = _tile(M, MM_ROWS, 16), _tile(N, 1024, LANES)
    per = J if J <= 4 else (2 if J % 2 == 0 else 1)
    return _mm_call(name, a, b, _NN, (M // tm, N // tn, J // per),
                    pl.BlockSpec((per, tm, n), lambda i, j, k: (k, i, 0)),
                    pl.BlockSpec((per * n, tn), lambda i, j, k: (k, j)),
                    pl.BlockSpec((tm, tn), lambda i, j, k: (i, j)),
                    jax.ShapeDtypeStruct((M, N), out_dtype), 2, res,
                    pl.BlockSpec((tm, tn), lambda i, j, k: (i, j)))


def _mm_tn_as(name, a, b, out_dtype):
    J, R, n = a.shape
    N = b.shape[1]
    tn, tr = _tile(N, 1024, LANES), _tile(R, REDUCE_ROWS, 16)
    return _mm_call(name, a, b, _TN, (J, N // tn, R // tr),
                    pl.BlockSpec((None, tr, n), lambda j, jn, k: (j, k, 0)),
                    pl.BlockSpec((tr, tn), lambda j, jn, k: (k, jn)),
                    pl.BlockSpec((n, tn), lambda j, jn, k: (j, jn)),
                    jax.ShapeDtypeStruct((J * n, N), out_dtype), 2)


def _mm_nt_abs(name, a, b, N, out_dtype, layer=0):
    J, M, n = a.shape
    tm, tn = _tile(M, MM_ROWS, 16), _tile(N, 1024, LANES)
    first = layer * (N // tn)
    return _mm_call(name, a, b, _NT, (M // tm, N // tn, J),
                    pl.BlockSpec((None, tm, n), lambda i, j, k: (k, i, 0)),
                    pl.BlockSpec((None, tn, n), lambda i, j, k: (k, first + j, 0)),
                    pl.BlockSpec((tm, tn), lambda i, j, k: (i, j)),
                    jax.ShapeDtypeStruct((M, N), out_dtype), 2)


def _mm_tn_bs(name, a, b, out_dtype):
    R, M = a.shape
    J, _, n = b.shape
    tm, tr = _tile(M, 1024, LANES), _tile(R, _reduce_rows(a, b), 16)
    return _mm_call(name, a, b, _TN, (J, M // tm, R // tr),
                    pl.BlockSpec((tr, tm), lambda j, i, k: (k, i)),
                    pl.BlockSpec((None, tr, n), lambda j, i, k: (j, k, 0)),
                    pl.BlockSpec((None, tm, n), lambda j, i, k: (j, i, 0)),
                    jax.ShapeDtypeStruct((J, M, n), out_dtype), 2)


def _rmsnorm(name, x, g, out_dtype):
    R, D = x.shape
    tm = _tile(R, 512, 16)

    def body(x_ref, g_ref, o_ref):
        xv = x_ref[...]
        r = lax.rsqrt(jnp.mean(xv * xv, axis=-1, keepdims=True) + EPS)
        o_ref[...] = (xv * r * g_ref[...]).astype(o_ref.dtype)

    return _pc(body, name=name, grid=(R // tm,),
               in_specs=[pl.BlockSpec((tm, D), lambda i: (i, 0)), pl.BlockSpec((1, D), lambda i: (0, 0))],
               out_specs=pl.BlockSpec((tm, D), lambda i: (i, 0)),
               out_shape=jax.ShapeDtypeStruct((R, D), out_dtype), compiler_params=_sem("parallel"))(x, g)


def _mm_nn_res_norm(name, a, b, res, gain):
    M, K = a.shape
    N = b.shape[1]
    tm = _tile(M, MM_ROWS, 16)

    def body(a_ref, b_ref, r_ref, g_ref, x_ref, h_ref):
        xv = r_ref[...] + jnp.dot(a_ref[...].astype(BF16), b_ref[...].astype(BF16), preferred_element_type=F32)
        x_ref[...] = xv
        r = lax.rsqrt(jnp.mean(xv * xv, axis=-1, keepdims=True) + EPS)
        h_ref[...] = (xv * r * g_ref[...]).astype(BF16)

    row = pl.BlockSpec((tm, N), lambda i: (i, 0))
    return _pc(body, name=name, grid=(M // tm,),
               in_specs=[pl.BlockSpec((tm, K), lambda i: (i, 0)), pl.BlockSpec((K, N), lambda i: (0, 0)), row,
                         pl.BlockSpec((1, N), lambda i: (0, 0))],
               out_specs=(row, row),
               out_shape=(jax.ShapeDtypeStruct((M, N), F32), jax.ShapeDtypeStruct((M, N), BF16)),
               compiler_params=_sem("parallel"))(a, b, res, gain)


def _rmsnorm_bwd(name, x, g, dh, dres=None):
    R, D = x.shape
    tm = _tile(R, 512, 16)

    def body(*refs):
        if dres is None:
            x_ref, g_ref, dh_ref, dx_ref, dg_ref = refs
            dres_ref = None
        else:
            x_ref, g_ref, dh_ref, dres_ref, dx_ref, dg_ref = refs
        xv = x_ref[...]
        r = lax.rsqrt(jnp.mean(xv * xv, axis=-1, keepdims=True) + EPS)
        xh = xv * r
        dhv = dh_ref[...].astype(F32)

        @pl.when(pl.program_id(0) == 0)
        def _():
            dg_ref[...] = jnp.zeros_like(dg_ref)

        dg_ref[...] += jnp.sum(dhv * xh, axis=0, keepdims=True)
        dxh = dhv * g_ref[...]
        dx = r * (dxh - xh * jnp.mean(dxh * xh, axis=-1, keepdims=True))
        if dres_ref is not None:
            dx = dx + dres_ref[...]
        dx_ref[...] = dx

    row = pl.BlockSpec((tm, D), lambda i: (i, 0))
    vec = pl.BlockSpec((1, D), lambda i: (0, 0))
    ins = [x, g, dh] + ([] if dres is None else [dres])
    specs = [row, vec, row] + ([] if dres is None else [row])
    return _pc(body, name=name, grid=(R // tm,), in_specs=specs, out_specs=(row, vec),
               out_shape=(jax.ShapeDtypeStruct((R, D), F32), jax.ShapeDtypeStruct((1, D), F32)),
               compiler_params=_sem("arbitrary"))(*ins)


def _mm_norm_bwd(name, a, b, grid, a_spec, b_spec, tm, x, gain, dres, after=None):
    M, D = x.shape
    nk = grid[2]

    def body(*refs):
        a_ref, b_ref, x_ref, g_ref, r_ref = refs[:5]
        dx_ref, dg_ref = refs[-3 if nk > 1 else -2:][:2]
        acc = refs[-1] if nk > 1 else None
        i, k = pl.program_id(0), pl.program_id(2)
        if len(a_ref.shape) == 3:
            prod = sum(lax.dot_general(a_ref[s].astype(BF16), b_ref[s].astype(BF16), _NT, preferred_element_type=F32)
                       for s in range(a_ref.shape[0]))
        else:
            prod = lax.dot_general(a_ref[...].astype(BF16), b_ref[...].astype(BF16), _NT, preferred_element_type=F32)

        @pl.when((i == 0) & (k == 0))
        def _():
            dg_ref[...] = jnp.zeros_like(dg_ref)

        def finish(dh):
            xv = x_ref[...]
            r = lax.rsqrt(jnp.mean(xv * xv, axis=-1, keepdims=True) + EPS)
            xh = xv * r
            dg_ref[...] += jnp.sum(dh * xh, axis=0, keepdims=True)
            dxh = dh * g_ref[...]
            dx_ref[...] = r * (dxh - xh * jnp.mean(dxh * xh, axis=-1, keepdims=True)) + r_ref[...]

        if nk == 1:
            finish(prod)
            return

        @pl.when(k == 0)
        def _():
            acc[...] = prod

        @pl.when(k > 0)
        def _():
            acc[...] += prod

        @pl.when(k == nk - 1)
        def _():
            finish(acc[...])

    row = pl.BlockSpec((tm, D), lambda i, j, k: (i, 0))
    vec = pl.BlockSpec((1, D), lambda i, j, k: (0, 0))
    ins = [a, b, x, gain, dres] + ([] if after is None else [after])
    specs = [a_spec, b_spec, row, vec, row] + ([] if after is None else [pl.BlockSpec(memory_space=pl.ANY)])
    return _pc(body, name=name, grid=grid, in_specs=specs, out_specs=(row, vec),
               out_shape=(jax.ShapeDtypeStruct((M, D), F32), jax.ShapeDtypeStruct((1, D), F32)),
               scratch_shapes=[pltpu.VMEM((tm, D), F32)] if nk > 1 else [],
               compiler_params=_sem("arbitrary", "arbitrary", "arbitrary"))(*ins)


def _mm_nt_norm_bwd(name, a, b, x, gain, dres, after=None):
    M, K = a.shape
    D = b.shape[0]
    tm, tk = _tile(M, MM_ROWS, 16), _tile(K, 1024, LANES)
    return _mm_norm_bwd(name, a, b, (M // tm, 1, K // tk),
                        pl.BlockSpec((tm, tk), lambda i, j, k: (i, k)),
                        pl.BlockSpec((D, tk), lambda i, j, k: (0, k)), tm, x, gain, dres, after)


def _mm_nt_abs_norm_bwd(name, a, b, x, gain, dres, layer=0):
    J, M, n = a.shape
    D = x.shape[1]
    tm = _tile(M, MM_ROWS, 16)
    per = 2 if J % 2 == 0 else 1
    return _mm_norm_bwd(name, a, b, (M // tm, 1, J // per),
                        pl.BlockSpec((per, tm, n), lambda i, j, k: (k, i, 0)),
                        pl.BlockSpec((per, D, n), lambda i, j, k: (k, layer, 0)), tm, x, gain, dres)


def _loss_head(name, x, g, tgt):
    R, D = x.shape
    tm = _tile(R, 512, 16)

    def body(x_ref, g_ref, t_ref, dx_ref, dg_ref, l_ref):
        xv = x_ref[...]
        r = lax.rsqrt(jnp.mean(xv * xv, axis=-1, keepdims=True) + EPS)
        xh = xv * r
        err = xh * g_ref[...] - t_ref[...]

        @pl.when(pl.program_id(0) == 0)
        def _():
            dg_ref[...] = jnp.zeros_like(dg_ref)
            l_ref[...] = jnp.zeros_like(l_ref)

        l_ref[...] += jnp.sum(err * err, axis=0, keepdims=True)
        dy = err * (1.0 / D)
        dg_ref[...] += jnp.sum(dy * xh, axis=0, keepdims=True)
        dxh = dy * g_ref[...]
        dx_ref[...] = r * (dxh - xh * jnp.mean(dxh * xh, axis=-1, keepdims=True))

    row = pl.BlockSpec((tm, D), lambda i: (i, 0))
    vec = pl.BlockSpec((1, D), lambda i: (0, 0))
    return _pc(body, name=name, grid=(R // tm,), in_specs=[row, vec, row], out_specs=(row, vec, vec),
               out_shape=(jax.ShapeDtypeStruct((R, D), F32), jax.ShapeDtypeStruct((1, D), F32),
                          jax.ShapeDtypeStruct((1, D), F32)),
               compiler_params=_sem("arbitrary"))(x, g, tgt)


def _rope_tables(S):
    n_rows = S // GRID_W
    row = jnp.repeat(jnp.arange(n_rows, dtype=F32), GRID_W)
    col = jnp.tile(jnp.arange(GRID_W, dtype=F32), n_rows)
    inv_freq = ROPE_THETA ** (-jnp.arange(ROPE_PAIRS, dtype=F32) / ROPE_PAIRS)
    ang = jnp.stack([row[:, None] * inv_freq, col[:, None] * inv_freq], axis=1)
    cos, sin = jnp.cos(ang), jnp.sin(ang)
    c = jnp.broadcast_to(cos[:, :, None, :], (S, 2, 2, ROPE_PAIRS)).reshape(S, HEAD_DIM)
    s = jnp.stack([-sin, sin], axis=2).reshape(S, HEAD_DIM)
    reps = LANES // HEAD_DIM
    return jnp.tile(c, (1, reps)), jnp.tile(s, (1, reps))


def _head_mean_matrix():
    h = jnp.arange(LANES) // HEAD_DIM
    m = jnp.where(h[:, None] == h[None, :], 1.0 / HEAD_DIM, 0.0).astype(BF16)
    return jnp.concatenate([m, m], axis=0)


def _head_mean(v, bd):
    hi = v.astype(BF16)
    lo = (v - hi.astype(F32)).astype(BF16)
    return jnp.dot(jnp.concatenate([hi, lo], axis=1), bd, preferred_element_type=F32)


def _swap_halves(y):
    lane = lax.broadcasted_iota(jnp.int32, y.shape, 1)
    return jnp.where(lane % 32 < 16, pltpu.roll(y, LANES - 16, 1), pltpu.roll(y, 16, 1))


def _qk_rope(name, qkv, gain, scale, cos, sin, bd, n_rot):
    S, W = qkv.shape
    tm = _tile(S, 2048, 16)
    per = LANES // HEAD_DIM

    def body(x_ref, g_ref, s_ref, c_ref, sn_ref, bd_ref, o_ref):
        j = pl.program_id(1)
        xv = x_ref[...]

        def put(v):
            for h in range(per):
                o_ref[h] = v[:, h * HEAD_DIM:(h + 1) * HEAD_DIM].astype(BF16)

        @pl.when(j < n_rot)
        def _():
            ms = _head_mean(xv * xv, bd_ref[...])
            y = xv * lax.rsqrt(ms + EPS) * g_ref[...] * s_ref[...]
            put(y * c_ref[...] + _swap_halves(y) * sn_ref[...])

        @pl.when(j >= n_rot)
        def _():
            put(xv)

    blk = pl.BlockSpec((tm, LANES), lambda i, j: (i, j))
    vec = pl.BlockSpec((1, LANES), lambda i, j: (0, j))
    tab = pl.BlockSpec((tm, LANES), lambda i, j: (i, 0))
    return _pc(body, name=name, grid=(S // tm, W // LANES),
               in_specs=[blk, vec, vec, tab, tab, pl.BlockSpec((2 * LANES, LANES), lambda i, j: (0, 0))],
               out_specs=pl.BlockSpec((per, tm, HEAD_DIM), lambda i, j: (j, i, 0)),
               out_shape=jax.ShapeDtypeStruct((W // HEAD_DIM, S, HEAD_DIM), BF16),
               compiler_params=_sem("parallel", "parallel"))(qkv, gain, scale, cos, sin, bd)


def _qk_rope_bwd(name, dq, dk, dv, qkv, gain, scale, cos, sin, bd):
    S, W = qkv.shape
    tm = _tile(S, 2048, 16)
    per = LANES // HEAD_DIM
    nq, nk, nv = dq.shape[0] // per, dk.shape[0] // per, dv.shape[0] // per
    n_rot = nq + nk

    def body(dq_ref, dk_ref, dv_ref, x_ref, g_ref, s_ref, c_ref, sn_ref, bd_ref, dx_ref, dg_ref):
        j, i = pl.program_id(0), pl.program_id(1)

        @pl.when(i == 0)
        def _():
            dg_ref[...] = jnp.zeros_like(dg_ref)

        def rotate_back(d_ref):
            dv = jnp.concatenate([d_ref[h] for h in range(per)], axis=1)
            xv = x_ref[...]
            ms = _head_mean(xv * xv, bd_ref[...])
            r = lax.rsqrt(ms + EPS)
            z = xv * r
            dy = (dv * c_ref[...] - _swap_halves(dv) * sn_ref[...]) * s_ref[...]
            dg_ref[...] += jnp.sum(dy * z, axis=0, keepdims=True)
            dz = dy * g_ref[...]
            mz = _head_mean(dz * z, bd_ref[...])
            dx_ref[...] = (r * (dz - z * mz)).astype(BF16)

        @pl.when(j < nq)
        def _():
            rotate_back(dq_ref)

        @pl.when((j >= nq) & (j < n_rot))
        def _():
            rotate_back(dk_ref)

        @pl.when(j >= n_rot)
        def _():
            dx_ref[...] = jnp.concatenate([dv_ref[h] for h in range(per)], axis=1).astype(BF16)

    def part(first, count):
        return pl.BlockSpec((per, tm, HEAD_DIM), lambda j, i: (jnp.clip(j - first, 0, count - 1), i, 0))

    blk = pl.BlockSpec((tm, LANES), lambda j, i: (i, j))
    vec = pl.BlockSpec((1, LANES), lambda j, i: (0, j))
    tab = pl.BlockSpec((tm, LANES), lambda j, i: (i, 0))
    return _pc(body, name=name, grid=(W // LANES, S // tm),
               in_specs=[part(0, nq), part(nq, nk), part(n_rot, nv), blk, vec, vec, tab, tab,
                         pl.BlockSpec((2 * LANES, LANES), lambda j, i: (0, 0))],
               out_specs=(blk, vec),
               out_shape=(jax.ShapeDtypeStruct((S, W), BF16), jax.ShapeDtypeStruct((1, W), F32)),
               compiler_params=_sem("parallel", "arbitrary"))(dq, dk, dv, qkv, gain, scale, cos, sin, bd)


def _softmax_rows(s):
    m = jnp.max(s, axis=-1, keepdims=True)
    p = jnp.exp(s - m)
    return p, jnp.sum(p, axis=-1, keepdims=True)


def _attn_fwd(name, qkv, H):
    _, S, dh = qkv.shape
    G = H // N_KV_HEADS
    tq = _tile(S, 256, 16)
    kc = _tile(S, KEY_CHUNK, LANES)
    R = G * tq

    def body(q_ref, k_ref, v_ref, o_ref, lse_ref):
        q = q_ref[...].reshape(R, dh)
        m = jnp.full((R, 1), -1e30, F32)
        l = jnp.zeros((R, 1), F32)
        acc = jnp.zeros((R, dh), F32)
        for c in range(S // kc):
            rows = slice(c * kc, (c + 1) * kc)
            s = lax.dot_general(q, k_ref[rows, :], _NT, preferred_element_type=F32)
            m_new = jnp.maximum(m, jnp.max(s, axis=-1, keepdims=True))
            alpha = jnp.exp(m - m_new)
            p = jnp.exp(s - m_new)
            l = alpha * l + jnp.sum(p, axis=-1, keepdims=True)
            acc = alpha * acc + jnp.dot(p.astype(BF16), v_ref[rows, :], preferred_element_type=F32)
            m = m_new
        o_ref[...] = (acc / l).astype(BF16).reshape(G, tq, dh)
        lse_ref[...] = (m + jnp.log(l)).reshape(G, tq, 1)

    qs = pl.BlockSpec((G, tq, dh), lambda kv, i: (kv, i, 0))
    ls = pl.BlockSpec((G, tq, 1), lambda kv, i: (kv, i, 0))
    ks = pl.BlockSpec((None, S, dh), lambda kv, i: (H + kv, 0, 0))
    vs = pl.BlockSpec((None, S, dh), lambda kv, i: (H + N_KV_HEADS + kv, 0, 0))
    return _pc(body, name=name, grid=(N_KV_HEADS, S // tq), in_specs=[qs, ks, vs], out_specs=(qs, ls),
               out_shape=(jax.ShapeDtypeStruct((H, S, dh), BF16), jax.ShapeDtypeStruct((H, S, 1), F32)),
               compiler_params=_sem("parallel", "parallel"))(qkv, qkv, qkv)


def _attn_bwd(name, qkv, o, lse, do):
    H, S, dh = o.shape
    G = H // N_KV_HEADS
    tq = _tile(S, 128, 16)
    kc = _tile(S, KEY_CHUNK, LANES)
    R = G * tq

    def body(q_ref, k_ref, v_ref, o_ref, lse_ref, do_ref, dq_ref, dk_ref, dv_ref):
        @pl.when(pl.program_id(1) == 0)
        def _():
            dk_ref[...] = jnp.zeros_like(dk_ref)
            dv_ref[...] = jnp.zeros_like(dv_ref)

        qq, dd = q_ref[...].reshape(R, dh), do_ref[...].reshape(R, dh)
        delta = jnp.sum(dd.astype(F32) * o_ref[...].reshape(R, dh).astype(F32), axis=-1, keepdims=True)
        lse = lse_ref[...].reshape(R, 1)
        dq = jnp.zeros((R, dh), F32)
        for c in range(S // kc):
            rows = slice(c * kc, (c + 1) * kc)
            kk, vv = k_ref[rows, :], v_ref[rows, :]
            p = jnp.exp(lax.dot_general(qq, kk, _NT, preferred_element_type=F32) - lse)
            dv_ref[rows, :] += lax.dot_general(p.astype(BF16), dd, _TN, preferred_element_type=F32)
            dp = lax.dot_general(dd, vv, _NT, preferred_element_type=F32)
            ds = (p * (dp - delta)).astype(BF16)
            dq = dq + jnp.dot(ds, kk, preferred_element_type=F32)
            dk_ref[rows, :] += lax.dot_general(ds, qq, _TN, preferred_element_type=F32)
        dq_ref[...] = dq.reshape(G, tq, dh)

    qs = pl.BlockSpec((G, tq, dh), lambda kv, i: (kv, i, 0))
    ls = pl.BlockSpec((G, tq, 1), lambda kv, i: (kv, i, 0))
    ks = pl.BlockSpec((None, S, dh), lambda kv, i: (H + kv, 0, 0))
    vs = pl.BlockSpec((None, S, dh), lambda kv, i: (H + N_KV_HEADS + kv, 0, 0))
    acc = pl.BlockSpec((None, S, dh), lambda kv, i: (kv, 0, 0))
    return _pc(body, name=name, grid=(N_KV_HEADS, S // tq), in_specs=[qs, ks, vs, qs, ls, qs],
               out_specs=(qs, acc, acc),
               out_shape=(jax.ShapeDtypeStruct((H, S, dh), F32), jax.ShapeDtypeStruct((N_KV_HEADS, S, dh), F32),
                          jax.ShapeDtypeStruct((N_KV_HEADS, S, dh), F32)),
               compiler_params=_sem("parallel", "arbitrary"))(qkv, qkv, qkv, o, lse, do)


def _xattn_fwd(name, q, kv):
    S, D = q.shape
    _, M, dh = kv.shape
    scale = dh ** -0.5
    tq = _tile(S, 512, 16)

    def body(q_ref, kv_ref, o_ref):
        for h in range(X_HEADS):
            lo, hi = h * dh, (h + 1) * dh
            s = lax.dot_general(q_ref[:, lo:hi], kv_ref[h], _NT, preferred_element_type=F32) * scale
            p, l = _softmax_rows(s)
            o = jnp.dot(p.astype(BF16), kv_ref[X_HEADS + h], preferred_element_type=F32)
            o_ref[:, lo:hi] = (o / l).astype(BF16)

    row = pl.BlockSpec((tq, D), lambda i: (i, 0))
    return _pc(body, name=name, grid=(S // tq,),
               in_specs=[row, pl.BlockSpec((2 * X_HEADS, M, dh), lambda i: (0, 0, 0))],
               out_specs=row, out_shape=jax.ShapeDtypeStruct((S, D), BF16),
               compiler_params=_sem("parallel"))(q, kv)


def _xattn_bwd(name, q, kv, do):
    S, D = q.shape
    _, M, dh = kv.shape
    scale = dh ** -0.5
    tq = _tile(S, 512, 16)

    def body(q_ref, kv_ref, do_ref, dq_ref, dkv_ref):
        @pl.when(pl.program_id(0) == 0)
        def _():
            dkv_ref[...] = jnp.zeros_like(dkv_ref)

        for h in range(X_HEADS):
            lo, hi = h * dh, (h + 1) * dh
            qh, kh, vh, doh = q_ref[:, lo:hi], kv_ref[h], kv_ref[X_HEADS + h], do_ref[:, lo:hi]
            s = lax.dot_general(qh, kh, _NT, preferred_element_type=F32) * scale
            p, l = _softmax_rows(s)
            pn = p / l
            dkv_ref[X_HEADS + h] += lax.dot_general(pn.astype(BF16), doh, _TN, preferred_element_type=F32)
            dp = lax.dot_general(doh, vh, _NT, preferred_element_type=F32)
            ds = (pn * (dp - jnp.sum(pn * dp, axis=-1, keepdims=True)) * scale).astype(BF16)
            dq_ref[:, lo:hi] = jnp.dot(ds, kh, preferred_element_type=F32).astype(BF16)
            dkv_ref[h] += lax.dot_general(ds, qh, _TN, preferred_element_type=F32)

    row = pl.BlockSpec((tq, D), lambda i: (i, 0))
    full = pl.BlockSpec((2 * X_HEADS, M, dh), lambda i: (0, 0, 0))
    return _pc(body, name=name, grid=(S // tq,), in_specs=[row, full, row], out_specs=(row, full),
               out_shape=(jax.ShapeDtypeStruct((S, D), BF16), jax.ShapeDtypeStruct((2 * X_HEADS, M, dh), F32)),
               compiler_params=_sem("arbitrary"))(q, kv, do)


def _sigmoid(x):
    return 1.0 / (1.0 + jnp.exp(-x))


def _halo_specs(tm, n, S):
    nb = tm // 8
    last8 = S // 8 - 1
    main = pl.BlockSpec((2, None, tm, n), lambda j, i: (0, j, i, 0))
    prev = pl.BlockSpec((2, None, 8, n), lambda j, i: (0, j, jnp.maximum(i * nb - 1, 0), 0))
    nxt = pl.BlockSpec((2, None, 8, n), lambda j, i: (0, j, jnp.minimum((i + 1) * nb, last8), 0))
    return main, prev, nxt


def _ffn_up_act(name, h, w, cw, cb, layer):
    S, K = h.shape
    _, J, _, n = w.shape
    tm = _tile(S, 512, FFN_HALO)
    nblk = S // tm
    hb, last = tm // FFN_HALO, S // FFN_HALO - 1
    te = tm + 2 * FFN_HALO

    def body(h_ref, hp_ref, hn_ref, w_ref, cw_ref, b_ref, u_ref, a_ref):
        i = pl.program_id(1)
        zero = jnp.zeros((FFN_HALO, K), BF16)
        he = jnp.concatenate([jnp.where(i == 0, zero, hp_ref[...]), h_ref[...],
                              jnp.where(i == nblk - 1, zero, hn_ref[...])], axis=0)
        mid = slice(FFN_HALO, tm + FFN_HALO)
        c = []
        for half in range(2):
            ue = jnp.dot(he, w_ref[half], preferred_element_type=F32)
            um = ue[mid]
            u_ref[half] = um
            k = cw_ref[half]
            c.append(pltpu.roll(ue, 1, 0)[mid] * k[0:1] + um * k[1:2] + pltpu.roll(ue, te - 1, 0)[mid] * k[2:3]
                     + b_ref[half])
        a_ref[...] = (c[0] * _sigmoid(c[0]) * c[1]).astype(BF16)

    return _pc(body, name=name, grid=(J, nblk),
               in_specs=[pl.BlockSpec((tm, K), lambda j, i: (i, 0)),
                         pl.BlockSpec((FFN_HALO, K), lambda j, i: (jnp.maximum(i * hb - 1, 0), 0)),
                         pl.BlockSpec((FFN_HALO, K), lambda j, i: (jnp.minimum((i + 1) * hb, last), 0)),
                         pl.BlockSpec((2, None, K, n), lambda j, i: (0, j, layer, 0)),
                         pl.BlockSpec((2, None, 3, n), lambda j, i: (0, j, 0, 0)),
                         pl.BlockSpec((2, None, 1, n), lambda j, i: (0, j, 0, 0))],
               out_specs=(pl.BlockSpec((2, None, tm, n), lambda j, i: (0, j, i, 0)),
                          pl.BlockSpec((None, tm, n), lambda j, i: (j, i, 0))),
               out_shape=(jax.ShapeDtypeStruct((2, J, S, n), F32), jax.ShapeDtypeStruct((J, S, n), BF16)),
               compiler_params=_sem("parallel", "parallel"))(h, h, h, w, cw, cb)


def _ffn_act_bwd(name, u, g, w_down, cw, cb):
    _, J, S, n = u.shape
    D = g.shape[1]
    tm = _tile(S, 512, 16)
    nblk = S // tm
    te = tm + 16
    nb = tm // 8
    last8 = S // 8 - 1

    def body(u_ref, up_ref, un_ref, g_ref, gp_ref, gn_ref, wd_ref, w_ref, b_ref, du_ref, st_ref):
        i = pl.program_id(1)

        @pl.when(i == 0)
        def _():
            st_ref[...] = jnp.zeros_like(st_ref)

        def extended(before, main, after):
            return jnp.concatenate([jnp.where(i == 0, 0.0, before), main, jnp.where(i == nblk - 1, 0.0, after)], axis=0)

        mid = slice(8, tm + 8)
        da_e = lax.dot_general(extended(gp_ref[...], g_ref[...], gn_ref[...]).astype(BF16), wd_ref[...], _NT,
                               preferred_element_type=F32)
        ue, c = [], []
        for half in range(2):
            e = extended(up_ref[half], u_ref[half], un_ref[half])
            w = w_ref[half]
            ue.append((pltpu.roll(e, 1, 0), e, pltpu.roll(e, te - 1, 0)))
            c.append(ue[half][0] * w[0:1] + e * w[1:2] + ue[half][2] * w[2:3] + b_ref[half])
        sg = _sigmoid(c[0])
        dc = [da_e * c[1] * (sg * (1.0 + c[0] * (1.0 - sg))), da_e * (c[0] * sg)]
        r8 = lax.broadcasted_iota(jnp.int32, (8, n), 0)
        for half in range(2):
            w, d, (e_before, e, e_after) = w_ref[half], dc[half], ue[half]
            dm = d[mid]
            du = pltpu.roll(d, te - 1, 0)[mid] * w[0:1] + dm * w[1:2] + pltpu.roll(d, 1, 0)[mid] * w[2:3]
            du_ref[half] = du.astype(BF16)
            s0 = jnp.sum(dm * e_before[mid], axis=0, keepdims=True)
            s1 = jnp.sum(dm * e[mid], axis=0, keepdims=True)
            s2 = jnp.sum(dm * e_after[mid], axis=0, keepdims=True)
            s3 = jnp.sum(dm, axis=0, keepdims=True)
            st_ref[half] += jnp.where(r8 == 0, s0, jnp.where(r8 == 1, s1, jnp.where(r8 == 2, s2,
                                      jnp.where(r8 == 3, s3, 0.0))))

    main, prev, nxt = _halo_specs(tm, n, S)
    gmain = pl.BlockSpec((tm, D), lambda j, i: (i, 0))
    gprev = pl.BlockSpec((8, D), lambda j, i: (jnp.maximum(i * nb - 1, 0), 0))
    gnxt = pl.BlockSpec((8, D), lambda j, i: (jnp.minimum((i + 1) * nb, last8), 0))
    return _pc(body, name=name, grid=(J, nblk),
               in_specs=[main, prev, nxt, gmain, gprev, gnxt, pl.BlockSpec((n, D), lambda j, i: (j, 0)),
                         pl.BlockSpec((2, None, 3, n), lambda j, i: (0, j, 0, 0)),
                         pl.BlockSpec((2, None, 1, n), lambda j, i: (0, j, 0, 0))],
               out_specs=(main, pl.BlockSpec((2, None, 8, n), lambda j, i: (0, j, 0, 0))),
               out_shape=(jax.ShapeDtypeStruct((2, J, S, n), BF16), jax.ShapeDtypeStruct((2, J, 8, n), F32)),
               compiler_params=_sem("parallel", "arbitrary"))(u, u, u, g, g, g, w_down, cw, cb)


def _window_count(t, w, S):
    lo = jnp.maximum(t - w // 2, 0)
    hi = jnp.minimum(t + w - w // 2, S)
    return (hi - lo).astype(F32)


def _trailing_sums(x, w):
    k = 1
    while k < w:
        x = x + pltpu.roll(x, k, 0)
        k *= 2
    return x


def _pool_window(name, h, group_w, adjoint, out_dtype):
    S, D = h.shape
    SP = S + 2 * POOL_PAD
    per_group = group_w // LANES

    def body(h_ref, o_ref, xp):
        g = pl.program_id(0) // per_group
        t = lax.broadcasted_iota(jnp.int32, (S, LANES), 0)
        xp[0:POOL_PAD, :] = jnp.zeros((POOL_PAD, LANES), F32)
        xp[S + POOL_PAD:SP, :] = jnp.zeros((POOL_PAD, LANES), F32)
        for gi, w in enumerate(POOL_WINDOWS):
            @pl.when(g == gi)
            def _():
                hv = h_ref[...]
                cnt = _window_count(t, w, S)
                xp[POOL_PAD:S + POOL_PAD, :] = hv / cnt if adjoint else hv
                ahead = w // 2 if adjoint else w // 2 - 1
                sw = _trailing_sums(xp[...], w)
                if ahead:
                    sw = pltpu.roll(sw, SP - ahead, 0)
                win = sw[POOL_PAD:S + POOL_PAD]
                o_ref[...] = ((win if adjoint else win / cnt) - hv).astype(out_dtype)

    col = pl.BlockSpec((S, LANES), lambda j: (0, j))
    return _pc(body, name=name, grid=(D // LANES,), in_specs=[col], out_specs=col,
               out_shape=jax.ShapeDtypeStruct((S, D), out_dtype),
               scratch_shapes=[pltpu.VMEM((SP, LANES), F32)], compiler_params=_sem("parallel"))(h)


def _pool_proj(name, mixed, w, scale, res):
    S, D = mixed.shape
    G, gw, _ = w.shape
    tm = _tile(S, 512, 16)

    def body(m_ref, w_ref, s_ref, r_ref, o_ref):
        for g in range(G):
            lo, hi = g * gw, (g + 1) * gw
            y = jnp.dot(m_ref[:, lo:hi], w_ref[g], preferred_element_type=F32)
            o_ref[:, lo:hi] = r_ref[:, lo:hi] + y * s_ref[:, lo:hi]

    row = pl.BlockSpec((tm, D), lambda i: (i, 0))
    return _pc(body, name=name, grid=(S // tm,),
               in_specs=[row, pl.BlockSpec((G, gw, gw), lambda i: (0, 0, 0)), pl.BlockSpec((1, D), lambda i: (0, 0)), row],
               out_specs=row, out_shape=jax.ShapeDtypeStruct((S, D), F32),
               compiler_params=_sem("parallel"))(mixed, w, scale, res)


def _pool_proj_bwd(name, dy, mixed, w, scale):
    S, D = mixed.shape
    G, gw, _ = w.shape
    tm = _tile(S, 512, 16)

    def body(dy_ref, m_ref, w_ref, s_ref, dm_ref, dw_ref, ds_ref):
        @pl.when(pl.program_id(0) == 0)
        def _():
            dw_ref[...] = jnp.zeros_like(dw_ref)
            ds_ref[...] = jnp.zeros_like(ds_ref)

        for g in range(G):
            lo, hi = g * gw, (g + 1) * gw
            mg, dyg = m_ref[:, lo:hi], dy_ref[:, lo:hi]
            y = jnp.dot(mg, w_ref[g], preferred_element_type=F32)
            ds_ref[:, lo:hi] += jnp.sum(dyg * y, axis=0, keepdims=True)
            dyp = (dyg * s_ref[:, lo:hi]).astype(BF16)
            dm_ref[:, lo:hi] = lax.dot_general(dyp, w_ref[g], _NT, preferred_element_type=F32)
            dw_ref[g] += lax.dot_general(mg, dyp, _TN, preferred_element_type=F32)

    row = pl.BlockSpec((tm, D), lambda i: (i, 0))
    wsp = pl.BlockSpec((G, gw, gw), lambda i: (0, 0, 0))
    vec = pl.BlockSpec((1, D), lambda i: (0, 0))
    return _pc(body, name=name, grid=(S // tm,), in_specs=[row, row, wsp, vec], out_specs=(row, wsp, vec),
               out_shape=(jax.ShapeDtypeStruct((S, D), F32), jax.ShapeDtypeStruct((G, gw, gw), F32),
                          jax.ShapeDtypeStruct((1, D), F32)),
               compiler_params=_sem("arbitrary"))(dy, mixed, w, scale)


def _adamw(name, w, g, m, v):
    shape = w.shape
    C = shape[-1]
    R = w.size // C
    tm = _tile(R, 512, 8)

    def body(w_ref, g_ref, m_ref, v_ref, d_ref, nm_ref, nv_ref):
        gv = g_ref[...]
        nm = ADAM_B1 * m_ref[...] + (1.0 - ADAM_B1) * gv
        nv = ADAM_B2 * v_ref[...] + (1.0 - ADAM_B2) * (gv * gv)
        m_hat = nm / (1.0 - ADAM_B1 ** ADAM_STEP)
        v_hat = nv / (1.0 - ADAM_B2 ** ADAM_STEP)
        d_ref[...] = -ADAM_LR * (m_hat / (jnp.sqrt(v_hat) + ADAM_EPS) + ADAM_WD * w_ref[...])
        nm_ref[...] = nm
        nv_ref[...] = nv

    blk = pl.BlockSpec((tm, C), lambda i: (i, 0))
    sd = jax.ShapeDtypeStruct((R, C), F32)
    outs = _pc(body, name=name, grid=(R // tm,), in_specs=[blk] * 4, out_specs=(blk,) * 3, out_shape=(sd,) * 3,
               compiler_params=_sem("parallel"))(*(a.reshape(R, C) for a in (w, g, m, v)))
    return tuple(o.reshape(shape) for o in outs)


def _position():
    return lax.axis_index("x"), lax.axis_index("y"), lax.axis_index("c")


def _flip(v, bit):
    return 1 - v if bit else v


def _allgather_small(name, v):
    R, W = v.shape

    def body(v_ref, out_ref, send_sems, recv_sems):
        x, y, c = _position()
        me = 4 * x + 2 * y + c
        out_ref[me] = v_ref[...]
        sends = []
        for k in range(1, N_DEV):
            peer = (_flip(x, k & 4), _flip(y, k & 2), _flip(c, k & 1))
            cp = pltpu.make_async_remote_copy(src_ref=v_ref, dst_ref=out_ref.at[me], send_sem=send_sems.at[k - 1],
                                              recv_sem=recv_sems.at[k - 1], device_id=peer, device_id_type=MESH)
            cp.start()
            sends.append(cp)
        for k in range(1, N_DEV):
            peer = (_flip(x, k & 4), _flip(y, k & 2), _flip(c, k & 1))
            slot = 4 * peer[0] + 2 * peer[1] + peer[2]
            pltpu.make_async_remote_copy(src_ref=v_ref, dst_ref=out_ref.at[slot], send_sem=send_sems.at[k - 1],
                                         recv_sem=recv_sems.at[k - 1], device_id=peer, device_id_type=MESH).wait_recv()
        for cp in sends:
            cp.wait_send()

    vm = pl.BlockSpec(memory_space=pltpu.VMEM)
    return _pc(body, name=name, in_specs=[vm], out_specs=vm, out_shape=jax.ShapeDtypeStruct((N_DEV, R, W), F32),
               scratch_shapes=[pltpu.SemaphoreType.DMA((N_DEV - 1,)), pltpu.SemaphoreType.DMA((N_DEV - 1,))])(v)


def _sum_slots(name, a):
    n, R, W = a.shape

    def body(a_ref, o_ref):
        acc = a_ref[0]
        for s in range(1, n):
            acc = acc + a_ref[s]
        o_ref[...] = acc

    return _pc(body, name=name, grid=(1,), in_specs=[pl.BlockSpec((n, R, W), lambda i: (0, 0, 0))],
               out_specs=pl.BlockSpec((R, W), lambda i: (0, 0)), out_shape=jax.ShapeDtypeStruct((R, W), F32))(a)


def _allgather_blocks(name, blocks):
    n = len(blocks)

    def body(*refs):
        b_refs, out_refs, token = refs[:n], refs[n:2 * n], refs[2 * n]
        send_sems, recv_sems, local_sems = refs[2 * n + 1:]
        token[...] = jnp.zeros_like(token)
        x, y, c = _position()
        me, sibling = (x, y, c), (x, y, 1 - c)
        chips = [(1 - x, y), (x, 1 - y), (1 - x, 1 - y)]

        def slot(i, px, py, pc):
            return out_refs[i].at[4 * px + 2 * py + pc]

        def copy(i, k, block, to, src=None):
            return pltpu.make_async_remote_copy(src_ref=slot(i, *block) if src is None else src, dst_ref=slot(i, *block),
                                                send_sem=send_sems.at[k, i], recv_sem=recv_sems.at[k, i],
                                                device_id=to, device_id_type=MESH)

        mine = [pltpu.make_async_copy(b_refs[i], slot(i, *me), local_sems.at[i]) for i in range(n)]
        first = [copy(i, 1 + j, me, (*chip, c), src=b_refs[i]) for i in range(n) for j, chip in enumerate(chips)]
        first += [copy(i, 0, me, sibling, src=b_refs[i]) for i in range(n)]
        for cp in mine + first:
            cp.start()
        passed = []
        for j, chip in enumerate(chips):
            for i in range(n):
                copy(i, 1 + j, (*chip, c), me).wait_recv()
                passed.append(copy(i, 4 + j, (*chip, c), sibling))
                passed[-1].start()
        for i in range(n):
            copy(i, 0, sibling, me).wait_recv()
        for j, chip in enumerate(chips):
            for i in range(n):
                copy(i, 4 + j, (*chip, 1 - c), me).wait_recv()
        for cp in first + passed:
            cp.wait_send()
        for cp in mine:
            cp.wait()

    hbm = pl.BlockSpec(memory_space=pl.ANY)
    return _pc(body, name=name, in_specs=[hbm] * n, out_specs=[hbm] * n + [pl.BlockSpec(memory_space=pltpu.VMEM)],
               out_shape=[jax.ShapeDtypeStruct((N_DEV,) + b.shape, b.dtype) for b in blocks]
               + [jax.ShapeDtypeStruct((8, LANES), F32)],
               scratch_shapes=[pltpu.SemaphoreType.DMA((7, n)), pltpu.SemaphoreType.DMA((7, n)),
                               pltpu.SemaphoreType.DMA((n,))])(*blocks)


def _add_sibling(name, g4, r1, pos):
    n, _, L, W = g4.shape
    tl = _tile(L, MM_ROWS, 16)

    def body(pos_ref, g_ref, r_ref, tb_ref, own_ref):
        t = g_ref[...] + r_ref[...]
        tb_ref[...] = t.astype(BF16)

        @pl.when(pl.program_id(1) == pos_ref[1])
        def _():
            own_ref[...] = t

    gs = pltpu.PrefetchScalarGridSpec(
        num_scalar_prefetch=1, grid=(L // tl, n),
        in_specs=[pl.BlockSpec((None, None, tl, W), lambda i, k, p: (k, p[0], i, 0)),
                  pl.BlockSpec((None, tl, W), lambda i, k, p: (k, i, 0))],
        out_specs=(pl.BlockSpec((None, tl, W), lambda i, k, p: (k, i, 0)),
                   pl.BlockSpec((tl, W), lambda i, k, p: (i, 0))))
    return _pc(body, name=name, grid_spec=gs,
               out_shape=(jax.ShapeDtypeStruct((n, L, W), BF16), jax.ShapeDtypeStruct((L, W), F32)),
               compiler_params=_sem("parallel", "arbitrary"))(pos, g4, r1)


def _add_chips(name, own, r2):
    L, W = own.shape
    tl = _tile(L, MM_ROWS, 16)

    def body(o_ref, r_ref, out_ref):
        acc = o_ref[...]
        for j in range(3):
            acc = acc + r_ref[j].astype(F32)
        out_ref[...] = acc

    return _pc(body, name=name, grid=(L // tl,),
               in_specs=[pl.BlockSpec((tl, W), lambda i: (i, 0)), pl.BlockSpec((3, tl, W), lambda i: (0, i, 0))],
               out_specs=pl.BlockSpec((tl, W), lambda i: (i, 0)), out_shape=jax.ShapeDtypeStruct((L, W), F32),
               compiler_params=_sem("parallel"))(own, r2)


_HBM = pl.BlockSpec(memory_space=pltpu.HBM)
_SEM = pl.BlockSpec(memory_space=pltpu.SEMAPHORE)
_EFFECT = pltpu.SideEffectType.DATAFLOW_SIDE_EFFECTING


def _in_hbm(a):
    return pltpu.with_memory_space_constraint(a, pltpu.HBM)


def _after(x, token):
    return x + token[0, 0].astype(x.dtype)


def _copies_start(name, bufs, sem_shape, plan):
    nb = len(bufs)

    def body(*refs):
        for cp in plan(refs[:nb], refs[nb], refs[nb + 1]):
            cp.start()
        refs[-1][...] = jnp.zeros_like(refs[-1])

    out = _pc(body, name=name, in_specs=[_HBM] * nb,
              out_specs=(_SEM, _SEM, *[_HBM] * nb, pl.BlockSpec(memory_space=pltpu.VMEM)),
              out_shape=(pltpu.SemaphoreType.DMA(sem_shape), pltpu.SemaphoreType.DMA(sem_shape),
                         *[pltpu.HBM(b.shape, b.dtype) for b in bufs], jax.ShapeDtypeStruct((8, LANES), F32)),
              input_output_aliases={i: 2 + i for i in range(nb)},
              compiler_params=pltpu.CompilerParams(has_side_effects=_EFFECT))(*[_in_hbm(b) for b in bufs])
    return out[0], out[1], list(out[2:2 + nb]), out[-1]


def _copies_wait(name, bufs, send_sems, recv_sems, plan, after):
    nb = len(bufs)

    def body(*refs):
        for cp in plan(refs[:nb], refs[nb], refs[nb + 1]):
            cp.wait_send()
            cp.wait_recv()

    return list(_pc(body, name=name, in_specs=[_HBM] * nb + [_SEM, _SEM, pl.BlockSpec(memory_space=pl.ANY)],
                    out_specs=[_HBM] * nb, out_shape=[pltpu.HBM(b.shape, b.dtype) for b in bufs],
                    input_output_aliases={i: i for i in range(nb)},
                    compiler_params=pltpu.CompilerParams(has_side_effects=_EFFECT))(*bufs, send_sems, recv_sems, after))


def _plan_gather_chips(n):
    def plan(refs, send_sems, recv_sems):
        x, y, c = _position()
        peers = [(x, y, 1 - c), (1 - x, y, c), (x, 1 - y, c), (1 - x, 1 - y, c)]
        return [pltpu.make_async_remote_copy(src_ref=refs[i], dst_ref=refs[n + i].at[4 * x + 2 * y + c],
                                             send_sem=send_sems.at[k * n + i], recv_sem=recv_sems.at[k * n + i],
                                             device_id=peer, device_id_type=MESH)
                for i in range(n) for k, peer in enumerate(peers)]
    return plan


def _plan_gather_sibling(n):
    def plan(refs, send_sems, recv_sems):
        x, y, c = _position()
        slots = [4 * (1 - x) + 2 * y + c, 4 * x + 2 * (1 - y) + c, 4 * (1 - x) + 2 * (1 - y) + c]
        return [pltpu.make_async_remote_copy(src_ref=refs[i].at[s], dst_ref=refs[i].at[s],
                                             send_sem=send_sems.at[k * n + i], recv_sem=recv_sems.at[k * n + i],
                                             device_id=(x, y, 1 - c), device_id_type=MESH)
                for i in range(n) for k, s in enumerate(slots)]
    return plan


def _plan_reduce_sibling(n):
    def plan(refs, send_sems, recv_sems):
        x, y, c = _position()
        return [pltpu.make_async_remote_copy(src_ref=refs[i].at[k, 1 - c], dst_ref=refs[n + i].at[k],
                                             send_sem=send_sems.at[k * n + i], recv_sem=recv_sems.at[k * n + i],
                                             device_id=(x, y, 1 - c), device_id_type=MESH)
                for i in range(n) for k in range(N_DEV // 2)]
    return plan


def _plan_reduce_chips(n):
    def plan(refs, send_sems, recv_sems):
        x, y, c = _position()
        cps = []
        for i in range(n):
            for j in range(1, 4):
                px, py = _flip(x, j & 2), _flip(y, j & 1)
                sem = (j - 1) * n + i
                cps.append(pltpu.make_async_remote_copy(src_ref=refs[i].at[2 * px + py], dst_ref=refs[n + i].at[j - 1],
                                                        send_sem=send_sems.at[sem], recv_sem=recv_sems.at[sem],
                                                        device_id=(px, py, c), device_id_type=MESH))
        return cps
    return plan


def _heads_major(name, a, after):
    S, W = a.shape
    H = W // HEAD_DIM
    tm = _tile(S, 512, 16)

    def body(a_ref, after_ref, o_ref):
        v = a_ref[...]
        for h in range(H):
            o_ref[h] = v[:, h * HEAD_DIM:(h + 1) * HEAD_DIM]

    return _pc(body, name=name, grid=(S // tm,),
               in_specs=[pl.BlockSpec((tm, W), lambda i: (i, 0)), pl.BlockSpec(memory_space=pl.ANY)],
               out_specs=pl.BlockSpec((H, tm, HEAD_DIM), lambda i: (0, i, 0)),
               out_shape=jax.ShapeDtypeStruct((H, S, HEAD_DIM), a.dtype), compiler_params=_sem("parallel"))(a, after)


def _heads_minor(name, a):
    H, S, _ = a.shape
    tm = _tile(S, 512, 16)

    def body(a_ref, o_ref):
        o_ref[...] = jnp.concatenate([a_ref[h] for h in range(H)], axis=1)

    return _pc(body, name=name, grid=(S // tm,), in_specs=[pl.BlockSpec((H, tm, HEAD_DIM), lambda i: (0, i, 0))],
               out_specs=pl.BlockSpec((tm, H * HEAD_DIM), lambda i: (i, 0)),
               out_shape=jax.ShapeDtypeStruct((S, H * HEAD_DIM), a.dtype), compiler_params=_sem("parallel"))(a)


def kernel(x, mem, attn_norm, attn_w_qkv, attn_q_gain, attn_k_gain, attn_w_o, pool_norm, pool_w, pool_scale, xattn_norm, mem_norm, xattn_w_q, xattn_w_kv, xattn_w_o, ffn_norm, ffn_w_up, ffn_conv_w, ffn_conv_b, ffn_w_down, final_norm, loss_target, m_attn_norm, m_attn_w_qkv, m_attn_q_gain, m_attn_k_gain, m_attn_w_o, m_pool_norm, m_pool_w, m_pool_scale, m_xattn_norm, m_mem_norm, m_xattn_w_q, m_xattn_w_kv, m_xattn_w_o, m_ffn_norm, m_ffn_w_up, m_ffn_conv_w, m_ffn_conv_b, m_ffn_w_down, m_final_norm, v_attn_norm, v_attn_w_qkv, v_attn_q_gain, v_attn_k_gain, v_attn_w_o, v_pool_norm, v_pool_w, v_pool_scale, v_xattn_norm, v_mem_norm, v_xattn_w_q, v_xattn_w_kv, v_xattn_w_o, v_ffn_norm, v_ffn_w_up, v_ffn_conv_w, v_ffn_conv_b, v_ffn_w_down, v_final_norm):
    names = ['attn_norm', 'attn_w_qkv', 'attn_q_gain', 'attn_k_gain', 'attn_w_o', 'pool_norm', 'pool_w', 'pool_scale',
             'xattn_norm', 'mem_norm', 'xattn_w_q', 'xattn_w_kv', 'xattn_w_o', 'ffn_norm', 'ffn_w_up', 'ffn_conv_w',
             'ffn_conv_b', 'ffn_w_down', 'final_norm']
    W = dict(zip(names, (attn_norm, attn_w_qkv, attn_q_gain, attn_k_gain, attn_w_o, pool_norm, pool_w, pool_scale,
                         xattn_norm, mem_norm, xattn_w_q, xattn_w_kv, xattn_w_o, ffn_norm, ffn_w_up, ffn_conv_w,
                         ffn_conv_b, ffn_w_down, final_norm)))
    Mo = dict(zip(names, (m_attn_norm, m_attn_w_qkv, m_attn_q_gain, m_attn_k_gain, m_attn_w_o, m_pool_norm, m_pool_w,
                          m_pool_scale, m_xattn_norm, m_mem_norm, m_xattn_w_q, m_xattn_w_kv, m_xattn_w_o, m_ffn_norm,
                          m_ffn_w_up, m_ffn_conv_w, m_ffn_conv_b, m_ffn_w_down, m_final_norm)))
    Vo = dict(zip(names, (v_attn_norm, v_attn_w_qkv, v_attn_q_gain, v_attn_k_gain, v_attn_w_o, v_pool_norm, v_pool_w,
                          v_pool_scale, v_xattn_norm, v_mem_norm, v_xattn_w_q, v_xattn_w_kv, v_xattn_w_o, v_ffn_norm,
                          v_ffn_w_up, v_ffn_conv_w, v_ffn_conv_b, v_ffn_w_down, v_final_norm)))

    S, D = x.shape[1], x.shape[2]
    n_layers = xattn_norm.shape[0]
    n_up = ffn_w_up.shape[2]
    qkv_w = attn_w_qkv.shape[2] * N_DEV
    n_heads = qkv_w // HEAD_DIM - 2 * N_KV_HEADS
    n_rot = (n_heads + N_KV_HEADS) * HEAD_DIM // LANES
    group_w = pool_w.shape[3]
    xs, mems, tgt = x[0], mem[0], loss_target[0]
    xi, yi, ci = _position()
    dev = 4 * xi + 2 * yi + ci
    pos = jnp.stack([ci, 2 * xi + yi]).astype(jnp.int32)

    layers = range(n_layers)
    n_groups = pool_w.shape[1]
    small_vec = jnp.concatenate([pool_norm.reshape(-1), pool_scale.reshape(-1), ffn_conv_w.reshape(-1)])
    small_rows = _round_up(-(-small_vec.size // PACK_W), 8)
    small_vec = jnp.pad(small_vec, (0, small_rows * PACK_W - small_vec.size)).reshape(small_rows, PACK_W)
    w_qkv, w_o, small, attn_token = _allgather_blocks(
        "allgather_attn", [attn_w_qkv[0].astype(BF16), attn_w_o[0].astype(BF16), small_vec])
    small = small.reshape(N_DEV, -1)
    w_qkv = w_qkv.transpose(1, 0, 2).reshape(D, qkv_w)
    w_o = w_o.reshape(-1, D)
    blocks = [pool_w.reshape(-1, group_w)] + [xattn_w_q[l] for l in layers] + [xattn_w_kv.reshape(n_layers * D, -1)]
    blocks += [xattn_w_o[l] for l in layers] + [ffn_w_up.reshape(n_layers * D, n_up)] + [ffn_w_down[l] for l in layers]
    blocks = [b.astype(BF16) for b in blocks]
    blocks[0] = _after(blocks[0], attn_token)
    n_blk = len(blocks)
    lands = [lax.dynamic_update_index_in_dim(lax.empty((N_DEV,) + b.shape, BF16), b, dev, 0) for b in blocks]
    plan_chips, plan_sibling = _plan_gather_chips(n_blk), _plan_gather_sibling(n_blk)
    gather_sems = _copies_start("gather_chips_start", blocks + lands, (4 * n_blk,), plan_chips)
    attn_norm_late = _after(attn_norm, gather_sems[3])

    d_sh = pool_norm.shape[1]
    pool_norm_f = small[:, :d_sh].reshape(1, D)
    pool_scale_f = small[:, d_sh:2 * d_sh].reshape(1, D)
    conv_w_f = small[:, 2 * d_sh:2 * d_sh + ffn_conv_w.size].reshape(N_DEV, n_layers, 3, n_up)
    conv_b_f = ffn_conv_b.reshape(n_layers, N_DEV, 1, n_up)

    cos, sin = _rope_tables(S)
    bd = _head_mean_matrix()
    pad_w = qkv_w - (n_heads + N_KV_HEADS) * HEAD_DIM
    qk_gain = jnp.concatenate([jnp.tile(attn_q_gain[0], n_heads), jnp.tile(attn_k_gain[0], N_KV_HEADS),
                               jnp.ones((pad_w,), F32)]).reshape(1, qkv_w)
    qk_scale = jnp.concatenate([jnp.full((n_heads * HEAD_DIM,), HEAD_DIM ** -0.5, F32),
                                jnp.ones((qkv_w - n_heads * HEAD_DIM,), F32)]).reshape(1, qkv_w)

    saved = []

    def xattn_ffn_fwd(l, xin, hx=None):
        if hx is None:
            hx = _rmsnorm(f"xattn_norm{l}", xin, xattn_norm[l:l + 1], BF16)
        memn = _rmsnorm(f"mem_norm{l}", mems, mem_norm[l:l + 1], BF16)
        qx = _mm_nn(f"xattn_q{l}", hx, w_xq[l], BF16)
        kv = _mm_nn_bs(f"xattn_kv{l}", memn, w_xkv, BF16, l)
        ox = _xattn_fwd(f"xattn_fwd{l}", qx, kv)
        x2, hf = _mm_nn_res_norm(f"xattn_o_norm{l}", ox, w_xo[l], xin, ffn_norm[l:l + 1])
        cw = conv_w_f[:, l].reshape(2, N_DEV // 2, 3, n_up)
        cb = conv_b_f[l].reshape(2, N_DEV // 2, 1, n_up)
        u, act = _ffn_up_act(f"ffn_up_act{l}", hf, w_up.reshape(2, N_DEV // 2, n_layers * D, n_up), cw, cb, l)
        x3 = _mm_nn_as(f"ffn_down{l}", act, w_down[l], F32, x2)
        saved.append(dict(xin=xin, hx=hx, memn=memn, qx=qx, kv=kv, ox=ox, x2=x2, hf=hf, u=u, cw=cw, cb=cb, act=act))
        return x3

    h0 = _rmsnorm("attn_norm", xs, attn_norm_late, BF16)
    qkv = _mm_nn("attn_qkv", h0, w_qkv, F32)
    qkr = _qk_rope("qk_rope", qkv, qk_gain, qk_scale, cos, sin, bd, n_rot)
    o_hm, lse = _attn_fwd("attn_fwd", qkr, n_heads)
    o_att = _heads_minor("attn_heads_minor", o_hm)
    arrived = _copies_wait("gather_chips_wait", gather_sems[2], gather_sems[0], gather_sems[1], plan_chips, o_att)
    pass_sems = _copies_start("gather_sibling_start", arrived[n_blk:], (3 * n_blk,), plan_sibling)
    x1, hx0 = _mm_nn_res_norm("attn_o_norm", o_att, _after(w_o, pass_sems[3]), xs, xattn_norm[0:1])
    gathered = iter(_copies_wait("gather_sibling_wait", pass_sems[2], pass_sems[0], pass_sems[1], plan_sibling, hx0))
    w_pool = (next(gathered).reshape(N_DEV, n_groups, -1, group_w).transpose(1, 0, 2, 3)
              .reshape(n_groups, group_w, group_w))
    w_xq = [next(gathered).reshape(D, D) for l in layers]
    w_xkv = next(gathered)
    w_xo = [next(gathered).reshape(D, D) for l in layers]
    w_up = next(gathered)
    w_down = [next(gathered).reshape(-1, D) for l in layers]
    x3 = xattn_ffn_fwd(0, x1, hx0)
    hp = _rmsnorm("pool_norm", x3, pool_norm_f, F32)
    mixed = _pool_window("pool_window", hp, group_w, False, BF16)
    x4 = _pool_proj("pool_proj", mixed, w_pool, pool_scale_f, x3)
    x6 = xattn_ffn_fwd(1, x4)

    G = {}
    g, d_final, lvec = _loss_head("loss_head", x6, final_norm.reshape(1, D), tgt)
    G['final_norm'] = d_final.reshape(D)
    loss_part = (0.5 * jnp.sum(lvec) / D).reshape(1)

    d_xn, d_mn, d_fn, d_xq, d_xkv, d_xo, d_up, d_cw, d_cb, d_down = ([None] * n_layers for _ in range(10))

    def xattn_ffn_bwd(l, g, conv_b_late=None, after_act=None):
        sv = saved[l]
        d_down[l] = _mm_tn_as(f"ffn_down_dw{l}", sv['act'], g, F32)
        du, st = _ffn_act_bwd(f"ffn_act_bwd{l}", sv['u'], g, w_down[l], sv['cw'],
                              sv['cb'] if conv_b_late is None else conv_b_late)
        du = du.reshape(N_DEV, S, n_up)
        ffn_gain = ffn_norm[l:l + 1] if after_act is None else _after(ffn_norm[l:l + 1], after_act(du))
        st = st.reshape(N_DEV, 8, n_up)
        d_cw[l], d_cb[l] = st[:, 0:3], st[:, 3].reshape(-1)
        d_up[l] = _mm_tn_bs(f"ffn_up_dw{l}", sv['hf'], du, F32)
        g, d_fn[l] = _mm_nt_abs_norm_bwd(f"ffn_up_dx_norm_bwd{l}", du, w_up, sv['x2'], ffn_gain, g, l)
        d_xo[l] = _mm_tn(f"xattn_o_dw{l}", sv['ox'], g, F32)
        do = _mm_nt(f"xattn_o_dx{l}", g, w_xo[l], BF16)
        dq, dkv = _xattn_bwd(f"xattn_bwd{l}", sv['qx'], sv['kv'], do)
        d_xq[l] = _mm_tn(f"xattn_q_dw{l}", sv['hx'], dq, F32)
        d_xkv[l] = _mm_tn_bs(f"xattn_kv_dw{l}", sv['memn'], dkv, F32)
        dmemn = _mm_nt_abs(f"xattn_kv_dx{l}", dkv, w_xkv, D, F32, l)
        _, d_mn[l] = _rmsnorm_bwd(f"mem_norm_bwd{l}", mems, mem_norm[l:l + 1], dmemn)
        g, d_xn[l] = _mm_nt_norm_bwd(f"xattn_q_dx_norm_bwd{l}", dq, w_xq[l], sv['xin'], xattn_norm[l:l + 1], g)
        return g

    def reduce_start(tag, bufs):
        n = len(bufs)
        g4s = [b.reshape((N_DEV // 2, 2) + b.shape[1:]) for b in bufs]
        lands = [lax.empty((N_DEV // 2,) + b.shape[1:], F32) for b in bufs]
        plan = _plan_reduce_sibling(n)
        return (n, plan) + _copies_start(f"reduce_sibling_start_{tag}", g4s + lands, (N_DEV // 2 * n,), plan)

    def reduce_between(tag, state, after):
        n, plan, send_sems, recv_sems, thru, _ = state
        got = _copies_wait(f"reduce_sibling_wait_{tag}", thru, send_sems, recv_sems, plan, after)
        sums = [_add_sibling(f"reduce_add_sibling_{tag}{i}", got[i], got[n + i], pos) for i in range(n)]
        lands = [lax.empty((3,) + tb.shape[1:], BF16) for tb, _ in sums]
        plan = _plan_reduce_chips(n)
        return (n, plan, [own for _, own in sums]) + _copies_start(f"reduce_chips_start_{tag}",
                                                                    [tb for tb, _ in sums] + lands, (3 * n,), plan)

    def reduce_finish(tag, state, after):
        n, plan, owns, send_sems, recv_sems, thru, _ = state
        got = _copies_wait(f"reduce_chips_wait_{tag}", thru, send_sems, recv_sems, plan, after)
        return [_add_chips(f"reduce_add_chips_{tag}{i}", owns[i], got[n + i]) for i in range(n)]

    def layer_bufs(l):
        return [d_xq[l].reshape(N_DEV, -1, D), d_xkv[l], d_xo[l].reshape(N_DEV, -1, D), d_up[l],
                d_down[l].reshape(N_DEV, -1, D)]

    g = xattn_ffn_bwd(1, g)
    d_mixed, d_pool_w, d_pool_scale = _pool_proj_bwd("pool_proj_bwd", g, mixed, w_pool, pool_scale_f)
    dhp = _pool_window("pool_window_bwd", d_mixed, group_w, True, F32)
    g, d_pool_norm = _rmsnorm_bwd("pool_norm_bwd", x3, pool_norm_f, dhp, g)
    upper = reduce_start("upper", [d_pool_w.reshape(n_groups, N_DEV, -1, group_w).transpose(1, 0, 2, 3)
                                   .reshape(N_DEV, -1, group_w)] + layer_bufs(1))
    between = []

    def upper_between(du):
        between.append(reduce_between("upper", upper, du))
        return between[0][-1]

    g = xattn_ffn_bwd(0, g, _after(saved[0]['cb'], upper[-1]), upper_between)
    lower = reduce_start("lower", layer_bufs(0))
    d_wo = _mm_tn("attn_o_dw", o_att, g, F32)
    do = _mm_nt("attn_o_dx", g, _after(w_o, lower[-1]), BF16)
    lower = reduce_between("lower", lower, do)
    do_hm = _heads_major("attn_heads_major", do, lower[-1])
    dq_hm, dk_hm, dv_hm = _attn_bwd("attn_bwd", qkr, o_hm, lse, do_hm)
    red_lower = reduce_finish("lower", lower, dq_hm)
    d_qkv, d_gain = _qk_rope_bwd("qk_rope_bwd", dq_hm, dk_hm, dv_hm, qkv, qk_gain, qk_scale, cos, sin, bd)
    red_upper = reduce_finish("upper", between[0], d_qkv)
    d_wqkv = _mm_tn("attn_qkv_dw", h0, d_qkv, F32)
    last = reduce_start("last", [d_wqkv.reshape(D, N_DEV, -1).transpose(1, 0, 2), d_wo.reshape(N_DEV, -1, D)])
    last = reduce_between("last", last, d_wqkv)
    grad_x, d_attn_norm = _mm_nt_norm_bwd("attn_qkv_dx_norm_bwd", d_qkv, w_qkv, xs, attn_norm, g, last[-1])
    G['pool_w'] = red_upper[0].reshape(pool_w.shape)
    per_layer = [red_lower, red_upper[1:]]
    for i, n in enumerate(['xattn_w_q', 'xattn_w_kv', 'xattn_w_o', 'ffn_w_up', 'ffn_w_down']):
        G[n] = jnp.stack([per_layer[l][i] for l in layers])

    hq = n_heads * HEAD_DIM
    small_g = {'attn_norm': d_attn_norm, 'attn_q_gain': d_gain[0, :hq].reshape(n_heads, HEAD_DIM).sum(0),
               'attn_k_gain': d_gain[0, hq:hq + N_KV_HEADS * HEAD_DIM].reshape(N_KV_HEADS, HEAD_DIM).sum(0),
               'pool_norm': d_pool_norm, 'pool_scale': d_pool_scale,
               'xattn_norm': jnp.concatenate(d_xn), 'mem_norm': jnp.concatenate(d_mn), 'ffn_norm': jnp.concatenate(d_fn),
               'ffn_conv_w': jnp.stack(d_cw, axis=1), 'ffn_conv_b': jnp.stack(d_cb)}
    order = list(small_g)
    flat = jnp.concatenate([loss_part] + [small_g[n].reshape(-1) for n in order] + [G['final_norm']])
    ar_rows = _round_up(-(-flat.size // PACK_W), 8)
    flat = jnp.pad(flat, (0, ar_rows * PACK_W - flat.size)).reshape(ar_rows, PACK_W)
    summed = _sum_slots("allreduce_sum", _allgather_small("allreduce_gather", flat)).reshape(-1)
    loss = summed[0]
    red_last = reduce_finish("last", last, summed)
    G['attn_w_qkv'], G['attn_w_o'] = red_last[0][None], red_last[1][None]
    at = 1
    for n in order + ['final_norm']:
        size = G['final_norm'].size if n == 'final_norm' else small_g[n].size
        piece = summed[at:at + size]
        at += size
        if n in ('pool_norm', 'pool_scale'):
            piece = lax.dynamic_slice(piece, (dev * d_sh,), (d_sh,))
        elif n == 'ffn_conv_w':
            piece = lax.dynamic_index_in_dim(piece.reshape(N_DEV, n_layers, 3, n_up), dev, 0, keepdims=False)
        G[n] = piece.reshape(W[n].shape)

    deltas, new_m, new_v = [], [], []
    for n in names:
        d, nm, nv = _adamw(f"adamw_{n}", W[n], G[n], Mo[n], Vo[n])
        deltas.append(d)
        new_m.append(nm)
        new_v.append(nv)
    return (loss, grad_x[None], *[G[n] for n in names], *deltas, *new_m, *new_v)
```
